```python
import math
import jax
import jax.numpy as jnp
from jax import lax
import numpy as np

D_MODEL = 1024
BATCH = 16
SEQ = 2048
DEPTH = 2

MIX_DIM = D_MODEL
HEAD_DIM = 64
RWKV_DIM = MIX_DIM // 2
RWKV_HEADS = RWKV_DIM // HEAD_DIM
RWKV_DECAY_LORA = 64
RWKV_A_LORA = 64
RWKV_GATE_LORA = 128
RWKV_GN_EPS = 64e-5
RWKV_COLS = 3 * RWKV_DIM + RWKV_DECAY_LORA + RWKV_A_LORA + RWKV_GATE_LORA
SSM_DIM = MIX_DIM // 2
SSM_HEAD_DIM = 64
SSM_HEADS = SSM_DIM // SSM_HEAD_DIM
SSM_GROUPS = 2
SSM_STATE = 128
SSM_CONV = 4
SSM_CHUNK = 128
SSM_XBC = SSM_DIM + 2 * SSM_GROUPS * SSM_STATE
SSM_COLS = SSM_DIM + SSM_XBC + SSM_HEADS
L0_COLS = RWKV_COLS + SSM_COLS
SB_DIM = MIX_DIM // 2
SB_HEADS = SB_DIM // HEAD_DIM
MLA_NOPE = 64
MLA_ROPE = 32
MLA_V = 64
MLA_HEADS = (MIX_DIM // 2) // MLA_V
MLA_Q_LORA = 256
MLA_KV_LORA = 128
ROPE_THETA = 10000.0
L1_COLS = 3 * SB_DIM + MLA_Q_LORA + MLA_KV_LORA + MLA_ROPE
Q_BLOCK = 128
D_FF = 2816
FFN_CONV = 3
ALPHA = (2 * DEPTH) ** 0.25
BETA = (8 * DEPTH) ** -0.25

kernel_name = 'hybrid_rwkv7_ssd_stickbreak_mla_convffn'


def _layer_norm(x, g, b, eps=1e-5):
    xf = x.astype(jnp.float32)
    mu = jnp.mean(xf, axis=-1, keepdims=True)
    var = jnp.mean(jnp.square(xf - mu), axis=-1, keepdims=True)
    return ((xf - mu) * lax.rsqrt(var + eps) * g + b).astype(x.dtype)


def _rms_norm(x, g, eps=1e-6):
    xf = x.astype(jnp.float32)
    return (xf * lax.rsqrt(jnp.mean(xf * xf, axis=-1, keepdims=True) + eps) * g).astype(x.dtype)


def _causal_dwconv(u, w, b):
    K = w.shape[0]
    T = u.shape[1]
    up = jnp.pad(u, ((0, 0), (K - 1, 0), (0, 0)))
    y = b + up[:, 0:T] * w[0]
    for i in range(1, K):
        y = y + up[:, i:i + T] * w[i]
    return y


def _to_heads(t, n):
    return t.reshape(t.shape[0], t.shape[1], n, -1)


def _rwkv7_scan(r, w, k, v, a, b):
    bsz, T, H, N = r.shape

    def step(S, inp):
        r_t, w_t, k_t, v_t, a_t, b_t = inp
        sa = jnp.einsum('bhij,bhj->bhi', S, a_t)
        S = S * w_t[:, :, None, :] + sa[..., None] * b_t[:, :, None, :] + v_t[..., None] * k_t[:, :, None, :]
        return S, jnp.einsum('bhij,bhj->bhi', S, r_t)

    xs = tuple(jnp.swapaxes(t, 0, 1) for t in (r, w, k, v, a, b))
    _, y = lax.scan(step, jnp.zeros((bsz, H, N, N), jnp.float32), xs)
    return jnp.swapaxes(y, 0, 1)


def _rwkv7_group(p, mix, w0, w2, a0, a2, g2, k_k, k_a, r_k, ln_g, ln_b):
    bsz, T, _ = p.shape
    prev = jnp.pad(p, ((0, 0), (1, 0), (0, 0)))[:, :-1]
    p = p + (prev - p) * mix
    cuts = [RWKV_DIM, 2 * RWKV_DIM, 3 * RWKV_DIM, 3 * RWKV_DIM + RWKV_DECAY_LORA,
            3 * RWKV_DIM + RWKV_DECAY_LORA + RWKV_A_LORA]
    r, k, v, w_lo, a_lo, g_lo = jnp.split(p, cuts, axis=-1)
    log_w = -jax.nn.softplus(-(w0 + jnp.tanh(w_lo) @ w2)) - 0.5
    decay = jnp.exp(-jnp.exp(log_w.astype(jnp.float32)))
    a = jax.nn.sigmoid(a0 + a_lo @ a2)
    g = jax.nn.sigmoid(g_lo) @ g2
    kk = _to_heads(k * k_k, RWKV_HEADS).astype(jnp.float32)
    kk = kk / jnp.maximum(jnp.sqrt(jnp.sum(kk * kk, axis=-1, keepdims=True)), 1e-12)
    k = k * (1 + (a - 1) * k_a)
    r_h, k_h, v_h, a_h, w_h = [_to_heads(t, RWKV_HEADS).astype(jnp.float32) for t in (r, k, v, a, decay)]
    y = _rwkv7_scan(r_h, w_h, k_h, v_h, -kk, kk * a_h)
    mu = jnp.mean(y, axis=-1, keepdims=True)
    var = jnp.mean(jnp.square(y - mu), axis=-1, keepdims=True)
    y = ((y - mu) * lax.rsqrt(var + RWKV_GN_EPS)).reshape(bsz, T, RWKV_DIM) * ln_g + ln_b
    bonus = jnp.sum(r_h * k_h * r_k, axis=-1, keepdims=True) * v_h
    return ((y + bonus.reshape(bsz, T, RWKV_DIM)) * g).astype(p.dtype)


def _segsum(a):
    L = a.shape[-1]
    ar = jnp.broadcast_to(a[..., :, None], a.shape + (L,))
    strict = jnp.tril(jnp.ones((L, L), bool), -1)
    s = jnp.cumsum(jnp.where(strict, ar, 0.0), axis=-2)
    return jnp.where(jnp.tril(jnp.ones((L, L), bool)), s, -jnp.inf)


def _ssd_chunked(xs, dt, A, Bm, Cm):
    bsz, T, H, P = xs.shape
    G, N = Bm.shape[2], Bm.shape[3]
    J = H // G
    c, l = T // SSM_CHUNK, SSM_CHUNK
    X = (xs * dt[..., None]).reshape(bsz, c, l, G, J, P)
    a_dt = (dt * A).reshape(bsz, c, l, G, J).transpose(0, 3, 4, 1, 2)
    Bc = Bm.reshape(bsz, c, l, G, N)
    Cc = Cm.reshape(bsz, c, l, G, N)
    a_cum = jnp.cumsum(a_dt, axis=-1)
    decay_in = jnp.exp(_segsum(a_dt))
    cb = jnp.einsum('bclgn,bcsgn->bgcls', Cc, Bc)
    y_diag = jnp.einsum('bgjcls,bcsgjp->bclgjp', cb[:, :, None] * decay_in, X)
    decay_to_end = jnp.exp(a_cum[..., -1:] - a_cum)
    states = jnp.einsum('bclgn,bgjcl,bclgjp->bcgjpn', Bc, decay_to_end, X)
    states = jnp.concatenate([jnp.zeros_like(states[:, :1]), states], axis=1)
    chunk_decay = jnp.exp(_segsum(jnp.pad(a_cum[..., -1], ((0, 0), (0, 0), (0, 0), (1, 0)))))
    states = jnp.einsum('bgjzc,bcgjpn->bzgjpn', chunk_decay, states)[:, :-1]
    y_off = jnp.einsum('bclgn,bcgjpn,bgjcl->bclgjp', Cc, states, jnp.exp(a_cum))
    return (y_diag + y_off).reshape(bsz, T, H, P)


def _mamba2_group(p, conv_w, conv_b, dt_bias, a_log, d_skip, norm_g):
    bsz, T, _ = p.shape
    z, xbc, dt_raw = jnp.split(p, [SSM_DIM, SSM_DIM + SSM_XBC], axis=-1)
    xbc = jax.nn.silu(_causal_dwconv(xbc, conv_w, conv_b))
    xs, Bm, Cm = jnp.split(xbc, [SSM_DIM, SSM_DIM + SSM_GROUPS * SSM_STATE], axis=-1)
    xs = xs.reshape(bsz, T, SSM_HEADS, SSM_HEAD_DIM).astype(jnp.float32)
    Bm = Bm.reshape(bsz, T, SSM_GROUPS, SSM_STATE).astype(jnp.float32)
    Cm = Cm.reshape(bsz, T, SSM_GROUPS, SSM_STATE).astype(jnp.float32)
    dt = jax.nn.softplus((dt_raw + dt_bias).astype(jnp.float32))
    A = -jnp.exp(a_log.astype(jnp.float32))
    y = _ssd_chunked(xs, dt, A, Bm, Cm) + xs * d_skip[:, None]
    u = (y.reshape(bsz, T, SSM_DIM) * jax.nn.silu(z.astype(jnp.float32))).reshape(bsz, T, SSM_GROUPS, -1)
    u = u * lax.rsqrt(jnp.mean(u * u, axis=-1, keepdims=True) + 1e-5)
    return (u.reshape(bsz, T, SSM_DIM) * norm_g).astype(p.dtype)


def _mixer_rwkv_ssd(h, w_in, mix, w0, w2, a0, a2, g2, k_k, k_a, r_k, ln_g, ln_b,
                    conv_w, conv_b, dt_bias, a_log, d_skip, norm_g, w_out):
    proj = h @ w_in
    y_a = _rwkv7_group(proj[..., :RWKV_COLS], mix, w0, w2, a0, a2, g2, k_k, k_a, r_k, ln_g, ln_b)
    y_b = _mamba2_group(proj[..., RWKV_COLS:], conv_w, conv_b, dt_bias, a_log, d_skip, norm_g)
    return jnp.concatenate([y_a, y_b], axis=-1) @ w_out


def _stick_breaking(q, k, v):
    T = q.shape[2]
    scale = q.shape[-1] ** -0.5
    outs = []
    for start in range(0, T, Q_BLOCK):
        end = start + Q_BLOCK
        z = jnp.einsum('bhqd,bhkd->bhqk', q[:, :, start:end], k[:, :, :end]).astype(jnp.float32) * scale
        strict = jnp.arange(end)[None, :] < jnp.arange(start, end)[:, None]
        log_keep = jnp.where(strict, jax.nn.log_sigmoid(-z), 0.0)
        log_att = jax.nn.log_sigmoid(z) + lax.cumsum(log_keep, axis=3, reverse=True) - log_keep
        att = jnp.where(strict, jnp.exp(log_att), 0.0)
        outs.append(jnp.einsum('bhqk,bhkd->bhqd', att.astype(v.dtype), v[:, :, :end]))
    return jnp.concatenate(outs, axis=2)


def _rope_tables(positions):
    inv_freq = 1.0 / (ROPE_THETA ** (jnp.arange(0, MLA_ROPE, 2, dtype=jnp.float32) / MLA_ROPE))
    ang = positions.astype(jnp.float32)[..., None] * inv_freq
    return jnp.cos(ang), jnp.sin(ang)


def _apply_rope(x, cos, sin):
    half = x.shape[-1] // 2
    x1, x2 = x[..., :half], x[..., half:]
    return jnp.concatenate([x1 * cos - x2 * sin, x2 * cos + x1 * sin], axis=-1)


def _mla_attention(q_nope, q_pe, k_nope, k_pe, v):
    T = q_nope.shape[2]
    scale = (MLA_NOPE + MLA_ROPE) ** -0.5
    outs = []
    for start in range(0, T, Q_BLOCK):
        end = start + Q_BLOCK
        s = (jnp.einsum('bhqd,bhkd->bhqk', q_nope[:, :, start:end], k_nope[:, :, :end])
             + jnp.einsum('bhqd,bkd->bhqk', q_pe[:, :, start:end], k_pe[:, :end])).astype(jnp.float32) * scale
        causal = jnp.arange(end)[None, :] <= jnp.arange(start, end)[:, None]
        prob = jax.nn.softmax(jnp.where(causal, s, -jnp.inf), axis=-1)
        outs.append(jnp.einsum('bhqk,bhkd->bhqd', prob.astype(v.dtype), v[:, :, :end]))
    return jnp.concatenate(outs, axis=2)


def _mixer_sb_mla(h, positions, w_in, q_norm_g, w_uq, kv_norm_g, w_ukv, w_out):
    bsz, T, _ = h.shape
    proj = h @ w_in
    cuts = [SB_DIM, 2 * SB_DIM, 3 * SB_DIM, 3 * SB_DIM + MLA_Q_LORA, 3 * SB_DIM + MLA_Q_LORA + MLA_KV_LORA]
    q_sb, k_sb, v_sb, c_q, c_kv, k_pe = jnp.split(proj, cuts, axis=-1)
    tr = lambda t: t.transpose(0, 2, 1, 3)
    y_c = _stick_breaking(tr(_to_heads(q_sb, SB_HEADS)), tr(_to_heads(k_sb, SB_HEADS)), tr(_to_heads(v_sb, SB_HEADS)))
    y_c = tr(y_c).reshape(bsz, T, SB_DIM)
    q = _to_heads(_rms_norm(c_q, q_norm_g) @ w_uq, MLA_HEADS)
    kv = _to_heads(_rms_norm(c_kv, kv_norm_g) @ w_ukv, MLA_HEADS)
    cos, sin = _rope_tables(positions)
    q_pe = _apply_rope(q[..., MLA_NOPE:], cos[:, :, None], sin[:, :, None])
    k_pe = _apply_rope(k_pe, cos, sin)
    y_d = _mla_attention(tr(q[..., :MLA_NOPE]), tr(q_pe), tr(kv[..., :MLA_NOPE]), k_pe, tr(kv[..., MLA_NOPE:]))
    y_d = tr(y_d).reshape(bsz, T, MLA_HEADS * MLA_V)
    return jnp.concatenate([y_c, y_d.astype(y_c.dtype)], axis=-1) @ w_out


def _conv_ffn(h, w_up, conv_w, conv_b, w_down):
    gate, up = jnp.split(h @ w_up, [D_FF], axis=-1)
    gate = _causal_dwconv(gate, conv_w, conv_b)
    return (jax.nn.silu(gate) * up) @ w_down


def _fwd_setup_inputs(seed: int = 0) -> dict:
    key = jax.random.key(seed)
    ks = iter(jax.random.split(key, 64))

    def nrm(shape, scale):
        return jax.random.normal(next(ks), shape, jnp.float32) * scale

    def uni(shape, lo, hi):
        return jax.random.uniform(next(ks), shape, jnp.float32, lo, hi)

    def gain(n):
        return 1.0 + nrm((n,), 0.02)

    inp = {}
    inp['x'] = nrm((BATCH, SEQ, D_MODEL), 1.0)
    inp['positions'] = (jax.random.randint(next(ks), (BATCH, 1), 0, 4096, dtype=jnp.int32)
                        + jnp.arange(SEQ, dtype=jnp.int32)[None, :])
    inp['l0_w_in'] = nrm((D_MODEL, L0_COLS), D_MODEL ** -0.5)
    inp['rwkv_mix'] = uni((RWKV_COLS,), 0.0, 1.0)
    inp['rwkv_w0'] = uni((RWKV_DIM,), -6.0, -1.0)
    inp['rwkv_w2'] = nrm((RWKV_DECAY_LORA, RWKV_DIM), 0.1)
    inp['rwkv_a0'] = nrm((RWKV_DIM,), 0.1)
    inp['rwkv_a2'] = nrm((RWKV_A_LORA, RWKV_DIM), 0.1)
    inp['rwkv_g2'] = nrm((RWKV_GATE_LORA, RWKV_DIM), RWKV_GATE_LORA ** -0.5)
    inp['rwkv_k_k'] = 0.85 + nrm((RWKV_DIM,), 0.05)
    inp['rwkv_k_a'] = 1.0 + nrm((RWKV_DIM,), 0.05)
    inp['rwkv_r_k'] = nrm((RWKV_HEADS, HEAD_DIM), 0.1)
    inp['rwkv_ln_g'] = gain(RWKV_DIM)
    inp['rwkv_ln_b'] = nrm((RWKV_DIM,), 0.02)
    inp['ssm_conv_w'] = nrm((SSM_CONV, SSM_XBC), 0.5)
    inp['ssm_conv_b'] = nrm((SSM_XBC,), 0.02)
    dt0 = jnp.exp(uni((SSM_HEADS,), math.log(1e-3), math.log(1e-1)))
    inp['ssm_dt_bias'] = dt0 + jnp.log(-jnp.expm1(-dt0))
    inp['ssm_a_log'] = jnp.log(uni((SSM_HEADS,), 1.0, 16.0))
    inp['ssm_d'] = 1.0 + nrm((SSM_HEADS,), 0.1)
    inp['ssm_norm_g'] = gain(SSM_DIM)
    inp['l0_w_out'] = nrm((MIX_DIM, D_MODEL), MIX_DIM ** -0.5 * BETA)
    inp['l0_ln1_g'] = gain(D_MODEL)
    inp['l0_ln1_b'] = nrm((D_MODEL,), 0.02)
    inp['ffn0_w_up'] = nrm((D_MODEL, 2 * D_FF), D_MODEL ** -0.5)
    inp['ffn0_conv_w'] = nrm((FFN_CONV, D_FF), FFN_CONV ** -0.5)
    inp['ffn0_conv_b'] = nrm((D_FF,), 0.02)
    inp['ffn0_w_down'] = nrm((D_FF, D_MODEL), D_FF ** -0.5 * BETA)
    inp['l0_ln2_g'] = gain(D_MODEL)
    inp['l0_ln2_b'] = nrm((D_MODEL,), 0.02)
    inp['l1_w_in'] = nrm((D_MODEL, L1_COLS), D_MODEL ** -0.5)
    inp['mla_q_norm_g'] = gain(MLA_Q_LORA)
    inp['mla_w_uq'] = nrm((MLA_Q_LORA, MLA_HEADS * (MLA_NOPE + MLA_ROPE)), MLA_Q_LORA ** -0.5)
    inp['mla_kv_norm_g'] = gain(MLA_KV_LORA)
    inp['mla_w_ukv'] = nrm((MLA_KV_LORA, MLA_HEADS * (MLA_NOPE + MLA_V)), MLA_KV_LORA ** -0.5)
    inp['l1_w_out'] = nrm((MIX_DIM, D_MODEL), MIX_DIM ** -0.5 * BETA)
    inp['l1_ln1_g'] = gain(D_MODEL)
    inp['l1_ln1_b'] = nrm((D_MODEL,), 0.02)
    inp['ffn1_w_up'] = nrm((D_MODEL, 2 * D_FF), D_MODEL ** -0.5)
    inp['ffn1_conv_w'] = nrm((FFN_CONV, D_FF), FFN_CONV ** -0.5)
    inp['ffn1_conv_b'] = nrm((D_FF,), 0.02)
    inp['ffn1_w_down'] = nrm((D_FF, D_MODEL), D_FF ** -0.5 * BETA)
    inp['l1_ln2_g'] = gain(D_MODEL)
    inp['l1_ln2_b'] = nrm((D_MODEL,), 0.02)
    return inp


def _fwd_reference(x, positions, l0_w_in, rwkv_mix, rwkv_w0, rwkv_w2, rwkv_a0, rwkv_a2, rwkv_g2,
              rwkv_k_k, rwkv_k_a, rwkv_r_k, rwkv_ln_g, rwkv_ln_b, ssm_conv_w, ssm_conv_b,
              ssm_dt_bias, ssm_a_log, ssm_d, ssm_norm_g, l0_w_out, l0_ln1_g, l0_ln1_b,
              ffn0_w_up, ffn0_conv_w, ffn0_conv_b, ffn0_w_down, l0_ln2_g, l0_ln2_b,
              l1_w_in, mla_q_norm_g, mla_w_uq, mla_kv_norm_g, mla_w_ukv, l1_w_out,
              l1_ln1_g, l1_ln1_b, ffn1_w_up, ffn1_conv_w, ffn1_conv_b, ffn1_w_down,
              l1_ln2_g, l1_ln2_b):
    mixers = (_mixer_rwkv_ssd, _mixer_sb_mla)
    mixer_args = (
        (l0_w_in, rwkv_mix, rwkv_w0, rwkv_w2, rwkv_a0, rwkv_a2, rwkv_g2, rwkv_k_k, rwkv_k_a,
         rwkv_r_k, rwkv_ln_g, rwkv_ln_b, ssm_conv_w, ssm_conv_b, ssm_dt_bias, ssm_a_log,
         ssm_d, ssm_norm_g, l0_w_out),
        (positions, l1_w_in, mla_q_norm_g, mla_w_uq, mla_kv_norm_g, mla_w_ukv, l1_w_out),
    )
    ffn_args = ((ffn0_w_up, ffn0_conv_w, ffn0_conv_b, ffn0_w_down),
                (ffn1_w_up, ffn1_conv_w, ffn1_conv_b, ffn1_w_down))
    ln_mix = ((l0_ln1_g, l0_ln1_b), (l1_ln1_g, l1_ln1_b))
    ln_ffn = ((l0_ln2_g, l0_ln2_b), (l1_ln2_g, l1_ln2_b))
    h = x
    for layer in range(DEPTH):
        mixed = mixers[layer % 2](h, *mixer_args[layer])
        h = _layer_norm(ALPHA * h + mixed, *ln_mix[layer])
        h = _layer_norm(ALPHA * h + _conv_ffn(h, *ffn_args[layer]), *ln_ffn[layer])
    return h.astype(x.dtype)


import jax as _jax
import jax.numpy as _jnp

TWIN_FORMAT = 'train_step'
FWD_PARAMS = ['x', 'positions', 'l0_w_in', 'rwkv_mix', 'rwkv_w0', 'rwkv_w2', 'rwkv_a0', 'rwkv_a2', 'rwkv_g2', 'rwkv_k_k', 'rwkv_k_a', 'rwkv_r_k', 'rwkv_ln_g', 'rwkv_ln_b', 'ssm_conv_w', 'ssm_conv_b', 'ssm_dt_bias', 'ssm_a_log', 'ssm_d', 'ssm_norm_g', 'l0_w_out', 'l0_ln1_g', 'l0_ln1_b', 'ffn0_w_up', 'ffn0_conv_w', 'ffn0_conv_b', 'ffn0_w_down', 'l0_ln2_g', 'l0_ln2_b', 'l1_w_in', 'mla_q_norm_g', 'mla_w_uq', 'mla_kv_norm_g', 'mla_w_ukv', 'l1_w_out', 'l1_ln1_g', 'l1_ln1_b', 'ffn1_w_up', 'ffn1_conv_w', 'ffn1_conv_b', 'ffn1_w_down', 'l1_ln2_g', 'l1_ln2_b']
TWIN_WEIGHTS = ['l0_w_in', 'rwkv_mix', 'rwkv_w0', 'rwkv_w2', 'rwkv_a0', 'rwkv_a2', 'rwkv_g2', 'rwkv_k_k', 'rwkv_k_a', 'rwkv_r_k', 'rwkv_ln_g', 'rwkv_ln_b', 'ssm_conv_w', 'ssm_conv_b', 'ssm_dt_bias', 'ssm_a_log', 'ssm_d', 'ssm_norm_g', 'l0_w_out', 'l0_ln1_g', 'l0_ln1_b', 'ffn0_w_up', 'ffn0_conv_w', 'ffn0_conv_b', 'ffn0_w_down', 'l0_ln2_g', 'l0_ln2_b', 'l1_w_in', 'mla_q_norm_g', 'mla_w_uq', 'mla_kv_norm_g', 'mla_w_ukv', 'l1_w_out', 'l1_ln1_g', 'l1_ln1_b', 'ffn1_w_up', 'ffn1_conv_w', 'ffn1_conv_b', 'ffn1_w_down', 'l1_ln2_g', 'l1_ln2_b']
TWIN_DIFF_INPUT = 'x'
TWIN_INPUTS = ['x', 'positions', 'l0_w_in', 'rwkv_mix', 'rwkv_w0', 'rwkv_w2', 'rwkv_a0', 'rwkv_a2', 'rwkv_g2', 'rwkv_k_k', 'rwkv_k_a', 'rwkv_r_k', 'rwkv_ln_g', 'rwkv_ln_b', 'ssm_conv_w', 'ssm_conv_b', 'ssm_dt_bias', 'ssm_a_log', 'ssm_d', 'ssm_norm_g', 'l0_w_out', 'l0_ln1_g', 'l0_ln1_b', 'ffn0_w_up', 'ffn0_conv_w', 'ffn0_conv_b', 'ffn0_w_down', 'l0_ln2_g', 'l0_ln2_b', 'l1_w_in', 'mla_q_norm_g', 'mla_w_uq', 'mla_kv_norm_g', 'mla_w_ukv', 'l1_w_out', 'l1_ln1_g', 'l1_ln1_b', 'ffn1_w_up', 'ffn1_conv_w', 'ffn1_conv_b', 'ffn1_w_down', 'l1_ln2_g', 'l1_ln2_b', 'loss_target', 'm_l0_w_in', 'm_rwkv_mix', 'm_rwkv_w0', 'm_rwkv_w2', 'm_rwkv_a0', 'm_rwkv_a2', 'm_rwkv_g2', 'm_rwkv_k_k', 'm_rwkv_k_a', 'm_rwkv_r_k', 'm_rwkv_ln_g', 'm_rwkv_ln_b', 'm_ssm_conv_w', 'm_ssm_conv_b', 'm_ssm_dt_bias', 'm_ssm_a_log', 'm_ssm_d', 'm_ssm_norm_g', 'm_l0_w_out', 'm_l0_ln1_g', 'm_l0_ln1_b', 'm_ffn0_w_up', 'm_ffn0_conv_w', 'm_ffn0_conv_b', 'm_ffn0_w_down', 'm_l0_ln2_g', 'm_l0_ln2_b', 'm_l1_w_in', 'm_mla_q_norm_g', 'm_mla_w_uq', 'm_mla_kv_norm_g', 'm_mla_w_ukv', 'm_l1_w_out', 'm_l1_ln1_g', 'm_l1_ln1_b', 'm_ffn1_w_up', 'm_ffn1_conv_w', 'm_ffn1_conv_b', 'm_ffn1_w_down', 'm_l1_ln2_g', 'm_l1_ln2_b', 'v_l0_w_in', 'v_rwkv_mix', 'v_rwkv_w0', 'v_rwkv_w2', 'v_rwkv_a0', 'v_rwkv_a2', 'v_rwkv_g2', 'v_rwkv_k_k', 'v_rwkv_k_a', 'v_rwkv_r_k', 'v_rwkv_ln_g', 'v_rwkv_ln_b', 'v_ssm_conv_w', 'v_ssm_conv_b', 'v_ssm_dt_bias', 'v_ssm_a_log', 'v_ssm_d', 'v_ssm_norm_g', 'v_l0_w_out', 'v_l0_ln1_g', 'v_l0_ln1_b', 'v_ffn0_w_up', 'v_ffn0_conv_w', 'v_ffn0_conv_b', 'v_ffn0_w_down', 'v_l0_ln2_g', 'v_l0_ln2_b', 'v_l1_w_in', 'v_mla_q_norm_g', 'v_mla_w_uq', 'v_mla_kv_norm_g', 'v_mla_w_ukv', 'v_l1_w_out', 'v_l1_ln1_g', 'v_l1_ln1_b', 'v_ffn1_w_up', 'v_ffn1_conv_w', 'v_ffn1_conv_b', 'v_ffn1_w_down', 'v_l1_ln2_g', 'v_l1_ln2_b']
TWIN_OUTPUTS = ['loss', 'grad_x', 'grad_l0_w_in', 'grad_rwkv_mix', 'grad_rwkv_w0', 'grad_rwkv_w2', 'grad_rwkv_a0', 'grad_rwkv_a2', 'grad_rwkv_g2', 'grad_rwkv_k_k', 'grad_rwkv_k_a', 'grad_rwkv_r_k', 'grad_rwkv_ln_g', 'grad_rwkv_ln_b', 'grad_ssm_conv_w', 'grad_ssm_conv_b', 'grad_ssm_dt_bias', 'grad_ssm_a_log', 'grad_ssm_d', 'grad_ssm_norm_g', 'grad_l0_w_out', 'grad_l0_ln1_g', 'grad_l0_ln1_b', 'grad_ffn0_w_up', 'grad_ffn0_conv_w', 'grad_ffn0_conv_b', 'grad_ffn0_w_down', 'grad_l0_ln2_g', 'grad_l0_ln2_b', 'grad_l1_w_in', 'grad_mla_q_norm_g', 'grad_mla_w_uq', 'grad_mla_kv_norm_g', 'grad_mla_w_ukv', 'grad_l1_w_out', 'grad_l1_ln1_g', 'grad_l1_ln1_b', 'grad_ffn1_w_up', 'grad_ffn1_conv_w', 'grad_ffn1_conv_b', 'grad_ffn1_w_down', 'grad_l1_ln2_g', 'grad_l1_ln2_b', 'delta_l0_w_in', 'delta_rwkv_mix', 'delta_rwkv_w0', 'delta_rwkv_w2', 'delta_rwkv_a0', 'delta_rwkv_a2', 'delta_rwkv_g2', 'delta_rwkv_k_k', 'delta_rwkv_k_a', 'delta_rwkv_r_k', 'delta_rwkv_ln_g', 'delta_rwkv_ln_b', 'delta_ssm_conv_w', 'delta_ssm_conv_b', 'delta_ssm_dt_bias', 'delta_ssm_a_log', 'delta_ssm_d', 'delta_ssm_norm_g', 'delta_l0_w_out', 'delta_l0_ln1_g', 'delta_l0_ln1_b', 'delta_ffn0_w_up', 'delta_ffn0_conv_w', 'delta_ffn0_conv_b', 'delta_ffn0_w_down', 'delta_l0_ln2_g', 'delta_l0_ln2_b', 'delta_l1_w_in', 'delta_mla_q_norm_g', 'delta_mla_w_uq', 'delta_mla_kv_norm_g', 'delta_mla_w_ukv', 'delta_l1_w_out', 'delta_l1_ln1_g', 'delta_l1_ln1_b', 'delta_ffn1_w_up', 'delta_ffn1_conv_w', 'delta_ffn1_conv_b', 'delta_ffn1_w_down', 'delta_l1_ln2_g', 'delta_l1_ln2_b', 'new_m_l0_w_in', 'new_m_rwkv_mix', 'new_m_rwkv_w0', 'new_m_rwkv_w2', 'new_m_rwkv_a0', 'new_m_rwkv_a2', 'new_m_rwkv_g2', 'new_m_rwkv_k_k', 'new_m_rwkv_k_a', 'new_m_rwkv_r_k', 'new_m_rwkv_ln_g', 'new_m_rwkv_ln_b', 'new_m_ssm_conv_w', 'new_m_ssm_conv_b', 'new_m_ssm_dt_bias', 'new_m_ssm_a_log', 'new_m_ssm_d', 'new_m_ssm_norm_g', 'new_m_l0_w_out', 'new_m_l0_ln1_g', 'new_m_l0_ln1_b', 'new_m_ffn0_w_up', 'new_m_ffn0_conv_w', 'new_m_ffn0_conv_b', 'new_m_ffn0_w_down', 'new_m_l0_ln2_g', 'new_m_l0_ln2_b', 'new_m_l1_w_in', 'new_m_mla_q_norm_g', 'new_m_mla_w_uq', 'new_m_mla_kv_norm_g', 'new_m_mla_w_ukv', 'new_m_l1_w_out', 'new_m_l1_ln1_g', 'new_m_l1_ln1_b', 'new_m_ffn1_w_up', 'new_m_ffn1_conv_w', 'new_m_ffn1_conv_b', 'new_m_ffn1_w_down', 'new_m_l1_ln2_g', 'new_m_l1_ln2_b', 'new_v_l0_w_in', 'new_v_rwkv_mix', 'new_v_rwkv_w0', 'new_v_rwkv_w2', 'new_v_rwkv_a0', 'new_v_rwkv_a2', 'new_v_rwkv_g2', 'new_v_rwkv_k_k', 'new_v_rwkv_k_a', 'new_v_rwkv_r_k', 'new_v_rwkv_ln_g', 'new_v_rwkv_ln_b', 'new_v_ssm_conv_w', 'new_v_ssm_conv_b', 'new_v_ssm_dt_bias', 'new_v_ssm_a_log', 'new_v_ssm_d', 'new_v_ssm_norm_g', 'new_v_l0_w_out', 'new_v_l0_ln1_g', 'new_v_l0_ln1_b', 'new_v_ffn0_w_up', 'new_v_ffn0_conv_w', 'new_v_ffn0_conv_b', 'new_v_ffn0_w_down', 'new_v_l0_ln2_g', 'new_v_l0_ln2_b', 'new_v_l1_w_in', 'new_v_mla_q_norm_g', 'new_v_mla_w_uq', 'new_v_mla_kv_norm_g', 'new_v_mla_w_ukv', 'new_v_l1_w_out', 'new_v_l1_ln1_g', 'new_v_l1_ln1_b', 'new_v_ffn1_w_up', 'new_v_ffn1_conv_w', 'new_v_ffn1_conv_b', 'new_v_ffn1_w_down', 'new_v_l1_ln2_g', 'new_v_l1_ln2_b']
TWIN_LEAF_KINDS = {'loss': 'loss', 'grad_x': 'grad_x', 'grad_l0_w_in': 'grad_w', 'grad_rwkv_mix': 'grad_w', 'grad_rwkv_w0': 'grad_w', 'grad_rwkv_w2': 'grad_w', 'grad_rwkv_a0': 'grad_w', 'grad_rwkv_a2': 'grad_w', 'grad_rwkv_g2': 'grad_w', 'grad_rwkv_k_k': 'grad_w', 'grad_rwkv_k_a': 'grad_w', 'grad_rwkv_r_k': 'grad_w', 'grad_rwkv_ln_g': 'grad_w', 'grad_rwkv_ln_b': 'grad_w', 'grad_ssm_conv_w': 'grad_w', 'grad_ssm_conv_b': 'grad_w', 'grad_ssm_dt_bias': 'grad_w', 'grad_ssm_a_log': 'grad_w', 'grad_ssm_d': 'grad_w', 'grad_ssm_norm_g': 'grad_w', 'grad_l0_w_out': 'grad_w', 'grad_l0_ln1_g': 'grad_w', 'grad_l0_ln1_b': 'grad_w', 'grad_ffn0_w_up': 'grad_w', 'grad_ffn0_conv_w': 'grad_w', 'grad_ffn0_conv_b': 'grad_w', 'grad_ffn0_w_down': 'grad_w', 'grad_l0_ln2_g': 'grad_w', 'grad_l0_ln2_b': 'grad_w', 'grad_l1_w_in': 'grad_w', 'grad_mla_q_norm_g': 'grad_w', 'grad_mla_w_uq': 'grad_w', 'grad_mla_kv_norm_g': 'grad_w', 'grad_mla_w_ukv': 'grad_w', 'grad_l1_w_out': 'grad_w', 'grad_l1_ln1_g': 'grad_w', 'grad_l1_ln1_b': 'grad_w', 'grad_ffn1_w_up': 'grad_w', 'grad_ffn1_conv_w': 'grad_w', 'grad_ffn1_conv_b': 'grad_w', 'grad_ffn1_w_down': 'grad_w', 'grad_l1_ln2_g': 'grad_w', 'grad_l1_ln2_b': 'grad_w', 'delta_l0_w_in': 'delta_w', 'delta_rwkv_mix': 'delta_w', 'delta_rwkv_w0': 'delta_w', 'delta_rwkv_w2': 'delta_w', 'delta_rwkv_a0': 'delta_w', 'delta_rwkv_a2': 'delta_w', 'delta_rwkv_g2': 'delta_w', 'delta_rwkv_k_k': 'delta_w', 'delta_rwkv_k_a': 'delta_w', 'delta_rwkv_r_k': 'delta_w', 'delta_rwkv_ln_g': 'delta_w', 'delta_rwkv_ln_b': 'delta_w', 'delta_ssm_conv_w': 'delta_w', 'delta_ssm_conv_b': 'delta_w', 'delta_ssm_dt_bias': 'delta_w', 'delta_ssm_a_log': 'delta_w', 'delta_ssm_d': 'delta_w', 'delta_ssm_norm_g': 'delta_w', 'delta_l0_w_out': 'delta_w', 'delta_l0_ln1_g': 'delta_w', 'delta_l0_ln1_b': 'delta_w', 'delta_ffn0_w_up': 'delta_w', 'delta_ffn0_conv_w': 'delta_w', 'delta_ffn0_conv_b': 'delta_w', 'delta_ffn0_w_down': 'delta_w', 'delta_l0_ln2_g': 'delta_w', 'delta_l0_ln2_b': 'delta_w', 'delta_l1_w_in': 'delta_w', 'delta_mla_q_norm_g': 'delta_w', 'delta_mla_w_uq': 'delta_w', 'delta_mla_kv_norm_g': 'delta_w', 'delta_mla_w_ukv': 'delta_w', 'delta_l1_w_out': 'delta_w', 'delta_l1_ln1_g': 'delta_w', 'delta_l1_ln1_b': 'delta_w', 'delta_ffn1_w_up': 'delta_w', 'delta_ffn1_conv_w': 'delta_w', 'delta_ffn1_conv_b': 'delta_w', 'delta_ffn1_w_down': 'delta_w', 'delta_l1_ln2_g': 'delta_w', 'delta_l1_ln2_b': 'delta_w', 'new_m_l0_w_in': 'new_m', 'new_m_rwkv_mix': 'new_m', 'new_m_rwkv_w0': 'new_m', 'new_m_rwkv_w2': 'new_m', 'new_m_rwkv_a0': 'new_m', 'new_m_rwkv_a2': 'new_m', 'new_m_rwkv_g2': 'new_m', 'new_m_rwkv_k_k': 'new_m', 'new_m_rwkv_k_a': 'new_m', 'new_m_rwkv_r_k': 'new_m', 'new_m_rwkv_ln_g': 'new_m', 'new_m_rwkv_ln_b': 'new_m', 'new_m_ssm_conv_w': 'new_m', 'new_m_ssm_conv_b': 'new_m', 'new_m_ssm_dt_bias': 'new_m', 'new_m_ssm_a_log': 'new_m', 'new_m_ssm_d': 'new_m', 'new_m_ssm_norm_g': 'new_m', 'new_m_l0_w_out': 'new_m', 'new_m_l0_ln1_g': 'new_m', 'new_m_l0_ln1_b': 'new_m', 'new_m_ffn0_w_up': 'new_m', 'new_m_ffn0_conv_w': 'new_m', 'new_m_ffn0_conv_b': 'new_m', 'new_m_ffn0_w_down': 'new_m', 'new_m_l0_ln2_g': 'new_m', 'new_m_l0_ln2_b': 'new_m', 'new_m_l1_w_in': 'new_m', 'new_m_mla_q_norm_g': 'new_m', 'new_m_mla_w_uq': 'new_m', 'new_m_mla_kv_norm_g': 'new_m', 'new_m_mla_w_ukv': 'new_m', 'new_m_l1_w_out': 'new_m', 'new_m_l1_ln1_g': 'new_m', 'new_m_l1_ln1_b': 'new_m', 'new_m_ffn1_w_up': 'new_m', 'new_m_ffn1_conv_w': 'new_m', 'new_m_ffn1_conv_b': 'new_m', 'new_m_ffn1_w_down': 'new_m', 'new_m_l1_ln2_g': 'new_m', 'new_m_l1_ln2_b': 'new_m', 'new_v_l0_w_in': 'new_v', 'new_v_rwkv_mix': 'new_v', 'new_v_rwkv_w0': 'new_v', 'new_v_rwkv_w2': 'new_v', 'new_v_rwkv_a0': 'new_v', 'new_v_rwkv_a2': 'new_v', 'new_v_rwkv_g2': 'new_v', 'new_v_rwkv_k_k': 'new_v', 'new_v_rwkv_k_a': 'new_v', 'new_v_rwkv_r_k': 'new_v', 'new_v_rwkv_ln_g': 'new_v', 'new_v_rwkv_ln_b': 'new_v', 'new_v_ssm_conv_w': 'new_v', 'new_v_ssm_conv_b': 'new_v', 'new_v_ssm_dt_bias': 'new_v', 'new_v_ssm_a_log': 'new_v', 'new_v_ssm_d': 'new_v', 'new_v_ssm_norm_g': 'new_v', 'new_v_l0_w_out': 'new_v', 'new_v_l0_ln1_g': 'new_v', 'new_v_l0_ln1_b': 'new_v', 'new_v_ffn0_w_up': 'new_v', 'new_v_ffn0_conv_w': 'new_v', 'new_v_ffn0_conv_b': 'new_v', 'new_v_ffn0_w_down': 'new_v', 'new_v_l0_ln2_g': 'new_v', 'new_v_l0_ln2_b': 'new_v', 'new_v_l1_w_in': 'new_v', 'new_v_mla_q_norm_g': 'new_v', 'new_v_mla_w_uq': 'new_v', 'new_v_mla_kv_norm_g': 'new_v', 'new_v_mla_w_ukv': 'new_v', 'new_v_l1_w_out': 'new_v', 'new_v_l1_ln1_g': 'new_v', 'new_v_l1_ln1_b': 'new_v', 'new_v_ffn1_w_up': 'new_v', 'new_v_ffn1_conv_w': 'new_v', 'new_v_ffn1_conv_b': 'new_v', 'new_v_ffn1_w_down': 'new_v', 'new_v_l1_ln2_g': 'new_v', 'new_v_l1_ln2_b': 'new_v'}


def _forward(args):
    return _fwd_reference(*[args[k] for k in FWD_PARAMS])


def _output_shape():
    out = _jax.eval_shape(lambda: _forward(_fwd_setup_inputs(0)))
    return out.shape, out.dtype

N_MICROBATCH = 1
ADAM_LR = 0.001
ADAM_B1 = 0.9
ADAM_B2 = 0.999
ADAM_EPS = 1e-08
ADAM_WD = 0.01
ADAM_STEP = 10
PER_EXAMPLE_BATCH_AXIS = {'x': 0, 'positions': 0, 'loss_target': 0}
SHARED_INPUTS = []
_WEIGHT_DTYPES = {'l0_w_in': _jnp.float32, 'rwkv_mix': _jnp.float32, 'rwkv_w0': _jnp.float32, 'rwkv_w2': _jnp.float32, 'rwkv_a0': _jnp.float32, 'rwkv_a2': _jnp.float32, 'rwkv_g2': _jnp.float32, 'rwkv_k_k': _jnp.float32, 'rwkv_k_a': _jnp.float32, 'rwkv_r_k': _jnp.float32, 'rwkv_ln_g': _jnp.float32, 'rwkv_ln_b': _jnp.float32, 'ssm_conv_w': _jnp.float32, 'ssm_conv_b': _jnp.float32, 'ssm_dt_bias': _jnp.float32, 'ssm_a_log': _jnp.float32, 'ssm_d': _jnp.float32, 'ssm_norm_g': _jnp.float32, 'l0_w_out': _jnp.float32, 'l0_ln1_g': _jnp.float32, 'l0_ln1_b': _jnp.float32, 'ffn0_w_up': _jnp.float32, 'ffn0_conv_w': _jnp.float32, 'ffn0_conv_b': _jnp.float32, 'ffn0_w_down': _jnp.float32, 'l0_ln2_g': _jnp.float32, 'l0_ln2_b': _jnp.float32, 'l1_w_in': _jnp.float32, 'mla_q_norm_g': _jnp.float32, 'mla_w_uq': _jnp.float32, 'mla_kv_norm_g': _jnp.float32, 'mla_w_ukv': _jnp.float32, 'l1_w_out': _jnp.float32, 'l1_ln1_g': _jnp.float32, 'l1_ln1_b': _jnp.float32, 'ffn1_w_up': _jnp.float32, 'ffn1_conv_w': _jnp.float32, 'ffn1_conv_b': _jnp.float32, 'ffn1_w_down': _jnp.float32, 'l1_ln2_g': _jnp.float32, 'l1_ln2_b': _jnp.float32}
MOMENT_SCALE = {'l0_w_in': 4.523575e-02, 'rwkv_mix': 6.010474e-02, 'rwkv_w0': 1.531939e-02, 'rwkv_w2': 1.690044e-03, 'rwkv_a0': 1.451514e-02, 'rwkv_a2': 1.298113e-02, 'rwkv_g2': 3.436378e-02, 'rwkv_k_k': 5.614035e-02, 'rwkv_k_a': 3.705775e-02, 'rwkv_r_k': 7.648597e-02, 'rwkv_ln_g': 3.727840e-02, 'rwkv_ln_b': 8.494815e-02, 'ssm_conv_w': 4.794944e-02, 'ssm_conv_b': 6.254287e-02, 'ssm_dt_bias': 9.600628e-02, 'ssm_a_log': 1.114825e-01, 'ssm_d': 5.286325e-01, 'ssm_norm_g': 6.647160e-02, 'l0_w_out': 1.057141e-01, 'l0_ln1_g': 9.091782e-01, 'l0_ln1_b': 4.357966e-01, 'ffn0_w_up': 2.391175e-02, 'ffn0_conv_w': 2.420510e-02, 'ffn0_conv_b': 2.340580e-02, 'ffn0_w_down': 7.801777e-02, 'l0_ln2_g': 9.688892e-01, 'l0_ln2_b': 4.394867e-01, 'l1_w_in': 2.711051e-02, 'mla_q_norm_g': 1.724364e-02, 'mla_w_uq': 9.590025e-03, 'mla_kv_norm_g': 4.164793e-02, 'mla_w_ukv': 1.247876e-02, 'l1_w_out': 6.123996e-02, 'l1_ln1_g': 1.007614e+00, 'l1_ln1_b': 4.236351e-01, 'ffn1_w_up': 2.270470e-02, 'ffn1_conv_w': 2.329686e-02, 'ffn1_conv_b': 2.217944e-02, 'ffn1_w_down': 7.433745e-02, 'l1_ln2_g': 3.205926e+01, 'l1_ln2_b': 1.432899e+00}


def _to_microbatches(a, axis):
    t = _jnp.moveaxis(a, axis, 0)
    t = t.reshape((N_MICROBATCH, t.shape[0] // N_MICROBATCH) + t.shape[1:])
    return _jnp.moveaxis(t, 1, axis + 1)


def setup_inputs(seed: int = 0) -> dict:
    inp = _fwd_setup_inputs(seed)
    key = _jax.random.fold_in(_jax.random.key(seed), 7919)
    shape, _ = _output_shape()
    out = dict(inp)
    out["loss_target"] = _jax.random.normal(_jax.random.fold_in(key, 0), shape, _jnp.float32)
    for i, name in enumerate(TWIN_WEIGHTS):
        w = inp[name].astype(_jnp.float32)
        if MOMENT_SCALE is None:
            s = _jnp.sqrt(_jnp.mean(_jnp.square(w)) + 1e-30)
        else:
            s = MOMENT_SCALE[name]
        km, kv = _jax.random.split(_jax.random.fold_in(key, i + 1))
        out[name] = w
        out["m_" + name] = s * _jax.random.normal(km, w.shape, _jnp.float32)
        out["v_" + name] = (s * s) * _jax.random.uniform(kv, w.shape, _jnp.float32, 0.5, 1.5)
    if N_MICROBATCH > 1:
        for name, axis in PER_EXAMPLE_BATCH_AXIS.items():
            out[name] = _to_microbatches(out[name], axis)
    return {'x': out['x'], 'positions': out['positions'], 'l0_w_in': out['l0_w_in'], 'rwkv_mix': out['rwkv_mix'], 'rwkv_w0': out['rwkv_w0'], 'rwkv_w2': out['rwkv_w2'], 'rwkv_a0': out['rwkv_a0'], 'rwkv_a2': out['rwkv_a2'], 'rwkv_g2': out['rwkv_g2'], 'rwkv_k_k': out['rwkv_k_k'], 'rwkv_k_a': out['rwkv_k_a'], 'rwkv_r_k': out['rwkv_r_k'], 'rwkv_ln_g': out['rwkv_ln_g'], 'rwkv_ln_b': out['rwkv_ln_b'], 'ssm_conv_w': out['ssm_conv_w'], 'ssm_conv_b': out['ssm_conv_b'], 'ssm_dt_bias': out['ssm_dt_bias'], 'ssm_a_log': out['ssm_a_log'], 'ssm_d': out['ssm_d'], 'ssm_norm_g': out['ssm_norm_g'], 'l0_w_out': out['l0_w_out'], 'l0_ln1_g': out['l0_ln1_g'], 'l0_ln1_b': out['l0_ln1_b'], 'ffn0_w_up': out['ffn0_w_up'], 'ffn0_conv_w': out['ffn0_conv_w'], 'ffn0_conv_b': out['ffn0_conv_b'], 'ffn0_w_down': out['ffn0_w_down'], 'l0_ln2_g': out['l0_ln2_g'], 'l0_ln2_b': out['l0_ln2_b'], 'l1_w_in': out['l1_w_in'], 'mla_q_norm_g': out['mla_q_norm_g'], 'mla_w_uq': out['mla_w_uq'], 'mla_kv_norm_g': out['mla_kv_norm_g'], 'mla_w_ukv': out['mla_w_ukv'], 'l1_w_out': out['l1_w_out'], 'l1_ln1_g': out['l1_ln1_g'], 'l1_ln1_b': out['l1_ln1_b'], 'ffn1_w_up': out['ffn1_w_up'], 'ffn1_conv_w': out['ffn1_conv_w'], 'ffn1_conv_b': out['ffn1_conv_b'], 'ffn1_w_down': out['ffn1_w_down'], 'l1_ln2_g': out['l1_ln2_g'], 'l1_ln2_b': out['l1_ln2_b'], 'loss_target': out['loss_target'], 'm_l0_w_in': out['m_l0_w_in'], 'm_rwkv_mix': out['m_rwkv_mix'], 'm_rwkv_w0': out['m_rwkv_w0'], 'm_rwkv_w2': out['m_rwkv_w2'], 'm_rwkv_a0': out['m_rwkv_a0'], 'm_rwkv_a2': out['m_rwkv_a2'], 'm_rwkv_g2': out['m_rwkv_g2'], 'm_rwkv_k_k': out['m_rwkv_k_k'], 'm_rwkv_k_a': out['m_rwkv_k_a'], 'm_rwkv_r_k': out['m_rwkv_r_k'], 'm_rwkv_ln_g': out['m_rwkv_ln_g'], 'm_rwkv_ln_b': out['m_rwkv_ln_b'], 'm_ssm_conv_w': out['m_ssm_conv_w'], 'm_ssm_conv_b': out['m_ssm_conv_b'], 'm_ssm_dt_bias': out['m_ssm_dt_bias'], 'm_ssm_a_log': out['m_ssm_a_log'], 'm_ssm_d': out['m_ssm_d'], 'm_ssm_norm_g': out['m_ssm_norm_g'], 'm_l0_w_out': out['m_l0_w_out'], 'm_l0_ln1_g': out['m_l0_ln1_g'], 'm_l0_ln1_b': out['m_l0_ln1_b'], 'm_ffn0_w_up': out['m_ffn0_w_up'], 'm_ffn0_conv_w': out['m_ffn0_conv_w'], 'm_ffn0_conv_b': out['m_ffn0_conv_b'], 'm_ffn0_w_down': out['m_ffn0_w_down'], 'm_l0_ln2_g': out['m_l0_ln2_g'], 'm_l0_ln2_b': out['m_l0_ln2_b'], 'm_l1_w_in': out['m_l1_w_in'], 'm_mla_q_norm_g': out['m_mla_q_norm_g'], 'm_mla_w_uq': out['m_mla_w_uq'], 'm_mla_kv_norm_g': out['m_mla_kv_norm_g'], 'm_mla_w_ukv': out['m_mla_w_ukv'], 'm_l1_w_out': out['m_l1_w_out'], 'm_l1_ln1_g': out['m_l1_ln1_g'], 'm_l1_ln1_b': out['m_l1_ln1_b'], 'm_ffn1_w_up': out['m_ffn1_w_up'], 'm_ffn1_conv_w': out['m_ffn1_conv_w'], 'm_ffn1_conv_b': out['m_ffn1_conv_b'], 'm_ffn1_w_down': out['m_ffn1_w_down'], 'm_l1_ln2_g': out['m_l1_ln2_g'], 'm_l1_ln2_b': out['m_l1_ln2_b'], 'v_l0_w_in': out['v_l0_w_in'], 'v_rwkv_mix': out['v_rwkv_mix'], 'v_rwkv_w0': out['v_rwkv_w0'], 'v_rwkv_w2': out['v_rwkv_w2'], 'v_rwkv_a0': out['v_rwkv_a0'], 'v_rwkv_a2': out['v_rwkv_a2'], 'v_rwkv_g2': out['v_rwkv_g2'], 'v_rwkv_k_k': out['v_rwkv_k_k'], 'v_rwkv_k_a': out['v_rwkv_k_a'], 'v_rwkv_r_k': out['v_rwkv_r_k'], 'v_rwkv_ln_g': out['v_rwkv_ln_g'], 'v_rwkv_ln_b': out['v_rwkv_ln_b'], 'v_ssm_conv_w': out['v_ssm_conv_w'], 'v_ssm_conv_b': out['v_ssm_conv_b'], 'v_ssm_dt_bias': out['v_ssm_dt_bias'], 'v_ssm_a_log': out['v_ssm_a_log'], 'v_ssm_d': out['v_ssm_d'], 'v_ssm_norm_g': out['v_ssm_norm_g'], 'v_l0_w_out': out['v_l0_w_out'], 'v_l0_ln1_g': out['v_l0_ln1_g'], 'v_l0_ln1_b': out['v_l0_ln1_b'], 'v_ffn0_w_up': out['v_ffn0_w_up'], 'v_ffn0_conv_w': out['v_ffn0_conv_w'], 'v_ffn0_conv_b': out['v_ffn0_conv_b'], 'v_ffn0_w_down': out['v_ffn0_w_down'], 'v_l0_ln2_g': out['v_l0_ln2_g'], 'v_l0_ln2_b': out['v_l0_ln2_b'], 'v_l1_w_in': out['v_l1_w_in'], 'v_mla_q_norm_g': out['v_mla_q_norm_g'], 'v_mla_w_uq': out['v_mla_w_uq'], 'v_mla_kv_norm_g': out['v_mla_kv_norm_g'], 'v_mla_w_ukv': out['v_mla_w_ukv'], 'v_l1_w_out': out['v_l1_w_out'], 'v_l1_ln1_g': out['v_l1_ln1_g'], 'v_l1_ln1_b': out['v_l1_ln1_b'], 'v_ffn1_w_up': out['v_ffn1_w_up'], 'v_ffn1_conv_w': out['v_ffn1_conv_w'], 'v_ffn1_conv_b': out['v_ffn1_conv_b'], 'v_ffn1_w_down': out['v_ffn1_w_down'], 'v_l1_ln2_g': out['v_l1_ln2_g'], 'v_l1_ln2_b': out['v_l1_ln2_b']}


def _loss(weights, diff, rest, loss_target):
    with _jax.named_scope("forward"):
        args = {**rest, TWIN_DIFF_INPUT: diff, **{k: w.astype(_WEIGHT_DTYPES[k]) for k, w in weights.items()}}
        y = _forward(args)
    with _jax.named_scope("loss_head"):
        err = _jnp.square(y.astype(_jnp.float32) - loss_target)
        return 0.5 * _jnp.sum(_jnp.mean(err, axis=-1)) if err.ndim else 0.5 * err


def _adamw(w, g, m, v):
    m = ADAM_B1 * m + (1.0 - ADAM_B1) * g
    v = ADAM_B2 * v + (1.0 - ADAM_B2) * _jnp.square(g)
    m_hat = m / (1.0 - ADAM_B1 ** ADAM_STEP)
    v_hat = v / (1.0 - ADAM_B2 ** ADAM_STEP)
    delta = -ADAM_LR * (m_hat / (_jnp.sqrt(v_hat) + ADAM_EPS) + ADAM_WD * w)
    return delta, m, v


def reference(x, positions, l0_w_in, rwkv_mix, rwkv_w0, rwkv_w2, rwkv_a0, rwkv_a2, rwkv_g2, rwkv_k_k, rwkv_k_a, rwkv_r_k, rwkv_ln_g, rwkv_ln_b, ssm_conv_w, ssm_conv_b, ssm_dt_bias, ssm_a_log, ssm_d, ssm_norm_g, l0_w_out, l0_ln1_g, l0_ln1_b, ffn0_w_up, ffn0_conv_w, ffn0_conv_b, ffn0_w_down, l0_ln2_g, l0_ln2_b, l1_w_in, mla_q_norm_g, mla_w_uq, mla_kv_norm_g, mla_w_ukv, l1_w_out, l1_ln1_g, l1_ln1_b, ffn1_w_up, ffn1_conv_w, ffn1_conv_b, ffn1_w_down, l1_ln2_g, l1_ln2_b, loss_target, m_l0_w_in, m_rwkv_mix, m_rwkv_w0, m_rwkv_w2, m_rwkv_a0, m_rwkv_a2, m_rwkv_g2, m_rwkv_k_k, m_rwkv_k_a, m_rwkv_r_k, m_rwkv_ln_g, m_rwkv_ln_b, m_ssm_conv_w, m_ssm_conv_b, m_ssm_dt_bias, m_ssm_a_log, m_ssm_d, m_ssm_norm_g, m_l0_w_out, m_l0_ln1_g, m_l0_ln1_b, m_ffn0_w_up, m_ffn0_conv_w, m_ffn0_conv_b, m_ffn0_w_down, m_l0_ln2_g, m_l0_ln2_b, m_l1_w_in, m_mla_q_norm_g, m_mla_w_uq, m_mla_kv_norm_g, m_mla_w_ukv, m_l1_w_out, m_l1_ln1_g, m_l1_ln1_b, m_ffn1_w_up, m_ffn1_conv_w, m_ffn1_conv_b, m_ffn1_w_down, m_l1_ln2_g, m_l1_ln2_b, v_l0_w_in, v_rwkv_mix, v_rwkv_w0, v_rwkv_w2, v_rwkv_a0, v_rwkv_a2, v_rwkv_g2, v_rwkv_k_k, v_rwkv_k_a, v_rwkv_r_k, v_rwkv_ln_g, v_rwkv_ln_b, v_ssm_conv_w, v_ssm_conv_b, v_ssm_dt_bias, v_ssm_a_log, v_ssm_d, v_ssm_norm_g, v_l0_w_out, v_l0_ln1_g, v_l0_ln1_b, v_ffn0_w_up, v_ffn0_conv_w, v_ffn0_conv_b, v_ffn0_w_down, v_l0_ln2_g, v_l0_ln2_b, v_l1_w_in, v_mla_q_norm_g, v_mla_w_uq, v_mla_kv_norm_g, v_mla_w_ukv, v_l1_w_out, v_l1_ln1_g, v_l1_ln1_b, v_ffn1_w_up, v_ffn1_conv_w, v_ffn1_conv_b, v_ffn1_w_down, v_l1_ln2_g, v_l1_ln2_b):
    given = dict(x=x, positions=positions, l0_w_in=l0_w_in, rwkv_mix=rwkv_mix, rwkv_w0=rwkv_w0, rwkv_w2=rwkv_w2, rwkv_a0=rwkv_a0, rwkv_a2=rwkv_a2, rwkv_g2=rwkv_g2, rwkv_k_k=rwkv_k_k, rwkv_k_a=rwkv_k_a, rwkv_r_k=rwkv_r_k, rwkv_ln_g=rwkv_ln_g, rwkv_ln_b=rwkv_ln_b, ssm_conv_w=ssm_conv_w, ssm_conv_b=ssm_conv_b, ssm_dt_bias=ssm_dt_bias, ssm_a_log=ssm_a_log, ssm_d=ssm_d, ssm_norm_g=ssm_norm_g, l0_w_out=l0_w_out, l0_ln1_g=l0_ln1_g, l0_ln1_b=l0_ln1_b, ffn0_w_up=ffn0_w_up, ffn0_conv_w=ffn0_conv_w, ffn0_conv_b=ffn0_conv_b, ffn0_w_down=ffn0_w_down, l0_ln2_g=l0_ln2_g, l0_ln2_b=l0_ln2_b, l1_w_in=l1_w_in, mla_q_norm_g=mla_q_norm_g, mla_w_uq=mla_w_uq, mla_kv_norm_g=mla_kv_norm_g, mla_w_ukv=mla_w_ukv, l1_w_out=l1_w_out, l1_ln1_g=l1_ln1_g, l1_ln1_b=l1_ln1_b, ffn1_w_up=ffn1_w_up, ffn1_conv_w=ffn1_conv_w, ffn1_conv_b=ffn1_conv_b, ffn1_w_down=ffn1_w_down, l1_ln2_g=l1_ln2_g, l1_ln2_b=l1_ln2_b, loss_target=loss_target, m_l0_w_in=m_l0_w_in, m_rwkv_mix=m_rwkv_mix, m_rwkv_w0=m_rwkv_w0, m_rwkv_w2=m_rwkv_w2, m_rwkv_a0=m_rwkv_a0, m_rwkv_a2=m_rwkv_a2, m_rwkv_g2=m_rwkv_g2, m_rwkv_k_k=m_rwkv_k_k, m_rwkv_k_a=m_rwkv_k_a, m_rwkv_r_k=m_rwkv_r_k, m_rwkv_ln_g=m_rwkv_ln_g, m_rwkv_ln_b=m_rwkv_ln_b, m_ssm_conv_w=m_ssm_conv_w, m_ssm_conv_b=m_ssm_conv_b, m_ssm_dt_bias=m_ssm_dt_bias, m_ssm_a_log=m_ssm_a_log, m_ssm_d=m_ssm_d, m_ssm_norm_g=m_ssm_norm_g, m_l0_w_out=m_l0_w_out, m_l0_ln1_g=m_l0_ln1_g, m_l0_ln1_b=m_l0_ln1_b, m_ffn0_w_up=m_ffn0_w_up, m_ffn0_conv_w=m_ffn0_conv_w, m_ffn0_conv_b=m_ffn0_conv_b, m_ffn0_w_down=m_ffn0_w_down, m_l0_ln2_g=m_l0_ln2_g, m_l0_ln2_b=m_l0_ln2_b, m_l1_w_in=m_l1_w_in, m_mla_q_norm_g=m_mla_q_norm_g, m_mla_w_uq=m_mla_w_uq, m_mla_kv_norm_g=m_mla_kv_norm_g, m_mla_w_ukv=m_mla_w_ukv, m_l1_w_out=m_l1_w_out, m_l1_ln1_g=m_l1_ln1_g, m_l1_ln1_b=m_l1_ln1_b, m_ffn1_w_up=m_ffn1_w_up, m_ffn1_conv_w=m_ffn1_conv_w, m_ffn1_conv_b=m_ffn1_conv_b, m_ffn1_w_down=m_ffn1_w_down, m_l1_ln2_g=m_l1_ln2_g, m_l1_ln2_b=m_l1_ln2_b, v_l0_w_in=v_l0_w_in, v_rwkv_mix=v_rwkv_mix, v_rwkv_w0=v_rwkv_w0, v_rwkv_w2=v_rwkv_w2, v_rwkv_a0=v_rwkv_a0, v_rwkv_a2=v_rwkv_a2, v_rwkv_g2=v_rwkv_g2, v_rwkv_k_k=v_rwkv_k_k, v_rwkv_k_a=v_rwkv_k_a, v_rwkv_r_k=v_rwkv_r_k, v_rwkv_ln_g=v_rwkv_ln_g, v_rwkv_ln_b=v_rwkv_ln_b, v_ssm_conv_w=v_ssm_conv_w, v_ssm_conv_b=v_ssm_conv_b, v_ssm_dt_bias=v_ssm_dt_bias, v_ssm_a_log=v_ssm_a_log, v_ssm_d=v_ssm_d, v_ssm_norm_g=v_ssm_norm_g, v_l0_w_out=v_l0_w_out, v_l0_ln1_g=v_l0_ln1_g, v_l0_ln1_b=v_l0_ln1_b, v_ffn0_w_up=v_ffn0_w_up, v_ffn0_conv_w=v_ffn0_conv_w, v_ffn0_conv_b=v_ffn0_conv_b, v_ffn0_w_down=v_ffn0_w_down, v_l0_ln2_g=v_l0_ln2_g, v_l0_ln2_b=v_l0_ln2_b, v_l1_w_in=v_l1_w_in, v_mla_q_norm_g=v_mla_q_norm_g, v_mla_w_uq=v_mla_w_uq, v_mla_kv_norm_g=v_mla_kv_norm_g, v_mla_w_ukv=v_mla_w_ukv, v_l1_w_out=v_l1_w_out, v_l1_ln1_g=v_l1_ln1_g, v_l1_ln1_b=v_l1_ln1_b, v_ffn1_w_up=v_ffn1_w_up, v_ffn1_conv_w=v_ffn1_conv_w, v_ffn1_conv_b=v_ffn1_conv_b, v_ffn1_w_down=v_ffn1_w_down, v_l1_ln2_g=v_l1_ln2_g, v_l1_ln2_b=v_l1_ln2_b)
    weights = {n: given[n] for n in TWIN_WEIGHTS}
    shared = {n: given[n] for n in SHARED_INPUTS}
    per_example = {n: given[n] for n in ['x', 'positions']}
    grad_fn = _jax.value_and_grad(_loss, argnums=(0, 1))

    def one_microbatch(ex, loss_target):
        ex = dict(ex)
        diff = ex.pop(TWIN_DIFF_INPUT)
        return grad_fn(weights, diff, {**shared, **ex}, loss_target)

    if N_MICROBATCH == 1:
        loss, (grad_w, grad_x) = one_microbatch(per_example, given["loss_target"])
    else:
        def body(carry, xs):
            loss_sum, grad_sum = carry
            l_k, (gw_k, gx_k) = one_microbatch(xs[0], xs[1])
            with _jax.named_scope("update"):
                return (loss_sum + l_k, _jax.tree.map(_jnp.add, grad_sum, gw_k)), gx_k

        init = (_jnp.zeros((), _jnp.float32), _jax.tree.map(_jnp.zeros_like, weights))
        (loss, grad_w), grad_x = _jax.lax.scan(body, init, (per_example, given["loss_target"]))
    with _jax.named_scope("update"):
        delta_w, new_m, new_v = {}, {}, {}
        for n in TWIN_WEIGHTS:
            delta_w[n], new_m[n], new_v[n] = _adamw(weights[n], grad_w[n], given["m_" + n], given["v_" + n])
    return (loss, grad_x, *[grad_w[n] for n in TWIN_WEIGHTS], *[delta_w[n] for n in TWIN_WEIGHTS],
            *[new_m[n] for n in TWIN_WEIGHTS], *[new_v[n] for n in TWIN_WEIGHTS])
```

```python
import functools
import math

import numpy as np
import jax
import jax.numpy as jnp
from jax import lax
from jax.experimental import pallas as pl
from jax.experimental.pallas import tpu as pltpu

f32 = jnp.float32
bf16 = jnp.bfloat16
HI = lax.Precision.HIGHEST

D_MODEL = 1024
HEAD_DIM = 64
N_HEADS = 8
RWKV_DIM = 512
RWKV_COLS = 1792
RWKV_GN_EPS = 64e-5
SSM_STATE = 128
SSM_CHUNK = 128
L0_COLS = 3336
L0_PAD = 3456
L1_COLS = 1952
L1_PAD = 2176
MLA_ROPE = 32
ROPE_THETA = 10000.0
D_FF = 2816
DEPTH = 2
ALPHA = (2 * DEPTH) ** 0.25
ADAM_LR = 0.001
ADAM_B1 = 0.9
ADAM_B2 = 0.999
ADAM_EPS = 1e-08
ADAM_WD = 0.01
ADAM_STEP = 10
RWKV_CHUNK = 64
LANE = 128
VMEM_LIMIT = 56 * 1024 * 1024

WEIGHTS = ['l0_w_in', 'rwkv_mix', 'rwkv_w0', 'rwkv_w2', 'rwkv_a0', 'rwkv_a2', 'rwkv_g2', 'rwkv_k_k', 'rwkv_k_a',
           'rwkv_r_k', 'rwkv_ln_g', 'rwkv_ln_b', 'ssm_conv_w', 'ssm_conv_b', 'ssm_dt_bias', 'ssm_a_log', 'ssm_d',
           'ssm_norm_g', 'l0_w_out', 'l0_ln1_g', 'l0_ln1_b', 'ffn0_w_up', 'ffn0_conv_w', 'ffn0_conv_b',
           'ffn0_w_down', 'l0_ln2_g', 'l0_ln2_b', 'l1_w_in', 'mla_q_norm_g', 'mla_w_uq', 'mla_kv_norm_g',
           'mla_w_ukv', 'l1_w_out', 'l1_ln1_g', 'l1_ln1_b', 'ffn1_w_up', 'ffn1_conv_w', 'ffn1_conv_b',
           'ffn1_w_down', 'l1_ln2_g', 'l1_ln2_b']
COL_SHARDED = ['l0_w_in', 'rwkv_w2', 'rwkv_a2', 'rwkv_g2', 'ssm_conv_w', 'ffn0_w_up', 'ffn0_conv_w', 'l1_w_in',
               'mla_w_uq', 'mla_w_ukv', 'ffn1_w_up', 'ffn1_conv_w']
ROW_SHARDED = ['l0_w_out', 'ffn0_w_down', 'l1_w_out', 'ffn1_w_down']
BIG = ['l0_w_in', 'l0_w_out', 'ffn0_w_up', 'ffn0_w_down', 'l1_w_in', 'l1_w_out', 'ffn1_w_up', 'ffn1_w_down']
N_SHARD = 4


def _cparams(sem):
    return pltpu.CompilerParams(dimension_semantics=sem, vmem_limit_bytes=VMEM_LIMIT)


def _dg(a, b, ca, cb, prec=None):
    return lax.dot_general(a, b, (((ca,), (cb,)), ((), ())), precision=prec, preferred_element_type=f32)


def hdot(a, b):
    return _dg(a, b, 1, 0, HI)


def hdot_nt(a, b):
    return _dg(a, b, 1, 1, HI)


def hdot_tn(a, b):
    return _dg(a, b, 0, 0, HI)


def _b(x):
    return x.astype(bf16)


@jax.custom_vjp
def bdot(x, w):
    return _dg(_b(x), _b(w), 1, 0)


def _bdot_fwd(x, w):
    return bdot(x, w), (x, w)


def _bdot_bwd(res, g):
    x, w = res
    return _dg(_b(g), _b(w), 1, 1).astype(x.dtype), _dg(_b(x), _b(g), 0, 0).astype(w.dtype)


bdot.defvjp(_bdot_fwd, _bdot_bwd)


@jax.custom_vjp
def bdot_nt(x, y):
    return _dg(_b(x), _b(y), 1, 1)


def _bdot_nt_fwd(x, y):
    return bdot_nt(x, y), (x, y)


def _bdot_nt_bwd(res, g):
    x, y = res
    return _dg(_b(g), _b(y), 1, 0), _dg(_b(g), _b(x), 0, 0)


bdot_nt.defvjp(_bdot_nt_fwd, _bdot_nt_bwd)


@jax.custom_vjp
def bdot_tn(x, y):
    return _dg(_b(x), _b(y), 0, 0)


def _bdot_tn_fwd(x, y):
    return bdot_tn(x, y), (x, y)


def _bdot_tn_bwd(res, g):
    x, y = res
    return _dg(_b(y), _b(g), 1, 1), _dg(_b(x), _b(g), 1, 0)


bdot_tn.defvjp(_bdot_tn_fwd, _bdot_tn_bwd)


def _sigmoid(x):
    return 1.0 / (1.0 + jnp.exp(-x))


@jax.custom_vjp
def softplus(x):
    e = jnp.exp(-jnp.abs(x))
    u = 1.0 + e
    log1p = jnp.where(u == 1.0, e, jnp.log(u) * e / jnp.where(u == 1.0, 1.0, u - 1.0))
    return jnp.maximum(x, 0.0) + log1p


def _softplus_fwd(x):
    return softplus(x), x


def _softplus_bwd(x, g):
    return (g * _sigmoid(x),)


softplus.defvjp(_softplus_fwd, _softplus_bwd)


def silu(x):
    return x * _sigmoid(x)


def _shift_rows(x, k, up):
    if k == 0:
        return x
    t = x.shape[0]
    rows = lax.broadcasted_iota(jnp.int32, x.shape, 0)
    if up:
        return jnp.where(rows < t - k, pltpu.roll(x, t - k, 0), 0.0)
    return jnp.where(rows >= k, pltpu.roll(x, k, 0), 0.0)


@functools.partial(jax.custom_vjp, nondiff_argnums=(1,))
def shift_down(x, k):
    return _shift_rows(x, k, False)


def _shift_down_fwd(x, k):
    return _shift_rows(x, k, False), None


def _shift_down_bwd(k, _, g):
    return (_shift_rows(g, k, True),)


shift_down.defvjp(_shift_down_fwd, _shift_down_bwd)


def _iota2(shape, axis):
    return lax.broadcasted_iota(jnp.int32, shape, axis)


class Op:
    def __init__(self, arr, block, imap, diff=True, acc=None, gshape=None, gimap=None):
        self.arr, self.block, self.imap, self.diff, self.acc = arr, tuple(block), imap, diff, acc
        self.gshape = tuple(arr.shape) if gshape is None else tuple(gshape)
        self.gimap = imap if gimap is None else gimap


class Out:
    def __init__(self, shape, block, imap, dtype=f32):
        self.shape, self.block, self.imap, self.dtype = tuple(shape), tuple(block), imap, dtype


def block_fwd(fn, name, grid, ops, outs):
    n_in = len(ops)

    def body(*refs):
        vals = [r[...] for r in refs[:n_in]]
        res = fn(*vals)
        for r, v in zip(refs[n_in:], res):
            r[...] = v.astype(r.dtype)

    res = pl.pallas_call(
        body, name=name, grid=grid,
        in_specs=[pl.BlockSpec(o.block, o.imap) for o in ops],
        out_specs=[pl.BlockSpec(o.block, o.imap) for o in outs],
        out_shape=[jax.ShapeDtypeStruct(o.shape, o.dtype) for o in outs],
        compiler_params=_cparams(("arbitrary", "arbitrary")),
    )(*[o.arr for o in ops])
    return tuple(res)


def block_bwd(fn, name, grid, ops, outs, douts):
    n_in, n_out = len(ops), len(outs)
    dix = [k for k, o in enumerate(ops) if o.diff]

    def body(*refs):
        vals = [r[...] for r in refs[:n_in]]
        dvals = tuple(r[...] for r in refs[n_in:n_in + n_out])
        grefs = refs[n_in + n_out:]

        def f(*d):
            full = list(vals)
            for k, v in zip(dix, d):
                full[k] = v
            return tuple(fn(*full))

        _, vjp = jax.vjp(f, *[vals[k] for k in dix])
        grads = vjp(dvals)
        j, i = pl.program_id(0), pl.program_id(1)
        for k, gref, g in zip(dix, grefs, grads):
            acc = ops[k].acc
            if acc is None:
                gref[...] = g.astype(gref.dtype)
            else:
                first = (i == 0) if acc == 'i' else jnp.logical_and(i == 0, j == 0)

                @pl.when(first)
                def _():
                    gref[...] = g

                @pl.when(jnp.logical_not(first))
                def _():
                    gref[...] += g

    gspecs = [pl.BlockSpec(ops[k].block, ops[k].gimap) for k in dix]
    gshapes = [jax.ShapeDtypeStruct(ops[k].gshape, f32) for k in dix]
    res = pl.pallas_call(
        body, name=name, grid=grid,
        in_specs=[pl.BlockSpec(o.block, o.imap) for o in ops] + [pl.BlockSpec(o.block, o.imap) for o in outs],
        out_specs=gspecs, out_shape=gshapes,
        compiler_params=_cparams(("arbitrary", "arbitrary")),
    )(*[o.arr for o in ops], *douts)
    return tuple(res)


def _rows(arr, tm, diff=True):
    return Op(arr, (tm, arr.shape[1]), lambda j, i: (i, 0), diff=diff)


def _param(arr, diff=True):
    return Op(arr, arr.shape, lambda j, i: (0,) * arr.ndim, diff=diff, acc='ij')


def _rows_out(n, c, tm):
    return Out((n, c), (tm, c), lambda j, i: (i, 0))


def _cols(arr, t, tc, off=0, width=None):
    width = arr.shape[1] if width is None else width
    return Op(arr, (t, tc), lambda j, i: (i, j + off), gshape=(arr.shape[0], width), gimap=lambda j, i: (i, j))


def _cparam(arr, tc):
    return Op(arr, (arr.shape[0], tc), lambda j, i: (0, j), acc='i')


def _tile(n, cap):
    best = None
    for t in range(LANE, min(n, cap) + 1, LANE):
        if n % t == 0:
            best = t
    return n if best is None else best


def mm(a, b, name, ta=False, tb=False, add=None):
    m = a.shape[1] if ta else a.shape[0]
    kd = a.shape[0] if ta else a.shape[1]
    n = b.shape[0] if tb else b.shape[1]
    tm, tn, tk = _tile(m, 512), _tile(n, 512), _tile(kd, 1024)
    nk = kd // tk
    ca, cb = (0 if ta else 1), (1 if tb else 0)

    def body(*refs):
        if add is None:
            a_ref, b_ref, o_ref, acc = refs
        else:
            a_ref, b_ref, add_ref, o_ref, acc = refs
        k = pl.program_id(2)

        @pl.when(k == 0)
        def _():
            acc[...] = jnp.zeros_like(acc)

        acc[...] += _dg(_b(a_ref[...]), _b(b_ref[...]), ca, cb)

        @pl.when(k == nk - 1)
        def _():
            o_ref[...] = acc[...] if add is None else acc[...] + add_ref[...]

    a_spec = pl.BlockSpec((tk, tm), lambda i, j, k: (k, i)) if ta else pl.BlockSpec((tm, tk), lambda i, j, k: (i, k))
    b_spec = pl.BlockSpec((tn, tk), lambda i, j, k: (j, k)) if tb else pl.BlockSpec((tk, tn), lambda i, j, k: (k, j))
    o_spec = pl.BlockSpec((tm, tn), lambda i, j, k: (i, j))
    args, specs = [a, b], [a_spec, b_spec]
    if add is not None:
        args.append(add)
        specs.append(o_spec)
    return pl.pallas_call(
        body, name=name, grid=(m // tm, n // tn, nk), in_specs=specs, out_specs=o_spec,
        out_shape=jax.ShapeDtypeStruct((m, n), f32), scratch_shapes=[pltpu.VMEM((tm, tn), f32)],
        compiler_params=_cparams(("parallel", "parallel", "arbitrary")),
    )(*args)


def f_ln(h, y, g, b):
    x = ALPHA * h + y
    mu = jnp.mean(x, axis=-1, keepdims=True)
    xc = x - mu
    var = jnp.mean(xc * xc, axis=-1, keepdims=True)
    return (xc * lax.rsqrt(var + 1e-5) * g + b,)


def f_shift_mix(p, mix):
    return (p + (shift_down(p, 1) - p) * mix,)


def f_rwkv_pre(k, wa_lo, g_lo, w0, w2, a0, a2, g2, k_k, k_a, gh):
    w_lo, a_lo = wa_lo[:, :64], wa_lo[:, 64:]
    log_w = -softplus(-(w0 + bdot(jnp.tanh(w_lo), w2))) - 0.5
    lw = -jnp.exp(log_w)
    a = _sigmoid(a0 + bdot(a_lo, a2))
    g = bdot(_sigmoid(g_lo), g2)
    kk = k * k_k
    kk = kk / jnp.maximum(jnp.sqrt(hdot(kk * kk, gh)), 1e-12)
    k2 = k * (1.0 + (a - 1.0) * k_a)
    return lw, k2, -kk, kk * a, g


def f_rwkv_post(y, r, k2, v, g, ln_g, ln_b, r_k, gh):
    mu = hdot(y, gh) * (1.0 / HEAD_DIM)
    yc = y - mu
    var = hdot(yc * yc, gh) * (1.0 / HEAD_DIM)
    yn = yc * lax.rsqrt(var + RWKV_GN_EPS) * ln_g + ln_b
    bonus = hdot(r * k2 * r_k, gh) * v
    return ((yn + bonus) * g,)


def f_conv4_silu(x, w0, w1, w2, w3, b):
    y = b + shift_down(x, 3) * w0 + shift_down(x, 2) * w1 + shift_down(x, 1) * w2 + x * w3
    return (silu(y),)


def f_ssm_post(y, z, norm_g, gg):
    u = y * silu(z)
    ms = hdot(u * u, gg) * (1.0 / 256.0)
    return (u * lax.rsqrt(ms + 1e-5) * norm_g,)


def f_ffn_act(gate, up, w0, w1, w2, b):
    gc = b + shift_down(gate, 2) * w0 + shift_down(gate, 1) * w1 + gate * w2
    return (silu(gc) * up,)


def _rms(x, g, eps=1e-6):
    return x * lax.rsqrt(jnp.mean(x * x, axis=-1, keepdims=True) + eps) * g


def f_mla_pre(c_q, c_kv, kx1, kx2, pos, q_g, w_qn, w_q1, w_q2, kv_g, w_ukv, inv8, inv1):
    qn_in = _rms(c_q, q_g)
    q_nope = bdot(qn_in, w_qn)
    q1 = bdot(qn_in, w_q1)
    q2 = bdot(qn_in, w_q2)
    kv = bdot(_rms(c_kv, kv_g), w_ukv)
    ang8 = pos * inv8
    c8, s8 = jnp.cos(ang8), jnp.sin(ang8)
    ang1 = pos * inv1
    c1, s1 = jnp.cos(ang1), jnp.sin(ang1)
    return (q_nope, q1 * c8 - q2 * s8, q2 * c8 + q1 * s8, kv, kx1 * c1 - kx2 * s1, kx2 * c1 + kx1 * s1)


def rwkv_chunk(s0, r, lw, k, v, a, b):
    l = r.shape[0]
    ri, ci = _iota2((l, l), 0), _iota2((l, l), 1)
    strict, incl = ri > ci, ri >= ci
    c = hdot(incl.astype(f32), lw)
    wt, wp, wi = jnp.exp(c), jnp.exp(c - lw), jnp.exp(-c)
    at, bt, kt, rt = a * wp, b * wi, k * wi, r * wt
    nab = jnp.where(strict, hdot_nt(at, bt), 0.0)
    nak = jnp.where(strict, hdot_nt(at, kt), 0.0)
    g = hdot_nt(at, s0) + hdot(nak, v)
    x = (ri == ci).astype(f32) + nab
    p = nab
    for _ in range(max(1, (l - 1).bit_length()) - 1):
        p = hdot(p, p)
        x = x + hdot(x, p)
    u = hdot(x, g)
    y = (hdot_nt(rt, s0) + hdot(jnp.where(incl, hdot_nt(rt, bt), 0.0), u)
         + hdot(jnp.where(incl, hdot_nt(rt, kt), 0.0), v))
    last = (_iota2((l, 1), 0) == l - 1).astype(f32)
    w_end = jnp.exp(jnp.sum(c * last, axis=0, keepdims=True))
    s1 = (s0 + hdot_tn(u, bt) + hdot_tn(v, kt)) * w_end
    return y, s1


def ssd_chunk(xs, bm, cm, dt_raw, s_in, dt_bias, a_log, d_skip, e_heads):
    l = xs.shape[0]
    ri, ci = _iota2((l, l), 0), _iota2((l, l), 1)
    incl = ri >= ci
    tri = incl.astype(f32)
    dt = softplus(dt_raw + dt_bias)
    a128 = dt * (-jnp.exp(a_log))
    lane0 = (_iota2((1, HEAD_DIM), 1) == 0).astype(f32)
    last = (_iota2((l, 1), 0) == l - 1).astype(f32)
    cb = [bdot_nt(cm[:, g * SSM_STATE:(g + 1) * SSM_STATE], bm[:, g * SSM_STATE:(g + 1) * SSM_STATE]) for g in range(2)]
    ys, s_out = [], []
    for h in range(N_HEADS):
        g = h // 4
        e = e_heads[h]
        x_h = xs[:, h * HEAD_DIM:(h + 1) * HEAD_DIM]
        dt_h = hdot(dt, e)
        ac = hdot(tri, hdot(a128, e))
        xd = x_h * dt_h
        col = jnp.broadcast_to(jnp.sum(ac * lane0, axis=1, keepdims=True), (l, l))
        decay = jnp.exp(jnp.where(incl, col - col.T, -1e30))
        y_diag = bdot(cb[g] * decay, xd)
        a_tot = jnp.sum(ac * last, axis=0, keepdims=True)
        b_g = bm[:, g * SSM_STATE:(g + 1) * SSM_STATE]
        c_g = cm[:, g * SSM_STATE:(g + 1) * SSM_STATE]
        s_new = jnp.exp(a_tot) * s_in[h] + bdot_tn(b_g, xd * jnp.exp(a_tot - ac))
        y_off = jnp.exp(ac) * bdot(c_g, s_in[h])
        ys.append(y_diag + y_off + x_h * hdot(jnp.broadcast_to(d_skip, (l, LANE)), e))
        s_out.append(s_new)
    return jnp.concatenate(ys, axis=1), s_out


def sb_block(q, k, v, q0):
    bq, t = q.shape[0], k.shape[0]
    scale = HEAD_DIM ** -0.5
    kb = LANE
    rows = q0 + _iota2((bq, kb), 0)
    upper = (_iota2((kb, kb), 0) > _iota2((kb, kb), 1)).astype(f32)
    run = jnp.zeros((bq, 1), f32)
    o = jnp.zeros((bq, HEAD_DIM), f32)
    for j in reversed(range(t // kb)):
        z = bdot_nt(q, k[j * kb:(j + 1) * kb]) * scale
        strict = (j * kb + _iota2((bq, kb), 1)) < rows
        lk = jnp.where(strict, -softplus(z), 0.0)
        log_att = z + lk + hdot(lk, upper) + run
        att = jnp.where(strict, jnp.exp(jnp.where(strict, log_att, 0.0)), 0.0)
        o = o + bdot(att, v[j * kb:(j + 1) * kb])
        run = run + jnp.sum(lk, axis=1, keepdims=True)
    return o


def mla_block(qn, qp, kn, kp, v, q0):
    bq, t = qn.shape[0], kn.shape[0]
    scale = (HEAD_DIM + MLA_ROPE) ** -0.5
    s = (bdot_nt(qn, kn) + bdot_nt(qp, kp)) * scale
    causal = _iota2((bq, t), 1) <= q0 + _iota2((bq, t), 0)
    s = jnp.where(causal, s, -1e30)
    m = jnp.max(s, axis=-1, keepdims=True)
    p = jnp.where(causal, jnp.exp(s - m), 0.0)
    p = p / jnp.sum(p, axis=-1, keepdims=True)
    return bdot(p, v)


def rwkv_scan_fwd(r, lw, k, v, a, b):
    bh, t, n = r.shape
    lc = RWKV_CHUNK
    nc = t // lc

    def body(r_ref, lw_ref, k_ref, v_ref, a_ref, b_ref, y_ref, s0_ref, s):
        @pl.when(pl.program_id(1) == 0)
        def _():
            s[...] = jnp.zeros_like(s)

        s0_ref[0, 0] = s[...]
        y, s1 = rwkv_chunk(s[...], r_ref[0], lw_ref[0], k_ref[0], v_ref[0], a_ref[0], b_ref[0])
        y_ref[0] = y
        s[...] = s1

    spec = pl.BlockSpec((1, lc, n), lambda i, j: (i, j, 0))
    return pl.pallas_call(
        body, name="rwkv_scan_fwd", grid=(bh, nc), in_specs=[spec] * 6,
        out_specs=[spec, pl.BlockSpec((1, 1, n, n), lambda i, j: (i, j, 0, 0))],
        out_shape=[jax.ShapeDtypeStruct((bh, t, n), f32), jax.ShapeDtypeStruct((bh, nc, n, n), f32)],
        scratch_shapes=[pltpu.VMEM((n, n), f32)],
        compiler_params=_cparams(("parallel", "arbitrary")),
    )(r, lw, k, v, a, b)


def rwkv_scan_bwd(s0, r, lw, k, v, a, b, dy):
    bh, t, n = r.shape
    lc = RWKV_CHUNK
    nc = t // lc

    def body(s0_ref, r_ref, lw_ref, k_ref, v_ref, a_ref, b_ref, dy_ref, *rest):
        grefs, ds = rest[:6], rest[6]

        @pl.when(pl.program_id(1) == 0)
        def _():
            ds[...] = jnp.zeros_like(ds)

        _, vjp = jax.vjp(rwkv_chunk, s0_ref[0, 0], r_ref[0], lw_ref[0], k_ref[0], v_ref[0], a_ref[0], b_ref[0])
        g = vjp((dy_ref[0], ds[...]))
        ds[...] = g[0]
        for ref, val in zip(grefs, g[1:]):
            ref[0] = val

    spec = pl.BlockSpec((1, lc, n), lambda i, j: (i, nc - 1 - j, 0))
    sspec = pl.BlockSpec((1, 1, n, n), lambda i, j: (i, nc - 1 - j, 0, 0))
    return pl.pallas_call(
        body, name="rwkv_scan_bwd", grid=(bh, nc), in_specs=[sspec] + [spec] * 7, out_specs=[spec] * 6,
        out_shape=[jax.ShapeDtypeStruct((bh, t, n), f32)] * 6, scratch_shapes=[pltpu.VMEM((n, n), f32)],
        compiler_params=_cparams(("parallel", "arbitrary")),
    )(s0, r, lw, k, v, a, b, dy)


def _ssd_specs(nb, nch, rev):
    def row(b, c):
        return b * nch + (nch - 1 - c if rev else c)

    l = SSM_CHUNK
    xs = pl.BlockSpec((l, 512), lambda b, c: (row(b, c), 0))
    bm = pl.BlockSpec((l, 256), lambda b, c: (row(b, c), 2))
    cm = pl.BlockSpec((l, 256), lambda b, c: (row(b, c), 3))
    dt = pl.BlockSpec((l, LANE), lambda b, c: (row(b, c), (L0_PAD - LANE) // LANE))
    st = pl.BlockSpec((1, 1, N_HEADS, SSM_STATE, HEAD_DIM), lambda b, c: (b, (nch - 1 - c if rev else c), 0, 0, 0))
    par = pl.BlockSpec((1, LANE), lambda b, c: (0, 0))
    eh = pl.BlockSpec((N_HEADS, LANE, HEAD_DIM), lambda b, c: (0, 0, 0))
    return xs, bm, cm, dt, st, par, eh, row


def ssd_fwd(xbc_act, proj0, dt_bias, a_log, d_skip, e_heads, nb, t):
    nch = t // SSM_CHUNK
    n_tok = nb * t
    xs, bm, cm, dt, st, par, eh, row = _ssd_specs(nb, nch, False)

    def body(x_ref, b_ref, c_ref, dt_ref, db_ref, al_ref, dsk_ref, e_ref, y_ref, st_ref, s):
        @pl.when(pl.program_id(1) == 0)
        def _():
            s[...] = jnp.zeros_like(s)

        st_ref[0, 0] = s[...]
        y, s_out = ssd_chunk(x_ref[...], b_ref[...], c_ref[...], dt_ref[...], [s[h] for h in range(N_HEADS)],
                             db_ref[...], al_ref[...], dsk_ref[...], [e_ref[h] for h in range(N_HEADS)])
        y_ref[...] = y
        for h in range(N_HEADS):
            s[h] = s_out[h]

    return pl.pallas_call(
        body, name="ssd_fwd", grid=(nb, nch), in_specs=[xs, bm, cm, dt, par, par, par, eh],
        out_specs=[pl.BlockSpec((SSM_CHUNK, 512), lambda b, c: (row(b, c), 0)), st],
        out_shape=[jax.ShapeDtypeStruct((n_tok, 512), f32),
                   jax.ShapeDtypeStruct((nb, nch, N_HEADS, SSM_STATE, HEAD_DIM), f32)],
        scratch_shapes=[pltpu.VMEM((N_HEADS, SSM_STATE, HEAD_DIM), f32)],
        compiler_params=_cparams(("arbitrary", "arbitrary")),
    )(xbc_act, xbc_act, xbc_act, proj0, dt_bias, a_log, d_skip, e_heads)


def ssd_bwd(xbc_act, proj0, dt_bias, a_log, d_skip, e_heads, states, dy, nb, t):
    nch = t // SSM_CHUNK
    n_tok = nb * t
    xs, bm, cm, dt, st, par, eh, row = _ssd_specs(nb, nch, True)

    def body(x_ref, b_ref, c_ref, dt_ref, db_ref, al_ref, dsk_ref, e_ref, st_ref, dy_ref,
             dx_ref, dbm_ref, dcm_ref, ddt_ref, ddb_ref, dal_ref, ddsk_ref, ds):
        first = jnp.logical_and(pl.program_id(0) == 0, pl.program_id(1) == 0)

        @pl.when(pl.program_id(1) == 0)
        def _():
            ds[...] = jnp.zeros_like(ds)

        e_list = [e_ref[h] for h in range(N_HEADS)]

        def f(x, bmv, cmv, dtr, s_in, dbv, alv, dskv):
            return ssd_chunk(x, bmv, cmv, dtr, s_in, dbv, alv, dskv, e_list)

        _, vjp = jax.vjp(f, x_ref[...], b_ref[...], c_ref[...], dt_ref[...],
                         [st_ref[0, 0, h] for h in range(N_HEADS)], db_ref[...], al_ref[...], dsk_ref[...])
        g = vjp((dy_ref[...], [ds[h] for h in range(N_HEADS)]))
        dx_ref[...], dbm_ref[...], dcm_ref[...], ddt_ref[...] = g[0], g[1], g[2], g[3]
        for h in range(N_HEADS):
            ds[h] = g[4][h]
        for ref, val in zip((ddb_ref, dal_ref, ddsk_ref), g[5:]):
            @pl.when(first)
            def _():
                ref[...] = val

            @pl.when(jnp.logical_not(first))
            def _():
                ref[...] += val

    rows_spec = lambda w: pl.BlockSpec((SSM_CHUNK, w), lambda b, c: (row(b, c), 0))
    return pl.pallas_call(
        body, name="ssd_bwd", grid=(nb, nch),
        in_specs=[xs, bm, cm, dt, par, par, par, eh, st, rows_spec(512)],
        out_specs=[rows_spec(512), rows_spec(256), rows_spec(256), rows_spec(LANE), par, par, par],
        out_shape=[jax.ShapeDtypeStruct((n_tok, 512), f32), jax.ShapeDtypeStruct((n_tok, 256), f32),
                   jax.ShapeDtypeStruct((n_tok, 256), f32), jax.ShapeDtypeStruct((n_tok, LANE), f32)]
        + [jax.ShapeDtypeStruct((1, LANE), f32)] * 3,
        scratch_shapes=[pltpu.VMEM((N_HEADS, SSM_STATE, HEAD_DIM), f32)],
        compiler_params=_cparams(("arbitrary", "arbitrary")),
    )(xbc_act, xbc_act, xbc_act, proj0, dt_bias, a_log, d_skip, e_heads, states, dy)


ATT_BQ = 256


def sb_fwd(q, k, v):
    bh, t, d = q.shape
    bq = min(ATT_BQ, t)

    def body(q_ref, k_ref, v_ref, o_ref):
        o_ref[0] = sb_block(q_ref[0], k_ref[0], v_ref[0], pl.program_id(1) * bq)

    qs = pl.BlockSpec((1, bq, d), lambda i, j: (i, j, 0))
    ks = pl.BlockSpec((1, t, d), lambda i, j: (i, 0, 0))
    return pl.pallas_call(
        body, name="sb_fwd", grid=(bh, t // bq), in_specs=[qs, ks, ks], out_specs=qs,
        out_shape=jax.ShapeDtypeStruct((bh, t, d), f32), compiler_params=_cparams(("parallel", "arbitrary")),
    )(q, k, v)


def sb_bwd(q, k, v, do):
    bh, t, d = q.shape
    bq = min(ATT_BQ, t)

    def body(q_ref, k_ref, v_ref, do_ref, dq_ref, dk_ref, dv_ref):
        q0 = pl.program_id(1) * bq
        _, vjp = jax.vjp(lambda a, b, c: sb_block(a, b, c, q0), q_ref[0], k_ref[0], v_ref[0])
        dq, dk, dv = vjp(do_ref[0])
        dq_ref[0] = dq

        @pl.when(pl.program_id(1) == 0)
        def _():
            dk_ref[0] = dk
            dv_ref[0] = dv

        @pl.when(pl.program_id(1) != 0)
        def _():
            dk_ref[0] += dk
            dv_ref[0] += dv

    qs = pl.BlockSpec((1, bq, d), lambda i, j: (i, j, 0))
    ks = pl.BlockSpec((1, t, d), lambda i, j: (i, 0, 0))
    return pl.pallas_call(
        body, name="sb_bwd", grid=(bh, t // bq), in_specs=[qs, ks, ks, qs], out_specs=[qs, ks, ks],
        out_shape=[jax.ShapeDtypeStruct((bh, t, d), f32)] * 3, compiler_params=_cparams(("parallel", "arbitrary")),
    )(q, k, v, do)


def _mla_specs(t, bq):
    qn = pl.BlockSpec((1, bq, HEAD_DIM), lambda i, j: (i, j, 0))
    qp = pl.BlockSpec((1, bq, MLA_ROPE), lambda i, j: (i, j, 0))
    kn = pl.BlockSpec((1, t, HEAD_DIM), lambda i, j: (i, 0, 0))
    kp = pl.BlockSpec((1, t, MLA_ROPE), lambda i, j: (i // N_HEADS, 0, 0))
    return qn, qp, kn, kp


def mla_fwd(qn, qp, kn, kp, v):
    bh, t, _ = qn.shape
    bq = min(ATT_BQ, t)

    def body(qn_ref, qp_ref, kn_ref, kp_ref, v_ref, o_ref):
        o_ref[0] = mla_block(qn_ref[0], qp_ref[0], kn_ref[0], kp_ref[0], v_ref[0], pl.program_id(1) * bq)

    sqn, sqp, skn, skp = _mla_specs(t, bq)
    return pl.pallas_call(
        body, name="mla_fwd", grid=(bh, t // bq), in_specs=[sqn, sqp, skn, skp, skn], out_specs=sqn,
        out_shape=jax.ShapeDtypeStruct((bh, t, HEAD_DIM), f32), compiler_params=_cparams(("parallel", "arbitrary")),
    )(qn, qp, kn, kp, v)


def mla_bwd(qn, qp, kn, kp, v, do):
    bh, t, _ = qn.shape
    bq = min(ATT_BQ, t)

    def body(qn_ref, qp_ref, kn_ref, kp_ref, v_ref, do_ref, dqn_ref, dqp_ref, dkn_ref, dkp_ref, dv_ref):
        q0 = pl.program_id(1) * bq
        _, vjp = jax.vjp(lambda a, b, c, d, e: mla_block(a, b, c, d, e, q0),
                         qn_ref[0], qp_ref[0], kn_ref[0], kp_ref[0], v_ref[0])
        dqn, dqp, dkn, dkp, dv = vjp(do_ref[0])
        dqn_ref[0] = dqn
        dqp_ref[0] = dqp
        first_q = pl.program_id(1) == 0
        first_kp = jnp.logical_and(first_q, pl.program_id(0) % N_HEADS == 0)

        @pl.when(first_q)
        def _():
            dkn_ref[0] = dkn
            dv_ref[0] = dv

        @pl.when(jnp.logical_not(first_q))
        def _():
            dkn_ref[0] += dkn
            dv_ref[0] += dv

        @pl.when(first_kp)
        def _():
            dkp_ref[0] = dkp

        @pl.when(jnp.logical_not(first_kp))
        def _():
            dkp_ref[0] += dkp

    sqn, sqp, skn, skp = _mla_specs(t, bq)
    return pl.pallas_call(
        body, name="mla_bwd", grid=(bh, t // bq), in_specs=[sqn, sqp, skn, skp, skn, sqn],
        out_specs=[sqn, sqp, skn, skp, skn],
        out_shape=[jax.ShapeDtypeStruct((bh, t, HEAD_DIM), f32), jax.ShapeDtypeStruct((bh, t, MLA_ROPE), f32),
                   jax.ShapeDtypeStruct((bh, t, HEAD_DIM), f32),
                   jax.ShapeDtypeStruct((bh // N_HEADS, t, MLA_ROPE), f32),
                   jax.ShapeDtypeStruct((bh, t, HEAD_DIM), f32)],
        compiler_params=_cparams(("arbitrary", "arbitrary")),
    )(qn, qp, kn, kp, v, do)


def loss_head(h, target):
    n, d = h.shape
    tm = _tile(n, 512)

    def body(h_ref, t_ref, l_ref, dh_ref):
        diff = h_ref[...] - t_ref[...]
        dh_ref[...] = diff * (1.0 / d)
        part = 0.5 * jnp.sum(jnp.sum(diff * diff, axis=1, keepdims=True) * (1.0 / d), axis=0, keepdims=True)

        @pl.when(pl.program_id(0) == 0)
        def _():
            l_ref[...] = jnp.zeros_like(l_ref)

        l_ref[...] += jnp.broadcast_to(part, l_ref.shape)

    spec = pl.BlockSpec((tm, d), lambda i: (i, 0))
    return pl.pallas_call(
        body, name="loss_head", grid=(n // tm,), in_specs=[spec, spec],
        out_specs=[pl.BlockSpec((8, LANE), lambda i: (0, 0)), spec],
        out_shape=[jax.ShapeDtypeStruct((8, LANE), f32), jax.ShapeDtypeStruct((n, d), f32)],
        compiler_params=_cparams(("arbitrary",)),
    )(h, target)


def _row(v):
    return v.reshape(1, -1)


def _pad_cols(a, n):
    return jnp.pad(a, ((0, 0), (0, n - a.shape[1])))


def _pad_row(v, n=LANE):
    return jnp.pad(v.reshape(1, -1), ((0, 0), (0, n - v.shape[0])))


def _heads(a, nb, t, dh):
    return a.reshape(nb, t, N_HEADS, dh).transpose(0, 2, 1, 3).reshape(nb * N_HEADS, t, dh)


def _unheads(a, nb, t, dh):
    return a.reshape(nb, N_HEADS, t, dh).transpose(0, 2, 1, 3).reshape(nb * t, N_HEADS * dh)


def _group_matrix(width, group):
    idx = np.arange(width) // group
    return jnp.asarray((idx[:, None] == idx[None, :]).astype(np.float32))


def _head_expand():
    e = np.zeros((N_HEADS, LANE, HEAD_DIM), np.float32)
    for h in range(N_HEADS):
        e[h, h, :] = 1.0
    return jnp.asarray(e)


def _rope_freqs():
    inv = 1.0 / (ROPE_THETA ** (np.arange(0, MLA_ROPE, 2, dtype=np.float32) / MLA_ROPE))
    inv = inv.astype(np.float32)
    inv8 = np.tile(inv, N_HEADS).reshape(1, LANE)
    inv1 = np.zeros((1, LANE), np.float32)
    inv1[0, :16] = inv
    return jnp.asarray(inv8), jnp.asarray(inv1)


def _w_in0_padded(w):
    return _pad_cols(w, L0_PAD)


def _w_in0_unpad(g):
    return g[:, :L0_COLS]


def _w_in1_padded(w):
    z = jnp.zeros((w.shape[0], LANE - 16), w.dtype)
    return jnp.concatenate([w[:, :1920], w[:, 1920:1936], z, w[:, 1936:1952], z], axis=1)


def _w_in1_unpad(g):
    return jnp.concatenate([g[:, :1920], g[:, 1920:1936], g[:, 2048:2064]], axis=1)


def _uq_split(w):
    w3 = w.reshape(w.shape[0], N_HEADS, 96)
    return (w3[:, :, :64].reshape(-1, 512), w3[:, :, 64:80].reshape(-1, LANE), w3[:, :, 80:96].reshape(-1, LANE))


def _uq_merge(gn, g1, g2):
    r = gn.shape[0]
    return jnp.concatenate([gn.reshape(r, N_HEADS, 64), g1.reshape(r, N_HEADS, 16), g2.reshape(r, N_HEADS, 16)],
                           axis=2).reshape(r, 768)


def local_step(x, positions, target, w):
    nb, t, d = x.shape
    n = nb * t
    tm = 256
    ni = n // tm
    tc = LANE
    h0 = x.reshape(n, d)
    tgt = target.reshape(n, d)
    pos = positions.reshape(n, 1).astype(f32)
    gh = _group_matrix(512, HEAD_DIM)
    gg = _group_matrix(512, 256)
    e_heads = _head_expand()
    inv8, inv1 = _rope_freqs()
    g = {}

    def ln_stage(h, y, gname, bname):
        ops = [_rows(h, tm), _rows(y, tm), _param(_row(w[gname])), _param(_row(w[bname]))]
        return ops, [_rows_out(n, d, tm)]

    def ffn_act_stage(u, cw, cb):
        nj = D_FF // tc
        ops = [_cols(u, t, tc, 0, D_FF), _cols(u, t, tc, nj, D_FF)] + [_cparam(cw[i:i + 1], tc) for i in range(3)] \
            + [_cparam(_row(cb), tc)]
        return ops, [Out((n, D_FF), (t, tc), lambda j, i: (i, j))], (nj, nb)

    w_in0 = _w_in0_padded(w['l0_w_in'])
    proj0 = mm(h0, w_in0, "l0_proj")

    shift_ops = [_cols(proj0, t, tc, 0, RWKV_COLS), _cparam(_row(w['rwkv_mix']), tc)]
    shift_outs = [Out((n, RWKV_COLS), (t, tc), lambda j, i: (i, j))]
    shift_grid = (RWKV_COLS // tc, nb)
    (ps,) = block_fwd(f_shift_mix, "rwkv_shift", shift_grid, shift_ops, shift_outs)

    def ps_cols(off, width):
        return Op(ps, (tm, width), lambda j, i: (i, off // width), gshape=(n, width), gimap=lambda j, i: (i, 0))

    pre_ops = [ps_cols(512, 512), ps_cols(1536, 128), ps_cols(1664, 128),
               _param(_row(w['rwkv_w0'])), _param(w['rwkv_w2']), _param(_row(w['rwkv_a0'])), _param(w['rwkv_a2']),
               _param(w['rwkv_g2']), _param(_row(w['rwkv_k_k'])), _param(_row(w['rwkv_k_a'])), _param(gh, diff=False)]
    pre_outs = [_rows_out(n, 512, tm) for _ in range(5)]
    lw, k2, na, bb, gate_r = block_fwd(f_rwkv_pre, "rwkv_pre", (1, ni), pre_ops, pre_outs)
    r_tok, v_tok = ps[:, 0:512], ps[:, 1024:1536]
    hm = lambda a_: _heads(a_, nb, t, HEAD_DIM)
    scan_in = [hm(r_tok), hm(lw), hm(k2), hm(v_tok), hm(na), hm(bb)]
    y_scan, s0_saved = rwkv_scan_fwd(*scan_in)
    y_tok = _unheads(y_scan, nb, t, HEAD_DIM)

    post_ops = [_rows(y_tok, tm), _rows(r_tok, tm), _rows(k2, tm), _rows(v_tok, tm), _rows(gate_r, tm),
                _param(_row(w['rwkv_ln_g'])), _param(_row(w['rwkv_ln_b'])), _param(w['rwkv_r_k'].reshape(1, 512)),
                _param(gh, diff=False)]
    post_outs = [_rows_out(n, 512, tm)]
    (y_a,) = block_fwd(f_rwkv_post, "rwkv_post", (1, ni), post_ops, post_outs)

    xbc_off = (RWKV_COLS + 512) // tc
    conv_ops = [_cols(proj0, t, tc, xbc_off, 1024)] + [_cparam(w['ssm_conv_w'][i:i + 1], tc) for i in range(4)] \
        + [_cparam(_row(w['ssm_conv_b']), tc)]
    conv_outs = [Out((n, 1024), (t, tc), lambda j, i: (i, j))]
    conv_grid = (1024 // tc, nb)
    (xbc_act,) = block_fwd(f_conv4_silu, "ssm_conv", conv_grid, conv_ops, conv_outs)

    dt_bias, a_log, d_skip = _pad_row(w['ssm_dt_bias']), _pad_row(w['ssm_a_log']), _pad_row(w['ssm_d'])
    y_ssd, ssd_states = ssd_fwd(xbc_act, proj0, dt_bias, a_log, d_skip, e_heads, nb, t)

    z_tok = proj0[:, RWKV_COLS:RWKV_COLS + 512]
    spost_ops = [_rows(y_ssd, tm), _rows(z_tok, tm), _param(_row(w['ssm_norm_g'])), _param(gg, diff=False)]
    spost_outs = [_rows_out(n, 512, tm)]
    (y_b,) = block_fwd(f_ssm_post, "ssm_post", (1, ni), spost_ops, spost_outs)

    y_cat0 = jnp.concatenate([y_a, y_b], axis=1)
    mixed0 = mm(y_cat0, w['l0_w_out'], "l0_out")
    ln1_ops, ln_outs = ln_stage(h0, mixed0, 'l0_ln1_g', 'l0_ln1_b')
    (h1,) = block_fwd(f_ln, "l0_ln1", (1, ni), ln1_ops, ln_outs)

    u0 = mm(h1, w['ffn0_w_up'], "ffn0_up")
    act0_ops, act_outs, act_grid = ffn_act_stage(u0, w['ffn0_conv_w'], w['ffn0_conv_b'])
    (act0,) = block_fwd(f_ffn_act, "ffn0_act", act_grid, act0_ops, act_outs)
    f0 = mm(act0, w['ffn0_w_down'], "ffn0_down")
    ln2_ops, _ = ln_stage(h1, f0, 'l0_ln2_g', 'l0_ln2_b')
    (h2,) = block_fwd(f_ln, "l0_ln2", (1, ni), ln2_ops, ln_outs)

    w_in1 = _w_in1_padded(w['l1_w_in'])
    proj1 = mm(h2, w_in1, "l1_proj")
    w_qn, w_q1, w_q2 = _uq_split(w['mla_w_uq'])

    def p1_cols(off, width):
        return Op(proj1, (tm, width), lambda j, i: (i, off // width), gshape=(n, width), gimap=lambda j, i: (i, 0))

    mpre_ops = [p1_cols(1536, 256), p1_cols(1792, 128), p1_cols(1920, 128), p1_cols(2048, 128),
                Op(pos, (tm, 1), lambda j, i: (i, 0), diff=False),
                _param(_row(w['mla_q_norm_g'])), _param(w_qn), _param(w_q1), _param(w_q2),
                _param(_row(w['mla_kv_norm_g'])), _param(w['mla_w_ukv']), _param(inv8, diff=False),
                _param(inv1, diff=False)]
    mpre_outs = [_rows_out(n, 512, tm), _rows_out(n, LANE, tm), _rows_out(n, LANE, tm), _rows_out(n, 1024, tm),
                 _rows_out(n, LANE, tm), _rows_out(n, LANE, tm)]
    q_nope, q_r1, q_r2, kv, k_r1, k_r2 = block_fwd(f_mla_pre, "mla_pre", (1, ni), mpre_ops, mpre_outs)

    sb_q, sb_k, sb_v = hm(proj1[:, 0:512]), hm(proj1[:, 512:1024]), hm(proj1[:, 1024:1536])
    o_sb = sb_fwd(sb_q, sb_k, sb_v)

    qn_h = hm(q_nope)
    qp_h = jnp.concatenate([_heads(q_r1, nb, t, 16), _heads(q_r2, nb, t, 16)], axis=2)
    kv4 = kv.reshape(nb, t, N_HEADS, 128)
    kn_h = kv4[..., :64].transpose(0, 2, 1, 3).reshape(nb * N_HEADS, t, 64)
    v_h = kv4[..., 64:].transpose(0, 2, 1, 3).reshape(nb * N_HEADS, t, 64)
    kp_b = jnp.concatenate([k_r1[:, :16], k_r2[:, :16]], axis=1).reshape(nb, t, MLA_ROPE)
    o_mla = mla_fwd(qn_h, qp_h, kn_h, kp_b, v_h)

    y_cat1 = jnp.concatenate([_unheads(o_sb, nb, t, 64), _unheads(o_mla, nb, t, 64)], axis=1)
    mixed1 = mm(y_cat1, w['l1_w_out'], "l1_out")
    ln3_ops, _ = ln_stage(h2, mixed1, 'l1_ln1_g', 'l1_ln1_b')
    (h3,) = block_fwd(f_ln, "l1_ln1", (1, ni), ln3_ops, ln_outs)
    u1 = mm(h3, w['ffn1_w_up'], "ffn1_up")
    act1_ops, _, _ = ffn_act_stage(u1, w['ffn1_conv_w'], w['ffn1_conv_b'])
    (act1,) = block_fwd(f_ffn_act, "ffn1_act", act_grid, act1_ops, act_outs)
    f1 = mm(act1, w['ffn1_w_down'], "ffn1_down")
    ln4_ops, _ = ln_stage(h3, f1, 'l1_ln2_g', 'l1_ln2_b')
    (h4,) = block_fwd(f_ln, "l1_ln2", (1, ni), ln4_ops, ln_outs)

    loss_part, dh4 = loss_head(h4, tgt)

    def ffn_bwd(tag, dh_out, ln_ops, act_ops, h_in, u, act, w_up, w_down, names):
        dh_res, df, g[names[4]], g[names[5]] = block_bwd(f_ln, tag + "_ln2_bwd", (1, ni), ln_ops, ln_outs, [dh_out])
        g[names[3]] = mm(act, df, tag + "_down_dw", ta=True)
        dact = mm(df, w_down, tag + "_down_dx", tb=True)
        dgate, dup, dw0, dw1, dw2, dcb = block_bwd(f_ffn_act, tag + "_act_bwd", act_grid, act_ops, act_outs, [dact])
        g[names[1]] = jnp.concatenate([dw0, dw1, dw2], axis=0)
        g[names[2]] = dcb.reshape(-1)
        du = jnp.concatenate([dgate, dup], axis=1)
        g[names[0]] = mm(h_in, du, tag + "_up_dw", ta=True)
        return mm(du, w_up, tag + "_up_dx", tb=True, add=dh_res)

    def vec(a_):
        return a_.reshape(-1)

    dh3 = ffn_bwd("ffn1", dh4, ln4_ops, act1_ops, h3, u1, act1, w['ffn1_w_up'], w['ffn1_w_down'],
                  ['ffn1_w_up', 'ffn1_conv_w', 'ffn1_conv_b', 'ffn1_w_down', 'l1_ln2_g', 'l1_ln2_b'])
    for nm in ('l1_ln2_g', 'l1_ln2_b'):
        g[nm] = vec(g[nm])

    dh2_res, dmixed1, g3g, g3b = block_bwd(f_ln, "l1_ln1_bwd", (1, ni), ln3_ops, ln_outs, [dh3])
    g['l1_ln1_g'], g['l1_ln1_b'] = vec(g3g), vec(g3b)
    g['l1_w_out'] = mm(y_cat1, dmixed1, "l1_out_dw", ta=True)
    dy_cat1 = mm(dmixed1, w['l1_w_out'], "l1_out_dx", tb=True)
    do_sb, do_mla = hm(dy_cat1[:, :512]), hm(dy_cat1[:, 512:])

    dqn_h, dqp_h, dkn_h, dkp_b, dv_h = mla_bwd(qn_h, qp_h, kn_h, kp_b, v_h, do_mla)
    dsb_q, dsb_k, dsb_v = sb_bwd(sb_q, sb_k, sb_v, do_sb)

    dq_nope = _unheads(dqn_h, nb, t, 64)
    dq_r1, dq_r2 = _unheads(dqp_h[..., :16], nb, t, 16), _unheads(dqp_h[..., 16:], nb, t, 16)
    dkv = jnp.concatenate([dkn_h.reshape(nb, N_HEADS, t, 64), dv_h.reshape(nb, N_HEADS, t, 64)], axis=3)
    dkv = dkv.transpose(0, 2, 1, 3).reshape(n, 1024)
    dkp = dkp_b.reshape(n, MLA_ROPE)
    dk_r1, dk_r2 = _pad_cols(dkp[:, :16], LANE), _pad_cols(dkp[:, 16:], LANE)
    (dc_q, dc_kv, dkx1, dkx2, gqg, gwqn, gwq1, gwq2, gkvg, g['mla_w_ukv']) = block_bwd(
        f_mla_pre, "mla_pre_bwd", (1, ni), mpre_ops, mpre_outs, [dq_nope, dq_r1, dq_r2, dkv, dk_r1, dk_r2])
    g['mla_q_norm_g'], g['mla_kv_norm_g'] = vec(gqg), vec(gkvg)
    g['mla_w_uq'] = _uq_merge(gwqn, gwq1, gwq2)
    dproj1 = jnp.concatenate([_unheads(dsb_q, nb, t, 64), _unheads(dsb_k, nb, t, 64), _unheads(dsb_v, nb, t, 64),
                              dc_q, dc_kv, dkx1, dkx2], axis=1)
    g['l1_w_in'] = _w_in1_unpad(mm(h2, dproj1, "l1_proj_dw", ta=True))
    dh2 = mm(dproj1, w_in1, "l1_proj_dx", tb=True, add=dh2_res)

    dh1 = ffn_bwd("ffn0", dh2, ln2_ops, act0_ops, h1, u0, act0, w['ffn0_w_up'], w['ffn0_w_down'],
                  ['ffn0_w_up', 'ffn0_conv_w', 'ffn0_conv_b', 'ffn0_w_down', 'l0_ln2_g', 'l0_ln2_b'])
    for nm in ('l0_ln2_g', 'l0_ln2_b'):
        g[nm] = vec(g[nm])

    dh0_res, dmixed0, g1g, g1b = block_bwd(f_ln, "l0_ln1_bwd", (1, ni), ln1_ops, ln_outs, [dh1])
    g['l0_ln1_g'], g['l0_ln1_b'] = vec(g1g), vec(g1b)
    g['l0_w_out'] = mm(y_cat0, dmixed0, "l0_out_dw", ta=True)
    dy_cat0 = mm(dmixed0, w['l0_w_out'], "l0_out_dx", tb=True)
    dy_a, dy_b = dy_cat0[:, :512], dy_cat0[:, 512:]

    dy_ssd, dz, gng = block_bwd(f_ssm_post, "ssm_post_bwd", (1, ni), spost_ops, spost_outs, [dy_b])
    g['ssm_norm_g'] = vec(gng)
    dxs, dbm, dcm, ddt_raw, gdb, gal, gdsk = ssd_bwd(xbc_act, proj0, dt_bias, a_log, d_skip, e_heads, ssd_states,
                                                     dy_ssd, nb, t)
    g['ssm_dt_bias'], g['ssm_a_log'], g['ssm_d'] = gdb[0, :8], gal[0, :8], gdsk[0, :8]
    dxbc_act = jnp.concatenate([dxs, dbm, dcm], axis=1)
    dxbc, cw0, cw1, cw2, cw3, gcb = block_bwd(f_conv4_silu, "ssm_conv_bwd", conv_grid, conv_ops, conv_outs, [dxbc_act])
    g['ssm_conv_w'] = jnp.concatenate([cw0, cw1, cw2, cw3], axis=0)
    g['ssm_conv_b'] = vec(gcb)

    dy_tok, dr_post, dk2_post, dv_post, dgate, glg, glb, grk = block_bwd(
        f_rwkv_post, "rwkv_post_bwd", (1, ni), post_ops, post_outs, [dy_a])
    g['rwkv_ln_g'], g['rwkv_ln_b'], g['rwkv_r_k'] = vec(glg), vec(glb), grk.reshape(N_HEADS, HEAD_DIM)
    dr_h, dlw_h, dk2_h, dv_h2, dna_h, dbb_h = rwkv_scan_bwd(s0_saved, *scan_in, hm(dy_tok))
    un = lambda a_: _unheads(a_, nb, t, HEAD_DIM)
    dk2 = un(dk2_h) + dk2_post
    (dk_pre, dwa_lo, dg_lo, gw0, g['rwkv_w2'], ga0, g['rwkv_a2'], g['rwkv_g2'], gkk, gka) = block_bwd(
        f_rwkv_pre, "rwkv_pre_bwd", (1, ni), pre_ops, pre_outs, [un(dlw_h), dk2, un(dna_h), un(dbb_h), dgate])
    g['rwkv_w0'], g['rwkv_a0'], g['rwkv_k_k'], g['rwkv_k_a'] = vec(gw0), vec(ga0), vec(gkk), vec(gka)
    dps = jnp.concatenate([un(dr_h) + dr_post, dk_pre, un(dv_h2) + dv_post, dwa_lo, dg_lo], axis=1)
    dp_rwkv, gmix = block_bwd(f_shift_mix, "rwkv_shift_bwd", shift_grid, shift_ops, shift_outs, [dps])
    g['rwkv_mix'] = vec(gmix)

    dproj0 = jnp.concatenate([dp_rwkv, dz, dxbc, ddt_raw], axis=1)
    g['l0_w_in'] = _w_in0_unpad(mm(h0, dproj0, "l0_proj_dw", ta=True))
    grad_x = mm(dproj0, w_in0, "l0_proj_dx", tb=True, add=dh0_res)
    return loss_part, grad_x.reshape(nb, t, d), g


MESH = pl.DeviceIdType.MESH
ANY = pl.BlockSpec(memory_space=pl.ANY)
AXES = ("x", "y", "c")


def _place():
    x, y, c = lax.axis_index("x"), lax.axis_index("y"), lax.axis_index("c")
    chips = [(1 - x, y), (x, 1 - y), (1 - x, 1 - y)]
    return x, y, c, chips


def gather_weights(big, small):
    def body(big_ref, small_ref, obig, osmall, send_sems, recv_sems, local_sems):
        x, y, c, chips = _place()
        me = 2 * x + y
        mine = [pltpu.make_async_copy(big_ref, obig.at[me], local_sems.at[0]),
                pltpu.make_async_copy(small_ref, osmall.at[me], local_sems.at[1])]
        for cp in mine:
            cp.start()
        sends = []
        for j, (cx, cy) in enumerate(chips):
            for k, (src, dst) in enumerate(((big_ref, obig), (small_ref, osmall))):
                sends.append(pltpu.make_async_remote_copy(
                    src_ref=src, dst_ref=dst.at[me], send_sem=send_sems.at[2 * j + k], recv_sem=recv_sems.at[2 * j + k],
                    device_id=(cx, cy, c), device_id_type=MESH))
        for cp in sends:
            cp.start()
        for j, (cx, cy) in enumerate(chips):
            for k, (src, dst) in enumerate(((big_ref, obig), (small_ref, osmall))):
                pltpu.make_async_remote_copy(
                    src_ref=src, dst_ref=dst.at[2 * cx + cy], send_sem=send_sems.at[2 * j + k],
                    recv_sem=recv_sems.at[2 * j + k], device_id=(cx, cy, c), device_id_type=MESH).wait_recv()
        for cp in sends:
            cp.wait_send()
        for cp in mine:
            cp.wait()

    return pl.pallas_call(
        body, name="gather_weights", in_specs=[ANY, ANY], out_specs=[ANY, ANY],
        out_shape=[jax.ShapeDtypeStruct((N_SHARD,) + big.shape, big.dtype),
                   jax.ShapeDtypeStruct((N_SHARD,) + small.shape, small.dtype)],
        scratch_shapes=[pltpu.SemaphoreType.DMA((6,)), pltpu.SemaphoreType.DMA((6,)), pltpu.SemaphoreType.DMA((2,))],
    )(big, small)


def swap_with_sibling(a):
    def body(a_ref, o_ref, send_sem, recv_sem):
        x, y, c, _ = _place()
        cp = pltpu.make_async_remote_copy(src_ref=a_ref, dst_ref=o_ref, send_sem=send_sem, recv_sem=recv_sem,
                                          device_id=(x, y, 1 - c), device_id_type=MESH)
        cp.start()
        cp.wait()

    return pl.pallas_call(
        body, name="swap_with_sibling", in_specs=[ANY], out_specs=ANY,
        out_shape=jax.ShapeDtypeStruct(a.shape, a.dtype),
        scratch_shapes=[pltpu.SemaphoreType.DMA, pltpu.SemaphoreType.DMA],
    )(a)


def scatter_to_chips(p):
    def body(p_ref, o_ref, send_sems, recv_sems, local_sem):
        x, y, c, chips = _place()
        me = 2 * x + y
        mine = pltpu.make_async_copy(p_ref.at[me], o_ref.at[me], local_sem)
        mine.start()
        sends = [pltpu.make_async_remote_copy(
            src_ref=p_ref.at[2 * cx + cy], dst_ref=o_ref.at[me], send_sem=send_sems.at[j], recv_sem=recv_sems.at[j],
            device_id=(cx, cy, c), device_id_type=MESH) for j, (cx, cy) in enumerate(chips)]
        for cp in sends:
            cp.start()
        for j, (cx, cy) in enumerate(chips):
            pltpu.make_async_remote_copy(
                src_ref=p_ref.at[me], dst_ref=o_ref.at[2 * cx + cy], send_sem=send_sems.at[j], recv_sem=recv_sems.at[j],
                device_id=(cx, cy, c), device_id_type=MESH).wait_recv()
        for cp in sends:
            cp.wait_send()
        mine.wait()

    return pl.pallas_call(
        body, name="scatter_to_chips", in_specs=[ANY], out_specs=ANY,
        out_shape=jax.ShapeDtypeStruct(p.shape, p.dtype),
        scratch_shapes=[pltpu.SemaphoreType.DMA((3,)), pltpu.SemaphoreType.DMA((3,)), pltpu.SemaphoreType.DMA],
    )(p)


def _ew_tile(rows):
    return _tile(rows, 2048)


def add2(a, b):
    shp = a.shape
    a2, b2 = a.reshape(-1, LANE), b.reshape(-1, LANE)
    tr = _ew_tile(a2.shape[0])

    def body(a_ref, b_ref, o_ref):
        o_ref[...] = a_ref[...] + b_ref[...]

    spec = pl.BlockSpec((tr, LANE), lambda i: (i, 0))
    return pl.pallas_call(body, name="pair_add", grid=(a2.shape[0] // tr,), in_specs=[spec, spec], out_specs=spec,
                          out_shape=jax.ShapeDtypeStruct(a2.shape, f32),
                          compiler_params=_cparams(("parallel",)))(a2, b2).reshape(shp)


def add4(p):
    h = p.shape[1]
    tr = _ew_tile(h)

    def body(p_ref, o_ref):
        o_ref[...] = ((p_ref[0] + p_ref[1]) + p_ref[2]) + p_ref[3]

    return pl.pallas_call(body, name="chip_add", grid=(h // tr,),
                          in_specs=[pl.BlockSpec((N_SHARD, tr, LANE), lambda i: (0, i, 0))],
                          out_specs=pl.BlockSpec((tr, LANE), lambda i: (i, 0)),
                          out_shape=jax.ShapeDtypeStruct((h, LANE), f32),
                          compiler_params=_cparams(("parallel",)))(p)


def adamw(w, g, m, v):
    rows = w.shape[0]
    tr = _ew_tile(rows)

    def body(w_ref, g_ref, m_ref, v_ref, d_ref, nm_ref, nv_ref):
        gv = g_ref[...]
        m_new = ADAM_B1 * m_ref[...] + (1.0 - ADAM_B1) * gv
        v_new = ADAM_B2 * v_ref[...] + (1.0 - ADAM_B2) * jnp.square(gv)
        m_hat = m_new / (1.0 - ADAM_B1 ** ADAM_STEP)
        v_hat = v_new / (1.0 - ADAM_B2 ** ADAM_STEP)
        d_ref[...] = -ADAM_LR * (m_hat / (jnp.sqrt(v_hat) + ADAM_EPS) + ADAM_WD * w_ref[...])
        nm_ref[...] = m_new
        nv_ref[...] = v_new

    spec = pl.BlockSpec((tr, LANE), lambda i: (i, 0))
    return pl.pallas_call(body, name="adamw", grid=(rows // tr,), in_specs=[spec] * 4, out_specs=[spec] * 3,
                          out_shape=[jax.ShapeDtypeStruct(w.shape, f32)] * 3,
                          compiler_params=_cparams(("parallel",)))(w, g, m, v)


def _pack_flat(parts, multiple):
    flat = jnp.concatenate([p.reshape(-1) for p in parts])
    pad = (-flat.shape[0]) % multiple
    return jnp.pad(flat, (0, pad)).reshape(-1, LANE)


def _unpack_flat(buf, shapes):
    flat = buf.reshape(-1)
    out, off = [], 0
    for s in shapes:
        cnt = int(np.prod(s))
        out.append(flat[off:off + cnt].reshape(s))
        off += cnt
    return out


def _full_from_shards(name, shards):
    return jnp.concatenate(shards, axis=1 if name in COL_SHARDED else 0)


def _pieces(name, grad):
    if name in COL_SHARDED:
        r, cdim = grad.shape
        return grad.reshape(r, N_SHARD, cdim // N_SHARD).transpose(1, 0, 2).reshape(N_SHARD, -1)
    if name in ROW_SHARDED:
        return grad.reshape(N_SHARD, -1)
    return jnp.broadcast_to(grad.reshape(1, -1), (N_SHARD, grad.size))


GRAD_ROWS_MULTIPLE = 4096 * LANE


def kernel(x, positions, l0_w_in, rwkv_mix, rwkv_w0, rwkv_w2, rwkv_a0, rwkv_a2, rwkv_g2, rwkv_k_k, rwkv_k_a, rwkv_r_k, rwkv_ln_g, rwkv_ln_b, ssm_conv_w, ssm_conv_b, ssm_dt_bias, ssm_a_log, ssm_d, ssm_norm_g, l0_w_out, l0_ln1_g, l0_ln1_b, ffn0_w_up, ffn0_conv_w, ffn0_conv_b, ffn0_w_down, l0_ln2_g, l0_ln2_b, l1_w_in, mla_q_norm_g, mla_w_uq, mla_kv_norm_g, mla_w_ukv, l1_w_out, l1_ln1_g, l1_ln1_b, ffn1_w_up, ffn1_conv_w, ffn1_conv_b, ffn1_w_down, l1_ln2_g, l1_ln2_b, loss_target, m_l0_w_in, m_rwkv_mix, m_rwkv_w0, m_rwkv_w2, m_rwkv_a0, m_rwkv_a2, m_rwkv_g2, m_rwkv_k_k, m_rwkv_k_a, m_rwkv_r_k, m_rwkv_ln_g, m_rwkv_ln_b, m_ssm_conv_w, m_ssm_conv_b, m_ssm_dt_bias, m_ssm_a_log, m_ssm_d, m_ssm_norm_g, m_l0_w_out, m_l0_ln1_g, m_l0_ln1_b, m_ffn0_w_up, m_ffn0_conv_w, m_ffn0_conv_b, m_ffn0_w_down, m_l0_ln2_g, m_l0_ln2_b, m_l1_w_in, m_mla_q_norm_g, m_mla_w_uq, m_mla_kv_norm_g, m_mla_w_ukv, m_l1_w_out, m_l1_ln1_g, m_l1_ln1_b, m_ffn1_w_up, m_ffn1_conv_w, m_ffn1_conv_b, m_ffn1_w_down, m_l1_ln2_g, m_l1_ln2_b, v_l0_w_in, v_rwkv_mix, v_rwkv_w0, v_rwkv_w2, v_rwkv_a0, v_rwkv_a2, v_rwkv_g2, v_rwkv_k_k, v_rwkv_k_a, v_rwkv_r_k, v_rwkv_ln_g, v_rwkv_ln_b, v_ssm_conv_w, v_ssm_conv_b, v_ssm_dt_bias, v_ssm_a_log, v_ssm_d, v_ssm_norm_g, v_l0_w_out, v_l0_ln1_g, v_l0_ln1_b, v_ffn0_w_up, v_ffn0_conv_w, v_ffn0_conv_b, v_ffn0_w_down, v_l0_ln2_g, v_l0_ln2_b, v_l1_w_in, v_mla_q_norm_g, v_mla_w_uq, v_mla_kv_norm_g, v_mla_w_ukv, v_l1_w_out, v_l1_ln1_g, v_l1_ln1_b, v_ffn1_w_up, v_ffn1_conv_w, v_ffn1_conv_b, v_ffn1_w_down, v_l1_ln2_g, v_l1_ln2_b):
    args = locals()
    w_loc = {n: args[n] for n in WEIGHTS}
    m_loc = {n: args["m_" + n] for n in WEIGHTS}
    v_loc = {n: args["v_" + n] for n in WEIGHTS}
    c = lax.axis_index("c")

    small_names = [n for n in COL_SHARDED if n not in BIG]
    big_buf = _pack_flat([w_loc[n].astype(bf16) for n in BIG], 16 * LANE)
    small_buf = _pack_flat([w_loc[n] for n in small_names], 8 * LANE)
    big_all, small_all = gather_weights(big_buf, small_buf)
    w_full = dict(w_loc)
    for names, buf in ((BIG, big_all), (small_names, small_all)):
        per_shard = [_unpack_flat(buf[s], [w_loc[n].shape for n in names]) for s in range(N_SHARD)]
        for k, n in enumerate(names):
            w_full[n] = _full_from_shards(n, [per_shard[s][k] for s in range(N_SHARD)])

    loss_part, grad_x, g_full = local_step(x, positions, loss_target, w_full)
    loss = lax.psum(loss_part[0, 0], AXES)

    pieces = jnp.concatenate([_pieces(n, g_full[n]) for n in WEIGHTS], axis=1)
    pad = (-pieces.shape[1]) % GRAD_ROWS_MULTIPLE
    pieces = jnp.pad(pieces, ((0, 0), (0, pad))).reshape(N_SHARD, -1, LANE)
    half = pieces.shape[1] // 2
    keep = lax.dynamic_slice_in_dim(pieces, c * half, half, axis=1)
    give = lax.dynamic_slice_in_dim(pieces, (1 - c) * half, half, axis=1)
    pair = add2(keep, swap_with_sibling(give))
    mine = add4(scatter_to_chips(pair))
    theirs = swap_with_sibling(mine)
    lo = jnp.where(c == 0, mine, theirs)
    hi = jnp.where(c == 0, theirs, mine)
    g_buf = jnp.concatenate([lo, hi], axis=0)

    shapes = [w_loc[n].shape for n in WEIGHTS]
    rows_mult = GRAD_ROWS_MULTIPLE
    w_buf = _pack_flat([w_loc[n] for n in WEIGHTS], rows_mult)
    m_buf = _pack_flat([m_loc[n] for n in WEIGHTS], rows_mult)
    v_buf = _pack_flat([v_loc[n] for n in WEIGHTS], rows_mult)
    d_buf, nm_buf, nv_buf = adamw(w_buf, g_buf, m_buf, v_buf)
    grads = _unpack_flat(g_buf, shapes)
    deltas = _unpack_flat(d_buf, shapes)
    new_m = _unpack_flat(nm_buf, shapes)
    new_v = _unpack_flat(nv_buf, shapes)
    return (loss, grad_x, *grads, *deltas, *new_m, *new_v)
```

```python
import functools

import numpy as np
import jax
import jax.numpy as jnp
from jax import lax
from jax.experimental import pallas as pl
from jax.experimental.pallas import tpu as pltpu

f32 = jnp.float32
bf16 = jnp.bfloat16
HI = lax.Precision.HIGHEST
MID = lax.Precision.HIGH

D_MODEL = 1024
HEAD_DIM = 64
N_HEADS = 8
RWKV_COLS = 1792
RWKV_GN_EPS = 64e-5
SSM_STATE = 128
SSM_CHUNK = 128
L0_COLS = 3336
L0_PAD = 3456
L1_COLS = 1952
L1_PAD = 2048
MLA_ROPE = 32
ROPE_THETA = 10000.0
D_FF = 2816
DEPTH = 2
ALPHA = (2 * DEPTH) ** 0.25
ADAM_LR = 0.001
ADAM_B1 = 0.9
ADAM_B2 = 0.999
ADAM_EPS = 1e-08
ADAM_WD = 0.01
ADAM_STEP = 10
RWKV_CHUNK = 64
RWKV_HEADS_PER_STEP = 4
LANE = 128
SUBLANE = 8
VMEM_LIMIT = 56 * 1024 * 1024

WEIGHTS = ['l0_w_in', 'rwkv_mix', 'rwkv_w0', 'rwkv_w2', 'rwkv_a0', 'rwkv_a2', 'rwkv_g2', 'rwkv_k_k', 'rwkv_k_a',
           'rwkv_r_k', 'rwkv_ln_g', 'rwkv_ln_b', 'ssm_conv_w', 'ssm_conv_b', 'ssm_dt_bias', 'ssm_a_log', 'ssm_d',
           'ssm_norm_g', 'l0_w_out', 'l0_ln1_g', 'l0_ln1_b', 'ffn0_w_up', 'ffn0_conv_w', 'ffn0_conv_b',
           'ffn0_w_down', 'l0_ln2_g', 'l0_ln2_b', 'l1_w_in', 'mla_q_norm_g', 'mla_w_uq', 'mla_kv_norm_g',
           'mla_w_ukv', 'l1_w_out', 'l1_ln1_g', 'l1_ln1_b', 'ffn1_w_up', 'ffn1_conv_w', 'ffn1_conv_b',
           'ffn1_w_down', 'l1_ln2_g', 'l1_ln2_b']
COL_SHARDED = ['l0_w_in', 'rwkv_w2', 'rwkv_a2', 'rwkv_g2', 'ssm_conv_w', 'ffn0_w_up', 'ffn0_conv_w', 'l1_w_in',
               'mla_w_uq', 'mla_w_ukv', 'ffn1_w_up', 'ffn1_conv_w']
ROW_SHARDED = ['l0_w_out', 'ffn0_w_down', 'l1_w_out', 'ffn1_w_down']
BIG = ['l0_w_in', 'l0_w_out', 'ffn0_w_up', 'ffn0_w_down', 'l1_w_in', 'l1_w_out', 'ffn1_w_up', 'ffn1_w_down']
SMALL = [n for n in WEIGHTS if n not in BIG]
N_SHARD = 4


def _cparams(sem):
    return pltpu.CompilerParams(dimension_semantics=sem, vmem_limit_bytes=VMEM_LIMIT)


def _dg(a, b, ca, cb, prec=None):
    return lax.dot_general(a, b, (((ca,), (cb,)), ((), ())), precision=prec, preferred_element_type=f32)


def hdot(a, b):
    return _dg(a, b, 1, 0, HI)


def mdot(a, b):
    return _dg(a, b, 1, 0, MID)


def mdot_nt(a, b):
    return _dg(a, b, 1, 1, MID)


def mdot_tn(a, b):
    return _dg(a, b, 0, 0, MID)


def _b(x):
    return x.astype(bf16)


@jax.custom_vjp
def bdot(x, w):
    return _dg(_b(x), _b(w), 1, 0)


def _bdot_fwd(x, w):
    return bdot(x, w), (x, w)


def _bdot_bwd(res, g):
    x, w = res
    return _dg(_b(g), _b(w), 1, 1).astype(x.dtype), _dg(_b(x), _b(g), 0, 0).astype(w.dtype)


bdot.defvjp(_bdot_fwd, _bdot_bwd)


@jax.custom_vjp
def bdot_nt(x, y):
    return _dg(_b(x), _b(y), 1, 1)


def _bdot_nt_fwd(x, y):
    return bdot_nt(x, y), (x, y)


def _bdot_nt_bwd(res, g):
    x, y = res
    return _dg(_b(g), _b(y), 1, 0), _dg(_b(g), _b(x), 0, 0)


bdot_nt.defvjp(_bdot_nt_fwd, _bdot_nt_bwd)


@jax.custom_vjp
def bdot_tn(x, y):
    return _dg(_b(x), _b(y), 0, 0)


def _bdot_tn_fwd(x, y):
    return bdot_tn(x, y), (x, y)


def _bdot_tn_bwd(res, g):
    x, y = res
    return _dg(_b(y), _b(g), 1, 1), _dg(_b(x), _b(g), 1, 0)


bdot_tn.defvjp(_bdot_tn_fwd, _bdot_tn_bwd)


def _sigmoid(x):
    return 1.0 / (1.0 + jnp.exp(-x))


@jax.custom_vjp
def softplus(x):
    e = jnp.exp(-jnp.abs(x))
    u = 1.0 + e
    log1p = jnp.where(u == 1.0, e, jnp.log(u) * e / jnp.where(u == 1.0, 1.0, u - 1.0))
    return jnp.maximum(x, 0.0) + log1p


def _softplus_fwd(x):
    return softplus(x), x


def _softplus_bwd(x, g):
    return (g * _sigmoid(x),)


softplus.defvjp(_softplus_fwd, _softplus_bwd)


def silu(x):
    return x * _sigmoid(x)


def _shift_rows(x, k, up):
    if k == 0:
        return x
    t = x.shape[0]
    rows = lax.broadcasted_iota(jnp.int32, x.shape, 0)
    if up:
        return jnp.where(rows < t - k, pltpu.roll(x, t - k, 0), 0.0)
    return jnp.where(rows >= k, pltpu.roll(x, k, 0), 0.0)


@functools.partial(jax.custom_vjp, nondiff_argnums=(1,))
def shift_down(x, k):
    return _shift_rows(x, k, False)


def _shift_down_fwd(x, k):
    return _shift_rows(x, k, False), None


def _shift_down_bwd(k, _, g):
    return (_shift_rows(g, k, True),)


shift_down.defvjp(_shift_down_fwd, _shift_down_bwd)


@functools.partial(jax.custom_vjp, nondiff_argnums=(1,))
def lane_roll(x, s):
    return pltpu.roll(x, s % x.shape[1], 1)


def _lane_roll_fwd(x, s):
    return lane_roll(x, s), None


def _lane_roll_bwd(s, _, g):
    return (pltpu.roll(g, (-s) % g.shape[1], 1),)


lane_roll.defvjp(_lane_roll_fwd, _lane_roll_bwd)


def rot_half32(x):
    first = (lax.broadcasted_iota(jnp.int32, x.shape, 1) % MLA_ROPE) < (MLA_ROPE // 2)
    return jnp.where(first, -lane_roll(x, -(MLA_ROPE // 2)), lane_roll(x, MLA_ROPE // 2))


def _iota2(shape, axis):
    return lax.broadcasted_iota(jnp.int32, shape, axis)


class Op:
    def __init__(self, arr, block, imap, diff=True, acc=None, gshape=None, gimap=None):
        self.arr, self.block, self.imap, self.diff, self.acc = arr, tuple(block), imap, diff, acc
        self.gshape = tuple(arr.shape) if gshape is None else tuple(gshape)
        self.gimap = imap if gimap is None else gimap


class Out:
    def __init__(self, shape, block, imap, dtype=f32):
        self.shape, self.block, self.imap, self.dtype = tuple(shape), tuple(block), imap, dtype


def block_fwd(fn, name, grid, ops, outs):
    n_in = len(ops)

    def body(*refs):
        vals = [r[...] for r in refs[:n_in]]
        res = fn(*vals)
        for r, v in zip(refs[n_in:], res):
            r[...] = v.astype(r.dtype)

    res = pl.pallas_call(
        body, name=name, grid=grid,
        in_specs=[pl.BlockSpec(o.block, o.imap) for o in ops],
        out_specs=[pl.BlockSpec(o.block, o.imap) for o in outs],
        out_shape=[jax.ShapeDtypeStruct(o.shape, o.dtype) for o in outs],
        compiler_params=_cparams(("arbitrary", "arbitrary")),
    )(*[o.arr for o in ops])
    return tuple(res)


def block_bwd(fn, name, grid, ops, outs, douts):
    n_in, n_out = len(ops), len(outs)
    dix = [k for k, o in enumerate(ops) if o.diff]

    def body(*refs):
        vals = [r[...] for r in refs[:n_in]]
        dvals = tuple(r[...] for r in refs[n_in:n_in + n_out])
        grefs = refs[n_in + n_out:]

        def f(*d):
            full = list(vals)
            for k, v in zip(dix, d):
                full[k] = v
            return tuple(fn(*full))

        _, vjp = jax.vjp(f, *[vals[k] for k in dix])
        grads = vjp(dvals)
        j, i = pl.program_id(0), pl.program_id(1)
        for k, gref, g in zip(dix, grefs, grads):
            acc = ops[k].acc
            if acc is None:
                gref[...] = g.astype(gref.dtype)
            else:
                first = (i == 0) if acc == 'i' else jnp.logical_and(i == 0, j == 0)

                @pl.when(first)
                def _():
                    gref[...] = g

                @pl.when(jnp.logical_not(first))
                def _():
                    gref[...] += g

    gspecs = [pl.BlockSpec(ops[k].block, ops[k].gimap) for k in dix]
    gshapes = [jax.ShapeDtypeStruct(ops[k].gshape, f32) for k in dix]
    res = pl.pallas_call(
        body, name=name, grid=grid,
        in_specs=[pl.BlockSpec(o.block, o.imap) for o in ops] + [pl.BlockSpec(o.block, o.imap) for o in outs],
        out_specs=gspecs, out_shape=gshapes,
        compiler_params=_cparams(("arbitrary", "arbitrary")),
    )(*[o.arr for o in ops], *douts)
    return tuple(res)


def _rows(arr, tm, diff=True):
    return Op(arr, (tm, arr.shape[1]), lambda j, i: (i, 0), diff=diff)


def _param(arr, diff=True):
    return Op(arr, arr.shape, lambda j, i: (0,) * arr.ndim, diff=diff, acc='ij')


def _rows_out(n, c, tm):
    return Out((n, c), (tm, c), lambda j, i: (i, 0))


def _cols(arr, t, tc, off=0, width=None):
    width = arr.shape[1] if width is None else width
    return Op(arr, (t, tc), lambda j, i: (i, j + off), gshape=(arr.shape[0], width), gimap=lambda j, i: (i, j))


def _cparam(arr, tc):
    return Op(arr, (arr.shape[0], tc), lambda j, i: (0, j), acc='i')


def _colblock(arr, tm, off, width):
    return Op(arr, (tm, width), lambda j, i: (i, off // width), gshape=(arr.shape[0], width),
              gimap=lambda j, i: (i, 0))


def _tile(n, cap):
    best = None
    for t in range(LANE, min(n, cap) + 1, LANE):
        if n % t == 0:
            best = t
    return n if best is None else best


def _rtile(rows, cols, cap_bytes=2 * 1024 * 1024):
    best = None
    for t in range(SUBLANE, rows + 1, SUBLANE):
        if rows % t == 0 and t * cols * 4 <= cap_bytes:
            best = t
    return rows if best is None else best


def mm(a, b, name, ta=False, add=None):
    m = a.shape[1] if ta else a.shape[0]
    kd = a.shape[0] if ta else a.shape[1]
    n = b.shape[1]
    tm, tn, tk = _tile(m, 512), _tile(n, 1024), _tile(kd, 1024)
    nk = kd // tk
    ca = 0 if ta else 1

    def body(*refs):
        if add is None:
            a_ref, b_ref, o_ref, acc = refs
        else:
            a_ref, b_ref, add_ref, o_ref, acc = refs
        k = pl.program_id(2)

        @pl.when(k == 0)
        def _():
            acc[...] = jnp.zeros_like(acc)

        acc[...] += _dg(_b(a_ref[...]), _b(b_ref[...]), ca, 0)

        @pl.when(k == nk - 1)
        def _():
            o_ref[...] = acc[...] if add is None else acc[...] + add_ref[...]

    a_spec = pl.BlockSpec((tk, tm), lambda i, j, k: (k, i)) if ta else pl.BlockSpec((tm, tk), lambda i, j, k: (i, k))
    b_spec = pl.BlockSpec((tk, tn), lambda i, j, k: (k, j))
    o_spec = pl.BlockSpec((tm, tn), lambda i, j, k: (i, j))
    args, specs = [a, b], [a_spec, b_spec]
    if add is not None:
        args.append(add)
        specs.append(o_spec)
    return pl.pallas_call(
        body, name=name, grid=(m // tm, n // tn, nk), in_specs=specs, out_specs=o_spec,
        out_shape=jax.ShapeDtypeStruct((m, n), f32), scratch_shapes=[pltpu.VMEM((tm, tn), f32)],
        compiler_params=_cparams(("parallel", "parallel", "arbitrary")),
    )(*args)


def f_ln(h, y, g, b):
    x = ALPHA * h + y
    mu = jnp.mean(x, axis=-1, keepdims=True)
    xc = x - mu
    var = jnp.mean(xc * xc, axis=-1, keepdims=True)
    return (xc * lax.rsqrt(var + 1e-5) * g + b,)


def f_shift_mix(p, mix):
    return (p + (shift_down(p, 1) - p) * mix,)


def f_rwkv_pre(k, wa_lo, g_lo, w0, w2, a0, a2, g2, k_k, k_a, gh):
    w_lo, a_lo = wa_lo[:, :64], wa_lo[:, 64:]
    log_w = -softplus(-(w0 + bdot(jnp.tanh(w_lo), w2))) - 0.5
    lw = -jnp.exp(log_w)
    a = _sigmoid(a0 + bdot(a_lo, a2))
    g = bdot(_sigmoid(g_lo), g2)
    kk = k * k_k
    kk = kk / jnp.maximum(jnp.sqrt(hdot(kk * kk, gh)), 1e-12)
    k2 = k * (1.0 + (a - 1.0) * k_a)
    return lw, k2, -kk, kk * a, g


def f_rwkv_post(y, r, k2, v, g, ln_g, ln_b, r_k, gh):
    mu = hdot(y, gh) * (1.0 / HEAD_DIM)
    yc = y - mu
    var = hdot(yc * yc, gh) * (1.0 / HEAD_DIM)
    yn = yc * lax.rsqrt(var + RWKV_GN_EPS) * ln_g + ln_b
    bonus = hdot(r * k2 * r_k, gh) * v
    return ((yn + bonus) * g,)


def f_conv4_silu(x, w0, w1, w2, w3, b):
    y = b + shift_down(x, 3) * w0 + shift_down(x, 2) * w1 + shift_down(x, 1) * w2 + x * w3
    return (silu(y),)


def f_ssm_post(y, z, norm_g, gg):
    u = y * silu(z)
    ms = hdot(u * u, gg) * (1.0 / 256.0)
    return (u * lax.rsqrt(ms + 1e-5) * norm_g,)


def f_ffn_act(gate, up, w0, w1, w2, b):
    gc = b + shift_down(gate, 2) * w0 + shift_down(gate, 1) * w1 + gate * w2
    return (silu(gc) * up,)


def _rms(x, g, eps=1e-6):
    return x * lax.rsqrt(jnp.mean(x * x, axis=-1, keepdims=True) + eps) * g


def f_mla_pre(c_q, c_kv, kpe, pos, q_g, w_qn, w_qr, kv_g, w_ukv, inv_q, inv_k):
    qn_in = _rms(c_q, q_g)
    q_nope = bdot(qn_in, w_qn)
    qr = bdot(qn_in, w_qr)
    kv = bdot(_rms(c_kv, kv_g), w_ukv)
    ang_q = pos * inv_q
    ang_k = pos * inv_k
    return (q_nope, qr * jnp.cos(ang_q) + rot_half32(qr) * jnp.sin(ang_q), kv,
            kpe * jnp.cos(ang_k) + rot_half32(kpe) * jnp.sin(ang_k))


def rwkv_chunk(s0, r, lw, k, v, a, b):
    l = r.shape[0]
    ri, ci = _iota2((l, l), 0), _iota2((l, l), 1)
    strict, incl = ri > ci, ri >= ci
    c = hdot(incl.astype(f32), lw)
    wt, wp, wi = jnp.exp(c), jnp.exp(c - lw), jnp.exp(-c)
    at, bt, kt, rt = a * wp, b * wi, k * wi, r * wt
    nab = jnp.where(strict, mdot_nt(at, bt), 0.0)
    nak = jnp.where(strict, mdot_nt(at, kt), 0.0)
    g = mdot_nt(at, s0) + mdot(nak, v)
    x = (ri == ci).astype(f32) + nab
    p = nab
    for _ in range(max(1, (l - 1).bit_length()) - 1):
        p = mdot(p, p)
        x = x + mdot(x, p)
    u = mdot(x, g)
    y = (mdot_nt(rt, s0) + mdot(jnp.where(incl, mdot_nt(rt, bt), 0.0), u)
         + mdot(jnp.where(incl, mdot_nt(rt, kt), 0.0), v))
    last = (_iota2((l, 1), 0) == l - 1).astype(f32)
    w_end = jnp.exp(jnp.sum(c * last, axis=0, keepdims=True))
    s1 = (s0 + mdot_tn(u, bt) + mdot_tn(v, kt)) * w_end
    return y, s1


def ssd_chunk(xs, bm, cm, dt_raw, s_in, dt_bias, a_log, d_skip, e_heads):
    l = xs.shape[0]
    ri, ci = _iota2((l, l), 0), _iota2((l, l), 1)
    incl = ri >= ci
    tri = incl.astype(f32)
    dt = softplus(dt_raw + dt_bias)
    a128 = dt * (-jnp.exp(a_log))
    lane0 = (_iota2((1, HEAD_DIM), 1) == 0).astype(f32)
    last = (_iota2((l, 1), 0) == l - 1).astype(f32)
    cb = [bdot_nt(cm[:, g * SSM_STATE:(g + 1) * SSM_STATE], bm[:, g * SSM_STATE:(g + 1) * SSM_STATE]) for g in range(2)]
    ys, s_out = [], []
    for h in range(N_HEADS):
        g = h // 4
        e = e_heads[h]
        x_h = xs[:, h * HEAD_DIM:(h + 1) * HEAD_DIM]
        dt_h = hdot(dt, e)
        ac = hdot(tri, hdot(a128, e))
        xd = x_h * dt_h
        col = jnp.broadcast_to(jnp.sum(ac * lane0, axis=1, keepdims=True), (l, l))
        decay = jnp.exp(jnp.where(incl, col - col.T, -1e30))
        y_diag = bdot(cb[g] * decay, xd)
        a_tot = jnp.sum(ac * last, axis=0, keepdims=True)
        b_g = bm[:, g * SSM_STATE:(g + 1) * SSM_STATE]
        c_g = cm[:, g * SSM_STATE:(g + 1) * SSM_STATE]
        s_new = jnp.exp(a_tot) * s_in[h] + bdot_tn(b_g, xd * jnp.exp(a_tot - ac))
        y_off = jnp.exp(ac) * bdot(c_g, s_in[h])
        ys.append(y_diag + y_off + x_h * hdot(jnp.broadcast_to(d_skip, (l, LANE)), e))
        s_out.append(s_new)
    return jnp.concatenate(ys, axis=1), s_out


def sb_block(q, k, v, q0):
    bq, t = q.shape[0], k.shape[0]
    scale = HEAD_DIM ** -0.5
    kb = LANE
    rows = q0 + _iota2((bq, kb), 0)
    upper = (_iota2((kb, kb), 0) > _iota2((kb, kb), 1)).astype(f32)
    run = jnp.zeros((bq, 1), f32)
    o = jnp.zeros((bq, HEAD_DIM), f32)
    for j in reversed(range(t // kb)):
        z = bdot_nt(q, k[j * kb:(j + 1) * kb]) * scale
        strict = (j * kb + _iota2((bq, kb), 1)) < rows
        lk = jnp.where(strict, -softplus(z), 0.0)
        log_att = z + lk + mdot(lk, upper) + run
        att = jnp.where(strict, jnp.exp(jnp.where(strict, log_att, 0.0)), 0.0)
        o = o + bdot(att, v[j * kb:(j + 1) * kb])
        run = run + jnp.sum(lk, axis=1, keepdims=True)
    return o


def mla_block(qn, qp, kn, kp, v, q0):
    bq, t = qn.shape[0], kn.shape[0]
    scale = (HEAD_DIM + MLA_ROPE) ** -0.5
    s = (bdot_nt(qn, kn) + bdot_nt(qp, kp)) * scale
    causal = _iota2((bq, t), 1) <= q0 + _iota2((bq, t), 0)
    s = jnp.where(causal, s, -1e30)
    m = jnp.max(s, axis=-1, keepdims=True)
    p = jnp.where(causal, jnp.exp(s - m), 0.0)
    p = p / jnp.sum(p, axis=-1, keepdims=True)
    return bdot(p, v)


def _head(h):
    return slice(h * HEAD_DIM, (h + 1) * HEAD_DIM)


def _rwkv_specs(nc, rev):
    hp = RWKV_HEADS_PER_STEP
    w = hp * HEAD_DIM
    chunk = (lambda c: nc - 1 - c) if rev else (lambda c: c)
    tok = lambda off: pl.BlockSpec((RWKV_CHUNK, w), lambda b, g, c: (b * nc + chunk(c), off // w + g))
    st = pl.BlockSpec((1, hp, HEAD_DIM, HEAD_DIM), lambda b, g, c: ((b * (N_HEADS // hp) + g) * nc + chunk(c), 0, 0, 0))
    return tok, st


def rwkv_scan_fwd(ps, lw, k2, na, bb, nb, t):
    hp, nc = RWKV_HEADS_PER_STEP, t // RWKV_CHUNK
    ng = N_HEADS // hp
    tok, st = _rwkv_specs(nc, False)

    def body(r_ref, v_ref, lw_ref, k_ref, a_ref, b_ref, y_ref, s0_ref, s):
        @pl.when(pl.program_id(2) == 0)
        def _():
            s[...] = jnp.zeros_like(s)

        s0_ref[0] = s[...]
        for h in range(hp):
            sl = _head(h)
            y, s1 = rwkv_chunk(s[h], r_ref[:, sl], lw_ref[:, sl], k_ref[:, sl], v_ref[:, sl], a_ref[:, sl], b_ref[:, sl])
            y_ref[:, sl] = y
            s[h] = s1

    return pl.pallas_call(
        body, name="rwkv_scan_fwd", grid=(nb, ng, nc),
        in_specs=[tok(0), tok(1024), tok(0), tok(0), tok(0), tok(0)], out_specs=[tok(0), st],
        out_shape=[jax.ShapeDtypeStruct((nb * t, N_HEADS * HEAD_DIM), f32),
                   jax.ShapeDtypeStruct((nb * ng * nc, hp, HEAD_DIM, HEAD_DIM), f32)],
        scratch_shapes=[pltpu.VMEM((hp, HEAD_DIM, HEAD_DIM), f32)],
        compiler_params=_cparams(("parallel", "parallel", "arbitrary")),
    )(ps, ps, lw, k2, na, bb)


def rwkv_scan_bwd(s0, ps, lw, k2, na, bb, dy, nb, t):
    hp, nc = RWKV_HEADS_PER_STEP, t // RWKV_CHUNK
    ng = N_HEADS // hp
    tok, st = _rwkv_specs(nc, True)

    def body(s0_ref, r_ref, v_ref, lw_ref, k_ref, a_ref, b_ref, dy_ref, dr, dlw, dk, dv, da, db, ds):
        @pl.when(pl.program_id(2) == 0)
        def _():
            ds[...] = jnp.zeros_like(ds)

        for h in range(hp):
            sl = _head(h)
            _, vjp = jax.vjp(rwkv_chunk, s0_ref[0, h], r_ref[:, sl], lw_ref[:, sl], k_ref[:, sl], v_ref[:, sl],
                             a_ref[:, sl], b_ref[:, sl])
            g = vjp((dy_ref[:, sl], ds[h]))
            ds[h] = g[0]
            for ref, val in zip((dr, dlw, dk, dv, da, db), g[1:]):
                ref[:, sl] = val

    return pl.pallas_call(
        body, name="rwkv_scan_bwd", grid=(nb, ng, nc),
        in_specs=[st, tok(0), tok(1024), tok(0), tok(0), tok(0), tok(0), tok(0)], out_specs=[tok(0)] * 6,
        out_shape=[jax.ShapeDtypeStruct((nb * t, N_HEADS * HEAD_DIM), f32)] * 6,
        scratch_shapes=[pltpu.VMEM((hp, HEAD_DIM, HEAD_DIM), f32)],
        compiler_params=_cparams(("parallel", "parallel", "arbitrary")),
    )(s0, ps, ps, lw, k2, na, bb, dy)


def _ssd_specs(nb, nch, rev):
    def row(b, c):
        return b * nch + (nch - 1 - c if rev else c)

    l = SSM_CHUNK
    xs = pl.BlockSpec((l, 512), lambda b, c: (row(b, c), 0))
    bm = pl.BlockSpec((l, 256), lambda b, c: (row(b, c), 2))
    cm = pl.BlockSpec((l, 256), lambda b, c: (row(b, c), 3))
    dt = pl.BlockSpec((l, LANE), lambda b, c: (row(b, c), (L0_PAD - LANE) // LANE))
    st = pl.BlockSpec((1, 1, N_HEADS, SSM_STATE, HEAD_DIM), lambda b, c: (b, (nch - 1 - c if rev else c), 0, 0, 0))
    par = pl.BlockSpec((1, LANE), lambda b, c: (0, 0))
    eh = pl.BlockSpec((N_HEADS, LANE, HEAD_DIM), lambda b, c: (0, 0, 0))
    return xs, bm, cm, dt, st, par, eh, row


def ssd_fwd(xbc_act, proj0, dt_bias, a_log, d_skip, e_heads, nb, t):
    nch = t // SSM_CHUNK
    n_tok = nb * t
    xs, bm, cm, dt, st, par, eh, row = _ssd_specs(nb, nch, False)

    def body(x_ref, b_ref, c_ref, dt_ref, db_ref, al_ref, dsk_ref, e_ref, y_ref, st_ref, s):
        @pl.when(pl.program_id(1) == 0)
        def _():
            s[...] = jnp.zeros_like(s)

        st_ref[0, 0] = s[...]
        y, s_out = ssd_chunk(x_ref[...], b_ref[...], c_ref[...], dt_ref[...], [s[h] for h in range(N_HEADS)],
                             db_ref[...], al_ref[...], dsk_ref[...], [e_ref[h] for h in range(N_HEADS)])
        y_ref[...] = y
        for h in range(N_HEADS):
            s[h] = s_out[h]

    return pl.pallas_call(
        body, name="ssd_fwd", grid=(nb, nch), in_specs=[xs, bm, cm, dt, par, par, par, eh],
        out_specs=[pl.BlockSpec((SSM_CHUNK, 512), lambda b, c: (row(b, c), 0)), st],
        out_shape=[jax.ShapeDtypeStruct((n_tok, 512), f32),
                   jax.ShapeDtypeStruct((nb, nch, N_HEADS, SSM_STATE, HEAD_DIM), f32)],
        scratch_shapes=[pltpu.VMEM((N_HEADS, SSM_STATE, HEAD_DIM), f32)],
        compiler_params=_cparams(("arbitrary", "arbitrary")),
    )(xbc_act, xbc_act, xbc_act, proj0, dt_bias, a_log, d_skip, e_heads)


def ssd_bwd(xbc_act, proj0, dt_bias, a_log, d_skip, e_heads, states, dy, nb, t):
    nch = t // SSM_CHUNK
    n_tok = nb * t
    xs, bm, cm, dt, st, par, eh, row = _ssd_specs(nb, nch, True)

    def body(x_ref, b_ref, c_ref, dt_ref, db_ref, al_ref, dsk_ref, e_ref, st_ref, dy_ref,
             dx_ref, dbm_ref, dcm_ref, ddt_ref, ddb_ref, dal_ref, ddsk_ref, ds):
        first = jnp.logical_and(pl.program_id(0) == 0, pl.program_id(1) == 0)

        @pl.when(pl.program_id(1) == 0)
        def _():
            ds[...] = jnp.zeros_like(ds)

        e_list = [e_ref[h] for h in range(N_HEADS)]

        def f(x, bmv, cmv, dtr, s_in, dbv, alv, dskv):
            return ssd_chunk(x, bmv, cmv, dtr, s_in, dbv, alv, dskv, e_list)

        _, vjp = jax.vjp(f, x_ref[...], b_ref[...], c_ref[...], dt_ref[...],
                         [st_ref[0, 0, h] for h in range(N_HEADS)], db_ref[...], al_ref[...], dsk_ref[...])
        g = vjp((dy_ref[...], [ds[h] for h in range(N_HEADS)]))
        dx_ref[...], dbm_ref[...], dcm_ref[...], ddt_ref[...] = g[0], g[1], g[2], g[3]
        for h in range(N_HEADS):
            ds[h] = g[4][h]
        for ref, val in zip((ddb_ref, dal_ref, ddsk_ref), g[5:]):
            @pl.when(first)
            def _():
                ref[...] = val

            @pl.when(jnp.logical_not(first))
            def _():
                ref[...] += val

    rows_spec = lambda w: pl.BlockSpec((SSM_CHUNK, w), lambda b, c: (row(b, c), 0))
    return pl.pallas_call(
        body, name="ssd_bwd", grid=(nb, nch),
        in_specs=[xs, bm, cm, dt, par, par, par, eh, st, rows_spec(512)],
        out_specs=[rows_spec(512), rows_spec(256), rows_spec(256), rows_spec(LANE), par, par, par],
        out_shape=[jax.ShapeDtypeStruct((n_tok, 512), f32), jax.ShapeDtypeStruct((n_tok, 256), f32),
                   jax.ShapeDtypeStruct((n_tok, 256), f32), jax.ShapeDtypeStruct((n_tok, LANE), f32)]
        + [jax.ShapeDtypeStruct((1, LANE), f32)] * 3,
        scratch_shapes=[pltpu.VMEM((N_HEADS, SSM_STATE, HEAD_DIM), f32)],
        compiler_params=_cparams(("arbitrary", "arbitrary")),
    )(xbc_act, xbc_act, xbc_act, proj0, dt_bias, a_log, d_skip, e_heads, states, dy)


ATT_BQ = 256
SB_HEADS_PER_STEP = 2
MLA_HEADS_PER_STEP = 4


def _sb_specs(t, bq, nq):
    w = SB_HEADS_PER_STEP * HEAD_DIM
    qs = lambda off: pl.BlockSpec((bq, w), lambda b, g, i: (b * nq + i, off // w + g))
    ks = lambda off: pl.BlockSpec((t, w), lambda b, g, i: (b, off // w + g))
    return qs, ks


def sb_fwd(proj1, nb, t):
    bq = min(ATT_BQ, t)
    nq = t // bq
    qs, ks = _sb_specs(t, bq, nq)

    def body(q_ref, k_ref, v_ref, o_ref):
        q0 = pl.program_id(2) * bq
        for h in range(SB_HEADS_PER_STEP):
            sl = _head(h)
            o_ref[:, sl] = sb_block(q_ref[:, sl], k_ref[:, sl], v_ref[:, sl], q0)

    return pl.pallas_call(
        body, name="sb_fwd", grid=(nb, N_HEADS // SB_HEADS_PER_STEP, nq), in_specs=[qs(0), ks(512), ks(1024)],
        out_specs=qs(0), out_shape=jax.ShapeDtypeStruct((nb * t, 512), f32),
        compiler_params=_cparams(("parallel", "parallel", "arbitrary")),
    )(proj1, proj1, proj1)


def sb_bwd(proj1, do, nb, t):
    bq = min(ATT_BQ, t)
    nq = t // bq
    qs, ks = _sb_specs(t, bq, nq)

    def body(q_ref, k_ref, v_ref, do_ref, dq_ref, dk_ref, dv_ref):
        q0 = pl.program_id(2) * bq
        first = pl.program_id(2) == 0
        for h in range(SB_HEADS_PER_STEP):
            sl = _head(h)
            _, vjp = jax.vjp(lambda a, b, c: sb_block(a, b, c, q0), q_ref[:, sl], k_ref[:, sl], v_ref[:, sl])
            dq, dk, dv = vjp(do_ref[:, sl])
            dq_ref[:, sl] = dq

            @pl.when(first)
            def _():
                dk_ref[:, sl] = dk
                dv_ref[:, sl] = dv

            @pl.when(jnp.logical_not(first))
            def _():
                dk_ref[:, sl] += dk
                dv_ref[:, sl] += dv

    return pl.pallas_call(
        body, name="sb_bwd", grid=(nb, N_HEADS // SB_HEADS_PER_STEP, nq),
        in_specs=[qs(0), ks(512), ks(1024), qs(0)], out_specs=[qs(0), ks(0), ks(0)],
        out_shape=[jax.ShapeDtypeStruct((nb * t, 512), f32)] * 3,
        compiler_params=_cparams(("parallel", "parallel", "arbitrary")),
    )(proj1, proj1, proj1, do)


def _mla_specs(t, bq, nq):
    hp = MLA_HEADS_PER_STEP
    qn = pl.BlockSpec((bq, hp * HEAD_DIM), lambda b, g, i: (b * nq + i, g))
    qr = pl.BlockSpec((bq, hp * MLA_ROPE), lambda b, g, i: (b * nq + i, g))
    kv = pl.BlockSpec((t, hp * 2 * HEAD_DIM), lambda b, g, i: (b, g))
    kp = pl.BlockSpec((t, LANE), lambda b, g, i: (b, 0))
    return qn, qr, kv, kp


def _mla_heads(qn_ref, qr_ref, kv_ref, kp_ref):
    out = []
    for h in range(MLA_HEADS_PER_STEP):
        out.append((qn_ref[:, _head(h)], qr_ref[:, h * MLA_ROPE:(h + 1) * MLA_ROPE], kv_ref[:, _head(2 * h)],
                    kp_ref[:, :MLA_ROPE], kv_ref[:, _head(2 * h + 1)]))
    return out


def mla_fwd(q_nope, qr, kv, kpe, nb, t):
    bq = min(ATT_BQ, t)
    nq = t // bq
    sqn, sqr, skv, skp = _mla_specs(t, bq, nq)

    def body(qn_ref, qr_ref, kv_ref, kp_ref, o_ref):
        q0 = pl.program_id(2) * bq
        for h, args in enumerate(_mla_heads(qn_ref, qr_ref, kv_ref, kp_ref)):
            o_ref[:, _head(h)] = mla_block(*args, q0)

    return pl.pallas_call(
        body, name="mla_fwd", grid=(nb, N_HEADS // MLA_HEADS_PER_STEP, nq), in_specs=[sqn, sqr, skv, skp],
        out_specs=sqn, out_shape=jax.ShapeDtypeStruct((nb * t, 512), f32),
        compiler_params=_cparams(("parallel", "arbitrary", "arbitrary")),
    )(q_nope, qr, kv, kpe)


def mla_bwd(q_nope, qr, kv, kpe, do, nb, t):
    bq = min(ATT_BQ, t)
    nq = t // bq
    sqn, sqr, skv, skp = _mla_specs(t, bq, nq)

    def body(qn_ref, qr_ref, kv_ref, kp_ref, do_ref, dqn_ref, dqr_ref, dkv_ref, dkp_ref):
        q0 = pl.program_id(2) * bq
        first_q = pl.program_id(2) == 0
        first_kp = jnp.logical_and(first_q, pl.program_id(1) == 0)

        @pl.when(first_kp)
        def _():
            dkp_ref[...] = jnp.zeros_like(dkp_ref)

        for h, args in enumerate(_mla_heads(qn_ref, qr_ref, kv_ref, kp_ref)):
            _, vjp = jax.vjp(lambda a, b, c, d, e: mla_block(a, b, c, d, e, q0), *args)
            dqn, dqp, dkn, dkp, dv = vjp(do_ref[:, _head(h)])
            dqn_ref[:, _head(h)] = dqn
            dqr_ref[:, h * MLA_ROPE:(h + 1) * MLA_ROPE] = dqp
            dkp_ref[:, :MLA_ROPE] += dkp

            @pl.when(first_q)
            def _():
                dkv_ref[:, _head(2 * h)] = dkn
                dkv_ref[:, _head(2 * h + 1)] = dv

            @pl.when(jnp.logical_not(first_q))
            def _():
                dkv_ref[:, _head(2 * h)] += dkn
                dkv_ref[:, _head(2 * h + 1)] += dv

    n = nb * t
    return pl.pallas_call(
        body, name="mla_bwd", grid=(nb, N_HEADS // MLA_HEADS_PER_STEP, nq),
        in_specs=[sqn, sqr, skv, skp, sqn], out_specs=[sqn, sqr, skv, skp],
        out_shape=[jax.ShapeDtypeStruct((n, 512), f32), jax.ShapeDtypeStruct((n, N_HEADS * MLA_ROPE), f32),
                   jax.ShapeDtypeStruct((n, 1024), f32), jax.ShapeDtypeStruct((n, LANE), f32)],
        compiler_params=_cparams(("arbitrary", "arbitrary", "arbitrary")),
    )(q_nope, qr, kv, kpe, do)


def loss_head(h, target):
    n, d = h.shape
    tm = _tile(n, 512)

    def body(h_ref, t_ref, l_ref, dh_ref):
        diff = h_ref[...] - t_ref[...]
        dh_ref[...] = diff * (1.0 / d)
        part = 0.5 * jnp.sum(jnp.sum(diff * diff, axis=1, keepdims=True) * (1.0 / d), axis=0, keepdims=True)

        @pl.when(pl.program_id(0) == 0)
        def _():
            l_ref[...] = jnp.zeros_like(l_ref)

        l_ref[...] += jnp.broadcast_to(part, l_ref.shape)

    spec = pl.BlockSpec((tm, d), lambda i: (i, 0))
    return pl.pallas_call(
        body, name="loss_head", grid=(n // tm,), in_specs=[spec, spec],
        out_specs=[pl.BlockSpec((8, LANE), lambda i: (0, 0)), spec],
        out_shape=[jax.ShapeDtypeStruct((8, LANE), f32), jax.ShapeDtypeStruct((n, d), f32)],
        compiler_params=_cparams(("arbitrary",)),
    )(h, target)


def _row(v):
    return v.reshape(1, -1)


def _pad_cols(a, n):
    return jnp.pad(a, ((0, 0), (0, n - a.shape[1])))


def _pad_row(v, n=LANE):
    return jnp.pad(v.reshape(1, -1), ((0, 0), (0, n - v.shape[0])))


def _group_matrix(width, group):
    idx = np.arange(width) // group
    return jnp.asarray((idx[:, None] == idx[None, :]).astype(np.float32))


def _head_expand():
    e = np.zeros((N_HEADS, LANE, HEAD_DIM), np.float32)
    for h in range(N_HEADS):
        e[h, h, :] = 1.0
    return jnp.asarray(e)


def _rope_freqs():
    inv = 1.0 / (ROPE_THETA ** (np.arange(0, MLA_ROPE, 2, dtype=np.float32) / MLA_ROPE))
    inv = np.tile(inv.astype(np.float32), 2)
    inv_q = np.tile(inv, N_HEADS).reshape(1, N_HEADS * MLA_ROPE)
    inv_k = np.zeros((1, LANE), np.float32)
    inv_k[0, :MLA_ROPE] = inv
    return jnp.asarray(inv_q), jnp.asarray(inv_k)


def _uq_split(w):
    w3 = w.reshape(w.shape[0], N_HEADS, HEAD_DIM + MLA_ROPE)
    return w3[:, :, :HEAD_DIM].reshape(-1, 512), w3[:, :, HEAD_DIM:].reshape(-1, N_HEADS * MLA_ROPE)


def _uq_merge(gn, gr):
    r = gn.shape[0]
    return jnp.concatenate([gn.reshape(r, N_HEADS, HEAD_DIM), gr.reshape(r, N_HEADS, MLA_ROPE)], axis=2).reshape(r, 768)


def local_step(x, positions, target, w):
    nb, t, d = x.shape
    n = nb * t
    tm = 256
    ni = n // tm
    tc = LANE
    h0 = x.reshape(n, d)
    tgt = target.reshape(n, d)
    pos = positions.reshape(n, 1).astype(f32)
    gh = _group_matrix(512, HEAD_DIM)
    gg = _group_matrix(512, 256)
    e_heads = _head_expand()
    inv_q, inv_k = _rope_freqs()
    g = {}

    def ln_stage(h, y, gname, bname):
        ops = [_rows(h, tm), _rows(y, tm), _param(_row(w[gname])), _param(_row(w[bname]))]
        return ops, [_rows_out(n, d, tm)]

    def ffn_act_stage(u, cw, cb):
        nj = D_FF // tc
        ops = [_cols(u, t, tc, 0, D_FF), _cols(u, t, tc, nj, D_FF)] + [_cparam(cw[i:i + 1], tc) for i in range(3)] \
            + [_cparam(_row(cb), tc)]
        return ops, [Out((n, D_FF), (t, tc), lambda j, i: (i, j))], (nj, nb)

    w_in0 = _pad_cols(w['l0_w_in'], L0_PAD)
    w_out0 = w['l0_w_out']
    proj0 = mm(h0, w_in0, "l0_proj")

    shift_ops = [_cols(proj0, t, tc, 0, RWKV_COLS), _cparam(_row(w['rwkv_mix']), tc)]
    shift_outs = [Out((n, RWKV_COLS), (t, tc), lambda j, i: (i, j))]
    shift_grid = (RWKV_COLS // tc, nb)
    (ps,) = block_fwd(f_shift_mix, "rwkv_shift", shift_grid, shift_ops, shift_outs)

    pre_ops = [_colblock(ps, tm, 512, 512), _colblock(ps, tm, 1536, 128), _colblock(ps, tm, 1664, 128),
               _param(_row(w['rwkv_w0'])), _param(w['rwkv_w2']), _param(_row(w['rwkv_a0'])), _param(w['rwkv_a2']),
               _param(w['rwkv_g2']), _param(_row(w['rwkv_k_k'])), _param(_row(w['rwkv_k_a'])), _param(gh, diff=False)]
    pre_outs = [_rows_out(n, 512, tm) for _ in range(5)]
    lw, k2, na, bb, gate_r = block_fwd(f_rwkv_pre, "rwkv_pre", (1, ni), pre_ops, pre_outs)
    y_tok, s0_saved = rwkv_scan_fwd(ps, lw, k2, na, bb, nb, t)

    post_ops = [_rows(y_tok, tm), _colblock(ps, tm, 0, 512), _rows(k2, tm), _colblock(ps, tm, 1024, 512),
                _rows(gate_r, tm), _param(_row(w['rwkv_ln_g'])), _param(_row(w['rwkv_ln_b'])),
                _param(w['rwkv_r_k'].reshape(1, 512)), _param(gh, diff=False)]
    post_outs = [_rows_out(n, 512, tm)]
    (y_a,) = block_fwd(f_rwkv_post, "rwkv_post", (1, ni), post_ops, post_outs)

    xbc_off = (RWKV_COLS + 512) // tc
    conv_ops = [_cols(proj0, t, tc, xbc_off, 1024)] + [_cparam(w['ssm_conv_w'][i:i + 1], tc) for i in range(4)] \
        + [_cparam(_row(w['ssm_conv_b']), tc)]
    conv_outs = [Out((n, 1024), (t, tc), lambda j, i: (i, j))]
    conv_grid = (1024 // tc, nb)
    (xbc_act,) = block_fwd(f_conv4_silu, "ssm_conv", conv_grid, conv_ops, conv_outs)

    dt_bias, a_log, d_skip = _pad_row(w['ssm_dt_bias']), _pad_row(w['ssm_a_log']), _pad_row(w['ssm_d'])
    y_ssd, ssd_states = ssd_fwd(xbc_act, proj0, dt_bias, a_log, d_skip, e_heads, nb, t)

    z_tok = proj0[:, RWKV_COLS:RWKV_COLS + 512]
    spost_ops = [_rows(y_ssd, tm), _rows(z_tok, tm), _param(_row(w['ssm_norm_g'])), _param(gg, diff=False)]
    spost_outs = [_rows_out(n, 512, tm)]
    (y_b,) = block_fwd(f_ssm_post, "ssm_post", (1, ni), spost_ops, spost_outs)

    mixed0 = mm(y_b, w_out0[512:], "l0_out_b", add=mm(y_a, w_out0[:512], "l0_out_a"))
    ln1_ops, ln_outs = ln_stage(h0, mixed0, 'l0_ln1_g', 'l0_ln1_b')
    (h1,) = block_fwd(f_ln, "l0_ln1", (1, ni), ln1_ops, ln_outs)

    u0 = mm(h1, w['ffn0_w_up'], "ffn0_up")
    act0_ops, act_outs, act_grid = ffn_act_stage(u0, w['ffn0_conv_w'], w['ffn0_conv_b'])
    (act0,) = block_fwd(f_ffn_act, "ffn0_act", act_grid, act0_ops, act_outs)
    f0 = mm(act0, w['ffn0_w_down'], "ffn0_down")
    ln2_ops, _ = ln_stage(h1, f0, 'l0_ln2_g', 'l0_ln2_b')
    (h2,) = block_fwd(f_ln, "l0_ln2", (1, ni), ln2_ops, ln_outs)

    w_in1 = _pad_cols(w['l1_w_in'], L1_PAD)
    w_out1 = w['l1_w_out']
    proj1 = mm(h2, w_in1, "l1_proj")
    w_qn, w_qr = _uq_split(w['mla_w_uq'])
    mpre_ops = [_colblock(proj1, tm, 1536, 256), _colblock(proj1, tm, 1792, 128), _colblock(proj1, tm, 1920, 128),
                Op(pos, (tm, 1), lambda j, i: (i, 0), diff=False),
                _param(_row(w['mla_q_norm_g'])), _param(w_qn), _param(w_qr),
                _param(_row(w['mla_kv_norm_g'])), _param(w['mla_w_ukv']), _param(inv_q, diff=False),
                _param(inv_k, diff=False)]
    mpre_outs = [_rows_out(n, 512, tm), _rows_out(n, N_HEADS * MLA_ROPE, tm), _rows_out(n, 1024, tm),
                 _rows_out(n, LANE, tm)]
    q_nope, q_rope, kv, kpe = block_fwd(f_mla_pre, "mla_pre", (1, ni), mpre_ops, mpre_outs)
    o_sb = sb_fwd(proj1, nb, t)
    o_mla = mla_fwd(q_nope, q_rope, kv, kpe, nb, t)

    mixed1 = mm(o_mla, w_out1[512:], "l1_out_b", add=mm(o_sb, w_out1[:512], "l1_out_a"))
    ln3_ops, _ = ln_stage(h2, mixed1, 'l1_ln1_g', 'l1_ln1_b')
    (h3,) = block_fwd(f_ln, "l1_ln1", (1, ni), ln3_ops, ln_outs)
    u1 = mm(h3, w['ffn1_w_up'], "ffn1_up")
    act1_ops, _, _ = ffn_act_stage(u1, w['ffn1_conv_w'], w['ffn1_conv_b'])
    (act1,) = block_fwd(f_ffn_act, "ffn1_act", act_grid, act1_ops, act_outs)
    f1 = mm(act1, w['ffn1_w_down'], "ffn1_down")
    ln4_ops, _ = ln_stage(h3, f1, 'l1_ln2_g', 'l1_ln2_b')
    (h4,) = block_fwd(f_ln, "l1_ln2", (1, ni), ln4_ops, ln_outs)

    loss_part, dh4 = loss_head(h4, tgt)

    def vec(a_):
        return a_.reshape(-1)

    def ffn_bwd(tag, dh_out, ln_ops, act_ops, h_in, act, w_up, w_down, names):
        dh_res, df, gg_, gb_ = block_bwd(f_ln, tag + "_ln2_bwd", (1, ni), ln_ops, ln_outs, [dh_out])
        g[names[4]], g[names[5]] = vec(gg_), vec(gb_)
        g[names[3]] = mm(act, df, tag + "_down_dw", ta=True)
        dact = mm(df, w_down.T, tag + "_down_dx")
        dgate, dup, dw0, dw1, dw2, dcb = block_bwd(f_ffn_act, tag + "_act_bwd", act_grid, act_ops, act_outs, [dact])
        g[names[1]] = jnp.concatenate([dw0, dw1, dw2], axis=0)
        g[names[2]] = vec(dcb)
        g[names[0]] = jnp.concatenate([mm(h_in, dgate, tag + "_gate_dw", ta=True),
                                       mm(h_in, dup, tag + "_upv_dw", ta=True)], axis=1)
        w_up_t = w_up.T
        dh = mm(dgate, w_up_t[:D_FF], tag + "_gate_dx", add=dh_res)
        return mm(dup, w_up_t[D_FF:], tag + "_upv_dx", add=dh)

    def out_bwd(tag, dmixed, y_first, y_second, w_out, name):
        g[name] = jnp.concatenate([mm(y_first, dmixed, tag + "_a_dw", ta=True),
                                   mm(y_second, dmixed, tag + "_b_dw", ta=True)], axis=0)
        w_t = w_out.T
        return mm(dmixed, w_t[:, :512], tag + "_a_dx"), mm(dmixed, w_t[:, 512:], tag + "_b_dx")

    dh3 = ffn_bwd("ffn1", dh4, ln4_ops, act1_ops, h3, act1, w['ffn1_w_up'], w['ffn1_w_down'],
                  ['ffn1_w_up', 'ffn1_conv_w', 'ffn1_conv_b', 'ffn1_w_down', 'l1_ln2_g', 'l1_ln2_b'])

    dh2_res, dmixed1, g3g, g3b = block_bwd(f_ln, "l1_ln1_bwd", (1, ni), ln3_ops, ln_outs, [dh3])
    g['l1_ln1_g'], g['l1_ln1_b'] = vec(g3g), vec(g3b)
    do_sb, do_mla = out_bwd("l1_out", dmixed1, o_sb, o_mla, w_out1, 'l1_w_out')

    dq_nope, dq_rope, dkv, dkpe = mla_bwd(q_nope, q_rope, kv, kpe, do_mla, nb, t)
    dsb_q, dsb_k, dsb_v = sb_bwd(proj1, do_sb, nb, t)
    (dc_q, dc_kv, dkpe_raw, gqg, gwqn, gwqr, gkvg, g['mla_w_ukv']) = block_bwd(
        f_mla_pre, "mla_pre_bwd", (1, ni), mpre_ops, mpre_outs, [dq_nope, dq_rope, dkv, dkpe])
    g['mla_q_norm_g'], g['mla_kv_norm_g'] = vec(gqg), vec(gkvg)
    g['mla_w_uq'] = _uq_merge(gwqn, gwqr)
    dproj1 = jnp.concatenate([dsb_q, dsb_k, dsb_v, dc_q, dc_kv, dkpe_raw], axis=1)
    g['l1_w_in'] = mm(h2, dproj1, "l1_proj_dw", ta=True)[:, :L1_COLS]
    dh2 = mm(dproj1, w_in1.T, "l1_proj_dx", add=dh2_res)

    dh1 = ffn_bwd("ffn0", dh2, ln2_ops, act0_ops, h1, act0, w['ffn0_w_up'], w['ffn0_w_down'],
                  ['ffn0_w_up', 'ffn0_conv_w', 'ffn0_conv_b', 'ffn0_w_down', 'l0_ln2_g', 'l0_ln2_b'])

    dh0_res, dmixed0, g1g, g1b = block_bwd(f_ln, "l0_ln1_bwd", (1, ni), ln1_ops, ln_outs, [dh1])
    g['l0_ln1_g'], g['l0_ln1_b'] = vec(g1g), vec(g1b)
    dy_a, dy_b = out_bwd("l0_out", dmixed0, y_a, y_b, w_out0, 'l0_w_out')

    dy_ssd, dz, gng = block_bwd(f_ssm_post, "ssm_post_bwd", (1, ni), spost_ops, spost_outs, [dy_b])
    g['ssm_norm_g'] = vec(gng)
    dxs, dbm, dcm, ddt_raw, gdb, gal, gdsk = ssd_bwd(xbc_act, proj0, dt_bias, a_log, d_skip, e_heads, ssd_states,
                                                     dy_ssd, nb, t)
    g['ssm_dt_bias'], g['ssm_a_log'], g['ssm_d'] = gdb[0, :8], gal[0, :8], gdsk[0, :8]
    dxbc_act = jnp.concatenate([dxs, dbm, dcm], axis=1)
    dxbc, cw0, cw1, cw2, cw3, gcb = block_bwd(f_conv4_silu, "ssm_conv_bwd", conv_grid, conv_ops, conv_outs, [dxbc_act])
    g['ssm_conv_w'] = jnp.concatenate([cw0, cw1, cw2, cw3], axis=0)
    g['ssm_conv_b'] = vec(gcb)

    dy_tok, dr_post, dk2_post, dv_post, dgate, glg, glb, grk = block_bwd(
        f_rwkv_post, "rwkv_post_bwd", (1, ni), post_ops, post_outs, [dy_a])
    g['rwkv_ln_g'], g['rwkv_ln_b'], g['rwkv_r_k'] = vec(glg), vec(glb), grk.reshape(N_HEADS, HEAD_DIM)
    dr, dlw, dk2, dv, dna, dbb = rwkv_scan_bwd(s0_saved, ps, lw, k2, na, bb, dy_tok, nb, t)
    (dk_pre, dwa_lo, dg_lo, gw0, g['rwkv_w2'], ga0, g['rwkv_a2'], g['rwkv_g2'], gkk, gka) = block_bwd(
        f_rwkv_pre, "rwkv_pre_bwd", (1, ni), pre_ops, pre_outs, [dlw, dk2 + dk2_post, dna, dbb, dgate])
    g['rwkv_w0'], g['rwkv_a0'], g['rwkv_k_k'], g['rwkv_k_a'] = vec(gw0), vec(ga0), vec(gkk), vec(gka)
    dps = jnp.concatenate([dr + dr_post, dk_pre, dv + dv_post, dwa_lo, dg_lo], axis=1)
    dp_rwkv, gmix = block_bwd(f_shift_mix, "rwkv_shift_bwd", shift_grid, shift_ops, shift_outs, [dps])
    g['rwkv_mix'] = vec(gmix)

    dproj0 = jnp.concatenate([dp_rwkv, dz, dxbc, ddt_raw], axis=1)
    g['l0_w_in'] = mm(h0, dproj0, "l0_proj_dw", ta=True)[:, :L0_COLS]
    grad_x = mm(dproj0, w_in0.T, "l0_proj_dx", add=dh0_res)
    return loss_part, grad_x.reshape(nb, t, d), g


MESH = pl.DeviceIdType.MESH
ANY = pl.BlockSpec(memory_space=pl.ANY)
AXES = ("x", "y", "c")


def _place():
    x, y, c = lax.axis_index("x"), lax.axis_index("y"), lax.axis_index("c")
    chips = [(1 - x, y), (x, 1 - y), (1 - x, 1 - y)]
    return x, y, c, chips


def _dma_sems(n):
    return pltpu.SemaphoreType.DMA((n,))


def gather_shards(shards):
    n = len(shards)

    def body(*refs):
        ins, outs = refs[:n], refs[n:2 * n]
        send_sems, recv_sems, local_sems = refs[2 * n:]
        x, y, c, chips = _place()
        me = 2 * x + y
        mine = [pltpu.make_async_copy(a, o.at[me], local_sems.at[k]) for k, (a, o) in enumerate(zip(ins, outs))]
        for cp in mine:
            cp.start()
        sends = [pltpu.make_async_remote_copy(
            src_ref=a, dst_ref=o.at[me], send_sem=send_sems.at[3 * k + j], recv_sem=recv_sems.at[3 * k + j],
            device_id=(cx, cy, c), device_id_type=MESH)
            for k, (a, o) in enumerate(zip(ins, outs)) for j, (cx, cy) in enumerate(chips)]
        for cp in sends:
            cp.start()
        for k, (a, o) in enumerate(zip(ins, outs)):
            for j, (cx, cy) in enumerate(chips):
                pltpu.make_async_remote_copy(
                    src_ref=a, dst_ref=o.at[2 * cx + cy], send_sem=send_sems.at[3 * k + j],
                    recv_sem=recv_sems.at[3 * k + j], device_id=(cx, cy, c), device_id_type=MESH).wait_recv()
        for cp in sends:
            cp.wait_send()
        for cp in mine:
            cp.wait()

    return pl.pallas_call(
        body, name="gather_shards", in_specs=[ANY] * n, out_specs=[ANY] * n,
        out_shape=[jax.ShapeDtypeStruct((N_SHARD,) + a.shape, a.dtype) for a in shards],
        scratch_shapes=[_dma_sems(3 * n), _dma_sems(3 * n), _dma_sems(n)],
    )(*shards)


def swap_halves(pieces):
    n = len(pieces)

    def body(*refs):
        ins, outs = refs[:n], refs[n:2 * n]
        send_sems, recv_sems = refs[2 * n:]
        x, y, c, _ = _place()
        cps = []
        for k, (a, o) in enumerate(zip(ins, outs)):
            h = a.shape[1] // 2
            cps.append(pltpu.make_async_remote_copy(
                src_ref=a.at[:, pl.ds((1 - c) * h, h), :], dst_ref=o, send_sem=send_sems.at[k],
                recv_sem=recv_sems.at[k], device_id=(x, y, 1 - c), device_id_type=MESH))
        for cp in cps:
            cp.start()
        for cp in cps:
            cp.wait()

    return pl.pallas_call(
        body, name="swap_halves", in_specs=[ANY] * n, out_specs=[ANY] * n,
        out_shape=[jax.ShapeDtypeStruct((a.shape[0], a.shape[1] // 2, a.shape[2]), a.dtype) for a in pieces],
        scratch_shapes=[_dma_sems(n), _dma_sems(n)],
    )(*pieces)


def scatter_to_chips(parts):
    n = len(parts)

    def body(*refs):
        ins, outs = refs[:n], refs[n:2 * n]
        send_sems, recv_sems, local_sems = refs[2 * n:]
        x, y, c, chips = _place()
        me = 2 * x + y
        mine = [pltpu.make_async_copy(a.at[me], o.at[me], local_sems.at[k]) for k, (a, o) in enumerate(zip(ins, outs))]
        for cp in mine:
            cp.start()
        sends = [pltpu.make_async_remote_copy(
            src_ref=a.at[2 * cx + cy], dst_ref=o.at[me], send_sem=send_sems.at[3 * k + j],
            recv_sem=recv_sems.at[3 * k + j], device_id=(cx, cy, c), device_id_type=MESH)
            for k, (a, o) in enumerate(zip(ins, outs)) for j, (cx, cy) in enumerate(chips)]
        for cp in sends:
            cp.start()
        for k, (a, o) in enumerate(zip(ins, outs)):
            for j, (cx, cy) in enumerate(chips):
                pltpu.make_async_remote_copy(
                    src_ref=a.at[me], dst_ref=o.at[2 * cx + cy], send_sem=send_sems.at[3 * k + j],
                    recv_sem=recv_sems.at[3 * k + j], device_id=(cx, cy, c), device_id_type=MESH).wait_recv()
        for cp in sends:
            cp.wait_send()
        for cp in mine:
            cp.wait()

    return pl.pallas_call(
        body, name="scatter_to_chips", in_specs=[ANY] * n, out_specs=[ANY] * n,
        out_shape=[jax.ShapeDtypeStruct(a.shape, a.dtype) for a in parts],
        scratch_shapes=[_dma_sems(3 * n), _dma_sems(3 * n), _dma_sems(n)],
    )(*parts)


def share_halves(halves):
    n = len(halves)

    def body(*refs):
        ins, outs = refs[:n], refs[n:2 * n]
        send_sems, recv_sems, local_sems = refs[2 * n:]
        x, y, c, _ = _place()
        cps = []
        for k, (a, o) in enumerate(zip(ins, outs)):
            h = a.shape[0]
            cps.append(pltpu.make_async_copy(a, o.at[pl.ds(c * h, h), :], local_sems.at[k]))
            cps.append(pltpu.make_async_remote_copy(
                src_ref=a, dst_ref=o.at[pl.ds(c * h, h), :], send_sem=send_sems.at[k], recv_sem=recv_sems.at[k],
                device_id=(x, y, 1 - c), device_id_type=MESH))
        for cp in cps:
            cp.start()
        for k, (a, o) in enumerate(zip(ins, outs)):
            h = a.shape[0]
            cps[2 * k].wait()
            cps[2 * k + 1].wait_send()
            pltpu.make_async_remote_copy(
                src_ref=a, dst_ref=o.at[pl.ds((1 - c) * h, h), :], send_sem=send_sems.at[k], recv_sem=recv_sems.at[k],
                device_id=(x, y, 1 - c), device_id_type=MESH).wait_recv()

    return pl.pallas_call(
        body, name="share_halves", in_specs=[ANY] * n, out_specs=[ANY] * n,
        out_shape=[jax.ShapeDtypeStruct((2 * a.shape[0], a.shape[1]), a.dtype) for a in halves],
        scratch_shapes=[_dma_sems(n), _dma_sems(n), _dma_sems(n)],
    )(*halves)


def pair_add(piece, recv, core, name):
    _, r, cdim = piece.shape
    h = r // 2
    tr = _rtile(h, cdim)
    nt = h // tr

    def body(c_ref, a_ref, b_ref, o_ref):
        o_ref[...] = a_ref[...] + b_ref[...]

    spec = pl.BlockSpec((1, tr, cdim), lambda p, i, c_ref: (p, i, 0))
    return pl.pallas_call(
        body, name=name,
        grid_spec=pltpu.PrefetchScalarGridSpec(
            num_scalar_prefetch=1, grid=(N_SHARD, nt),
            in_specs=[pl.BlockSpec((1, tr, cdim), lambda p, i, c_ref: (p, c_ref[0] * nt + i, 0)), spec],
            out_specs=spec),
        out_shape=jax.ShapeDtypeStruct((N_SHARD, h, cdim), f32),
        compiler_params=_cparams(("parallel", "parallel")),
    )(core, piece, recv)


def chip_add(parts, name):
    _, h, cdim = parts.shape
    tr = _rtile(h, cdim, 1024 * 1024)

    def body(p_ref, o_ref):
        o_ref[...] = ((p_ref[0] + p_ref[1]) + p_ref[2]) + p_ref[3]

    return pl.pallas_call(
        body, name=name, grid=(h // tr,),
        in_specs=[pl.BlockSpec((N_SHARD, tr, cdim), lambda i: (0, i, 0))],
        out_specs=pl.BlockSpec((tr, cdim), lambda i: (i, 0)),
        out_shape=jax.ShapeDtypeStruct((h, cdim), f32), compiler_params=_cparams(("parallel",)),
    )(parts)


def adamw(w, g, m, v, name):
    rows, cdim = w.shape
    tr = _rtile(rows, cdim, 1024 * 1024)

    def body(w_ref, g_ref, m_ref, v_ref, d_ref, nm_ref, nv_ref):
        gv = g_ref[...]
        m_new = ADAM_B1 * m_ref[...] + (1.0 - ADAM_B1) * gv
        v_new = ADAM_B2 * v_ref[...] + (1.0 - ADAM_B2) * jnp.square(gv)
        m_hat = m_new / (1.0 - ADAM_B1 ** ADAM_STEP)
        v_hat = v_new / (1.0 - ADAM_B2 ** ADAM_STEP)
        d_ref[...] = -ADAM_LR * (m_hat / (jnp.sqrt(v_hat) + ADAM_EPS) + ADAM_WD * w_ref[...])
        nm_ref[...] = m_new
        nv_ref[...] = v_new

    spec = pl.BlockSpec((tr, cdim), lambda i: (i, 0))
    return pl.pallas_call(body, name=name, grid=(rows // tr,), in_specs=[spec] * 4, out_specs=[spec] * 3,
                          out_shape=[jax.ShapeDtypeStruct(w.shape, f32)] * 3,
                          compiler_params=_cparams(("parallel",)))(w, g, m, v)


SMALL_MULTIPLE = 16 * LANE


def _pack_flat(parts, multiple=SMALL_MULTIPLE):
    flat = jnp.concatenate([p.reshape(-1) for p in parts])
    pad = (-flat.shape[0]) % multiple
    return jnp.pad(flat, (0, pad)).reshape(-1, LANE)


def _unpack_flat(buf, shapes):
    flat = buf.reshape(-1)
    out, off = [], 0
    for s in shapes:
        cnt = int(np.prod(s))
        out.append(flat[off:off + cnt].reshape(s))
        off += cnt
    return out


def _full_from_shards(name, gathered):
    if name in COL_SHARDED:
        return jnp.concatenate([gathered[s] for s in range(N_SHARD)], axis=1)
    return gathered.reshape(-1, gathered.shape[2])


def _pieces(name, grad):
    if name in COL_SHARDED:
        r, cdim = grad.shape
        return grad.reshape(r, N_SHARD, cdim // N_SHARD).transpose(1, 0, 2)
    return grad.reshape(N_SHARD, grad.shape[0] // N_SHARD, grad.shape[1])


def _small_pieces(name, grad):
    if name in COL_SHARDED or name in ROW_SHARDED:
        return _pieces(name, grad).reshape(N_SHARD, -1)
    return jnp.broadcast_to(grad.reshape(1, -1), (N_SHARD, grad.size))


def kernel(x, positions, l0_w_in, rwkv_mix, rwkv_w0, rwkv_w2, rwkv_a0, rwkv_a2, rwkv_g2, rwkv_k_k, rwkv_k_a, rwkv_r_k, rwkv_ln_g, rwkv_ln_b, ssm_conv_w, ssm_conv_b, ssm_dt_bias, ssm_a_log, ssm_d, ssm_norm_g, l0_w_out, l0_ln1_g, l0_ln1_b, ffn0_w_up, ffn0_conv_w, ffn0_conv_b, ffn0_w_down, l0_ln2_g, l0_ln2_b, l1_w_in, mla_q_norm_g, mla_w_uq, mla_kv_norm_g, mla_w_ukv, l1_w_out, l1_ln1_g, l1_ln1_b, ffn1_w_up, ffn1_conv_w, ffn1_conv_b, ffn1_w_down, l1_ln2_g, l1_ln2_b, loss_target, m_l0_w_in, m_rwkv_mix, m_rwkv_w0, m_rwkv_w2, m_rwkv_a0, m_rwkv_a2, m_rwkv_g2, m_rwkv_k_k, m_rwkv_k_a, m_rwkv_r_k, m_rwkv_ln_g, m_rwkv_ln_b, m_ssm_conv_w, m_ssm_conv_b, m_ssm_dt_bias, m_ssm_a_log, m_ssm_d, m_ssm_norm_g, m_l0_w_out, m_l0_ln1_g, m_l0_ln1_b, m_ffn0_w_up, m_ffn0_conv_w, m_ffn0_conv_b, m_ffn0_w_down, m_l0_ln2_g, m_l0_ln2_b, m_l1_w_in, m_mla_q_norm_g, m_mla_w_uq, m_mla_kv_norm_g, m_mla_w_ukv, m_l1_w_out, m_l1_ln1_g, m_l1_ln1_b, m_ffn1_w_up, m_ffn1_conv_w, m_ffn1_conv_b, m_ffn1_w_down, m_l1_ln2_g, m_l1_ln2_b, v_l0_w_in, v_rwkv_mix, v_rwkv_w0, v_rwkv_w2, v_rwkv_a0, v_rwkv_a2, v_rwkv_g2, v_rwkv_k_k, v_rwkv_k_a, v_rwkv_r_k, v_rwkv_ln_g, v_rwkv_ln_b, v_ssm_conv_w, v_ssm_conv_b, v_ssm_dt_bias, v_ssm_a_log, v_ssm_d, v_ssm_norm_g, v_l0_w_out, v_l0_ln1_g, v_l0_ln1_b, v_ffn0_w_up, v_ffn0_conv_w, v_ffn0_conv_b, v_ffn0_w_down, v_l0_ln2_g, v_l0_ln2_b, v_l1_w_in, v_mla_q_norm_g, v_mla_w_uq, v_mla_kv_norm_g, v_mla_w_ukv, v_l1_w_out, v_l1_ln1_g, v_l1_ln1_b, v_ffn1_w_up, v_ffn1_conv_w, v_ffn1_conv_b, v_ffn1_w_down, v_l1_ln2_g, v_l1_ln2_b):
    args = locals()
    w_loc = {n: args[n] for n in WEIGHTS}
    m_loc = {n: args["m_" + n] for n in WEIGHTS}
    v_loc = {n: args["v_" + n] for n in WEIGHTS}
    core = lax.axis_index("c").astype(jnp.int32).reshape(1)

    small_sharded = [n for n in SMALL if n in COL_SHARDED]
    gathered = gather_shards([w_loc[n].astype(bf16) for n in BIG] + [_pack_flat([w_loc[n] for n in small_sharded])])
    w_full = dict(w_loc)
    for n, got in zip(BIG, gathered):
        w_full[n] = _full_from_shards(n, got)
    per_shard = [_unpack_flat(gathered[-1][s], [w_loc[n].shape for n in small_sharded]) for s in range(N_SHARD)]
    for k, n in enumerate(small_sharded):
        w_full[n] = jnp.concatenate([per_shard[s][k] for s in range(N_SHARD)], axis=1)

    loss_part, grad_x, g_full = local_step(x, positions, loss_target, w_full)
    loss = lax.psum(loss_part[0, 0], AXES)

    small_flat = jnp.concatenate([_small_pieces(n, g_full[n]) for n in SMALL], axis=1)
    pad = (-small_flat.shape[1]) % SMALL_MULTIPLE
    small_pieces = jnp.pad(small_flat, ((0, 0), (0, pad))).reshape(N_SHARD, -1, LANE)
    units = BIG + ['small']
    pieces = [_pieces(n, g_full[n]) for n in BIG] + [small_pieces]
    from_sibling = swap_halves(pieces)
    pair = [pair_add(p, r, core, "pair_add_" + n) for n, p, r in zip(units, pieces, from_sibling)]
    from_chips = scatter_to_chips(pair)
    reduced = share_halves([chip_add(p, "chip_add_" + n) for n, p in zip(units, from_chips)])

    out = {}
    for n, gred in zip(BIG, reduced):
        out[n] = (gred,) + tuple(adamw(w_loc[n], gred, m_loc[n], v_loc[n], "adamw_" + n))
    shapes = [w_loc[n].shape for n in SMALL]
    packs = [_pack_flat([d[n] for n in SMALL]) for d in (w_loc, m_loc, v_loc)]
    small_res = (reduced[-1],) + tuple(adamw(packs[0], reduced[-1], packs[1], packs[2], "adamw_small"))
    small_unpacked = [_unpack_flat(b, shapes) for b in small_res]
    for k, n in enumerate(SMALL):
        out[n] = tuple(u[k] for u in small_unpacked)
    return (loss, grad_x, *[out[n][0] for n in WEIGHTS], *[out[n][1] for n in WEIGHTS],
            *[out[n][2] for n in WEIGHTS], *[out[n][3] for n in WEIGHTS])
```

```python
import functools

import numpy as np
import jax
import jax.numpy as jnp
from jax import lax
from jax.experimental import pallas as pl
from jax.experimental.pallas import tpu as pltpu

f32 = jnp.float32
bf16 = jnp.bfloat16
HI = lax.Precision.HIGHEST
MID = lax.Precision.HIGH

D_MODEL = 1024
HEAD_DIM = 64
N_HEADS = 8
RWKV_COLS = 1792
RWKV_GN_EPS = 64e-5
SSM_STATE = 128
SSM_CHUNK = 128
L0_COLS = 3336
L0_PAD = 3456
L1_COLS = 1952
L1_PAD = 2048
MLA_ROPE = 32
ROPE_THETA = 10000.0
D_FF = 2816
DEPTH = 2
ALPHA = (2 * DEPTH) ** 0.25
ADAM_LR = 0.001
ADAM_B1 = 0.9
ADAM_B2 = 0.999
ADAM_EPS = 1e-08
ADAM_WD = 0.01
ADAM_STEP = 10
RWKV_CHUNK = 64
RWKV_HEADS_PER_STEP = 8
LANE = 128
SUBLANE = 8
VMEM_LIMIT = 56 * 1024 * 1024

WEIGHTS = ['l0_w_in', 'rwkv_mix', 'rwkv_w0', 'rwkv_w2', 'rwkv_a0', 'rwkv_a2', 'rwkv_g2', 'rwkv_k_k', 'rwkv_k_a',
           'rwkv_r_k', 'rwkv_ln_g', 'rwkv_ln_b', 'ssm_conv_w', 'ssm_conv_b', 'ssm_dt_bias', 'ssm_a_log', 'ssm_d',
           'ssm_norm_g', 'l0_w_out', 'l0_ln1_g', 'l0_ln1_b', 'ffn0_w_up', 'ffn0_conv_w', 'ffn0_conv_b',
           'ffn0_w_down', 'l0_ln2_g', 'l0_ln2_b', 'l1_w_in', 'mla_q_norm_g', 'mla_w_uq', 'mla_kv_norm_g',
           'mla_w_ukv', 'l1_w_out', 'l1_ln1_g', 'l1_ln1_b', 'ffn1_w_up', 'ffn1_conv_w', 'ffn1_conv_b',
           'ffn1_w_down', 'l1_ln2_g', 'l1_ln2_b']
COL_SHARDED = ['l0_w_in', 'rwkv_w2', 'rwkv_a2', 'rwkv_g2', 'ssm_conv_w', 'ffn0_w_up', 'ffn0_conv_w', 'l1_w_in',
               'mla_w_uq', 'mla_w_ukv', 'ffn1_w_up', 'ffn1_conv_w']
ROW_SHARDED = ['l0_w_out', 'ffn0_w_down', 'l1_w_out', 'ffn1_w_down']
BIG = ['l0_w_in', 'l0_w_out', 'ffn0_w_up', 'ffn0_w_down', 'l1_w_in', 'l1_w_out', 'ffn1_w_up', 'ffn1_w_down']
SMALL = [n for n in WEIGHTS if n not in BIG]
N_SHARD = 4


def _cparams(sem):
    return pltpu.CompilerParams(dimension_semantics=sem, vmem_limit_bytes=VMEM_LIMIT)


def _dg(a, b, ca, cb, prec=None):
    return lax.dot_general(a, b, (((ca,), (cb,)), ((), ())), precision=prec, preferred_element_type=f32)


def hdot(a, b):
    return _dg(a, b, 1, 0, HI)


def mdot(a, b):
    return _dg(a, b, 1, 0, MID)


def mdot_nt(a, b):
    return _dg(a, b, 1, 1, MID)


def mdot_tn(a, b):
    return _dg(a, b, 0, 0, MID)


def _b(x):
    return x.astype(bf16)


@jax.custom_vjp
def bdot(x, w):
    return _dg(_b(x), _b(w), 1, 0)


def _bdot_fwd(x, w):
    return bdot(x, w), (x, w)


def _bdot_bwd(res, g):
    x, w = res
    return _dg(_b(g), _b(w), 1, 1).astype(x.dtype), _dg(_b(x), _b(g), 0, 0).astype(w.dtype)


bdot.defvjp(_bdot_fwd, _bdot_bwd)


@jax.custom_vjp
def bdot_nt(x, y):
    return _dg(_b(x), _b(y), 1, 1)


def _bdot_nt_fwd(x, y):
    return bdot_nt(x, y), (x, y)


def _bdot_nt_bwd(res, g):
    x, y = res
    return _dg(_b(g), _b(y), 1, 0), _dg(_b(g), _b(x), 0, 0)


bdot_nt.defvjp(_bdot_nt_fwd, _bdot_nt_bwd)


@jax.custom_vjp
def bdot_tn(x, y):
    return _dg(_b(x), _b(y), 0, 0)


def _bdot_tn_fwd(x, y):
    return bdot_tn(x, y), (x, y)


def _bdot_tn_bwd(res, g):
    x, y = res
    return _dg(_b(y), _b(g), 1, 1), _dg(_b(x), _b(g), 1, 0)


bdot_tn.defvjp(_bdot_tn_fwd, _bdot_tn_bwd)


def _sigmoid(x):
    return 1.0 / (1.0 + jnp.exp(-x))


@jax.custom_vjp
def softplus(x):
    e = jnp.exp(-jnp.abs(x))
    u = 1.0 + e
    log1p = jnp.where(u == 1.0, e, jnp.log(u) * e / jnp.where(u == 1.0, 1.0, u - 1.0))
    return jnp.maximum(x, 0.0) + log1p


def _softplus_fwd(x):
    return softplus(x), x


def _softplus_bwd(x, g):
    return (g * _sigmoid(x),)


softplus.defvjp(_softplus_fwd, _softplus_bwd)


def silu(x):
    return x * _sigmoid(x)


def _shift_rows(x, k, up):
    if k == 0:
        return x
    t = x.shape[0]
    rows = lax.broadcasted_iota(jnp.int32, x.shape, 0)
    if up:
        return jnp.where(rows < t - k, pltpu.roll(x, t - k, 0), 0.0)
    return jnp.where(rows >= k, pltpu.roll(x, k, 0), 0.0)


@functools.partial(jax.custom_vjp, nondiff_argnums=(1,))
def shift_down(x, k):
    return _shift_rows(x, k, False)


def _shift_down_fwd(x, k):
    return _shift_rows(x, k, False), None


def _shift_down_bwd(k, _, g):
    return (_shift_rows(g, k, True),)


shift_down.defvjp(_shift_down_fwd, _shift_down_bwd)


@functools.partial(jax.custom_vjp, nondiff_argnums=(1,))
def lane_roll(x, s):
    return pltpu.roll(x, s % x.shape[1], 1)


def _lane_roll_fwd(x, s):
    return lane_roll(x, s), None


def _lane_roll_bwd(s, _, g):
    return (pltpu.roll(g, (-s) % g.shape[1], 1),)


lane_roll.defvjp(_lane_roll_fwd, _lane_roll_bwd)


def rot_half32(x):
    first = (lax.broadcasted_iota(jnp.int32, x.shape, 1) % MLA_ROPE) < (MLA_ROPE // 2)
    return jnp.where(first, -lane_roll(x, -(MLA_ROPE // 2)), lane_roll(x, MLA_ROPE // 2))


def _iota2(shape, axis):
    return lax.broadcasted_iota(jnp.int32, shape, axis)


class Op:
    def __init__(self, arr, block, imap, diff=True, acc=None, gshape=None, gimap=None, gdtype=f32):
        self.arr, self.block, self.imap, self.diff, self.acc = arr, tuple(block), imap, diff, acc
        self.gshape = tuple(arr.shape) if gshape is None else tuple(gshape)
        self.gimap = imap if gimap is None else gimap
        self.gdtype = gdtype


class Out:
    def __init__(self, shape, block, imap, dtype=f32):
        self.shape, self.block, self.imap, self.dtype = tuple(shape), tuple(block), imap, dtype


def block_fwd(fn, name, grid, ops, outs):
    n_in = len(ops)

    def body(*refs):
        vals = [r[...] for r in refs[:n_in]]
        res = fn(*vals)
        for r, v in zip(refs[n_in:], res):
            r[...] = v.astype(r.dtype)

    res = pl.pallas_call(
        body, name=name, grid=grid,
        in_specs=[pl.BlockSpec(o.block, o.imap) for o in ops],
        out_specs=[pl.BlockSpec(o.block, o.imap) for o in outs],
        out_shape=[jax.ShapeDtypeStruct(o.shape, o.dtype) for o in outs],
        compiler_params=_cparams(("arbitrary", "arbitrary")),
    )(*[o.arr for o in ops])
    return tuple(res)


def block_bwd(fn, name, grid, ops, outs, douts):
    n_in, n_out = len(ops), len(outs)
    dix = [k for k, o in enumerate(ops) if o.diff]

    def body(*refs):
        vals = [r[...] for r in refs[:n_in]]
        dvals = tuple(r[...] for r in refs[n_in:n_in + n_out])
        grefs = refs[n_in + n_out:]

        def f(*d):
            full = list(vals)
            for k, v in zip(dix, d):
                full[k] = v
            return tuple(fn(*full))

        _, vjp = jax.vjp(f, *[vals[k] for k in dix])
        grads = vjp(dvals)
        j, i = pl.program_id(0), pl.program_id(1)
        for k, gref, g in zip(dix, grefs, grads):
            acc = ops[k].acc
            if acc is None:
                gref[...] = g.astype(gref.dtype)
            else:
                first = (i == 0) if acc == 'i' else jnp.logical_and(i == 0, j == 0)

                @pl.when(first)
                def _():
                    gref[...] = g

                @pl.when(jnp.logical_not(first))
                def _():
                    gref[...] += g

    gspecs = [pl.BlockSpec(ops[k].block, ops[k].gimap) for k in dix]
    gshapes = [jax.ShapeDtypeStruct(ops[k].gshape, ops[k].gdtype) for k in dix]
    res = pl.pallas_call(
        body, name=name, grid=grid,
        in_specs=[pl.BlockSpec(o.block, o.imap) for o in ops] + [pl.BlockSpec(o.block, o.imap) for o in outs],
        out_specs=gspecs, out_shape=gshapes,
        compiler_params=_cparams(("arbitrary", "arbitrary")),
    )(*[o.arr for o in ops], *douts)
    return tuple(res)


def _rows(arr, tm, diff=True, gdtype=f32):
    return Op(arr, (tm, arr.shape[1]), lambda j, i: (i, 0), diff=diff, gdtype=gdtype)


def _param(arr, diff=True):
    return Op(arr, arr.shape, lambda j, i: (0,) * arr.ndim, diff=diff, acc='ij')


def _rows_out(n, c, tm, dtype=f32):
    return Out((n, c), (tm, c), lambda j, i: (i, 0), dtype)


def _cols(arr, t, tc, off=0, width=None, gdtype=f32):
    width = arr.shape[1] if width is None else width
    return Op(arr, (t, tc), lambda j, i: (i, j + off), gshape=(arr.shape[0], width), gimap=lambda j, i: (i, j),
              gdtype=gdtype)


def _cparam(arr, tc):
    return Op(arr, (arr.shape[0], tc), lambda j, i: (0, j), acc='i')


def _colblock(arr, tm, off, width, gdtype=f32):
    return Op(arr, (tm, width), lambda j, i: (i, off // width), gshape=(arr.shape[0], width),
              gimap=lambda j, i: (i, 0), gdtype=gdtype)


def _tile(n, cap):
    best = None
    for t in range(LANE, min(n, cap) + 1, LANE):
        if n % t == 0:
            best = t
    return n if best is None else best


def _rtile(rows, cols, cap_bytes=2 * 1024 * 1024):
    best = None
    for t in range(SUBLANE, rows + 1, SUBLANE):
        if rows % t == 0 and t * cols * 4 <= cap_bytes:
            best = t
    return rows if best is None else best


def mm(a, b, name, ta=False, add=None):
    m = a.shape[1] if ta else a.shape[0]
    kd = a.shape[0] if ta else a.shape[1]
    n = b.shape[1]
    tm, tn = _tile(m, 1024), _tile(n, 512)
    tk = kd if kd <= 2048 else _tile(kd, 1408)
    nk = kd // tk
    ca = 0 if ta else 1

    def body(*refs):
        if add is None:
            a_ref, b_ref, o_ref, acc = refs
        else:
            a_ref, b_ref, add_ref, o_ref, acc = refs
        k = pl.program_id(2)

        @pl.when(k == 0)
        def _():
            acc[...] = jnp.zeros_like(acc)

        acc[...] += _dg(_b(a_ref[...]), _b(b_ref[...]), ca, 0)

        @pl.when(k == nk - 1)
        def _():
            o_ref[...] = acc[...] if add is None else acc[...] + add_ref[...]

    a_spec = pl.BlockSpec((tk, tm), lambda i, j, k: (k, i)) if ta else pl.BlockSpec((tm, tk), lambda i, j, k: (i, k))
    b_spec = pl.BlockSpec((tk, tn), lambda i, j, k: (k, j))
    o_spec = pl.BlockSpec((tm, tn), lambda i, j, k: (i, j))
    args, specs = [a, b], [a_spec, b_spec]
    if add is not None:
        args.append(add)
        specs.append(o_spec)
    return pl.pallas_call(
        body, name=name, grid=(m // tm, n // tn, nk), in_specs=specs, out_specs=o_spec,
        out_shape=jax.ShapeDtypeStruct((m, n), f32), scratch_shapes=[pltpu.VMEM((tm, tn), f32)],
        compiler_params=_cparams(("parallel", "parallel", "arbitrary")),
    )(*args)


def f_ln(h, y, g, b):
    x = ALPHA * h + y
    mu = jnp.mean(x, axis=-1, keepdims=True)
    xc = x - mu
    var = jnp.mean(xc * xc, axis=-1, keepdims=True)
    return (xc * lax.rsqrt(var + 1e-5) * g + b,)


def f_shift_mix(p, mix):
    return (p + (shift_down(p, 1) - p) * mix,)


def f_rwkv_pre(k, wa_lo, g_lo, w0, w2, a0, a2, g2, k_k, k_a, gh):
    w_lo, a_lo = wa_lo[:, :64], wa_lo[:, 64:]
    log_w = -softplus(-(w0 + bdot(jnp.tanh(w_lo), w2))) - 0.5
    lw = -jnp.exp(log_w)
    a = _sigmoid(a0 + bdot(a_lo, a2))
    g = bdot(_sigmoid(g_lo), g2)
    kk = k * k_k
    kk = kk / jnp.maximum(jnp.sqrt(hdot(kk * kk, gh)), 1e-12)
    k2 = k * (1.0 + (a - 1.0) * k_a)
    return lw, k2, -kk, kk * a, g


def f_rwkv_post(y, r, k2, v, g, ln_g, ln_b, r_k, gh):
    mu = hdot(y, gh) * (1.0 / HEAD_DIM)
    yc = y - mu
    var = hdot(yc * yc, gh) * (1.0 / HEAD_DIM)
    yn = yc * lax.rsqrt(var + RWKV_GN_EPS) * ln_g + ln_b
    bonus = hdot(r * k2 * r_k, gh) * v
    return ((yn + bonus) * g,)


def f_conv4_silu(x, w0, w1, w2, w3, b):
    y = b + shift_down(x, 3) * w0 + shift_down(x, 2) * w1 + shift_down(x, 1) * w2 + x * w3
    return (silu(y),)


def f_ssm_post(y, z, norm_g, gg):
    u = y * silu(z)
    ms = hdot(u * u, gg) * (1.0 / 256.0)
    return (u * lax.rsqrt(ms + 1e-5) * norm_g,)


def f_ffn_act(gate, up, w0, w1, w2, b):
    gc = b + shift_down(gate, 2) * w0 + shift_down(gate, 1) * w1 + gate * w2
    return (silu(gc) * up,)


def _rms(x, g, eps=1e-6):
    return x * lax.rsqrt(jnp.mean(x * x, axis=-1, keepdims=True) + eps) * g


def f_mla_pre(c_q, c_kv, kpe, pos, q_g, w_qn, w_qr, kv_g, w_ukv, inv_q, inv_k):
    qn_in = _rms(c_q, q_g)
    q_nope = bdot(qn_in, w_qn)
    qr = bdot(qn_in, w_qr)
    kv = bdot(_rms(c_kv, kv_g), w_ukv)
    ang_q = pos * inv_q
    ang_k = pos * inv_k
    return (q_nope, qr * jnp.cos(ang_q) + rot_half32(qr) * jnp.sin(ang_q), kv,
            kpe * jnp.cos(ang_k) + rot_half32(kpe) * jnp.sin(ang_k))


def rwkv_chunk(s0, r, lw, k, v, a, b):
    hs = range(len(r))
    l = r[0].shape[0]
    ri, ci = _iota2((l, l), 0), _iota2((l, l), 1)
    strict, incl = ri > ci, ri >= ci
    tri, eye = incl.astype(f32), (ri == ci).astype(f32)
    last = (_iota2((l, 1), 0) == l - 1).astype(f32)
    c = [hdot(tri, lw[h]) for h in hs]
    at = [a[h] * jnp.exp(c[h] - lw[h]) for h in hs]
    wi = [jnp.exp(-c[h]) for h in hs]
    bt = [b[h] * wi[h] for h in hs]
    kt = [k[h] * wi[h] for h in hs]
    rt = [r[h] * jnp.exp(c[h]) for h in hs]
    nab = [jnp.where(strict, mdot_nt(at[h], bt[h]), 0.0) for h in hs]
    nak = [jnp.where(strict, mdot_nt(at[h], kt[h]), 0.0) for h in hs]
    g = [mdot_nt(at[h], s0[h]) + mdot(nak[h], v[h]) for h in hs]
    x = [eye + nab[h] for h in hs]
    p = [mdot(nab[h], nab[h]) for h in hs]
    steps = max(1, (l - 1).bit_length()) - 1
    for it in range(steps):
        x = [x[h] + mdot(p[h], x[h]) for h in hs]
        if it < steps - 1:
            p = [mdot(p[h], p[h]) for h in hs]
    u = [mdot(x[h], g[h]) for h in hs]
    mrb = [jnp.where(incl, mdot_nt(rt[h], bt[h]), 0.0) for h in hs]
    mrk = [jnp.where(incl, mdot_nt(rt[h], kt[h]), 0.0) for h in hs]
    y = [mdot_nt(rt[h], s0[h]) + mdot(mrb[h], u[h]) + mdot(mrk[h], v[h]) for h in hs]
    s1 = [(s0[h] + mdot_tn(u[h], bt[h]) + mdot_tn(v[h], kt[h])) * jnp.exp(jnp.sum(c[h] * last, axis=0, keepdims=True))
          for h in hs]
    return y, s1


def ssd_chunk(xs, bm, cm, dt_raw, s_in, dt_bias, a_log, d_skip, e_heads):
    l = xs.shape[0]
    ri, ci = _iota2((l, l), 0), _iota2((l, l), 1)
    incl = ri >= ci
    tri = incl.astype(f32)
    dt = softplus(dt_raw + dt_bias)
    a128 = dt * (-jnp.exp(a_log))
    lane0 = (_iota2((1, HEAD_DIM), 1) == 0).astype(f32)
    last = (_iota2((l, 1), 0) == l - 1).astype(f32)
    cb = [bdot_nt(cm[:, g * SSM_STATE:(g + 1) * SSM_STATE], bm[:, g * SSM_STATE:(g + 1) * SSM_STATE]) for g in range(2)]
    ys, s_out = [], []
    for h in range(N_HEADS):
        g = h // 4
        e = e_heads[h]
        x_h = xs[:, h * HEAD_DIM:(h + 1) * HEAD_DIM]
        dt_h = hdot(dt, e)
        ac = hdot(tri, hdot(a128, e))
        xd = x_h * dt_h
        col = jnp.broadcast_to(jnp.sum(ac * lane0, axis=1, keepdims=True), (l, l))
        decay = jnp.exp(jnp.where(incl, col - col.T, -1e30))
        y_diag = bdot(cb[g] * decay, xd)
        a_tot = jnp.sum(ac * last, axis=0, keepdims=True)
        b_g = bm[:, g * SSM_STATE:(g + 1) * SSM_STATE]
        c_g = cm[:, g * SSM_STATE:(g + 1) * SSM_STATE]
        s_new = jnp.exp(a_tot) * s_in[h] + bdot_tn(b_g, xd * jnp.exp(a_tot - ac))
        y_off = jnp.exp(ac) * bdot(c_g, s_in[h])
        ys.append(y_diag + y_off + x_h * hdot(jnp.broadcast_to(d_skip, (l, LANE)), e))
        s_out.append(s_new)
    return jnp.concatenate(ys, axis=1), s_out


def sb_block(q, k, v, q0):
    bq, t = q.shape[0], k.shape[0]
    scale = HEAD_DIM ** -0.5
    kb = LANE
    rows = q0 + _iota2((bq, kb), 0)
    upper = (_iota2((kb, kb), 0) > _iota2((kb, kb), 1)).astype(f32)
    run = jnp.zeros((bq, 1), f32)
    o = jnp.zeros((bq, HEAD_DIM), f32)
    for j in reversed(range(t // kb)):
        z = bdot_nt(q, k[j * kb:(j + 1) * kb]) * scale
        if (j + 1) * kb > q0:
            strict = (j * kb + _iota2((bq, kb), 1)) < rows
            lk = jnp.where(strict, -softplus(z), 0.0)
            log_att = z + lk + mdot(lk, upper) + run
            att = jnp.where(strict, jnp.exp(jnp.where(strict, log_att, 0.0)), 0.0)
        else:
            lk = -softplus(z)
            att = jnp.exp(z + lk + mdot(lk, upper) + run)
        o = o + bdot(att, v[j * kb:(j + 1) * kb])
        run = run + jnp.sum(lk, axis=1, keepdims=True)
    return o


def mla_block(qn, qp, kn, kp, v, q0):
    bq, t = qn.shape[0], kn.shape[0]
    scale = (HEAD_DIM + MLA_ROPE) ** -0.5
    s = (bdot_nt(qn, kn) + bdot_nt(qp, kp)) * scale
    causal = _iota2((bq, t), 1) <= q0 + _iota2((bq, t), 0)
    s = jnp.where(causal, s, -1e30)
    m = jnp.max(s, axis=-1, keepdims=True)
    p = jnp.where(causal, jnp.exp(s - m), 0.0)
    p = p / jnp.sum(p, axis=-1, keepdims=True)
    return bdot(p, v)


def _head(h):
    return slice(h * HEAD_DIM, (h + 1) * HEAD_DIM)


def _rwkv_specs(nc, rev):
    hp = RWKV_HEADS_PER_STEP
    w = hp * HEAD_DIM
    chunk = (lambda c: nc - 1 - c) if rev else (lambda c: c)
    tok = lambda off: pl.BlockSpec((RWKV_CHUNK, w), lambda b, g, c: (b * nc + chunk(c), off // w + g))
    st = pl.BlockSpec((1, hp, HEAD_DIM, HEAD_DIM), lambda b, g, c: ((b * (N_HEADS // hp) + g) * nc + chunk(c), 0, 0, 0))
    return tok, st


def rwkv_scan_fwd(ps, lw, k2, na, bb, nb, t):
    hp, nc = RWKV_HEADS_PER_STEP, t // RWKV_CHUNK
    ng = N_HEADS // hp
    tok, st = _rwkv_specs(nc, False)

    def body(r_ref, v_ref, lw_ref, k_ref, a_ref, b_ref, y_ref, s0_ref, s):
        @pl.when(pl.program_id(2) == 0)
        def _():
            s[...] = jnp.zeros_like(s)

        s0_ref[0] = s[...]
        heads = lambda ref: [ref[:, _head(h)] for h in range(hp)]
        y, s1 = rwkv_chunk([s[h] for h in range(hp)], heads(r_ref), heads(lw_ref), heads(k_ref), heads(v_ref),
                           heads(a_ref), heads(b_ref))
        for h in range(hp):
            y_ref[:, _head(h)] = y[h]
            s[h] = s1[h]

    return pl.pallas_call(
        body, name="rwkv_scan_fwd", grid=(nb, ng, nc),
        in_specs=[tok(0), tok(1024), tok(0), tok(0), tok(0), tok(0)], out_specs=[tok(0), st],
        out_shape=[jax.ShapeDtypeStruct((nb * t, N_HEADS * HEAD_DIM), f32),
                   jax.ShapeDtypeStruct((nb * ng * nc, hp, HEAD_DIM, HEAD_DIM), f32)],
        scratch_shapes=[pltpu.VMEM((hp, HEAD_DIM, HEAD_DIM), f32)],
        compiler_params=_cparams(("parallel", "parallel", "arbitrary")),
    )(ps, ps, lw, k2, na, bb)


def rwkv_scan_bwd(s0, ps, lw, k2, na, bb, dy, nb, t):
    hp, nc = RWKV_HEADS_PER_STEP, t // RWKV_CHUNK
    ng = N_HEADS // hp
    tok, st = _rwkv_specs(nc, True)

    def body(s0_ref, r_ref, v_ref, lw_ref, k_ref, a_ref, b_ref, dy_ref, dr, dlw, dk, dv, da, db, ds):
        @pl.when(pl.program_id(2) == 0)
        def _():
            ds[...] = jnp.zeros_like(ds)

        heads = lambda ref: [ref[:, _head(h)] for h in range(hp)]
        _, vjp = jax.vjp(rwkv_chunk, [s0_ref[0, h] for h in range(hp)], heads(r_ref), heads(lw_ref), heads(k_ref),
                         heads(v_ref), heads(a_ref), heads(b_ref))
        g = vjp((heads(dy_ref), [ds[h] for h in range(hp)]))
        for h in range(hp):
            ds[h] = g[0][h]
            for ref, val in zip((dr, dlw, dk, dv, da, db), g[1:]):
                ref[:, _head(h)] = val[h]

    return pl.pallas_call(
        body, name="rwkv_scan_bwd", grid=(nb, ng, nc),
        in_specs=[st, tok(0), tok(1024), tok(0), tok(0), tok(0), tok(0), tok(0)], out_specs=[tok(0)] * 6,
        out_shape=[jax.ShapeDtypeStruct((nb * t, N_HEADS * HEAD_DIM), f32)] * 6,
        scratch_shapes=[pltpu.VMEM((hp, HEAD_DIM, HEAD_DIM), f32)],
        compiler_params=_cparams(("parallel", "parallel", "arbitrary")),
    )(s0, ps, ps, lw, k2, na, bb, dy)


def _ssd_specs(nb, nch, rev):
    def row(b, c):
        return b * nch + (nch - 1 - c if rev else c)

    l = SSM_CHUNK
    xs = pl.BlockSpec((l, 512), lambda b, c: (row(b, c), 0))
    bm = pl.BlockSpec((l, 256), lambda b, c: (row(b, c), 2))
    cm = pl.BlockSpec((l, 256), lambda b, c: (row(b, c), 3))
    dt = pl.BlockSpec((l, LANE), lambda b, c: (row(b, c), (L0_PAD - LANE) // LANE))
    st = pl.BlockSpec((1, 1, N_HEADS, SSM_STATE, HEAD_DIM), lambda b, c: (b, (nch - 1 - c if rev else c), 0, 0, 0))
    par = pl.BlockSpec((1, LANE), lambda b, c: (0, 0))
    eh = pl.BlockSpec((N_HEADS, LANE, HEAD_DIM), lambda b, c: (0, 0, 0))
    return xs, bm, cm, dt, st, par, eh, row


def ssd_fwd(xbc_act, proj0, dt_bias, a_log, d_skip, e_heads, nb, t):
    nch = t // SSM_CHUNK
    n_tok = nb * t
    xs, bm, cm, dt, st, par, eh, row = _ssd_specs(nb, nch, False)

    def body(x_ref, b_ref, c_ref, dt_ref, db_ref, al_ref, dsk_ref, e_ref, y_ref, st_ref, s):
        @pl.when(pl.program_id(1) == 0)
        def _():
            s[...] = jnp.zeros_like(s)

        st_ref[0, 0] = s[...]
        y, s_out = ssd_chunk(x_ref[...], b_ref[...], c_ref[...], dt_ref[...], [s[h] for h in range(N_HEADS)],
                             db_ref[...], al_ref[...], dsk_ref[...], [e_ref[h] for h in range(N_HEADS)])
        y_ref[...] = y
        for h in range(N_HEADS):
            s[h] = s_out[h]

    return pl.pallas_call(
        body, name="ssd_fwd", grid=(nb, nch), in_specs=[xs, bm, cm, dt, par, par, par, eh],
        out_specs=[pl.BlockSpec((SSM_CHUNK, 512), lambda b, c: (row(b, c), 0)), st],
        out_shape=[jax.ShapeDtypeStruct((n_tok, 512), f32),
                   jax.ShapeDtypeStruct((nb, nch, N_HEADS, SSM_STATE, HEAD_DIM), f32)],
        scratch_shapes=[pltpu.VMEM((N_HEADS, SSM_STATE, HEAD_DIM), f32)],
        compiler_params=_cparams(("arbitrary", "arbitrary")),
    )(xbc_act, xbc_act, xbc_act, proj0, dt_bias, a_log, d_skip, e_heads)


def ssd_bwd(xbc_act, proj0, dt_bias, a_log, d_skip, e_heads, states, dy, nb, t):
    nch = t // SSM_CHUNK
    n_tok = nb * t
    xs, bm, cm, dt, st, par, eh, row = _ssd_specs(nb, nch, True)

    def body(x_ref, b_ref, c_ref, dt_ref, db_ref, al_ref, dsk_ref, e_ref, st_ref, dy_ref,
             dx_ref, dbm_ref, dcm_ref, ddt_ref, ddb_ref, dal_ref, ddsk_ref, ds):
        first = jnp.logical_and(pl.program_id(0) == 0, pl.program_id(1) == 0)

        @pl.when(pl.program_id(1) == 0)
        def _():
            ds[...] = jnp.zeros_like(ds)

        e_list = [e_ref[h] for h in range(N_HEADS)]

        def f(x, bmv, cmv, dtr, s_in, dbv, alv, dskv):
            return ssd_chunk(x, bmv, cmv, dtr, s_in, dbv, alv, dskv, e_list)

        _, vjp = jax.vjp(f, x_ref[...], b_ref[...], c_ref[...], dt_ref[...],
                         [st_ref[0, 0, h] for h in range(N_HEADS)], db_ref[...], al_ref[...], dsk_ref[...])
        g = vjp((dy_ref[...], [ds[h] for h in range(N_HEADS)]))
        dx_ref[...], dbm_ref[...], dcm_ref[...], ddt_ref[...] = g[0], g[1], g[2], g[3].astype(bf16)
        for h in range(N_HEADS):
            ds[h] = g[4][h]
        for ref, val in zip((ddb_ref, dal_ref, ddsk_ref), g[5:]):
            @pl.when(first)
            def _():
                ref[...] = val

            @pl.when(jnp.logical_not(first))
            def _():
                ref[...] += val

    rows_spec = lambda w: pl.BlockSpec((SSM_CHUNK, w), lambda b, c: (row(b, c), 0))
    return pl.pallas_call(
        body, name="ssd_bwd", grid=(nb, nch),
        in_specs=[xs, bm, cm, dt, par, par, par, eh, st, rows_spec(512)],
        out_specs=[rows_spec(512), rows_spec(256), rows_spec(256), rows_spec(LANE), par, par, par],
        out_shape=[jax.ShapeDtypeStruct((n_tok, 512), f32), jax.ShapeDtypeStruct((n_tok, 256), f32),
                   jax.ShapeDtypeStruct((n_tok, 256), f32), jax.ShapeDtypeStruct((n_tok, LANE), bf16)]
        + [jax.ShapeDtypeStruct((1, LANE), f32)] * 3,
        scratch_shapes=[pltpu.VMEM((N_HEADS, SSM_STATE, HEAD_DIM), f32)],
        compiler_params=_cparams(("arbitrary", "arbitrary")),
    )(xbc_act, xbc_act, xbc_act, proj0, dt_bias, a_log, d_skip, e_heads, states, dy)


ATT_BQ = 256
SB_HEADS_PER_STEP = 2
MLA_HEADS_PER_STEP = 4


def _sb_specs(t, bq, nq):
    w = SB_HEADS_PER_STEP * HEAD_DIM
    qs = lambda off: pl.BlockSpec((bq, w), lambda b, g, i: (b * nq + i, off // w + g))
    ks = lambda off: pl.BlockSpec((t, w), lambda b, g, i: (b, off // w + g))
    return qs, ks


def sb_fwd(proj1, nb, t):
    bq = min(ATT_BQ, t)
    nq = t // bq
    qs, ks = _sb_specs(t, bq, nq)

    def body(q_ref, k_ref, v_ref, o_ref):
        for qi in range(nq):
            @pl.when(pl.program_id(2) == qi)
            def _():
                nk = (qi + 1) * bq
                for h in range(SB_HEADS_PER_STEP):
                    sl = _head(h)
                    o_ref[:, sl] = sb_block(q_ref[:, sl], k_ref[0:nk, sl], v_ref[0:nk, sl], qi * bq).astype(bf16)

    return pl.pallas_call(
        body, name="sb_fwd", grid=(nb, N_HEADS // SB_HEADS_PER_STEP, nq), in_specs=[qs(0), ks(512), ks(1024)],
        out_specs=qs(0), out_shape=jax.ShapeDtypeStruct((nb * t, 512), bf16),
        compiler_params=_cparams(("parallel", "parallel", "arbitrary")),
    )(proj1, proj1, proj1)


def sb_bwd(proj1, do, nb, t):
    bq = min(ATT_BQ, t)
    nq = t // bq
    qs, ks = _sb_specs(t, bq, nq)

    def body(q_ref, k_ref, v_ref, do_ref, dq_ref, dk_ref, dv_ref):
        @pl.when(pl.program_id(2) == 0)
        def _():
            dk_ref[...] = jnp.zeros_like(dk_ref)
            dv_ref[...] = jnp.zeros_like(dv_ref)

        for qi in range(nq):
            @pl.when(pl.program_id(2) == qi)
            def _():
                nk = (qi + 1) * bq
                for h in range(SB_HEADS_PER_STEP):
                    sl = _head(h)
                    _, vjp = jax.vjp(lambda a, b, c: sb_block(a, b, c, qi * bq),
                                     q_ref[:, sl], k_ref[0:nk, sl], v_ref[0:nk, sl])
                    dq, dk, dv = vjp(do_ref[:, sl])
                    dq_ref[:, sl] = dq
                    dk_ref[0:nk, sl] += dk
                    dv_ref[0:nk, sl] += dv

    return pl.pallas_call(
        body, name="sb_bwd", grid=(nb, N_HEADS // SB_HEADS_PER_STEP, nq),
        in_specs=[qs(0), ks(512), ks(1024), qs(0)], out_specs=[qs(0), ks(0), ks(0)],
        out_shape=[jax.ShapeDtypeStruct((nb * t, 512), f32)] * 3,
        compiler_params=_cparams(("parallel", "parallel", "arbitrary")),
    )(proj1, proj1, proj1, do)


def _mla_specs(t, bq, nq):
    hp = MLA_HEADS_PER_STEP
    qn = pl.BlockSpec((bq, hp * HEAD_DIM), lambda b, g, i: (b * nq + i, g))
    qr = pl.BlockSpec((bq, hp * MLA_ROPE), lambda b, g, i: (b * nq + i, g))
    kv = pl.BlockSpec((t, hp * 2 * HEAD_DIM), lambda b, g, i: (b, g))
    kp = pl.BlockSpec((t, LANE), lambda b, g, i: (b, 0))
    return qn, qr, kv, kp


def _mla_heads(qn_ref, qr_ref, kv_ref, kp_ref, nk):
    out = []
    for h in range(MLA_HEADS_PER_STEP):
        out.append((qn_ref[:, _head(h)], qr_ref[:, h * MLA_ROPE:(h + 1) * MLA_ROPE], kv_ref[0:nk, _head(2 * h)],
                    kp_ref[0:nk, :MLA_ROPE], kv_ref[0:nk, _head(2 * h + 1)]))
    return out


def mla_fwd(q_nope, qr, kv, kpe, nb, t):
    bq = min(ATT_BQ, t)
    nq = t // bq
    sqn, sqr, skv, skp = _mla_specs(t, bq, nq)

    def body(qn_ref, qr_ref, kv_ref, kp_ref, o_ref):
        for qi in range(nq):
            @pl.when(pl.program_id(2) == qi)
            def _():
                for h, args in enumerate(_mla_heads(qn_ref, qr_ref, kv_ref, kp_ref, (qi + 1) * bq)):
                    o_ref[:, _head(h)] = mla_block(*args, qi * bq).astype(bf16)

    return pl.pallas_call(
        body, name="mla_fwd", grid=(nb, N_HEADS // MLA_HEADS_PER_STEP, nq), in_specs=[sqn, sqr, skv, skp],
        out_specs=sqn, out_shape=jax.ShapeDtypeStruct((nb * t, 512), bf16),
        compiler_params=_cparams(("parallel", "arbitrary", "arbitrary")),
    )(q_nope, qr, kv, kpe)


def mla_bwd(q_nope, qr, kv, kpe, do, nb, t):
    bq = min(ATT_BQ, t)
    nq = t // bq
    sqn, sqr, skv, skp = _mla_specs(t, bq, nq)

    def body(qn_ref, qr_ref, kv_ref, kp_ref, do_ref, dqn_ref, dqr_ref, dkv_ref, dkp_ref):
        first_q = pl.program_id(2) == 0

        @pl.when(first_q)
        def _():
            dkv_ref[...] = jnp.zeros_like(dkv_ref)

        @pl.when(jnp.logical_and(first_q, pl.program_id(1) == 0))
        def _():
            dkp_ref[...] = jnp.zeros_like(dkp_ref)

        for qi in range(nq):
            @pl.when(pl.program_id(2) == qi)
            def _():
                nk = (qi + 1) * bq
                for h, args in enumerate(_mla_heads(qn_ref, qr_ref, kv_ref, kp_ref, nk)):
                    _, vjp = jax.vjp(lambda a, b, c, d, e: mla_block(a, b, c, d, e, qi * bq), *args)
                    dqn, dqp, dkn, dkp, dv = vjp(do_ref[:, _head(h)])
                    dqn_ref[:, _head(h)] = dqn
                    dqr_ref[:, h * MLA_ROPE:(h + 1) * MLA_ROPE] = dqp
                    dkp_ref[0:nk, :MLA_ROPE] += dkp
                    dkv_ref[0:nk, _head(2 * h)] += dkn
                    dkv_ref[0:nk, _head(2 * h + 1)] += dv

    n = nb * t
    return pl.pallas_call(
        body, name="mla_bwd", grid=(nb, N_HEADS // MLA_HEADS_PER_STEP, nq),
        in_specs=[sqn, sqr, skv, skp, sqn], out_specs=[sqn, sqr, skv, skp],
        out_shape=[jax.ShapeDtypeStruct((n, 512), f32), jax.ShapeDtypeStruct((n, N_HEADS * MLA_ROPE), f32),
                   jax.ShapeDtypeStruct((n, 1024), f32), jax.ShapeDtypeStruct((n, LANE), f32)],
        compiler_params=_cparams(("arbitrary", "arbitrary", "arbitrary")),
    )(q_nope, qr, kv, kpe, do)


def loss_head(h, target):
    n, d = h.shape
    tm = _tile(n, 512)

    def body(h_ref, t_ref, l_ref, dh_ref):
        diff = h_ref[...] - t_ref[...]
        dh_ref[...] = diff * (1.0 / d)
        part = 0.5 * jnp.sum(jnp.sum(diff * diff, axis=1, keepdims=True) * (1.0 / d), axis=0, keepdims=True)

        @pl.when(pl.program_id(0) == 0)
        def _():
            l_ref[...] = jnp.zeros_like(l_ref)

        l_ref[...] += jnp.broadcast_to(part, l_ref.shape)

    spec = pl.BlockSpec((tm, d), lambda i: (i, 0))
    return pl.pallas_call(
        body, name="loss_head", grid=(n // tm,), in_specs=[spec, spec],
        out_specs=[pl.BlockSpec((8, LANE), lambda i: (0, 0)), spec],
        out_shape=[jax.ShapeDtypeStruct((8, LANE), f32), jax.ShapeDtypeStruct((n, d), f32)],
        compiler_params=_cparams(("arbitrary",)),
    )(h, target)


def _row(v):
    return v.reshape(1, -1)


def _pad_cols(a, n):
    return jnp.pad(a, ((0, 0), (0, n - a.shape[1])))


def _pad_row(v, n=LANE):
    return jnp.pad(v.reshape(1, -1), ((0, 0), (0, n - v.shape[0])))


def _group_matrix(width, group):
    idx = np.arange(width) // group
    return jnp.asarray((idx[:, None] == idx[None, :]).astype(np.float32))


def _head_expand():
    e = np.zeros((N_HEADS, LANE, HEAD_DIM), np.float32)
    for h in range(N_HEADS):
        e[h, h, :] = 1.0
    return jnp.asarray(e)


def _rope_freqs():
    inv = 1.0 / (ROPE_THETA ** (np.arange(0, MLA_ROPE, 2, dtype=np.float32) / MLA_ROPE))
    inv = np.tile(inv.astype(np.float32), 2)
    inv_q = np.tile(inv, N_HEADS).reshape(1, N_HEADS * MLA_ROPE)
    inv_k = np.zeros((1, LANE), np.float32)
    inv_k[0, :MLA_ROPE] = inv
    return jnp.asarray(inv_q), jnp.asarray(inv_k)


def _uq_split(w):
    w3 = w.reshape(w.shape[0], N_HEADS, HEAD_DIM + MLA_ROPE)
    return w3[:, :, :HEAD_DIM].reshape(-1, 512), w3[:, :, HEAD_DIM:].reshape(-1, N_HEADS * MLA_ROPE)


def _uq_merge(gn, gr):
    r = gn.shape[0]
    return jnp.concatenate([gn.reshape(r, N_HEADS, HEAD_DIM), gr.reshape(r, N_HEADS, MLA_ROPE)], axis=2).reshape(r, 768)


def local_step(x, positions, target, w):
    nb, t, d = x.shape
    n = nb * t
    tm = 256
    ni = n // tm
    tc = LANE
    h0 = x.reshape(n, d)
    tgt = target.reshape(n, d)
    pos = positions.reshape(n, 1).astype(f32)
    gh = _group_matrix(512, HEAD_DIM)
    gg = _group_matrix(512, 256)
    e_heads = _head_expand()
    inv_q, inv_k = _rope_freqs()
    g = {}

    def ln_stage(h, y, gname, bname):
        ops = [_rows(h, tm), _rows(y, tm, gdtype=bf16), _param(_row(w[gname])), _param(_row(w[bname]))]
        return ops, [_rows_out(n, d, tm)]

    def ln_fwd(name, ops):
        return block_fwd(lambda *a: f_ln(*a) * 2, name, (1, ni), ops, [_rows_out(n, d, tm), _rows_out(n, d, tm, bf16)])

    def ffn_act_stage(u, cw, cb):
        nj = D_FF // tc
        ops = [_cols(u, t, tc, 0, D_FF, bf16), _cols(u, t, tc, nj, D_FF, bf16)] \
            + [_cparam(cw[i:i + 1], tc) for i in range(3)] + [_cparam(_row(cb), tc)]
        return ops, [Out((n, D_FF), (t, tc), lambda j, i: (i, j), bf16)], (nj, nb)

    w_in0 = _pad_cols(w['l0_w_in'], L0_PAD)
    w_out0 = w['l0_w_out']
    h0b = h0.astype(bf16)
    proj0 = mm(h0b, w_in0, "l0_proj")

    shift_ops = [_cols(proj0, t, tc, 0, RWKV_COLS, bf16), _cparam(_row(w['rwkv_mix']), tc)]
    shift_outs = [Out((n, RWKV_COLS), (t, tc), lambda j, i: (i, j))]
    shift_grid = (RWKV_COLS // tc, nb)
    (ps,) = block_fwd(f_shift_mix, "rwkv_shift", shift_grid, shift_ops, shift_outs)

    pre_ops = [_colblock(ps, tm, 512, 512), _colblock(ps, tm, 1536, 128), _colblock(ps, tm, 1664, 128),
               _param(_row(w['rwkv_w0'])), _param(w['rwkv_w2']), _param(_row(w['rwkv_a0'])), _param(w['rwkv_a2']),
               _param(w['rwkv_g2']), _param(_row(w['rwkv_k_k'])), _param(_row(w['rwkv_k_a'])), _param(gh, diff=False)]
    pre_outs = [_rows_out(n, 512, tm) for _ in range(5)]
    lw, k2, na, bb, gate_r = block_fwd(f_rwkv_pre, "rwkv_pre", (1, ni), pre_ops, pre_outs)
    y_tok, s0_saved = rwkv_scan_fwd(ps, lw, k2, na, bb, nb, t)

    post_ops = [_rows(y_tok, tm), _colblock(ps, tm, 0, 512), _rows(k2, tm), _colblock(ps, tm, 1024, 512),
                _rows(gate_r, tm), _param(_row(w['rwkv_ln_g'])), _param(_row(w['rwkv_ln_b'])),
                _param(w['rwkv_r_k'].reshape(1, 512)), _param(gh, diff=False)]
    post_outs = [_rows_out(n, 512, tm, bf16)]
    (y_a,) = block_fwd(f_rwkv_post, "rwkv_post", (1, ni), post_ops, post_outs)

    xbc_off = (RWKV_COLS + 512) // tc
    conv_ops = [_cols(proj0, t, tc, xbc_off, 1024, bf16)] + [_cparam(w['ssm_conv_w'][i:i + 1], tc) for i in range(4)] \
        + [_cparam(_row(w['ssm_conv_b']), tc)]
    conv_outs = [Out((n, 1024), (t, tc), lambda j, i: (i, j))]
    conv_grid = (1024 // tc, nb)
    (xbc_act,) = block_fwd(f_conv4_silu, "ssm_conv", conv_grid, conv_ops, conv_outs)

    dt_bias, a_log, d_skip = _pad_row(w['ssm_dt_bias']), _pad_row(w['ssm_a_log']), _pad_row(w['ssm_d'])
    y_ssd, ssd_states = ssd_fwd(xbc_act, proj0, dt_bias, a_log, d_skip, e_heads, nb, t)

    z_tok = proj0[:, RWKV_COLS:RWKV_COLS + 512]
    spost_ops = [_rows(y_ssd, tm), _rows(z_tok, tm, gdtype=bf16), _param(_row(w['ssm_norm_g'])), _param(gg, diff=False)]
    spost_outs = [_rows_out(n, 512, tm, bf16)]
    (y_b,) = block_fwd(f_ssm_post, "ssm_post", (1, ni), spost_ops, spost_outs)

    mixed0 = mm(y_b, w_out0[512:], "l0_out_b", add=mm(y_a, w_out0[:512], "l0_out_a"))
    ln1_ops, ln_outs = ln_stage(h0, mixed0, 'l0_ln1_g', 'l0_ln1_b')
    h1, h1b = ln_fwd("l0_ln1", ln1_ops)

    u0 = mm(h1b, w['ffn0_w_up'], "ffn0_up")
    act0_ops, act_outs, act_grid = ffn_act_stage(u0, w['ffn0_conv_w'], w['ffn0_conv_b'])
    (act0,) = block_fwd(f_ffn_act, "ffn0_act", act_grid, act0_ops, act_outs)
    f0 = mm(act0, w['ffn0_w_down'], "ffn0_down")
    ln2_ops, _ = ln_stage(h1, f0, 'l0_ln2_g', 'l0_ln2_b')
    h2, h2b = ln_fwd("l0_ln2", ln2_ops)

    w_in1 = _pad_cols(w['l1_w_in'], L1_PAD)
    w_out1 = w['l1_w_out']
    proj1 = mm(h2b, w_in1, "l1_proj")
    w_qn, w_qr = _uq_split(w['mla_w_uq'])
    mpre_ops = [_colblock(proj1, tm, 1536, 256, bf16), _colblock(proj1, tm, 1792, 128, bf16),
                _colblock(proj1, tm, 1920, 128, bf16),
                Op(pos, (tm, 1), lambda j, i: (i, 0), diff=False),
                _param(_row(w['mla_q_norm_g'])), _param(w_qn), _param(w_qr),
                _param(_row(w['mla_kv_norm_g'])), _param(w['mla_w_ukv']), _param(inv_q, diff=False),
                _param(inv_k, diff=False)]
    mpre_outs = [_rows_out(n, 512, tm), _rows_out(n, N_HEADS * MLA_ROPE, tm), _rows_out(n, 1024, tm),
                 _rows_out(n, LANE, tm)]
    q_nope, q_rope, kv, kpe = block_fwd(f_mla_pre, "mla_pre", (1, ni), mpre_ops, mpre_outs)
    o_sb = sb_fwd(proj1, nb, t)
    o_mla = mla_fwd(q_nope, q_rope, kv, kpe, nb, t)

    mixed1 = mm(o_mla, w_out1[512:], "l1_out_b", add=mm(o_sb, w_out1[:512], "l1_out_a"))
    ln3_ops, _ = ln_stage(h2, mixed1, 'l1_ln1_g', 'l1_ln1_b')
    h3, h3b = ln_fwd("l1_ln1", ln3_ops)
    u1 = mm(h3b, w['ffn1_w_up'], "ffn1_up")
    act1_ops, _, _ = ffn_act_stage(u1, w['ffn1_conv_w'], w['ffn1_conv_b'])
    (act1,) = block_fwd(f_ffn_act, "ffn1_act", act_grid, act1_ops, act_outs)
    f1 = mm(act1, w['ffn1_w_down'], "ffn1_down")
    ln4_ops, _ = ln_stage(h3, f1, 'l1_ln2_g', 'l1_ln2_b')
    (h4,) = block_fwd(f_ln, "l1_ln2", (1, ni), ln4_ops, ln_outs)

    loss_part, dh4 = loss_head(h4, tgt)

    def vec(a_):
        return a_.reshape(-1)

    def ffn_bwd(tag, dh_out, ln_ops, act_ops, h_in, act, w_up, w_down, names):
        dh_res, df, gg_, gb_ = block_bwd(f_ln, tag + "_ln2_bwd", (1, ni), ln_ops, ln_outs, [dh_out])
        g[names[4]], g[names[5]] = vec(gg_), vec(gb_)
        g[names[3]] = mm(act, df, tag + "_down_dw", ta=True)
        dact = mm(df, w_down.T, tag + "_down_dx")
        dgate, dup, dw0, dw1, dw2, dcb = block_bwd(f_ffn_act, tag + "_act_bwd", act_grid, act_ops, act_outs, [dact])
        g[names[1]] = jnp.concatenate([dw0, dw1, dw2], axis=0)
        g[names[2]] = vec(dcb)
        g[names[0]] = jnp.concatenate([mm(h_in, dgate, tag + "_gate_dw", ta=True),
                                       mm(h_in, dup, tag + "_upv_dw", ta=True)], axis=1)
        w_up_t = w_up.T
        dh = mm(dgate, w_up_t[:D_FF], tag + "_gate_dx", add=dh_res)
        return mm(dup, w_up_t[D_FF:], tag + "_upv_dx", add=dh)

    def out_bwd(tag, dmixed, y_first, y_second, w_out, name):
        g[name] = jnp.concatenate([mm(y_first, dmixed, tag + "_a_dw", ta=True),
                                   mm(y_second, dmixed, tag + "_b_dw", ta=True)], axis=0)
        w_t = w_out.T
        return mm(dmixed, w_t[:, :512], tag + "_a_dx"), mm(dmixed, w_t[:, 512:], tag + "_b_dx")

    dh3 = ffn_bwd("ffn1", dh4, ln4_ops, act1_ops, h3b, act1, w['ffn1_w_up'], w['ffn1_w_down'],
                  ['ffn1_w_up', 'ffn1_conv_w', 'ffn1_conv_b', 'ffn1_w_down', 'l1_ln2_g', 'l1_ln2_b'])

    dh2_res, dmixed1, g3g, g3b = block_bwd(f_ln, "l1_ln1_bwd", (1, ni), ln3_ops, ln_outs, [dh3])
    g['l1_ln1_g'], g['l1_ln1_b'] = vec(g3g), vec(g3b)
    do_sb, do_mla = out_bwd("l1_out", dmixed1, o_sb, o_mla, w_out1, 'l1_w_out')

    dq_nope, dq_rope, dkv, dkpe = mla_bwd(q_nope, q_rope, kv, kpe, do_mla, nb, t)
    dsb_q, dsb_k, dsb_v = sb_bwd(proj1, do_sb, nb, t)
    (dc_q, dc_kv, dkpe_raw, gqg, gwqn, gwqr, gkvg, g['mla_w_ukv']) = block_bwd(
        f_mla_pre, "mla_pre_bwd", (1, ni), mpre_ops, mpre_outs, [dq_nope, dq_rope, dkv, dkpe])
    g['mla_q_norm_g'], g['mla_kv_norm_g'] = vec(gqg), vec(gkvg)
    g['mla_w_uq'] = _uq_merge(gwqn, gwqr)
    dproj1 = jnp.concatenate([dsb_q.astype(bf16), dsb_k.astype(bf16), dsb_v.astype(bf16), dc_q, dc_kv, dkpe_raw],
                             axis=1)
    g['l1_w_in'] = mm(h2b, dproj1, "l1_proj_dw", ta=True)[:, :L1_COLS]
    dh2 = mm(dproj1, w_in1.T, "l1_proj_dx", add=dh2_res)

    dh1 = ffn_bwd("ffn0", dh2, ln2_ops, act0_ops, h1b, act0, w['ffn0_w_up'], w['ffn0_w_down'],
                  ['ffn0_w_up', 'ffn0_conv_w', 'ffn0_conv_b', 'ffn0_w_down', 'l0_ln2_g', 'l0_ln2_b'])

    dh0_res, dmixed0, g1g, g1b = block_bwd(f_ln, "l0_ln1_bwd", (1, ni), ln1_ops, ln_outs, [dh1])
    g['l0_ln1_g'], g['l0_ln1_b'] = vec(g1g), vec(g1b)
    dy_a, dy_b = out_bwd("l0_out", dmixed0, y_a, y_b, w_out0, 'l0_w_out')

    dy_ssd, dz, gng = block_bwd(f_ssm_post, "ssm_post_bwd", (1, ni), spost_ops, spost_outs, [dy_b])
    g['ssm_norm_g'] = vec(gng)
    dxs, dbm, dcm, ddt_raw, gdb, gal, gdsk = ssd_bwd(xbc_act, proj0, dt_bias, a_log, d_skip, e_heads, ssd_states,
                                                     dy_ssd, nb, t)
    g['ssm_dt_bias'], g['ssm_a_log'], g['ssm_d'] = gdb[0, :8], gal[0, :8], gdsk[0, :8]
    dxbc_act = jnp.concatenate([dxs, dbm, dcm], axis=1)
    dxbc, cw0, cw1, cw2, cw3, gcb = block_bwd(f_conv4_silu, "ssm_conv_bwd", conv_grid, conv_ops, conv_outs, [dxbc_act])
    g['ssm_conv_w'] = jnp.concatenate([cw0, cw1, cw2, cw3], axis=0)
    g['ssm_conv_b'] = vec(gcb)

    dy_tok, dr_post, dk2_post, dv_post, dgate, glg, glb, grk = block_bwd(
        f_rwkv_post, "rwkv_post_bwd", (1, ni), post_ops, post_outs, [dy_a])
    g['rwkv_ln_g'], g['rwkv_ln_b'], g['rwkv_r_k'] = vec(glg), vec(glb), grk.reshape(N_HEADS, HEAD_DIM)
    dr, dlw, dk2, dv, dna, dbb = rwkv_scan_bwd(s0_saved, ps, lw, k2, na, bb, dy_tok, nb, t)
    (dk_pre, dwa_lo, dg_lo, gw0, g['rwkv_w2'], ga0, g['rwkv_a2'], g['rwkv_g2'], gkk, gka) = block_bwd(
        f_rwkv_pre, "rwkv_pre_bwd", (1, ni), pre_ops, pre_outs, [dlw, dk2 + dk2_post, dna, dbb, dgate])
    g['rwkv_w0'], g['rwkv_a0'], g['rwkv_k_k'], g['rwkv_k_a'] = vec(gw0), vec(ga0), vec(gkk), vec(gka)
    dps = jnp.concatenate([dr + dr_post, dk_pre, dv + dv_post, dwa_lo, dg_lo], axis=1)
    dp_rwkv, gmix = block_bwd(f_shift_mix, "rwkv_shift_bwd", shift_grid, shift_ops, shift_outs, [dps])
    g['rwkv_mix'] = vec(gmix)

    dproj0 = jnp.concatenate([dp_rwkv, dz, dxbc, ddt_raw], axis=1)
    g['l0_w_in'] = mm(h0b, dproj0, "l0_proj_dw", ta=True)[:, :L0_COLS]
    grad_x = mm(dproj0, w_in0.T, "l0_proj_dx", add=dh0_res)
    return loss_part, grad_x.reshape(nb, t, d), g


MESH = pl.DeviceIdType.MESH
ANY = pl.BlockSpec(memory_space=pl.ANY)
AXES = ("x", "y", "c")


def _place():
    x, y, c = lax.axis_index("x"), lax.axis_index("y"), lax.axis_index("c")
    chips = [(1 - x, y), (x, 1 - y), (1 - x, 1 - y)]
    return x, y, c, chips


def _dma_sems(n):
    return pltpu.SemaphoreType.DMA((n,))


def gather_shards(shards):
    n = len(shards)

    def body(*refs):
        ins, outs = refs[:n], refs[n:2 * n]
        send_sems, recv_sems, local_sems = refs[2 * n:]
        x, y, c, chips = _place()
        me = 2 * x + y
        mine = [pltpu.make_async_copy(a, o.at[me], local_sems.at[k]) for k, (a, o) in enumerate(zip(ins, outs))]
        for cp in mine:
            cp.start()
        sends = [pltpu.make_async_remote_copy(
            src_ref=a, dst_ref=o.at[me], send_sem=send_sems.at[3 * k + j], recv_sem=recv_sems.at[3 * k + j],
            device_id=(cx, cy, c), device_id_type=MESH)
            for k, (a, o) in enumerate(zip(ins, outs)) for j, (cx, cy) in enumerate(chips)]
        for cp in sends:
            cp.start()
        for k, (a, o) in enumerate(zip(ins, outs)):
            for j, (cx, cy) in enumerate(chips):
                pltpu.make_async_remote_copy(
                    src_ref=a, dst_ref=o.at[2 * cx + cy], send_sem=send_sems.at[3 * k + j],
                    recv_sem=recv_sems.at[3 * k + j], device_id=(cx, cy, c), device_id_type=MESH).wait_recv()
        for cp in sends:
            cp.wait_send()
        for cp in mine:
            cp.wait()

    return pl.pallas_call(
        body, name="gather_shards", in_specs=[ANY] * n, out_specs=[ANY] * n,
        out_shape=[jax.ShapeDtypeStruct((N_SHARD,) + a.shape, a.dtype) for a in shards],
        scratch_shapes=[_dma_sems(3 * n), _dma_sems(3 * n), _dma_sems(n)],
    )(*shards)


def swap_halves(pieces):
    n = len(pieces)

    def body(*refs):
        ins, outs = refs[:n], refs[n:2 * n]
        send_sems, recv_sems = refs[2 * n:]
        x, y, c, _ = _place()
        cps = []
        for k, (a, o) in enumerate(zip(ins, outs)):
            h = a.shape[1] // 2
            cps.append(pltpu.make_async_remote_copy(
                src_ref=a.at[:, pl.ds((1 - c) * h, h), :], dst_ref=o, send_sem=send_sems.at[k],
                recv_sem=recv_sems.at[k], device_id=(x, y, 1 - c), device_id_type=MESH))
        for cp in cps:
            cp.start()
        for cp in cps:
            cp.wait()

    return pl.pallas_call(
        body, name="swap_halves", in_specs=[ANY] * n, out_specs=[ANY] * n,
        out_shape=[jax.ShapeDtypeStruct((a.shape[0], a.shape[1] // 2, a.shape[2]), a.dtype) for a in pieces],
        scratch_shapes=[_dma_sems(n), _dma_sems(n)],
    )(*pieces)


def scatter_to_chips(parts):
    n = len(parts)

    def body(*refs):
        ins, outs = refs[:n], refs[n:2 * n]
        send_sems, recv_sems, local_sems = refs[2 * n:]
        x, y, c, chips = _place()
        me = 2 * x + y
        mine = [pltpu.make_async_copy(a.at[me], o.at[me], local_sems.at[k]) for k, (a, o) in enumerate(zip(ins, outs))]
        for cp in mine:
            cp.start()
        sends = [pltpu.make_async_remote_copy(
            src_ref=a.at[2 * cx + cy], dst_ref=o.at[me], send_sem=send_sems.at[3 * k + j],
            recv_sem=recv_sems.at[3 * k + j], device_id=(cx, cy, c), device_id_type=MESH)
            for k, (a, o) in enumerate(zip(ins, outs)) for j, (cx, cy) in enumerate(chips)]
        for cp in sends:
            cp.start()
        for k, (a, o) in enumerate(zip(ins, outs)):
            for j, (cx, cy) in enumerate(chips):
                pltpu.make_async_remote_copy(
                    src_ref=a.at[me], dst_ref=o.at[2 * cx + cy], send_sem=send_sems.at[3 * k + j],
                    recv_sem=recv_sems.at[3 * k + j], device_id=(cx, cy, c), device_id_type=MESH).wait_recv()
        for cp in sends:
            cp.wait_send()
        for cp in mine:
            cp.wait()

    return pl.pallas_call(
        body, name="scatter_to_chips", in_specs=[ANY] * n, out_specs=[ANY] * n,
        out_shape=[jax.ShapeDtypeStruct(a.shape, a.dtype) for a in parts],
        scratch_shapes=[_dma_sems(3 * n), _dma_sems(3 * n), _dma_sems(n)],
    )(*parts)


def share_halves(bufs):
    n = len(bufs)

    def body(*refs):
        ins, outs = refs[:n], refs[n:2 * n]
        send_sems, recv_sems = refs[2 * n:]
        x, y, c, _ = _place()
        cps = [pltpu.make_async_remote_copy(
            src_ref=a.at[c], dst_ref=o.at[c], send_sem=send_sems.at[k], recv_sem=recv_sems.at[k],
            device_id=(x, y, 1 - c), device_id_type=MESH) for k, (a, o) in enumerate(zip(ins, outs))]
        for cp in cps:
            cp.start()
        for k, (a, o) in enumerate(zip(ins, outs)):
            cps[k].wait_send()
            pltpu.make_async_remote_copy(
                src_ref=a.at[c], dst_ref=o.at[1 - c], send_sem=send_sems.at[k], recv_sem=recv_sems.at[k],
                device_id=(x, y, 1 - c), device_id_type=MESH).wait_recv()

    return pl.pallas_call(
        body, name="share_halves", in_specs=[ANY] * n, out_specs=[ANY] * n,
        out_shape=[jax.ShapeDtypeStruct(a.shape, a.dtype) for a in bufs],
        input_output_aliases={k: k for k in range(n)},
        scratch_shapes=[_dma_sems(n), _dma_sems(n)],
    )(*bufs)


def pair_add(piece, recv, core, name):
    _, r, cdim = piece.shape
    h = r // 2
    tr = _rtile(h, cdim)
    nt = h // tr

    def body(c_ref, a_ref, b_ref, o_ref):
        o_ref[...] = a_ref[...] + b_ref[...]

    spec = pl.BlockSpec((1, tr, cdim), lambda p, i, c_ref: (p, i, 0))
    return pl.pallas_call(
        body, name=name,
        grid_spec=pltpu.PrefetchScalarGridSpec(
            num_scalar_prefetch=1, grid=(N_SHARD, nt),
            in_specs=[pl.BlockSpec((1, tr, cdim), lambda p, i, c_ref: (p, c_ref[0] * nt + i, 0)), spec],
            out_specs=spec),
        out_shape=jax.ShapeDtypeStruct((N_SHARD, h, cdim), f32),
        compiler_params=_cparams(("parallel", "parallel")),
    )(core, piece, recv)


def chip_add(parts, core, name):
    _, h, cdim = parts.shape
    tr = _rtile(h, cdim, 1024 * 1024)

    def body(c_ref, p_ref, o_ref):
        o_ref[0] = ((p_ref[0] + p_ref[1]) + p_ref[2]) + p_ref[3]

    return pl.pallas_call(
        body, name=name,
        grid_spec=pltpu.PrefetchScalarGridSpec(
            num_scalar_prefetch=1, grid=(h // tr,),
            in_specs=[pl.BlockSpec((N_SHARD, tr, cdim), lambda i, c_ref: (0, i, 0))],
            out_specs=pl.BlockSpec((1, tr, cdim), lambda i, c_ref: (c_ref[0], i, 0))),
        out_shape=jax.ShapeDtypeStruct((2, h, cdim), f32), compiler_params=_cparams(("parallel",)),
    )(core, parts)


def adamw(w, g, m, v, name):
    rows, cdim = w.shape
    tr = _rtile(rows, cdim, 1024 * 1024)

    def body(w_ref, g_ref, m_ref, v_ref, d_ref, nm_ref, nv_ref):
        gv = g_ref[...]
        m_new = ADAM_B1 * m_ref[...] + (1.0 - ADAM_B1) * gv
        v_new = ADAM_B2 * v_ref[...] + (1.0 - ADAM_B2) * jnp.square(gv)
        m_hat = m_new / (1.0 - ADAM_B1 ** ADAM_STEP)
        v_hat = v_new / (1.0 - ADAM_B2 ** ADAM_STEP)
        d_ref[...] = -ADAM_LR * (m_hat / (jnp.sqrt(v_hat) + ADAM_EPS) + ADAM_WD * w_ref[...])
        nm_ref[...] = m_new
        nv_ref[...] = v_new

    spec = pl.BlockSpec((tr, cdim), lambda i: (i, 0))
    return pl.pallas_call(body, name=name, grid=(rows // tr,), in_specs=[spec] * 4, out_specs=[spec] * 3,
                          out_shape=[jax.ShapeDtypeStruct(w.shape, f32)] * 3,
                          compiler_params=_cparams(("parallel",)))(w, g, m, v)


SMALL_MULTIPLE = 16 * LANE


def _pack_flat(parts, multiple=SMALL_MULTIPLE):
    flat = jnp.concatenate([p.reshape(-1) for p in parts])
    pad = (-flat.shape[0]) % multiple
    return jnp.pad(flat, (0, pad)).reshape(-1, LANE)


def _unpack_flat(buf, shapes):
    flat = buf.reshape(-1)
    out, off = [], 0
    for s in shapes:
        cnt = int(np.prod(s))
        out.append(flat[off:off + cnt].reshape(s))
        off += cnt
    return out


def _full_from_shards(name, gathered):
    if name in COL_SHARDED:
        return jnp.concatenate([gathered[s] for s in range(N_SHARD)], axis=1)
    return gathered.reshape(-1, gathered.shape[2])


def _pieces(name, grad):
    if name in COL_SHARDED:
        r, cdim = grad.shape
        return grad.reshape(r, N_SHARD, cdim // N_SHARD).transpose(1, 0, 2)
    return grad.reshape(N_SHARD, grad.shape[0] // N_SHARD, grad.shape[1])


def _small_pieces(name, grad):
    if name in COL_SHARDED or name in ROW_SHARDED:
        return _pieces(name, grad).reshape(N_SHARD, -1)
    return jnp.broadcast_to(grad.reshape(1, -1), (N_SHARD, grad.size))


def kernel(x, positions, l0_w_in, rwkv_mix, rwkv_w0, rwkv_w2, rwkv_a0, rwkv_a2, rwkv_g2, rwkv_k_k, rwkv_k_a, rwkv_r_k, rwkv_ln_g, rwkv_ln_b, ssm_conv_w, ssm_conv_b, ssm_dt_bias, ssm_a_log, ssm_d, ssm_norm_g, l0_w_out, l0_ln1_g, l0_ln1_b, ffn0_w_up, ffn0_conv_w, ffn0_conv_b, ffn0_w_down, l0_ln2_g, l0_ln2_b, l1_w_in, mla_q_norm_g, mla_w_uq, mla_kv_norm_g, mla_w_ukv, l1_w_out, l1_ln1_g, l1_ln1_b, ffn1_w_up, ffn1_conv_w, ffn1_conv_b, ffn1_w_down, l1_ln2_g, l1_ln2_b, loss_target, m_l0_w_in, m_rwkv_mix, m_rwkv_w0, m_rwkv_w2, m_rwkv_a0, m_rwkv_a2, m_rwkv_g2, m_rwkv_k_k, m_rwkv_k_a, m_rwkv_r_k, m_rwkv_ln_g, m_rwkv_ln_b, m_ssm_conv_w, m_ssm_conv_b, m_ssm_dt_bias, m_ssm_a_log, m_ssm_d, m_ssm_norm_g, m_l0_w_out, m_l0_ln1_g, m_l0_ln1_b, m_ffn0_w_up, m_ffn0_conv_w, m_ffn0_conv_b, m_ffn0_w_down, m_l0_ln2_g, m_l0_ln2_b, m_l1_w_in, m_mla_q_norm_g, m_mla_w_uq, m_mla_kv_norm_g, m_mla_w_ukv, m_l1_w_out, m_l1_ln1_g, m_l1_ln1_b, m_ffn1_w_up, m_ffn1_conv_w, m_ffn1_conv_b, m_ffn1_w_down, m_l1_ln2_g, m_l1_ln2_b, v_l0_w_in, v_rwkv_mix, v_rwkv_w0, v_rwkv_w2, v_rwkv_a0, v_rwkv_a2, v_rwkv_g2, v_rwkv_k_k, v_rwkv_k_a, v_rwkv_r_k, v_rwkv_ln_g, v_rwkv_ln_b, v_ssm_conv_w, v_ssm_conv_b, v_ssm_dt_bias, v_ssm_a_log, v_ssm_d, v_ssm_norm_g, v_l0_w_out, v_l0_ln1_g, v_l0_ln1_b, v_ffn0_w_up, v_ffn0_conv_w, v_ffn0_conv_b, v_ffn0_w_down, v_l0_ln2_g, v_l0_ln2_b, v_l1_w_in, v_mla_q_norm_g, v_mla_w_uq, v_mla_kv_norm_g, v_mla_w_ukv, v_l1_w_out, v_l1_ln1_g, v_l1_ln1_b, v_ffn1_w_up, v_ffn1_conv_w, v_ffn1_conv_b, v_ffn1_w_down, v_l1_ln2_g, v_l1_ln2_b):
    args = locals()
    w_loc = {n: args[n] for n in WEIGHTS}
    m_loc = {n: args["m_" + n] for n in WEIGHTS}
    v_loc = {n: args["v_" + n] for n in WEIGHTS}
    core = lax.axis_index("c").astype(jnp.int32).reshape(1)

    small_sharded = [n for n in SMALL if n in COL_SHARDED]
    gathered = gather_shards([w_loc[n].astype(bf16) for n in BIG] + [_pack_flat([w_loc[n] for n in small_sharded])])
    w_full = dict(w_loc)
    for n, got in zip(BIG, gathered):
        w_full[n] = _full_from_shards(n, got)
    per_shard = [_unpack_flat(gathered[-1][s], [w_loc[n].shape for n in small_sharded]) for s in range(N_SHARD)]
    for k, n in enumerate(small_sharded):
        w_full[n] = jnp.concatenate([per_shard[s][k] for s in range(N_SHARD)], axis=1)

    loss_part, grad_x, g_full = local_step(x, positions, loss_target, w_full)
    loss = lax.psum(loss_part[0, 0], AXES)

    small_flat = jnp.concatenate([_small_pieces(n, g_full[n]) for n in SMALL], axis=1)
    pad = (-small_flat.shape[1]) % SMALL_MULTIPLE
    small_pieces = jnp.pad(small_flat, ((0, 0), (0, pad))).reshape(N_SHARD, -1, LANE)
    units = BIG + ['small']
    pieces = [_pieces(n, g_full[n]) for n in BIG] + [small_pieces]
    from_sibling = swap_halves(pieces)
    pair = [pair_add(p, r, core, "pair_add_" + n) for n, p, r in zip(units, pieces, from_sibling)]
    from_chips = scatter_to_chips(pair)
    both = share_halves([chip_add(p, core, "chip_add_" + n) for n, p in zip(units, from_chips)])
    reduced = [b.reshape(-1, b.shape[2]) for b in both]

    out = {}
    for n, gred in zip(BIG, reduced):
        out[n] = (gred,) + tuple(adamw(w_loc[n], gred, m_loc[n], v_loc[n], "adamw_" + n))
    shapes = [w_loc[n].shape for n in SMALL]
    packs = [_pack_flat([d[n] for n in SMALL]) for d in (w_loc, m_loc, v_loc)]
    small_res = (reduced[-1],) + tuple(adamw(packs[0], reduced[-1], packs[1], packs[2], "adamw_small"))
    small_unpacked = [_unpack_flat(b, shapes) for b in small_res]
    for k, n in enumerate(SMALL):
        out[n] = tuple(u[k] for u in small_unpacked)
    return (loss, grad_x, *[out[n][0] for n in WEIGHTS], *[out[n][1] for n in WEIGHTS],
            *[out[n][2] for n in WEIGHTS], *[out[n][3] for n in WEIGHTS])
```

```python
import functools

import numpy as np
import jax
import jax.numpy as jnp
from jax import lax
from jax.experimental import pallas as pl
from jax.experimental.pallas import tpu as pltpu

f32 = jnp.float32
bf16 = jnp.bfloat16
HI = lax.Precision.HIGHEST
MID = lax.Precision.HIGH

D_MODEL = 1024
HEAD_DIM = 64
N_HEADS = 8
RWKV_COLS = 1792
RWKV_GN_EPS = 64e-5
SSM_STATE = 128
SSM_CHUNK = 128
L0_COLS = 3336
L0_PAD = 3456
L1_COLS = 1952
L1_PAD = 2048
MLA_ROPE = 32
ROPE_THETA = 10000.0
D_FF = 2816
DEPTH = 2
ALPHA = (2 * DEPTH) ** 0.25
ADAM_LR = 0.001
ADAM_B1 = 0.9
ADAM_B2 = 0.999
ADAM_EPS = 1e-08
ADAM_WD = 0.01
ADAM_STEP = 10
RWKV_CHUNK = 64
RWKV_HEADS_PER_STEP = 8
LANE = 128
SUBLANE = 8
VMEM_LIMIT = 56 * 1024 * 1024

WEIGHTS = ['l0_w_in', 'rwkv_mix', 'rwkv_w0', 'rwkv_w2', 'rwkv_a0', 'rwkv_a2', 'rwkv_g2', 'rwkv_k_k', 'rwkv_k_a',
           'rwkv_r_k', 'rwkv_ln_g', 'rwkv_ln_b', 'ssm_conv_w', 'ssm_conv_b', 'ssm_dt_bias', 'ssm_a_log', 'ssm_d',
           'ssm_norm_g', 'l0_w_out', 'l0_ln1_g', 'l0_ln1_b', 'ffn0_w_up', 'ffn0_conv_w', 'ffn0_conv_b',
           'ffn0_w_down', 'l0_ln2_g', 'l0_ln2_b', 'l1_w_in', 'mla_q_norm_g', 'mla_w_uq', 'mla_kv_norm_g',
           'mla_w_ukv', 'l1_w_out', 'l1_ln1_g', 'l1_ln1_b', 'ffn1_w_up', 'ffn1_conv_w', 'ffn1_conv_b',
           'ffn1_w_down', 'l1_ln2_g', 'l1_ln2_b']
COL_SHARDED = ['l0_w_in', 'rwkv_w2', 'rwkv_a2', 'rwkv_g2', 'ssm_conv_w', 'ffn0_w_up', 'ffn0_conv_w', 'l1_w_in',
               'mla_w_uq', 'mla_w_ukv', 'ffn1_w_up', 'ffn1_conv_w']
ROW_SHARDED = ['l0_w_out', 'ffn0_w_down', 'l1_w_out', 'ffn1_w_down']
BIG = ['l0_w_in', 'l0_w_out', 'ffn0_w_up', 'ffn0_w_down', 'l1_w_in', 'l1_w_out', 'ffn1_w_up', 'ffn1_w_down']
SMALL = [n for n in WEIGHTS if n not in BIG]
N_SHARD = 4


def _cparams(sem):
    return pltpu.CompilerParams(dimension_semantics=sem, vmem_limit_bytes=VMEM_LIMIT)


def _dg(a, b, ca, cb, prec=None):
    return lax.dot_general(a, b, (((ca,), (cb,)), ((), ())), precision=prec, preferred_element_type=f32)


def hdot(a, b):
    return _dg(a, b, 1, 0, HI)


def mdot(a, b):
    return _dg(a, b, 1, 0, MID)


def mdot_nt(a, b):
    return _dg(a, b, 1, 1, MID)


def mdot_tn(a, b):
    return _dg(a, b, 0, 0, MID)


def _b(x):
    return x.astype(bf16)


@jax.custom_vjp
def bdot(x, w):
    return _dg(_b(x), _b(w), 1, 0)


def _bdot_fwd(x, w):
    return bdot(x, w), (x, w)


def _bdot_bwd(res, g):
    x, w = res
    return _dg(_b(g), _b(w), 1, 1).astype(x.dtype), _dg(_b(x), _b(g), 0, 0).astype(w.dtype)


bdot.defvjp(_bdot_fwd, _bdot_bwd)


@jax.custom_vjp
def bdot_nt(x, y):
    return _dg(_b(x), _b(y), 1, 1)


def _bdot_nt_fwd(x, y):
    return bdot_nt(x, y), (x, y)


def _bdot_nt_bwd(res, g):
    x, y = res
    return _dg(_b(g), _b(y), 1, 0), _dg(_b(g), _b(x), 0, 0)


bdot_nt.defvjp(_bdot_nt_fwd, _bdot_nt_bwd)


@jax.custom_vjp
def bdot_tn(x, y):
    return _dg(_b(x), _b(y), 0, 0)


def _bdot_tn_fwd(x, y):
    return bdot_tn(x, y), (x, y)


def _bdot_tn_bwd(res, g):
    x, y = res
    return _dg(_b(y), _b(g), 1, 1), _dg(_b(x), _b(g), 1, 0)


bdot_tn.defvjp(_bdot_tn_fwd, _bdot_tn_bwd)


def _sigmoid(x):
    return 1.0 / (1.0 + jnp.exp(-x))


@jax.custom_vjp
def softplus(x):
    e = jnp.exp(-jnp.abs(x))
    u = 1.0 + e
    log1p = jnp.where(u == 1.0, e, jnp.log(u) * e / jnp.where(u == 1.0, 1.0, u - 1.0))
    return jnp.maximum(x, 0.0) + log1p


def _softplus_fwd(x):
    return softplus(x), x


def _softplus_bwd(x, g):
    return (g * _sigmoid(x),)


softplus.defvjp(_softplus_fwd, _softplus_bwd)


def silu(x):
    return x * _sigmoid(x)


def _shift_rows(x, k, up):
    if k == 0:
        return x
    t = x.shape[0]
    rows = lax.broadcasted_iota(jnp.int32, x.shape, 0)
    if up:
        return jnp.where(rows < t - k, pltpu.roll(x, t - k, 0), 0.0)
    return jnp.where(rows >= k, pltpu.roll(x, k, 0), 0.0)


@functools.partial(jax.custom_vjp, nondiff_argnums=(1,))
def shift_down(x, k):
    return _shift_rows(x, k, False)


def _shift_down_fwd(x, k):
    return _shift_rows(x, k, False), None


def _shift_down_bwd(k, _, g):
    return (_shift_rows(g, k, True),)


shift_down.defvjp(_shift_down_fwd, _shift_down_bwd)


@functools.partial(jax.custom_vjp, nondiff_argnums=(1,))
def lane_roll(x, s):
    return pltpu.roll(x, s % x.shape[1], 1)


def _lane_roll_fwd(x, s):
    return lane_roll(x, s), None


def _lane_roll_bwd(s, _, g):
    return (pltpu.roll(g, (-s) % g.shape[1], 1),)


lane_roll.defvjp(_lane_roll_fwd, _lane_roll_bwd)


def rot_half32(x):
    first = (lax.broadcasted_iota(jnp.int32, x.shape, 1) % MLA_ROPE) < (MLA_ROPE // 2)
    return jnp.where(first, -lane_roll(x, -(MLA_ROPE // 2)), lane_roll(x, MLA_ROPE // 2))


def _iota2(shape, axis):
    return lax.broadcasted_iota(jnp.int32, shape, axis)


class Op:
    def __init__(self, arr, block, imap, diff=True, acc=None, gshape=None, gimap=None, gdtype=f32):
        self.arr, self.block, self.imap, self.diff, self.acc = arr, tuple(block), imap, diff, acc
        self.gshape = tuple(arr.shape) if gshape is None else tuple(gshape)
        self.gimap = imap if gimap is None else gimap
        self.gdtype = gdtype


class Out:
    def __init__(self, shape, block, imap, dtype=f32):
        self.shape, self.block, self.imap, self.dtype = tuple(shape), tuple(block), imap, dtype


def block_fwd(fn, name, grid, ops, outs):
    n_in = len(ops)

    def body(*refs):
        vals = [r[...] for r in refs[:n_in]]
        res = fn(*vals)
        for r, v in zip(refs[n_in:], res):
            r[...] = v.astype(r.dtype)

    res = pl.pallas_call(
        body, name=name, grid=grid,
        in_specs=[pl.BlockSpec(o.block, o.imap) for o in ops],
        out_specs=[pl.BlockSpec(o.block, o.imap) for o in outs],
        out_shape=[jax.ShapeDtypeStruct(o.shape, o.dtype) for o in outs],
        compiler_params=_cparams(("arbitrary", "arbitrary")),
    )(*[o.arr for o in ops])
    return tuple(res)


def block_bwd(fn, name, grid, ops, outs, douts):
    n_in, n_out = len(ops), len(outs)
    dix = [k for k, o in enumerate(ops) if o.diff]

    def body(*refs):
        vals = [r[...] for r in refs[:n_in]]
        dvals = tuple(r[...] for r in refs[n_in:n_in + n_out])
        grefs = refs[n_in + n_out:]

        def f(*d):
            full = list(vals)
            for k, v in zip(dix, d):
                full[k] = v
            return tuple(fn(*full))

        _, vjp = jax.vjp(f, *[vals[k] for k in dix])
        grads = vjp(dvals)
        j, i = pl.program_id(0), pl.program_id(1)
        for k, gref, g in zip(dix, grefs, grads):
            acc = ops[k].acc
            if acc is None:
                gref[...] = g.astype(gref.dtype)
            else:
                first = (i == 0) if acc == 'i' else jnp.logical_and(i == 0, j == 0)

                @pl.when(first)
                def _():
                    gref[...] = g

                @pl.when(jnp.logical_not(first))
                def _():
                    gref[...] += g

    gspecs = [pl.BlockSpec(ops[k].block, ops[k].gimap) for k in dix]
    gshapes = [jax.ShapeDtypeStruct(ops[k].gshape, ops[k].gdtype) for k in dix]
    res = pl.pallas_call(
        body, name=name, grid=grid,
        in_specs=[pl.BlockSpec(o.block, o.imap) for o in ops] + [pl.BlockSpec(o.block, o.imap) for o in outs],
        out_specs=gspecs, out_shape=gshapes,
        compiler_params=_cparams(("arbitrary", "arbitrary")),
    )(*[o.arr for o in ops], *douts)
    return tuple(res)


def _rows(arr, tm, diff=True, gdtype=f32):
    return Op(arr, (tm, arr.shape[1]), lambda j, i: (i, 0), diff=diff, gdtype=gdtype)


def _param(arr, diff=True):
    return Op(arr, arr.shape, lambda j, i: (0,) * arr.ndim, diff=diff, acc='ij')


def _rows_out(n, c, tm, dtype=f32):
    return Out((n, c), (tm, c), lambda j, i: (i, 0), dtype)


def _cols(arr, t, tc, off=0, width=None, gdtype=f32):
    width = arr.shape[1] if width is None else width
    return Op(arr, (t, tc), lambda j, i: (i, j + off), gshape=(arr.shape[0], width), gimap=lambda j, i: (i, j),
              gdtype=gdtype)


def _cparam(arr, tc):
    return Op(arr, (arr.shape[0], tc), lambda j, i: (0, j), acc='i')


def _colblock(arr, tm, off, width, gdtype=f32):
    return Op(arr, (tm, width), lambda j, i: (i, off // width), gshape=(arr.shape[0], width),
              gimap=lambda j, i: (i, 0), gdtype=gdtype)


def _tile(n, cap):
    best = None
    for t in range(LANE, min(n, cap) + 1, LANE):
        if n % t == 0:
            best = t
    return n if best is None else best


def _rtile(rows, cols, cap_bytes=2 * 1024 * 1024):
    best = None
    for t in range(SUBLANE, rows + 1, SUBLANE):
        if rows % t == 0 and t * cols * 4 <= cap_bytes:
            best = t
    return rows if best is None else best


def mm(a, b, name, ta=False, add=None):
    m = a.shape[1] if ta else a.shape[0]
    kd = a.shape[0] if ta else a.shape[1]
    n = b.shape[1]
    tm, tn = _tile(m, 1024), _tile(n, 512)
    tk = kd if kd <= 2048 else _tile(kd, 1408)
    nk = kd // tk
    ca = 0 if ta else 1

    def body(*refs):
        if add is None:
            a_ref, b_ref, o_ref, acc = refs
        else:
            a_ref, b_ref, add_ref, o_ref, acc = refs
        k = pl.program_id(2)

        @pl.when(k == 0)
        def _():
            acc[...] = jnp.zeros_like(acc)

        acc[...] += _dg(_b(a_ref[...]), _b(b_ref[...]), ca, 0)

        @pl.when(k == nk - 1)
        def _():
            o_ref[...] = acc[...] if add is None else acc[...] + add_ref[...]

    a_spec = pl.BlockSpec((tk, tm), lambda i, j, k: (k, i)) if ta else pl.BlockSpec((tm, tk), lambda i, j, k: (i, k))
    b_spec = pl.BlockSpec((tk, tn), lambda i, j, k: (k, j))
    o_spec = pl.BlockSpec((tm, tn), lambda i, j, k: (i, j))
    args, specs = [a, b], [a_spec, b_spec]
    if add is not None:
        args.append(add)
        specs.append(o_spec)
    return pl.pallas_call(
        body, name=name, grid=(m // tm, n // tn, nk), in_specs=specs, out_specs=o_spec,
        out_shape=jax.ShapeDtypeStruct((m, n), f32), scratch_shapes=[pltpu.VMEM((tm, tn), f32)],
        compiler_params=_cparams(("parallel", "parallel", "arbitrary")),
    )(*args)


def f_ln(h, y, g, b):
    x = ALPHA * h + y
    mu = jnp.mean(x, axis=-1, keepdims=True)
    xc = x - mu
    var = jnp.mean(xc * xc, axis=-1, keepdims=True)
    return (xc * lax.rsqrt(var + 1e-5) * g + b,)


def f_shift_mix(p, mix):
    return (p + (shift_down(p, 1) - p) * mix,)


def f_rwkv_pre(k, wa_lo, g_lo, w0, w2, a0, a2, g2, k_k, k_a, gh):
    w_lo, a_lo = wa_lo[:, :64], wa_lo[:, 64:]
    log_w = -softplus(-(w0 + bdot(jnp.tanh(w_lo), w2))) - 0.5
    lw = -jnp.exp(log_w)
    a = _sigmoid(a0 + bdot(a_lo, a2))
    g = bdot(_sigmoid(g_lo), g2)
    kk = k * k_k
    kk = kk / jnp.maximum(jnp.sqrt(hdot(kk * kk, gh)), 1e-12)
    k2 = k * (1.0 + (a - 1.0) * k_a)
    return lw, k2, -kk, kk * a, g


def f_rwkv_post(y, r, k2, v, g, ln_g, ln_b, r_k, gh):
    mu = hdot(y, gh) * (1.0 / HEAD_DIM)
    yc = y - mu
    var = hdot(yc * yc, gh) * (1.0 / HEAD_DIM)
    yn = yc * lax.rsqrt(var + RWKV_GN_EPS) * ln_g + ln_b
    bonus = hdot(r * k2 * r_k, gh) * v
    return ((yn + bonus) * g,)


def f_conv4_silu(x, w0, w1, w2, w3, b):
    y = b + shift_down(x, 3) * w0 + shift_down(x, 2) * w1 + shift_down(x, 1) * w2 + x * w3
    return (silu(y),)


def f_ssm_post(y, z, norm_g, gg):
    u = y * silu(z)
    ms = hdot(u * u, gg) * (1.0 / 256.0)
    return (u * lax.rsqrt(ms + 1e-5) * norm_g,)


def f_ffn_act(gate, up, w0, w1, w2, b):
    gc = b + shift_down(gate, 2) * w0 + shift_down(gate, 1) * w1 + gate * w2
    return (silu(gc) * up,)


def _rms(x, g, eps=1e-6):
    return x * lax.rsqrt(jnp.mean(x * x, axis=-1, keepdims=True) + eps) * g


def f_mla_pre(c_q, c_kv, kpe, pos, q_g, w_qn, w_qr, kv_g, w_ukv, inv_q, inv_k):
    qn_in = _rms(c_q, q_g)
    q_nope = bdot(qn_in, w_qn)
    qr = bdot(qn_in, w_qr)
    kv = bdot(_rms(c_kv, kv_g), w_ukv)
    ang_q = pos * inv_q
    ang_k = pos * inv_k
    return (q_nope, qr * jnp.cos(ang_q) + rot_half32(qr) * jnp.sin(ang_q), kv,
            kpe * jnp.cos(ang_k) + rot_half32(kpe) * jnp.sin(ang_k))


def rwkv_chunk(s0, r, lw, k, v, a, b):
    hs = range(len(r))
    l = r[0].shape[0]
    ri, ci = _iota2((l, l), 0), _iota2((l, l), 1)
    strict, incl = ri > ci, ri >= ci
    tri, eye = incl.astype(f32), (ri == ci).astype(f32)
    last = (_iota2((l, 1), 0) == l - 1).astype(f32)
    c = [hdot(tri, lw[h]) for h in hs]
    at = [a[h] * jnp.exp(c[h] - lw[h]) for h in hs]
    wi = [jnp.exp(-c[h]) for h in hs]
    bt = [b[h] * wi[h] for h in hs]
    kt = [k[h] * wi[h] for h in hs]
    rt = [r[h] * jnp.exp(c[h]) for h in hs]
    nab = [jnp.where(strict, mdot_nt(at[h], bt[h]), 0.0) for h in hs]
    nak = [jnp.where(strict, mdot_nt(at[h], kt[h]), 0.0) for h in hs]
    g = [mdot_nt(at[h], s0[h]) + mdot(nak[h], v[h]) for h in hs]
    x = [eye + nab[h] for h in hs]
    p = [mdot(nab[h], nab[h]) for h in hs]
    steps = max(1, (l - 1).bit_length()) - 1
    for it in range(steps):
        x = [x[h] + mdot(p[h], x[h]) for h in hs]
        if it < steps - 1:
            p = [mdot(p[h], p[h]) for h in hs]
    u = [mdot(x[h], g[h]) for h in hs]
    mrb = [jnp.where(incl, mdot_nt(rt[h], bt[h]), 0.0) for h in hs]
    mrk = [jnp.where(incl, mdot_nt(rt[h], kt[h]), 0.0) for h in hs]
    y = [mdot_nt(rt[h], s0[h]) + mdot(mrb[h], u[h]) + mdot(mrk[h], v[h]) for h in hs]
    s1 = [(s0[h] + mdot_tn(u[h], bt[h]) + mdot_tn(v[h], kt[h])) * jnp.exp(jnp.sum(c[h] * last, axis=0, keepdims=True))
          for h in hs]
    return y, s1


def ssd_chunk(xs, bm, cm, dt_raw, s_in, dt_bias, a_log, d_skip, e_heads):
    l = xs.shape[0]
    ri, ci = _iota2((l, l), 0), _iota2((l, l), 1)
    incl = ri >= ci
    tri = incl.astype(f32)
    dt = softplus(dt_raw + dt_bias)
    a128 = dt * (-jnp.exp(a_log))
    lane0 = (_iota2((1, HEAD_DIM), 1) == 0).astype(f32)
    last = (_iota2((l, 1), 0) == l - 1).astype(f32)
    cb = [bdot_nt(cm[:, g * SSM_STATE:(g + 1) * SSM_STATE], bm[:, g * SSM_STATE:(g + 1) * SSM_STATE]) for g in range(2)]
    ys, s_out = [], []
    for h in range(N_HEADS):
        g = h // 4
        e = e_heads[h]
        x_h = xs[:, h * HEAD_DIM:(h + 1) * HEAD_DIM]
        dt_h = hdot(dt, e)
        ac = hdot(tri, hdot(a128, e))
        xd = x_h * dt_h
        col = jnp.broadcast_to(jnp.sum(ac * lane0, axis=1, keepdims=True), (l, l))
        decay = jnp.exp(jnp.where(incl, col - col.T, -1e30))
        y_diag = bdot(cb[g] * decay, xd)
        a_tot = jnp.sum(ac * last, axis=0, keepdims=True)
        b_g = bm[:, g * SSM_STATE:(g + 1) * SSM_STATE]
        c_g = cm[:, g * SSM_STATE:(g + 1) * SSM_STATE]
        s_new = jnp.exp(a_tot) * s_in[h] + bdot_tn(b_g, xd * jnp.exp(a_tot - ac))
        y_off = jnp.exp(ac) * bdot(c_g, s_in[h])
        ys.append(y_diag + y_off + x_h * hdot(jnp.broadcast_to(d_skip, (l, LANE)), e))
        s_out.append(s_new)
    return jnp.concatenate(ys, axis=1), s_out


SB_KEYS = LANE
MLA_KEYS = 256


def sb_tile(q, k, v, run, q0, k0):
    bq, kb = q.shape[0], k.shape[0]
    z = bdot_nt(q, k) * HEAD_DIM ** -0.5
    strict = (k0 + _iota2((bq, kb), 1)) < (q0 + _iota2((bq, kb), 0))
    upper = (_iota2((kb, kb), 0) > _iota2((kb, kb), 1)).astype(f32)
    lk = jnp.where(strict, -softplus(z), 0.0)
    log_att = z + lk + mdot(lk, upper) + run
    att = jnp.where(strict, jnp.exp(jnp.where(strict, log_att, 0.0)), 0.0)
    return bdot(att, v), jnp.sum(lk, axis=1, keepdims=True)


def mla_scores(qn, qp, kn, kp, q0, k0):
    bq, kb = qn.shape[0], kn.shape[0]
    s = (bdot_nt(qn, kn) + bdot_nt(qp, kp)) * (HEAD_DIM + MLA_ROPE) ** -0.5
    causal = (k0 + _iota2((bq, kb), 1)) <= (q0 + _iota2((bq, kb), 0))
    return jnp.where(causal, s, -1e30), causal


def mla_tile_loss(qn, qp, kn, kp, v, do, lse, dsum, q0, k0):
    s, causal = mla_scores(qn, qp, kn, kp, q0, k0)
    p = jnp.where(causal, jnp.exp(s - lse), 0.0)
    return jnp.sum(do * bdot(p, v)) - jnp.sum(dsum * jnp.sum(p, axis=1, keepdims=True))


def _head(h):
    return slice(h * HEAD_DIM, (h + 1) * HEAD_DIM)


def _rwkv_specs(nc, rev):
    hp = RWKV_HEADS_PER_STEP
    w = hp * HEAD_DIM
    chunk = (lambda c: nc - 1 - c) if rev else (lambda c: c)
    tok = lambda off: pl.BlockSpec((RWKV_CHUNK, w), lambda b, g, c: (b * nc + chunk(c), off // w + g))
    st = pl.BlockSpec((1, hp, HEAD_DIM, HEAD_DIM), lambda b, g, c: ((b * (N_HEADS // hp) + g) * nc + chunk(c), 0, 0, 0))
    return tok, st


def rwkv_scan_fwd(ps, lw, k2, na, bb, nb, t):
    hp, nc = RWKV_HEADS_PER_STEP, t // RWKV_CHUNK
    ng = N_HEADS // hp
    tok, st = _rwkv_specs(nc, False)

    def body(r_ref, v_ref, lw_ref, k_ref, a_ref, b_ref, y_ref, s0_ref, s):
        @pl.when(pl.program_id(2) == 0)
        def _():
            s[...] = jnp.zeros_like(s)

        s0_ref[0] = s[...]
        heads = lambda ref: [ref[:, _head(h)] for h in range(hp)]
        y, s1 = rwkv_chunk([s[h] for h in range(hp)], heads(r_ref), heads(lw_ref), heads(k_ref), heads(v_ref),
                           heads(a_ref), heads(b_ref))
        for h in range(hp):
            y_ref[:, _head(h)] = y[h]
            s[h] = s1[h]

    return pl.pallas_call(
        body, name="rwkv_scan_fwd", grid=(nb, ng, nc),
        in_specs=[tok(0), tok(1024), tok(0), tok(0), tok(0), tok(0)], out_specs=[tok(0), st],
        out_shape=[jax.ShapeDtypeStruct((nb * t, N_HEADS * HEAD_DIM), f32),
                   jax.ShapeDtypeStruct((nb * ng * nc, hp, HEAD_DIM, HEAD_DIM), f32)],
        scratch_shapes=[pltpu.VMEM((hp, HEAD_DIM, HEAD_DIM), f32)],
        compiler_params=_cparams(("parallel", "parallel", "arbitrary")),
    )(ps, ps, lw, k2, na, bb)


def rwkv_scan_bwd(s0, ps, lw, k2, na, bb, dy, nb, t):
    hp, nc = RWKV_HEADS_PER_STEP, t // RWKV_CHUNK
    ng = N_HEADS // hp
    tok, st = _rwkv_specs(nc, True)

    def body(s0_ref, r_ref, v_ref, lw_ref, k_ref, a_ref, b_ref, dy_ref, dr, dlw, dk, dv, da, db, ds):
        @pl.when(pl.program_id(2) == 0)
        def _():
            ds[...] = jnp.zeros_like(ds)

        heads = lambda ref: [ref[:, _head(h)] for h in range(hp)]
        _, vjp = jax.vjp(rwkv_chunk, [s0_ref[0, h] for h in range(hp)], heads(r_ref), heads(lw_ref), heads(k_ref),
                         heads(v_ref), heads(a_ref), heads(b_ref))
        g = vjp((heads(dy_ref), [ds[h] for h in range(hp)]))
        for h in range(hp):
            ds[h] = g[0][h]
            for ref, val in zip((dr, dlw, dk, dv, da, db), g[1:]):
                ref[:, _head(h)] = val[h]

    return pl.pallas_call(
        body, name="rwkv_scan_bwd", grid=(nb, ng, nc),
        in_specs=[st, tok(0), tok(1024), tok(0), tok(0), tok(0), tok(0), tok(0)], out_specs=[tok(0)] * 6,
        out_shape=[jax.ShapeDtypeStruct((nb * t, N_HEADS * HEAD_DIM), f32)] * 6,
        scratch_shapes=[pltpu.VMEM((hp, HEAD_DIM, HEAD_DIM), f32)],
        compiler_params=_cparams(("parallel", "parallel", "arbitrary")),
    )(s0, ps, ps, lw, k2, na, bb, dy)


def _ssd_specs(nb, nch, rev):
    def row(b, c):
        return b * nch + (nch - 1 - c if rev else c)

    l = SSM_CHUNK
    xs = pl.BlockSpec((l, 512), lambda b, c: (row(b, c), 0))
    bm = pl.BlockSpec((l, 256), lambda b, c: (row(b, c), 2))
    cm = pl.BlockSpec((l, 256), lambda b, c: (row(b, c), 3))
    dt = pl.BlockSpec((l, LANE), lambda b, c: (row(b, c), (L0_PAD - LANE) // LANE))
    st = pl.BlockSpec((1, 1, N_HEADS, SSM_STATE, HEAD_DIM), lambda b, c: (b, (nch - 1 - c if rev else c), 0, 0, 0))
    par = pl.BlockSpec((1, LANE), lambda b, c: (0, 0))
    eh = pl.BlockSpec((N_HEADS, LANE, HEAD_DIM), lambda b, c: (0, 0, 0))
    return xs, bm, cm, dt, st, par, eh, row


def ssd_fwd(xbc_act, proj0, dt_bias, a_log, d_skip, e_heads, nb, t):
    nch = t // SSM_CHUNK
    n_tok = nb * t
    xs, bm, cm, dt, st, par, eh, row = _ssd_specs(nb, nch, False)

    def body(x_ref, b_ref, c_ref, dt_ref, db_ref, al_ref, dsk_ref, e_ref, y_ref, st_ref, s):
        @pl.when(pl.program_id(1) == 0)
        def _():
            s[...] = jnp.zeros_like(s)

        st_ref[0, 0] = s[...]
        y, s_out = ssd_chunk(x_ref[...], b_ref[...], c_ref[...], dt_ref[...], [s[h] for h in range(N_HEADS)],
                             db_ref[...], al_ref[...], dsk_ref[...], [e_ref[h] for h in range(N_HEADS)])
        y_ref[...] = y
        for h in range(N_HEADS):
            s[h] = s_out[h]

    return pl.pallas_call(
        body, name="ssd_fwd", grid=(nb, nch), in_specs=[xs, bm, cm, dt, par, par, par, eh],
        out_specs=[pl.BlockSpec((SSM_CHUNK, 512), lambda b, c: (row(b, c), 0)), st],
        out_shape=[jax.ShapeDtypeStruct((n_tok, 512), f32),
                   jax.ShapeDtypeStruct((nb, nch, N_HEADS, SSM_STATE, HEAD_DIM), f32)],
        scratch_shapes=[pltpu.VMEM((N_HEADS, SSM_STATE, HEAD_DIM), f32)],
        compiler_params=_cparams(("arbitrary", "arbitrary")),
    )(xbc_act, xbc_act, xbc_act, proj0, dt_bias, a_log, d_skip, e_heads)


def ssd_bwd(xbc_act, proj0, dt_bias, a_log, d_skip, e_heads, states, dy, nb, t):
    nch = t // SSM_CHUNK
    n_tok = nb * t
    xs, bm, cm, dt, st, par, eh, row = _ssd_specs(nb, nch, True)

    def body(x_ref, b_ref, c_ref, dt_ref, db_ref, al_ref, dsk_ref, e_ref, st_ref, dy_ref,
             dx_ref, dbm_ref, dcm_ref, ddt_ref, ddb_ref, dal_ref, ddsk_ref, ds):
        first = jnp.logical_and(pl.program_id(0) == 0, pl.program_id(1) == 0)

        @pl.when(pl.program_id(1) == 0)
        def _():
            ds[...] = jnp.zeros_like(ds)

        e_list = [e_ref[h] for h in range(N_HEADS)]

        def f(x, bmv, cmv, dtr, s_in, dbv, alv, dskv):
            return ssd_chunk(x, bmv, cmv, dtr, s_in, dbv, alv, dskv, e_list)

        _, vjp = jax.vjp(f, x_ref[...], b_ref[...], c_ref[...], dt_ref[...],
                         [st_ref[0, 0, h] for h in range(N_HEADS)], db_ref[...], al_ref[...], dsk_ref[...])
        g = vjp((dy_ref[...], [ds[h] for h in range(N_HEADS)]))
        dx_ref[...], dbm_ref[...], dcm_ref[...], ddt_ref[...] = g[0], g[1], g[2], g[3].astype(bf16)
        for h in range(N_HEADS):
            ds[h] = g[4][h]
        for ref, val in zip((ddb_ref, dal_ref, ddsk_ref), g[5:]):
            @pl.when(first)
            def _():
                ref[...] = val

            @pl.when(jnp.logical_not(first))
            def _():
                ref[...] += val

    rows_spec = lambda w: pl.BlockSpec((SSM_CHUNK, w), lambda b, c: (row(b, c), 0))
    return pl.pallas_call(
        body, name="ssd_bwd", grid=(nb, nch),
        in_specs=[xs, bm, cm, dt, par, par, par, eh, st, rows_spec(512)],
        out_specs=[rows_spec(512), rows_spec(256), rows_spec(256), rows_spec(LANE), par, par, par],
        out_shape=[jax.ShapeDtypeStruct((n_tok, 512), f32), jax.ShapeDtypeStruct((n_tok, 256), f32),
                   jax.ShapeDtypeStruct((n_tok, 256), f32), jax.ShapeDtypeStruct((n_tok, LANE), bf16)]
        + [jax.ShapeDtypeStruct((1, LANE), f32)] * 3,
        scratch_shapes=[pltpu.VMEM((N_HEADS, SSM_STATE, HEAD_DIM), f32)],
        compiler_params=_cparams(("arbitrary", "arbitrary")),
    )(xbc_act, xbc_act, xbc_act, proj0, dt_bias, a_log, d_skip, e_heads, states, dy)


ATT_BQ = 256
SB_HEADS_PER_STEP = 2
MLA_HEADS_PER_STEP = 4


def _sb_specs(t, bq, nq):
    w = SB_HEADS_PER_STEP * HEAD_DIM
    qs = lambda off: pl.BlockSpec((bq, w), lambda b, g, i: (b * nq + i, off // w + g))
    ks = lambda off: pl.BlockSpec((t, w), lambda b, g, i: (b, off // w + g))
    return qs, ks


def sb_fwd(proj1, nb, t):
    bq = min(ATT_BQ, t)
    nq = t // bq
    qs, ks = _sb_specs(t, bq, nq)

    def body(q_ref, k_ref, v_ref, o_ref):
        q0 = pl.program_id(2) * bq
        n_tiles = (q0 + bq) // SB_KEYS
        hs = range(SB_HEADS_PER_STEP)
        q = [q_ref[:, _head(h)] for h in hs]

        def step(i, carry):
            k0 = pl.multiple_of((n_tiles - 1 - i) * SB_KEYS, SB_KEYS)
            out = []
            for h in hs:
                o, run = carry[h]
                o_t, mass = sb_tile(q[h], k_ref[pl.ds(k0, SB_KEYS), _head(h)], v_ref[pl.ds(k0, SB_KEYS), _head(h)],
                                    run, q0, k0)
                out.append((o + o_t, run + mass))
            return out

        res = lax.fori_loop(0, n_tiles, step, [(jnp.zeros((bq, HEAD_DIM), f32), jnp.zeros((bq, 1), f32)) for _ in hs])
        for h in hs:
            o_ref[:, _head(h)] = res[h][0].astype(bf16)

    return pl.pallas_call(
        body, name="sb_fwd", grid=(nb, N_HEADS // SB_HEADS_PER_STEP, nq), in_specs=[qs(0), ks(512), ks(1024)],
        out_specs=qs(0), out_shape=jax.ShapeDtypeStruct((nb * t, 512), bf16),
        compiler_params=_cparams(("parallel", "parallel", "arbitrary")),
    )(proj1, proj1, proj1)


def sb_bwd(proj1, do, nb, t):
    bq = min(ATT_BQ, t)
    nq = t // bq
    qs, ks = _sb_specs(t, bq, nq)

    def body(q_ref, k_ref, v_ref, do_ref, dq_ref, dk_ref, dv_ref, mass):
        @pl.when(pl.program_id(2) == 0)
        def _():
            dk_ref[...] = jnp.zeros_like(dk_ref)
            dv_ref[...] = jnp.zeros_like(dv_ref)

        q0 = pl.program_id(2) * bq
        n_tiles = (q0 + bq) // SB_KEYS
        hs = range(SB_HEADS_PER_STEP)
        q = [q_ref[:, _head(h)] for h in hs]
        do = [do_ref[:, _head(h)].astype(f32) for h in hs]
        col0 = jnp.zeros((bq, 1), f32)

        def tile(ref, k0, h):
            return ref[pl.ds(k0, SB_KEYS), _head(h)]

        def masses(j, total):
            k0 = pl.multiple_of(j * SB_KEYS, SB_KEYS)
            out = []
            for h in hs:
                _, m = sb_tile(q[h], tile(k_ref, k0, h), tile(v_ref, k0, h), col0, q0, k0)
                mass[h, j] = m
                out.append(total[h] + m)
            return out

        total = lax.fori_loop(0, n_tiles, masses, [col0 for _ in hs])

        def grads(j, carry):
            k0 = pl.multiple_of(j * SB_KEYS, SB_KEYS)
            out = []
            for h in hs:
                dq, c, prefix = carry[h]
                prefix = prefix + mass[h, j]
                _, vjp = jax.vjp(lambda a, b, d, r: sb_tile(a, b, d, r, q0, k0),
                                 q[h], tile(k_ref, k0, h), tile(v_ref, k0, h), total[h] - prefix)
                dq_t, dk_t, dv_t, drun = vjp((do[h], c))
                dk_ref[pl.ds(k0, SB_KEYS), _head(h)] += dk_t
                dv_ref[pl.ds(k0, SB_KEYS), _head(h)] += dv_t
                out.append((dq + dq_t, drun + c, prefix))
            return out

        res = lax.fori_loop(0, n_tiles, grads, [(jnp.zeros((bq, HEAD_DIM), f32), col0, col0) for _ in hs])
        for h in hs:
            dq_ref[:, _head(h)] = res[h][0]

    return pl.pallas_call(
        body, name="sb_bwd", grid=(nb, N_HEADS // SB_HEADS_PER_STEP, nq),
        in_specs=[qs(0), ks(512), ks(1024), qs(0)], out_specs=[qs(0), ks(0), ks(0)],
        out_shape=[jax.ShapeDtypeStruct((nb * t, 512), f32)] * 3,
        scratch_shapes=[pltpu.VMEM((SB_HEADS_PER_STEP, t // SB_KEYS, bq, 1), f32)],
        compiler_params=_cparams(("parallel", "parallel", "arbitrary")),
    )(proj1, proj1, proj1, do)


def _mla_specs(t, bq, nq):
    hp = MLA_HEADS_PER_STEP
    qn = pl.BlockSpec((bq, hp * HEAD_DIM), lambda b, g, i: (b * nq + i, g))
    qr = pl.BlockSpec((bq, hp * MLA_ROPE), lambda b, g, i: (b * nq + i, g))
    kv = pl.BlockSpec((t, hp * 2 * HEAD_DIM), lambda b, g, i: (b, g))
    kp = pl.BlockSpec((t, LANE), lambda b, g, i: (b, 0))
    return qn, qr, kv, kp


def _mla_softmax_pass(qn, qp, kv_ref, kp_ref, q0, n_tiles, bq):
    hs = range(MLA_HEADS_PER_STEP)

    def step(j, carry):
        k0 = pl.multiple_of(j * MLA_KEYS, MLA_KEYS)
        kp = kp_ref[pl.ds(k0, MLA_KEYS), :MLA_ROPE]
        out = []
        for h in hs:
            m, l, acc = carry[h]
            s, _ = mla_scores(qn[h], qp[h], kv_ref[pl.ds(k0, MLA_KEYS), _head(2 * h)], kp, q0, k0)
            m_new = jnp.maximum(m, jnp.max(s, axis=1, keepdims=True))
            alpha, p = jnp.exp(m - m_new), jnp.exp(s - m_new)
            out.append((m_new, alpha * l + jnp.sum(p, axis=1, keepdims=True),
                        alpha * acc + bdot(p, kv_ref[pl.ds(k0, MLA_KEYS), _head(2 * h + 1)])))
        return out

    init = [(jnp.full((bq, 1), -1e30, f32), jnp.zeros((bq, 1), f32), jnp.zeros((bq, HEAD_DIM), f32)) for _ in hs]
    return lax.fori_loop(0, n_tiles, step, init)


def mla_fwd(q_nope, qr, kv, kpe, nb, t):
    bq = min(ATT_BQ, t)
    nq = t // bq
    sqn, sqr, skv, skp = _mla_specs(t, bq, nq)

    def body(qn_ref, qr_ref, kv_ref, kp_ref, o_ref):
        q0 = pl.program_id(2) * bq
        hs = range(MLA_HEADS_PER_STEP)
        qn = [qn_ref[:, _head(h)] for h in hs]
        qp = [qr_ref[:, h * MLA_ROPE:(h + 1) * MLA_ROPE] for h in hs]
        res = _mla_softmax_pass(qn, qp, kv_ref, kp_ref, q0, (q0 + bq) // MLA_KEYS, bq)
        for h in hs:
            _, l, acc = res[h]
            o_ref[:, _head(h)] = (acc / l).astype(bf16)

    return pl.pallas_call(
        body, name="mla_fwd", grid=(nb, N_HEADS // MLA_HEADS_PER_STEP, nq), in_specs=[sqn, sqr, skv, skp],
        out_specs=sqn, out_shape=jax.ShapeDtypeStruct((nb * t, 512), bf16),
        compiler_params=_cparams(("parallel", "arbitrary", "arbitrary")),
    )(q_nope, qr, kv, kpe)


def mla_bwd(q_nope, qr, kv, kpe, do, nb, t):
    bq = min(ATT_BQ, t)
    nq = t // bq
    sqn, sqr, skv, skp = _mla_specs(t, bq, nq)

    def body(qn_ref, qr_ref, kv_ref, kp_ref, do_ref, dqn_ref, dqr_ref, dkv_ref, dkp_ref):
        first_q = pl.program_id(2) == 0

        @pl.when(first_q)
        def _():
            dkv_ref[...] = jnp.zeros_like(dkv_ref)

        @pl.when(jnp.logical_and(first_q, pl.program_id(1) == 0))
        def _():
            dkp_ref[...] = jnp.zeros_like(dkp_ref)

        q0 = pl.program_id(2) * bq
        n_tiles = (q0 + bq) // MLA_KEYS
        hs = range(MLA_HEADS_PER_STEP)
        qn = [qn_ref[:, _head(h)] for h in hs]
        qp = [qr_ref[:, h * MLA_ROPE:(h + 1) * MLA_ROPE] for h in hs]
        do = [do_ref[:, _head(h)].astype(f32) for h in hs]
        stats = _mla_softmax_pass(qn, qp, kv_ref, kp_ref, q0, n_tiles, bq)
        lse = [m + jnp.log(l) for m, l, _ in stats]
        dsum = [jnp.sum(do[h] * (stats[h][2] / stats[h][1]), axis=1, keepdims=True) for h in hs]

        def grads(j, carry):
            k0 = pl.multiple_of(j * MLA_KEYS, MLA_KEYS)
            rows = pl.ds(k0, MLA_KEYS)
            kp = kp_ref[rows, :MLA_ROPE]
            out = []
            for h in hs:
                dqn, dqp = carry[h]
                g = jax.grad(mla_tile_loss, argnums=(0, 1, 2, 3, 4))(
                    qn[h], qp[h], kv_ref[rows, _head(2 * h)], kp, kv_ref[rows, _head(2 * h + 1)],
                    do[h], lse[h], dsum[h], q0, k0)
                dkv_ref[rows, _head(2 * h)] += g[2]
                dkp_ref[rows, :MLA_ROPE] += g[3]
                dkv_ref[rows, _head(2 * h + 1)] += g[4]
                out.append((dqn + g[0], dqp + g[1]))
            return out

        res = lax.fori_loop(0, n_tiles, grads,
                            [(jnp.zeros((bq, HEAD_DIM), f32), jnp.zeros((bq, MLA_ROPE), f32)) for _ in hs])
        for h in hs:
            dqn_ref[:, _head(h)] = res[h][0]
            dqr_ref[:, h * MLA_ROPE:(h + 1) * MLA_ROPE] = res[h][1]

    n = nb * t
    return pl.pallas_call(
        body, name="mla_bwd", grid=(nb, N_HEADS // MLA_HEADS_PER_STEP, nq),
        in_specs=[sqn, sqr, skv, skp, sqn], out_specs=[sqn, sqr, skv, skp],
        out_shape=[jax.ShapeDtypeStruct((n, 512), f32), jax.ShapeDtypeStruct((n, N_HEADS * MLA_ROPE), f32),
                   jax.ShapeDtypeStruct((n, 1024), f32), jax.ShapeDtypeStruct((n, LANE), f32)],
        compiler_params=_cparams(("arbitrary", "arbitrary", "arbitrary")),
    )(q_nope, qr, kv, kpe, do)


def loss_head(h, target):
    n, d = h.shape
    tm = _tile(n, 512)

    def body(h_ref, t_ref, l_ref, dh_ref):
        diff = h_ref[...] - t_ref[...]
        dh_ref[...] = diff * (1.0 / d)
        part = 0.5 * jnp.sum(jnp.sum(diff * diff, axis=1, keepdims=True) * (1.0 / d), axis=0, keepdims=True)

        @pl.when(pl.program_id(0) == 0)
        def _():
            l_ref[...] = jnp.zeros_like(l_ref)

        l_ref[...] += jnp.broadcast_to(part, l_ref.shape)

    spec = pl.BlockSpec((tm, d), lambda i: (i, 0))
    return pl.pallas_call(
        body, name="loss_head", grid=(n // tm,), in_specs=[spec, spec],
        out_specs=[pl.BlockSpec((8, LANE), lambda i: (0, 0)), spec],
        out_shape=[jax.ShapeDtypeStruct((8, LANE), f32), jax.ShapeDtypeStruct((n, d), f32)],
        compiler_params=_cparams(("arbitrary",)),
    )(h, target)


def _row(v):
    return v.reshape(1, -1)


def _pad_cols(a, n):
    return jnp.pad(a, ((0, 0), (0, n - a.shape[1])))


def _pad_row(v, n=LANE):
    return jnp.pad(v.reshape(1, -1), ((0, 0), (0, n - v.shape[0])))


def _group_matrix(width, group):
    idx = np.arange(width) // group
    return jnp.asarray((idx[:, None] == idx[None, :]).astype(np.float32))


def _head_expand():
    e = np.zeros((N_HEADS, LANE, HEAD_DIM), np.float32)
    for h in range(N_HEADS):
        e[h, h, :] = 1.0
    return jnp.asarray(e)


def _rope_freqs():
    inv = 1.0 / (ROPE_THETA ** (np.arange(0, MLA_ROPE, 2, dtype=np.float32) / MLA_ROPE))
    inv = np.tile(inv.astype(np.float32), 2)
    inv_q = np.tile(inv, N_HEADS).reshape(1, N_HEADS * MLA_ROPE)
    inv_k = np.zeros((1, LANE), np.float32)
    inv_k[0, :MLA_ROPE] = inv
    return jnp.asarray(inv_q), jnp.asarray(inv_k)


def _uq_split(w):
    w3 = w.reshape(w.shape[0], N_HEADS, HEAD_DIM + MLA_ROPE)
    return w3[:, :, :HEAD_DIM].reshape(-1, 512), w3[:, :, HEAD_DIM:].reshape(-1, N_HEADS * MLA_ROPE)


def _uq_merge(gn, gr):
    r = gn.shape[0]
    return jnp.concatenate([gn.reshape(r, N_HEADS, HEAD_DIM), gr.reshape(r, N_HEADS, MLA_ROPE)], axis=2).reshape(r, 768)


def local_step(x, positions, target, w):
    nb, t, d = x.shape
    n = nb * t
    tm = 256
    ni = n // tm
    tc = LANE
    h0 = x.reshape(n, d)
    tgt = target.reshape(n, d)
    pos = positions.reshape(n, 1).astype(f32)
    gh = _group_matrix(512, HEAD_DIM)
    gg = _group_matrix(512, 256)
    e_heads = _head_expand()
    inv_q, inv_k = _rope_freqs()
    g = {}

    def ln_stage(h, y, gname, bname):
        ops = [_rows(h, tm), _rows(y, tm, gdtype=bf16), _param(_row(w[gname])), _param(_row(w[bname]))]
        return ops, [_rows_out(n, d, tm)]

    def ln_fwd(name, ops):
        return block_fwd(lambda *a: f_ln(*a) * 2, name, (1, ni), ops, [_rows_out(n, d, tm), _rows_out(n, d, tm, bf16)])

    def ffn_act_stage(u, cw, cb):
        nj = D_FF // tc
        ops = [_cols(u, t, tc, 0, D_FF, bf16), _cols(u, t, tc, nj, D_FF, bf16)] \
            + [_cparam(cw[i:i + 1], tc) for i in range(3)] + [_cparam(_row(cb), tc)]
        return ops, [Out((n, D_FF), (t, tc), lambda j, i: (i, j), bf16)], (nj, nb)

    w_in0 = _pad_cols(w['l0_w_in'], L0_PAD)
    w_out0 = w['l0_w_out']
    h0b = h0.astype(bf16)
    proj0 = mm(h0b, w_in0, "l0_proj")

    shift_ops = [_cols(proj0, t, tc, 0, RWKV_COLS, bf16), _cparam(_row(w['rwkv_mix']), tc)]
    shift_outs = [Out((n, RWKV_COLS), (t, tc), lambda j, i: (i, j))]
    shift_grid = (RWKV_COLS // tc, nb)
    (ps,) = block_fwd(f_shift_mix, "rwkv_shift", shift_grid, shift_ops, shift_outs)

    pre_ops = [_colblock(ps, tm, 512, 512), _colblock(ps, tm, 1536, 128), _colblock(ps, tm, 1664, 128),
               _param(_row(w['rwkv_w0'])), _param(w['rwkv_w2']), _param(_row(w['rwkv_a0'])), _param(w['rwkv_a2']),
               _param(w['rwkv_g2']), _param(_row(w['rwkv_k_k'])), _param(_row(w['rwkv_k_a'])), _param(gh, diff=False)]
    pre_outs = [_rows_out(n, 512, tm) for _ in range(5)]
    lw, k2, na, bb, gate_r = block_fwd(f_rwkv_pre, "rwkv_pre", (1, ni), pre_ops, pre_outs)
    y_tok, s0_saved = rwkv_scan_fwd(ps, lw, k2, na, bb, nb, t)

    post_ops = [_rows(y_tok, tm), _colblock(ps, tm, 0, 512), _rows(k2, tm), _colblock(ps, tm, 1024, 512),
                _rows(gate_r, tm), _param(_row(w['rwkv_ln_g'])), _param(_row(w['rwkv_ln_b'])),
                _param(w['rwkv_r_k'].reshape(1, 512)), _param(gh, diff=False)]
    post_outs = [_rows_out(n, 512, tm, bf16)]
    (y_a,) = block_fwd(f_rwkv_post, "rwkv_post", (1, ni), post_ops, post_outs)

    xbc_off = (RWKV_COLS + 512) // tc
    conv_ops = [_cols(proj0, t, tc, xbc_off, 1024, bf16)] + [_cparam(w['ssm_conv_w'][i:i + 1], tc) for i in range(4)] \
        + [_cparam(_row(w['ssm_conv_b']), tc)]
    conv_outs = [Out((n, 1024), (t, tc), lambda j, i: (i, j))]
    conv_grid = (1024 // tc, nb)
    (xbc_act,) = block_fwd(f_conv4_silu, "ssm_conv", conv_grid, conv_ops, conv_outs)

    dt_bias, a_log, d_skip = _pad_row(w['ssm_dt_bias']), _pad_row(w['ssm_a_log']), _pad_row(w['ssm_d'])
    y_ssd, ssd_states = ssd_fwd(xbc_act, proj0, dt_bias, a_log, d_skip, e_heads, nb, t)

    z_tok = proj0[:, RWKV_COLS:RWKV_COLS + 512]
    spost_ops = [_rows(y_ssd, tm), _rows(z_tok, tm, gdtype=bf16), _param(_row(w['ssm_norm_g'])), _param(gg, diff=False)]
    spost_outs = [_rows_out(n, 512, tm, bf16)]
    (y_b,) = block_fwd(f_ssm_post, "ssm_post", (1, ni), spost_ops, spost_outs)

    mixed0 = mm(y_b, w_out0[512:], "l0_out_b", add=mm(y_a, w_out0[:512], "l0_out_a"))
    ln1_ops, ln_outs = ln_stage(h0, mixed0, 'l0_ln1_g', 'l0_ln1_b')
    h1, h1b = ln_fwd("l0_ln1", ln1_ops)

    u0 = mm(h1b, w['ffn0_w_up'], "ffn0_up")
    act0_ops, act_outs, act_grid = ffn_act_stage(u0, w['ffn0_conv_w'], w['ffn0_conv_b'])
    (act0,) = block_fwd(f_ffn_act, "ffn0_act", act_grid, act0_ops, act_outs)
    f0 = mm(act0, w['ffn0_w_down'], "ffn0_down")
    ln2_ops, _ = ln_stage(h1, f0, 'l0_ln2_g', 'l0_ln2_b')
    h2, h2b = ln_fwd("l0_ln2", ln2_ops)

    w_in1 = _pad_cols(w['l1_w_in'], L1_PAD)
    w_out1 = w['l1_w_out']
    proj1 = mm(h2b, w_in1, "l1_proj")
    w_qn, w_qr = _uq_split(w['mla_w_uq'])
    mpre_ops = [_colblock(proj1, tm, 1536, 256, bf16), _colblock(proj1, tm, 1792, 128, bf16),
                _colblock(proj1, tm, 1920, 128, bf16),
                Op(pos, (tm, 1), lambda j, i: (i, 0), diff=False),
                _param(_row(w['mla_q_norm_g'])), _param(w_qn), _param(w_qr),
                _param(_row(w['mla_kv_norm_g'])), _param(w['mla_w_ukv']), _param(inv_q, diff=False),
                _param(inv_k, diff=False)]
    mpre_outs = [_rows_out(n, 512, tm), _rows_out(n, N_HEADS * MLA_ROPE, tm), _rows_out(n, 1024, tm),
                 _rows_out(n, LANE, tm)]
    q_nope, q_rope, kv, kpe = block_fwd(f_mla_pre, "mla_pre", (1, ni), mpre_ops, mpre_outs)
    o_sb = sb_fwd(proj1, nb, t)
    o_mla = mla_fwd(q_nope, q_rope, kv, kpe, nb, t)

    mixed1 = mm(o_mla, w_out1[512:], "l1_out_b", add=mm(o_sb, w_out1[:512], "l1_out_a"))
    ln3_ops, _ = ln_stage(h2, mixed1, 'l1_ln1_g', 'l1_ln1_b')
    h3, h3b = ln_fwd("l1_ln1", ln3_ops)
    u1 = mm(h3b, w['ffn1_w_up'], "ffn1_up")
    act1_ops, _, _ = ffn_act_stage(u1, w['ffn1_conv_w'], w['ffn1_conv_b'])
    (act1,) = block_fwd(f_ffn_act, "ffn1_act", act_grid, act1_ops, act_outs)
    f1 = mm(act1, w['ffn1_w_down'], "ffn1_down")
    ln4_ops, _ = ln_stage(h3, f1, 'l1_ln2_g', 'l1_ln2_b')
    (h4,) = block_fwd(f_ln, "l1_ln2", (1, ni), ln4_ops, ln_outs)

    loss_part, dh4 = loss_head(h4, tgt)

    def vec(a_):
        return a_.reshape(-1)

    def ffn_bwd(tag, dh_out, ln_ops, act_ops, h_in, act, w_up, w_down, names):
        dh_res, df, gg_, gb_ = block_bwd(f_ln, tag + "_ln2_bwd", (1, ni), ln_ops, ln_outs, [dh_out])
        g[names[4]], g[names[5]] = vec(gg_), vec(gb_)
        g[names[3]] = mm(act, df, tag + "_down_dw", ta=True)
        dact = mm(df, w_down.T, tag + "_down_dx")
        dgate, dup, dw0, dw1, dw2, dcb = block_bwd(f_ffn_act, tag + "_act_bwd", act_grid, act_ops, act_outs, [dact])
        g[names[1]] = jnp.concatenate([dw0, dw1, dw2], axis=0)
        g[names[2]] = vec(dcb)
        g[names[0]] = jnp.concatenate([mm(h_in, dgate, tag + "_gate_dw", ta=True),
                                       mm(h_in, dup, tag + "_upv_dw", ta=True)], axis=1)
        w_up_t = w_up.T
        dh = mm(dgate, w_up_t[:D_FF], tag + "_gate_dx", add=dh_res)
        return mm(dup, w_up_t[D_FF:], tag + "_upv_dx", add=dh)

    def out_bwd(tag, dmixed, y_first, y_second, w_out, name):
        g[name] = jnp.concatenate([mm(y_first, dmixed, tag + "_a_dw", ta=True),
                                   mm(y_second, dmixed, tag + "_b_dw", ta=True)], axis=0)
        w_t = w_out.T
        return mm(dmixed, w_t[:, :512], tag + "_a_dx"), mm(dmixed, w_t[:, 512:], tag + "_b_dx")

    dh3 = ffn_bwd("ffn1", dh4, ln4_ops, act1_ops, h3b, act1, w['ffn1_w_up'], w['ffn1_w_down'],
                  ['ffn1_w_up', 'ffn1_conv_w', 'ffn1_conv_b', 'ffn1_w_down', 'l1_ln2_g', 'l1_ln2_b'])

    dh2_res, dmixed1, g3g, g3b = block_bwd(f_ln, "l1_ln1_bwd", (1, ni), ln3_ops, ln_outs, [dh3])
    g['l1_ln1_g'], g['l1_ln1_b'] = vec(g3g), vec(g3b)
    do_sb, do_mla = out_bwd("l1_out", dmixed1, o_sb, o_mla, w_out1, 'l1_w_out')

    dq_nope, dq_rope, dkv, dkpe = mla_bwd(q_nope, q_rope, kv, kpe, do_mla, nb, t)
    dsb_q, dsb_k, dsb_v = sb_bwd(proj1, do_sb, nb, t)
    (dc_q, dc_kv, dkpe_raw, gqg, gwqn, gwqr, gkvg, g['mla_w_ukv']) = block_bwd(
        f_mla_pre, "mla_pre_bwd", (1, ni), mpre_ops, mpre_outs, [dq_nope, dq_rope, dkv, dkpe])
    g['mla_q_norm_g'], g['mla_kv_norm_g'] = vec(gqg), vec(gkvg)
    g['mla_w_uq'] = _uq_merge(gwqn, gwqr)
    dproj1 = jnp.concatenate([dsb_q.astype(bf16), dsb_k.astype(bf16), dsb_v.astype(bf16), dc_q, dc_kv, dkpe_raw],
                             axis=1)
    g['l1_w_in'] = mm(h2b, dproj1, "l1_proj_dw", ta=True)[:, :L1_COLS]
    dh2 = mm(dproj1, w_in1.T, "l1_proj_dx", add=dh2_res)

    dh1 = ffn_bwd("ffn0", dh2, ln2_ops, act0_ops, h1b, act0, w['ffn0_w_up'], w['ffn0_w_down'],
                  ['ffn0_w_up', 'ffn0_conv_w', 'ffn0_conv_b', 'ffn0_w_down', 'l0_ln2_g', 'l0_ln2_b'])

    dh0_res, dmixed0, g1g, g1b = block_bwd(f_ln, "l0_ln1_bwd", (1, ni), ln1_ops, ln_outs, [dh1])
    g['l0_ln1_g'], g['l0_ln1_b'] = vec(g1g), vec(g1b)
    dy_a, dy_b = out_bwd("l0_out", dmixed0, y_a, y_b, w_out0, 'l0_w_out')

    dy_ssd, dz, gng = block_bwd(f_ssm_post, "ssm_post_bwd", (1, ni), spost_ops, spost_outs, [dy_b])
    g['ssm_norm_g'] = vec(gng)
    dxs, dbm, dcm, ddt_raw, gdb, gal, gdsk = ssd_bwd(xbc_act, proj0, dt_bias, a_log, d_skip, e_heads, ssd_states,
                                                     dy_ssd, nb, t)
    g['ssm_dt_bias'], g['ssm_a_log'], g['ssm_d'] = gdb[0, :8], gal[0, :8], gdsk[0, :8]
    dxbc_act = jnp.concatenate([dxs, dbm, dcm], axis=1)
    dxbc, cw0, cw1, cw2, cw3, gcb = block_bwd(f_conv4_silu, "ssm_conv_bwd", conv_grid, conv_ops, conv_outs, [dxbc_act])
    g['ssm_conv_w'] = jnp.concatenate([cw0, cw1, cw2, cw3], axis=0)
    g['ssm_conv_b'] = vec(gcb)

    dy_tok, dr_post, dk2_post, dv_post, dgate, glg, glb, grk = block_bwd(
        f_rwkv_post, "rwkv_post_bwd", (1, ni), post_ops, post_outs, [dy_a])
    g['rwkv_ln_g'], g['rwkv_ln_b'], g['rwkv_r_k'] = vec(glg), vec(glb), grk.reshape(N_HEADS, HEAD_DIM)
    dr, dlw, dk2, dv, dna, dbb = rwkv_scan_bwd(s0_saved, ps, lw, k2, na, bb, dy_tok, nb, t)
    (dk_pre, dwa_lo, dg_lo, gw0, g['rwkv_w2'], ga0, g['rwkv_a2'], g['rwkv_g2'], gkk, gka) = block_bwd(
        f_rwkv_pre, "rwkv_pre_bwd", (1, ni), pre_ops, pre_outs, [dlw, dk2 + dk2_post, dna, dbb, dgate])
    g['rwkv_w0'], g['rwkv_a0'], g['rwkv_k_k'], g['rwkv_k_a'] = vec(gw0), vec(ga0), vec(gkk), vec(gka)
    dps = jnp.concatenate([dr + dr_post, dk_pre, dv + dv_post, dwa_lo, dg_lo], axis=1)
    dp_rwkv, gmix = block_bwd(f_shift_mix, "rwkv_shift_bwd", shift_grid, shift_ops, shift_outs, [dps])
    g['rwkv_mix'] = vec(gmix)

    dproj0 = jnp.concatenate([dp_rwkv, dz, dxbc, ddt_raw], axis=1)
    g['l0_w_in'] = mm(h0b, dproj0, "l0_proj_dw", ta=True)[:, :L0_COLS]
    grad_x = mm(dproj0, w_in0.T, "l0_proj_dx", add=dh0_res)
    return loss_part, grad_x.reshape(nb, t, d), g


MESH = pl.DeviceIdType.MESH
ANY = pl.BlockSpec(memory_space=pl.ANY)
AXES = ("x", "y", "c")


def _place():
    x, y, c = lax.axis_index("x"), lax.axis_index("y"), lax.axis_index("c")
    chips = [(1 - x, y), (x, 1 - y), (1 - x, 1 - y)]
    return x, y, c, chips


def _dma_sems(n):
    return pltpu.SemaphoreType.DMA((n,))


def gather_shards(shards):
    n = len(shards)

    def body(*refs):
        ins, outs = refs[:n], refs[n:2 * n]
        ici_send, ici_recv, d2d_send, d2d_recv, local_sems = refs[2 * n:]
        x, y, c, chips = _place()
        me = 2 * x + y
        pairs = list(enumerate(zip(ins, outs)))
        mine = [pltpu.make_async_copy(a, o.at[me], local_sems.at[k]) for k, (a, o) in pairs]
        for cp in mine:
            cp.start()

        def over_ici(k, j, a, o, slot, to):
            return pltpu.make_async_remote_copy(
                src_ref=a.at[c], dst_ref=o.at[slot, c], send_sem=ici_send.at[3 * k + j],
                recv_sem=ici_recv.at[3 * k + j], device_id=to, device_id_type=MESH)

        def to_sibling(k, j, o, slot, half):
            return pltpu.make_async_remote_copy(
                src_ref=o.at[slot, half], dst_ref=o.at[slot, half], send_sem=d2d_send.at[3 * k + j],
                recv_sem=d2d_recv.at[3 * k + j], device_id=(x, y, 1 - c), device_id_type=MESH)

        sends = [over_ici(k, j, a, o, me, (cx, cy, c)) for k, (a, o) in pairs for j, (cx, cy) in enumerate(chips)]
        for cp in sends:
            cp.start()
        passed = []
        for k, (a, o) in pairs:
            for j, (cx, cy) in enumerate(chips):
                over_ici(k, j, a, o, 2 * cx + cy, (cx, cy, c)).wait_recv()
                passed.append(to_sibling(k, j, o, 2 * cx + cy, c))
                passed[-1].start()
        for k, (a, o) in pairs:
            for j, (cx, cy) in enumerate(chips):
                to_sibling(k, j, o, 2 * cx + cy, 1 - c).wait_recv()
        for cp in sends + passed:
            cp.wait_send()
        for cp in mine:
            cp.wait()

    return pl.pallas_call(
        body, name="gather_shards", in_specs=[ANY] * n, out_specs=[ANY] * n,
        out_shape=[jax.ShapeDtypeStruct((N_SHARD,) + a.shape, a.dtype) for a in shards],
        scratch_shapes=[_dma_sems(3 * n), _dma_sems(3 * n), _dma_sems(3 * n), _dma_sems(3 * n), _dma_sems(n)],
    )(*shards)


def swap_halves(pieces):
    n = len(pieces)

    def body(*refs):
        ins, outs = refs[:n], refs[n:2 * n]
        send_sems, recv_sems = refs[2 * n:]
        x, y, c, _ = _place()
        cps = [pltpu.make_async_remote_copy(
            src_ref=a.at[:, 1 - c], dst_ref=o, send_sem=send_sems.at[k], recv_sem=recv_sems.at[k],
            device_id=(x, y, 1 - c), device_id_type=MESH) for k, (a, o) in enumerate(zip(ins, outs))]
        for cp in cps:
            cp.start()
        for cp in cps:
            cp.wait()

    return pl.pallas_call(
        body, name="swap_halves", in_specs=[ANY] * n, out_specs=[ANY] * n,
        out_shape=[jax.ShapeDtypeStruct((a.shape[0],) + a.shape[2:], a.dtype) for a in pieces],
        scratch_shapes=[_dma_sems(n), _dma_sems(n)],
    )(*pieces)


def scatter_to_chips(parts):
    n = len(parts)

    def body(*refs):
        ins, outs = refs[:n], refs[n:2 * n]
        send_sems, recv_sems, local_sems = refs[2 * n:]
        x, y, c, chips = _place()
        me = 2 * x + y
        mine = [pltpu.make_async_copy(a.at[me], o.at[me], local_sems.at[k]) for k, (a, o) in enumerate(zip(ins, outs))]
        for cp in mine:
            cp.start()
        sends = [pltpu.make_async_remote_copy(
            src_ref=a.at[2 * cx + cy], dst_ref=o.at[me], send_sem=send_sems.at[3 * k + j],
            recv_sem=recv_sems.at[3 * k + j], device_id=(cx, cy, c), device_id_type=MESH)
            for k, (a, o) in enumerate(zip(ins, outs)) for j, (cx, cy) in enumerate(chips)]
        for cp in sends:
            cp.start()
        for k, (a, o) in enumerate(zip(ins, outs)):
            for j, (cx, cy) in enumerate(chips):
                pltpu.make_async_remote_copy(
                    src_ref=a.at[me], dst_ref=o.at[2 * cx + cy], send_sem=send_sems.at[3 * k + j],
                    recv_sem=recv_sems.at[3 * k + j], device_id=(cx, cy, c), device_id_type=MESH).wait_recv()
        for cp in sends:
            cp.wait_send()
        for cp in mine:
            cp.wait()

    return pl.pallas_call(
        body, name="scatter_to_chips", in_specs=[ANY] * n, out_specs=[ANY] * n,
        out_shape=[jax.ShapeDtypeStruct(a.shape, a.dtype) for a in parts],
        scratch_shapes=[_dma_sems(3 * n), _dma_sems(3 * n), _dma_sems(n)],
    )(*parts)


def share_halves(bufs):
    n = len(bufs)

    def body(*refs):
        ins, outs = refs[:n], refs[n:2 * n]
        send_sems, recv_sems = refs[2 * n:]
        x, y, c, _ = _place()
        cps = [pltpu.make_async_remote_copy(
            src_ref=a.at[c], dst_ref=o.at[c], send_sem=send_sems.at[k], recv_sem=recv_sems.at[k],
            device_id=(x, y, 1 - c), device_id_type=MESH) for k, (a, o) in enumerate(zip(ins, outs))]
        for cp in cps:
            cp.start()
        for k, (a, o) in enumerate(zip(ins, outs)):
            cps[k].wait_send()
            pltpu.make_async_remote_copy(
                src_ref=a.at[c], dst_ref=o.at[1 - c], send_sem=send_sems.at[k], recv_sem=recv_sems.at[k],
                device_id=(x, y, 1 - c), device_id_type=MESH).wait_recv()

    return pl.pallas_call(
        body, name="share_halves", in_specs=[ANY] * n, out_specs=[ANY] * n,
        out_shape=[jax.ShapeDtypeStruct(a.shape, a.dtype) for a in bufs],
        input_output_aliases={k: k for k in range(n)},
        scratch_shapes=[_dma_sems(n), _dma_sems(n)],
    )(*bufs)


def pair_add(piece, recv, core, name, out_dtype):
    _, _, h, cdim = piece.shape
    tr = _rtile(h, cdim)

    def body(c_ref, a_ref, b_ref, o_ref):
        o_ref[...] = (a_ref[0] + b_ref[...]).astype(o_ref.dtype)

    spec = pl.BlockSpec((1, tr, cdim), lambda p, i, c_ref: (p, i, 0))
    return pl.pallas_call(
        body, name=name,
        grid_spec=pltpu.PrefetchScalarGridSpec(
            num_scalar_prefetch=1, grid=(N_SHARD, h // tr),
            in_specs=[pl.BlockSpec((1, 1, tr, cdim), lambda p, i, c_ref: (p, c_ref[0], i, 0)), spec],
            out_specs=spec),
        out_shape=jax.ShapeDtypeStruct((N_SHARD, h, cdim), out_dtype),
        compiler_params=_cparams(("parallel", "parallel")),
    )(core, piece, recv)


def chip_add(parts, core, name):
    _, h, cdim = parts.shape
    tr = _rtile(h, cdim, 1024 * 1024)

    def body(c_ref, p_ref, o_ref):
        p = [p_ref[s].astype(f32) for s in range(N_SHARD)]
        o_ref[0] = ((p[0] + p[1]) + p[2]) + p[3]

    return pl.pallas_call(
        body, name=name,
        grid_spec=pltpu.PrefetchScalarGridSpec(
            num_scalar_prefetch=1, grid=(h // tr,),
            in_specs=[pl.BlockSpec((N_SHARD, tr, cdim), lambda i, c_ref: (0, i, 0))],
            out_specs=pl.BlockSpec((1, tr, cdim), lambda i, c_ref: (c_ref[0], i, 0))),
        out_shape=jax.ShapeDtypeStruct((2, h, cdim), f32), compiler_params=_cparams(("parallel",)),
    )(core, parts)


def adamw(w, g, m, v, name):
    rows, cdim = w.shape
    tr = _rtile(rows, cdim, 1024 * 1024)

    def body(w_ref, g_ref, m_ref, v_ref, d_ref, nm_ref, nv_ref):
        gv = g_ref[...]
        m_new = ADAM_B1 * m_ref[...] + (1.0 - ADAM_B1) * gv
        v_new = ADAM_B2 * v_ref[...] + (1.0 - ADAM_B2) * jnp.square(gv)
        m_hat = m_new / (1.0 - ADAM_B1 ** ADAM_STEP)
        v_hat = v_new / (1.0 - ADAM_B2 ** ADAM_STEP)
        d_ref[...] = -ADAM_LR * (m_hat / (jnp.sqrt(v_hat) + ADAM_EPS) + ADAM_WD * w_ref[...])
        nm_ref[...] = m_new
        nv_ref[...] = v_new

    spec = pl.BlockSpec((tr, cdim), lambda i: (i, 0))
    return pl.pallas_call(body, name=name, grid=(rows // tr,), in_specs=[spec] * 4, out_specs=[spec] * 3,
                          out_shape=[jax.ShapeDtypeStruct(w.shape, f32)] * 3,
                          compiler_params=_cparams(("parallel",)))(w, g, m, v)


SMALL_MULTIPLE = 16 * LANE


def _pack_flat(parts, multiple=SMALL_MULTIPLE):
    flat = jnp.concatenate([p.reshape(-1) for p in parts])
    pad = (-flat.shape[0]) % multiple
    return jnp.pad(flat, (0, pad)).reshape(-1, LANE)


def _unpack_flat(buf, shapes):
    flat = buf.reshape(-1)
    out, off = [], 0
    for s in shapes:
        cnt = int(np.prod(s))
        out.append(flat[off:off + cnt].reshape(s))
        off += cnt
    return out


def _full_from_shards(name, gathered):
    if name in COL_SHARDED:
        return jnp.concatenate([gathered[s] for s in range(N_SHARD)], axis=1)
    return gathered.reshape(-1, gathered.shape[2])


def _pieces(name, grad):
    if name in COL_SHARDED:
        r, cdim = grad.shape
        return grad.reshape(r, N_SHARD, cdim // N_SHARD).transpose(1, 0, 2)
    return grad.reshape(N_SHARD, grad.shape[0] // N_SHARD, grad.shape[1])


def _small_pieces(name, grad):
    if name in COL_SHARDED or name in ROW_SHARDED:
        return _pieces(name, grad).reshape(N_SHARD, -1)
    return jnp.broadcast_to(grad.reshape(1, -1), (N_SHARD, grad.size))


def kernel(x, positions, l0_w_in, rwkv_mix, rwkv_w0, rwkv_w2, rwkv_a0, rwkv_a2, rwkv_g2, rwkv_k_k, rwkv_k_a, rwkv_r_k, rwkv_ln_g, rwkv_ln_b, ssm_conv_w, ssm_conv_b, ssm_dt_bias, ssm_a_log, ssm_d, ssm_norm_g, l0_w_out, l0_ln1_g, l0_ln1_b, ffn0_w_up, ffn0_conv_w, ffn0_conv_b, ffn0_w_down, l0_ln2_g, l0_ln2_b, l1_w_in, mla_q_norm_g, mla_w_uq, mla_kv_norm_g, mla_w_ukv, l1_w_out, l1_ln1_g, l1_ln1_b, ffn1_w_up, ffn1_conv_w, ffn1_conv_b, ffn1_w_down, l1_ln2_g, l1_ln2_b, loss_target, m_l0_w_in, m_rwkv_mix, m_rwkv_w0, m_rwkv_w2, m_rwkv_a0, m_rwkv_a2, m_rwkv_g2, m_rwkv_k_k, m_rwkv_k_a, m_rwkv_r_k, m_rwkv_ln_g, m_rwkv_ln_b, m_ssm_conv_w, m_ssm_conv_b, m_ssm_dt_bias, m_ssm_a_log, m_ssm_d, m_ssm_norm_g, m_l0_w_out, m_l0_ln1_g, m_l0_ln1_b, m_ffn0_w_up, m_ffn0_conv_w, m_ffn0_conv_b, m_ffn0_w_down, m_l0_ln2_g, m_l0_ln2_b, m_l1_w_in, m_mla_q_norm_g, m_mla_w_uq, m_mla_kv_norm_g, m_mla_w_ukv, m_l1_w_out, m_l1_ln1_g, m_l1_ln1_b, m_ffn1_w_up, m_ffn1_conv_w, m_ffn1_conv_b, m_ffn1_w_down, m_l1_ln2_g, m_l1_ln2_b, v_l0_w_in, v_rwkv_mix, v_rwkv_w0, v_rwkv_w2, v_rwkv_a0, v_rwkv_a2, v_rwkv_g2, v_rwkv_k_k, v_rwkv_k_a, v_rwkv_r_k, v_rwkv_ln_g, v_rwkv_ln_b, v_ssm_conv_w, v_ssm_conv_b, v_ssm_dt_bias, v_ssm_a_log, v_ssm_d, v_ssm_norm_g, v_l0_w_out, v_l0_ln1_g, v_l0_ln1_b, v_ffn0_w_up, v_ffn0_conv_w, v_ffn0_conv_b, v_ffn0_w_down, v_l0_ln2_g, v_l0_ln2_b, v_l1_w_in, v_mla_q_norm_g, v_mla_w_uq, v_mla_kv_norm_g, v_mla_w_ukv, v_l1_w_out, v_l1_ln1_g, v_l1_ln1_b, v_ffn1_w_up, v_ffn1_conv_w, v_ffn1_conv_b, v_ffn1_w_down, v_l1_ln2_g, v_l1_ln2_b):
    args = locals()
    w_loc = {n: args[n] for n in WEIGHTS}
    m_loc = {n: args["m_" + n] for n in WEIGHTS}
    v_loc = {n: args["v_" + n] for n in WEIGHTS}
    core = lax.axis_index("c").astype(jnp.int32).reshape(1)

    small_sharded = [n for n in SMALL if n in COL_SHARDED]
    halves = lambda a: a.reshape(2, a.shape[0] // 2, a.shape[1])
    gathered = gather_shards([halves(w_loc[n].astype(bf16)) for n in BIG]
                             + [halves(_pack_flat([w_loc[n] for n in small_sharded]))])
    gathered = [got.reshape(N_SHARD, -1, got.shape[3]) for got in gathered]
    w_full = dict(w_loc)
    for n, got in zip(BIG, gathered):
        w_full[n] = _full_from_shards(n, got)
    per_shard = [_unpack_flat(gathered[-1][s], [w_loc[n].shape for n in small_sharded]) for s in range(N_SHARD)]
    for k, n in enumerate(small_sharded):
        w_full[n] = jnp.concatenate([per_shard[s][k] for s in range(N_SHARD)], axis=1)

    loss_part, grad_x, g_full = local_step(x, positions, loss_target, w_full)
    loss = lax.psum(loss_part[0, 0], AXES)

    small_flat = jnp.concatenate([_small_pieces(n, g_full[n]) for n in SMALL], axis=1)
    pad = (-small_flat.shape[1]) % SMALL_MULTIPLE
    small_pieces = jnp.pad(small_flat, ((0, 0), (0, pad))).reshape(N_SHARD, -1, LANE)
    units = BIG + ['small']
    pieces = [_pieces(n, g_full[n]) for n in BIG] + [small_pieces]
    pieces = [p.reshape(N_SHARD, 2, p.shape[1] // 2, p.shape[2]) for p in pieces]
    from_sibling = swap_halves(pieces)
    pair = [pair_add(p, r, core, "pair_add_" + n, f32 if n == 'small' else bf16)
            for n, p, r in zip(units, pieces, from_sibling)]
    from_chips = scatter_to_chips(pair)
    both = share_halves([chip_add(p, core, "chip_add_" + n) for n, p in zip(units, from_chips)])
    reduced = [b.reshape(-1, b.shape[2]) for b in both]

    out = {}
    for n, gred in zip(BIG, reduced):
        out[n] = (gred,) + tuple(adamw(w_loc[n], gred, m_loc[n], v_loc[n], "adamw_" + n))
    shapes = [w_loc[n].shape for n in SMALL]
    packs = [_pack_flat([d[n] for n in SMALL]) for d in (w_loc, m_loc, v_loc)]
    small_res = (reduced[-1],) + tuple(adamw(packs[0], reduced[-1], packs[1], packs[2], "adamw_small"))
    small_unpacked = [_unpack_flat(b, shapes) for b in small_res]
    for k, n in enumerate(SMALL):
        out[n] = tuple(u[k] for u in small_unpacked)
    return (loss, grad_x, *[out[n][0] for n in WEIGHTS], *[out[n][1] for n in WEIGHTS],
            *[out[n][2] for n in WEIGHTS], *[out[n][3] for n in WEIGHTS])
```

```python
import functools

import numpy as np
import jax
import jax.numpy as jnp
from jax import lax
from jax.experimental import pallas as pl
from jax.experimental.pallas import tpu as pltpu

f32 = jnp.float32
bf16 = jnp.bfloat16
HI = lax.Precision.HIGHEST
MID = lax.Precision.HIGH

D_MODEL = 1024
HEAD_DIM = 64
N_HEADS = 8
RWKV_COLS = 1792
RWKV_GN_EPS = 64e-5
SSM_STATE = 128
SSM_CHUNK = 128
L0_COLS = 3336
L0_PAD = 3456
L1_COLS = 1952
L1_PAD = 2048
MLA_ROPE = 32
ROPE_THETA = 10000.0
D_FF = 2816
DEPTH = 2
ALPHA = (2 * DEPTH) ** 0.25
ADAM_LR = 0.001
ADAM_B1 = 0.9
ADAM_B2 = 0.999
ADAM_EPS = 1e-08
ADAM_WD = 0.01
ADAM_STEP = 10
RWKV_CHUNK = 64
RWKV_HEADS_PER_STEP = 8
LANE = 128
SUBLANE = 8
VMEM_LIMIT = 56 * 1024 * 1024

WEIGHTS = ['l0_w_in', 'rwkv_mix', 'rwkv_w0', 'rwkv_w2', 'rwkv_a0', 'rwkv_a2', 'rwkv_g2', 'rwkv_k_k', 'rwkv_k_a',
           'rwkv_r_k', 'rwkv_ln_g', 'rwkv_ln_b', 'ssm_conv_w', 'ssm_conv_b', 'ssm_dt_bias', 'ssm_a_log', 'ssm_d',
           'ssm_norm_g', 'l0_w_out', 'l0_ln1_g', 'l0_ln1_b', 'ffn0_w_up', 'ffn0_conv_w', 'ffn0_conv_b',
           'ffn0_w_down', 'l0_ln2_g', 'l0_ln2_b', 'l1_w_in', 'mla_q_norm_g', 'mla_w_uq', 'mla_kv_norm_g',
           'mla_w_ukv', 'l1_w_out', 'l1_ln1_g', 'l1_ln1_b', 'ffn1_w_up', 'ffn1_conv_w', 'ffn1_conv_b',
           'ffn1_w_down', 'l1_ln2_g', 'l1_ln2_b']
COL_SHARDED = ['l0_w_in', 'rwkv_w2', 'rwkv_a2', 'rwkv_g2', 'ssm_conv_w', 'ffn0_w_up', 'ffn0_conv_w', 'l1_w_in',
               'mla_w_uq', 'mla_w_ukv', 'ffn1_w_up', 'ffn1_conv_w']
ROW_SHARDED = ['l0_w_out', 'ffn0_w_down', 'l1_w_out', 'ffn1_w_down']
BIG = ['l0_w_in', 'l0_w_out', 'ffn0_w_up', 'ffn0_w_down', 'l1_w_in', 'l1_w_out', 'ffn1_w_up', 'ffn1_w_down']
SMALL = [n for n in WEIGHTS if n not in BIG]
N_SHARD = 4


def _cparams(sem):
    return pltpu.CompilerParams(dimension_semantics=sem, vmem_limit_bytes=VMEM_LIMIT)


def _dg(a, b, ca, cb, prec=None):
    return lax.dot_general(a, b, (((ca,), (cb,)), ((), ())), precision=prec, preferred_element_type=f32)


def hdot(a, b):
    return _dg(a, b, 1, 0, HI)


def mdot(a, b):
    return _dg(a, b, 1, 0, MID)


def mdot_nt(a, b):
    return _dg(a, b, 1, 1, MID)


def mdot_tn(a, b):
    return _dg(a, b, 0, 0, MID)


def _b(x):
    return x.astype(bf16)


@jax.custom_vjp
def bdot(x, w):
    return _dg(_b(x), _b(w), 1, 0)


def _bdot_fwd(x, w):
    return bdot(x, w), (x, w)


def _bdot_bwd(res, g):
    x, w = res
    return _dg(_b(g), _b(w), 1, 1).astype(x.dtype), _dg(_b(x), _b(g), 0, 0).astype(w.dtype)


bdot.defvjp(_bdot_fwd, _bdot_bwd)


@jax.custom_vjp
def bdot_nt(x, y):
    return _dg(_b(x), _b(y), 1, 1)


def _bdot_nt_fwd(x, y):
    return bdot_nt(x, y), (x, y)


def _bdot_nt_bwd(res, g):
    x, y = res
    return _dg(_b(g), _b(y), 1, 0), _dg(_b(g), _b(x), 0, 0)


bdot_nt.defvjp(_bdot_nt_fwd, _bdot_nt_bwd)


@jax.custom_vjp
def bdot_tn(x, y):
    return _dg(_b(x), _b(y), 0, 0)


def _bdot_tn_fwd(x, y):
    return bdot_tn(x, y), (x, y)


def _bdot_tn_bwd(res, g):
    x, y = res
    return _dg(_b(y), _b(g), 1, 1), _dg(_b(x), _b(g), 1, 0)


bdot_tn.defvjp(_bdot_tn_fwd, _bdot_tn_bwd)


def _sigmoid(x):
    return 1.0 / (1.0 + jnp.exp(-x))


@jax.custom_vjp
def softplus(x):
    e = jnp.exp(-jnp.abs(x))
    u = 1.0 + e
    log1p = jnp.where(u == 1.0, e, jnp.log(u) * e / jnp.where(u == 1.0, 1.0, u - 1.0))
    return jnp.maximum(x, 0.0) + log1p


def _softplus_fwd(x):
    return softplus(x), x


def _softplus_bwd(x, g):
    return (g * _sigmoid(x),)


softplus.defvjp(_softplus_fwd, _softplus_bwd)


@jax.custom_vjp
def softplus_abs(x):
    return jnp.maximum(x, 0.0) + jnp.log(1.0 + jnp.exp(-jnp.abs(x)))


def _softplus_abs_fwd(x):
    return softplus_abs(x), x


softplus_abs.defvjp(_softplus_abs_fwd, _softplus_bwd)


def _two_pass(x, m):
    hi = _b(x)
    lo = _b(x - hi.astype(f32))
    m16 = _b(m)
    return _dg(hi, m16, 1, 0) + _dg(lo, m16, 1, 0)


def _upper(n):
    return (_iota2((n, n), 0) > _iota2((n, n), 1)).astype(f32)


@jax.custom_vjp
def suffix_sums(x):
    return _two_pass(x, _upper(x.shape[1]))


def _suffix_sums_fwd(x):
    return suffix_sums(x), None


def _suffix_sums_bwd(_, g):
    return (_two_pass(g, _upper(g.shape[1]).T),)


suffix_sums.defvjp(_suffix_sums_fwd, _suffix_sums_bwd)


def silu(x):
    return x * _sigmoid(x)


def _shift_rows(x, k, up):
    if k == 0:
        return x
    t = x.shape[0]
    rows = lax.broadcasted_iota(jnp.int32, x.shape, 0)
    if up:
        return jnp.where(rows < t - k, pltpu.roll(x, t - k, 0), 0.0)
    return jnp.where(rows >= k, pltpu.roll(x, k, 0), 0.0)


@functools.partial(jax.custom_vjp, nondiff_argnums=(1,))
def shift_down(x, k):
    return _shift_rows(x, k, False)


def _shift_down_fwd(x, k):
    return _shift_rows(x, k, False), None


def _shift_down_bwd(k, _, g):
    return (_shift_rows(g, k, True),)


shift_down.defvjp(_shift_down_fwd, _shift_down_bwd)


@functools.partial(jax.custom_vjp, nondiff_argnums=(1,))
def lane_roll(x, s):
    return pltpu.roll(x, s % x.shape[1], 1)


def _lane_roll_fwd(x, s):
    return lane_roll(x, s), None


def _lane_roll_bwd(s, _, g):
    return (pltpu.roll(g, (-s) % g.shape[1], 1),)


lane_roll.defvjp(_lane_roll_fwd, _lane_roll_bwd)


def rot_half32(x):
    first = (lax.broadcasted_iota(jnp.int32, x.shape, 1) % MLA_ROPE) < (MLA_ROPE // 2)
    return jnp.where(first, -lane_roll(x, -(MLA_ROPE // 2)), lane_roll(x, MLA_ROPE // 2))


def _iota2(shape, axis):
    return lax.broadcasted_iota(jnp.int32, shape, axis)


class Op:
    def __init__(self, arr, block, imap, diff=True, acc=None, gshape=None, gimap=None, gdtype=f32):
        self.arr, self.block, self.imap, self.diff, self.acc = arr, tuple(block), imap, diff, acc
        self.gshape = tuple(arr.shape) if gshape is None else tuple(gshape)
        self.gimap = imap if gimap is None else gimap
        self.gdtype = gdtype


class Out:
    def __init__(self, shape, block, imap, dtype=f32):
        self.shape, self.block, self.imap, self.dtype = tuple(shape), tuple(block), imap, dtype


def block_fwd(fn, name, grid, ops, outs):
    n_in = len(ops)

    def body(*refs):
        vals = [r[...] for r in refs[:n_in]]
        res = fn(*vals)
        for r, v in zip(refs[n_in:], res):
            r[...] = v.astype(r.dtype)

    res = pl.pallas_call(
        body, name=name, grid=grid,
        in_specs=[pl.BlockSpec(o.block, o.imap) for o in ops],
        out_specs=[pl.BlockSpec(o.block, o.imap) for o in outs],
        out_shape=[jax.ShapeDtypeStruct(o.shape, o.dtype) for o in outs],
        compiler_params=_cparams(("arbitrary", "arbitrary")),
    )(*[o.arr for o in ops])
    return tuple(res)


def block_bwd(fn, name, grid, ops, outs, douts):
    n_in, n_out = len(ops), len(outs)
    dix = [k for k, o in enumerate(ops) if o.diff]

    def body(*refs):
        vals = [r[...] for r in refs[:n_in]]
        dvals = tuple(r[...] for r in refs[n_in:n_in + n_out])
        grefs = refs[n_in + n_out:]

        def f(*d):
            full = list(vals)
            for k, v in zip(dix, d):
                full[k] = v
            return tuple(fn(*full))

        _, vjp = jax.vjp(f, *[vals[k] for k in dix])
        grads = vjp(dvals)
        j, i = pl.program_id(0), pl.program_id(1)
        for k, gref, g in zip(dix, grefs, grads):
            acc = ops[k].acc
            if acc is None:
                gref[...] = g.astype(gref.dtype)
            else:
                first = (i == 0) if acc == 'i' else jnp.logical_and(i == 0, j == 0)

                @pl.when(first)
                def _():
                    gref[...] = g

                @pl.when(jnp.logical_not(first))
                def _():
                    gref[...] += g

    gspecs = [pl.BlockSpec(ops[k].block, ops[k].gimap) for k in dix]
    gshapes = [jax.ShapeDtypeStruct(ops[k].gshape, ops[k].gdtype) for k in dix]
    res = pl.pallas_call(
        body, name=name, grid=grid,
        in_specs=[pl.BlockSpec(o.block, o.imap) for o in ops] + [pl.BlockSpec(o.block, o.imap) for o in outs],
        out_specs=gspecs, out_shape=gshapes,
        compiler_params=_cparams(("arbitrary", "arbitrary")),
    )(*[o.arr for o in ops], *douts)
    return tuple(res)


def _rows(arr, tm, diff=True, gdtype=f32):
    return Op(arr, (tm, arr.shape[1]), lambda j, i: (i, 0), diff=diff, gdtype=gdtype)


def _param(arr, diff=True):
    return Op(arr, arr.shape, lambda j, i: (0,) * arr.ndim, diff=diff, acc='ij')


def _rows_out(n, c, tm, dtype=f32):
    return Out((n, c), (tm, c), lambda j, i: (i, 0), dtype)


def _cols(arr, t, tc, off=0, width=None, gdtype=f32):
    width = arr.shape[1] if width is None else width
    return Op(arr, (t, tc), lambda j, i: (i, j + off), gshape=(arr.shape[0], width), gimap=lambda j, i: (i, j),
              gdtype=gdtype)


def _cparam(arr, tc):
    return Op(arr, (arr.shape[0], tc), lambda j, i: (0, j), acc='i')


def _colblock(arr, tm, off, width, gdtype=f32):
    return Op(arr, (tm, width), lambda j, i: (i, off // width), gshape=(arr.shape[0], width),
              gimap=lambda j, i: (i, 0), gdtype=gdtype)


def _tile(n, cap):
    best = None
    for t in range(LANE, min(n, cap) + 1, LANE):
        if n % t == 0:
            best = t
    return n if best is None else best


def _rtile(rows, cols, cap_bytes=2 * 1024 * 1024):
    best = None
    for t in range(SUBLANE, rows + 1, SUBLANE):
        if rows % t == 0 and t * cols * 4 <= cap_bytes:
            best = t
    return rows if best is None else best


def mm(a, b, name, ta=False, add=None):
    m = a.shape[1] if ta else a.shape[0]
    kd = a.shape[0] if ta else a.shape[1]
    n = b.shape[1]
    tm, tn = _tile(m, 1408), _tile(n, 1408)
    tk = kd if kd <= 2048 else _tile(kd, 1408)
    nk = kd // tk
    ca = 0 if ta else 1

    def body(*refs):
        if add is None:
            a_ref, b_ref, o_ref, acc = refs
        else:
            a_ref, b_ref, add_ref, o_ref, acc = refs
        k = pl.program_id(2)

        @pl.when(k == 0)
        def _():
            acc[...] = jnp.zeros_like(acc)

        acc[...] += _dg(_b(a_ref[...]), _b(b_ref[...]), ca, 0)

        @pl.when(k == nk - 1)
        def _():
            o_ref[...] = acc[...] if add is None else acc[...] + add_ref[...]

    a_spec = pl.BlockSpec((tk, tm), lambda i, j, k: (k, i)) if ta else pl.BlockSpec((tm, tk), lambda i, j, k: (i, k))
    b_spec = pl.BlockSpec((tk, tn), lambda i, j, k: (k, j))
    o_spec = pl.BlockSpec((tm, tn), lambda i, j, k: (i, j))
    args, specs = [a, b], [a_spec, b_spec]
    if add is not None:
        args.append(add)
        specs.append(o_spec)
    return pl.pallas_call(
        body, name=name, grid=(m // tm, n // tn, nk), in_specs=specs, out_specs=o_spec,
        out_shape=jax.ShapeDtypeStruct((m, n), f32), scratch_shapes=[pltpu.VMEM((tm, tn), f32)],
        compiler_params=_cparams(("parallel", "parallel", "arbitrary")),
    )(*args)


def f_ln(h, y, g, b):
    x = ALPHA * h + y
    mu = jnp.mean(x, axis=-1, keepdims=True)
    xc = x - mu
    var = jnp.mean(xc * xc, axis=-1, keepdims=True)
    return (xc * lax.rsqrt(var + 1e-5) * g + b,)


def f_shift_mix(p, mix):
    return (p + (shift_down(p, 1) - p) * mix,)


def f_rwkv_pre(k, wa_lo, g_lo, w0, w2, a0, a2, g2, k_k, k_a, gh):
    w_lo, a_lo = wa_lo[:, :64], wa_lo[:, 64:]
    log_w = -softplus(-(w0 + bdot(jnp.tanh(w_lo), w2))) - 0.5
    lw = -jnp.exp(log_w)
    a = _sigmoid(a0 + bdot(a_lo, a2))
    g = bdot(_sigmoid(g_lo), g2)
    kk = k * k_k
    kk = kk / jnp.maximum(jnp.sqrt(hdot(kk * kk, gh)), 1e-12)
    k2 = k * (1.0 + (a - 1.0) * k_a)
    return lw, k2, -kk, kk * a, g


def f_rwkv_post(y, r, k2, v, g, ln_g, ln_b, r_k, gh):
    mu = hdot(y, gh) * (1.0 / HEAD_DIM)
    yc = y - mu
    var = hdot(yc * yc, gh) * (1.0 / HEAD_DIM)
    yn = yc * lax.rsqrt(var + RWKV_GN_EPS) * ln_g + ln_b
    bonus = hdot(r * k2 * r_k, gh) * v
    return ((yn + bonus) * g,)


def f_conv4_silu(x, w0, w1, w2, w3, b):
    y = b + shift_down(x, 3) * w0 + shift_down(x, 2) * w1 + shift_down(x, 1) * w2 + x * w3
    return (silu(y),)


def f_ssm_post(y, z, norm_g, gg):
    u = y * silu(z)
    ms = hdot(u * u, gg) * (1.0 / 256.0)
    return (u * lax.rsqrt(ms + 1e-5) * norm_g,)


def f_ffn_act(gate, up, w0, w1, w2, b):
    gc = b + shift_down(gate, 2) * w0 + shift_down(gate, 1) * w1 + gate * w2
    return (silu(gc) * up,)


def _rms(x, g, eps=1e-6):
    return x * lax.rsqrt(jnp.mean(x * x, axis=-1, keepdims=True) + eps) * g


def f_mla_pre(c_q, c_kv, kpe, pos, q_g, w_qn, w_qr, kv_g, w_ukv, inv_q, inv_k):
    qn_in = _rms(c_q, q_g)
    q_nope = bdot(qn_in, w_qn)
    qr = bdot(qn_in, w_qr)
    kv = bdot(_rms(c_kv, kv_g), w_ukv)
    ang_q = pos * inv_q
    ang_k = pos * inv_k
    return (q_nope, qr * jnp.cos(ang_q) + rot_half32(qr) * jnp.sin(ang_q), kv,
            kpe * jnp.cos(ang_k) + rot_half32(kpe) * jnp.sin(ang_k))


def rwkv_chunk(s0, r, lw, k, v, a, b):
    hs = range(len(r))
    l = r[0].shape[0]
    ri, ci = _iota2((l, l), 0), _iota2((l, l), 1)
    strict, incl = ri > ci, ri >= ci
    tri, eye = incl.astype(f32), (ri == ci).astype(f32)
    last = (_iota2((l, 1), 0) == l - 1).astype(f32)
    c = [hdot(tri, lw[h]) for h in hs]
    at = [a[h] * jnp.exp(c[h] - lw[h]) for h in hs]
    wi = [jnp.exp(-c[h]) for h in hs]
    bt = [b[h] * wi[h] for h in hs]
    kt = [k[h] * wi[h] for h in hs]
    rt = [r[h] * jnp.exp(c[h]) for h in hs]
    nab = [jnp.where(strict, mdot_nt(at[h], bt[h]), 0.0) for h in hs]
    nak = [jnp.where(strict, mdot_nt(at[h], kt[h]), 0.0) for h in hs]
    g = [mdot_nt(at[h], s0[h]) + mdot(nak[h], v[h]) for h in hs]
    x = [eye + nab[h] for h in hs]
    p = [mdot(nab[h], nab[h]) for h in hs]
    steps = max(1, (l - 1).bit_length()) - 1
    for it in range(steps):
        x = [x[h] + mdot(p[h], x[h]) for h in hs]
        if it < steps - 1:
            p = [mdot(p[h], p[h]) for h in hs]
    u = [mdot(x[h], g[h]) for h in hs]
    mrb = [jnp.where(incl, mdot_nt(rt[h], bt[h]), 0.0) for h in hs]
    mrk = [jnp.where(incl, mdot_nt(rt[h], kt[h]), 0.0) for h in hs]
    y = [mdot_nt(rt[h], s0[h]) + mdot(mrb[h], u[h]) + mdot(mrk[h], v[h]) for h in hs]
    s1 = [(s0[h] + mdot_tn(u[h], bt[h]) + mdot_tn(v[h], kt[h])) * jnp.exp(jnp.sum(c[h] * last, axis=0, keepdims=True))
          for h in hs]
    return y, s1


def ssd_chunk(xs, bm, cm, dt_raw, s_in, dt_bias, a_log, d_skip, e_heads):
    l = xs.shape[0]
    ri, ci = _iota2((l, l), 0), _iota2((l, l), 1)
    incl = ri >= ci
    tri = incl.astype(f32)
    dt = softplus(dt_raw + dt_bias)
    a128 = dt * (-jnp.exp(a_log))
    lane0 = (_iota2((1, HEAD_DIM), 1) == 0).astype(f32)
    last = (_iota2((l, 1), 0) == l - 1).astype(f32)
    hs = range(N_HEADS)
    group = lambda m, g: m[:, g * SSM_STATE:(g + 1) * SSM_STATE]
    cb = [bdot_nt(group(cm, g), group(bm, g)) for g in range(2)]
    e_all = jnp.concatenate(e_heads, axis=1)
    dt_all = hdot(dt, e_all)
    ac_all = hdot(tri, hdot(a128, e_all))
    xd_all = xs * dt_all
    skip_all = xs * hdot(jnp.broadcast_to(d_skip, (l, LANE)), e_all)
    ac = [ac_all[:, _head(h)] for h in hs]
    xd = [xd_all[:, _head(h)] for h in hs]
    col = [jnp.broadcast_to(jnp.sum(ac[h] * lane0, axis=1, keepdims=True), (l, l)) for h in hs]
    decay = [jnp.exp(jnp.where(incl, col[h] - col[h].T, -1e30)) for h in hs]
    y_diag = [bdot(cb[h // 4] * decay[h], xd[h]) for h in hs]
    a_tot = [jnp.sum(ac[h] * last, axis=0, keepdims=True) for h in hs]
    y_off = [jnp.exp(ac[h]) * bdot(group(cm, h // 4), s_in[h]) for h in hs]
    s_out = [jnp.exp(a_tot[h]) * s_in[h] + bdot_tn(group(bm, h // 4), xd[h] * jnp.exp(a_tot[h] - ac[h])) for h in hs]
    return jnp.concatenate([y_diag[h] + y_off[h] for h in hs], axis=1) + skip_all, s_out


SB_KEYS = LANE
MLA_KEYS = 256


def sb_tile(q, k, v, run, q0, k0):
    bq, kb = q.shape[0], k.shape[0]
    z = bdot_nt(q, k) * HEAD_DIM ** -0.5
    strict = (k0 + _iota2((bq, kb), 1)) < (q0 + _iota2((bq, kb), 0))
    lk = jnp.where(strict, -softplus_abs(z), 0.0)
    log_att = z + lk + suffix_sums(lk) + run
    att = jnp.where(strict, jnp.exp(jnp.where(strict, log_att, 0.0)), 0.0)
    return bdot(att, v), jnp.sum(lk, axis=1, keepdims=True)


def mla_scores(qn, qp, kn, kp, q0, k0):
    bq, kb = qn.shape[0], kn.shape[0]
    s = (bdot_nt(qn, kn) + bdot_nt(qp, kp)) * (HEAD_DIM + MLA_ROPE) ** -0.5
    causal = (k0 + _iota2((bq, kb), 1)) <= (q0 + _iota2((bq, kb), 0))
    return jnp.where(causal, s, -1e30), causal


def mla_tile_loss(qn, qp, kn, kp, v, do, lse, dsum, q0, k0):
    s, causal = mla_scores(qn, qp, kn, kp, q0, k0)
    p = jnp.where(causal, jnp.exp(s - lse), 0.0)
    return jnp.sum(do * bdot(p, v)) - jnp.sum(dsum * jnp.sum(p, axis=1, keepdims=True))


def _head(h):
    return slice(h * HEAD_DIM, (h + 1) * HEAD_DIM)


def _rwkv_specs(nc, rev):
    hp = RWKV_HEADS_PER_STEP
    w = hp * HEAD_DIM
    chunk = (lambda c: nc - 1 - c) if rev else (lambda c: c)
    tok = lambda off: pl.BlockSpec((RWKV_CHUNK, w), lambda b, g, c: (b * nc + chunk(c), off // w + g))
    st = pl.BlockSpec((1, hp, HEAD_DIM, HEAD_DIM), lambda b, g, c: ((b * (N_HEADS // hp) + g) * nc + chunk(c), 0, 0, 0))
    return tok, st


def rwkv_scan_fwd(ps, lw, k2, na, bb, nb, t):
    hp, nc = RWKV_HEADS_PER_STEP, t // RWKV_CHUNK
    ng = N_HEADS // hp
    tok, st = _rwkv_specs(nc, False)

    def body(r_ref, v_ref, lw_ref, k_ref, a_ref, b_ref, y_ref, s0_ref, s):
        @pl.when(pl.program_id(2) == 0)
        def _():
            s[...] = jnp.zeros_like(s)

        s0_ref[0] = s[...]
        heads = lambda ref: [ref[:, _head(h)] for h in range(hp)]
        y, s1 = rwkv_chunk([s[h] for h in range(hp)], heads(r_ref), heads(lw_ref), heads(k_ref), heads(v_ref),
                           heads(a_ref), heads(b_ref))
        for h in range(hp):
            y_ref[:, _head(h)] = y[h]
            s[h] = s1[h]

    return pl.pallas_call(
        body, name="rwkv_scan_fwd", grid=(nb, ng, nc),
        in_specs=[tok(0), tok(1024), tok(0), tok(0), tok(0), tok(0)], out_specs=[tok(0), st],
        out_shape=[jax.ShapeDtypeStruct((nb * t, N_HEADS * HEAD_DIM), f32),
                   jax.ShapeDtypeStruct((nb * ng * nc, hp, HEAD_DIM, HEAD_DIM), f32)],
        scratch_shapes=[pltpu.VMEM((hp, HEAD_DIM, HEAD_DIM), f32)],
        compiler_params=_cparams(("parallel", "parallel", "arbitrary")),
    )(ps, ps, lw, k2, na, bb)


def rwkv_scan_bwd(s0, ps, lw, k2, na, bb, dy, nb, t):
    hp, nc = RWKV_HEADS_PER_STEP, t // RWKV_CHUNK
    ng = N_HEADS // hp
    tok, st = _rwkv_specs(nc, True)

    def body(s0_ref, r_ref, v_ref, lw_ref, k_ref, a_ref, b_ref, dy_ref, dr, dlw, dk, dv, da, db, ds):
        @pl.when(pl.program_id(2) == 0)
        def _():
            ds[...] = jnp.zeros_like(ds)

        heads = lambda ref: [ref[:, _head(h)] for h in range(hp)]
        _, vjp = jax.vjp(rwkv_chunk, [s0_ref[0, h] for h in range(hp)], heads(r_ref), heads(lw_ref), heads(k_ref),
                         heads(v_ref), heads(a_ref), heads(b_ref))
        g = vjp((heads(dy_ref), [ds[h] for h in range(hp)]))
        for h in range(hp):
            ds[h] = g[0][h]
            for ref, val in zip((dr, dlw, dk, dv, da, db), g[1:]):
                ref[:, _head(h)] = val[h]

    return pl.pallas_call(
        body, name="rwkv_scan_bwd", grid=(nb, ng, nc),
        in_specs=[st, tok(0), tok(1024), tok(0), tok(0), tok(0), tok(0), tok(0)], out_specs=[tok(0)] * 6,
        out_shape=[jax.ShapeDtypeStruct((nb * t, N_HEADS * HEAD_DIM), f32)] * 6,
        scratch_shapes=[pltpu.VMEM((hp, HEAD_DIM, HEAD_DIM), f32)],
        compiler_params=_cparams(("parallel", "parallel", "arbitrary")),
    )(s0, ps, ps, lw, k2, na, bb, dy)


def _ssd_specs(nb, nch, rev):
    def row(b, c):
        return b * nch + (nch - 1 - c if rev else c)

    l = SSM_CHUNK
    xs = pl.BlockSpec((l, 512), lambda b, c: (row(b, c), 0))
    bm = pl.BlockSpec((l, 256), lambda b, c: (row(b, c), 2))
    cm = pl.BlockSpec((l, 256), lambda b, c: (row(b, c), 3))
    dt = pl.BlockSpec((l, LANE), lambda b, c: (row(b, c), (L0_PAD - LANE) // LANE))
    st = pl.BlockSpec((1, 1, N_HEADS, SSM_STATE, HEAD_DIM), lambda b, c: (b, (nch - 1 - c if rev else c), 0, 0, 0))
    par = pl.BlockSpec((1, LANE), lambda b, c: (0, 0))
    eh = pl.BlockSpec((N_HEADS, LANE, HEAD_DIM), lambda b, c: (0, 0, 0))
    return xs, bm, cm, dt, st, par, eh, row


def ssd_fwd(xbc_act, proj0, dt_bias, a_log, d_skip, e_heads, nb, t):
    nch = t // SSM_CHUNK
    n_tok = nb * t
    xs, bm, cm, dt, st, par, eh, row = _ssd_specs(nb, nch, False)

    def body(x_ref, b_ref, c_ref, dt_ref, db_ref, al_ref, dsk_ref, e_ref, y_ref, st_ref, s):
        @pl.when(pl.program_id(1) == 0)
        def _():
            s[...] = jnp.zeros_like(s)

        st_ref[0, 0] = s[...]
        y, s_out = ssd_chunk(x_ref[...], b_ref[...], c_ref[...], dt_ref[...], [s[h] for h in range(N_HEADS)],
                             db_ref[...], al_ref[...], dsk_ref[...], [e_ref[h] for h in range(N_HEADS)])
        y_ref[...] = y
        for h in range(N_HEADS):
            s[h] = s_out[h]

    return pl.pallas_call(
        body, name="ssd_fwd", grid=(nb, nch), in_specs=[xs, bm, cm, dt, par, par, par, eh],
        out_specs=[pl.BlockSpec((SSM_CHUNK, 512), lambda b, c: (row(b, c), 0)), st],
        out_shape=[jax.ShapeDtypeStruct((n_tok, 512), f32),
                   jax.ShapeDtypeStruct((nb, nch, N_HEADS, SSM_STATE, HEAD_DIM), f32)],
        scratch_shapes=[pltpu.VMEM((N_HEADS, SSM_STATE, HEAD_DIM), f32)],
        compiler_params=_cparams(("arbitrary", "arbitrary")),
    )(xbc_act, xbc_act, xbc_act, proj0, dt_bias, a_log, d_skip, e_heads)


def ssd_bwd(xbc_act, proj0, dt_bias, a_log, d_skip, e_heads, states, dy, nb, t):
    nch = t // SSM_CHUNK
    n_tok = nb * t
    xs, bm, cm, dt, st, par, eh, row = _ssd_specs(nb, nch, True)

    def body(x_ref, b_ref, c_ref, dt_ref, db_ref, al_ref, dsk_ref, e_ref, st_ref, dy_ref,
             dx_ref, dbm_ref, dcm_ref, ddt_ref, ddb_ref, dal_ref, ddsk_ref, ds):
        first = jnp.logical_and(pl.program_id(0) == 0, pl.program_id(1) == 0)

        @pl.when(pl.program_id(1) == 0)
        def _():
            ds[...] = jnp.zeros_like(ds)

        e_list = [e_ref[h] for h in range(N_HEADS)]

        def f(x, bmv, cmv, dtr, s_in, dbv, alv, dskv):
            return ssd_chunk(x, bmv, cmv, dtr, s_in, dbv, alv, dskv, e_list)

        _, vjp = jax.vjp(f, x_ref[...], b_ref[...], c_ref[...], dt_ref[...],
                         [st_ref[0, 0, h] for h in range(N_HEADS)], db_ref[...], al_ref[...], dsk_ref[...])
        g = vjp((dy_ref[...], [ds[h] for h in range(N_HEADS)]))
        dx_ref[...], dbm_ref[...], dcm_ref[...], ddt_ref[...] = g[0], g[1], g[2], g[3].astype(bf16)
        for h in range(N_HEADS):
            ds[h] = g[4][h]
        for ref, val in zip((ddb_ref, dal_ref, ddsk_ref), g[5:]):
            @pl.when(first)
            def _():
                ref[...] = val

            @pl.when(jnp.logical_not(first))
            def _():
                ref[...] += val

    rows_spec = lambda w: pl.BlockSpec((SSM_CHUNK, w), lambda b, c: (row(b, c), 0))
    return pl.pallas_call(
        body, name="ssd_bwd", grid=(nb, nch),
        in_specs=[xs, bm, cm, dt, par, par, par, eh, st, rows_spec(512)],
        out_specs=[rows_spec(512), rows_spec(256), rows_spec(256), rows_spec(LANE), par, par, par],
        out_shape=[jax.ShapeDtypeStruct((n_tok, 512), f32), jax.ShapeDtypeStruct((n_tok, 256), f32),
                   jax.ShapeDtypeStruct((n_tok, 256), f32), jax.ShapeDtypeStruct((n_tok, LANE), bf16)]
        + [jax.ShapeDtypeStruct((1, LANE), f32)] * 3,
        scratch_shapes=[pltpu.VMEM((N_HEADS, SSM_STATE, HEAD_DIM), f32)],
        compiler_params=_cparams(("arbitrary", "arbitrary")),
    )(xbc_act, xbc_act, xbc_act, proj0, dt_bias, a_log, d_skip, e_heads, states, dy)


ATT_BQ = 256
SB_HEADS_PER_STEP = 4
MLA_HEADS_PER_STEP = 4


def _sb_specs(t, bq, nq):
    w = SB_HEADS_PER_STEP * HEAD_DIM
    qs = lambda off: pl.BlockSpec((bq, w), lambda b, g, i: (b * nq + i, off // w + g))
    ks = lambda off: pl.BlockSpec((t, w), lambda b, g, i: (b, off // w + g))
    return qs, ks


def sb_fwd(proj1, nb, t):
    bq = min(ATT_BQ, t)
    nq = t // bq
    qs, ks = _sb_specs(t, bq, nq)

    def body(q_ref, k_ref, v_ref, o_ref):
        q0 = pl.program_id(2) * bq
        n_tiles = (q0 + bq) // SB_KEYS
        hs = range(SB_HEADS_PER_STEP)
        q = [q_ref[:, _head(h)] for h in hs]

        def step(i, carry):
            k0 = pl.multiple_of((n_tiles - 1 - i) * SB_KEYS, SB_KEYS)
            out = []
            for h in hs:
                o, run = carry[h]
                o_t, mass = sb_tile(q[h], k_ref[pl.ds(k0, SB_KEYS), _head(h)], v_ref[pl.ds(k0, SB_KEYS), _head(h)],
                                    run, q0, k0)
                out.append((o + o_t, run + mass))
            return out

        res = lax.fori_loop(0, n_tiles // 2, lambda i, cr: step(2 * i + 1, step(2 * i, cr)),
                            [(jnp.zeros((bq, HEAD_DIM), f32), jnp.zeros((bq, 1), f32)) for _ in hs])
        for h in hs:
            o_ref[:, _head(h)] = res[h][0].astype(bf16)

    return pl.pallas_call(
        body, name="sb_fwd", grid=(nb, N_HEADS // SB_HEADS_PER_STEP, nq), in_specs=[qs(0), ks(512), ks(1024)],
        out_specs=qs(0), out_shape=jax.ShapeDtypeStruct((nb * t, 512), bf16),
        compiler_params=_cparams(("parallel", "parallel", "arbitrary")),
    )(proj1, proj1, proj1)


def sb_bwd(proj1, do, nb, t):
    bq = min(ATT_BQ, t)
    nq = t // bq
    qs, ks = _sb_specs(t, bq, nq)

    def body(q_ref, k_ref, v_ref, do_ref, dq_ref, dk_ref, dv_ref, mass):
        @pl.when(pl.program_id(2) == 0)
        def _():
            dk_ref[...] = jnp.zeros_like(dk_ref)
            dv_ref[...] = jnp.zeros_like(dv_ref)

        q0 = pl.program_id(2) * bq
        n_tiles = (q0 + bq) // SB_KEYS
        hs = range(SB_HEADS_PER_STEP)
        q = [q_ref[:, _head(h)] for h in hs]
        do = [do_ref[:, _head(h)].astype(f32) for h in hs]
        col0 = jnp.zeros((bq, 1), f32)

        def tile(ref, k0, h):
            return ref[pl.ds(k0, SB_KEYS), _head(h)]

        def masses(j, total):
            k0 = pl.multiple_of(j * SB_KEYS, SB_KEYS)
            out = []
            for h in hs:
                _, m = sb_tile(q[h], tile(k_ref, k0, h), tile(v_ref, k0, h), col0, q0, k0)
                mass[h, j] = m
                out.append(total[h] + m)
            return out

        pairs = n_tiles // 2
        total = lax.fori_loop(0, pairs, lambda i, cr: masses(2 * i + 1, masses(2 * i, cr)), [col0 for _ in hs])

        def grads(j, carry):
            k0 = pl.multiple_of(j * SB_KEYS, SB_KEYS)
            out = []
            for h in hs:
                dq, c, prefix = carry[h]
                prefix = prefix + mass[h, j]
                _, vjp = jax.vjp(lambda a, b, d, r: sb_tile(a, b, d, r, q0, k0),
                                 q[h], tile(k_ref, k0, h), tile(v_ref, k0, h), total[h] - prefix)
                dq_t, dk_t, dv_t, drun = vjp((do[h], c))
                dk_ref[pl.ds(k0, SB_KEYS), _head(h)] += dk_t
                dv_ref[pl.ds(k0, SB_KEYS), _head(h)] += dv_t
                out.append((dq + dq_t, drun + c, prefix))
            return out

        res = lax.fori_loop(0, pairs, lambda i, cr: grads(2 * i + 1, grads(2 * i, cr)),
                            [(jnp.zeros((bq, HEAD_DIM), f32), col0, col0) for _ in hs])
        for h in hs:
            dq_ref[:, _head(h)] = res[h][0]

    return pl.pallas_call(
        body, name="sb_bwd", grid=(nb, N_HEADS // SB_HEADS_PER_STEP, nq),
        in_specs=[qs(0), ks(512), ks(1024), qs(0)], out_specs=[qs(0), ks(0), ks(0)],
        out_shape=[jax.ShapeDtypeStruct((nb * t, 512), f32)] * 3,
        scratch_shapes=[pltpu.VMEM((SB_HEADS_PER_STEP, t // SB_KEYS, bq, 1), f32)],
        compiler_params=_cparams(("parallel", "parallel", "arbitrary")),
    )(proj1, proj1, proj1, do)


def _mla_specs(t, bq, nq):
    hp = MLA_HEADS_PER_STEP
    qn = pl.BlockSpec((bq, hp * HEAD_DIM), lambda b, g, i: (b * nq + i, g))
    qr = pl.BlockSpec((bq, hp * MLA_ROPE), lambda b, g, i: (b * nq + i, g))
    kv = pl.BlockSpec((t, hp * 2 * HEAD_DIM), lambda b, g, i: (b, g))
    kp = pl.BlockSpec((t, LANE), lambda b, g, i: (b, 0))
    return qn, qr, kv, kp


def _mla_softmax_pass(qn, qp, kv_ref, kp_ref, q0, n_tiles, bq):
    hs = range(MLA_HEADS_PER_STEP)

    def step(j, carry):
        k0 = pl.multiple_of(j * MLA_KEYS, MLA_KEYS)
        kp = kp_ref[pl.ds(k0, MLA_KEYS), :MLA_ROPE]
        out = []
        for h in hs:
            m, l, acc = carry[h]
            s, _ = mla_scores(qn[h], qp[h], kv_ref[pl.ds(k0, MLA_KEYS), _head(2 * h)], kp, q0, k0)
            m_new = jnp.maximum(m, jnp.max(s, axis=1, keepdims=True))
            alpha, p = jnp.exp(m - m_new), jnp.exp(s - m_new)
            out.append((m_new, alpha * l + jnp.sum(p, axis=1, keepdims=True),
                        alpha * acc + bdot(p, kv_ref[pl.ds(k0, MLA_KEYS), _head(2 * h + 1)])))
        return out

    init = [(jnp.full((bq, 1), -1e30, f32), jnp.zeros((bq, 1), f32), jnp.zeros((bq, HEAD_DIM), f32)) for _ in hs]
    return lax.fori_loop(0, n_tiles, step, init)


def mla_fwd(q_nope, qr, kv, kpe, nb, t):
    bq = min(ATT_BQ, t)
    nq = t // bq
    sqn, sqr, skv, skp = _mla_specs(t, bq, nq)

    def body(qn_ref, qr_ref, kv_ref, kp_ref, o_ref):
        q0 = pl.program_id(2) * bq
        hs = range(MLA_HEADS_PER_STEP)
        qn = [qn_ref[:, _head(h)] for h in hs]
        qp = [qr_ref[:, h * MLA_ROPE:(h + 1) * MLA_ROPE] for h in hs]
        res = _mla_softmax_pass(qn, qp, kv_ref, kp_ref, q0, (q0 + bq) // MLA_KEYS, bq)
        for h in hs:
            _, l, acc = res[h]
            o_ref[:, _head(h)] = (acc / l).astype(bf16)

    return pl.pallas_call(
        body, name="mla_fwd", grid=(nb, N_HEADS // MLA_HEADS_PER_STEP, nq), in_specs=[sqn, sqr, skv, skp],
        out_specs=sqn, out_shape=jax.ShapeDtypeStruct((nb * t, 512), bf16),
        compiler_params=_cparams(("parallel", "arbitrary", "arbitrary")),
    )(q_nope, qr, kv, kpe)


def mla_bwd(q_nope, qr, kv, kpe, do, nb, t):
    bq = min(ATT_BQ, t)
    nq = t // bq
    sqn, sqr, skv, skp = _mla_specs(t, bq, nq)

    def body(qn_ref, qr_ref, kv_ref, kp_ref, do_ref, dqn_ref, dqr_ref, dkv_ref, dkp_ref):
        first_q = pl.program_id(2) == 0

        @pl.when(first_q)
        def _():
            dkv_ref[...] = jnp.zeros_like(dkv_ref)

        @pl.when(jnp.logical_and(first_q, pl.program_id(1) == 0))
        def _():
            dkp_ref[...] = jnp.zeros_like(dkp_ref)

        q0 = pl.program_id(2) * bq
        n_tiles = (q0 + bq) // MLA_KEYS
        hs = range(MLA_HEADS_PER_STEP)
        qn = [qn_ref[:, _head(h)] for h in hs]
        qp = [qr_ref[:, h * MLA_ROPE:(h + 1) * MLA_ROPE] for h in hs]
        do = [do_ref[:, _head(h)].astype(f32) for h in hs]
        stats = _mla_softmax_pass(qn, qp, kv_ref, kp_ref, q0, n_tiles, bq)
        lse = [m + jnp.log(l) for m, l, _ in stats]
        dsum = [jnp.sum(do[h] * (stats[h][2] / stats[h][1]), axis=1, keepdims=True) for h in hs]

        def grads(j, carry):
            k0 = pl.multiple_of(j * MLA_KEYS, MLA_KEYS)
            rows = pl.ds(k0, MLA_KEYS)
            kp = kp_ref[rows, :MLA_ROPE]
            out = []
            for h in hs:
                dqn, dqp = carry[h]
                g = jax.grad(mla_tile_loss, argnums=(0, 1, 2, 3, 4))(
                    qn[h], qp[h], kv_ref[rows, _head(2 * h)], kp, kv_ref[rows, _head(2 * h + 1)],
                    do[h], lse[h], dsum[h], q0, k0)
                dkv_ref[rows, _head(2 * h)] += g[2]
                dkp_ref[rows, :MLA_ROPE] += g[3]
                dkv_ref[rows, _head(2 * h + 1)] += g[4]
                out.append((dqn + g[0], dqp + g[1]))
            return out

        res = lax.fori_loop(0, n_tiles, grads,
                            [(jnp.zeros((bq, HEAD_DIM), f32), jnp.zeros((bq, MLA_ROPE), f32)) for _ in hs])
        for h in hs:
            dqn_ref[:, _head(h)] = res[h][0]
            dqr_ref[:, h * MLA_ROPE:(h + 1) * MLA_ROPE] = res[h][1]

    n = nb * t
    return pl.pallas_call(
        body, name="mla_bwd", grid=(nb, N_HEADS // MLA_HEADS_PER_STEP, nq),
        in_specs=[sqn, sqr, skv, skp, sqn], out_specs=[sqn, sqr, skv, skp],
        out_shape=[jax.ShapeDtypeStruct((n, 512), f32), jax.ShapeDtypeStruct((n, N_HEADS * MLA_ROPE), f32),
                   jax.ShapeDtypeStruct((n, 1024), f32), jax.ShapeDtypeStruct((n, LANE), f32)],
        compiler_params=_cparams(("arbitrary", "arbitrary", "arbitrary")),
    )(q_nope, qr, kv, kpe, do)


def loss_head(h, target):
    n, d = h.shape
    tm = _tile(n, 512)

    def body(h_ref, t_ref, l_ref, dh_ref):
        diff = h_ref[...] - t_ref[...]
        dh_ref[...] = diff * (1.0 / d)
        part = 0.5 * jnp.sum(jnp.sum(diff * diff, axis=1, keepdims=True) * (1.0 / d), axis=0, keepdims=True)

        @pl.when(pl.program_id(0) == 0)
        def _():
            l_ref[...] = jnp.zeros_like(l_ref)

        l_ref[...] += jnp.broadcast_to(part, l_ref.shape)

    spec = pl.BlockSpec((tm, d), lambda i: (i, 0))
    return pl.pallas_call(
        body, name="loss_head", grid=(n // tm,), in_specs=[spec, spec],
        out_specs=[pl.BlockSpec((8, LANE), lambda i: (0, 0)), spec],
        out_shape=[jax.ShapeDtypeStruct((8, LANE), f32), jax.ShapeDtypeStruct((n, d), f32)],
        compiler_params=_cparams(("arbitrary",)),
    )(h, target)


def _row(v):
    return v.reshape(1, -1)


def _pad_cols(a, n):
    return jnp.pad(a, ((0, 0), (0, n - a.shape[1])))


def _pad_row(v, n=LANE):
    return jnp.pad(v.reshape(1, -1), ((0, 0), (0, n - v.shape[0])))


def _group_matrix(width, group):
    idx = np.arange(width) // group
    return jnp.asarray((idx[:, None] == idx[None, :]).astype(np.float32))


def _head_expand():
    e = np.zeros((N_HEADS, LANE, HEAD_DIM), np.float32)
    for h in range(N_HEADS):
        e[h, h, :] = 1.0
    return jnp.asarray(e)


def _rope_freqs():
    inv = 1.0 / (ROPE_THETA ** (np.arange(0, MLA_ROPE, 2, dtype=np.float32) / MLA_ROPE))
    inv = np.tile(inv.astype(np.float32), 2)
    inv_q = np.tile(inv, N_HEADS).reshape(1, N_HEADS * MLA_ROPE)
    inv_k = np.zeros((1, LANE), np.float32)
    inv_k[0, :MLA_ROPE] = inv
    return jnp.asarray(inv_q), jnp.asarray(inv_k)


def _uq_split(w):
    w3 = w.reshape(w.shape[0], N_HEADS, HEAD_DIM + MLA_ROPE)
    return w3[:, :, :HEAD_DIM].reshape(-1, 512), w3[:, :, HEAD_DIM:].reshape(-1, N_HEADS * MLA_ROPE)


def _uq_merge(gn, gr):
    r = gn.shape[0]
    return jnp.concatenate([gn.reshape(r, N_HEADS, HEAD_DIM), gr.reshape(r, N_HEADS, MLA_ROPE)], axis=2).reshape(r, 768)


def local_step(x, positions, target, w):
    nb, t, d = x.shape
    n = nb * t
    tm = 256
    ni = n // tm
    tc = LANE
    h0 = x.reshape(n, d)
    tgt = target.reshape(n, d)
    pos = positions.reshape(n, 1).astype(f32)
    gh = _group_matrix(512, HEAD_DIM)
    gg = _group_matrix(512, 256)
    e_heads = _head_expand()
    inv_q, inv_k = _rope_freqs()
    g = {}

    def ln_stage(h, y, gname, bname):
        ops = [_rows(h, tm), _rows(y, tm, gdtype=bf16), _param(_row(w[gname])), _param(_row(w[bname]))]
        return ops, [_rows_out(n, d, tm)]

    def ln_fwd(name, ops):
        return block_fwd(lambda *a: f_ln(*a) * 2, name, (1, ni), ops, [_rows_out(n, d, tm), _rows_out(n, d, tm, bf16)])

    def ffn_act_stage(u, cw, cb):
        nj = D_FF // tc
        ops = [_cols(u, t, tc, 0, D_FF, bf16), _cols(u, t, tc, nj, D_FF, bf16)] \
            + [_cparam(cw[i:i + 1], tc) for i in range(3)] + [_cparam(_row(cb), tc)]
        return ops, [Out((n, D_FF), (t, tc), lambda j, i: (i, j), bf16)], (nj, nb)

    w_in0 = _pad_cols(w['l0_w_in'], L0_PAD)
    w_out0 = w['l0_w_out']
    h0b = h0.astype(bf16)
    proj0 = mm(h0b, w_in0, "l0_proj")

    shift_ops = [_cols(proj0, t, tc, 0, RWKV_COLS, bf16), _cparam(_row(w['rwkv_mix']), tc)]
    shift_outs = [Out((n, RWKV_COLS), (t, tc), lambda j, i: (i, j))]
    shift_grid = (RWKV_COLS // tc, nb)
    (ps,) = block_fwd(f_shift_mix, "rwkv_shift", shift_grid, shift_ops, shift_outs)

    pre_ops = [_colblock(ps, tm, 512, 512), _colblock(ps, tm, 1536, 128), _colblock(ps, tm, 1664, 128),
               _param(_row(w['rwkv_w0'])), _param(w['rwkv_w2']), _param(_row(w['rwkv_a0'])), _param(w['rwkv_a2']),
               _param(w['rwkv_g2']), _param(_row(w['rwkv_k_k'])), _param(_row(w['rwkv_k_a'])), _param(gh, diff=False)]
    pre_outs = [_rows_out(n, 512, tm) for _ in range(5)]
    lw, k2, na, bb, gate_r = block_fwd(f_rwkv_pre, "rwkv_pre", (1, ni), pre_ops, pre_outs)
    y_tok, s0_saved = rwkv_scan_fwd(ps, lw, k2, na, bb, nb, t)

    post_ops = [_rows(y_tok, tm), _colblock(ps, tm, 0, 512), _rows(k2, tm), _colblock(ps, tm, 1024, 512),
                _rows(gate_r, tm), _param(_row(w['rwkv_ln_g'])), _param(_row(w['rwkv_ln_b'])),
                _param(w['rwkv_r_k'].reshape(1, 512)), _param(gh, diff=False)]
    post_outs = [_rows_out(n, 512, tm, bf16)]
    (y_a,) = block_fwd(f_rwkv_post, "rwkv_post", (1, ni), post_ops, post_outs)

    xbc_off = (RWKV_COLS + 512) // tc
    conv_ops = [_cols(proj0, t, tc, xbc_off, 1024, bf16)] + [_cparam(w['ssm_conv_w'][i:i + 1], tc) for i in range(4)] \
        + [_cparam(_row(w['ssm_conv_b']), tc)]
    conv_outs = [Out((n, 1024), (t, tc), lambda j, i: (i, j))]
    conv_grid = (1024 // tc, nb)
    (xbc_act,) = block_fwd(f_conv4_silu, "ssm_conv", conv_grid, conv_ops, conv_outs)

    dt_bias, a_log, d_skip = _pad_row(w['ssm_dt_bias']), _pad_row(w['ssm_a_log']), _pad_row(w['ssm_d'])
    y_ssd, ssd_states = ssd_fwd(xbc_act, proj0, dt_bias, a_log, d_skip, e_heads, nb, t)

    z_tok = proj0[:, RWKV_COLS:RWKV_COLS + 512]
    spost_ops = [_rows(y_ssd, tm), _rows(z_tok, tm, gdtype=bf16), _param(_row(w['ssm_norm_g'])), _param(gg, diff=False)]
    spost_outs = [_rows_out(n, 512, tm, bf16)]
    (y_b,) = block_fwd(f_ssm_post, "ssm_post", (1, ni), spost_ops, spost_outs)

    mixed0 = mm(y_b, w_out0[512:], "l0_out_b", add=mm(y_a, w_out0[:512], "l0_out_a"))
    ln1_ops, ln_outs = ln_stage(h0, mixed0, 'l0_ln1_g', 'l0_ln1_b')
    h1, h1b = ln_fwd("l0_ln1", ln1_ops)

    u0 = mm(h1b, w['ffn0_w_up'], "ffn0_up")
    act0_ops, act_outs, act_grid = ffn_act_stage(u0, w['ffn0_conv_w'], w['ffn0_conv_b'])
    (act0,) = block_fwd(f_ffn_act, "ffn0_act", act_grid, act0_ops, act_outs)
    f0 = mm(act0, w['ffn0_w_down'], "ffn0_down")
    ln2_ops, _ = ln_stage(h1, f0, 'l0_ln2_g', 'l0_ln2_b')
    h2, h2b = ln_fwd("l0_ln2", ln2_ops)

    w_in1 = _pad_cols(w['l1_w_in'], L1_PAD)
    w_out1 = w['l1_w_out']
    proj1 = mm(h2b, w_in1, "l1_proj")
    w_qn, w_qr = _uq_split(w['mla_w_uq'])
    mpre_ops = [_colblock(proj1, tm, 1536, 256, bf16), _colblock(proj1, tm, 1792, 128, bf16),
                _colblock(proj1, tm, 1920, 128, bf16),
                Op(pos, (tm, 1), lambda j, i: (i, 0), diff=False),
                _param(_row(w['mla_q_norm_g'])), _param(w_qn), _param(w_qr),
                _param(_row(w['mla_kv_norm_g'])), _param(w['mla_w_ukv']), _param(inv_q, diff=False),
                _param(inv_k, diff=False)]
    mpre_outs = [_rows_out(n, 512, tm), _rows_out(n, N_HEADS * MLA_ROPE, tm), _rows_out(n, 1024, tm),
                 _rows_out(n, LANE, tm)]
    q_nope, q_rope, kv, kpe = block_fwd(f_mla_pre, "mla_pre", (1, ni), mpre_ops, mpre_outs)
    o_sb = sb_fwd(proj1, nb, t)
    o_mla = mla_fwd(q_nope, q_rope, kv, kpe, nb, t)

    mixed1 = mm(o_mla, w_out1[512:], "l1_out_b", add=mm(o_sb, w_out1[:512], "l1_out_a"))
    ln3_ops, _ = ln_stage(h2, mixed1, 'l1_ln1_g', 'l1_ln1_b')
    h3, h3b = ln_fwd("l1_ln1", ln3_ops)
    u1 = mm(h3b, w['ffn1_w_up'], "ffn1_up")
    act1_ops, _, _ = ffn_act_stage(u1, w['ffn1_conv_w'], w['ffn1_conv_b'])
    (act1,) = block_fwd(f_ffn_act, "ffn1_act", act_grid, act1_ops, act_outs)
    f1 = mm(act1, w['ffn1_w_down'], "ffn1_down")
    ln4_ops, _ = ln_stage(h3, f1, 'l1_ln2_g', 'l1_ln2_b')
    (h4,) = block_fwd(f_ln, "l1_ln2", (1, ni), ln4_ops, ln_outs)

    loss_part, dh4 = loss_head(h4, tgt)

    def vec(a_):
        return a_.reshape(-1)

    def ffn_bwd(tag, dh_out, ln_ops, act_ops, h_in, act, w_up, w_down, names):
        dh_res, df, gg_, gb_ = block_bwd(f_ln, tag + "_ln2_bwd", (1, ni), ln_ops, ln_outs, [dh_out])
        g[names[4]], g[names[5]] = vec(gg_), vec(gb_)
        g[names[3]] = mm(act, df, tag + "_down_dw", ta=True)
        dact = mm(df, w_down.T, tag + "_down_dx")
        dgate, dup, dw0, dw1, dw2, dcb = block_bwd(f_ffn_act, tag + "_act_bwd", act_grid, act_ops, act_outs, [dact])
        g[names[1]] = jnp.concatenate([dw0, dw1, dw2], axis=0)
        g[names[2]] = vec(dcb)
        g[names[0]] = jnp.concatenate([mm(h_in, dgate, tag + "_gate_dw", ta=True),
                                       mm(h_in, dup, tag + "_upv_dw", ta=True)], axis=1)
        w_up_t = w_up.T
        dh = mm(dgate, w_up_t[:D_FF], tag + "_gate_dx", add=dh_res)
        return mm(dup, w_up_t[D_FF:], tag + "_upv_dx", add=dh)

    def out_bwd(tag, dmixed, y_first, y_second, w_out, name):
        g[name] = jnp.concatenate([mm(y_first, dmixed, tag + "_a_dw", ta=True),
                                   mm(y_second, dmixed, tag + "_b_dw", ta=True)], axis=0)
        w_t = w_out.T
        return mm(dmixed, w_t[:, :512], tag + "_a_dx"), mm(dmixed, w_t[:, 512:], tag + "_b_dx")

    dh3 = ffn_bwd("ffn1", dh4, ln4_ops, act1_ops, h3b, act1, w['ffn1_w_up'], w['ffn1_w_down'],
                  ['ffn1_w_up', 'ffn1_conv_w', 'ffn1_conv_b', 'ffn1_w_down', 'l1_ln2_g', 'l1_ln2_b'])

    dh2_res, dmixed1, g3g, g3b = block_bwd(f_ln, "l1_ln1_bwd", (1, ni), ln3_ops, ln_outs, [dh3])
    g['l1_ln1_g'], g['l1_ln1_b'] = vec(g3g), vec(g3b)
    do_sb, do_mla = out_bwd("l1_out", dmixed1, o_sb, o_mla, w_out1, 'l1_w_out')

    dq_nope, dq_rope, dkv, dkpe = mla_bwd(q_nope, q_rope, kv, kpe, do_mla, nb, t)
    dsb_q, dsb_k, dsb_v = sb_bwd(proj1, do_sb, nb, t)
    (dc_q, dc_kv, dkpe_raw, gqg, gwqn, gwqr, gkvg, g['mla_w_ukv']) = block_bwd(
        f_mla_pre, "mla_pre_bwd", (1, ni), mpre_ops, mpre_outs, [dq_nope, dq_rope, dkv, dkpe])
    g['mla_q_norm_g'], g['mla_kv_norm_g'] = vec(gqg), vec(gkvg)
    g['mla_w_uq'] = _uq_merge(gwqn, gwqr)
    dproj1 = jnp.concatenate([dsb_q.astype(bf16), dsb_k.astype(bf16), dsb_v.astype(bf16), dc_q, dc_kv, dkpe_raw],
                             axis=1)
    g['l1_w_in'] = mm(h2b, dproj1, "l1_proj_dw", ta=True)[:, :L1_COLS]
    dh2 = mm(dproj1, w_in1.T, "l1_proj_dx", add=dh2_res)

    dh1 = ffn_bwd("ffn0", dh2, ln2_ops, act0_ops, h1b, act0, w['ffn0_w_up'], w['ffn0_w_down'],
                  ['ffn0_w_up', 'ffn0_conv_w', 'ffn0_conv_b', 'ffn0_w_down', 'l0_ln2_g', 'l0_ln2_b'])

    dh0_res, dmixed0, g1g, g1b = block_bwd(f_ln, "l0_ln1_bwd", (1, ni), ln1_ops, ln_outs, [dh1])
    g['l0_ln1_g'], g['l0_ln1_b'] = vec(g1g), vec(g1b)
    dy_a, dy_b = out_bwd("l0_out", dmixed0, y_a, y_b, w_out0, 'l0_w_out')

    dy_ssd, dz, gng = block_bwd(f_ssm_post, "ssm_post_bwd", (1, ni), spost_ops, spost_outs, [dy_b])
    g['ssm_norm_g'] = vec(gng)
    dxs, dbm, dcm, ddt_raw, gdb, gal, gdsk = ssd_bwd(xbc_act, proj0, dt_bias, a_log, d_skip, e_heads, ssd_states,
                                                     dy_ssd, nb, t)
    g['ssm_dt_bias'], g['ssm_a_log'], g['ssm_d'] = gdb[0, :8], gal[0, :8], gdsk[0, :8]
    dxbc_act = jnp.concatenate([dxs, dbm, dcm], axis=1)
    dxbc, cw0, cw1, cw2, cw3, gcb = block_bwd(f_conv4_silu, "ssm_conv_bwd", conv_grid, conv_ops, conv_outs, [dxbc_act])
    g['ssm_conv_w'] = jnp.concatenate([cw0, cw1, cw2, cw3], axis=0)
    g['ssm_conv_b'] = vec(gcb)

    dy_tok, dr_post, dk2_post, dv_post, dgate, glg, glb, grk = block_bwd(
        f_rwkv_post, "rwkv_post_bwd", (1, ni), post_ops, post_outs, [dy_a])
    g['rwkv_ln_g'], g['rwkv_ln_b'], g['rwkv_r_k'] = vec(glg), vec(glb), grk.reshape(N_HEADS, HEAD_DIM)
    dr, dlw, dk2, dv, dna, dbb = rwkv_scan_bwd(s0_saved, ps, lw, k2, na, bb, dy_tok, nb, t)
    (dk_pre, dwa_lo, dg_lo, gw0, g['rwkv_w2'], ga0, g['rwkv_a2'], g['rwkv_g2'], gkk, gka) = block_bwd(
        f_rwkv_pre, "rwkv_pre_bwd", (1, ni), pre_ops, pre_outs, [dlw, dk2 + dk2_post, dna, dbb, dgate])
    g['rwkv_w0'], g['rwkv_a0'], g['rwkv_k_k'], g['rwkv_k_a'] = vec(gw0), vec(ga0), vec(gkk), vec(gka)
    dps = jnp.concatenate([dr + dr_post, dk_pre, dv + dv_post, dwa_lo, dg_lo], axis=1)
    dp_rwkv, gmix = block_bwd(f_shift_mix, "rwkv_shift_bwd", shift_grid, shift_ops, shift_outs, [dps])
    g['rwkv_mix'] = vec(gmix)

    dproj0 = jnp.concatenate([dp_rwkv, dz, dxbc, ddt_raw], axis=1)
    g['l0_w_in'] = mm(h0b, dproj0, "l0_proj_dw", ta=True)[:, :L0_COLS]
    grad_x = mm(dproj0, w_in0.T, "l0_proj_dx", add=dh0_res)
    return loss_part, grad_x.reshape(nb, t, d), g


MESH = pl.DeviceIdType.MESH
ANY = pl.BlockSpec(memory_space=pl.ANY)
AXES = ("x", "y", "c")


def _place():
    x, y, c = lax.axis_index("x"), lax.axis_index("y"), lax.axis_index("c")
    chips = [(1 - x, y), (x, 1 - y), (1 - x, 1 - y)]
    return x, y, c, chips


def _dma_sems(n):
    return pltpu.SemaphoreType.DMA((n,))


def gather_shards(shards):
    n = len(shards)

    def body(*refs):
        ins, outs = refs[:n], refs[n:2 * n]
        ici_send, ici_recv, d2d_send, d2d_recv, local_sems = refs[2 * n:]
        x, y, c, chips = _place()
        me = 2 * x + y
        pairs = list(enumerate(zip(ins, outs)))
        mine = [pltpu.make_async_copy(a, o.at[me], local_sems.at[k]) for k, (a, o) in pairs]
        for cp in mine:
            cp.start()

        def over_ici(k, j, a, o, slot, to):
            return pltpu.make_async_remote_copy(
                src_ref=a.at[c], dst_ref=o.at[slot, c], send_sem=ici_send.at[3 * k + j],
                recv_sem=ici_recv.at[3 * k + j], device_id=to, device_id_type=MESH)

        def to_sibling(k, j, o, slot, half):
            return pltpu.make_async_remote_copy(
                src_ref=o.at[slot, half], dst_ref=o.at[slot, half], send_sem=d2d_send.at[3 * k + j],
                recv_sem=d2d_recv.at[3 * k + j], device_id=(x, y, 1 - c), device_id_type=MESH)

        sends = [over_ici(k, j, a, o, me, (cx, cy, c)) for k, (a, o) in pairs for j, (cx, cy) in enumerate(chips)]
        for cp in sends:
            cp.start()
        passed = []
        for k, (a, o) in pairs:
            for j, (cx, cy) in enumerate(chips):
                over_ici(k, j, a, o, 2 * cx + cy, (cx, cy, c)).wait_recv()
                passed.append(to_sibling(k, j, o, 2 * cx + cy, c))
                passed[-1].start()
        for k, (a, o) in pairs:
            for j, (cx, cy) in enumerate(chips):
                to_sibling(k, j, o, 2 * cx + cy, 1 - c).wait_recv()
        for cp in sends + passed:
            cp.wait_send()
        for cp in mine:
            cp.wait()

    return pl.pallas_call(
        body, name="gather_shards", in_specs=[ANY] * n, out_specs=[ANY] * n,
        out_shape=[jax.ShapeDtypeStruct((N_SHARD,) + a.shape, a.dtype) for a in shards],
        scratch_shapes=[_dma_sems(3 * n), _dma_sems(3 * n), _dma_sems(3 * n), _dma_sems(3 * n), _dma_sems(n)],
    )(*shards)


def swap_halves(pieces):
    n = len(pieces)

    def body(*refs):
        ins, outs = refs[:n], refs[n:2 * n]
        send_sems, recv_sems = refs[2 * n:]
        x, y, c, _ = _place()
        cps = [pltpu.make_async_remote_copy(
            src_ref=a.at[:, 1 - c], dst_ref=o, send_sem=send_sems.at[k], recv_sem=recv_sems.at[k],
            device_id=(x, y, 1 - c), device_id_type=MESH) for k, (a, o) in enumerate(zip(ins, outs))]
        for cp in cps:
            cp.start()
        for cp in cps:
            cp.wait()

    return pl.pallas_call(
        body, name="swap_halves", in_specs=[ANY] * n, out_specs=[ANY] * n,
        out_shape=[jax.ShapeDtypeStruct((a.shape[0],) + a.shape[2:], a.dtype) for a in pieces],
        scratch_shapes=[_dma_sems(n), _dma_sems(n)],
    )(*pieces)


def scatter_to_chips(parts):
    n = len(parts)

    def body(*refs):
        ins, outs = refs[:n], refs[n:2 * n]
        send_sems, recv_sems, local_sems = refs[2 * n:]
        x, y, c, chips = _place()
        me = 2 * x + y
        mine = [pltpu.make_async_copy(a.at[me], o.at[me], local_sems.at[k]) for k, (a, o) in enumerate(zip(ins, outs))]
        for cp in mine:
            cp.start()
        sends = [pltpu.make_async_remote_copy(
            src_ref=a.at[2 * cx + cy], dst_ref=o.at[me], send_sem=send_sems.at[3 * k + j],
            recv_sem=recv_sems.at[3 * k + j], device_id=(cx, cy, c), device_id_type=MESH)
            for k, (a, o) in enumerate(zip(ins, outs)) for j, (cx, cy) in enumerate(chips)]
        for cp in sends:
            cp.start()
        for k, (a, o) in enumerate(zip(ins, outs)):
            for j, (cx, cy) in enumerate(chips):
                pltpu.make_async_remote_copy(
                    src_ref=a.at[me], dst_ref=o.at[2 * cx + cy], send_sem=send_sems.at[3 * k + j],
                    recv_sem=recv_sems.at[3 * k + j], device_id=(cx, cy, c), device_id_type=MESH).wait_recv()
        for cp in sends:
            cp.wait_send()
        for cp in mine:
            cp.wait()

    return pl.pallas_call(
        body, name="scatter_to_chips", in_specs=[ANY] * n, out_specs=[ANY] * n,
        out_shape=[jax.ShapeDtypeStruct(a.shape, a.dtype) for a in parts],
        scratch_shapes=[_dma_sems(3 * n), _dma_sems(3 * n), _dma_sems(n)],
    )(*parts)


def share_halves(bufs):
    n = len(bufs)

    def body(*refs):
        ins, outs = refs[:n], refs[n:2 * n]
        send_sems, recv_sems = refs[2 * n:]
        x, y, c, _ = _place()
        cps = [pltpu.make_async_remote_copy(
            src_ref=a.at[c], dst_ref=o.at[c], send_sem=send_sems.at[k], recv_sem=recv_sems.at[k],
            device_id=(x, y, 1 - c), device_id_type=MESH) for k, (a, o) in enumerate(zip(ins, outs))]
        for cp in cps:
            cp.start()
        for k, (a, o) in enumerate(zip(ins, outs)):
            cps[k].wait_send()
            pltpu.make_async_remote_copy(
                src_ref=a.at[c], dst_ref=o.at[1 - c], send_sem=send_sems.at[k], recv_sem=recv_sems.at[k],
                device_id=(x, y, 1 - c), device_id_type=MESH).wait_recv()

    return pl.pallas_call(
        body, name="share_halves", in_specs=[ANY] * n, out_specs=[ANY] * n,
        out_shape=[jax.ShapeDtypeStruct(a.shape, a.dtype) for a in bufs],
        input_output_aliases={k: k for k in range(n)},
        scratch_shapes=[_dma_sems(n), _dma_sems(n)],
    )(*bufs)


def pair_add(piece, recv, core, name, out_dtype):
    _, _, h, cdim = piece.shape
    tr = _rtile(h, cdim)

    def body(c_ref, a_ref, b_ref, o_ref):
        o_ref[...] = (a_ref[0] + b_ref[...]).astype(o_ref.dtype)

    spec = pl.BlockSpec((1, tr, cdim), lambda p, i, c_ref: (p, i, 0))
    return pl.pallas_call(
        body, name=name,
        grid_spec=pltpu.PrefetchScalarGridSpec(
            num_scalar_prefetch=1, grid=(N_SHARD, h // tr),
            in_specs=[pl.BlockSpec((1, 1, tr, cdim), lambda p, i, c_ref: (p, c_ref[0], i, 0)), spec],
            out_specs=spec),
        out_shape=jax.ShapeDtypeStruct((N_SHARD, h, cdim), out_dtype),
        compiler_params=_cparams(("parallel", "parallel")),
    )(core, piece, recv)


def chip_add(parts, core, name):
    _, h, cdim = parts.shape
    tr = _rtile(h, cdim, 1024 * 1024)

    def body(c_ref, p_ref, o_ref):
        p = [p_ref[s].astype(f32) for s in range(N_SHARD)]
        o_ref[0] = ((p[0] + p[1]) + p[2]) + p[3]

    return pl.pallas_call(
        body, name=name,
        grid_spec=pltpu.PrefetchScalarGridSpec(
            num_scalar_prefetch=1, grid=(h // tr,),
            in_specs=[pl.BlockSpec((N_SHARD, tr, cdim), lambda i, c_ref: (0, i, 0))],
            out_specs=pl.BlockSpec((1, tr, cdim), lambda i, c_ref: (c_ref[0], i, 0))),
        out_shape=jax.ShapeDtypeStruct((2, h, cdim), f32), compiler_params=_cparams(("parallel",)),
    )(core, parts)


def adamw(w, g, m, v, name):
    rows, cdim = w.shape
    tr = _rtile(rows, cdim, 1024 * 1024)

    def body(w_ref, g_ref, m_ref, v_ref, d_ref, nm_ref, nv_ref):
        gv = g_ref[...]
        m_new = ADAM_B1 * m_ref[...] + (1.0 - ADAM_B1) * gv
        v_new = ADAM_B2 * v_ref[...] + (1.0 - ADAM_B2) * jnp.square(gv)
        m_hat = m_new / (1.0 - ADAM_B1 ** ADAM_STEP)
        v_hat = v_new / (1.0 - ADAM_B2 ** ADAM_STEP)
        d_ref[...] = -ADAM_LR * (m_hat / (jnp.sqrt(v_hat) + ADAM_EPS) + ADAM_WD * w_ref[...])
        nm_ref[...] = m_new
        nv_ref[...] = v_new

    spec = pl.BlockSpec((tr, cdim), lambda i: (i, 0))
    return pl.pallas_call(body, name=name, grid=(rows // tr,), in_specs=[spec] * 4, out_specs=[spec] * 3,
                          out_shape=[jax.ShapeDtypeStruct(w.shape, f32)] * 3,
                          compiler_params=_cparams(("parallel",)))(w, g, m, v)


SMALL_MULTIPLE = 16 * LANE


def _pack_flat(parts, multiple=SMALL_MULTIPLE):
    flat = jnp.concatenate([p.reshape(-1) for p in parts])
    pad = (-flat.shape[0]) % multiple
    return jnp.pad(flat, (0, pad)).reshape(-1, LANE)


def _unpack_flat(buf, shapes):
    flat = buf.reshape(-1)
    out, off = [], 0
    for s in shapes:
        cnt = int(np.prod(s))
        out.append(flat[off:off + cnt].reshape(s))
        off += cnt
    return out


def _full_from_shards(name, gathered):
    if name in COL_SHARDED:
        return jnp.concatenate([gathered[s] for s in range(N_SHARD)], axis=1)
    return gathered.reshape(-1, gathered.shape[2])


def _pieces(name, grad):
    if name in COL_SHARDED:
        r, cdim = grad.shape
        return grad.reshape(r, N_SHARD, cdim // N_SHARD).transpose(1, 0, 2)
    return grad.reshape(N_SHARD, grad.shape[0] // N_SHARD, grad.shape[1])


def _small_pieces(name, grad):
    if name in COL_SHARDED or name in ROW_SHARDED:
        return _pieces(name, grad).reshape(N_SHARD, -1)
    return jnp.broadcast_to(grad.reshape(1, -1), (N_SHARD, grad.size))


def kernel(x, positions, l0_w_in, rwkv_mix, rwkv_w0, rwkv_w2, rwkv_a0, rwkv_a2, rwkv_g2, rwkv_k_k, rwkv_k_a, rwkv_r_k, rwkv_ln_g, rwkv_ln_b, ssm_conv_w, ssm_conv_b, ssm_dt_bias, ssm_a_log, ssm_d, ssm_norm_g, l0_w_out, l0_ln1_g, l0_ln1_b, ffn0_w_up, ffn0_conv_w, ffn0_conv_b, ffn0_w_down, l0_ln2_g, l0_ln2_b, l1_w_in, mla_q_norm_g, mla_w_uq, mla_kv_norm_g, mla_w_ukv, l1_w_out, l1_ln1_g, l1_ln1_b, ffn1_w_up, ffn1_conv_w, ffn1_conv_b, ffn1_w_down, l1_ln2_g, l1_ln2_b, loss_target, m_l0_w_in, m_rwkv_mix, m_rwkv_w0, m_rwkv_w2, m_rwkv_a0, m_rwkv_a2, m_rwkv_g2, m_rwkv_k_k, m_rwkv_k_a, m_rwkv_r_k, m_rwkv_ln_g, m_rwkv_ln_b, m_ssm_conv_w, m_ssm_conv_b, m_ssm_dt_bias, m_ssm_a_log, m_ssm_d, m_ssm_norm_g, m_l0_w_out, m_l0_ln1_g, m_l0_ln1_b, m_ffn0_w_up, m_ffn0_conv_w, m_ffn0_conv_b, m_ffn0_w_down, m_l0_ln2_g, m_l0_ln2_b, m_l1_w_in, m_mla_q_norm_g, m_mla_w_uq, m_mla_kv_norm_g, m_mla_w_ukv, m_l1_w_out, m_l1_ln1_g, m_l1_ln1_b, m_ffn1_w_up, m_ffn1_conv_w, m_ffn1_conv_b, m_ffn1_w_down, m_l1_ln2_g, m_l1_ln2_b, v_l0_w_in, v_rwkv_mix, v_rwkv_w0, v_rwkv_w2, v_rwkv_a0, v_rwkv_a2, v_rwkv_g2, v_rwkv_k_k, v_rwkv_k_a, v_rwkv_r_k, v_rwkv_ln_g, v_rwkv_ln_b, v_ssm_conv_w, v_ssm_conv_b, v_ssm_dt_bias, v_ssm_a_log, v_ssm_d, v_ssm_norm_g, v_l0_w_out, v_l0_ln1_g, v_l0_ln1_b, v_ffn0_w_up, v_ffn0_conv_w, v_ffn0_conv_b, v_ffn0_w_down, v_l0_ln2_g, v_l0_ln2_b, v_l1_w_in, v_mla_q_norm_g, v_mla_w_uq, v_mla_kv_norm_g, v_mla_w_ukv, v_l1_w_out, v_l1_ln1_g, v_l1_ln1_b, v_ffn1_w_up, v_ffn1_conv_w, v_ffn1_conv_b, v_ffn1_w_down, v_l1_ln2_g, v_l1_ln2_b):
    args = locals()
    w_loc = {n: args[n] for n in WEIGHTS}
    m_loc = {n: args["m_" + n] for n in WEIGHTS}
    v_loc = {n: args["v_" + n] for n in WEIGHTS}
    core = lax.axis_index("c").astype(jnp.int32).reshape(1)

    small_sharded = [n for n in SMALL if n in COL_SHARDED]
    halves = lambda a: a.reshape(2, a.shape[0] // 2, a.shape[1])
    gathered = gather_shards([halves(w_loc[n].astype(bf16)) for n in BIG]
                             + [halves(_pack_flat([w_loc[n] for n in small_sharded]))])
    gathered = [got.reshape(N_SHARD, -1, got.shape[3]) for got in gathered]
    w_full = dict(w_loc)
    for n, got in zip(BIG, gathered):
        w_full[n] = _full_from_shards(n, got)
    per_shard = [_unpack_flat(gathered[-1][s], [w_loc[n].shape for n in small_sharded]) for s in range(N_SHARD)]
    for k, n in enumerate(small_sharded):
        w_full[n] = jnp.concatenate([per_shard[s][k] for s in range(N_SHARD)], axis=1)

    loss_part, grad_x, g_full = local_step(x, positions, loss_target, w_full)
    loss = lax.psum(loss_part[0, 0], AXES)

    small_flat = jnp.concatenate([_small_pieces(n, g_full[n]) for n in SMALL], axis=1)
    pad = (-small_flat.shape[1]) % SMALL_MULTIPLE
    small_pieces = jnp.pad(small_flat, ((0, 0), (0, pad))).reshape(N_SHARD, -1, LANE)
    units = BIG + ['small']
    pieces = [_pieces(n, g_full[n]) for n in BIG] + [small_pieces]
    pieces = [p.reshape(N_SHARD, 2, p.shape[1] // 2, p.shape[2]) for p in pieces]
    from_sibling = swap_halves(pieces)
    pair = [pair_add(p, r, core, "pair_add_" + n, f32 if n == 'small' else bf16)
            for n, p, r in zip(units, pieces, from_sibling)]
    from_chips = scatter_to_chips(pair)
    both = share_halves([chip_add(p, core, "chip_add_" + n) for n, p in zip(units, from_chips)])
    reduced = [b.reshape(-1, b.shape[2]) for b in both]

    out = {}
    for n, gred in zip(BIG, reduced):
        out[n] = (gred,) + tuple(adamw(w_loc[n], gred, m_loc[n], v_loc[n], "adamw_" + n))
    shapes = [w_loc[n].shape for n in SMALL]
    packs = [_pack_flat([d[n] for n in SMALL]) for d in (w_loc, m_loc, v_loc)]
    small_res = (reduced[-1],) + tuple(adamw(packs[0], reduced[-1], packs[1], packs[2], "adamw_small"))
    small_unpacked = [_unpack_flat(b, shapes) for b in small_res]
    for k, n in enumerate(SMALL):
        out[n] = tuple(u[k] for u in small_unpacked)
    return (loss, grad_x, *[out[n][0] for n in WEIGHTS], *[out[n][1] for n in WEIGHTS],
            *[out[n][2] for n in WEIGHTS], *[out[n][3] for n in WEIGHTS])
```

```python
import functools

import numpy as np
import jax
import jax.numpy as jnp
from jax import lax
from jax.experimental import pallas as pl
from jax.experimental.pallas import tpu as pltpu

f32 = jnp.float32
bf16 = jnp.bfloat16
HI = lax.Precision.HIGHEST
MID = lax.Precision.HIGH

D_MODEL = 1024
HEAD_DIM = 64
N_HEADS = 8
RWKV_COLS = 1792
RWKV_GN_EPS = 64e-5
SSM_STATE = 128
SSM_CHUNK = 128
L0_COLS = 3336
L0_PAD = 3456
L1_COLS = 1952
L1_PAD = 2048
MLA_ROPE = 32
ROPE_THETA = 10000.0
D_FF = 2816
DEPTH = 2
ALPHA = (2 * DEPTH) ** 0.25
ADAM_LR = 0.001
ADAM_B1 = 0.9
ADAM_B2 = 0.999
ADAM_EPS = 1e-08
ADAM_WD = 0.01
ADAM_STEP = 10
RWKV_CHUNK = 64
RWKV_HEADS_PER_STEP = 8
LANE = 128
SUBLANE = 8
VMEM_LIMIT = 56 * 1024 * 1024

WEIGHTS = ['l0_w_in', 'rwkv_mix', 'rwkv_w0', 'rwkv_w2', 'rwkv_a0', 'rwkv_a2', 'rwkv_g2', 'rwkv_k_k', 'rwkv_k_a',
           'rwkv_r_k', 'rwkv_ln_g', 'rwkv_ln_b', 'ssm_conv_w', 'ssm_conv_b', 'ssm_dt_bias', 'ssm_a_log', 'ssm_d',
           'ssm_norm_g', 'l0_w_out', 'l0_ln1_g', 'l0_ln1_b', 'ffn0_w_up', 'ffn0_conv_w', 'ffn0_conv_b',
           'ffn0_w_down', 'l0_ln2_g', 'l0_ln2_b', 'l1_w_in', 'mla_q_norm_g', 'mla_w_uq', 'mla_kv_norm_g',
           'mla_w_ukv', 'l1_w_out', 'l1_ln1_g', 'l1_ln1_b', 'ffn1_w_up', 'ffn1_conv_w', 'ffn1_conv_b',
           'ffn1_w_down', 'l1_ln2_g', 'l1_ln2_b']
COL_SHARDED = ['l0_w_in', 'rwkv_w2', 'rwkv_a2', 'rwkv_g2', 'ssm_conv_w', 'ffn0_w_up', 'ffn0_conv_w', 'l1_w_in',
               'mla_w_uq', 'mla_w_ukv', 'ffn1_w_up', 'ffn1_conv_w']
ROW_SHARDED = ['l0_w_out', 'ffn0_w_down', 'l1_w_out', 'ffn1_w_down']
BIG = ['l0_w_in', 'l0_w_out', 'ffn0_w_up', 'ffn0_w_down', 'l1_w_in', 'l1_w_out', 'ffn1_w_up', 'ffn1_w_down']
SMALL = [n for n in WEIGHTS if n not in BIG]
N_SHARD = 4


def _cparams(sem):
    return pltpu.CompilerParams(dimension_semantics=sem, vmem_limit_bytes=VMEM_LIMIT)


def _dg(a, b, ca, cb, prec=None):
    return lax.dot_general(a, b, (((ca,), (cb,)), ((), ())), precision=prec, preferred_element_type=f32)


def hdot(a, b):
    return _dg(a, b, 1, 0, HI)


def mdot(a, b):
    return _dg(a, b, 1, 0, MID)


def mdot_nt(a, b):
    return _dg(a, b, 1, 1, MID)


def mdot_tn(a, b):
    return _dg(a, b, 0, 0, MID)


def _b(x):
    return x.astype(bf16)


@jax.custom_vjp
def bdot(x, w):
    return _dg(_b(x), _b(w), 1, 0)


def _bdot_fwd(x, w):
    return bdot(x, w), (x, w)


def _bdot_bwd(res, g):
    x, w = res
    return _dg(_b(g), _b(w), 1, 1).astype(x.dtype), _dg(_b(x), _b(g), 0, 0).astype(w.dtype)


bdot.defvjp(_bdot_fwd, _bdot_bwd)


@jax.custom_vjp
def bdot_nt(x, y):
    return _dg(_b(x), _b(y), 1, 1)


def _bdot_nt_fwd(x, y):
    return bdot_nt(x, y), (x, y)


def _bdot_nt_bwd(res, g):
    x, y = res
    return _dg(_b(g), _b(y), 1, 0), _dg(_b(g), _b(x), 0, 0)


bdot_nt.defvjp(_bdot_nt_fwd, _bdot_nt_bwd)


@jax.custom_vjp
def bdot_tn(x, y):
    return _dg(_b(x), _b(y), 0, 0)


def _bdot_tn_fwd(x, y):
    return bdot_tn(x, y), (x, y)


def _bdot_tn_bwd(res, g):
    x, y = res
    return _dg(_b(y), _b(g), 1, 1), _dg(_b(x), _b(g), 1, 0)


bdot_tn.defvjp(_bdot_tn_fwd, _bdot_tn_bwd)


def _sigmoid(x):
    return 1.0 / (1.0 + jnp.exp(-x))


@jax.custom_vjp
def softplus(x):
    e = jnp.exp(-jnp.abs(x))
    u = 1.0 + e
    log1p = jnp.where(u == 1.0, e, jnp.log(u) * e / jnp.where(u == 1.0, 1.0, u - 1.0))
    return jnp.maximum(x, 0.0) + log1p


def _softplus_fwd(x):
    return softplus(x), x


def _softplus_bwd(x, g):
    return (g * _sigmoid(x),)


softplus.defvjp(_softplus_fwd, _softplus_bwd)


@jax.custom_vjp
def softplus_abs(x):
    return jnp.maximum(x, 0.0) + jnp.log(1.0 + jnp.exp(-jnp.abs(x)))


def _softplus_abs_fwd(x):
    return softplus_abs(x), x


softplus_abs.defvjp(_softplus_abs_fwd, _softplus_bwd)


def _two_pass(x, m):
    hi = _b(x)
    lo = _b(x - hi.astype(f32))
    m16 = _b(m)
    return _dg(hi, m16, 1, 0) + _dg(lo, m16, 1, 0)


def _upper(n):
    return (_iota2((n, n), 0) > _iota2((n, n), 1)).astype(f32)


@jax.custom_vjp
def suffix_sums(x):
    return _two_pass(x, _upper(x.shape[1]))


def _suffix_sums_fwd(x):
    return suffix_sums(x), None


def _suffix_sums_bwd(_, g):
    return (_two_pass(g, _upper(g.shape[1]).T),)


suffix_sums.defvjp(_suffix_sums_fwd, _suffix_sums_bwd)


def silu(x):
    return x * _sigmoid(x)


def _shift_rows(x, k, up):
    if k == 0:
        return x
    t = x.shape[0]
    rows = lax.broadcasted_iota(jnp.int32, x.shape, 0)
    if up:
        return jnp.where(rows < t - k, pltpu.roll(x, t - k, 0), 0.0)
    return jnp.where(rows >= k, pltpu.roll(x, k, 0), 0.0)


@functools.partial(jax.custom_vjp, nondiff_argnums=(1,))
def shift_down(x, k):
    return _shift_rows(x, k, False)


def _shift_down_fwd(x, k):
    return _shift_rows(x, k, False), None


def _shift_down_bwd(k, _, g):
    return (_shift_rows(g, k, True),)


shift_down.defvjp(_shift_down_fwd, _shift_down_bwd)


@functools.partial(jax.custom_vjp, nondiff_argnums=(1,))
def lane_roll(x, s):
    return pltpu.roll(x, s % x.shape[1], 1)


def _lane_roll_fwd(x, s):
    return lane_roll(x, s), None


def _lane_roll_bwd(s, _, g):
    return (pltpu.roll(g, (-s) % g.shape[1], 1),)


lane_roll.defvjp(_lane_roll_fwd, _lane_roll_bwd)


def rot_half32(x):
    first = (lax.broadcasted_iota(jnp.int32, x.shape, 1) % MLA_ROPE) < (MLA_ROPE // 2)
    return jnp.where(first, -lane_roll(x, -(MLA_ROPE // 2)), lane_roll(x, MLA_ROPE // 2))


def _iota2(shape, axis):
    return lax.broadcasted_iota(jnp.int32, shape, axis)


class Op:
    def __init__(self, arr, block, imap, diff=True, acc=None, gshape=None, gimap=None, gdtype=f32):
        self.arr, self.block, self.imap, self.diff, self.acc = arr, tuple(block), imap, diff, acc
        self.gshape = tuple(arr.shape) if gshape is None else tuple(gshape)
        self.gimap = imap if gimap is None else gimap
        self.gdtype = gdtype


class Out:
    def __init__(self, shape, block, imap, dtype=f32):
        self.shape, self.block, self.imap, self.dtype = tuple(shape), tuple(block), imap, dtype


def block_fwd(fn, name, grid, ops, outs):
    n_in = len(ops)

    def body(*refs):
        vals = [r[...] for r in refs[:n_in]]
        res = fn(*vals)
        for r, v in zip(refs[n_in:], res):
            r[...] = v.astype(r.dtype)

    res = pl.pallas_call(
        body, name=name, grid=grid,
        in_specs=[pl.BlockSpec(o.block, o.imap) for o in ops],
        out_specs=[pl.BlockSpec(o.block, o.imap) for o in outs],
        out_shape=[jax.ShapeDtypeStruct(o.shape, o.dtype) for o in outs],
        compiler_params=_cparams(("arbitrary", "arbitrary")),
    )(*[o.arr for o in ops])
    return tuple(res)


def block_bwd(fn, name, grid, ops, outs, douts):
    n_in, n_out = len(ops), len(outs)
    dix = [k for k, o in enumerate(ops) if o.diff]

    def body(*refs):
        vals = [r[...] for r in refs[:n_in]]
        dvals = tuple(r[...] for r in refs[n_in:n_in + n_out])
        grefs = refs[n_in + n_out:]

        def f(*d):
            full = list(vals)
            for k, v in zip(dix, d):
                full[k] = v
            return tuple(fn(*full))

        _, vjp = jax.vjp(f, *[vals[k] for k in dix])
        grads = vjp(dvals)
        j, i = pl.program_id(0), pl.program_id(1)
        for k, gref, g in zip(dix, grefs, grads):
            acc = ops[k].acc
            if acc is None:
                gref[...] = g.astype(gref.dtype)
            else:
                first = (i == 0) if acc == 'i' else jnp.logical_and(i == 0, j == 0)

                @pl.when(first)
                def _():
                    gref[...] = g

                @pl.when(jnp.logical_not(first))
                def _():
                    gref[...] += g

    gspecs = [pl.BlockSpec(ops[k].block, ops[k].gimap) for k in dix]
    gshapes = [jax.ShapeDtypeStruct(ops[k].gshape, ops[k].gdtype) for k in dix]
    res = pl.pallas_call(
        body, name=name, grid=grid,
        in_specs=[pl.BlockSpec(o.block, o.imap) for o in ops] + [pl.BlockSpec(o.block, o.imap) for o in outs],
        out_specs=gspecs, out_shape=gshapes,
        compiler_params=_cparams(("arbitrary", "arbitrary")),
    )(*[o.arr for o in ops], *douts)
    return tuple(res)


def _rows(arr, tm, diff=True, gdtype=f32):
    return Op(arr, (tm, arr.shape[1]), lambda j, i: (i, 0), diff=diff, gdtype=gdtype)


def _param(arr, diff=True):
    return Op(arr, arr.shape, lambda j, i: (0,) * arr.ndim, diff=diff, acc='ij')


def _rows_out(n, c, tm, dtype=f32):
    return Out((n, c), (tm, c), lambda j, i: (i, 0), dtype)


def _cols(arr, t, tc, off=0, width=None, gdtype=f32):
    width = arr.shape[1] if width is None else width
    return Op(arr, (t, tc), lambda j, i: (i, j + off), gshape=(arr.shape[0], width), gimap=lambda j, i: (i, j),
              gdtype=gdtype)


def _cparam(arr, tc):
    return Op(arr, (arr.shape[0], tc), lambda j, i: (0, j), acc='i')


def _colblock(arr, tm, off, width, gdtype=f32):
    return Op(arr, (tm, width), lambda j, i: (i, off // width), gshape=(arr.shape[0], width),
              gimap=lambda j, i: (i, 0), gdtype=gdtype)


def _tile(n, cap):
    best = None
    for t in range(LANE, min(n, cap) + 1, LANE):
        if n % t == 0:
            best = t
    return n if best is None else best


def _rtile(rows, cols, cap_bytes=2 * 1024 * 1024):
    best = None
    for t in range(SUBLANE, rows + 1, SUBLANE):
        if rows % t == 0 and t * cols * 4 <= cap_bytes:
            best = t
    return rows if best is None else best


def mm(a, b, name, ta=False, add=None):
    m = a.shape[1] if ta else a.shape[0]
    kd = a.shape[0] if ta else a.shape[1]
    n = b.shape[1]
    tm, tn = _tile(m, 1408), _tile(n, 1408)
    tk = kd if kd <= 2048 else _tile(kd, 1408)
    nk = kd // tk
    ca = 0 if ta else 1

    def body(*refs):
        if add is None:
            a_ref, b_ref, o_ref, acc = refs
        else:
            a_ref, b_ref, add_ref, o_ref, acc = refs
        k = pl.program_id(2)

        @pl.when(k == 0)
        def _():
            acc[...] = jnp.zeros_like(acc)

        acc[...] += _dg(_b(a_ref[...]), _b(b_ref[...]), ca, 0)

        @pl.when(k == nk - 1)
        def _():
            o_ref[...] = acc[...] if add is None else acc[...] + add_ref[...]

    a_spec = pl.BlockSpec((tk, tm), lambda i, j, k: (k, i)) if ta else pl.BlockSpec((tm, tk), lambda i, j, k: (i, k))
    b_spec = pl.BlockSpec((tk, tn), lambda i, j, k: (k, j))
    o_spec = pl.BlockSpec((tm, tn), lambda i, j, k: (i, j))
    args, specs = [a, b], [a_spec, b_spec]
    if add is not None:
        args.append(add)
        specs.append(o_spec)
    return pl.pallas_call(
        body, name=name, grid=(m // tm, n // tn, nk), in_specs=specs, out_specs=o_spec,
        out_shape=jax.ShapeDtypeStruct((m, n), f32), scratch_shapes=[pltpu.VMEM((tm, tn), f32)],
        compiler_params=_cparams(("parallel", "parallel", "arbitrary")),
    )(*args)


def f_ln(h, y, g, b):
    x = ALPHA * h + y
    mu = jnp.mean(x, axis=-1, keepdims=True)
    xc = x - mu
    var = jnp.mean(xc * xc, axis=-1, keepdims=True)
    return (xc * lax.rsqrt(var + 1e-5) * g + b,)


def f_shift_mix(p, mix):
    return (p + (shift_down(p, 1) - p) * mix,)


def f_rwkv_pre(k, wa_lo, g_lo, w0, w2, a0, a2, g2, k_k, k_a, gh):
    w_lo, a_lo = wa_lo[:, :64], wa_lo[:, 64:]
    log_w = -softplus(-(w0 + bdot(jnp.tanh(w_lo), w2))) - 0.5
    lw = -jnp.exp(log_w)
    a = _sigmoid(a0 + bdot(a_lo, a2))
    g = bdot(_sigmoid(g_lo), g2)
    kk = k * k_k
    kk = kk / jnp.maximum(jnp.sqrt(hdot(kk * kk, gh)), 1e-12)
    k2 = k * (1.0 + (a - 1.0) * k_a)
    return lw, k2, -kk, kk * a, g


def f_rwkv_post(y, r, k2, v, g, ln_g, ln_b, r_k, gh):
    mu = hdot(y, gh) * (1.0 / HEAD_DIM)
    yc = y - mu
    var = hdot(yc * yc, gh) * (1.0 / HEAD_DIM)
    yn = yc * lax.rsqrt(var + RWKV_GN_EPS) * ln_g + ln_b
    bonus = hdot(r * k2 * r_k, gh) * v
    return ((yn + bonus) * g,)


def f_conv4_silu(x, w0, w1, w2, w3, b):
    y = b + shift_down(x, 3) * w0 + shift_down(x, 2) * w1 + shift_down(x, 1) * w2 + x * w3
    return (silu(y),)


def f_ssm_post(y, z, norm_g, gg):
    u = y * silu(z)
    ms = hdot(u * u, gg) * (1.0 / 256.0)
    return (u * lax.rsqrt(ms + 1e-5) * norm_g,)


def f_ffn_act(gate, up, w0, w1, w2, b):
    gc = b + shift_down(gate, 2) * w0 + shift_down(gate, 1) * w1 + gate * w2
    return (silu(gc) * up,)


def _rms(x, g, eps=1e-6):
    return x * lax.rsqrt(jnp.mean(x * x, axis=-1, keepdims=True) + eps) * g


def f_mla_pre(c_q, c_kv, kpe, pos, q_g, w_qn, w_qr, kv_g, w_ukv, inv_q, inv_k):
    qn_in = _rms(c_q, q_g)
    q_nope = bdot(qn_in, w_qn)
    qr = bdot(qn_in, w_qr)
    kv = bdot(_rms(c_kv, kv_g), w_ukv)
    ang_q = pos * inv_q
    ang_k = pos * inv_k
    return (q_nope, qr * jnp.cos(ang_q) + rot_half32(qr) * jnp.sin(ang_q), kv,
            kpe * jnp.cos(ang_k) + rot_half32(kpe) * jnp.sin(ang_k))


def rwkv_chunk(s0, r, lw, k, v, a, b):
    hs = range(len(r))
    l = r[0].shape[0]
    ri, ci = _iota2((l, l), 0), _iota2((l, l), 1)
    strict, incl = ri > ci, ri >= ci
    tri, eye = incl.astype(f32), (ri == ci).astype(f32)
    last = (_iota2((l, 1), 0) == l - 1).astype(f32)
    c = [hdot(tri, lw[h]) for h in hs]
    at = [a[h] * jnp.exp(c[h] - lw[h]) for h in hs]
    wi = [jnp.exp(-c[h]) for h in hs]
    bt = [b[h] * wi[h] for h in hs]
    kt = [k[h] * wi[h] for h in hs]
    rt = [r[h] * jnp.exp(c[h]) for h in hs]
    nab = [jnp.where(strict, mdot_nt(at[h], bt[h]), 0.0) for h in hs]
    nak = [jnp.where(strict, mdot_nt(at[h], kt[h]), 0.0) for h in hs]
    g = [mdot_nt(at[h], s0[h]) + mdot(nak[h], v[h]) for h in hs]
    x = [eye + nab[h] for h in hs]
    p = [mdot(nab[h], nab[h]) for h in hs]
    steps = max(1, (l - 1).bit_length()) - 1
    for it in range(steps):
        x = [x[h] + mdot(p[h], x[h]) for h in hs]
        if it < steps - 1:
            p = [mdot(p[h], p[h]) for h in hs]
    u = [mdot(x[h], g[h]) for h in hs]
    mrb = [jnp.where(incl, mdot_nt(rt[h], bt[h]), 0.0) for h in hs]
    mrk = [jnp.where(incl, mdot_nt(rt[h], kt[h]), 0.0) for h in hs]
    y = [mdot_nt(rt[h], s0[h]) + mdot(mrb[h], u[h]) + mdot(mrk[h], v[h]) for h in hs]
    s1 = [(s0[h] + mdot_tn(u[h], bt[h]) + mdot_tn(v[h], kt[h])) * jnp.exp(jnp.sum(c[h] * last, axis=0, keepdims=True))
          for h in hs]
    return y, s1


def ssd_chunk(xs, bm, cm, dt_raw, s_in, dt_bias, a_log, d_skip, e_heads):
    l = xs.shape[0]
    ri, ci = _iota2((l, l), 0), _iota2((l, l), 1)
    incl = ri >= ci
    tri = incl.astype(f32)
    dt = softplus(dt_raw + dt_bias)
    a128 = dt * (-jnp.exp(a_log))
    lane0 = (_iota2((1, HEAD_DIM), 1) == 0).astype(f32)
    last = (_iota2((l, 1), 0) == l - 1).astype(f32)
    hs = range(N_HEADS)
    group = lambda m, g: m[:, g * SSM_STATE:(g + 1) * SSM_STATE]
    cb = [bdot_nt(group(cm, g), group(bm, g)) for g in range(2)]
    e_all = jnp.concatenate(e_heads, axis=1)
    dt_all = hdot(dt, e_all)
    ac_all = hdot(tri, hdot(a128, e_all))
    xd_all = xs * dt_all
    skip_all = xs * hdot(jnp.broadcast_to(d_skip, (l, LANE)), e_all)
    ac = [ac_all[:, _head(h)] for h in hs]
    xd = [xd_all[:, _head(h)] for h in hs]
    col = [jnp.broadcast_to(jnp.sum(ac[h] * lane0, axis=1, keepdims=True), (l, l)) for h in hs]
    decay = [jnp.exp(jnp.where(incl, col[h] - col[h].T, -1e30)) for h in hs]
    y_diag = [bdot(cb[h // 4] * decay[h], xd[h]) for h in hs]
    a_tot = [jnp.sum(ac[h] * last, axis=0, keepdims=True) for h in hs]
    y_off = [jnp.exp(ac[h]) * bdot(group(cm, h // 4), s_in[h]) for h in hs]
    s_out = [jnp.exp(a_tot[h]) * s_in[h] + bdot_tn(group(bm, h // 4), xd[h] * jnp.exp(a_tot[h] - ac[h])) for h in hs]
    return jnp.concatenate([y_diag[h] + y_off[h] for h in hs], axis=1) + skip_all, s_out


SB_KEYS = LANE
MLA_KEYS = 256


def sb_tile(q, k, v, run, q0, k0):
    bq, kb = q.shape[0], k.shape[0]
    z = bdot_nt(q, k) * HEAD_DIM ** -0.5
    strict = (k0 + _iota2((bq, kb), 1)) < (q0 + _iota2((bq, kb), 0))
    lk = jnp.where(strict, -softplus_abs(z), 0.0)
    log_att = z + lk + suffix_sums(lk) + run
    att = jnp.where(strict, jnp.exp(jnp.where(strict, log_att, 0.0)), 0.0)
    return bdot(att, v), jnp.sum(lk, axis=1, keepdims=True)


def mla_scores(qn, qp, kn, kp, q0, k0):
    bq, kb = qn.shape[0], kn.shape[0]
    s = (bdot_nt(qn, kn) + bdot_nt(qp, kp)) * (HEAD_DIM + MLA_ROPE) ** -0.5
    causal = (k0 + _iota2((bq, kb), 1)) <= (q0 + _iota2((bq, kb), 0))
    return jnp.where(causal, s, -1e30), causal


def mla_tile_loss(qn, qp, kn, kp, v, do, lse, dsum, q0, k0):
    s, causal = mla_scores(qn, qp, kn, kp, q0, k0)
    p = jnp.where(causal, jnp.exp(s - lse), 0.0)
    return jnp.sum(do * bdot(p, v)) - jnp.sum(dsum * jnp.sum(p, axis=1, keepdims=True))


def _head(h):
    return slice(h * HEAD_DIM, (h + 1) * HEAD_DIM)


def _rwkv_specs(nc, rev):
    hp = RWKV_HEADS_PER_STEP
    w = hp * HEAD_DIM
    chunk = (lambda c: nc - 1 - c) if rev else (lambda c: c)
    tok = lambda off: pl.BlockSpec((RWKV_CHUNK, w), lambda b, g, c: (b * nc + chunk(c), off // w + g))
    st = pl.BlockSpec((1, hp, HEAD_DIM, HEAD_DIM), lambda b, g, c: ((b * (N_HEADS // hp) + g) * nc + chunk(c), 0, 0, 0))
    return tok, st


def _hosted_call(work, name, grid, in_specs, out_specs, out_shape, scratch, args, ride):
    n_in, n_out, n_scr = len(in_specs), len(out_specs), len(scratch)
    k = 0 if ride is None else len(ride.inputs)

    def body(*refs):
        ins, r_in = refs[:n_in], refs[n_in:n_in + k]
        outs, r_out = refs[n_in + k:n_in + k + n_out], refs[n_in + k + n_out:n_in + 2 * k + n_out]
        scr, r_sems = refs[n_in + 2 * k + n_out:n_in + 2 * k + n_out + n_scr], refs[n_in + 2 * k + n_out + n_scr:]
        ids = [pl.program_id(a) for a in range(len(grid))]
        if ride is not None:
            @pl.when(functools.reduce(jnp.logical_and, [i == 0 for i in ids]))
            def _():
                ride.start(r_in, r_out, r_sems)

        work(ins, outs, scr)
        if ride is not None:
            @pl.when(functools.reduce(jnp.logical_and, [i == g - 1 for i, g in zip(ids, grid)]))
            def _():
                ride.finish(r_in, r_out, r_sems)

    res = pl.pallas_call(
        body, name=name, grid=grid, in_specs=list(in_specs) + [ANY] * k, out_specs=list(out_specs) + [ANY] * k,
        out_shape=list(out_shape) + ([] if ride is None else ride.out_shapes),
        scratch_shapes=list(scratch) + ([] if ride is None else ride.scratch),
        compiler_params=_cparams(("arbitrary",) * len(grid)),
    )(*args, *([] if ride is None else ride.inputs))
    return res[:n_out], res[n_out:]


def rwkv_scan_fwd(ps, lw, k2, na, bb, nb, t, ride=None):
    hp, nc = RWKV_HEADS_PER_STEP, t // RWKV_CHUNK
    ng = N_HEADS // hp
    tok, st = _rwkv_specs(nc, False)

    def work(ins, outs, scr):
        r_ref, v_ref, lw_ref, k_ref, a_ref, b_ref = ins
        y_ref, s0_ref = outs
        (s,) = scr

        @pl.when(pl.program_id(2) == 0)
        def _():
            s[...] = jnp.zeros_like(s)

        s0_ref[0] = s[...]
        heads = lambda ref: [ref[:, _head(h)] for h in range(hp)]
        y, s1 = rwkv_chunk([s[h] for h in range(hp)], heads(r_ref), heads(lw_ref), heads(k_ref), heads(v_ref),
                           heads(a_ref), heads(b_ref))
        for h in range(hp):
            y_ref[:, _head(h)] = y[h]
            s[h] = s1[h]

    return _hosted_call(
        work, "rwkv_scan_fwd", (nb, ng, nc), [tok(0), tok(1024), tok(0), tok(0), tok(0), tok(0)], [tok(0), st],
        [jax.ShapeDtypeStruct((nb * t, N_HEADS * HEAD_DIM), f32),
         jax.ShapeDtypeStruct((nb * ng * nc, hp, HEAD_DIM, HEAD_DIM), f32)],
        [pltpu.VMEM((hp, HEAD_DIM, HEAD_DIM), f32)], (ps, ps, lw, k2, na, bb), ride)


def rwkv_scan_bwd(s0, ps, lw, k2, na, bb, dy, nb, t, ride=None):
    hp, nc = RWKV_HEADS_PER_STEP, t // RWKV_CHUNK
    ng = N_HEADS // hp
    tok, st = _rwkv_specs(nc, True)

    def work(ins, outs, scr):
        s0_ref, r_ref, v_ref, lw_ref, k_ref, a_ref, b_ref, dy_ref = ins
        (ds,) = scr

        @pl.when(pl.program_id(2) == 0)
        def _():
            ds[...] = jnp.zeros_like(ds)

        heads = lambda ref: [ref[:, _head(h)] for h in range(hp)]
        _, vjp = jax.vjp(rwkv_chunk, [s0_ref[0, h] for h in range(hp)], heads(r_ref), heads(lw_ref), heads(k_ref),
                         heads(v_ref), heads(a_ref), heads(b_ref))
        g = vjp((heads(dy_ref), [ds[h] for h in range(hp)]))
        for h in range(hp):
            ds[h] = g[0][h]
            for ref, val in zip(outs, g[1:]):
                ref[:, _head(h)] = val[h]

    return _hosted_call(
        work, "rwkv_scan_bwd", (nb, ng, nc), [st, tok(0), tok(1024), tok(0), tok(0), tok(0), tok(0), tok(0)],
        [tok(0)] * 6, [jax.ShapeDtypeStruct((nb * t, N_HEADS * HEAD_DIM), f32)] * 6,
        [pltpu.VMEM((hp, HEAD_DIM, HEAD_DIM), f32)], (s0, ps, ps, lw, k2, na, bb, dy), ride)


def _ssd_specs(nb, nch, rev):
    def row(b, c):
        return b * nch + (nch - 1 - c if rev else c)

    l = SSM_CHUNK
    xs = pl.BlockSpec((l, 512), lambda b, c: (row(b, c), 0))
    bm = pl.BlockSpec((l, 256), lambda b, c: (row(b, c), 2))
    cm = pl.BlockSpec((l, 256), lambda b, c: (row(b, c), 3))
    dt = pl.BlockSpec((l, LANE), lambda b, c: (row(b, c), (L0_PAD - LANE) // LANE))
    st = pl.BlockSpec((1, 1, N_HEADS, SSM_STATE, HEAD_DIM), lambda b, c: (b, (nch - 1 - c if rev else c), 0, 0, 0))
    par = pl.BlockSpec((1, LANE), lambda b, c: (0, 0))
    eh = pl.BlockSpec((N_HEADS, LANE, HEAD_DIM), lambda b, c: (0, 0, 0))
    return xs, bm, cm, dt, st, par, eh, row


def ssd_fwd(xbc_act, proj0, dt_bias, a_log, d_skip, e_heads, nb, t):
    nch = t // SSM_CHUNK
    n_tok = nb * t
    xs, bm, cm, dt, st, par, eh, row = _ssd_specs(nb, nch, False)

    def body(x_ref, b_ref, c_ref, dt_ref, db_ref, al_ref, dsk_ref, e_ref, y_ref, st_ref, s):
        @pl.when(pl.program_id(1) == 0)
        def _():
            s[...] = jnp.zeros_like(s)

        st_ref[0, 0] = s[...]
        y, s_out = ssd_chunk(x_ref[...], b_ref[...], c_ref[...], dt_ref[...], [s[h] for h in range(N_HEADS)],
                             db_ref[...], al_ref[...], dsk_ref[...], [e_ref[h] for h in range(N_HEADS)])
        y_ref[...] = y
        for h in range(N_HEADS):
            s[h] = s_out[h]

    return pl.pallas_call(
        body, name="ssd_fwd", grid=(nb, nch), in_specs=[xs, bm, cm, dt, par, par, par, eh],
        out_specs=[pl.BlockSpec((SSM_CHUNK, 512), lambda b, c: (row(b, c), 0)), st],
        out_shape=[jax.ShapeDtypeStruct((n_tok, 512), f32),
                   jax.ShapeDtypeStruct((nb, nch, N_HEADS, SSM_STATE, HEAD_DIM), f32)],
        scratch_shapes=[pltpu.VMEM((N_HEADS, SSM_STATE, HEAD_DIM), f32)],
        compiler_params=_cparams(("arbitrary", "arbitrary")),
    )(xbc_act, xbc_act, xbc_act, proj0, dt_bias, a_log, d_skip, e_heads)


def ssd_bwd(xbc_act, proj0, dt_bias, a_log, d_skip, e_heads, states, dy, nb, t):
    nch = t // SSM_CHUNK
    n_tok = nb * t
    xs, bm, cm, dt, st, par, eh, row = _ssd_specs(nb, nch, True)

    def body(x_ref, b_ref, c_ref, dt_ref, db_ref, al_ref, dsk_ref, e_ref, st_ref, dy_ref,
             dx_ref, dbm_ref, dcm_ref, ddt_ref, ddb_ref, dal_ref, ddsk_ref, ds):
        first = jnp.logical_and(pl.program_id(0) == 0, pl.program_id(1) == 0)

        @pl.when(pl.program_id(1) == 0)
        def _():
            ds[...] = jnp.zeros_like(ds)

        e_list = [e_ref[h] for h in range(N_HEADS)]

        def f(x, bmv, cmv, dtr, s_in, dbv, alv, dskv):
            return ssd_chunk(x, bmv, cmv, dtr, s_in, dbv, alv, dskv, e_list)

        _, vjp = jax.vjp(f, x_ref[...], b_ref[...], c_ref[...], dt_ref[...],
                         [st_ref[0, 0, h] for h in range(N_HEADS)], db_ref[...], al_ref[...], dsk_ref[...])
        g = vjp((dy_ref[...], [ds[h] for h in range(N_HEADS)]))
        dx_ref[...], dbm_ref[...], dcm_ref[...], ddt_ref[...] = g[0], g[1], g[2], g[3].astype(bf16)
        for h in range(N_HEADS):
            ds[h] = g[4][h]
        for ref, val in zip((ddb_ref, dal_ref, ddsk_ref), g[5:]):
            @pl.when(first)
            def _():
                ref[...] = val

            @pl.when(jnp.logical_not(first))
            def _():
                ref[...] += val

    rows_spec = lambda w: pl.BlockSpec((SSM_CHUNK, w), lambda b, c: (row(b, c), 0))
    return pl.pallas_call(
        body, name="ssd_bwd", grid=(nb, nch),
        in_specs=[xs, bm, cm, dt, par, par, par, eh, st, rows_spec(512)],
        out_specs=[rows_spec(512), rows_spec(256), rows_spec(256), rows_spec(LANE), par, par, par],
        out_shape=[jax.ShapeDtypeStruct((n_tok, 512), f32), jax.ShapeDtypeStruct((n_tok, 256), f32),
                   jax.ShapeDtypeStruct((n_tok, 256), f32), jax.ShapeDtypeStruct((n_tok, LANE), bf16)]
        + [jax.ShapeDtypeStruct((1, LANE), f32)] * 3,
        scratch_shapes=[pltpu.VMEM((N_HEADS, SSM_STATE, HEAD_DIM), f32)],
        compiler_params=_cparams(("arbitrary", "arbitrary")),
    )(xbc_act, xbc_act, xbc_act, proj0, dt_bias, a_log, d_skip, e_heads, states, dy)


ATT_BQ = 256
SB_HEADS_PER_STEP = 4
MLA_HEADS_PER_STEP = 4


def _sb_specs(t, bq, nq):
    w = SB_HEADS_PER_STEP * HEAD_DIM
    qs = lambda off: pl.BlockSpec((bq, w), lambda b, g, i: (b * nq + i, off // w + g))
    ks = lambda off: pl.BlockSpec((t, w), lambda b, g, i: (b, off // w + g))
    return qs, ks


def sb_fwd(proj1, nb, t):
    bq = min(ATT_BQ, t)
    nq = t // bq
    qs, ks = _sb_specs(t, bq, nq)

    def body(q_ref, k_ref, v_ref, o_ref):
        q0 = pl.program_id(2) * bq
        n_tiles = (q0 + bq) // SB_KEYS
        hs = range(SB_HEADS_PER_STEP)
        q = [q_ref[:, _head(h)] for h in hs]

        def step(i, carry):
            k0 = pl.multiple_of((n_tiles - 1 - i) * SB_KEYS, SB_KEYS)
            out = []
            for h in hs:
                o, run = carry[h]
                o_t, mass = sb_tile(q[h], k_ref[pl.ds(k0, SB_KEYS), _head(h)], v_ref[pl.ds(k0, SB_KEYS), _head(h)],
                                    run, q0, k0)
                out.append((o + o_t, run + mass))
            return out

        res = lax.fori_loop(0, n_tiles // 2, lambda i, cr: step(2 * i + 1, step(2 * i, cr)),
                            [(jnp.zeros((bq, HEAD_DIM), f32), jnp.zeros((bq, 1), f32)) for _ in hs])
        for h in hs:
            o_ref[:, _head(h)] = res[h][0].astype(bf16)

    return pl.pallas_call(
        body, name="sb_fwd", grid=(nb, N_HEADS // SB_HEADS_PER_STEP, nq), in_specs=[qs(0), ks(512), ks(1024)],
        out_specs=qs(0), out_shape=jax.ShapeDtypeStruct((nb * t, 512), bf16),
        compiler_params=_cparams(("parallel", "parallel", "arbitrary")),
    )(proj1, proj1, proj1)


def sb_bwd(proj1, do, nb, t):
    bq = min(ATT_BQ, t)
    nq = t // bq
    qs, ks = _sb_specs(t, bq, nq)

    def body(q_ref, k_ref, v_ref, do_ref, dq_ref, dk_ref, dv_ref, mass):
        @pl.when(pl.program_id(2) == 0)
        def _():
            dk_ref[...] = jnp.zeros_like(dk_ref)
            dv_ref[...] = jnp.zeros_like(dv_ref)

        q0 = pl.program_id(2) * bq
        n_tiles = (q0 + bq) // SB_KEYS
        hs = range(SB_HEADS_PER_STEP)
        q = [q_ref[:, _head(h)] for h in hs]
        do = [do_ref[:, _head(h)].astype(f32) for h in hs]
        col0 = jnp.zeros((bq, 1), f32)

        def tile(ref, k0, h):
            return ref[pl.ds(k0, SB_KEYS), _head(h)]

        def masses(j, total):
            k0 = pl.multiple_of(j * SB_KEYS, SB_KEYS)
            out = []
            for h in hs:
                _, m = sb_tile(q[h], tile(k_ref, k0, h), tile(v_ref, k0, h), col0, q0, k0)
                mass[h, j] = m
                out.append(total[h] + m)
            return out

        pairs = n_tiles // 2
        total = lax.fori_loop(0, pairs, lambda i, cr: masses(2 * i + 1, masses(2 * i, cr)), [col0 for _ in hs])

        def grads(j, carry):
            k0 = pl.multiple_of(j * SB_KEYS, SB_KEYS)
            out = []
            for h in hs:
                dq, c, prefix = carry[h]
                prefix = prefix + mass[h, j]
                _, vjp = jax.vjp(lambda a, b, d, r: sb_tile(a, b, d, r, q0, k0),
                                 q[h], tile(k_ref, k0, h), tile(v_ref, k0, h), total[h] - prefix)
                dq_t, dk_t, dv_t, drun = vjp((do[h], c))
                dk_ref[pl.ds(k0, SB_KEYS), _head(h)] += dk_t
                dv_ref[pl.ds(k0, SB_KEYS), _head(h)] += dv_t
                out.append((dq + dq_t, drun + c, prefix))
            return out

        res = lax.fori_loop(0, pairs, lambda i, cr: grads(2 * i + 1, grads(2 * i, cr)),
                            [(jnp.zeros((bq, HEAD_DIM), f32), col0, col0) for _ in hs])
        for h in hs:
            dq_ref[:, _head(h)] = res[h][0]

    return pl.pallas_call(
        body, name="sb_bwd", grid=(nb, N_HEADS // SB_HEADS_PER_STEP, nq),
        in_specs=[qs(0), ks(512), ks(1024), qs(0)], out_specs=[qs(0), ks(0), ks(0)],
        out_shape=[jax.ShapeDtypeStruct((nb * t, 512), f32)] * 3,
        scratch_shapes=[pltpu.VMEM((SB_HEADS_PER_STEP, t // SB_KEYS, bq, 1), f32)],
        compiler_params=_cparams(("parallel", "parallel", "arbitrary")),
    )(proj1, proj1, proj1, do)


def _mla_specs(t, bq, nq):
    hp = MLA_HEADS_PER_STEP
    qn = pl.BlockSpec((bq, hp * HEAD_DIM), lambda b, g, i: (b * nq + i, g))
    qr = pl.BlockSpec((bq, hp * MLA_ROPE), lambda b, g, i: (b * nq + i, g))
    kv = pl.BlockSpec((t, hp * 2 * HEAD_DIM), lambda b, g, i: (b, g))
    kp = pl.BlockSpec((t, LANE), lambda b, g, i: (b, 0))
    return qn, qr, kv, kp


def _mla_softmax_pass(qn, qp, kv_ref, kp_ref, q0, n_tiles, bq):
    hs = range(MLA_HEADS_PER_STEP)

    def step(j, carry):
        k0 = pl.multiple_of(j * MLA_KEYS, MLA_KEYS)
        kp = kp_ref[pl.ds(k0, MLA_KEYS), :MLA_ROPE]
        out = []
        for h in hs:
            m, l, acc = carry[h]
            s, _ = mla_scores(qn[h], qp[h], kv_ref[pl.ds(k0, MLA_KEYS), _head(2 * h)], kp, q0, k0)
            m_new = jnp.maximum(m, jnp.max(s, axis=1, keepdims=True))
            alpha, p = jnp.exp(m - m_new), jnp.exp(s - m_new)
            out.append((m_new, alpha * l + jnp.sum(p, axis=1, keepdims=True),
                        alpha * acc + bdot(p, kv_ref[pl.ds(k0, MLA_KEYS), _head(2 * h + 1)])))
        return out

    init = [(jnp.full((bq, 1), -1e30, f32), jnp.zeros((bq, 1), f32), jnp.zeros((bq, HEAD_DIM), f32)) for _ in hs]
    return lax.fori_loop(0, n_tiles, step, init)


def mla_fwd(q_nope, qr, kv, kpe, nb, t):
    bq = min(ATT_BQ, t)
    nq = t // bq
    sqn, sqr, skv, skp = _mla_specs(t, bq, nq)

    def body(qn_ref, qr_ref, kv_ref, kp_ref, o_ref):
        q0 = pl.program_id(2) * bq
        hs = range(MLA_HEADS_PER_STEP)
        qn = [qn_ref[:, _head(h)] for h in hs]
        qp = [qr_ref[:, h * MLA_ROPE:(h + 1) * MLA_ROPE] for h in hs]
        res = _mla_softmax_pass(qn, qp, kv_ref, kp_ref, q0, (q0 + bq) // MLA_KEYS, bq)
        for h in hs:
            _, l, acc = res[h]
            o_ref[:, _head(h)] = (acc / l).astype(bf16)

    return pl.pallas_call(
        body, name="mla_fwd", grid=(nb, N_HEADS // MLA_HEADS_PER_STEP, nq), in_specs=[sqn, sqr, skv, skp],
        out_specs=sqn, out_shape=jax.ShapeDtypeStruct((nb * t, 512), bf16),
        compiler_params=_cparams(("parallel", "arbitrary", "arbitrary")),
    )(q_nope, qr, kv, kpe)


def mla_bwd(q_nope, qr, kv, kpe, do, nb, t):
    bq = min(ATT_BQ, t)
    nq = t // bq
    sqn, sqr, skv, skp = _mla_specs(t, bq, nq)

    def body(qn_ref, qr_ref, kv_ref, kp_ref, do_ref, dqn_ref, dqr_ref, dkv_ref, dkp_ref):
        first_q = pl.program_id(2) == 0

        @pl.when(first_q)
        def _():
            dkv_ref[...] = jnp.zeros_like(dkv_ref)

        @pl.when(jnp.logical_and(first_q, pl.program_id(1) == 0))
        def _():
            dkp_ref[...] = jnp.zeros_like(dkp_ref)

        q0 = pl.program_id(2) * bq
        n_tiles = (q0 + bq) // MLA_KEYS
        hs = range(MLA_HEADS_PER_STEP)
        qn = [qn_ref[:, _head(h)] for h in hs]
        qp = [qr_ref[:, h * MLA_ROPE:(h + 1) * MLA_ROPE] for h in hs]
        do = [do_ref[:, _head(h)].astype(f32) for h in hs]
        stats = _mla_softmax_pass(qn, qp, kv_ref, kp_ref, q0, n_tiles, bq)
        lse = [m + jnp.log(l) for m, l, _ in stats]
        dsum = [jnp.sum(do[h] * (stats[h][2] / stats[h][1]), axis=1, keepdims=True) for h in hs]

        def grads(j, carry):
            k0 = pl.multiple_of(j * MLA_KEYS, MLA_KEYS)
            rows = pl.ds(k0, MLA_KEYS)
            kp = kp_ref[rows, :MLA_ROPE]
            out = []
            for h in hs:
                dqn, dqp = carry[h]
                g = jax.grad(mla_tile_loss, argnums=(0, 1, 2, 3, 4))(
                    qn[h], qp[h], kv_ref[rows, _head(2 * h)], kp, kv_ref[rows, _head(2 * h + 1)],
                    do[h], lse[h], dsum[h], q0, k0)
                dkv_ref[rows, _head(2 * h)] += g[2]
                dkp_ref[rows, :MLA_ROPE] += g[3]
                dkv_ref[rows, _head(2 * h + 1)] += g[4]
                out.append((dqn + g[0], dqp + g[1]))
            return out

        res = lax.fori_loop(0, n_tiles, grads,
                            [(jnp.zeros((bq, HEAD_DIM), f32), jnp.zeros((bq, MLA_ROPE), f32)) for _ in hs])
        for h in hs:
            dqn_ref[:, _head(h)] = res[h][0]
            dqr_ref[:, h * MLA_ROPE:(h + 1) * MLA_ROPE] = res[h][1]

    n = nb * t
    return pl.pallas_call(
        body, name="mla_bwd", grid=(nb, N_HEADS // MLA_HEADS_PER_STEP, nq),
        in_specs=[sqn, sqr, skv, skp, sqn], out_specs=[sqn, sqr, skv, skp],
        out_shape=[jax.ShapeDtypeStruct((n, 512), f32), jax.ShapeDtypeStruct((n, N_HEADS * MLA_ROPE), f32),
                   jax.ShapeDtypeStruct((n, 1024), f32), jax.ShapeDtypeStruct((n, LANE), f32)],
        compiler_params=_cparams(("arbitrary", "arbitrary", "arbitrary")),
    )(q_nope, qr, kv, kpe, do)


def loss_head(h, target):
    n, d = h.shape
    tm = _tile(n, 512)

    def body(h_ref, t_ref, l_ref, dh_ref):
        diff = h_ref[...] - t_ref[...]
        dh_ref[...] = diff * (1.0 / d)
        part = 0.5 * jnp.sum(jnp.sum(diff * diff, axis=1, keepdims=True) * (1.0 / d), axis=0, keepdims=True)

        @pl.when(pl.program_id(0) == 0)
        def _():
            l_ref[...] = jnp.zeros_like(l_ref)

        l_ref[...] += jnp.broadcast_to(part, l_ref.shape)

    spec = pl.BlockSpec((tm, d), lambda i: (i, 0))
    return pl.pallas_call(
        body, name="loss_head", grid=(n // tm,), in_specs=[spec, spec],
        out_specs=[pl.BlockSpec((8, LANE), lambda i: (0, 0)), spec],
        out_shape=[jax.ShapeDtypeStruct((8, LANE), f32), jax.ShapeDtypeStruct((n, d), f32)],
        compiler_params=_cparams(("arbitrary",)),
    )(h, target)


def _row(v):
    return v.reshape(1, -1)


def _pad_cols(a, n):
    return jnp.pad(a, ((0, 0), (0, n - a.shape[1])))


def _pad_row(v, n=LANE):
    return jnp.pad(v.reshape(1, -1), ((0, 0), (0, n - v.shape[0])))


def _group_matrix(width, group):
    idx = np.arange(width) // group
    return jnp.asarray((idx[:, None] == idx[None, :]).astype(np.float32))


def _head_expand():
    e = np.zeros((N_HEADS, LANE, HEAD_DIM), np.float32)
    for h in range(N_HEADS):
        e[h, h, :] = 1.0
    return jnp.asarray(e)


def _rope_freqs():
    inv = 1.0 / (ROPE_THETA ** (np.arange(0, MLA_ROPE, 2, dtype=np.float32) / MLA_ROPE))
    inv = np.tile(inv.astype(np.float32), 2)
    inv_q = np.tile(inv, N_HEADS).reshape(1, N_HEADS * MLA_ROPE)
    inv_k = np.zeros((1, LANE), np.float32)
    inv_k[0, :MLA_ROPE] = inv
    return jnp.asarray(inv_q), jnp.asarray(inv_k)


def _uq_split(w):
    w3 = w.reshape(w.shape[0], N_HEADS, HEAD_DIM + MLA_ROPE)
    return w3[:, :, :HEAD_DIM].reshape(-1, 512), w3[:, :, HEAD_DIM:].reshape(-1, N_HEADS * MLA_ROPE)


def _uq_merge(gn, gr):
    r = gn.shape[0]
    return jnp.concatenate([gn.reshape(r, N_HEADS, HEAD_DIM), gr.reshape(r, N_HEADS, MLA_ROPE)], axis=2).reshape(r, 768)


EARLY_GRADS = ['ffn1_w_up', 'ffn1_w_down', 'l1_w_in', 'l1_w_out', 'ffn0_w_up', 'ffn0_w_down', 'l0_w_out']


def local_step(x, positions, target, w, late_weights=None, scatter_early=None):
    w = dict(w)
    nb, t, d = x.shape
    n = nb * t
    tm = 256
    ni = n // tm
    tc = 2 * LANE
    h0 = x.reshape(n, d)
    tgt = target.reshape(n, d)
    pos = positions.reshape(n, 1).astype(f32)
    gh = _group_matrix(512, HEAD_DIM)
    gg = _group_matrix(512, 256)
    e_heads = _head_expand()
    inv_q, inv_k = _rope_freqs()
    g = {}

    def ln_stage(h, y, gname, bname):
        ops = [_rows(h, tm), _rows(y, tm, gdtype=bf16), _param(_row(w[gname])), _param(_row(w[bname]))]
        return ops, [_rows_out(n, d, tm)]

    def ln_fwd(name, ops):
        return block_fwd(lambda *a: f_ln(*a) * 2, name, (1, ni), ops, [_rows_out(n, d, tm), _rows_out(n, d, tm, bf16)])

    def ffn_act_stage(u, cw, cb):
        nj = D_FF // tc
        ops = [_cols(u, t, tc, 0, D_FF, bf16), _cols(u, t, tc, nj, D_FF, bf16)] \
            + [_cparam(cw[i:i + 1], tc) for i in range(3)] + [_cparam(_row(cb), tc)]
        return ops, [Out((n, D_FF), (t, tc), lambda j, i: (i, j), bf16)], (nj, nb)

    w_in0 = _pad_cols(w['l0_w_in'], L0_PAD)
    h0b = h0.astype(bf16)
    proj0 = mm(h0b, w_in0, "l0_proj")

    shift_ops = [_cols(proj0, t, tc, 0, RWKV_COLS, bf16), _cparam(_row(w['rwkv_mix']), tc)]
    shift_outs = [Out((n, RWKV_COLS), (t, tc), lambda j, i: (i, j))]
    shift_grid = (RWKV_COLS // tc, nb)
    (ps,) = block_fwd(f_shift_mix, "rwkv_shift", shift_grid, shift_ops, shift_outs)

    pre_ops = [_colblock(ps, tm, 512, 512), _colblock(ps, tm, 1536, 128), _colblock(ps, tm, 1664, 128),
               _param(_row(w['rwkv_w0'])), _param(w['rwkv_w2']), _param(_row(w['rwkv_a0'])), _param(w['rwkv_a2']),
               _param(w['rwkv_g2']), _param(_row(w['rwkv_k_k'])), _param(_row(w['rwkv_k_a'])), _param(gh, diff=False)]
    pre_outs = [_rows_out(n, 512, tm) for _ in range(5)]
    lw, k2, na, bb, gate_r = block_fwd(f_rwkv_pre, "rwkv_pre", (1, ni), pre_ops, pre_outs)
    ride = None if late_weights is None else GatherRide(late_weights[1])
    (y_tok, s0_saved), gathered = rwkv_scan_fwd(ps, lw, k2, na, bb, nb, t, ride)
    if late_weights is not None:
        for name, got in zip(late_weights[0], gathered):
            w[name] = late_weights[2](name, got)
    w_out0 = w['l0_w_out']

    post_ops = [_rows(y_tok, tm), _colblock(ps, tm, 0, 512), _rows(k2, tm), _colblock(ps, tm, 1024, 512),
                _rows(gate_r, tm), _param(_row(w['rwkv_ln_g'])), _param(_row(w['rwkv_ln_b'])),
                _param(w['rwkv_r_k'].reshape(1, 512)), _param(gh, diff=False)]
    post_outs = [_rows_out(n, 512, tm, bf16)]
    (y_a,) = block_fwd(f_rwkv_post, "rwkv_post", (1, ni), post_ops, post_outs)

    xbc_off = (RWKV_COLS + 512) // tc
    conv_ops = [_cols(proj0, t, tc, xbc_off, 1024, bf16)] + [_cparam(w['ssm_conv_w'][i:i + 1], tc) for i in range(4)] \
        + [_cparam(_row(w['ssm_conv_b']), tc)]
    conv_outs = [Out((n, 1024), (t, tc), lambda j, i: (i, j))]
    conv_grid = (1024 // tc, nb)
    (xbc_act,) = block_fwd(f_conv4_silu, "ssm_conv", conv_grid, conv_ops, conv_outs)

    dt_bias, a_log, d_skip = _pad_row(w['ssm_dt_bias']), _pad_row(w['ssm_a_log']), _pad_row(w['ssm_d'])
    y_ssd, ssd_states = ssd_fwd(xbc_act, proj0, dt_bias, a_log, d_skip, e_heads, nb, t)

    z_tok = proj0[:, RWKV_COLS:RWKV_COLS + 512]
    spost_ops = [_rows(y_ssd, tm), _rows(z_tok, tm, gdtype=bf16), _param(_row(w['ssm_norm_g'])), _param(gg, diff=False)]
    spost_outs = [_rows_out(n, 512, tm, bf16)]
    (y_b,) = block_fwd(f_ssm_post, "ssm_post", (1, ni), spost_ops, spost_outs)

    mixed0 = mm(y_b, w_out0[512:], "l0_out_b", add=mm(y_a, w_out0[:512], "l0_out_a"))
    ln1_ops, ln_outs = ln_stage(h0, mixed0, 'l0_ln1_g', 'l0_ln1_b')
    h1, h1b = ln_fwd("l0_ln1", ln1_ops)

    u0 = mm(h1b, w['ffn0_w_up'], "ffn0_up")
    act0_ops, act_outs, act_grid = ffn_act_stage(u0, w['ffn0_conv_w'], w['ffn0_conv_b'])
    (act0,) = block_fwd(f_ffn_act, "ffn0_act", act_grid, act0_ops, act_outs)
    f0 = mm(act0, w['ffn0_w_down'], "ffn0_down")
    ln2_ops, _ = ln_stage(h1, f0, 'l0_ln2_g', 'l0_ln2_b')
    h2, h2b = ln_fwd("l0_ln2", ln2_ops)

    w_in1 = _pad_cols(w['l1_w_in'], L1_PAD)
    w_out1 = w['l1_w_out']
    proj1 = mm(h2b, w_in1, "l1_proj")
    w_qn, w_qr = _uq_split(w['mla_w_uq'])
    mpre_ops = [_colblock(proj1, tm, 1536, 256, bf16), _colblock(proj1, tm, 1792, 128, bf16),
                _colblock(proj1, tm, 1920, 128, bf16),
                Op(pos, (tm, 1), lambda j, i: (i, 0), diff=False),
                _param(_row(w['mla_q_norm_g'])), _param(w_qn), _param(w_qr),
                _param(_row(w['mla_kv_norm_g'])), _param(w['mla_w_ukv']), _param(inv_q, diff=False),
                _param(inv_k, diff=False)]
    mpre_outs = [_rows_out(n, 512, tm), _rows_out(n, N_HEADS * MLA_ROPE, tm), _rows_out(n, 1024, tm),
                 _rows_out(n, LANE, tm)]
    q_nope, q_rope, kv, kpe = block_fwd(f_mla_pre, "mla_pre", (1, ni), mpre_ops, mpre_outs)
    o_sb = sb_fwd(proj1, nb, t)
    o_mla = mla_fwd(q_nope, q_rope, kv, kpe, nb, t)

    mixed1 = mm(o_mla, w_out1[512:], "l1_out_b", add=mm(o_sb, w_out1[:512], "l1_out_a"))
    ln3_ops, _ = ln_stage(h2, mixed1, 'l1_ln1_g', 'l1_ln1_b')
    h3, h3b = ln_fwd("l1_ln1", ln3_ops)
    u1 = mm(h3b, w['ffn1_w_up'], "ffn1_up")
    act1_ops, _, _ = ffn_act_stage(u1, w['ffn1_conv_w'], w['ffn1_conv_b'])
    (act1,) = block_fwd(f_ffn_act, "ffn1_act", act_grid, act1_ops, act_outs)
    f1 = mm(act1, w['ffn1_w_down'], "ffn1_down")
    ln4_ops, _ = ln_stage(h3, f1, 'l1_ln2_g', 'l1_ln2_b')
    (h4,) = block_fwd(f_ln, "l1_ln2", (1, ni), ln4_ops, ln_outs)

    loss_part, dh4 = loss_head(h4, tgt)

    def vec(a_):
        return a_.reshape(-1)

    def ffn_bwd(tag, dh_out, ln_ops, act_ops, h_in, act, w_up, w_down, names):
        dh_res, df, gg_, gb_ = block_bwd(f_ln, tag + "_ln2_bwd", (1, ni), ln_ops, ln_outs, [dh_out])
        g[names[4]], g[names[5]] = vec(gg_), vec(gb_)
        g[names[3]] = mm(act, df, tag + "_down_dw", ta=True)
        dact = mm(df, w_down.T, tag + "_down_dx")
        dgate, dup, dw0, dw1, dw2, dcb = block_bwd(f_ffn_act, tag + "_act_bwd", act_grid, act_ops, act_outs, [dact])
        g[names[1]] = jnp.concatenate([dw0, dw1, dw2], axis=0)
        g[names[2]] = vec(dcb)
        g[names[0]] = jnp.concatenate([mm(h_in, dgate, tag + "_gate_dw", ta=True),
                                       mm(h_in, dup, tag + "_upv_dw", ta=True)], axis=1)
        w_up_t = w_up.T
        dh = mm(dgate, w_up_t[:D_FF], tag + "_gate_dx", add=dh_res)
        return mm(dup, w_up_t[D_FF:], tag + "_upv_dx", add=dh)

    def out_bwd(tag, dmixed, y_first, y_second, w_out, name):
        g[name] = jnp.concatenate([mm(y_first, dmixed, tag + "_a_dw", ta=True),
                                   mm(y_second, dmixed, tag + "_b_dw", ta=True)], axis=0)
        w_t = w_out.T
        return mm(dmixed, w_t[:, :512], tag + "_a_dx"), mm(dmixed, w_t[:, 512:], tag + "_b_dx")

    dh3 = ffn_bwd("ffn1", dh4, ln4_ops, act1_ops, h3b, act1, w['ffn1_w_up'], w['ffn1_w_down'],
                  ['ffn1_w_up', 'ffn1_conv_w', 'ffn1_conv_b', 'ffn1_w_down', 'l1_ln2_g', 'l1_ln2_b'])

    dh2_res, dmixed1, g3g, g3b = block_bwd(f_ln, "l1_ln1_bwd", (1, ni), ln3_ops, ln_outs, [dh3])
    g['l1_ln1_g'], g['l1_ln1_b'] = vec(g3g), vec(g3b)
    do_sb, do_mla = out_bwd("l1_out", dmixed1, o_sb, o_mla, w_out1, 'l1_w_out')

    dq_nope, dq_rope, dkv, dkpe = mla_bwd(q_nope, q_rope, kv, kpe, do_mla, nb, t)
    dsb_q, dsb_k, dsb_v = sb_bwd(proj1, do_sb, nb, t)
    (dc_q, dc_kv, dkpe_raw, gqg, gwqn, gwqr, gkvg, g['mla_w_ukv']) = block_bwd(
        f_mla_pre, "mla_pre_bwd", (1, ni), mpre_ops, mpre_outs, [dq_nope, dq_rope, dkv, dkpe])
    g['mla_q_norm_g'], g['mla_kv_norm_g'] = vec(gqg), vec(gkvg)
    g['mla_w_uq'] = _uq_merge(gwqn, gwqr)
    dproj1 = jnp.concatenate([dsb_q.astype(bf16), dsb_k.astype(bf16), dsb_v.astype(bf16), dc_q, dc_kv, dkpe_raw],
                             axis=1)
    g['l1_w_in'] = mm(h2b, dproj1, "l1_proj_dw", ta=True)[:, :L1_COLS]
    dh2 = mm(dproj1, w_in1.T, "l1_proj_dx", add=dh2_res)

    dh1 = ffn_bwd("ffn0", dh2, ln2_ops, act0_ops, h1b, act0, w['ffn0_w_up'], w['ffn0_w_down'],
                  ['ffn0_w_up', 'ffn0_conv_w', 'ffn0_conv_b', 'ffn0_w_down', 'l0_ln2_g', 'l0_ln2_b'])

    dh0_res, dmixed0, g1g, g1b = block_bwd(f_ln, "l0_ln1_bwd", (1, ni), ln1_ops, ln_outs, [dh1])
    g['l0_ln1_g'], g['l0_ln1_b'] = vec(g1g), vec(g1b)
    dy_a, dy_b = out_bwd("l0_out", dmixed0, y_a, y_b, w_out0, 'l0_w_out')

    dy_ssd, dz, gng = block_bwd(f_ssm_post, "ssm_post_bwd", (1, ni), spost_ops, spost_outs, [dy_b])
    g['ssm_norm_g'] = vec(gng)
    dxs, dbm, dcm, ddt_raw, gdb, gal, gdsk = ssd_bwd(xbc_act, proj0, dt_bias, a_log, d_skip, e_heads, ssd_states,
                                                     dy_ssd, nb, t)
    g['ssm_dt_bias'], g['ssm_a_log'], g['ssm_d'] = gdb[0, :8], gal[0, :8], gdsk[0, :8]
    dxbc_act = jnp.concatenate([dxs, dbm, dcm], axis=1)
    dxbc, cw0, cw1, cw2, cw3, gcb = block_bwd(f_conv4_silu, "ssm_conv_bwd", conv_grid, conv_ops, conv_outs, [dxbc_act])
    g['ssm_conv_w'] = jnp.concatenate([cw0, cw1, cw2, cw3], axis=0)
    g['ssm_conv_b'] = vec(gcb)

    dy_tok, dr_post, dk2_post, dv_post, dgate, glg, glb, grk = block_bwd(
        f_rwkv_post, "rwkv_post_bwd", (1, ni), post_ops, post_outs, [dy_a])
    g['rwkv_ln_g'], g['rwkv_ln_b'], g['rwkv_r_k'] = vec(glg), vec(glb), grk.reshape(N_HEADS, HEAD_DIM)
    ride = None if scatter_early is None else ScatterRide(scatter_early({name: g[name] for name in EARLY_GRADS}))
    (dr, dlw, dk2, dv, dna, dbb), early = rwkv_scan_bwd(s0_saved, ps, lw, k2, na, bb, dy_tok, nb, t, ride)
    (dk_pre, dwa_lo, dg_lo, gw0, g['rwkv_w2'], ga0, g['rwkv_a2'], g['rwkv_g2'], gkk, gka) = block_bwd(
        f_rwkv_pre, "rwkv_pre_bwd", (1, ni), pre_ops, pre_outs, [dlw, dk2 + dk2_post, dna, dbb, dgate])
    g['rwkv_w0'], g['rwkv_a0'], g['rwkv_k_k'], g['rwkv_k_a'] = vec(gw0), vec(ga0), vec(gkk), vec(gka)
    dps = jnp.concatenate([dr + dr_post, dk_pre, dv + dv_post, dwa_lo, dg_lo], axis=1)
    dp_rwkv, gmix = block_bwd(f_shift_mix, "rwkv_shift_bwd", shift_grid, shift_ops, shift_outs, [dps])
    g['rwkv_mix'] = vec(gmix)

    dproj0 = jnp.concatenate([dp_rwkv, dz, dxbc, ddt_raw], axis=1)
    g['l0_w_in'] = mm(h0b, dproj0, "l0_proj_dw", ta=True)[:, :L0_COLS]
    grad_x = mm(dproj0, w_in0.T, "l0_proj_dx", add=dh0_res)
    return loss_part, grad_x.reshape(nb, t, d), g, early


MESH = pl.DeviceIdType.MESH
ANY = pl.BlockSpec(memory_space=pl.ANY)
AXES = ("x", "y", "c")


def _place():
    x, y, c = lax.axis_index("x"), lax.axis_index("y"), lax.axis_index("c")
    chips = [(1 - x, y), (x, 1 - y), (1 - x, 1 - y)]
    return x, y, c, chips


def _dma_sems(n):
    return pltpu.SemaphoreType.DMA((n,))


class GatherRide:
    def __init__(self, shards):
        n = len(shards)
        self.inputs = list(shards)
        self.out_shapes = [jax.ShapeDtypeStruct((N_SHARD,) + a.shape, a.dtype) for a in shards]
        self.scratch = [_dma_sems(3 * n), _dma_sems(3 * n), _dma_sems(3 * n), _dma_sems(3 * n), _dma_sems(n)]

    def _copies(self, ins, outs, sems):
        ici_send, ici_recv, d2d_send, d2d_recv, local_sems = sems
        x, y, c, chips = _place()
        me = 2 * x + y
        pairs = list(enumerate(zip(ins, outs)))

        def over_ici(k, j, slot, to):
            return pltpu.make_async_remote_copy(
                src_ref=ins[k].at[c], dst_ref=outs[k].at[slot, c], send_sem=ici_send.at[3 * k + j],
                recv_sem=ici_recv.at[3 * k + j], device_id=to, device_id_type=MESH)

        def to_sibling(k, j, slot, half):
            return pltpu.make_async_remote_copy(
                src_ref=outs[k].at[slot, half], dst_ref=outs[k].at[slot, half], send_sem=d2d_send.at[3 * k + j],
                recv_sem=d2d_recv.at[3 * k + j], device_id=(x, y, 1 - c), device_id_type=MESH)

        mine = [pltpu.make_async_copy(a, o.at[me], local_sems.at[k]) for k, (a, o) in pairs]
        sends = [over_ici(k, j, me, (cx, cy, c)) for k, _ in pairs for j, (cx, cy) in enumerate(chips)]
        return c, chips, pairs, over_ici, to_sibling, mine, sends

    def start(self, ins, outs, sems):
        _, _, _, _, _, mine, sends = self._copies(ins, outs, sems)
        for cp in mine + sends:
            cp.start()

    def finish(self, ins, outs, sems):
        c, chips, pairs, over_ici, to_sibling, mine, sends = self._copies(ins, outs, sems)
        passed = []
        for k, _ in pairs:
            for j, (cx, cy) in enumerate(chips):
                over_ici(k, j, 2 * cx + cy, (cx, cy, c)).wait_recv()
                passed.append(to_sibling(k, j, 2 * cx + cy, c))
                passed[-1].start()
        for k, _ in pairs:
            for j, (cx, cy) in enumerate(chips):
                to_sibling(k, j, 2 * cx + cy, 1 - c).wait_recv()
        for cp in sends + passed:
            cp.wait_send()
        for cp in mine:
            cp.wait()


class ScatterRide:
    def __init__(self, parts):
        n = len(parts)
        self.inputs = list(parts)
        self.out_shapes = [jax.ShapeDtypeStruct(a.shape, a.dtype) for a in parts]
        self.scratch = [_dma_sems(3 * n), _dma_sems(3 * n), _dma_sems(n)]

    def _copies(self, ins, outs, sems):
        send_sems, recv_sems, local_sems = sems
        x, y, c, chips = _place()
        me = 2 * x + y
        pairs = list(enumerate(zip(ins, outs)))

        def over_ici(k, j, src_slot, dst_slot, to):
            return pltpu.make_async_remote_copy(
                src_ref=ins[k].at[src_slot], dst_ref=outs[k].at[dst_slot], send_sem=send_sems.at[3 * k + j],
                recv_sem=recv_sems.at[3 * k + j], device_id=to, device_id_type=MESH)

        mine = [pltpu.make_async_copy(a.at[me], o.at[me], local_sems.at[k]) for k, (a, o) in pairs]
        sends = [over_ici(k, j, 2 * cx + cy, me, (cx, cy, c)) for k, _ in pairs for j, (cx, cy) in enumerate(chips)]
        arrivals = [over_ici(k, j, me, 2 * cx + cy, (cx, cy, c)) for k, _ in pairs for j, (cx, cy) in enumerate(chips)]
        return mine, sends, arrivals

    def start(self, ins, outs, sems):
        mine, sends, _ = self._copies(ins, outs, sems)
        for cp in mine + sends:
            cp.start()

    def finish(self, ins, outs, sems):
        mine, sends, arrivals = self._copies(ins, outs, sems)
        for cp in arrivals:
            cp.wait_recv()
        for cp in sends:
            cp.wait_send()
        for cp in mine:
            cp.wait()


def _run_ride(ride, name):
    n = len(ride.inputs)

    def body(*refs):
        ins, outs, sems = refs[:n], refs[n:2 * n], refs[2 * n:]
        ride.start(ins, outs, sems)
        ride.finish(ins, outs, sems)

    return pl.pallas_call(body, name=name, in_specs=[ANY] * n, out_specs=[ANY] * n, out_shape=ride.out_shapes,
                          scratch_shapes=ride.scratch)(*ride.inputs)


def gather_shards(shards):
    return _run_ride(GatherRide(shards), "gather_shards")


def swap_halves(pieces, name):
    n = len(pieces)

    def body(*refs):
        ins, outs = refs[:n], refs[n:2 * n]
        send_sems, recv_sems = refs[2 * n:]
        x, y, c, _ = _place()
        cps = [pltpu.make_async_remote_copy(
            src_ref=a.at[:, 1 - c], dst_ref=o, send_sem=send_sems.at[k], recv_sem=recv_sems.at[k],
            device_id=(x, y, 1 - c), device_id_type=MESH) for k, (a, o) in enumerate(zip(ins, outs))]
        for cp in cps:
            cp.start()
        for cp in cps:
            cp.wait()

    return pl.pallas_call(
        body, name=name, in_specs=[ANY] * n, out_specs=[ANY] * n,
        out_shape=[jax.ShapeDtypeStruct((a.shape[0],) + a.shape[2:], a.dtype) for a in pieces],
        scratch_shapes=[_dma_sems(n), _dma_sems(n)],
    )(*pieces)


def scatter_to_chips(parts):
    return _run_ride(ScatterRide(parts), "scatter_to_chips")


def share_halves(bufs):
    n = len(bufs)

    def body(*refs):
        ins, outs = refs[:n], refs[n:2 * n]
        send_sems, recv_sems = refs[2 * n:]
        x, y, c, _ = _place()
        cps = [pltpu.make_async_remote_copy(
            src_ref=a.at[c], dst_ref=o.at[c], send_sem=send_sems.at[k], recv_sem=recv_sems.at[k],
            device_id=(x, y, 1 - c), device_id_type=MESH) for k, (a, o) in enumerate(zip(ins, outs))]
        for cp in cps:
            cp.start()
        for k, (a, o) in enumerate(zip(ins, outs)):
            cps[k].wait_send()
            pltpu.make_async_remote_copy(
                src_ref=a.at[c], dst_ref=o.at[1 - c], send_sem=send_sems.at[k], recv_sem=recv_sems.at[k],
                device_id=(x, y, 1 - c), device_id_type=MESH).wait_recv()

    return pl.pallas_call(
        body, name="share_halves", in_specs=[ANY] * n, out_specs=[ANY] * n,
        out_shape=[jax.ShapeDtypeStruct(a.shape, a.dtype) for a in bufs],
        input_output_aliases={k: k for k in range(n)},
        scratch_shapes=[_dma_sems(n), _dma_sems(n)],
    )(*bufs)


def pair_add(piece, recv, core, name, out_dtype):
    _, _, h, cdim = piece.shape
    tr = _rtile(h, cdim)

    def body(c_ref, a_ref, b_ref, o_ref):
        o_ref[...] = (a_ref[0] + b_ref[...]).astype(o_ref.dtype)

    spec = pl.BlockSpec((1, tr, cdim), lambda p, i, c_ref: (p, i, 0))
    return pl.pallas_call(
        body, name=name,
        grid_spec=pltpu.PrefetchScalarGridSpec(
            num_scalar_prefetch=1, grid=(N_SHARD, h // tr),
            in_specs=[pl.BlockSpec((1, 1, tr, cdim), lambda p, i, c_ref: (p, c_ref[0], i, 0)), spec],
            out_specs=spec),
        out_shape=jax.ShapeDtypeStruct((N_SHARD, h, cdim), out_dtype),
        compiler_params=_cparams(("parallel", "parallel")),
    )(core, piece, recv)


def chip_add(parts, core, name):
    _, h, cdim = parts.shape
    tr = _rtile(h, cdim, 1024 * 1024)

    def body(c_ref, p_ref, o_ref):
        p = [p_ref[s].astype(f32) for s in range(N_SHARD)]
        o_ref[0] = ((p[0] + p[1]) + p[2]) + p[3]

    return pl.pallas_call(
        body, name=name,
        grid_spec=pltpu.PrefetchScalarGridSpec(
            num_scalar_prefetch=1, grid=(h // tr,),
            in_specs=[pl.BlockSpec((N_SHARD, tr, cdim), lambda i, c_ref: (0, i, 0))],
            out_specs=pl.BlockSpec((1, tr, cdim), lambda i, c_ref: (c_ref[0], i, 0))),
        out_shape=jax.ShapeDtypeStruct((2, h, cdim), f32), compiler_params=_cparams(("parallel",)),
    )(core, parts)


def adamw(w, g, m, v, name):
    rows, cdim = w.shape
    tr = _rtile(rows, cdim, 1024 * 1024)

    def body(w_ref, g_ref, m_ref, v_ref, d_ref, nm_ref, nv_ref):
        gv = g_ref[...]
        m_new = ADAM_B1 * m_ref[...] + (1.0 - ADAM_B1) * gv
        v_new = ADAM_B2 * v_ref[...] + (1.0 - ADAM_B2) * jnp.square(gv)
        m_hat = m_new / (1.0 - ADAM_B1 ** ADAM_STEP)
        v_hat = v_new / (1.0 - ADAM_B2 ** ADAM_STEP)
        d_ref[...] = -ADAM_LR * (m_hat / (jnp.sqrt(v_hat) + ADAM_EPS) + ADAM_WD * w_ref[...])
        nm_ref[...] = m_new
        nv_ref[...] = v_new

    spec = pl.BlockSpec((tr, cdim), lambda i: (i, 0))
    return pl.pallas_call(body, name=name, grid=(rows // tr,), in_specs=[spec] * 4, out_specs=[spec] * 3,
                          out_shape=[jax.ShapeDtypeStruct(w.shape, f32)] * 3,
                          compiler_params=_cparams(("parallel",)))(w, g, m, v)


SMALL_MULTIPLE = 16 * LANE


def _pack_flat(parts, multiple=SMALL_MULTIPLE):
    flat = jnp.concatenate([p.reshape(-1) for p in parts])
    pad = (-flat.shape[0]) % multiple
    return jnp.pad(flat, (0, pad)).reshape(-1, LANE)


def _unpack_flat(buf, shapes):
    flat = buf.reshape(-1)
    out, off = [], 0
    for s in shapes:
        cnt = int(np.prod(s))
        out.append(flat[off:off + cnt].reshape(s))
        off += cnt
    return out


def _full_from_shards(name, gathered):
    if name in COL_SHARDED:
        return jnp.concatenate([gathered[s] for s in range(N_SHARD)], axis=1)
    return gathered.reshape(-1, gathered.shape[2])


def _pieces(name, grad):
    if name in COL_SHARDED:
        r, cdim = grad.shape
        return grad.reshape(r, N_SHARD, cdim // N_SHARD).transpose(1, 0, 2)
    return grad.reshape(N_SHARD, grad.shape[0] // N_SHARD, grad.shape[1])


def _small_pieces(name, grad):
    if name in COL_SHARDED or name in ROW_SHARDED:
        return _pieces(name, grad).reshape(N_SHARD, -1)
    return jnp.broadcast_to(grad.reshape(1, -1), (N_SHARD, grad.size))


def kernel(x, positions, l0_w_in, rwkv_mix, rwkv_w0, rwkv_w2, rwkv_a0, rwkv_a2, rwkv_g2, rwkv_k_k, rwkv_k_a, rwkv_r_k, rwkv_ln_g, rwkv_ln_b, ssm_conv_w, ssm_conv_b, ssm_dt_bias, ssm_a_log, ssm_d, ssm_norm_g, l0_w_out, l0_ln1_g, l0_ln1_b, ffn0_w_up, ffn0_conv_w, ffn0_conv_b, ffn0_w_down, l0_ln2_g, l0_ln2_b, l1_w_in, mla_q_norm_g, mla_w_uq, mla_kv_norm_g, mla_w_ukv, l1_w_out, l1_ln1_g, l1_ln1_b, ffn1_w_up, ffn1_conv_w, ffn1_conv_b, ffn1_w_down, l1_ln2_g, l1_ln2_b, loss_target, m_l0_w_in, m_rwkv_mix, m_rwkv_w0, m_rwkv_w2, m_rwkv_a0, m_rwkv_a2, m_rwkv_g2, m_rwkv_k_k, m_rwkv_k_a, m_rwkv_r_k, m_rwkv_ln_g, m_rwkv_ln_b, m_ssm_conv_w, m_ssm_conv_b, m_ssm_dt_bias, m_ssm_a_log, m_ssm_d, m_ssm_norm_g, m_l0_w_out, m_l0_ln1_g, m_l0_ln1_b, m_ffn0_w_up, m_ffn0_conv_w, m_ffn0_conv_b, m_ffn0_w_down, m_l0_ln2_g, m_l0_ln2_b, m_l1_w_in, m_mla_q_norm_g, m_mla_w_uq, m_mla_kv_norm_g, m_mla_w_ukv, m_l1_w_out, m_l1_ln1_g, m_l1_ln1_b, m_ffn1_w_up, m_ffn1_conv_w, m_ffn1_conv_b, m_ffn1_w_down, m_l1_ln2_g, m_l1_ln2_b, v_l0_w_in, v_rwkv_mix, v_rwkv_w0, v_rwkv_w2, v_rwkv_a0, v_rwkv_a2, v_rwkv_g2, v_rwkv_k_k, v_rwkv_k_a, v_rwkv_r_k, v_rwkv_ln_g, v_rwkv_ln_b, v_ssm_conv_w, v_ssm_conv_b, v_ssm_dt_bias, v_ssm_a_log, v_ssm_d, v_ssm_norm_g, v_l0_w_out, v_l0_ln1_g, v_l0_ln1_b, v_ffn0_w_up, v_ffn0_conv_w, v_ffn0_conv_b, v_ffn0_w_down, v_l0_ln2_g, v_l0_ln2_b, v_l1_w_in, v_mla_q_norm_g, v_mla_w_uq, v_mla_kv_norm_g, v_mla_w_ukv, v_l1_w_out, v_l1_ln1_g, v_l1_ln1_b, v_ffn1_w_up, v_ffn1_conv_w, v_ffn1_conv_b, v_ffn1_w_down, v_l1_ln2_g, v_l1_ln2_b):
    args = locals()
    w_loc = {n: args[n] for n in WEIGHTS}
    m_loc = {n: args["m_" + n] for n in WEIGHTS}
    v_loc = {n: args["v_" + n] for n in WEIGHTS}
    core = lax.axis_index("c").astype(jnp.int32).reshape(1)

    small_sharded = [n for n in SMALL if n in COL_SHARDED]
    halves = lambda a: a.reshape(2, a.shape[0] // 2, a.shape[1])
    whole = lambda name, got: _full_from_shards(name, got.reshape(N_SHARD, -1, got.shape[3]))
    first = gather_shards([halves(w_loc['l0_w_in'].astype(bf16)), halves(_pack_flat([w_loc[n] for n in small_sharded]))])
    w_have = {n: w_loc[n] for n in WEIGHTS if n not in BIG}
    w_have['l0_w_in'] = whole('l0_w_in', first[0])
    small_all = first[1].reshape(N_SHARD, -1, LANE)
    per_shard = [_unpack_flat(small_all[s], [w_loc[n].shape for n in small_sharded]) for s in range(N_SHARD)]
    for k, n in enumerate(small_sharded):
        w_have[n] = jnp.concatenate([per_shard[s][k] for s in range(N_SHARD)], axis=1)
    late = [n for n in BIG if n != 'l0_w_in']

    def pair_sums(names, pieces, tag):
        pieces = [p.reshape(N_SHARD, 2, p.shape[1] // 2, p.shape[2]) for p in pieces]
        return [pair_add(p, r, core, "pair_add_" + n, f32 if n == 'small' else bf16)
                for n, p, r in zip(names, pieces, swap_halves(pieces, "swap_halves_" + tag))]

    loss_part, grad_x, g_full, early = local_step(
        x, positions, loss_target, w_have, (late, [halves(w_loc[n].astype(bf16)) for n in late], whole),
        lambda gd: pair_sums(EARLY_GRADS, [_pieces(n, gd[n]) for n in EARLY_GRADS], "early"))
    loss = lax.psum(loss_part[0, 0], AXES)

    small_flat = jnp.concatenate([_small_pieces(n, g_full[n]) for n in SMALL], axis=1)
    pad = (-small_flat.shape[1]) % SMALL_MULTIPLE
    small_pieces = jnp.pad(small_flat, ((0, 0), (0, pad))).reshape(N_SHARD, -1, LANE)
    rest = scatter_to_chips(pair_sums(['l0_w_in', 'small'], [_pieces('l0_w_in', g_full['l0_w_in']), small_pieces],
                                      "rest"))
    from_chips = dict(zip(EARLY_GRADS + ['l0_w_in', 'small'], list(early) + list(rest)))
    units = BIG + ['small']
    both = share_halves([chip_add(from_chips[n], core, "chip_add_" + n) for n in units])
    reduced = [b.reshape(-1, b.shape[2]) for b in both]

    out = {}
    for n, gred in zip(BIG, reduced):
        out[n] = (gred,) + tuple(adamw(w_loc[n], gred, m_loc[n], v_loc[n], "adamw_" + n))
    shapes = [w_loc[n].shape for n in SMALL]
    packs = [_pack_flat([d[n] for n in SMALL]) for d in (w_loc, m_loc, v_loc)]
    small_res = (reduced[-1],) + tuple(adamw(packs[0], reduced[-1], packs[1], packs[2], "adamw_small"))
    small_unpacked = [_unpack_flat(b, shapes) for b in small_res]
    for k, n in enumerate(SMALL):
        out[n] = tuple(u[k] for u in small_unpacked)
    return (loss, grad_x, *[out[n][0] for n in WEIGHTS], *[out[n][1] for n in WEIGHTS],
            *[out[n][2] for n in WEIGHTS], *[out[n][3] for n in WEIGHTS])
```

```python
import functools

import numpy as np
import jax
import jax.numpy as jnp
from jax import lax
from jax.experimental import pallas as pl
from jax.experimental.pallas import tpu as pltpu

f32 = jnp.float32
bf16 = jnp.bfloat16
HI = lax.Precision.HIGHEST
MID = lax.Precision.HIGH

D_MODEL = 1024
HEAD_DIM = 64
N_HEADS = 8
RWKV_COLS = 1792
RWKV_GN_EPS = 64e-5
SSM_STATE = 128
SSM_CHUNK = 128
L0_COLS = 3336
L0_PAD = 3456
L1_COLS = 1952
L1_PAD = 2048
MLA_ROPE = 32
ROPE_THETA = 10000.0
D_FF = 2816
DEPTH = 2
ALPHA = (2 * DEPTH) ** 0.25
ADAM_LR = 0.001
ADAM_B1 = 0.9
ADAM_B2 = 0.999
ADAM_EPS = 1e-08
ADAM_WD = 0.01
ADAM_STEP = 10
RWKV_CHUNK = 64
RWKV_HEADS_PER_STEP = 8
LANE = 128
SUBLANE = 8
VMEM_LIMIT = 56 * 1024 * 1024

WEIGHTS = ['l0_w_in', 'rwkv_mix', 'rwkv_w0', 'rwkv_w2', 'rwkv_a0', 'rwkv_a2', 'rwkv_g2', 'rwkv_k_k', 'rwkv_k_a',
           'rwkv_r_k', 'rwkv_ln_g', 'rwkv_ln_b', 'ssm_conv_w', 'ssm_conv_b', 'ssm_dt_bias', 'ssm_a_log', 'ssm_d',
           'ssm_norm_g', 'l0_w_out', 'l0_ln1_g', 'l0_ln1_b', 'ffn0_w_up', 'ffn0_conv_w', 'ffn0_conv_b',
           'ffn0_w_down', 'l0_ln2_g', 'l0_ln2_b', 'l1_w_in', 'mla_q_norm_g', 'mla_w_uq', 'mla_kv_norm_g',
           'mla_w_ukv', 'l1_w_out', 'l1_ln1_g', 'l1_ln1_b', 'ffn1_w_up', 'ffn1_conv_w', 'ffn1_conv_b',
           'ffn1_w_down', 'l1_ln2_g', 'l1_ln2_b']
COL_SHARDED = ['l0_w_in', 'rwkv_w2', 'rwkv_a2', 'rwkv_g2', 'ssm_conv_w', 'ffn0_w_up', 'ffn0_conv_w', 'l1_w_in',
               'mla_w_uq', 'mla_w_ukv', 'ffn1_w_up', 'ffn1_conv_w']
ROW_SHARDED = ['l0_w_out', 'ffn0_w_down', 'l1_w_out', 'ffn1_w_down']
BIG = ['l0_w_in', 'l0_w_out', 'ffn0_w_up', 'ffn0_w_down', 'l1_w_in', 'l1_w_out', 'ffn1_w_up', 'ffn1_w_down']
SMALL = [n for n in WEIGHTS if n not in BIG]
N_SHARD = 4


def _cparams(sem):
    return pltpu.CompilerParams(dimension_semantics=sem, vmem_limit_bytes=VMEM_LIMIT)


def _dg(a, b, ca, cb, prec=None):
    return lax.dot_general(a, b, (((ca,), (cb,)), ((), ())), precision=prec, preferred_element_type=f32)


def hdot(a, b):
    return _dg(a, b, 1, 0, HI)


def mdot(a, b):
    return _dg(a, b, 1, 0, MID)


def mdot_nt(a, b):
    return _dg(a, b, 1, 1, MID)


def mdot_tn(a, b):
    return _dg(a, b, 0, 0, MID)


def _b(x):
    return x.astype(bf16)


@jax.custom_vjp
def bdot(x, w):
    return _dg(_b(x), _b(w), 1, 0)


def _bdot_fwd(x, w):
    return bdot(x, w), (x, w)


def _bdot_bwd(res, g):
    x, w = res
    return _dg(_b(g), _b(w), 1, 1).astype(x.dtype), _dg(_b(x), _b(g), 0, 0).astype(w.dtype)


bdot.defvjp(_bdot_fwd, _bdot_bwd)


@jax.custom_vjp
def bdot_nt(x, y):
    return _dg(_b(x), _b(y), 1, 1)


def _bdot_nt_fwd(x, y):
    return bdot_nt(x, y), (x, y)


def _bdot_nt_bwd(res, g):
    x, y = res
    return _dg(_b(g), _b(y), 1, 0), _dg(_b(g), _b(x), 0, 0)


bdot_nt.defvjp(_bdot_nt_fwd, _bdot_nt_bwd)


@jax.custom_vjp
def bdot_tn(x, y):
    return _dg(_b(x), _b(y), 0, 0)


def _bdot_tn_fwd(x, y):
    return bdot_tn(x, y), (x, y)


def _bdot_tn_bwd(res, g):
    x, y = res
    return _dg(_b(y), _b(g), 1, 1), _dg(_b(x), _b(g), 1, 0)


bdot_tn.defvjp(_bdot_tn_fwd, _bdot_tn_bwd)


def _sigmoid(x):
    return 1.0 / (1.0 + jnp.exp(-x))


@jax.custom_vjp
def softplus(x):
    e = jnp.exp(-jnp.abs(x))
    u = 1.0 + e
    log1p = jnp.where(u == 1.0, e, jnp.log(u) * e / jnp.where(u == 1.0, 1.0, u - 1.0))
    return jnp.maximum(x, 0.0) + log1p


def _softplus_fwd(x):
    return softplus(x), x


def _softplus_bwd(x, g):
    return (g * _sigmoid(x),)


softplus.defvjp(_softplus_fwd, _softplus_bwd)


@jax.custom_vjp
def softplus_abs(x):
    return jnp.maximum(x, 0.0) + jnp.log(1.0 + jnp.exp(-jnp.abs(x)))


def _softplus_abs_fwd(x):
    return softplus_abs(x), x


softplus_abs.defvjp(_softplus_abs_fwd, _softplus_bwd)


def _two_pass(x, m):
    hi = _b(x)
    lo = _b(x - hi.astype(f32))
    m16 = _b(m)
    return _dg(hi, m16, 1, 0) + _dg(lo, m16, 1, 0)


def _upper(n):
    return (_iota2((n, n), 0) > _iota2((n, n), 1)).astype(f32)


@jax.custom_vjp
def suffix_sums(x):
    return _two_pass(x, _upper(x.shape[1]))


def _suffix_sums_fwd(x):
    return suffix_sums(x), None


def _suffix_sums_bwd(_, g):
    return (_two_pass(g, _upper(g.shape[1]).T),)


suffix_sums.defvjp(_suffix_sums_fwd, _suffix_sums_bwd)


def silu(x):
    return x * _sigmoid(x)


def _shift_rows(x, k, up):
    if k == 0:
        return x
    t = x.shape[0]
    rows = lax.broadcasted_iota(jnp.int32, x.shape, 0)
    if up:
        return jnp.where(rows < t - k, pltpu.roll(x, t - k, 0), 0.0)
    return jnp.where(rows >= k, pltpu.roll(x, k, 0), 0.0)


@functools.partial(jax.custom_vjp, nondiff_argnums=(1,))
def shift_down(x, k):
    return _shift_rows(x, k, False)


def _shift_down_fwd(x, k):
    return _shift_rows(x, k, False), None


def _shift_down_bwd(k, _, g):
    return (_shift_rows(g, k, True),)


shift_down.defvjp(_shift_down_fwd, _shift_down_bwd)


@functools.partial(jax.custom_vjp, nondiff_argnums=(1,))
def lane_roll(x, s):
    return pltpu.roll(x, s % x.shape[1], 1)


def _lane_roll_fwd(x, s):
    return lane_roll(x, s), None


def _lane_roll_bwd(s, _, g):
    return (pltpu.roll(g, (-s) % g.shape[1], 1),)


lane_roll.defvjp(_lane_roll_fwd, _lane_roll_bwd)


def rot_half32(x):
    first = (lax.broadcasted_iota(jnp.int32, x.shape, 1) % MLA_ROPE) < (MLA_ROPE // 2)
    return jnp.where(first, -lane_roll(x, -(MLA_ROPE // 2)), lane_roll(x, MLA_ROPE // 2))


def _iota2(shape, axis):
    return lax.broadcasted_iota(jnp.int32, shape, axis)


class Op:
    def __init__(self, arr, block, imap, diff=True, acc=None, gshape=None, gimap=None, gdtype=f32):
        self.arr, self.block, self.imap, self.diff, self.acc = arr, tuple(block), imap, diff, acc
        self.gshape = tuple(arr.shape) if gshape is None else tuple(gshape)
        self.gimap = imap if gimap is None else gimap
        self.gdtype = gdtype


class Out:
    def __init__(self, shape, block, imap, dtype=f32):
        self.shape, self.block, self.imap, self.dtype = tuple(shape), tuple(block), imap, dtype


def block_fwd(fn, name, grid, ops, outs):
    n_in = len(ops)

    def body(*refs):
        vals = [r[...] for r in refs[:n_in]]
        res = fn(*vals)
        for r, v in zip(refs[n_in:], res):
            r[...] = v.astype(r.dtype)

    res = pl.pallas_call(
        body, name=name, grid=grid,
        in_specs=[pl.BlockSpec(o.block, o.imap) for o in ops],
        out_specs=[pl.BlockSpec(o.block, o.imap) for o in outs],
        out_shape=[jax.ShapeDtypeStruct(o.shape, o.dtype) for o in outs],
        compiler_params=_cparams(("arbitrary", "arbitrary")),
    )(*[o.arr for o in ops])
    return tuple(res)


def block_bwd(fn, name, grid, ops, outs, douts):
    n_in, n_out = len(ops), len(outs)
    dix = [k for k, o in enumerate(ops) if o.diff]

    def body(*refs):
        vals = [r[...] for r in refs[:n_in]]
        dvals = tuple(r[...] for r in refs[n_in:n_in + n_out])
        grefs = refs[n_in + n_out:]

        def f(*d):
            full = list(vals)
            for k, v in zip(dix, d):
                full[k] = v
            return tuple(fn(*full))

        _, vjp = jax.vjp(f, *[vals[k] for k in dix])
        grads = vjp(dvals)
        j, i = pl.program_id(0), pl.program_id(1)
        for k, gref, g in zip(dix, grefs, grads):
            acc = ops[k].acc
            if acc is None:
                gref[...] = g.astype(gref.dtype)
            else:
                first = (i == 0) if acc == 'i' else jnp.logical_and(i == 0, j == 0)

                @pl.when(first)
                def _():
                    gref[...] = g

                @pl.when(jnp.logical_not(first))
                def _():
                    gref[...] += g

    gspecs = [pl.BlockSpec(ops[k].block, ops[k].gimap) for k in dix]
    gshapes = [jax.ShapeDtypeStruct(ops[k].gshape, ops[k].gdtype) for k in dix]
    res = pl.pallas_call(
        body, name=name, grid=grid,
        in_specs=[pl.BlockSpec(o.block, o.imap) for o in ops] + [pl.BlockSpec(o.block, o.imap) for o in outs],
        out_specs=gspecs, out_shape=gshapes,
        compiler_params=_cparams(("arbitrary", "arbitrary")),
    )(*[o.arr for o in ops], *douts)
    return tuple(res)


def _rows(arr, tm, diff=True, gdtype=f32):
    return Op(arr, (tm, arr.shape[1]), lambda j, i: (i, 0), diff=diff, gdtype=gdtype)


def _param(arr, diff=True):
    return Op(arr, arr.shape, lambda j, i: (0,) * arr.ndim, diff=diff, acc='ij')


def _rows_out(n, c, tm, dtype=f32):
    return Out((n, c), (tm, c), lambda j, i: (i, 0), dtype)


def _cols(arr, t, tc, off=0, width=None, gdtype=f32):
    width = arr.shape[1] if width is None else width
    return Op(arr, (t, tc), lambda j, i: (i, j + off), gshape=(arr.shape[0], width), gimap=lambda j, i: (i, j),
              gdtype=gdtype)


def _cparam(arr, tc):
    return Op(arr, (arr.shape[0], tc), lambda j, i: (0, j), acc='i')


def _colblock(arr, tm, off, width, gdtype=f32):
    return Op(arr, (tm, width), lambda j, i: (i, off // width), gshape=(arr.shape[0], width),
              gimap=lambda j, i: (i, 0), gdtype=gdtype)


def _tile(n, cap):
    best = None
    for t in range(LANE, min(n, cap) + 1, LANE):
        if n % t == 0:
            best = t
    return n if best is None else best


def _rtile(rows, cols, cap_bytes=2 * 1024 * 1024):
    best = None
    for t in range(SUBLANE, rows + 1, SUBLANE):
        if rows % t == 0 and t * cols * 4 <= cap_bytes:
            best = t
    return rows if best is None else best


def mm(a, b, name, ta=False, add=None):
    m = a.shape[1] if ta else a.shape[0]
    kd = a.shape[0] if ta else a.shape[1]
    n = b.shape[1]
    tm, tn = _tile(m, 1408), _tile(n, 1408)
    tk = kd if kd <= 2048 else _tile(kd, 1408)
    nk = kd // tk
    ca = 0 if ta else 1

    def body(*refs):
        if add is None:
            a_ref, b_ref, o_ref, acc = refs
        else:
            a_ref, b_ref, add_ref, o_ref, acc = refs
        k = pl.program_id(2)

        @pl.when(k == 0)
        def _():
            acc[...] = jnp.zeros_like(acc)

        acc[...] += _dg(_b(a_ref[...]), _b(b_ref[...]), ca, 0)

        @pl.when(k == nk - 1)
        def _():
            o_ref[...] = acc[...] if add is None else acc[...] + add_ref[...]

    a_spec = pl.BlockSpec((tk, tm), lambda i, j, k: (k, i)) if ta else pl.BlockSpec((tm, tk), lambda i, j, k: (i, k))
    b_spec = pl.BlockSpec((tk, tn), lambda i, j, k: (k, j))
    o_spec = pl.BlockSpec((tm, tn), lambda i, j, k: (i, j))
    args, specs = [a, b], [a_spec, b_spec]
    if add is not None:
        args.append(add)
        specs.append(o_spec)
    return pl.pallas_call(
        body, name=name, grid=(m // tm, n // tn, nk), in_specs=specs, out_specs=o_spec,
        out_shape=jax.ShapeDtypeStruct((m, n), f32), scratch_shapes=[pltpu.VMEM((tm, tn), f32)],
        compiler_params=_cparams(("parallel", "parallel", "arbitrary")),
    )(*args)


def f_ln(h, y, g, b):
    x = ALPHA * h + y
    mu = jnp.mean(x, axis=-1, keepdims=True)
    xc = x - mu
    var = jnp.mean(xc * xc, axis=-1, keepdims=True)
    return (xc * lax.rsqrt(var + 1e-5) * g + b,)


def f_shift_mix(p, mix):
    return (p + (shift_down(p, 1) - p) * mix,)


def f_rwkv_pre(k, wa_lo, g_lo, w0, w2, a0, a2, g2, k_k, k_a, gh):
    w_lo, a_lo = wa_lo[:, :64], wa_lo[:, 64:]
    log_w = -softplus(-(w0 + bdot(jnp.tanh(w_lo), w2))) - 0.5
    lw = -jnp.exp(log_w)
    a = _sigmoid(a0 + bdot(a_lo, a2))
    g = bdot(_sigmoid(g_lo), g2)
    kk = k * k_k
    kk = kk / jnp.maximum(jnp.sqrt(hdot(kk * kk, gh)), 1e-12)
    k2 = k * (1.0 + (a - 1.0) * k_a)
    return lw, k2, -kk, kk * a, g


def f_rwkv_post(y, r, k2, v, g, ln_g, ln_b, r_k, gh):
    mu = hdot(y, gh) * (1.0 / HEAD_DIM)
    yc = y - mu
    var = hdot(yc * yc, gh) * (1.0 / HEAD_DIM)
    yn = yc * lax.rsqrt(var + RWKV_GN_EPS) * ln_g + ln_b
    bonus = hdot(r * k2 * r_k, gh) * v
    return ((yn + bonus) * g,)


def f_conv4_silu(x, w0, w1, w2, w3, b):
    y = b + shift_down(x, 3) * w0 + shift_down(x, 2) * w1 + shift_down(x, 1) * w2 + x * w3
    return (silu(y),)


def f_ssm_post(y, z, norm_g, gg):
    u = y * silu(z)
    ms = hdot(u * u, gg) * (1.0 / 256.0)
    return (u * lax.rsqrt(ms + 1e-5) * norm_g,)


def f_ffn_act(gate, up, w0, w1, w2, b):
    gc = b + shift_down(gate, 2) * w0 + shift_down(gate, 1) * w1 + gate * w2
    return (silu(gc) * up,)


def _rms(x, g, eps=1e-6):
    return x * lax.rsqrt(jnp.mean(x * x, axis=-1, keepdims=True) + eps) * g


def f_mla_pre(c_q, c_kv, kpe, pos, q_g, w_qn, w_qr, kv_g, w_ukv, inv_q, inv_k):
    qn_in = _rms(c_q, q_g)
    q_nope = bdot(qn_in, w_qn)
    qr = bdot(qn_in, w_qr)
    kv = bdot(_rms(c_kv, kv_g), w_ukv)
    ang_q = pos * inv_q
    ang_k = pos * inv_k
    return (q_nope, qr * jnp.cos(ang_q) + rot_half32(qr) * jnp.sin(ang_q), kv,
            kpe * jnp.cos(ang_k) + rot_half32(kpe) * jnp.sin(ang_k))


def rwkv_chunk(s0, r, lw, k, v, a, b):
    hs = range(len(r))
    l = r[0].shape[0]
    ri, ci = _iota2((l, l), 0), _iota2((l, l), 1)
    strict, incl = ri > ci, ri >= ci
    tri, eye = incl.astype(f32), (ri == ci).astype(f32)
    last = (_iota2((l, 1), 0) == l - 1).astype(f32)
    c = [hdot(tri, lw[h]) for h in hs]
    at = [a[h] * jnp.exp(c[h] - lw[h]) for h in hs]
    wi = [jnp.exp(-c[h]) for h in hs]
    bt = [b[h] * wi[h] for h in hs]
    kt = [k[h] * wi[h] for h in hs]
    rt = [r[h] * jnp.exp(c[h]) for h in hs]
    nab = [jnp.where(strict, mdot_nt(at[h], bt[h]), 0.0) for h in hs]
    nak = [jnp.where(strict, mdot_nt(at[h], kt[h]), 0.0) for h in hs]
    g = [mdot_nt(at[h], s0[h]) + mdot(nak[h], v[h]) for h in hs]
    x = [eye + nab[h] for h in hs]
    p = [mdot(nab[h], nab[h]) for h in hs]
    steps = max(1, (l - 1).bit_length()) - 1
    for it in range(steps):
        x = [x[h] + mdot(p[h], x[h]) for h in hs]
        if it < steps - 1:
            p = [mdot(p[h], p[h]) for h in hs]
    u = [mdot(x[h], g[h]) for h in hs]
    mrb = [jnp.where(incl, mdot_nt(rt[h], bt[h]), 0.0) for h in hs]
    mrk = [jnp.where(incl, mdot_nt(rt[h], kt[h]), 0.0) for h in hs]
    y = [mdot_nt(rt[h], s0[h]) + mdot(mrb[h], u[h]) + mdot(mrk[h], v[h]) for h in hs]
    s1 = [(s0[h] + mdot_tn(u[h], bt[h]) + mdot_tn(v[h], kt[h])) * jnp.exp(jnp.sum(c[h] * last, axis=0, keepdims=True))
          for h in hs]
    return y, s1


def ssd_chunk(xs, bm, cm, dt_raw, s_in, dt_bias, a_log, d_skip, e_heads):
    l = xs.shape[0]
    ri, ci = _iota2((l, l), 0), _iota2((l, l), 1)
    incl = ri >= ci
    tri = incl.astype(f32)
    dt = softplus(dt_raw + dt_bias)
    a128 = dt * (-jnp.exp(a_log))
    lane0 = (_iota2((1, HEAD_DIM), 1) == 0).astype(f32)
    last = (_iota2((l, 1), 0) == l - 1).astype(f32)
    hs = range(N_HEADS)
    group = lambda m, g: m[:, g * SSM_STATE:(g + 1) * SSM_STATE]
    cb = [bdot_nt(group(cm, g), group(bm, g)) for g in range(2)]
    e_all = jnp.concatenate(e_heads, axis=1)
    dt_all = hdot(dt, e_all)
    ac_all = hdot(tri, hdot(a128, e_all))
    xd_all = xs * dt_all
    skip_all = xs * hdot(jnp.broadcast_to(d_skip, (l, LANE)), e_all)
    ac = [ac_all[:, _head(h)] for h in hs]
    xd = [xd_all[:, _head(h)] for h in hs]
    col = [jnp.broadcast_to(jnp.sum(ac[h] * lane0, axis=1, keepdims=True), (l, l)) for h in hs]
    decay = [jnp.exp(jnp.where(incl, col[h] - col[h].T, -1e30)) for h in hs]
    y_diag = [bdot(cb[h // 4] * decay[h], xd[h]) for h in hs]
    a_tot = [jnp.sum(ac[h] * last, axis=0, keepdims=True) for h in hs]
    y_off = [jnp.exp(ac[h]) * bdot(group(cm, h // 4), s_in[h]) for h in hs]
    s_out = [jnp.exp(a_tot[h]) * s_in[h] + bdot_tn(group(bm, h // 4), xd[h] * jnp.exp(a_tot[h] - ac[h])) for h in hs]
    return jnp.concatenate([y_diag[h] + y_off[h] for h in hs], axis=1) + skip_all, s_out


SB_KEYS = LANE
MLA_KEYS = LANE


def sb_tile(q, k, v, run, q0, k0):
    bq, kb = q.shape[0], k.shape[0]
    z = bdot_nt(q, k) * HEAD_DIM ** -0.5
    strict = (k0 + _iota2((bq, kb), 1)) < (q0 + _iota2((bq, kb), 0))
    lk = jnp.where(strict, -softplus_abs(z), 0.0)
    log_att = z + lk + suffix_sums(lk) + run
    att = jnp.where(strict, jnp.exp(jnp.where(strict, log_att, 0.0)), 0.0)
    return bdot(att, v), jnp.sum(lk, axis=1, keepdims=True)


def mla_scores(qn, qp, kn, kp, q0, k0):
    bq, kb = qn.shape[0], kn.shape[0]
    s = (bdot_nt(qn, kn) + bdot_nt(qp, kp)) * (HEAD_DIM + MLA_ROPE) ** -0.5
    causal = (k0 + _iota2((bq, kb), 1)) <= (q0 + _iota2((bq, kb), 0))
    return jnp.where(causal, s, -1e30), causal


def mla_tile_loss(qn, qp, kn, kp, v, do, lse, dsum, q0, k0):
    s, causal = mla_scores(qn, qp, kn, kp, q0, k0)
    p = jnp.where(causal, jnp.exp(s - lse), 0.0)
    return jnp.sum(do * bdot(p, v)) - jnp.sum(dsum * jnp.sum(p, axis=1, keepdims=True))


def _head(h):
    return slice(h * HEAD_DIM, (h + 1) * HEAD_DIM)


def _rwkv_specs(nc, rev):
    hp = RWKV_HEADS_PER_STEP
    w = hp * HEAD_DIM
    chunk = (lambda c: nc - 1 - c) if rev else (lambda c: c)
    tok = lambda off: pl.BlockSpec((RWKV_CHUNK, w), lambda b, g, c: (b * nc + chunk(c), off // w + g))
    st = pl.BlockSpec((1, hp, HEAD_DIM, HEAD_DIM), lambda b, g, c: ((b * (N_HEADS // hp) + g) * nc + chunk(c), 0, 0, 0))
    return tok, st


def _hosted_call(work, name, grid, in_specs, out_specs, out_shape, scratch, args, ride):
    n_in, n_out, n_scr = len(in_specs), len(out_specs), len(scratch)
    k = 0 if ride is None else len(ride.inputs)

    def body(*refs):
        ins, r_in = refs[:n_in], refs[n_in:n_in + k]
        outs, r_out = refs[n_in + k:n_in + k + n_out], refs[n_in + k + n_out:n_in + 2 * k + n_out]
        scr, r_sems = refs[n_in + 2 * k + n_out:n_in + 2 * k + n_out + n_scr], refs[n_in + 2 * k + n_out + n_scr:]
        ids = [pl.program_id(a) for a in range(len(grid))]
        if ride is not None:
            @pl.when(functools.reduce(jnp.logical_and, [i == 0 for i in ids]))
            def _():
                ride.start(r_in, r_out, r_sems)

        work(ins, outs, scr)
        if ride is not None:
            @pl.when(functools.reduce(jnp.logical_and, [i == g - 1 for i, g in zip(ids, grid)]))
            def _():
                ride.finish(r_in, r_out, r_sems)

    res = pl.pallas_call(
        body, name=name, grid=grid, in_specs=list(in_specs) + [ANY] * k, out_specs=list(out_specs) + [ANY] * k,
        out_shape=list(out_shape) + ([] if ride is None else ride.out_shapes),
        scratch_shapes=list(scratch) + ([] if ride is None else ride.scratch),
        compiler_params=_cparams(("arbitrary",) * len(grid)),
    )(*args, *([] if ride is None else ride.inputs))
    return res[:n_out], res[n_out:]


def rwkv_scan_fwd(ps, lw, k2, na, bb, nb, t, ride=None):
    hp, nc = RWKV_HEADS_PER_STEP, t // RWKV_CHUNK
    ng = N_HEADS // hp
    tok, st = _rwkv_specs(nc, False)

    def work(ins, outs, scr):
        r_ref, v_ref, lw_ref, k_ref, a_ref, b_ref = ins
        y_ref, s0_ref = outs
        (s,) = scr

        @pl.when(pl.program_id(2) == 0)
        def _():
            s[...] = jnp.zeros_like(s)

        s0_ref[0] = s[...]
        heads = lambda ref: [ref[:, _head(h)] for h in range(hp)]
        y, s1 = rwkv_chunk([s[h] for h in range(hp)], heads(r_ref), heads(lw_ref), heads(k_ref), heads(v_ref),
                           heads(a_ref), heads(b_ref))
        for h in range(hp):
            y_ref[:, _head(h)] = y[h]
            s[h] = s1[h]

    return _hosted_call(
        work, "rwkv_scan_fwd", (nb, ng, nc), [tok(0), tok(1024), tok(0), tok(0), tok(0), tok(0)], [tok(0), st],
        [jax.ShapeDtypeStruct((nb * t, N_HEADS * HEAD_DIM), f32),
         jax.ShapeDtypeStruct((nb * ng * nc, hp, HEAD_DIM, HEAD_DIM), f32)],
        [pltpu.VMEM((hp, HEAD_DIM, HEAD_DIM), f32)], (ps, ps, lw, k2, na, bb), ride)


def rwkv_scan_bwd(s0, ps, lw, k2, na, bb, dy, nb, t, ride=None):
    hp, nc = RWKV_HEADS_PER_STEP, t // RWKV_CHUNK
    ng = N_HEADS // hp
    tok, st = _rwkv_specs(nc, True)

    def work(ins, outs, scr):
        s0_ref, r_ref, v_ref, lw_ref, k_ref, a_ref, b_ref, dy_ref = ins
        (ds,) = scr

        @pl.when(pl.program_id(2) == 0)
        def _():
            ds[...] = jnp.zeros_like(ds)

        heads = lambda ref: [ref[:, _head(h)] for h in range(hp)]
        _, vjp = jax.vjp(rwkv_chunk, [s0_ref[0, h] for h in range(hp)], heads(r_ref), heads(lw_ref), heads(k_ref),
                         heads(v_ref), heads(a_ref), heads(b_ref))
        g = vjp((heads(dy_ref), [ds[h] for h in range(hp)]))
        for h in range(hp):
            ds[h] = g[0][h]
            for ref, val in zip(outs, g[1:]):
                ref[:, _head(h)] = val[h]

    return _hosted_call(
        work, "rwkv_scan_bwd", (nb, ng, nc), [st, tok(0), tok(1024), tok(0), tok(0), tok(0), tok(0), tok(0)],
        [tok(0)] * 6, [jax.ShapeDtypeStruct((nb * t, N_HEADS * HEAD_DIM), f32)] * 6,
        [pltpu.VMEM((hp, HEAD_DIM, HEAD_DIM), f32)], (s0, ps, ps, lw, k2, na, bb, dy), ride)


def _ssd_specs(nb, nch, rev):
    def row(b, c):
        return b * nch + (nch - 1 - c if rev else c)

    l = SSM_CHUNK
    xs = pl.BlockSpec((l, 512), lambda b, c: (row(b, c), 0))
    bm = pl.BlockSpec((l, 256), lambda b, c: (row(b, c), 2))
    cm = pl.BlockSpec((l, 256), lambda b, c: (row(b, c), 3))
    dt = pl.BlockSpec((l, LANE), lambda b, c: (row(b, c), (L0_PAD - LANE) // LANE))
    st = pl.BlockSpec((1, 1, N_HEADS, SSM_STATE, HEAD_DIM), lambda b, c: (b, (nch - 1 - c if rev else c), 0, 0, 0))
    par = pl.BlockSpec((1, LANE), lambda b, c: (0, 0))
    eh = pl.BlockSpec((N_HEADS, LANE, HEAD_DIM), lambda b, c: (0, 0, 0))
    return xs, bm, cm, dt, st, par, eh, row


def ssd_fwd(xbc_act, proj0, dt_bias, a_log, d_skip, e_heads, nb, t):
    nch = t // SSM_CHUNK
    n_tok = nb * t
    xs, bm, cm, dt, st, par, eh, row = _ssd_specs(nb, nch, False)

    def body(x_ref, b_ref, c_ref, dt_ref, db_ref, al_ref, dsk_ref, e_ref, y_ref, st_ref, s):
        @pl.when(pl.program_id(1) == 0)
        def _():
            s[...] = jnp.zeros_like(s)

        st_ref[0, 0] = s[...]
        y, s_out = ssd_chunk(x_ref[...], b_ref[...], c_ref[...], dt_ref[...], [s[h] for h in range(N_HEADS)],
                             db_ref[...], al_ref[...], dsk_ref[...], [e_ref[h] for h in range(N_HEADS)])
        y_ref[...] = y
        for h in range(N_HEADS):
            s[h] = s_out[h]

    return pl.pallas_call(
        body, name="ssd_fwd", grid=(nb, nch), in_specs=[xs, bm, cm, dt, par, par, par, eh],
        out_specs=[pl.BlockSpec((SSM_CHUNK, 512), lambda b, c: (row(b, c), 0)), st],
        out_shape=[jax.ShapeDtypeStruct((n_tok, 512), f32),
                   jax.ShapeDtypeStruct((nb, nch, N_HEADS, SSM_STATE, HEAD_DIM), f32)],
        scratch_shapes=[pltpu.VMEM((N_HEADS, SSM_STATE, HEAD_DIM), f32)],
        compiler_params=_cparams(("arbitrary", "arbitrary")),
    )(xbc_act, xbc_act, xbc_act, proj0, dt_bias, a_log, d_skip, e_heads)


def ssd_bwd(xbc_act, proj0, dt_bias, a_log, d_skip, e_heads, states, dy, nb, t):
    nch = t // SSM_CHUNK
    n_tok = nb * t
    xs, bm, cm, dt, st, par, eh, row = _ssd_specs(nb, nch, True)

    def body(x_ref, b_ref, c_ref, dt_ref, db_ref, al_ref, dsk_ref, e_ref, st_ref, dy_ref,
             dx_ref, dbm_ref, dcm_ref, ddt_ref, ddb_ref, dal_ref, ddsk_ref, ds):
        first = jnp.logical_and(pl.program_id(0) == 0, pl.program_id(1) == 0)

        @pl.when(pl.program_id(1) == 0)
        def _():
            ds[...] = jnp.zeros_like(ds)

        e_list = [e_ref[h] for h in range(N_HEADS)]

        def f(x, bmv, cmv, dtr, s_in, dbv, alv, dskv):
            return ssd_chunk(x, bmv, cmv, dtr, s_in, dbv, alv, dskv, e_list)

        _, vjp = jax.vjp(f, x_ref[...], b_ref[...], c_ref[...], dt_ref[...],
                         [st_ref[0, 0, h] for h in range(N_HEADS)], db_ref[...], al_ref[...], dsk_ref[...])
        g = vjp((dy_ref[...], [ds[h] for h in range(N_HEADS)]))
        dx_ref[...], dbm_ref[...], dcm_ref[...], ddt_ref[...] = g[0], g[1], g[2], g[3].astype(bf16)
        for h in range(N_HEADS):
            ds[h] = g[4][h]
        for ref, val in zip((ddb_ref, dal_ref, ddsk_ref), g[5:]):
            @pl.when(first)
            def _():
                ref[...] = val

            @pl.when(jnp.logical_not(first))
            def _():
                ref[...] += val

    rows_spec = lambda w: pl.BlockSpec((SSM_CHUNK, w), lambda b, c: (row(b, c), 0))
    return pl.pallas_call(
        body, name="ssd_bwd", grid=(nb, nch),
        in_specs=[xs, bm, cm, dt, par, par, par, eh, st, rows_spec(512)],
        out_specs=[rows_spec(512), rows_spec(256), rows_spec(256), rows_spec(LANE), par, par, par],
        out_shape=[jax.ShapeDtypeStruct((n_tok, 512), f32), jax.ShapeDtypeStruct((n_tok, 256), f32),
                   jax.ShapeDtypeStruct((n_tok, 256), f32), jax.ShapeDtypeStruct((n_tok, LANE), bf16)]
        + [jax.ShapeDtypeStruct((1, LANE), f32)] * 3,
        scratch_shapes=[pltpu.VMEM((N_HEADS, SSM_STATE, HEAD_DIM), f32)],
        compiler_params=_cparams(("arbitrary", "arbitrary")),
    )(xbc_act, xbc_act, xbc_act, proj0, dt_bias, a_log, d_skip, e_heads, states, dy)


ATT_BQ = 256
SB_HEADS_PER_STEP = 4
MLA_HEADS_PER_STEP = 4


def _sb_specs(t, bq, nq):
    w = SB_HEADS_PER_STEP * HEAD_DIM
    qs = lambda off: pl.BlockSpec((bq, w), lambda b, g, i: (b * nq + i, off // w + g))
    ks = lambda off: pl.BlockSpec((t, w), lambda b, g, i: (b, off // w + g))
    return qs, ks


def _sb_mass_spec(bq, nq):
    return pl.BlockSpec((bq, SB_HEADS_PER_STEP * LANE), lambda b, g, i: (b * nq + i, g))


def sb_fwd(proj1, nb, t):
    bq = min(ATT_BQ, t)
    nq = t // bq
    qs, ks = _sb_specs(t, bq, nq)

    def body(q_ref, k_ref, v_ref, o_ref, mass_ref):
        q0 = pl.program_id(2) * bq
        n_tiles = (q0 + bq) // SB_KEYS
        hs = range(SB_HEADS_PER_STEP)
        q = [q_ref[:, _head(h)] for h in hs]
        lanes = _iota2((1, LANE), 1)

        def step(i, carry):
            j = n_tiles - 1 - i
            k0 = pl.multiple_of(j * SB_KEYS, SB_KEYS)
            out = []
            for h in hs:
                o, run, kept = carry[h]
                o_t, mass = sb_tile(q[h], k_ref[pl.ds(k0, SB_KEYS), _head(h)], v_ref[pl.ds(k0, SB_KEYS), _head(h)],
                                    run, q0, k0)
                out.append((o + o_t, run + mass, kept + mass * (lanes == j).astype(f32)))
            return out

        res = lax.fori_loop(0, n_tiles // 2, lambda i, cr: step(2 * i + 1, step(2 * i, cr)),
                            [(jnp.zeros((bq, HEAD_DIM), f32), jnp.zeros((bq, 1), f32), jnp.zeros((bq, LANE), f32))
                             for _ in hs])
        for h in hs:
            o_ref[:, _head(h)] = res[h][0].astype(bf16)
            mass_ref[:, h * LANE:(h + 1) * LANE] = res[h][2]

    return pl.pallas_call(
        body, name="sb_fwd", grid=(nb, N_HEADS // SB_HEADS_PER_STEP, nq), in_specs=[qs(0), ks(512), ks(1024)],
        out_specs=[qs(0), _sb_mass_spec(bq, nq)],
        out_shape=[jax.ShapeDtypeStruct((nb * t, 512), bf16), jax.ShapeDtypeStruct((nb * t, N_HEADS * LANE), f32)],
        compiler_params=_cparams(("parallel", "parallel", "arbitrary")),
    )(proj1, proj1, proj1)


def sb_bwd(proj1, masses, do, nb, t):
    bq = min(ATT_BQ, t)
    nq = t // bq
    qs, ks = _sb_specs(t, bq, nq)

    def body(q_ref, k_ref, v_ref, mass_ref, do_ref, dq_ref, dk_ref, dv_ref):
        @pl.when(pl.program_id(2) == 0)
        def _():
            dk_ref[...] = jnp.zeros_like(dk_ref)
            dv_ref[...] = jnp.zeros_like(dv_ref)

        q0 = pl.program_id(2) * bq
        n_tiles = (q0 + bq) // SB_KEYS
        hs = range(SB_HEADS_PER_STEP)
        q = [q_ref[:, _head(h)] for h in hs]
        do = [do_ref[:, _head(h)].astype(f32) for h in hs]
        col0 = jnp.zeros((bq, 1), f32)
        lanes = _iota2((1, LANE), 1)
        run_all = [hdot(mass_ref[:, h * LANE:(h + 1) * LANE], _upper(LANE)) for h in hs]

        def tile(ref, k0, h):
            return ref[pl.ds(k0, SB_KEYS), _head(h)]

        def grads(j, carry):
            k0 = pl.multiple_of(j * SB_KEYS, SB_KEYS)
            pick = (lanes == j).astype(f32)
            out = []
            for h in hs:
                dq, c = carry[h]
                run_in = jnp.sum(run_all[h] * pick, axis=1, keepdims=True)
                _, vjp = jax.vjp(lambda a, b, d, r: sb_tile(a, b, d, r, q0, k0),
                                 q[h], tile(k_ref, k0, h), tile(v_ref, k0, h), run_in)
                dq_t, dk_t, dv_t, drun = vjp((do[h], c))
                dk_ref[pl.ds(k0, SB_KEYS), _head(h)] += dk_t
                dv_ref[pl.ds(k0, SB_KEYS), _head(h)] += dv_t
                out.append((dq + dq_t, drun + c))
            return out

        res = lax.fori_loop(0, n_tiles // 2, lambda i, cr: grads(2 * i + 1, grads(2 * i, cr)),
                            [(jnp.zeros((bq, HEAD_DIM), f32), col0) for _ in hs])
        for h in hs:
            dq_ref[:, _head(h)] = res[h][0]

    return pl.pallas_call(
        body, name="sb_bwd", grid=(nb, N_HEADS // SB_HEADS_PER_STEP, nq),
        in_specs=[qs(0), ks(512), ks(1024), _sb_mass_spec(bq, nq), qs(0)], out_specs=[qs(0), ks(0), ks(0)],
        out_shape=[jax.ShapeDtypeStruct((nb * t, 512), f32)] * 3,
        compiler_params=_cparams(("parallel", "parallel", "arbitrary")),
    )(proj1, proj1, proj1, masses, do)


def _mla_specs(t, bq, nq):
    hp = MLA_HEADS_PER_STEP
    qn = pl.BlockSpec((bq, hp * HEAD_DIM), lambda b, g, i: (b * nq + i, g))
    qr = pl.BlockSpec((bq, hp * MLA_ROPE), lambda b, g, i: (b * nq + i, g))
    kv = pl.BlockSpec((t, hp * 2 * HEAD_DIM), lambda b, g, i: (b, g))
    kp = pl.BlockSpec((t, LANE), lambda b, g, i: (b, 0))
    return qn, qr, kv, kp


def _mla_softmax_pass(qn, qp, kv_ref, kp_ref, q0, n_tiles, bq):
    hs = range(MLA_HEADS_PER_STEP)

    def step(j, carry):
        k0 = pl.multiple_of(j * MLA_KEYS, MLA_KEYS)
        kp = kp_ref[pl.ds(k0, MLA_KEYS), :MLA_ROPE]
        out = []
        for h in hs:
            m, l, acc = carry[h]
            s, _ = mla_scores(qn[h], qp[h], kv_ref[pl.ds(k0, MLA_KEYS), _head(2 * h)], kp, q0, k0)
            m_new = jnp.maximum(m, jnp.max(s, axis=1, keepdims=True))
            alpha, p = jnp.exp(m - m_new), jnp.exp(s - m_new)
            out.append((m_new, alpha * l + jnp.sum(p, axis=1, keepdims=True),
                        alpha * acc + bdot(p, kv_ref[pl.ds(k0, MLA_KEYS), _head(2 * h + 1)])))
        return out

    init = [(jnp.full((bq, 1), -1e30, f32), jnp.zeros((bq, 1), f32), jnp.zeros((bq, HEAD_DIM), f32)) for _ in hs]
    return lax.fori_loop(0, n_tiles // 2, lambda i, cr: step(2 * i + 1, step(2 * i, cr)), init)


def mla_fwd(q_nope, qr, kv, kpe, nb, t):
    bq = min(ATT_BQ, t)
    nq = t // bq
    sqn, sqr, skv, skp = _mla_specs(t, bq, nq)

    def body(qn_ref, qr_ref, kv_ref, kp_ref, o_ref, o32_ref, lse_ref):
        q0 = pl.program_id(2) * bq
        hs = range(MLA_HEADS_PER_STEP)
        qn = [qn_ref[:, _head(h)] for h in hs]
        qp = [qr_ref[:, h * MLA_ROPE:(h + 1) * MLA_ROPE] for h in hs]
        res = _mla_softmax_pass(qn, qp, kv_ref, kp_ref, q0, (q0 + bq) // MLA_KEYS, bq)
        for h in hs:
            m, l, acc = res[h]
            o = acc / l
            o_ref[:, _head(h)] = o.astype(bf16)
            o32_ref[:, _head(h)] = o
            lse_ref[:, _head(h)] = jnp.broadcast_to(m + jnp.log(l), (bq, HEAD_DIM))

    n = nb * t
    return pl.pallas_call(
        body, name="mla_fwd", grid=(nb, N_HEADS // MLA_HEADS_PER_STEP, nq), in_specs=[sqn, sqr, skv, skp],
        out_specs=[sqn, sqn, sqn],
        out_shape=[jax.ShapeDtypeStruct((n, 512), bf16), jax.ShapeDtypeStruct((n, 512), f32),
                   jax.ShapeDtypeStruct((n, 512), f32)],
        compiler_params=_cparams(("parallel", "arbitrary", "arbitrary")),
    )(q_nope, qr, kv, kpe)


def mla_bwd(q_nope, qr, kv, kpe, o32, lse_b, do, nb, t):
    bq = min(ATT_BQ, t)
    nq = t // bq
    sqn, sqr, skv, skp = _mla_specs(t, bq, nq)

    def body(qn_ref, qr_ref, kv_ref, kp_ref, o_ref, lse_ref, do_ref, dqn_ref, dqr_ref, dkv_ref, dkp_ref):
        first_q = pl.program_id(2) == 0

        @pl.when(first_q)
        def _():
            dkv_ref[...] = jnp.zeros_like(dkv_ref)

        @pl.when(jnp.logical_and(first_q, pl.program_id(1) == 0))
        def _():
            dkp_ref[...] = jnp.zeros_like(dkp_ref)

        q0 = pl.program_id(2) * bq
        n_tiles = (q0 + bq) // MLA_KEYS
        hs = range(MLA_HEADS_PER_STEP)
        qn = [qn_ref[:, _head(h)] for h in hs]
        qp = [qr_ref[:, h * MLA_ROPE:(h + 1) * MLA_ROPE] for h in hs]
        do = [do_ref[:, _head(h)].astype(f32) for h in hs]
        lse = [lse_ref[:, h * HEAD_DIM:h * HEAD_DIM + 1] for h in hs]
        dsum = [jnp.sum(do[h] * o_ref[:, _head(h)], axis=1, keepdims=True) for h in hs]

        def grads(j, carry):
            k0 = pl.multiple_of(j * MLA_KEYS, MLA_KEYS)
            rows = pl.ds(k0, MLA_KEYS)
            kp = kp_ref[rows, :MLA_ROPE]
            out = []
            for h in hs:
                dqn, dqp = carry[h]
                g = jax.grad(mla_tile_loss, argnums=(0, 1, 2, 3, 4))(
                    qn[h], qp[h], kv_ref[rows, _head(2 * h)], kp, kv_ref[rows, _head(2 * h + 1)],
                    do[h], lse[h], dsum[h], q0, k0)
                dkv_ref[rows, _head(2 * h)] += g[2]
                dkp_ref[rows, :MLA_ROPE] += g[3]
                dkv_ref[rows, _head(2 * h + 1)] += g[4]
                out.append((dqn + g[0], dqp + g[1]))
            return out

        res = lax.fori_loop(0, n_tiles // 2, lambda i, cr: grads(2 * i + 1, grads(2 * i, cr)),
                            [(jnp.zeros((bq, HEAD_DIM), f32), jnp.zeros((bq, MLA_ROPE), f32)) for _ in hs])
        for h in hs:
            dqn_ref[:, _head(h)] = res[h][0]
            dqr_ref[:, h * MLA_ROPE:(h + 1) * MLA_ROPE] = res[h][1]

    n = nb * t
    return pl.pallas_call(
        body, name="mla_bwd", grid=(nb, N_HEADS // MLA_HEADS_PER_STEP, nq),
        in_specs=[sqn, sqr, skv, skp, sqn, sqn, sqn], out_specs=[sqn, sqr, skv, skp],
        out_shape=[jax.ShapeDtypeStruct((n, 512), f32), jax.ShapeDtypeStruct((n, N_HEADS * MLA_ROPE), f32),
                   jax.ShapeDtypeStruct((n, 1024), f32), jax.ShapeDtypeStruct((n, LANE), f32)],
        compiler_params=_cparams(("arbitrary", "arbitrary", "arbitrary")),
    )(q_nope, qr, kv, kpe, o32, lse_b, do)


def loss_head(h, target):
    n, d = h.shape
    tm = _tile(n, 512)

    def body(h_ref, t_ref, l_ref, dh_ref):
        diff = h_ref[...] - t_ref[...]
        dh_ref[...] = diff * (1.0 / d)
        part = 0.5 * jnp.sum(jnp.sum(diff * diff, axis=1, keepdims=True) * (1.0 / d), axis=0, keepdims=True)

        @pl.when(pl.program_id(0) == 0)
        def _():
            l_ref[...] = jnp.zeros_like(l_ref)

        l_ref[...] += jnp.broadcast_to(part, l_ref.shape)

    spec = pl.BlockSpec((tm, d), lambda i: (i, 0))
    return pl.pallas_call(
        body, name="loss_head", grid=(n // tm,), in_specs=[spec, spec],
        out_specs=[pl.BlockSpec((8, LANE), lambda i: (0, 0)), spec],
        out_shape=[jax.ShapeDtypeStruct((8, LANE), f32), jax.ShapeDtypeStruct((n, d), f32)],
        compiler_params=_cparams(("arbitrary",)),
    )(h, target)


def _row(v):
    return v.reshape(1, -1)


def _pad_cols(a, n):
    return jnp.pad(a, ((0, 0), (0, n - a.shape[1])))


def _pad_row(v, n=LANE):
    return jnp.pad(v.reshape(1, -1), ((0, 0), (0, n - v.shape[0])))


def _group_matrix(width, group):
    idx = np.arange(width) // group
    return jnp.asarray((idx[:, None] == idx[None, :]).astype(np.float32))


def _head_expand():
    e = np.zeros((N_HEADS, LANE, HEAD_DIM), np.float32)
    for h in range(N_HEADS):
        e[h, h, :] = 1.0
    return jnp.asarray(e)


def _rope_freqs():
    inv = 1.0 / (ROPE_THETA ** (np.arange(0, MLA_ROPE, 2, dtype=np.float32) / MLA_ROPE))
    inv = np.tile(inv.astype(np.float32), 2)
    inv_q = np.tile(inv, N_HEADS).reshape(1, N_HEADS * MLA_ROPE)
    inv_k = np.zeros((1, LANE), np.float32)
    inv_k[0, :MLA_ROPE] = inv
    return jnp.asarray(inv_q), jnp.asarray(inv_k)


def _uq_split(w):
    w3 = w.reshape(w.shape[0], N_HEADS, HEAD_DIM + MLA_ROPE)
    return w3[:, :, :HEAD_DIM].reshape(-1, 512), w3[:, :, HEAD_DIM:].reshape(-1, N_HEADS * MLA_ROPE)


def _uq_merge(gn, gr):
    r = gn.shape[0]
    return jnp.concatenate([gn.reshape(r, N_HEADS, HEAD_DIM), gr.reshape(r, N_HEADS, MLA_ROPE)], axis=2).reshape(r, 768)


EARLY_GRADS = ['ffn1_w_up', 'ffn1_w_down', 'l1_w_in', 'l1_w_out', 'ffn0_w_up', 'ffn0_w_down', 'l0_w_out']


def local_step(x, positions, target, w, late_weights=None, scatter_early=None):
    w = dict(w)
    nb, t, d = x.shape
    n = nb * t
    tm = 256
    ni = n // tm
    tc = 2 * LANE
    h0 = x.reshape(n, d)
    tgt = target.reshape(n, d)
    pos = positions.reshape(n, 1).astype(f32)
    gh = _group_matrix(512, HEAD_DIM)
    gg = _group_matrix(512, 256)
    e_heads = _head_expand()
    inv_q, inv_k = _rope_freqs()
    g = {}

    def ln_stage(h, y, gname, bname):
        ops = [_rows(h, tm), _rows(y, tm, gdtype=bf16), _param(_row(w[gname])), _param(_row(w[bname]))]
        return ops, [_rows_out(n, d, tm)]

    def ln_fwd(name, ops):
        return block_fwd(lambda *a: f_ln(*a) * 2, name, (1, ni), ops, [_rows_out(n, d, tm), _rows_out(n, d, tm, bf16)])

    def ffn_act_stage(u, cw, cb):
        nj = D_FF // tc
        ops = [_cols(u, t, tc, 0, D_FF, bf16), _cols(u, t, tc, nj, D_FF, bf16)] \
            + [_cparam(cw[i:i + 1], tc) for i in range(3)] + [_cparam(_row(cb), tc)]
        return ops, [Out((n, D_FF), (t, tc), lambda j, i: (i, j), bf16)], (nj, nb)

    w_in0 = _pad_cols(w['l0_w_in'], L0_PAD)
    h0b = h0.astype(bf16)
    proj0 = mm(h0b, w_in0, "l0_proj")

    shift_ops = [_cols(proj0, t, tc, 0, RWKV_COLS, bf16), _cparam(_row(w['rwkv_mix']), tc)]
    shift_outs = [Out((n, RWKV_COLS), (t, tc), lambda j, i: (i, j))]
    shift_grid = (RWKV_COLS // tc, nb)
    (ps,) = block_fwd(f_shift_mix, "rwkv_shift", shift_grid, shift_ops, shift_outs)

    pre_ops = [_colblock(ps, tm, 512, 512), _colblock(ps, tm, 1536, 128), _colblock(ps, tm, 1664, 128),
               _param(_row(w['rwkv_w0'])), _param(w['rwkv_w2']), _param(_row(w['rwkv_a0'])), _param(w['rwkv_a2']),
               _param(w['rwkv_g2']), _param(_row(w['rwkv_k_k'])), _param(_row(w['rwkv_k_a'])), _param(gh, diff=False)]
    pre_outs = [_rows_out(n, 512, tm) for _ in range(5)]
    lw, k2, na, bb, gate_r = block_fwd(f_rwkv_pre, "rwkv_pre", (1, ni), pre_ops, pre_outs)
    ride = None if late_weights is None else GatherRide(late_weights[1])
    (y_tok, s0_saved), gathered = rwkv_scan_fwd(ps, lw, k2, na, bb, nb, t, ride)
    if late_weights is not None:
        for name, got in zip(late_weights[0], gathered):
            w[name] = late_weights[2](name, got)
    w_out0 = w['l0_w_out']

    post_ops = [_rows(y_tok, tm), _colblock(ps, tm, 0, 512), _rows(k2, tm), _colblock(ps, tm, 1024, 512),
                _rows(gate_r, tm), _param(_row(w['rwkv_ln_g'])), _param(_row(w['rwkv_ln_b'])),
                _param(w['rwkv_r_k'].reshape(1, 512)), _param(gh, diff=False)]
    post_outs = [_rows_out(n, 512, tm, bf16)]
    (y_a,) = block_fwd(f_rwkv_post, "rwkv_post", (1, ni), post_ops, post_outs)

    xbc_off = (RWKV_COLS + 512) // tc
    conv_ops = [_cols(proj0, t, tc, xbc_off, 1024, bf16)] + [_cparam(w['ssm_conv_w'][i:i + 1], tc) for i in range(4)] \
        + [_cparam(_row(w['ssm_conv_b']), tc)]
    conv_outs = [Out((n, 1024), (t, tc), lambda j, i: (i, j))]
    conv_grid = (1024 // tc, nb)
    (xbc_act,) = block_fwd(f_conv4_silu, "ssm_conv", conv_grid, conv_ops, conv_outs)

    dt_bias, a_log, d_skip = _pad_row(w['ssm_dt_bias']), _pad_row(w['ssm_a_log']), _pad_row(w['ssm_d'])
    y_ssd, ssd_states = ssd_fwd(xbc_act, proj0, dt_bias, a_log, d_skip, e_heads, nb, t)

    z_tok = proj0[:, RWKV_COLS:RWKV_COLS + 512]
    spost_ops = [_rows(y_ssd, tm), _rows(z_tok, tm, gdtype=bf16), _param(_row(w['ssm_norm_g'])), _param(gg, diff=False)]
    spost_outs = [_rows_out(n, 512, tm, bf16)]
    (y_b,) = block_fwd(f_ssm_post, "ssm_post", (1, ni), spost_ops, spost_outs)

    mixed0 = mm(y_b, w_out0[512:], "l0_out_b", add=mm(y_a, w_out0[:512], "l0_out_a"))
    ln1_ops, ln_outs = ln_stage(h0, mixed0, 'l0_ln1_g', 'l0_ln1_b')
    h1, h1b = ln_fwd("l0_ln1", ln1_ops)

    u0 = mm(h1b, w['ffn0_w_up'], "ffn0_up")
    act0_ops, act_outs, act_grid = ffn_act_stage(u0, w['ffn0_conv_w'], w['ffn0_conv_b'])
    (act0,) = block_fwd(f_ffn_act, "ffn0_act", act_grid, act0_ops, act_outs)
    f0 = mm(act0, w['ffn0_w_down'], "ffn0_down")
    ln2_ops, _ = ln_stage(h1, f0, 'l0_ln2_g', 'l0_ln2_b')
    h2, h2b = ln_fwd("l0_ln2", ln2_ops)

    w_in1 = _pad_cols(w['l1_w_in'], L1_PAD)
    w_out1 = w['l1_w_out']
    proj1 = mm(h2b, w_in1, "l1_proj")
    w_qn, w_qr = _uq_split(w['mla_w_uq'])
    mpre_ops = [_colblock(proj1, tm, 1536, 256, bf16), _colblock(proj1, tm, 1792, 128, bf16),
                _colblock(proj1, tm, 1920, 128, bf16),
                Op(pos, (tm, 1), lambda j, i: (i, 0), diff=False),
                _param(_row(w['mla_q_norm_g'])), _param(w_qn), _param(w_qr),
                _param(_row(w['mla_kv_norm_g'])), _param(w['mla_w_ukv']), _param(inv_q, diff=False),
                _param(inv_k, diff=False)]
    mpre_outs = [_rows_out(n, 512, tm), _rows_out(n, N_HEADS * MLA_ROPE, tm), _rows_out(n, 1024, tm),
                 _rows_out(n, LANE, tm)]
    q_nope, q_rope, kv, kpe = block_fwd(f_mla_pre, "mla_pre", (1, ni), mpre_ops, mpre_outs)
    o_sb, sb_masses = sb_fwd(proj1, nb, t)
    o_mla, o_mla32, mla_lse = mla_fwd(q_nope, q_rope, kv, kpe, nb, t)

    mixed1 = mm(o_mla, w_out1[512:], "l1_out_b", add=mm(o_sb, w_out1[:512], "l1_out_a"))
    ln3_ops, _ = ln_stage(h2, mixed1, 'l1_ln1_g', 'l1_ln1_b')
    h3, h3b = ln_fwd("l1_ln1", ln3_ops)
    u1 = mm(h3b, w['ffn1_w_up'], "ffn1_up")
    act1_ops, _, _ = ffn_act_stage(u1, w['ffn1_conv_w'], w['ffn1_conv_b'])
    (act1,) = block_fwd(f_ffn_act, "ffn1_act", act_grid, act1_ops, act_outs)
    f1 = mm(act1, w['ffn1_w_down'], "ffn1_down")
    ln4_ops, _ = ln_stage(h3, f1, 'l1_ln2_g', 'l1_ln2_b')
    (h4,) = block_fwd(f_ln, "l1_ln2", (1, ni), ln4_ops, ln_outs)

    loss_part, dh4 = loss_head(h4, tgt)

    def vec(a_):
        return a_.reshape(-1)

    def ffn_bwd(tag, dh_out, ln_ops, act_ops, h_in, act, w_up, w_down, names):
        dh_res, df, gg_, gb_ = block_bwd(f_ln, tag + "_ln2_bwd", (1, ni), ln_ops, ln_outs, [dh_out])
        g[names[4]], g[names[5]] = vec(gg_), vec(gb_)
        g[names[3]] = mm(act, df, tag + "_down_dw", ta=True)
        dact = mm(df, w_down.T, tag + "_down_dx")
        dgate, dup, dw0, dw1, dw2, dcb = block_bwd(f_ffn_act, tag + "_act_bwd", act_grid, act_ops, act_outs, [dact])
        g[names[1]] = jnp.concatenate([dw0, dw1, dw2], axis=0)
        g[names[2]] = vec(dcb)
        g[names[0]] = jnp.concatenate([mm(h_in, dgate, tag + "_gate_dw", ta=True),
                                       mm(h_in, dup, tag + "_upv_dw", ta=True)], axis=1)
        w_up_t = w_up.T
        dh = mm(dgate, w_up_t[:D_FF], tag + "_gate_dx", add=dh_res)
        return mm(dup, w_up_t[D_FF:], tag + "_upv_dx", add=dh)

    def out_bwd(tag, dmixed, y_first, y_second, w_out, name):
        g[name] = jnp.concatenate([mm(y_first, dmixed, tag + "_a_dw", ta=True),
                                   mm(y_second, dmixed, tag + "_b_dw", ta=True)], axis=0)
        w_t = w_out.T
        return mm(dmixed, w_t[:, :512], tag + "_a_dx"), mm(dmixed, w_t[:, 512:], tag + "_b_dx")

    dh3 = ffn_bwd("ffn1", dh4, ln4_ops, act1_ops, h3b, act1, w['ffn1_w_up'], w['ffn1_w_down'],
                  ['ffn1_w_up', 'ffn1_conv_w', 'ffn1_conv_b', 'ffn1_w_down', 'l1_ln2_g', 'l1_ln2_b'])

    dh2_res, dmixed1, g3g, g3b = block_bwd(f_ln, "l1_ln1_bwd", (1, ni), ln3_ops, ln_outs, [dh3])
    g['l1_ln1_g'], g['l1_ln1_b'] = vec(g3g), vec(g3b)
    do_sb, do_mla = out_bwd("l1_out", dmixed1, o_sb, o_mla, w_out1, 'l1_w_out')

    dq_nope, dq_rope, dkv, dkpe = mla_bwd(q_nope, q_rope, kv, kpe, o_mla32, mla_lse, do_mla, nb, t)
    dsb_q, dsb_k, dsb_v = sb_bwd(proj1, sb_masses, do_sb, nb, t)
    (dc_q, dc_kv, dkpe_raw, gqg, gwqn, gwqr, gkvg, g['mla_w_ukv']) = block_bwd(
        f_mla_pre, "mla_pre_bwd", (1, ni), mpre_ops, mpre_outs, [dq_nope, dq_rope, dkv, dkpe])
    g['mla_q_norm_g'], g['mla_kv_norm_g'] = vec(gqg), vec(gkvg)
    g['mla_w_uq'] = _uq_merge(gwqn, gwqr)
    dproj1 = jnp.concatenate([dsb_q.astype(bf16), dsb_k.astype(bf16), dsb_v.astype(bf16), dc_q, dc_kv, dkpe_raw],
                             axis=1)
    g['l1_w_in'] = mm(h2b, dproj1, "l1_proj_dw", ta=True)[:, :L1_COLS]
    dh2 = mm(dproj1, w_in1.T, "l1_proj_dx", add=dh2_res)

    dh1 = ffn_bwd("ffn0", dh2, ln2_ops, act0_ops, h1b, act0, w['ffn0_w_up'], w['ffn0_w_down'],
                  ['ffn0_w_up', 'ffn0_conv_w', 'ffn0_conv_b', 'ffn0_w_down', 'l0_ln2_g', 'l0_ln2_b'])

    dh0_res, dmixed0, g1g, g1b = block_bwd(f_ln, "l0_ln1_bwd", (1, ni), ln1_ops, ln_outs, [dh1])
    g['l0_ln1_g'], g['l0_ln1_b'] = vec(g1g), vec(g1b)
    dy_a, dy_b = out_bwd("l0_out", dmixed0, y_a, y_b, w_out0, 'l0_w_out')

    dy_ssd, dz, gng = block_bwd(f_ssm_post, "ssm_post_bwd", (1, ni), spost_ops, spost_outs, [dy_b])
    g['ssm_norm_g'] = vec(gng)
    dxs, dbm, dcm, ddt_raw, gdb, gal, gdsk = ssd_bwd(xbc_act, proj0, dt_bias, a_log, d_skip, e_heads, ssd_states,
                                                     dy_ssd, nb, t)
    g['ssm_dt_bias'], g['ssm_a_log'], g['ssm_d'] = gdb[0, :8], gal[0, :8], gdsk[0, :8]
    dxbc_act = jnp.concatenate([dxs, dbm, dcm], axis=1)
    dxbc, cw0, cw1, cw2, cw3, gcb = block_bwd(f_conv4_silu, "ssm_conv_bwd", conv_grid, conv_ops, conv_outs, [dxbc_act])
    g['ssm_conv_w'] = jnp.concatenate([cw0, cw1, cw2, cw3], axis=0)
    g['ssm_conv_b'] = vec(gcb)

    dy_tok, dr_post, dk2_post, dv_post, dgate, glg, glb, grk = block_bwd(
        f_rwkv_post, "rwkv_post_bwd", (1, ni), post_ops, post_outs, [dy_a])
    g['rwkv_ln_g'], g['rwkv_ln_b'], g['rwkv_r_k'] = vec(glg), vec(glb), grk.reshape(N_HEADS, HEAD_DIM)
    ride = None if scatter_early is None else ScatterRide(scatter_early({name: g[name] for name in EARLY_GRADS}))
    (dr, dlw, dk2, dv, dna, dbb), early = rwkv_scan_bwd(s0_saved, ps, lw, k2, na, bb, dy_tok, nb, t, ride)
    (dk_pre, dwa_lo, dg_lo, gw0, g['rwkv_w2'], ga0, g['rwkv_a2'], g['rwkv_g2'], gkk, gka) = block_bwd(
        f_rwkv_pre, "rwkv_pre_bwd", (1, ni), pre_ops, pre_outs, [dlw, dk2 + dk2_post, dna, dbb, dgate])
    g['rwkv_w0'], g['rwkv_a0'], g['rwkv_k_k'], g['rwkv_k_a'] = vec(gw0), vec(ga0), vec(gkk), vec(gka)
    dps = jnp.concatenate([dr + dr_post, dk_pre, dv + dv_post, dwa_lo, dg_lo], axis=1)
    dp_rwkv, gmix = block_bwd(f_shift_mix, "rwkv_shift_bwd", shift_grid, shift_ops, shift_outs, [dps])
    g['rwkv_mix'] = vec(gmix)

    dproj0 = jnp.concatenate([dp_rwkv, dz, dxbc, ddt_raw], axis=1)
    g['l0_w_in'] = mm(h0b, dproj0, "l0_proj_dw", ta=True)[:, :L0_COLS]
    grad_x = mm(dproj0, w_in0.T, "l0_proj_dx", add=dh0_res)
    return loss_part, grad_x.reshape(nb, t, d), g, early


MESH = pl.DeviceIdType.MESH
ANY = pl.BlockSpec(memory_space=pl.ANY)
AXES = ("x", "y", "c")


def _place():
    x, y, c = lax.axis_index("x"), lax.axis_index("y"), lax.axis_index("c")
    chips = [(1 - x, y), (x, 1 - y), (1 - x, 1 - y)]
    return x, y, c, chips


def _dma_sems(n):
    return pltpu.SemaphoreType.DMA((n,))


class GatherRide:
    def __init__(self, shards):
        n = len(shards)
        self.inputs = list(shards)
        self.out_shapes = [jax.ShapeDtypeStruct((N_SHARD,) + a.shape, a.dtype) for a in shards]
        self.scratch = [_dma_sems(3 * n), _dma_sems(3 * n), _dma_sems(3 * n), _dma_sems(3 * n), _dma_sems(n)]

    def _copies(self, ins, outs, sems):
        ici_send, ici_recv, d2d_send, d2d_recv, local_sems = sems
        x, y, c, chips = _place()
        me = 2 * x + y
        pairs = list(enumerate(zip(ins, outs)))

        def over_ici(k, j, slot, to):
            return pltpu.make_async_remote_copy(
                src_ref=ins[k].at[c], dst_ref=outs[k].at[slot, c], send_sem=ici_send.at[3 * k + j],
                recv_sem=ici_recv.at[3 * k + j], device_id=to, device_id_type=MESH)

        def to_sibling(k, j, slot, half):
            return pltpu.make_async_remote_copy(
                src_ref=outs[k].at[slot, half], dst_ref=outs[k].at[slot, half], send_sem=d2d_send.at[3 * k + j],
                recv_sem=d2d_recv.at[3 * k + j], device_id=(x, y, 1 - c), device_id_type=MESH)

        mine = [pltpu.make_async_copy(a, o.at[me], local_sems.at[k]) for k, (a, o) in pairs]
        sends = [over_ici(k, j, me, (cx, cy, c)) for k, _ in pairs for j, (cx, cy) in enumerate(chips)]
        return c, chips, pairs, over_ici, to_sibling, mine, sends

    def start(self, ins, outs, sems):
        _, _, _, _, _, mine, sends = self._copies(ins, outs, sems)
        for cp in mine + sends:
            cp.start()

    def finish(self, ins, outs, sems):
        c, chips, pairs, over_ici, to_sibling, mine, sends = self._copies(ins, outs, sems)
        passed = []
        for k, _ in pairs:
            for j, (cx, cy) in enumerate(chips):
                over_ici(k, j, 2 * cx + cy, (cx, cy, c)).wait_recv()
                passed.append(to_sibling(k, j, 2 * cx + cy, c))
                passed[-1].start()
        for k, _ in pairs:
            for j, (cx, cy) in enumerate(chips):
                to_sibling(k, j, 2 * cx + cy, 1 - c).wait_recv()
        for cp in sends + passed:
            cp.wait_send()
        for cp in mine:
            cp.wait()


class ScatterRide:
    def __init__(self, parts):
        n = len(parts)
        self.inputs = list(parts)
        self.out_shapes = [jax.ShapeDtypeStruct(a.shape, a.dtype) for a in parts]
        self.scratch = [_dma_sems(3 * n), _dma_sems(3 * n), _dma_sems(n)]

    def _copies(self, ins, outs, sems):
        send_sems, recv_sems, local_sems = sems
        x, y, c, chips = _place()
        me = 2 * x + y
        pairs = list(enumerate(zip(ins, outs)))

        def over_ici(k, j, src_slot, dst_slot, to):
            return pltpu.make_async_remote_copy(
                src_ref=ins[k].at[src_slot], dst_ref=outs[k].at[dst_slot], send_sem=send_sems.at[3 * k + j],
                recv_sem=recv_sems.at[3 * k + j], device_id=to, device_id_type=MESH)

        mine = [pltpu.make_async_copy(a.at[me], o.at[me], local_sems.at[k]) for k, (a, o) in pairs]
        sends = [over_ici(k, j, 2 * cx + cy, me, (cx, cy, c)) for k, _ in pairs for j, (cx, cy) in enumerate(chips)]
        arrivals = lambda: [over_ici(k, j, me, 2 * cx + cy, (cx, cy, c))
                            for k, _ in pairs for j, (cx, cy) in enumerate(chips)]
        return mine, sends, arrivals

    def start(self, ins, outs, sems):
        mine, sends, _ = self._copies(ins, outs, sems)
        for cp in mine + sends:
            cp.start()

    def finish(self, ins, outs, sems):
        mine, sends, arrivals = self._copies(ins, outs, sems)
        for cp in arrivals():
            cp.wait_recv()
        for cp in sends:
            cp.wait_send()
        for cp in mine:
            cp.wait()


def _run_ride(ride, name):
    n = len(ride.inputs)

    def body(*refs):
        ins, outs, sems = refs[:n], refs[n:2 * n], refs[2 * n:]
        ride.start(ins, outs, sems)
        ride.finish(ins, outs, sems)

    return pl.pallas_call(body, name=name, in_specs=[ANY] * n, out_specs=[ANY] * n, out_shape=ride.out_shapes,
                          scratch_shapes=ride.scratch)(*ride.inputs)


def gather_shards(shards):
    return _run_ride(GatherRide(shards), "gather_shards")


def swap_halves(pieces, name):
    n = len(pieces)

    def body(*refs):
        ins, outs = refs[:n], refs[n:2 * n]
        send_sems, recv_sems = refs[2 * n:]
        x, y, c, _ = _place()
        cps = [pltpu.make_async_remote_copy(
            src_ref=a.at[:, 1 - c], dst_ref=o, send_sem=send_sems.at[k], recv_sem=recv_sems.at[k],
            device_id=(x, y, 1 - c), device_id_type=MESH) for k, (a, o) in enumerate(zip(ins, outs))]
        for cp in cps:
            cp.start()
        for cp in cps:
            cp.wait()

    return pl.pallas_call(
        body, name=name, in_specs=[ANY] * n, out_specs=[ANY] * n,
        out_shape=[jax.ShapeDtypeStruct((a.shape[0],) + a.shape[2:], a.dtype) for a in pieces],
        scratch_shapes=[_dma_sems(n), _dma_sems(n)],
    )(*pieces)


def scatter_to_chips(parts):
    return _run_ride(ScatterRide(parts), "scatter_to_chips")


def share_halves(bufs):
    n = len(bufs)

    def body(*refs):
        ins, outs = refs[:n], refs[n:2 * n]
        send_sems, recv_sems = refs[2 * n:]
        x, y, c, _ = _place()
        cps = [pltpu.make_async_remote_copy(
            src_ref=a.at[c], dst_ref=o.at[c], send_sem=send_sems.at[k], recv_sem=recv_sems.at[k],
            device_id=(x, y, 1 - c), device_id_type=MESH) for k, (a, o) in enumerate(zip(ins, outs))]
        for cp in cps:
            cp.start()
        for k, (a, o) in enumerate(zip(ins, outs)):
            cps[k].wait_send()
            pltpu.make_async_remote_copy(
                src_ref=a.at[c], dst_ref=o.at[1 - c], send_sem=send_sems.at[k], recv_sem=recv_sems.at[k],
                device_id=(x, y, 1 - c), device_id_type=MESH).wait_recv()

    return pl.pallas_call(
        body, name="share_halves", in_specs=[ANY] * n, out_specs=[ANY] * n,
        out_shape=[jax.ShapeDtypeStruct(a.shape, a.dtype) for a in bufs],
        input_output_aliases={k: k for k in range(n)},
        scratch_shapes=[_dma_sems(n), _dma_sems(n)],
    )(*bufs)


def pair_add(piece, recv, core, name, out_dtype):
    _, _, h, cdim = piece.shape
    tr = _rtile(h, cdim)

    def body(c_ref, a_ref, b_ref, o_ref):
        o_ref[...] = (a_ref[0] + b_ref[...]).astype(o_ref.dtype)

    spec = pl.BlockSpec((1, tr, cdim), lambda p, i, c_ref: (p, i, 0))
    return pl.pallas_call(
        body, name=name,
        grid_spec=pltpu.PrefetchScalarGridSpec(
            num_scalar_prefetch=1, grid=(N_SHARD, h // tr),
            in_specs=[pl.BlockSpec((1, 1, tr, cdim), lambda p, i, c_ref: (p, c_ref[0], i, 0)), spec],
            out_specs=spec),
        out_shape=jax.ShapeDtypeStruct((N_SHARD, h, cdim), out_dtype),
        compiler_params=_cparams(("parallel", "parallel")),
    )(core, piece, recv)


def chip_add(parts, core, name):
    _, h, cdim = parts.shape
    tr = _rtile(h, cdim, 1024 * 1024)

    def body(c_ref, p_ref, o_ref):
        p = [p_ref[s].astype(f32) for s in range(N_SHARD)]
        o_ref[0] = ((p[0] + p[1]) + p[2]) + p[3]

    return pl.pallas_call(
        body, name=name,
        grid_spec=pltpu.PrefetchScalarGridSpec(
            num_scalar_prefetch=1, grid=(h // tr,),
            in_specs=[pl.BlockSpec((N_SHARD, tr, cdim), lambda i, c_ref: (0, i, 0))],
            out_specs=pl.BlockSpec((1, tr, cdim), lambda i, c_ref: (c_ref[0], i, 0))),
        out_shape=jax.ShapeDtypeStruct((2, h, cdim), f32), compiler_params=_cparams(("parallel",)),
    )(core, parts)


def adamw(w, g, m, v, name):
    rows, cdim = w.shape
    tr = _rtile(rows, cdim, 1024 * 1024)

    def body(w_ref, g_ref, m_ref, v_ref, d_ref, nm_ref, nv_ref):
        gv = g_ref[...]
        m_new = ADAM_B1 * m_ref[...] + (1.0 - ADAM_B1) * gv
        v_new = ADAM_B2 * v_ref[...] + (1.0 - ADAM_B2) * jnp.square(gv)
        m_hat = m_new / (1.0 - ADAM_B1 ** ADAM_STEP)
        v_hat = v_new / (1.0 - ADAM_B2 ** ADAM_STEP)
        d_ref[...] = -ADAM_LR * (m_hat / (jnp.sqrt(v_hat) + ADAM_EPS) + ADAM_WD * w_ref[...])
        nm_ref[...] = m_new
        nv_ref[...] = v_new

    spec = pl.BlockSpec((tr, cdim), lambda i: (i, 0))
    return pl.pallas_call(body, name=name, grid=(rows // tr,), in_specs=[spec] * 4, out_specs=[spec] * 3,
                          out_shape=[jax.ShapeDtypeStruct(w.shape, f32)] * 3,
                          compiler_params=_cparams(("parallel",)))(w, g, m, v)


SMALL_MULTIPLE = 16 * LANE


def _pack_flat(parts, multiple=SMALL_MULTIPLE):
    flat = jnp.concatenate([p.reshape(-1) for p in parts])
    pad = (-flat.shape[0]) % multiple
    return jnp.pad(flat, (0, pad)).reshape(-1, LANE)


def _unpack_flat(buf, shapes):
    flat = buf.reshape(-1)
    out, off = [], 0
    for s in shapes:
        cnt = int(np.prod(s))
        out.append(flat[off:off + cnt].reshape(s))
        off += cnt
    return out


def _full_from_shards(name, gathered):
    if name in COL_SHARDED:
        return jnp.concatenate([gathered[s] for s in range(N_SHARD)], axis=1)
    return gathered.reshape(-1, gathered.shape[2])


def _pieces(name, grad):
    if name in COL_SHARDED:
        r, cdim = grad.shape
        return grad.reshape(r, N_SHARD, cdim // N_SHARD).transpose(1, 0, 2)
    return grad.reshape(N_SHARD, grad.shape[0] // N_SHARD, grad.shape[1])


def _small_pieces(name, grad):
    if name in COL_SHARDED or name in ROW_SHARDED:
        return _pieces(name, grad).reshape(N_SHARD, -1)
    return jnp.broadcast_to(grad.reshape(1, -1), (N_SHARD, grad.size))


def kernel(x, positions, l0_w_in, rwkv_mix, rwkv_w0, rwkv_w2, rwkv_a0, rwkv_a2, rwkv_g2, rwkv_k_k, rwkv_k_a, rwkv_r_k, rwkv_ln_g, rwkv_ln_b, ssm_conv_w, ssm_conv_b, ssm_dt_bias, ssm_a_log, ssm_d, ssm_norm_g, l0_w_out, l0_ln1_g, l0_ln1_b, ffn0_w_up, ffn0_conv_w, ffn0_conv_b, ffn0_w_down, l0_ln2_g, l0_ln2_b, l1_w_in, mla_q_norm_g, mla_w_uq, mla_kv_norm_g, mla_w_ukv, l1_w_out, l1_ln1_g, l1_ln1_b, ffn1_w_up, ffn1_conv_w, ffn1_conv_b, ffn1_w_down, l1_ln2_g, l1_ln2_b, loss_target, m_l0_w_in, m_rwkv_mix, m_rwkv_w0, m_rwkv_w2, m_rwkv_a0, m_rwkv_a2, m_rwkv_g2, m_rwkv_k_k, m_rwkv_k_a, m_rwkv_r_k, m_rwkv_ln_g, m_rwkv_ln_b, m_ssm_conv_w, m_ssm_conv_b, m_ssm_dt_bias, m_ssm_a_log, m_ssm_d, m_ssm_norm_g, m_l0_w_out, m_l0_ln1_g, m_l0_ln1_b, m_ffn0_w_up, m_ffn0_conv_w, m_ffn0_conv_b, m_ffn0_w_down, m_l0_ln2_g, m_l0_ln2_b, m_l1_w_in, m_mla_q_norm_g, m_mla_w_uq, m_mla_kv_norm_g, m_mla_w_ukv, m_l1_w_out, m_l1_ln1_g, m_l1_ln1_b, m_ffn1_w_up, m_ffn1_conv_w, m_ffn1_conv_b, m_ffn1_w_down, m_l1_ln2_g, m_l1_ln2_b, v_l0_w_in, v_rwkv_mix, v_rwkv_w0, v_rwkv_w2, v_rwkv_a0, v_rwkv_a2, v_rwkv_g2, v_rwkv_k_k, v_rwkv_k_a, v_rwkv_r_k, v_rwkv_ln_g, v_rwkv_ln_b, v_ssm_conv_w, v_ssm_conv_b, v_ssm_dt_bias, v_ssm_a_log, v_ssm_d, v_ssm_norm_g, v_l0_w_out, v_l0_ln1_g, v_l0_ln1_b, v_ffn0_w_up, v_ffn0_conv_w, v_ffn0_conv_b, v_ffn0_w_down, v_l0_ln2_g, v_l0_ln2_b, v_l1_w_in, v_mla_q_norm_g, v_mla_w_uq, v_mla_kv_norm_g, v_mla_w_ukv, v_l1_w_out, v_l1_ln1_g, v_l1_ln1_b, v_ffn1_w_up, v_ffn1_conv_w, v_ffn1_conv_b, v_ffn1_w_down, v_l1_ln2_g, v_l1_ln2_b):
    args = locals()
    w_loc = {n: args[n] for n in WEIGHTS}
    m_loc = {n: args["m_" + n] for n in WEIGHTS}
    v_loc = {n: args["v_" + n] for n in WEIGHTS}
    core = lax.axis_index("c").astype(jnp.int32).reshape(1)

    small_sharded = [n for n in SMALL if n in COL_SHARDED]
    halves = lambda a: a.reshape(2, a.shape[0] // 2, a.shape[1])
    whole = lambda name, got: _full_from_shards(name, got.reshape(N_SHARD, -1, got.shape[3]))
    first = gather_shards([halves(w_loc['l0_w_in'].astype(bf16)), halves(_pack_flat([w_loc[n] for n in small_sharded]))])
    w_have = {n: w_loc[n] for n in WEIGHTS if n not in BIG}
    w_have['l0_w_in'] = whole('l0_w_in', first[0])
    small_all = first[1].reshape(N_SHARD, -1, LANE)
    per_shard = [_unpack_flat(small_all[s], [w_loc[n].shape for n in small_sharded]) for s in range(N_SHARD)]
    for k, n in enumerate(small_sharded):
        w_have[n] = jnp.concatenate([per_shard[s][k] for s in range(N_SHARD)], axis=1)
    late = [n for n in BIG if n != 'l0_w_in']

    def pair_sums(names, pieces, tag):
        pieces = [p.reshape(N_SHARD, 2, p.shape[1] // 2, p.shape[2]) for p in pieces]
        return [pair_add(p, r, core, "pair_add_" + n, f32 if n == 'small' else bf16)
                for n, p, r in zip(names, pieces, swap_halves(pieces, "swap_halves_" + tag))]

    loss_part, grad_x, g_full, early = local_step(
        x, positions, loss_target, w_have, (late, [halves(w_loc[n].astype(bf16)) for n in late], whole),
        lambda gd: pair_sums(EARLY_GRADS, [_pieces(n, gd[n]) for n in EARLY_GRADS], "early"))
    loss = lax.psum(loss_part[0, 0], AXES)

    small_flat = jnp.concatenate([_small_pieces(n, g_full[n]) for n in SMALL], axis=1)
    pad = (-small_flat.shape[1]) % SMALL_MULTIPLE
    small_pieces = jnp.pad(small_flat, ((0, 0), (0, pad))).reshape(N_SHARD, -1, LANE)
    rest = scatter_to_chips(pair_sums(['l0_w_in', 'small'], [_pieces('l0_w_in', g_full['l0_w_in']), small_pieces],
                                      "rest"))
    from_chips = dict(zip(EARLY_GRADS + ['l0_w_in', 'small'], list(early) + list(rest)))
    units = BIG + ['small']
    both = share_halves([chip_add(from_chips[n], core, "chip_add_" + n) for n in units])
    reduced = [b.reshape(-1, b.shape[2]) for b in both]

    out = {}
    for n, gred in zip(BIG, reduced):
        out[n] = (gred,) + tuple(adamw(w_loc[n], gred, m_loc[n], v_loc[n], "adamw_" + n))
    shapes = [w_loc[n].shape for n in SMALL]
    packs = [_pack_flat([d[n] for n in SMALL]) for d in (w_loc, m_loc, v_loc)]
    small_res = (reduced[-1],) + tuple(adamw(packs[0], reduced[-1], packs[1], packs[2], "adamw_small"))
    small_unpacked = [_unpack_flat(b, shapes) for b in small_res]
    for k, n in enumerate(SMALL):
        out[n] = tuple(u[k] for u in small_unpacked)
    return (loss, grad_x, *[out[n][0] for n in WEIGHTS], *[out[n][1] for n in WEIGHTS],
            *[out[n][2] for n in WEIGHTS], *[out[n][3] for n in WEIGHTS])
```

```python
import functools

import numpy as np
import jax
import jax.numpy as jnp
from jax import lax
from jax.experimental import pallas as pl
from jax.experimental.pallas import tpu as pltpu

f32 = jnp.float32
bf16 = jnp.bfloat16
HI = lax.Precision.HIGHEST
MID = lax.Precision.HIGH

D_MODEL = 1024
HEAD_DIM = 64
N_HEADS = 8
RWKV_COLS = 1792
RWKV_GN_EPS = 64e-5
SSM_STATE = 128
SSM_CHUNK = 128
L0_COLS = 3336
L0_PAD = 3456
L1_COLS = 1952
L1_PAD = 2048
MLA_ROPE = 32
ROPE_THETA = 10000.0
D_FF = 2816
DEPTH = 2
ALPHA = (2 * DEPTH) ** 0.25
ADAM_LR = 0.001
ADAM_B1 = 0.9
ADAM_B2 = 0.999
ADAM_EPS = 1e-08
ADAM_WD = 0.01
ADAM_STEP = 10
RWKV_CHUNK = 64
RWKV_HEADS_PER_STEP = 8
LANE = 128
SUBLANE = 8
VMEM_LIMIT = 56 * 1024 * 1024

WEIGHTS = ['l0_w_in', 'rwkv_mix', 'rwkv_w0', 'rwkv_w2', 'rwkv_a0', 'rwkv_a2', 'rwkv_g2', 'rwkv_k_k', 'rwkv_k_a',
           'rwkv_r_k', 'rwkv_ln_g', 'rwkv_ln_b', 'ssm_conv_w', 'ssm_conv_b', 'ssm_dt_bias', 'ssm_a_log', 'ssm_d',
           'ssm_norm_g', 'l0_w_out', 'l0_ln1_g', 'l0_ln1_b', 'ffn0_w_up', 'ffn0_conv_w', 'ffn0_conv_b',
           'ffn0_w_down', 'l0_ln2_g', 'l0_ln2_b', 'l1_w_in', 'mla_q_norm_g', 'mla_w_uq', 'mla_kv_norm_g',
           'mla_w_ukv', 'l1_w_out', 'l1_ln1_g', 'l1_ln1_b', 'ffn1_w_up', 'ffn1_conv_w', 'ffn1_conv_b',
           'ffn1_w_down', 'l1_ln2_g', 'l1_ln2_b']
COL_SHARDED = ['l0_w_in', 'rwkv_w2', 'rwkv_a2', 'rwkv_g2', 'ssm_conv_w', 'ffn0_w_up', 'ffn0_conv_w', 'l1_w_in',
               'mla_w_uq', 'mla_w_ukv', 'ffn1_w_up', 'ffn1_conv_w']
ROW_SHARDED = ['l0_w_out', 'ffn0_w_down', 'l1_w_out', 'ffn1_w_down']
BIG = ['l0_w_in', 'l0_w_out', 'ffn0_w_up', 'ffn0_w_down', 'l1_w_in', 'l1_w_out', 'ffn1_w_up', 'ffn1_w_down']
SMALL = [n for n in WEIGHTS if n not in BIG]
N_SHARD = 4


def _cparams(sem):
    return pltpu.CompilerParams(dimension_semantics=sem, vmem_limit_bytes=VMEM_LIMIT)


def _dg(a, b, ca, cb, prec=None):
    return lax.dot_general(a, b, (((ca,), (cb,)), ((), ())), precision=prec, preferred_element_type=f32)


def hdot(a, b):
    return _dg(a, b, 1, 0, HI)


def mdot(a, b):
    return _dg(a, b, 1, 0, MID)


def mdot_nt(a, b):
    return _dg(a, b, 1, 1, MID)


def mdot_tn(a, b):
    return _dg(a, b, 0, 0, MID)


def _b(x):
    return x.astype(bf16)


@jax.custom_vjp
def bdot(x, w):
    return _dg(_b(x), _b(w), 1, 0)


def _bdot_fwd(x, w):
    return bdot(x, w), (x, w)


def _bdot_bwd(res, g):
    x, w = res
    return _dg(_b(g), _b(w), 1, 1).astype(x.dtype), _dg(_b(x), _b(g), 0, 0).astype(w.dtype)


bdot.defvjp(_bdot_fwd, _bdot_bwd)


@jax.custom_vjp
def bdot_nt(x, y):
    return _dg(_b(x), _b(y), 1, 1)


def _bdot_nt_fwd(x, y):
    return bdot_nt(x, y), (x, y)


def _bdot_nt_bwd(res, g):
    x, y = res
    return _dg(_b(g), _b(y), 1, 0), _dg(_b(g), _b(x), 0, 0)


bdot_nt.defvjp(_bdot_nt_fwd, _bdot_nt_bwd)


@jax.custom_vjp
def bdot_tn(x, y):
    return _dg(_b(x), _b(y), 0, 0)


def _bdot_tn_fwd(x, y):
    return bdot_tn(x, y), (x, y)


def _bdot_tn_bwd(res, g):
    x, y = res
    return _dg(_b(y), _b(g), 1, 1), _dg(_b(x), _b(g), 1, 0)


bdot_tn.defvjp(_bdot_tn_fwd, _bdot_tn_bwd)


def _sigmoid(x):
    return 1.0 / (1.0 + jnp.exp(-x))


@jax.custom_vjp
def softplus(x):
    e = jnp.exp(-jnp.abs(x))
    u = 1.0 + e
    log1p = jnp.where(u == 1.0, e, jnp.log(u) * e / jnp.where(u == 1.0, 1.0, u - 1.0))
    return jnp.maximum(x, 0.0) + log1p


def _softplus_fwd(x):
    return softplus(x), x


def _softplus_bwd(x, g):
    return (g * _sigmoid(x),)


softplus.defvjp(_softplus_fwd, _softplus_bwd)


@jax.custom_vjp
def softplus_abs(x):
    return jnp.maximum(x, 0.0) + jnp.log(1.0 + jnp.exp(-jnp.abs(x)))


def _softplus_abs_fwd(x):
    return softplus_abs(x), x


softplus_abs.defvjp(_softplus_abs_fwd, _softplus_bwd)


def _two_pass(x, m):
    hi = _b(x)
    lo = _b(x - hi.astype(f32))
    m16 = _b(m)
    return _dg(hi, m16, 1, 0) + _dg(lo, m16, 1, 0)


def _upper(n):
    return (_iota2((n, n), 0) > _iota2((n, n), 1)).astype(f32)


@jax.custom_vjp
def suffix_sums(x):
    return _two_pass(x, _upper(x.shape[1]))


def _suffix_sums_fwd(x):
    return suffix_sums(x), None


def _suffix_sums_bwd(_, g):
    return (_two_pass(g, _upper(g.shape[1]).T),)


suffix_sums.defvjp(_suffix_sums_fwd, _suffix_sums_bwd)


def silu(x):
    return x * _sigmoid(x)


def _shift_rows(x, k, up):
    if k == 0:
        return x
    t = x.shape[0]
    rows = lax.broadcasted_iota(jnp.int32, x.shape, 0)
    if up:
        return jnp.where(rows < t - k, pltpu.roll(x, t - k, 0), 0.0)
    return jnp.where(rows >= k, pltpu.roll(x, k, 0), 0.0)


@functools.partial(jax.custom_vjp, nondiff_argnums=(1,))
def shift_down(x, k):
    return _shift_rows(x, k, False)


def _shift_down_fwd(x, k):
    return _shift_rows(x, k, False), None


def _shift_down_bwd(k, _, g):
    return (_shift_rows(g, k, True),)


shift_down.defvjp(_shift_down_fwd, _shift_down_bwd)


@functools.partial(jax.custom_vjp, nondiff_argnums=(1,))
def lane_roll(x, s):
    return pltpu.roll(x, s % x.shape[1], 1)


def _lane_roll_fwd(x, s):
    return lane_roll(x, s), None


def _lane_roll_bwd(s, _, g):
    return (pltpu.roll(g, (-s) % g.shape[1], 1),)


lane_roll.defvjp(_lane_roll_fwd, _lane_roll_bwd)


def rot_half32(x):
    first = (lax.broadcasted_iota(jnp.int32, x.shape, 1) % MLA_ROPE) < (MLA_ROPE // 2)
    return jnp.where(first, -lane_roll(x, -(MLA_ROPE // 2)), lane_roll(x, MLA_ROPE // 2))


def _iota2(shape, axis):
    return lax.broadcasted_iota(jnp.int32, shape, axis)


class Op:
    def __init__(self, arr, block, imap, diff=True, acc=None, gshape=None, gimap=None, gdtype=f32):
        self.arr, self.block, self.imap, self.diff, self.acc = arr, tuple(block), imap, diff, acc
        self.gshape = tuple(arr.shape) if gshape is None else tuple(gshape)
        self.gimap = imap if gimap is None else gimap
        self.gdtype = gdtype


class Out:
    def __init__(self, shape, block, imap, dtype=f32):
        self.shape, self.block, self.imap, self.dtype = tuple(shape), tuple(block), imap, dtype


def block_fwd(fn, name, grid, ops, outs):
    n_in = len(ops)

    def body(*refs):
        vals = [r[...] for r in refs[:n_in]]
        res = fn(*vals)
        for r, v in zip(refs[n_in:], res):
            r[...] = v.astype(r.dtype)

    res = pl.pallas_call(
        body, name=name, grid=grid,
        in_specs=[pl.BlockSpec(o.block, o.imap) for o in ops],
        out_specs=[pl.BlockSpec(o.block, o.imap) for o in outs],
        out_shape=[jax.ShapeDtypeStruct(o.shape, o.dtype) for o in outs],
        compiler_params=_cparams(("arbitrary", "arbitrary")),
    )(*[o.arr for o in ops])
    return tuple(res)


def block_bwd(fn, name, grid, ops, outs, douts):
    n_in, n_out = len(ops), len(outs)
    dix = [k for k, o in enumerate(ops) if o.diff]

    def body(*refs):
        vals = [r[...] for r in refs[:n_in]]
        dvals = tuple(r[...] for r in refs[n_in:n_in + n_out])
        grefs = refs[n_in + n_out:]

        def f(*d):
            full = list(vals)
            for k, v in zip(dix, d):
                full[k] = v
            return tuple(fn(*full))

        _, vjp = jax.vjp(f, *[vals[k] for k in dix])
        grads = vjp(dvals)
        j, i = pl.program_id(0), pl.program_id(1)
        for k, gref, g in zip(dix, grefs, grads):
            acc = ops[k].acc
            if acc is None:
                gref[...] = g.astype(gref.dtype)
            else:
                first = (i == 0) if acc == 'i' else jnp.logical_and(i == 0, j == 0)

                @pl.when(first)
                def _():
                    gref[...] = g

                @pl.when(jnp.logical_not(first))
                def _():
                    gref[...] += g

    gspecs = [pl.BlockSpec(ops[k].block, ops[k].gimap) for k in dix]
    gshapes = [jax.ShapeDtypeStruct(ops[k].gshape, ops[k].gdtype) for k in dix]
    res = pl.pallas_call(
        body, name=name, grid=grid,
        in_specs=[pl.BlockSpec(o.block, o.imap) for o in ops] + [pl.BlockSpec(o.block, o.imap) for o in outs],
        out_specs=gspecs, out_shape=gshapes,
        compiler_params=_cparams(("arbitrary", "arbitrary")),
    )(*[o.arr for o in ops], *douts)
    return tuple(res)


def _rows(arr, tm, diff=True, gdtype=f32):
    return Op(arr, (tm, arr.shape[1]), lambda j, i: (i, 0), diff=diff, gdtype=gdtype)


def _param(arr, diff=True):
    return Op(arr, arr.shape, lambda j, i: (0,) * arr.ndim, diff=diff, acc='ij')


def _rows_out(n, c, tm, dtype=f32):
    return Out((n, c), (tm, c), lambda j, i: (i, 0), dtype)


def _cols(arr, t, tc, off=0, width=None, gdtype=f32):
    width = arr.shape[1] if width is None else width
    return Op(arr, (t, tc), lambda j, i: (i, j + off), gshape=(arr.shape[0], width), gimap=lambda j, i: (i, j),
              gdtype=gdtype)


def _cparam(arr, tc):
    return Op(arr, (arr.shape[0], tc), lambda j, i: (0, j), acc='i')


def _colblock(arr, tm, off, width, gdtype=f32):
    return Op(arr, (tm, width), lambda j, i: (i, off // width), gshape=(arr.shape[0], width),
              gimap=lambda j, i: (i, 0), gdtype=gdtype)


def _tile(n, cap):
    best = None
    for t in range(LANE, min(n, cap) + 1, LANE):
        if n % t == 0:
            best = t
    return n if best is None else best


def _rtile(rows, cols, cap_bytes=2 * 1024 * 1024):
    best = None
    for t in range(SUBLANE, rows + 1, SUBLANE):
        if rows % t == 0 and t * cols * 4 <= cap_bytes:
            best = t
    return rows if best is None else best


def mm(a, b, name, ta=False, add=None):
    m = a.shape[1] if ta else a.shape[0]
    kd = a.shape[0] if ta else a.shape[1]
    n = b.shape[1]
    tm, tn = _tile(m, 1408), _tile(n, 1408)
    tk = kd if kd <= 2048 else _tile(kd, 1408)
    nk = kd // tk
    ca = 0 if ta else 1

    def body(*refs):
        if add is None:
            a_ref, b_ref, o_ref, acc = refs
        else:
            a_ref, b_ref, add_ref, o_ref, acc = refs
        k = pl.program_id(2)

        @pl.when(k == 0)
        def _():
            acc[...] = jnp.zeros_like(acc)

        acc[...] += _dg(_b(a_ref[...]), _b(b_ref[...]), ca, 0)

        @pl.when(k == nk - 1)
        def _():
            o_ref[...] = acc[...] if add is None else acc[...] + add_ref[...]

    a_spec = pl.BlockSpec((tk, tm), lambda i, j, k: (k, i)) if ta else pl.BlockSpec((tm, tk), lambda i, j, k: (i, k))
    b_spec = pl.BlockSpec((tk, tn), lambda i, j, k: (k, j))
    o_spec = pl.BlockSpec((tm, tn), lambda i, j, k: (i, j))
    args, specs = [a, b], [a_spec, b_spec]
    if add is not None:
        args.append(add)
        specs.append(o_spec)
    return pl.pallas_call(
        body, name=name, grid=(m // tm, n // tn, nk), in_specs=specs, out_specs=o_spec,
        out_shape=jax.ShapeDtypeStruct((m, n), f32), scratch_shapes=[pltpu.VMEM((tm, tn), f32)],
        compiler_params=_cparams(("parallel", "parallel", "arbitrary")),
    )(*args)


def f_ln(h, y, g, b):
    x = ALPHA * h + y
    mu = jnp.mean(x, axis=-1, keepdims=True)
    xc = x - mu
    var = jnp.mean(xc * xc, axis=-1, keepdims=True)
    return (xc * lax.rsqrt(var + 1e-5) * g + b,)


def f_shift_mix(p, mix):
    return (p + (shift_down(p, 1) - p) * mix,)


def f_rwkv_pre(k, wa_lo, g_lo, w0, w2, a0, a2, g2, k_k, k_a, gh):
    w_lo, a_lo = wa_lo[:, :64], wa_lo[:, 64:]
    log_w = -softplus(-(w0 + bdot(jnp.tanh(w_lo), w2))) - 0.5
    lw = -jnp.exp(log_w)
    a = _sigmoid(a0 + bdot(a_lo, a2))
    g = bdot(_sigmoid(g_lo), g2)
    kk = k * k_k
    kk = kk / jnp.maximum(jnp.sqrt(mdot(kk * kk, gh)), 1e-12)
    k2 = k * (1.0 + (a - 1.0) * k_a)
    return lw, k2, -kk, kk * a, g


def f_rwkv_post(y, r, k2, v, g, ln_g, ln_b, r_k, gh):
    mu = mdot(y, gh) * (1.0 / HEAD_DIM)
    yc = y - mu
    var = mdot(yc * yc, gh) * (1.0 / HEAD_DIM)
    yn = yc * lax.rsqrt(var + RWKV_GN_EPS) * ln_g + ln_b
    bonus = mdot(r * k2 * r_k, gh) * v
    return ((yn + bonus) * g,)


def f_conv4_silu(x, w0, w1, w2, w3, b):
    y = b + shift_down(x, 3) * w0 + shift_down(x, 2) * w1 + shift_down(x, 1) * w2 + x * w3
    return (silu(y),)


def f_ssm_post(y, z, norm_g, gg):
    u = y * silu(z)
    ms = mdot(u * u, gg) * (1.0 / 256.0)
    return (u * lax.rsqrt(ms + 1e-5) * norm_g,)


def f_ffn_act(gate, up, w0, w1, w2, b):
    gc = b + shift_down(gate, 2) * w0 + shift_down(gate, 1) * w1 + gate * w2
    return (silu(gc) * up,)


def _rms(x, g, eps=1e-6):
    return x * lax.rsqrt(jnp.mean(x * x, axis=-1, keepdims=True) + eps) * g


def f_mla_pre(c_q, c_kv, kpe, pos, q_g, w_qn, w_qr, kv_g, w_ukv, inv_q, inv_k):
    qn_in = _rms(c_q, q_g)
    q_nope = bdot(qn_in, w_qn)
    qr = bdot(qn_in, w_qr)
    kv = bdot(_rms(c_kv, kv_g), w_ukv)
    ang_q = pos * inv_q
    ang_k = pos * inv_k
    return (q_nope, qr * jnp.cos(ang_q) + rot_half32(qr) * jnp.sin(ang_q), kv,
            kpe * jnp.cos(ang_k) + rot_half32(kpe) * jnp.sin(ang_k))


def rwkv_chunk(s0, r, lw, k, v, a, b):
    hs = range(len(r))
    l = r[0].shape[0]
    ri, ci = _iota2((l, l), 0), _iota2((l, l), 1)
    strict, incl = ri > ci, ri >= ci
    tri, eye = incl.astype(f32), (ri == ci).astype(f32)
    last = (_iota2((l, 1), 0) == l - 1).astype(f32)
    c = [hdot(tri, lw[h]) for h in hs]
    at = [a[h] * jnp.exp(c[h] - lw[h]) for h in hs]
    wi = [jnp.exp(-c[h]) for h in hs]
    bt = [b[h] * wi[h] for h in hs]
    kt = [k[h] * wi[h] for h in hs]
    rt = [r[h] * jnp.exp(c[h]) for h in hs]
    nab = [jnp.where(strict, mdot_nt(at[h], bt[h]), 0.0) for h in hs]
    nak = [jnp.where(strict, mdot_nt(at[h], kt[h]), 0.0) for h in hs]
    g = [bdot_nt(at[h], s0[h]) + bdot(nak[h], v[h]) for h in hs]
    x = [eye + nab[h] for h in hs]
    p = [mdot(nab[h], nab[h]) for h in hs]
    steps = max(1, (l - 1).bit_length()) - 1
    for it in range(steps):
        x = [x[h] + mdot(p[h], x[h]) for h in hs]
        if it < steps - 1:
            p = [mdot(p[h], p[h]) for h in hs]
    u = [mdot(x[h], g[h]) for h in hs]
    mrb = [jnp.where(incl, mdot_nt(rt[h], bt[h]), 0.0) for h in hs]
    mrk = [jnp.where(incl, mdot_nt(rt[h], kt[h]), 0.0) for h in hs]
    y = [bdot_nt(rt[h], s0[h]) + bdot(mrb[h], u[h]) + bdot(mrk[h], v[h]) for h in hs]
    s1 = [(s0[h] + bdot_tn(u[h], bt[h]) + bdot_tn(v[h], kt[h])) * jnp.exp(jnp.sum(c[h] * last, axis=0, keepdims=True))
          for h in hs]
    return y, s1


def ssd_chunk(xs, bm, cm, dt_raw, s_in, dt_bias, a_log, d_skip, e_heads):
    l = xs.shape[0]
    ri, ci = _iota2((l, l), 0), _iota2((l, l), 1)
    incl = ri >= ci
    tri = incl.astype(f32)
    dt = softplus(dt_raw + dt_bias)
    a128 = dt * (-jnp.exp(a_log))
    lane0 = (_iota2((1, HEAD_DIM), 1) == 0).astype(f32)
    last = (_iota2((l, 1), 0) == l - 1).astype(f32)
    hs = range(N_HEADS)
    group = lambda m, g: m[:, g * SSM_STATE:(g + 1) * SSM_STATE]
    cb = [bdot_nt(group(cm, g), group(bm, g)) for g in range(2)]
    e_all = jnp.concatenate(e_heads, axis=1)
    dt_all = hdot(dt, e_all)
    ac_all = hdot(tri, hdot(a128, e_all))
    xd_all = xs * dt_all
    skip_all = xs * hdot(jnp.broadcast_to(d_skip, (l, LANE)), e_all)
    ac = [ac_all[:, _head(h)] for h in hs]
    xd = [xd_all[:, _head(h)] for h in hs]
    col = [jnp.broadcast_to(jnp.sum(ac[h] * lane0, axis=1, keepdims=True), (l, l)) for h in hs]
    decay = [jnp.exp(jnp.where(incl, col[h] - col[h].T, -1e30)) for h in hs]
    y_diag = [bdot(cb[h // 4] * decay[h], xd[h]) for h in hs]
    a_tot = [jnp.sum(ac[h] * last, axis=0, keepdims=True) for h in hs]
    y_off = [jnp.exp(ac[h]) * bdot(group(cm, h // 4), s_in[h]) for h in hs]
    s_out = [jnp.exp(a_tot[h]) * s_in[h] + bdot_tn(group(bm, h // 4), xd[h] * jnp.exp(a_tot[h] - ac[h])) for h in hs]
    return jnp.concatenate([y_diag[h] + y_off[h] for h in hs], axis=1) + skip_all, s_out


SB_KEYS = LANE
MLA_KEYS = 256


def sb_tile(q, k, v, run, q0, k0):
    bq, kb = q.shape[0], k.shape[0]
    z = bdot_nt(q, k) * HEAD_DIM ** -0.5
    strict = (k0 + _iota2((bq, kb), 1)) < (q0 + _iota2((bq, kb), 0))
    lk = jnp.where(strict, -softplus_abs(z), 0.0)
    log_att = z + lk + suffix_sums(lk) + run
    att = jnp.where(strict, jnp.exp(jnp.where(strict, log_att, 0.0)), 0.0)
    return bdot(att, v), jnp.sum(lk, axis=1, keepdims=True)


def mla_scores(qn, qp, kn, kp, q0, k0):
    bq, kb = qn.shape[0], kn.shape[0]
    s = (bdot_nt(qn, kn) + bdot_nt(qp, kp)) * (HEAD_DIM + MLA_ROPE) ** -0.5
    causal = (k0 + _iota2((bq, kb), 1)) <= (q0 + _iota2((bq, kb), 0))
    return jnp.where(causal, s, -1e30), causal


def mla_tile_loss(qn, qp, kn, kp, v, do, lse, dsum, q0, k0):
    s, causal = mla_scores(qn, qp, kn, kp, q0, k0)
    p = jnp.where(causal, jnp.exp(s - lse), 0.0)
    return jnp.sum(do * bdot(p, v)) - jnp.sum(dsum * jnp.sum(p, axis=1, keepdims=True))


def _head(h):
    return slice(h * HEAD_DIM, (h + 1) * HEAD_DIM)


def _rwkv_specs(nc, rev):
    hp = RWKV_HEADS_PER_STEP
    w = hp * HEAD_DIM
    chunk = (lambda c: nc - 1 - c) if rev else (lambda c: c)
    tok = lambda off: pl.BlockSpec((RWKV_CHUNK, w), lambda b, g, c: (b * nc + chunk(c), off // w + g))
    st = pl.BlockSpec((1, hp, HEAD_DIM, HEAD_DIM), lambda b, g, c: ((b * (N_HEADS // hp) + g) * nc + chunk(c), 0, 0, 0))
    return tok, st


def _hosted_call(work, name, grid, in_specs, out_specs, out_shape, scratch, args, ride):
    n_in, n_out, n_scr = len(in_specs), len(out_specs), len(scratch)
    k = 0 if ride is None else len(ride.inputs)

    def body(*refs):
        ins, r_in = refs[:n_in], refs[n_in:n_in + k]
        outs, r_out = refs[n_in + k:n_in + k + n_out], refs[n_in + k + n_out:n_in + 2 * k + n_out]
        scr, r_sems = refs[n_in + 2 * k + n_out:n_in + 2 * k + n_out + n_scr], refs[n_in + 2 * k + n_out + n_scr:]
        ids = [pl.program_id(a) for a in range(len(grid))]
        if ride is not None:
            @pl.when(functools.reduce(jnp.logical_and, [i == 0 for i in ids]))
            def _():
                ride.start(r_in, r_out, r_sems)

        work(ins, outs, scr)
        if ride is not None:
            @pl.when(functools.reduce(jnp.logical_and, [i == g - 1 for i, g in zip(ids, grid)]))
            def _():
                ride.finish(r_in, r_out, r_sems)

    res = pl.pallas_call(
        body, name=name, grid=grid, in_specs=list(in_specs) + [ANY] * k, out_specs=list(out_specs) + [ANY] * k,
        out_shape=list(out_shape) + ([] if ride is None else ride.out_shapes),
        scratch_shapes=list(scratch) + ([] if ride is None else ride.scratch),
        compiler_params=_cparams(("arbitrary",) * len(grid)),
    )(*args, *([] if ride is None else ride.inputs))
    return res[:n_out], res[n_out:]


def rwkv_scan_fwd(ps, lw, k2, na, bb, nb, t, ride=None):
    hp, nc = RWKV_HEADS_PER_STEP, t // RWKV_CHUNK
    ng = N_HEADS // hp
    tok, st = _rwkv_specs(nc, False)

    def work(ins, outs, scr):
        r_ref, v_ref, lw_ref, k_ref, a_ref, b_ref = ins
        y_ref, s0_ref = outs
        (s,) = scr

        @pl.when(pl.program_id(2) == 0)
        def _():
            s[...] = jnp.zeros_like(s)

        s0_ref[0] = s[...]
        heads = lambda ref: [ref[:, _head(h)] for h in range(hp)]
        y, s1 = rwkv_chunk([s[h] for h in range(hp)], heads(r_ref), heads(lw_ref), heads(k_ref), heads(v_ref),
                           heads(a_ref), heads(b_ref))
        for h in range(hp):
            y_ref[:, _head(h)] = y[h]
            s[h] = s1[h]

    return _hosted_call(
        work, "rwkv_scan_fwd", (nb, ng, nc), [tok(0), tok(1024), tok(0), tok(0), tok(0), tok(0)], [tok(0), st],
        [jax.ShapeDtypeStruct((nb * t, N_HEADS * HEAD_DIM), f32),
         jax.ShapeDtypeStruct((nb * ng * nc, hp, HEAD_DIM, HEAD_DIM), f32)],
        [pltpu.VMEM((hp, HEAD_DIM, HEAD_DIM), f32)], (ps, ps, lw, k2, na, bb), ride)


def rwkv_scan_bwd(s0, ps, lw, k2, na, bb, dy, nb, t, ride=None):
    hp, nc = RWKV_HEADS_PER_STEP, t // RWKV_CHUNK
    ng = N_HEADS // hp
    tok, st = _rwkv_specs(nc, True)

    def work(ins, outs, scr):
        s0_ref, r_ref, v_ref, lw_ref, k_ref, a_ref, b_ref, dy_ref = ins
        (ds,) = scr

        @pl.when(pl.program_id(2) == 0)
        def _():
            ds[...] = jnp.zeros_like(ds)

        heads = lambda ref: [ref[:, _head(h)] for h in range(hp)]
        _, vjp = jax.vjp(rwkv_chunk, [s0_ref[0, h] for h in range(hp)], heads(r_ref), heads(lw_ref), heads(k_ref),
                         heads(v_ref), heads(a_ref), heads(b_ref))
        g = vjp((heads(dy_ref), [ds[h] for h in range(hp)]))
        for h in range(hp):
            ds[h] = g[0][h]
            for ref, val in zip(outs, g[1:]):
                ref[:, _head(h)] = val[h]

    return _hosted_call(
        work, "rwkv_scan_bwd", (nb, ng, nc), [st, tok(0), tok(1024), tok(0), tok(0), tok(0), tok(0), tok(0)],
        [tok(0)] * 6, [jax.ShapeDtypeStruct((nb * t, N_HEADS * HEAD_DIM), f32)] * 6,
        [pltpu.VMEM((hp, HEAD_DIM, HEAD_DIM), f32)], (s0, ps, ps, lw, k2, na, bb, dy), ride)


def _ssd_specs(nb, nch, rev):
    def row(b, c):
        return b * nch + (nch - 1 - c if rev else c)

    l = SSM_CHUNK
    xs = pl.BlockSpec((l, 512), lambda b, c: (row(b, c), 0))
    bm = pl.BlockSpec((l, 256), lambda b, c: (row(b, c), 2))
    cm = pl.BlockSpec((l, 256), lambda b, c: (row(b, c), 3))
    dt = pl.BlockSpec((l, LANE), lambda b, c: (row(b, c), (L0_PAD - LANE) // LANE))
    st = pl.BlockSpec((1, 1, N_HEADS, SSM_STATE, HEAD_DIM), lambda b, c: (b, (nch - 1 - c if rev else c), 0, 0, 0))
    par = pl.BlockSpec((1, LANE), lambda b, c: (0, 0))
    eh = pl.BlockSpec((N_HEADS, LANE, HEAD_DIM), lambda b, c: (0, 0, 0))
    return xs, bm, cm, dt, st, par, eh, row


def ssd_fwd(xbc_act, proj0, dt_bias, a_log, d_skip, e_heads, nb, t):
    nch = t // SSM_CHUNK
    n_tok = nb * t
    xs, bm, cm, dt, st, par, eh, row = _ssd_specs(nb, nch, False)

    def body(x_ref, b_ref, c_ref, dt_ref, db_ref, al_ref, dsk_ref, e_ref, y_ref, st_ref, s):
        @pl.when(pl.program_id(1) == 0)
        def _():
            s[...] = jnp.zeros_like(s)

        st_ref[0, 0] = s[...]
        y, s_out = ssd_chunk(x_ref[...], b_ref[...], c_ref[...], dt_ref[...], [s[h] for h in range(N_HEADS)],
                             db_ref[...], al_ref[...], dsk_ref[...], [e_ref[h] for h in range(N_HEADS)])
        y_ref[...] = y
        for h in range(N_HEADS):
            s[h] = s_out[h]

    return pl.pallas_call(
        body, name="ssd_fwd", grid=(nb, nch), in_specs=[xs, bm, cm, dt, par, par, par, eh],
        out_specs=[pl.BlockSpec((SSM_CHUNK, 512), lambda b, c: (row(b, c), 0)), st],
        out_shape=[jax.ShapeDtypeStruct((n_tok, 512), f32),
                   jax.ShapeDtypeStruct((nb, nch, N_HEADS, SSM_STATE, HEAD_DIM), f32)],
        scratch_shapes=[pltpu.VMEM((N_HEADS, SSM_STATE, HEAD_DIM), f32)],
        compiler_params=_cparams(("arbitrary", "arbitrary")),
    )(xbc_act, xbc_act, xbc_act, proj0, dt_bias, a_log, d_skip, e_heads)


def ssd_bwd(xbc_act, proj0, dt_bias, a_log, d_skip, e_heads, states, dy, nb, t):
    nch = t // SSM_CHUNK
    n_tok = nb * t
    xs, bm, cm, dt, st, par, eh, row = _ssd_specs(nb, nch, True)

    def body(x_ref, b_ref, c_ref, dt_ref, db_ref, al_ref, dsk_ref, e_ref, st_ref, dy_ref,
             dx_ref, dbm_ref, dcm_ref, ddt_ref, ddb_ref, dal_ref, ddsk_ref, ds):
        first = jnp.logical_and(pl.program_id(0) == 0, pl.program_id(1) == 0)

        @pl.when(pl.program_id(1) == 0)
        def _():
            ds[...] = jnp.zeros_like(ds)

        e_list = [e_ref[h] for h in range(N_HEADS)]

        def f(x, bmv, cmv, dtr, s_in, dbv, alv, dskv):
            return ssd_chunk(x, bmv, cmv, dtr, s_in, dbv, alv, dskv, e_list)

        _, vjp = jax.vjp(f, x_ref[...], b_ref[...], c_ref[...], dt_ref[...],
                         [st_ref[0, 0, h] for h in range(N_HEADS)], db_ref[...], al_ref[...], dsk_ref[...])
        g = vjp((dy_ref[...], [ds[h] for h in range(N_HEADS)]))
        dx_ref[...], dbm_ref[...], dcm_ref[...], ddt_ref[...] = g[0], g[1], g[2], g[3].astype(bf16)
        for h in range(N_HEADS):
            ds[h] = g[4][h]
        for ref, val in zip((ddb_ref, dal_ref, ddsk_ref), g[5:]):
            @pl.when(first)
            def _():
                ref[...] = val

            @pl.when(jnp.logical_not(first))
            def _():
                ref[...] += val

    rows_spec = lambda w: pl.BlockSpec((SSM_CHUNK, w), lambda b, c: (row(b, c), 0))
    return pl.pallas_call(
        body, name="ssd_bwd", grid=(nb, nch),
        in_specs=[xs, bm, cm, dt, par, par, par, eh, st, rows_spec(512)],
        out_specs=[rows_spec(512), rows_spec(256), rows_spec(256), rows_spec(LANE), par, par, par],
        out_shape=[jax.ShapeDtypeStruct((n_tok, 512), f32), jax.ShapeDtypeStruct((n_tok, 256), f32),
                   jax.ShapeDtypeStruct((n_tok, 256), f32), jax.ShapeDtypeStruct((n_tok, LANE), bf16)]
        + [jax.ShapeDtypeStruct((1, LANE), f32)] * 3,
        scratch_shapes=[pltpu.VMEM((N_HEADS, SSM_STATE, HEAD_DIM), f32)],
        compiler_params=_cparams(("arbitrary", "arbitrary")),
    )(xbc_act, xbc_act, xbc_act, proj0, dt_bias, a_log, d_skip, e_heads, states, dy)


ATT_BQ = 256
SB_HEADS_PER_STEP = 4
MLA_HEADS_PER_STEP = 4


def _sb_specs(t, bq, nq):
    w = SB_HEADS_PER_STEP * HEAD_DIM
    qs = lambda off: pl.BlockSpec((bq, w), lambda b, g, i: (b * nq + i, off // w + g))
    ks = lambda off: pl.BlockSpec((t, w), lambda b, g, i: (b, off // w + g))
    return qs, ks


def _sb_mass_spec(bq, nq):
    return pl.BlockSpec((bq, SB_HEADS_PER_STEP * LANE), lambda b, g, i: (b * nq + i, g))


def sb_fwd(proj1, nb, t, ride=None):
    bq = min(ATT_BQ, t)
    nq = t // bq
    qs, ks = _sb_specs(t, bq, nq)

    def work(ins, outs, _):
        q_ref, k_ref, v_ref = ins
        o_ref, mass_ref = outs
        q0 = pl.program_id(2) * bq
        n_tiles = (q0 + bq) // SB_KEYS
        hs = range(SB_HEADS_PER_STEP)
        q = [q_ref[:, _head(h)] for h in hs]
        lanes = _iota2((1, LANE), 1)

        def step(i, carry):
            j = n_tiles - 1 - i
            k0 = pl.multiple_of(j * SB_KEYS, SB_KEYS)
            out = []
            for h in hs:
                o, run, kept = carry[h]
                o_t, mass = sb_tile(q[h], k_ref[pl.ds(k0, SB_KEYS), _head(h)], v_ref[pl.ds(k0, SB_KEYS), _head(h)],
                                    run, q0, k0)
                out.append((o + o_t, run + mass, kept + mass * (lanes == j).astype(f32)))
            return out

        res = lax.fori_loop(0, n_tiles // 2, lambda i, cr: step(2 * i + 1, step(2 * i, cr)),
                            [(jnp.zeros((bq, HEAD_DIM), f32), jnp.zeros((bq, 1), f32), jnp.zeros((bq, LANE), f32))
                             for _ in hs])
        for h in hs:
            o_ref[:, _head(h)] = res[h][0].astype(bf16)
            mass_ref[:, h * LANE:(h + 1) * LANE] = res[h][2]

    return _hosted_call(
        work, "sb_fwd", (nb, N_HEADS // SB_HEADS_PER_STEP, nq), [qs(0), ks(512), ks(1024)],
        [qs(0), _sb_mass_spec(bq, nq)],
        [jax.ShapeDtypeStruct((nb * t, 512), bf16), jax.ShapeDtypeStruct((nb * t, N_HEADS * LANE), f32)],
        [], (proj1, proj1, proj1), ride)


def sb_bwd(proj1, masses, do, nb, t):
    bq = min(ATT_BQ, t)
    nq = t // bq
    qs, ks = _sb_specs(t, bq, nq)

    def body(q_ref, k_ref, v_ref, mass_ref, do_ref, dq_ref, dk_ref, dv_ref):
        @pl.when(pl.program_id(2) == 0)
        def _():
            dk_ref[...] = jnp.zeros_like(dk_ref)
            dv_ref[...] = jnp.zeros_like(dv_ref)

        q0 = pl.program_id(2) * bq
        n_tiles = (q0 + bq) // SB_KEYS
        hs = range(SB_HEADS_PER_STEP)
        q = [q_ref[:, _head(h)] for h in hs]
        do = [do_ref[:, _head(h)].astype(f32) for h in hs]
        col0 = jnp.zeros((bq, 1), f32)
        lanes = _iota2((1, LANE), 1)
        run_all = [hdot(mass_ref[:, h * LANE:(h + 1) * LANE], _upper(LANE)) for h in hs]

        def tile(ref, k0, h):
            return ref[pl.ds(k0, SB_KEYS), _head(h)]

        def grads(j, carry):
            k0 = pl.multiple_of(j * SB_KEYS, SB_KEYS)
            pick = (lanes == j).astype(f32)
            out = []
            for h in hs:
                dq, c = carry[h]
                run_in = jnp.sum(run_all[h] * pick, axis=1, keepdims=True)
                _, vjp = jax.vjp(lambda a, b, d, r: sb_tile(a, b, d, r, q0, k0),
                                 q[h], tile(k_ref, k0, h), tile(v_ref, k0, h), run_in)
                dq_t, dk_t, dv_t, drun = vjp((do[h], c))
                dk_ref[pl.ds(k0, SB_KEYS), _head(h)] += dk_t
                dv_ref[pl.ds(k0, SB_KEYS), _head(h)] += dv_t
                out.append((dq + dq_t, drun + c))
            return out

        res = lax.fori_loop(0, n_tiles // 2, lambda i, cr: grads(2 * i + 1, grads(2 * i, cr)),
                            [(jnp.zeros((bq, HEAD_DIM), f32), col0) for _ in hs])
        for h in hs:
            dq_ref[:, _head(h)] = res[h][0]

    return pl.pallas_call(
        body, name="sb_bwd", grid=(nb, N_HEADS // SB_HEADS_PER_STEP, nq),
        in_specs=[qs(0), ks(512), ks(1024), _sb_mass_spec(bq, nq), qs(0)], out_specs=[qs(0), ks(0), ks(0)],
        out_shape=[jax.ShapeDtypeStruct((nb * t, 512), f32)] * 3,
        compiler_params=_cparams(("parallel", "parallel", "arbitrary")),
    )(proj1, proj1, proj1, masses, do)


def _mla_specs(t, bq, nq):
    hp = MLA_HEADS_PER_STEP
    qn = pl.BlockSpec((bq, hp * HEAD_DIM), lambda b, g, i: (b * nq + i, g))
    qr = pl.BlockSpec((bq, hp * MLA_ROPE), lambda b, g, i: (b * nq + i, g))
    kv = pl.BlockSpec((t, hp * 2 * HEAD_DIM), lambda b, g, i: (b, g))
    kp = pl.BlockSpec((t, LANE), lambda b, g, i: (b, 0))
    return qn, qr, kv, kp


def _mla_softmax_pass(qn, qp, kv_ref, kp_ref, q0, n_tiles, bq):
    hs = range(MLA_HEADS_PER_STEP)

    def step(j, carry):
        k0 = pl.multiple_of(j * MLA_KEYS, MLA_KEYS)
        kp = kp_ref[pl.ds(k0, MLA_KEYS), :MLA_ROPE]
        out = []
        for h in hs:
            m, l, acc = carry[h]
            s, _ = mla_scores(qn[h], qp[h], kv_ref[pl.ds(k0, MLA_KEYS), _head(2 * h)], kp, q0, k0)
            m_new = jnp.maximum(m, jnp.max(s, axis=1, keepdims=True))
            alpha, p = jnp.exp(m - m_new), jnp.exp(s - m_new)
            out.append((m_new, alpha * l + jnp.sum(p, axis=1, keepdims=True),
                        alpha * acc + bdot(p, kv_ref[pl.ds(k0, MLA_KEYS), _head(2 * h + 1)])))
        return out

    init = [(jnp.full((bq, 1), -1e30, f32), jnp.zeros((bq, 1), f32), jnp.zeros((bq, HEAD_DIM), f32)) for _ in hs]
    return lax.fori_loop(0, n_tiles, step, init)


def mla_fwd(q_nope, qr, kv, kpe, nb, t):
    bq = min(ATT_BQ, t)
    nq = t // bq
    sqn, sqr, skv, skp = _mla_specs(t, bq, nq)

    def body(qn_ref, qr_ref, kv_ref, kp_ref, o_ref, o32_ref, lse_ref):
        q0 = pl.program_id(2) * bq
        hs = range(MLA_HEADS_PER_STEP)
        qn = [qn_ref[:, _head(h)] for h in hs]
        qp = [qr_ref[:, h * MLA_ROPE:(h + 1) * MLA_ROPE] for h in hs]
        res = _mla_softmax_pass(qn, qp, kv_ref, kp_ref, q0, (q0 + bq) // MLA_KEYS, bq)
        for h in hs:
            m, l, acc = res[h]
            o = acc / l
            o_ref[:, _head(h)] = o.astype(bf16)
            o32_ref[:, _head(h)] = o
            lse_ref[:, _head(h)] = jnp.broadcast_to(m + jnp.log(l), (bq, HEAD_DIM))

    n = nb * t
    return pl.pallas_call(
        body, name="mla_fwd", grid=(nb, N_HEADS // MLA_HEADS_PER_STEP, nq), in_specs=[sqn, sqr, skv, skp],
        out_specs=[sqn, sqn, sqn],
        out_shape=[jax.ShapeDtypeStruct((n, 512), bf16), jax.ShapeDtypeStruct((n, 512), f32),
                   jax.ShapeDtypeStruct((n, 512), f32)],
        compiler_params=_cparams(("parallel", "arbitrary", "arbitrary")),
    )(q_nope, qr, kv, kpe)


def mla_bwd(q_nope, qr, kv, kpe, o32, lse_b, do, nb, t):
    bq = min(ATT_BQ, t)
    nq = t // bq
    sqn, sqr, skv, skp = _mla_specs(t, bq, nq)

    def body(qn_ref, qr_ref, kv_ref, kp_ref, o_ref, lse_ref, do_ref, dqn_ref, dqr_ref, dkv_ref, dkp_ref):
        first_q = pl.program_id(2) == 0

        @pl.when(first_q)
        def _():
            dkv_ref[...] = jnp.zeros_like(dkv_ref)

        @pl.when(jnp.logical_and(first_q, pl.program_id(1) == 0))
        def _():
            dkp_ref[...] = jnp.zeros_like(dkp_ref)

        q0 = pl.program_id(2) * bq
        n_tiles = (q0 + bq) // MLA_KEYS
        hs = range(MLA_HEADS_PER_STEP)
        qn = [qn_ref[:, _head(h)] for h in hs]
        qp = [qr_ref[:, h * MLA_ROPE:(h + 1) * MLA_ROPE] for h in hs]
        do = [do_ref[:, _head(h)].astype(f32) for h in hs]
        lse = [lse_ref[:, h * HEAD_DIM:h * HEAD_DIM + 1] for h in hs]
        dsum = [jnp.sum(do[h] * o_ref[:, _head(h)], axis=1, keepdims=True) for h in hs]

        def grads(j, carry):
            k0 = pl.multiple_of(j * MLA_KEYS, MLA_KEYS)
            rows = pl.ds(k0, MLA_KEYS)
            kp = kp_ref[rows, :MLA_ROPE]
            out = []
            for h in hs:
                dqn, dqp = carry[h]
                g = jax.grad(mla_tile_loss, argnums=(0, 1, 2, 3, 4))(
                    qn[h], qp[h], kv_ref[rows, _head(2 * h)], kp, kv_ref[rows, _head(2 * h + 1)],
                    do[h], lse[h], dsum[h], q0, k0)
                dkv_ref[rows, _head(2 * h)] += g[2]
                dkp_ref[rows, :MLA_ROPE] += g[3]
                dkv_ref[rows, _head(2 * h + 1)] += g[4]
                out.append((dqn + g[0], dqp + g[1]))
            return out

        res = lax.fori_loop(0, n_tiles, grads,
                            [(jnp.zeros((bq, HEAD_DIM), f32), jnp.zeros((bq, MLA_ROPE), f32)) for _ in hs])
        for h in hs:
            dqn_ref[:, _head(h)] = res[h][0]
            dqr_ref[:, h * MLA_ROPE:(h + 1) * MLA_ROPE] = res[h][1]

    n = nb * t
    return pl.pallas_call(
        body, name="mla_bwd", grid=(nb, N_HEADS // MLA_HEADS_PER_STEP, nq),
        in_specs=[sqn, sqr, skv, skp, sqn, sqn, sqn], out_specs=[sqn, sqr, skv, skp],
        out_shape=[jax.ShapeDtypeStruct((n, 512), f32), jax.ShapeDtypeStruct((n, N_HEADS * MLA_ROPE), f32),
                   jax.ShapeDtypeStruct((n, 1024), f32), jax.ShapeDtypeStruct((n, LANE), f32)],
        compiler_params=_cparams(("arbitrary", "arbitrary", "arbitrary")),
    )(q_nope, qr, kv, kpe, o32, lse_b, do)


def loss_head(h, target):
    n, d = h.shape
    tm = _tile(n, 512)

    def body(h_ref, t_ref, l_ref, dh_ref):
        diff = h_ref[...] - t_ref[...]
        dh_ref[...] = diff * (1.0 / d)
        part = 0.5 * jnp.sum(jnp.sum(diff * diff, axis=1, keepdims=True) * (1.0 / d), axis=0, keepdims=True)

        @pl.when(pl.program_id(0) == 0)
        def _():
            l_ref[...] = jnp.zeros_like(l_ref)

        l_ref[...] += jnp.broadcast_to(part, l_ref.shape)

    spec = pl.BlockSpec((tm, d), lambda i: (i, 0))
    return pl.pallas_call(
        body, name="loss_head", grid=(n // tm,), in_specs=[spec, spec],
        out_specs=[pl.BlockSpec((8, LANE), lambda i: (0, 0)), spec],
        out_shape=[jax.ShapeDtypeStruct((8, LANE), f32), jax.ShapeDtypeStruct((n, d), f32)],
        compiler_params=_cparams(("arbitrary",)),
    )(h, target)


def _row(v):
    return v.reshape(1, -1)


def _pad_cols(a, n):
    return jnp.pad(a, ((0, 0), (0, n - a.shape[1])))


def _pad_row(v, n=LANE):
    return jnp.pad(v.reshape(1, -1), ((0, 0), (0, n - v.shape[0])))


def _group_matrix(width, group):
    idx = np.arange(width) // group
    return jnp.asarray((idx[:, None] == idx[None, :]).astype(np.float32))


def _head_expand():
    e = np.zeros((N_HEADS, LANE, HEAD_DIM), np.float32)
    for h in range(N_HEADS):
        e[h, h, :] = 1.0
    return jnp.asarray(e)


def _rope_freqs():
    inv = 1.0 / (ROPE_THETA ** (np.arange(0, MLA_ROPE, 2, dtype=np.float32) / MLA_ROPE))
    inv = np.tile(inv.astype(np.float32), 2)
    inv_q = np.tile(inv, N_HEADS).reshape(1, N_HEADS * MLA_ROPE)
    inv_k = np.zeros((1, LANE), np.float32)
    inv_k[0, :MLA_ROPE] = inv
    return jnp.asarray(inv_q), jnp.asarray(inv_k)


def _uq_split(w):
    w3 = w.reshape(w.shape[0], N_HEADS, HEAD_DIM + MLA_ROPE)
    return w3[:, :, :HEAD_DIM].reshape(-1, 512), w3[:, :, HEAD_DIM:].reshape(-1, N_HEADS * MLA_ROPE)


def _uq_merge(gn, gr):
    r = gn.shape[0]
    return jnp.concatenate([gn.reshape(r, N_HEADS, HEAD_DIM), gr.reshape(r, N_HEADS, MLA_ROPE)], axis=2).reshape(r, 768)


EARLY_GRADS = ['ffn1_w_up', 'ffn1_w_down', 'l1_w_in', 'l1_w_out', 'ffn0_w_up', 'ffn0_w_down', 'l0_w_out']


def local_step(x, positions, target, w, late_weights=None, scatter_early=None):
    w = dict(w)
    nb, t, d = x.shape
    n = nb * t
    tm = 256
    ni = n // tm
    tc = 2 * LANE
    h0 = x.reshape(n, d)
    tgt = target.reshape(n, d)
    pos = positions.reshape(n, 1).astype(f32)
    gh = _group_matrix(512, HEAD_DIM)
    gg = _group_matrix(512, 256)
    e_heads = _head_expand()
    inv_q, inv_k = _rope_freqs()
    g = {}

    def ln_stage(h, y, gname, bname):
        ops = [_rows(h, tm), _rows(y, tm, gdtype=bf16), _param(_row(w[gname])), _param(_row(w[bname]))]
        return ops, [_rows_out(n, d, tm)]

    def ln_fwd(name, ops):
        return block_fwd(lambda *a: f_ln(*a) * 2, name, (1, ni), ops, [_rows_out(n, d, tm), _rows_out(n, d, tm, bf16)])

    def ffn_act_stage(u, cw, cb):
        nj = D_FF // tc
        ops = [_cols(u, t, tc, 0, D_FF, bf16), _cols(u, t, tc, nj, D_FF, bf16)] \
            + [_cparam(cw[i:i + 1], tc) for i in range(3)] + [_cparam(_row(cb), tc)]
        return ops, [Out((n, D_FF), (t, tc), lambda j, i: (i, j), bf16)], (nj, nb)

    w_in0 = _pad_cols(w['l0_w_in'], L0_PAD)
    h0b = h0.astype(bf16)
    proj0 = mm(h0b, w_in0, "l0_proj")

    shift_ops = [_cols(proj0, t, tc, 0, RWKV_COLS, bf16), _cparam(_row(w['rwkv_mix']), tc)]
    shift_outs = [Out((n, RWKV_COLS), (t, tc), lambda j, i: (i, j))]
    shift_grid = (RWKV_COLS // tc, nb)
    (ps,) = block_fwd(f_shift_mix, "rwkv_shift", shift_grid, shift_ops, shift_outs)

    pre_ops = [_colblock(ps, tm, 512, 512), _colblock(ps, tm, 1536, 128), _colblock(ps, tm, 1664, 128),
               _param(_row(w['rwkv_w0'])), _param(w['rwkv_w2']), _param(_row(w['rwkv_a0'])), _param(w['rwkv_a2']),
               _param(w['rwkv_g2']), _param(_row(w['rwkv_k_k'])), _param(_row(w['rwkv_k_a'])), _param(gh, diff=False)]
    pre_outs = [_rows_out(n, 512, tm) for _ in range(5)]
    lw, k2, na, bb, gate_r = block_fwd(f_rwkv_pre, "rwkv_pre", (1, ni), pre_ops, pre_outs)
    def arrived(group, gathered):
        if late_weights is not None:
            for name, got in zip(late_weights[group][0], gathered):
                w[name] = late_weights[2](name, got)

    ride = None if late_weights is None else GatherRide(late_weights[0][1])
    (y_tok, s0_saved), gathered = rwkv_scan_fwd(ps, lw, k2, na, bb, nb, t, ride)
    arrived(0, gathered)
    w_out0 = w['l0_w_out']

    post_ops = [_rows(y_tok, tm), _colblock(ps, tm, 0, 512), _rows(k2, tm), _colblock(ps, tm, 1024, 512),
                _rows(gate_r, tm), _param(_row(w['rwkv_ln_g'])), _param(_row(w['rwkv_ln_b'])),
                _param(w['rwkv_r_k'].reshape(1, 512)), _param(gh, diff=False)]
    post_outs = [_rows_out(n, 512, tm, bf16)]
    (y_a,) = block_fwd(f_rwkv_post, "rwkv_post", (1, ni), post_ops, post_outs)

    xbc_off = (RWKV_COLS + 512) // tc
    conv_ops = [_cols(proj0, t, tc, xbc_off, 1024, bf16)] + [_cparam(w['ssm_conv_w'][i:i + 1], tc) for i in range(4)] \
        + [_cparam(_row(w['ssm_conv_b']), tc)]
    conv_outs = [Out((n, 1024), (t, tc), lambda j, i: (i, j))]
    conv_grid = (1024 // tc, nb)
    (xbc_act,) = block_fwd(f_conv4_silu, "ssm_conv", conv_grid, conv_ops, conv_outs)

    dt_bias, a_log, d_skip = _pad_row(w['ssm_dt_bias']), _pad_row(w['ssm_a_log']), _pad_row(w['ssm_d'])
    y_ssd, ssd_states = ssd_fwd(xbc_act, proj0, dt_bias, a_log, d_skip, e_heads, nb, t)

    z_tok = proj0[:, RWKV_COLS:RWKV_COLS + 512]
    spost_ops = [_rows(y_ssd, tm), _rows(z_tok, tm, gdtype=bf16), _param(_row(w['ssm_norm_g'])), _param(gg, diff=False)]
    spost_outs = [_rows_out(n, 512, tm, bf16)]
    (y_b,) = block_fwd(f_ssm_post, "ssm_post", (1, ni), spost_ops, spost_outs)

    mixed0 = mm(y_b, w_out0[512:], "l0_out_b", add=mm(y_a, w_out0[:512], "l0_out_a"))
    ln1_ops, ln_outs = ln_stage(h0, mixed0, 'l0_ln1_g', 'l0_ln1_b')
    h1, h1b = ln_fwd("l0_ln1", ln1_ops)

    u0 = mm(h1b, w['ffn0_w_up'], "ffn0_up")
    act0_ops, act_outs, act_grid = ffn_act_stage(u0, w['ffn0_conv_w'], w['ffn0_conv_b'])
    (act0,) = block_fwd(f_ffn_act, "ffn0_act", act_grid, act0_ops, act_outs)
    f0 = mm(act0, w['ffn0_w_down'], "ffn0_down")
    ln2_ops, _ = ln_stage(h1, f0, 'l0_ln2_g', 'l0_ln2_b')
    h2, h2b = ln_fwd("l0_ln2", ln2_ops)

    w_in1 = _pad_cols(w['l1_w_in'], L1_PAD)
    proj1 = mm(h2b, w_in1, "l1_proj")
    w_qn, w_qr = _uq_split(w['mla_w_uq'])
    mpre_ops = [_colblock(proj1, tm, 1536, 256, bf16), _colblock(proj1, tm, 1792, 128, bf16),
                _colblock(proj1, tm, 1920, 128, bf16),
                Op(pos, (tm, 1), lambda j, i: (i, 0), diff=False),
                _param(_row(w['mla_q_norm_g'])), _param(w_qn), _param(w_qr),
                _param(_row(w['mla_kv_norm_g'])), _param(w['mla_w_ukv']), _param(inv_q, diff=False),
                _param(inv_k, diff=False)]
    mpre_outs = [_rows_out(n, 512, tm), _rows_out(n, N_HEADS * MLA_ROPE, tm), _rows_out(n, 1024, tm),
                 _rows_out(n, LANE, tm)]
    q_nope, q_rope, kv, kpe = block_fwd(f_mla_pre, "mla_pre", (1, ni), mpre_ops, mpre_outs)
    ride = None if late_weights is None else GatherRide(late_weights[1][1])
    (o_sb, sb_masses), gathered = sb_fwd(proj1, nb, t, ride)
    arrived(1, gathered)
    w_out1 = w['l1_w_out']
    o_mla, o_mla32, mla_lse = mla_fwd(q_nope, q_rope, kv, kpe, nb, t)

    mixed1 = mm(o_mla, w_out1[512:], "l1_out_b", add=mm(o_sb, w_out1[:512], "l1_out_a"))
    ln3_ops, _ = ln_stage(h2, mixed1, 'l1_ln1_g', 'l1_ln1_b')
    h3, h3b = ln_fwd("l1_ln1", ln3_ops)
    u1 = mm(h3b, w['ffn1_w_up'], "ffn1_up")
    act1_ops, _, _ = ffn_act_stage(u1, w['ffn1_conv_w'], w['ffn1_conv_b'])
    (act1,) = block_fwd(f_ffn_act, "ffn1_act", act_grid, act1_ops, act_outs)
    f1 = mm(act1, w['ffn1_w_down'], "ffn1_down")
    ln4_ops, _ = ln_stage(h3, f1, 'l1_ln2_g', 'l1_ln2_b')
    (h4,) = block_fwd(f_ln, "l1_ln2", (1, ni), ln4_ops, ln_outs)

    loss_part, dh4 = loss_head(h4, tgt)

    def vec(a_):
        return a_.reshape(-1)

    def ffn_bwd(tag, dh_out, ln_ops, act_ops, h_in, act, w_up, w_down, names):
        dh_res, df, gg_, gb_ = block_bwd(f_ln, tag + "_ln2_bwd", (1, ni), ln_ops, ln_outs, [dh_out])
        g[names[4]], g[names[5]] = vec(gg_), vec(gb_)
        g[names[3]] = mm(act, df, tag + "_down_dw", ta=True)
        dact = mm(df, w_down.T, tag + "_down_dx")
        dgate, dup, dw0, dw1, dw2, dcb = block_bwd(f_ffn_act, tag + "_act_bwd", act_grid, act_ops, act_outs, [dact])
        g[names[1]] = jnp.concatenate([dw0, dw1, dw2], axis=0)
        g[names[2]] = vec(dcb)
        g[names[0]] = jnp.concatenate([mm(h_in, dgate, tag + "_gate_dw", ta=True),
                                       mm(h_in, dup, tag + "_upv_dw", ta=True)], axis=1)
        w_up_t = w_up.T
        dh = mm(dgate, w_up_t[:D_FF], tag + "_gate_dx", add=dh_res)
        return mm(dup, w_up_t[D_FF:], tag + "_upv_dx", add=dh)

    def out_bwd(tag, dmixed, y_first, y_second, w_out, name):
        g[name] = jnp.concatenate([mm(y_first, dmixed, tag + "_a_dw", ta=True),
                                   mm(y_second, dmixed, tag + "_b_dw", ta=True)], axis=0)
        w_t = w_out.T
        return mm(dmixed, w_t[:, :512], tag + "_a_dx"), mm(dmixed, w_t[:, 512:], tag + "_b_dx")

    dh3 = ffn_bwd("ffn1", dh4, ln4_ops, act1_ops, h3b, act1, w['ffn1_w_up'], w['ffn1_w_down'],
                  ['ffn1_w_up', 'ffn1_conv_w', 'ffn1_conv_b', 'ffn1_w_down', 'l1_ln2_g', 'l1_ln2_b'])

    dh2_res, dmixed1, g3g, g3b = block_bwd(f_ln, "l1_ln1_bwd", (1, ni), ln3_ops, ln_outs, [dh3])
    g['l1_ln1_g'], g['l1_ln1_b'] = vec(g3g), vec(g3b)
    do_sb, do_mla = out_bwd("l1_out", dmixed1, o_sb, o_mla, w_out1, 'l1_w_out')

    dq_nope, dq_rope, dkv, dkpe = mla_bwd(q_nope, q_rope, kv, kpe, o_mla32, mla_lse, do_mla, nb, t)
    dsb_q, dsb_k, dsb_v = sb_bwd(proj1, sb_masses, do_sb, nb, t)
    (dc_q, dc_kv, dkpe_raw, gqg, gwqn, gwqr, gkvg, g['mla_w_ukv']) = block_bwd(
        f_mla_pre, "mla_pre_bwd", (1, ni), mpre_ops, mpre_outs, [dq_nope, dq_rope, dkv, dkpe])
    g['mla_q_norm_g'], g['mla_kv_norm_g'] = vec(gqg), vec(gkvg)
    g['mla_w_uq'] = _uq_merge(gwqn, gwqr)
    dproj1 = jnp.concatenate([dsb_q.astype(bf16), dsb_k.astype(bf16), dsb_v.astype(bf16), dc_q, dc_kv, dkpe_raw],
                             axis=1)
    g['l1_w_in'] = mm(h2b, dproj1, "l1_proj_dw", ta=True)[:, :L1_COLS]
    dh2 = mm(dproj1, w_in1.T, "l1_proj_dx", add=dh2_res)

    dh1 = ffn_bwd("ffn0", dh2, ln2_ops, act0_ops, h1b, act0, w['ffn0_w_up'], w['ffn0_w_down'],
                  ['ffn0_w_up', 'ffn0_conv_w', 'ffn0_conv_b', 'ffn0_w_down', 'l0_ln2_g', 'l0_ln2_b'])

    dh0_res, dmixed0, g1g, g1b = block_bwd(f_ln, "l0_ln1_bwd", (1, ni), ln1_ops, ln_outs, [dh1])
    g['l0_ln1_g'], g['l0_ln1_b'] = vec(g1g), vec(g1b)
    dy_a, dy_b = out_bwd("l0_out", dmixed0, y_a, y_b, w_out0, 'l0_w_out')

    dy_ssd, dz, gng = block_bwd(f_ssm_post, "ssm_post_bwd", (1, ni), spost_ops, spost_outs, [dy_b])
    g['ssm_norm_g'] = vec(gng)
    dxs, dbm, dcm, ddt_raw, gdb, gal, gdsk = ssd_bwd(xbc_act, proj0, dt_bias, a_log, d_skip, e_heads, ssd_states,
                                                     dy_ssd, nb, t)
    g['ssm_dt_bias'], g['ssm_a_log'], g['ssm_d'] = gdb[0, :8], gal[0, :8], gdsk[0, :8]
    dxbc_act = jnp.concatenate([dxs, dbm, dcm], axis=1)
    dxbc, cw0, cw1, cw2, cw3, gcb = block_bwd(f_conv4_silu, "ssm_conv_bwd", conv_grid, conv_ops, conv_outs, [dxbc_act])
    g['ssm_conv_w'] = jnp.concatenate([cw0, cw1, cw2, cw3], axis=0)
    g['ssm_conv_b'] = vec(gcb)

    dy_tok, dr_post, dk2_post, dv_post, dgate, glg, glb, grk = block_bwd(
        f_rwkv_post, "rwkv_post_bwd", (1, ni), post_ops, post_outs, [dy_a])
    g['rwkv_ln_g'], g['rwkv_ln_b'], g['rwkv_r_k'] = vec(glg), vec(glb), grk.reshape(N_HEADS, HEAD_DIM)
    ride = None if scatter_early is None else ScatterRide(scatter_early({name: g[name] for name in EARLY_GRADS}))
    (dr, dlw, dk2, dv, dna, dbb), early = rwkv_scan_bwd(s0_saved, ps, lw, k2, na, bb, dy_tok, nb, t, ride)
    (dk_pre, dwa_lo, dg_lo, gw0, g['rwkv_w2'], ga0, g['rwkv_a2'], g['rwkv_g2'], gkk, gka) = block_bwd(
        f_rwkv_pre, "rwkv_pre_bwd", (1, ni), pre_ops, pre_outs, [dlw, dk2 + dk2_post, dna, dbb, dgate])
    g['rwkv_w0'], g['rwkv_a0'], g['rwkv_k_k'], g['rwkv_k_a'] = vec(gw0), vec(ga0), vec(gkk), vec(gka)
    dps = jnp.concatenate([dr + dr_post, dk_pre, dv + dv_post, dwa_lo, dg_lo], axis=1)
    dp_rwkv, gmix = block_bwd(f_shift_mix, "rwkv_shift_bwd", shift_grid, shift_ops, shift_outs, [dps])
    g['rwkv_mix'] = vec(gmix)

    dproj0 = jnp.concatenate([dp_rwkv, dz, dxbc, ddt_raw], axis=1)
    g['l0_w_in'] = mm(h0b, dproj0, "l0_proj_dw", ta=True)[:, :L0_COLS]
    grad_x = mm(dproj0, w_in0.T, "l0_proj_dx", add=dh0_res)
    return loss_part, grad_x.reshape(nb, t, d), g, early


MESH = pl.DeviceIdType.MESH
ANY = pl.BlockSpec(memory_space=pl.ANY)
AXES = ("x", "y", "c")


def _place():
    x, y, c = lax.axis_index("x"), lax.axis_index("y"), lax.axis_index("c")
    chips = [(1 - x, y), (x, 1 - y), (1 - x, 1 - y)]
    return x, y, c, chips


def _dma_sems(n):
    return pltpu.SemaphoreType.DMA((n,))


class GatherRide:
    def __init__(self, shards):
        n = len(shards)
        self.inputs = list(shards)
        self.out_shapes = [jax.ShapeDtypeStruct((N_SHARD,) + a.shape, a.dtype) for a in shards]
        self.scratch = [_dma_sems(3 * n), _dma_sems(3 * n), _dma_sems(3 * n), _dma_sems(3 * n), _dma_sems(n)]

    def _copies(self, ins, outs, sems):
        ici_send, ici_recv, d2d_send, d2d_recv, local_sems = sems
        x, y, c, chips = _place()
        me = 2 * x + y
        pairs = list(enumerate(zip(ins, outs)))

        def over_ici(k, j, slot, to):
            return pltpu.make_async_remote_copy(
                src_ref=ins[k].at[c], dst_ref=outs[k].at[slot, c], send_sem=ici_send.at[3 * k + j],
                recv_sem=ici_recv.at[3 * k + j], device_id=to, device_id_type=MESH)

        def to_sibling(k, j, slot, half):
            return pltpu.make_async_remote_copy(
                src_ref=outs[k].at[slot, half], dst_ref=outs[k].at[slot, half], send_sem=d2d_send.at[3 * k + j],
                recv_sem=d2d_recv.at[3 * k + j], device_id=(x, y, 1 - c), device_id_type=MESH)

        mine = [pltpu.make_async_copy(a, o.at[me], local_sems.at[k]) for k, (a, o) in pairs]
        sends = [over_ici(k, j, me, (cx, cy, c)) for k, _ in pairs for j, (cx, cy) in enumerate(chips)]
        return c, chips, pairs, over_ici, to_sibling, mine, sends

    def start(self, ins, outs, sems):
        _, _, _, _, _, mine, sends = self._copies(ins, outs, sems)
        for cp in mine + sends:
            cp.start()

    def finish(self, ins, outs, sems):
        c, chips, pairs, over_ici, to_sibling, mine, sends = self._copies(ins, outs, sems)
        passed = []
        for k, _ in pairs:
            for j, (cx, cy) in enumerate(chips):
                over_ici(k, j, 2 * cx + cy, (cx, cy, c)).wait_recv()
                passed.append(to_sibling(k, j, 2 * cx + cy, c))
                passed[-1].start()
        for k, _ in pairs:
            for j, (cx, cy) in enumerate(chips):
                to_sibling(k, j, 2 * cx + cy, 1 - c).wait_recv()
        for cp in sends + passed:
            cp.wait_send()
        for cp in mine:
            cp.wait()


class ScatterRide:
    def __init__(self, parts):
        n = len(parts)
        self.inputs = list(parts)
        self.out_shapes = [jax.ShapeDtypeStruct(a.shape, a.dtype) for a in parts]
        self.scratch = [_dma_sems(3 * n), _dma_sems(3 * n), _dma_sems(n)]

    def _copies(self, ins, outs, sems):
        send_sems, recv_sems, local_sems = sems
        x, y, c, chips = _place()
        me = 2 * x + y
        pairs = list(enumerate(zip(ins, outs)))

        def over_ici(k, j, src_slot, dst_slot, to):
            return pltpu.make_async_remote_copy(
                src_ref=ins[k].at[src_slot], dst_ref=outs[k].at[dst_slot], send_sem=send_sems.at[3 * k + j],
                recv_sem=recv_sems.at[3 * k + j], device_id=to, device_id_type=MESH)

        mine = [pltpu.make_async_copy(a.at[me], o.at[me], local_sems.at[k]) for k, (a, o) in pairs]
        sends = [over_ici(k, j, 2 * cx + cy, me, (cx, cy, c)) for k, _ in pairs for j, (cx, cy) in enumerate(chips)]
        arrivals = lambda: [over_ici(k, j, me, 2 * cx + cy, (cx, cy, c))
                            for k, _ in pairs for j, (cx, cy) in enumerate(chips)]
        return mine, sends, arrivals

    def start(self, ins, outs, sems):
        mine, sends, _ = self._copies(ins, outs, sems)
        for cp in mine + sends:
            cp.start()

    def finish(self, ins, outs, sems):
        mine, sends, arrivals = self._copies(ins, outs, sems)
        for cp in arrivals():
            cp.wait_recv()
        for cp in sends:
            cp.wait_send()
        for cp in mine:
            cp.wait()


def _run_ride(ride, name):
    n = len(ride.inputs)

    def body(*refs):
        ins, outs, sems = refs[:n], refs[n:2 * n], refs[2 * n:]
        ride.start(ins, outs, sems)
        ride.finish(ins, outs, sems)

    return pl.pallas_call(body, name=name, in_specs=[ANY] * n, out_specs=[ANY] * n, out_shape=ride.out_shapes,
                          scratch_shapes=ride.scratch)(*ride.inputs)


def gather_shards(shards):
    return _run_ride(GatherRide(shards), "gather_shards")


def swap_halves(pieces, name):
    n = len(pieces)

    def body(*refs):
        ins, outs = refs[:n], refs[n:2 * n]
        send_sems, recv_sems = refs[2 * n:]
        x, y, c, _ = _place()
        cps = [pltpu.make_async_remote_copy(
            src_ref=a.at[:, 1 - c], dst_ref=o, send_sem=send_sems.at[k], recv_sem=recv_sems.at[k],
            device_id=(x, y, 1 - c), device_id_type=MESH) for k, (a, o) in enumerate(zip(ins, outs))]
        for cp in cps:
            cp.start()
        for cp in cps:
            cp.wait()

    return pl.pallas_call(
        body, name=name, in_specs=[ANY] * n, out_specs=[ANY] * n,
        out_shape=[jax.ShapeDtypeStruct((a.shape[0],) + a.shape[2:], a.dtype) for a in pieces],
        scratch_shapes=[_dma_sems(n), _dma_sems(n)],
    )(*pieces)


def scatter_to_chips(parts):
    return _run_ride(ScatterRide(parts), "scatter_to_chips")


def share_halves(bufs):
    n = len(bufs)

    def body(*refs):
        ins, outs = refs[:n], refs[n:2 * n]
        send_sems, recv_sems = refs[2 * n:]
        x, y, c, _ = _place()
        cps = [pltpu.make_async_remote_copy(
            src_ref=a.at[c], dst_ref=o.at[c], send_sem=send_sems.at[k], recv_sem=recv_sems.at[k],
            device_id=(x, y, 1 - c), device_id_type=MESH) for k, (a, o) in enumerate(zip(ins, outs))]
        for cp in cps:
            cp.start()
        for k, (a, o) in enumerate(zip(ins, outs)):
            cps[k].wait_send()
            pltpu.make_async_remote_copy(
                src_ref=a.at[c], dst_ref=o.at[1 - c], send_sem=send_sems.at[k], recv_sem=recv_sems.at[k],
                device_id=(x, y, 1 - c), device_id_type=MESH).wait_recv()

    return pl.pallas_call(
        body, name="share_halves", in_specs=[ANY] * n, out_specs=[ANY] * n,
        out_shape=[jax.ShapeDtypeStruct(a.shape, a.dtype) for a in bufs],
        input_output_aliases={k: k for k in range(n)},
        scratch_shapes=[_dma_sems(n), _dma_sems(n)],
    )(*bufs)


def pair_add(piece, recv, core, name, out_dtype):
    _, _, h, cdim = piece.shape
    tr = _rtile(h, cdim)

    def body(c_ref, a_ref, b_ref, o_ref):
        o_ref[...] = (a_ref[0] + b_ref[...]).astype(o_ref.dtype)

    spec = pl.BlockSpec((1, tr, cdim), lambda p, i, c_ref: (p, i, 0))
    return pl.pallas_call(
        body, name=name,
        grid_spec=pltpu.PrefetchScalarGridSpec(
            num_scalar_prefetch=1, grid=(N_SHARD, h // tr),
            in_specs=[pl.BlockSpec((1, 1, tr, cdim), lambda p, i, c_ref: (p, c_ref[0], i, 0)), spec],
            out_specs=spec),
        out_shape=jax.ShapeDtypeStruct((N_SHARD, h, cdim), out_dtype),
        compiler_params=_cparams(("parallel", "parallel")),
    )(core, piece, recv)


def chip_add(parts, core, name):
    _, h, cdim = parts.shape
    tr = _rtile(h, cdim, 1024 * 1024)

    def body(c_ref, p_ref, o_ref):
        p = [p_ref[s].astype(f32) for s in range(N_SHARD)]
        o_ref[0] = ((p[0] + p[1]) + p[2]) + p[3]

    return pl.pallas_call(
        body, name=name,
        grid_spec=pltpu.PrefetchScalarGridSpec(
            num_scalar_prefetch=1, grid=(h // tr,),
            in_specs=[pl.BlockSpec((N_SHARD, tr, cdim), lambda i, c_ref: (0, i, 0))],
            out_specs=pl.BlockSpec((1, tr, cdim), lambda i, c_ref: (c_ref[0], i, 0))),
        out_shape=jax.ShapeDtypeStruct((2, h, cdim), f32), compiler_params=_cparams(("parallel",)),
    )(core, parts)


def adamw(w, g, m, v, name):
    rows, cdim = w.shape
    tr = _rtile(rows, cdim, 1024 * 1024)

    def body(w_ref, g_ref, m_ref, v_ref, d_ref, nm_ref, nv_ref):
        gv = g_ref[...]
        m_new = ADAM_B1 * m_ref[...] + (1.0 - ADAM_B1) * gv
        v_new = ADAM_B2 * v_ref[...] + (1.0 - ADAM_B2) * jnp.square(gv)
        m_hat = m_new / (1.0 - ADAM_B1 ** ADAM_STEP)
        v_hat = v_new / (1.0 - ADAM_B2 ** ADAM_STEP)
        d_ref[...] = -ADAM_LR * (m_hat / (jnp.sqrt(v_hat) + ADAM_EPS) + ADAM_WD * w_ref[...])
        nm_ref[...] = m_new
        nv_ref[...] = v_new

    spec = pl.BlockSpec((tr, cdim), lambda i: (i, 0))
    return pl.pallas_call(body, name=name, grid=(rows // tr,), in_specs=[spec] * 4, out_specs=[spec] * 3,
                          out_shape=[jax.ShapeDtypeStruct(w.shape, f32)] * 3,
                          compiler_params=_cparams(("parallel",)))(w, g, m, v)


SMALL_MULTIPLE = 16 * LANE


def _pack_flat(parts, multiple=SMALL_MULTIPLE):
    flat = jnp.concatenate([p.reshape(-1) for p in parts])
    pad = (-flat.shape[0]) % multiple
    return jnp.pad(flat, (0, pad)).reshape(-1, LANE)


def _unpack_flat(buf, shapes):
    flat = buf.reshape(-1)
    out, off = [], 0
    for s in shapes:
        cnt = int(np.prod(s))
        out.append(flat[off:off + cnt].reshape(s))
        off += cnt
    return out


def _full_from_shards(name, gathered):
    if name in COL_SHARDED:
        return jnp.concatenate([gathered[s] for s in range(N_SHARD)], axis=1)
    return gathered.reshape(-1, gathered.shape[2])


def _pieces(name, grad):
    if name in COL_SHARDED:
        r, cdim = grad.shape
        return grad.reshape(r, N_SHARD, cdim // N_SHARD).transpose(1, 0, 2)
    return grad.reshape(N_SHARD, grad.shape[0] // N_SHARD, grad.shape[1])


def _small_pieces(name, grad):
    if name in COL_SHARDED or name in ROW_SHARDED:
        return _pieces(name, grad).reshape(N_SHARD, -1)
    return jnp.broadcast_to(grad.reshape(1, -1), (N_SHARD, grad.size))


def kernel(x, positions, l0_w_in, rwkv_mix, rwkv_w0, rwkv_w2, rwkv_a0, rwkv_a2, rwkv_g2, rwkv_k_k, rwkv_k_a, rwkv_r_k, rwkv_ln_g, rwkv_ln_b, ssm_conv_w, ssm_conv_b, ssm_dt_bias, ssm_a_log, ssm_d, ssm_norm_g, l0_w_out, l0_ln1_g, l0_ln1_b, ffn0_w_up, ffn0_conv_w, ffn0_conv_b, ffn0_w_down, l0_ln2_g, l0_ln2_b, l1_w_in, mla_q_norm_g, mla_w_uq, mla_kv_norm_g, mla_w_ukv, l1_w_out, l1_ln1_g, l1_ln1_b, ffn1_w_up, ffn1_conv_w, ffn1_conv_b, ffn1_w_down, l1_ln2_g, l1_ln2_b, loss_target, m_l0_w_in, m_rwkv_mix, m_rwkv_w0, m_rwkv_w2, m_rwkv_a0, m_rwkv_a2, m_rwkv_g2, m_rwkv_k_k, m_rwkv_k_a, m_rwkv_r_k, m_rwkv_ln_g, m_rwkv_ln_b, m_ssm_conv_w, m_ssm_conv_b, m_ssm_dt_bias, m_ssm_a_log, m_ssm_d, m_ssm_norm_g, m_l0_w_out, m_l0_ln1_g, m_l0_ln1_b, m_ffn0_w_up, m_ffn0_conv_w, m_ffn0_conv_b, m_ffn0_w_down, m_l0_ln2_g, m_l0_ln2_b, m_l1_w_in, m_mla_q_norm_g, m_mla_w_uq, m_mla_kv_norm_g, m_mla_w_ukv, m_l1_w_out, m_l1_ln1_g, m_l1_ln1_b, m_ffn1_w_up, m_ffn1_conv_w, m_ffn1_conv_b, m_ffn1_w_down, m_l1_ln2_g, m_l1_ln2_b, v_l0_w_in, v_rwkv_mix, v_rwkv_w0, v_rwkv_w2, v_rwkv_a0, v_rwkv_a2, v_rwkv_g2, v_rwkv_k_k, v_rwkv_k_a, v_rwkv_r_k, v_rwkv_ln_g, v_rwkv_ln_b, v_ssm_conv_w, v_ssm_conv_b, v_ssm_dt_bias, v_ssm_a_log, v_ssm_d, v_ssm_norm_g, v_l0_w_out, v_l0_ln1_g, v_l0_ln1_b, v_ffn0_w_up, v_ffn0_conv_w, v_ffn0_conv_b, v_ffn0_w_down, v_l0_ln2_g, v_l0_ln2_b, v_l1_w_in, v_mla_q_norm_g, v_mla_w_uq, v_mla_kv_norm_g, v_mla_w_ukv, v_l1_w_out, v_l1_ln1_g, v_l1_ln1_b, v_ffn1_w_up, v_ffn1_conv_w, v_ffn1_conv_b, v_ffn1_w_down, v_l1_ln2_g, v_l1_ln2_b):
    args = locals()
    w_loc = {n: args[n] for n in WEIGHTS}
    m_loc = {n: args["m_" + n] for n in WEIGHTS}
    v_loc = {n: args["v_" + n] for n in WEIGHTS}
    core = lax.axis_index("c").astype(jnp.int32).reshape(1)

    small_sharded = [n for n in SMALL if n in COL_SHARDED]
    halves = lambda a: a.reshape(2, a.shape[0] // 2, a.shape[1])
    whole = lambda name, got: _full_from_shards(name, got.reshape(N_SHARD, -1, got.shape[3]))
    first = gather_shards([halves(w_loc['l0_w_in'].astype(bf16)), halves(_pack_flat([w_loc[n] for n in small_sharded]))])
    w_have = {n: w_loc[n] for n in WEIGHTS if n not in BIG}
    w_have['l0_w_in'] = whole('l0_w_in', first[0])
    small_all = first[1].reshape(N_SHARD, -1, LANE)
    per_shard = [_unpack_flat(small_all[s], [w_loc[n].shape for n in small_sharded]) for s in range(N_SHARD)]
    for k, n in enumerate(small_sharded):
        w_have[n] = jnp.concatenate([per_shard[s][k] for s in range(N_SHARD)], axis=1)
    shards_of = lambda names: (names, [halves(w_loc[n].astype(bf16)) for n in names])
    late = (shards_of(['l0_w_out', 'ffn0_w_up', 'ffn0_w_down', 'l1_w_in']),
            shards_of(['l1_w_out', 'ffn1_w_up', 'ffn1_w_down']), whole)

    def pair_sums(names, pieces, tag):
        pieces = [p.reshape(N_SHARD, 2, p.shape[1] // 2, p.shape[2]) for p in pieces]
        return [pair_add(p, r, core, "pair_add_" + n, f32 if n == 'small' else bf16)
                for n, p, r in zip(names, pieces, swap_halves(pieces, "swap_halves_" + tag))]

    loss_part, grad_x, g_full, early = local_step(
        x, positions, loss_target, w_have, late,
        lambda gd: pair_sums(EARLY_GRADS, [_pieces(n, gd[n]) for n in EARLY_GRADS], "early"))
    loss = lax.psum(loss_part[0, 0], AXES)

    small_flat = jnp.concatenate([_small_pieces(n, g_full[n]) for n in SMALL], axis=1)
    pad = (-small_flat.shape[1]) % SMALL_MULTIPLE
    small_pieces = jnp.pad(small_flat, ((0, 0), (0, pad))).reshape(N_SHARD, -1, LANE)
    rest = scatter_to_chips(pair_sums(['l0_w_in', 'small'], [_pieces('l0_w_in', g_full['l0_w_in']), small_pieces],
                                      "rest"))
    from_chips = dict(zip(EARLY_GRADS + ['l0_w_in', 'small'], list(early) + list(rest)))
    units = BIG + ['small']
    both = share_halves([chip_add(from_chips[n], core, "chip_add_" + n) for n in units])
    reduced = [b.reshape(-1, b.shape[2]) for b in both]

    out = {}
    for n, gred in zip(BIG, reduced):
        out[n] = (gred,) + tuple(adamw(w_loc[n], gred, m_loc[n], v_loc[n], "adamw_" + n))
    shapes = [w_loc[n].shape for n in SMALL]
    packs = [_pack_flat([d[n] for n in SMALL]) for d in (w_loc, m_loc, v_loc)]
    small_res = (reduced[-1],) + tuple(adamw(packs[0], reduced[-1], packs[1], packs[2], "adamw_small"))
    small_unpacked = [_unpack_flat(b, shapes) for b in small_res]
    for k, n in enumerate(SMALL):
        out[n] = tuple(u[k] for u in small_unpacked)
    return (loss, grad_x, *[out[n][0] for n in WEIGHTS], *[out[n][1] for n in WEIGHTS],
            *[out[n][2] for n in WEIGHTS], *[out[n][3] for n in WEIGHTS])
```

```python
import functools

import numpy as np
import jax
import jax.numpy as jnp
from jax import lax
from jax.experimental import pallas as pl
from jax.experimental.pallas import tpu as pltpu

f32 = jnp.float32
bf16 = jnp.bfloat16
HI = lax.Precision.HIGHEST
MID = lax.Precision.HIGH

D_MODEL = 1024
HEAD_DIM = 64
N_HEADS = 8
RWKV_COLS = 1792
RWKV_GN_EPS = 64e-5
SSM_STATE = 128
SSM_CHUNK = 128
L0_COLS = 3336
L0_PAD = 3456
L1_COLS = 1952
L1_PAD = 2048
MLA_ROPE = 32
ROPE_THETA = 10000.0
D_FF = 2816
DEPTH = 2
ALPHA = (2 * DEPTH) ** 0.25
ADAM_LR = 0.001
ADAM_B1 = 0.9
ADAM_B2 = 0.999
ADAM_EPS = 1e-08
ADAM_WD = 0.01
ADAM_STEP = 10
RWKV_CHUNK = 64
RWKV_HEADS_PER_STEP = 8
LANE = 128
SUBLANE = 8
VMEM_LIMIT = 56 * 1024 * 1024

WEIGHTS = ['l0_w_in', 'rwkv_mix', 'rwkv_w0', 'rwkv_w2', 'rwkv_a0', 'rwkv_a2', 'rwkv_g2', 'rwkv_k_k', 'rwkv_k_a',
           'rwkv_r_k', 'rwkv_ln_g', 'rwkv_ln_b', 'ssm_conv_w', 'ssm_conv_b', 'ssm_dt_bias', 'ssm_a_log', 'ssm_d',
           'ssm_norm_g', 'l0_w_out', 'l0_ln1_g', 'l0_ln1_b', 'ffn0_w_up', 'ffn0_conv_w', 'ffn0_conv_b',
           'ffn0_w_down', 'l0_ln2_g', 'l0_ln2_b', 'l1_w_in', 'mla_q_norm_g', 'mla_w_uq', 'mla_kv_norm_g',
           'mla_w_ukv', 'l1_w_out', 'l1_ln1_g', 'l1_ln1_b', 'ffn1_w_up', 'ffn1_conv_w', 'ffn1_conv_b',
           'ffn1_w_down', 'l1_ln2_g', 'l1_ln2_b']
COL_SHARDED = ['l0_w_in', 'rwkv_w2', 'rwkv_a2', 'rwkv_g2', 'ssm_conv_w', 'ffn0_w_up', 'ffn0_conv_w', 'l1_w_in',
               'mla_w_uq', 'mla_w_ukv', 'ffn1_w_up', 'ffn1_conv_w']
ROW_SHARDED = ['l0_w_out', 'ffn0_w_down', 'l1_w_out', 'ffn1_w_down']
BIG = ['l0_w_in', 'l0_w_out', 'ffn0_w_up', 'ffn0_w_down', 'l1_w_in', 'l1_w_out', 'ffn1_w_up', 'ffn1_w_down']
SMALL = [n for n in WEIGHTS if n not in BIG]
N_SHARD = 4


def _cparams(sem):
    return pltpu.CompilerParams(dimension_semantics=sem, vmem_limit_bytes=VMEM_LIMIT)


def _dg(a, b, ca, cb, prec=None):
    return lax.dot_general(a, b, (((ca,), (cb,)), ((), ())), precision=prec, preferred_element_type=f32)


def hdot(a, b):
    return _dg(a, b, 1, 0, HI)


def mdot(a, b):
    return _dg(a, b, 1, 0, MID)


def mdot_nt(a, b):
    return _dg(a, b, 1, 1, MID)


def mdot_tn(a, b):
    return _dg(a, b, 0, 0, MID)


def _b(x):
    return x.astype(bf16)


@jax.custom_vjp
def bdot(x, w):
    return _dg(_b(x), _b(w), 1, 0)


def _bdot_fwd(x, w):
    return bdot(x, w), (x, w)


def _bdot_bwd(res, g):
    x, w = res
    return _dg(_b(g), _b(w), 1, 1).astype(x.dtype), _dg(_b(x), _b(g), 0, 0).astype(w.dtype)


bdot.defvjp(_bdot_fwd, _bdot_bwd)


@jax.custom_vjp
def bdot_nt(x, y):
    return _dg(_b(x), _b(y), 1, 1)


def _bdot_nt_fwd(x, y):
    return bdot_nt(x, y), (x, y)


def _bdot_nt_bwd(res, g):
    x, y = res
    return _dg(_b(g), _b(y), 1, 0), _dg(_b(g), _b(x), 0, 0)


bdot_nt.defvjp(_bdot_nt_fwd, _bdot_nt_bwd)


@jax.custom_vjp
def bdot_tn(x, y):
    return _dg(_b(x), _b(y), 0, 0)


def _bdot_tn_fwd(x, y):
    return bdot_tn(x, y), (x, y)


def _bdot_tn_bwd(res, g):
    x, y = res
    return _dg(_b(y), _b(g), 1, 1), _dg(_b(x), _b(g), 1, 0)


bdot_tn.defvjp(_bdot_tn_fwd, _bdot_tn_bwd)


def _sigmoid(x):
    return 1.0 / (1.0 + jnp.exp(-x))


@jax.custom_vjp
def softplus(x):
    e = jnp.exp(-jnp.abs(x))
    u = 1.0 + e
    log1p = jnp.where(u == 1.0, e, jnp.log(u) * e / jnp.where(u == 1.0, 1.0, u - 1.0))
    return jnp.maximum(x, 0.0) + log1p


def _softplus_fwd(x):
    return softplus(x), x


def _softplus_bwd(x, g):
    return (g * _sigmoid(x),)


softplus.defvjp(_softplus_fwd, _softplus_bwd)


@jax.custom_vjp
def softplus_abs(x):
    return jnp.maximum(x, 0.0) + jnp.log(1.0 + jnp.exp(-jnp.abs(x)))


def _softplus_abs_fwd(x):
    return softplus_abs(x), x


softplus_abs.defvjp(_softplus_abs_fwd, _softplus_bwd)


def _two_pass(x, m):
    hi = _b(x)
    lo = _b(x - hi.astype(f32))
    m16 = _b(m)
    return _dg(hi, m16, 1, 0) + _dg(lo, m16, 1, 0)


def _upper(n):
    return (_iota2((n, n), 0) > _iota2((n, n), 1)).astype(f32)


@jax.custom_vjp
def suffix_sums(x):
    return _two_pass(x, _upper(x.shape[1]))


def _suffix_sums_fwd(x):
    return suffix_sums(x), None


def _suffix_sums_bwd(_, g):
    return (_two_pass(g, _upper(g.shape[1]).T),)


suffix_sums.defvjp(_suffix_sums_fwd, _suffix_sums_bwd)


def silu(x):
    return x * _sigmoid(x)


def _shift_rows(x, k, up):
    if k == 0:
        return x
    t = x.shape[0]
    rows = lax.broadcasted_iota(jnp.int32, x.shape, 0)
    if up:
        return jnp.where(rows < t - k, pltpu.roll(x, t - k, 0), 0.0)
    return jnp.where(rows >= k, pltpu.roll(x, k, 0), 0.0)


@functools.partial(jax.custom_vjp, nondiff_argnums=(1,))
def shift_down(x, k):
    return _shift_rows(x, k, False)


def _shift_down_fwd(x, k):
    return _shift_rows(x, k, False), None


def _shift_down_bwd(k, _, g):
    return (_shift_rows(g, k, True),)


shift_down.defvjp(_shift_down_fwd, _shift_down_bwd)


@functools.partial(jax.custom_vjp, nondiff_argnums=(1,))
def lane_roll(x, s):
    return pltpu.roll(x, s % x.shape[1], 1)


def _lane_roll_fwd(x, s):
    return lane_roll(x, s), None


def _lane_roll_bwd(s, _, g):
    return (pltpu.roll(g, (-s) % g.shape[1], 1),)


lane_roll.defvjp(_lane_roll_fwd, _lane_roll_bwd)


def rot_half32(x):
    first = (lax.broadcasted_iota(jnp.int32, x.shape, 1) % MLA_ROPE) < (MLA_ROPE // 2)
    return jnp.where(first, -lane_roll(x, -(MLA_ROPE // 2)), lane_roll(x, MLA_ROPE // 2))


def _iota2(shape, axis):
    return lax.broadcasted_iota(jnp.int32, shape, axis)


class Op:
    def __init__(self, arr, block, imap, diff=True, acc=None, gshape=None, gimap=None, gdtype=f32):
        self.arr, self.block, self.imap, self.diff, self.acc = arr, tuple(block), imap, diff, acc
        self.gshape = tuple(arr.shape) if gshape is None else tuple(gshape)
        self.gimap = imap if gimap is None else gimap
        self.gdtype = gdtype


class Out:
    def __init__(self, shape, block, imap, dtype=f32):
        self.shape, self.block, self.imap, self.dtype = tuple(shape), tuple(block), imap, dtype


def block_fwd(fn, name, grid, ops, outs):
    n_in = len(ops)

    def body(*refs):
        vals = [r[...] for r in refs[:n_in]]
        res = fn(*vals)
        for r, v in zip(refs[n_in:], res):
            r[...] = v.astype(r.dtype)

    res = pl.pallas_call(
        body, name=name, grid=grid,
        in_specs=[pl.BlockSpec(o.block, o.imap) for o in ops],
        out_specs=[pl.BlockSpec(o.block, o.imap) for o in outs],
        out_shape=[jax.ShapeDtypeStruct(o.shape, o.dtype) for o in outs],
        compiler_params=_cparams(("arbitrary", "arbitrary")),
    )(*[o.arr for o in ops])
    return tuple(res)


def block_bwd(fn, name, grid, ops, outs, douts):
    n_in, n_out = len(ops), len(outs)
    dix = [k for k, o in enumerate(ops) if o.diff]

    def body(*refs):
        vals = [r[...] for r in refs[:n_in]]
        dvals = tuple(r[...] for r in refs[n_in:n_in + n_out])
        grefs = refs[n_in + n_out:]

        def f(*d):
            full = list(vals)
            for k, v in zip(dix, d):
                full[k] = v
            return tuple(fn(*full))

        _, vjp = jax.vjp(f, *[vals[k] for k in dix])
        grads = vjp(dvals)
        j, i = pl.program_id(0), pl.program_id(1)
        for k, gref, g in zip(dix, grefs, grads):
            acc = ops[k].acc
            if acc is None:
                gref[...] = g.astype(gref.dtype)
            else:
                first = (i == 0) if acc == 'i' else jnp.logical_and(i == 0, j == 0)

                @pl.when(first)
                def _():
                    gref[...] = g

                @pl.when(jnp.logical_not(first))
                def _():
                    gref[...] += g

    gspecs = [pl.BlockSpec(ops[k].block, ops[k].gimap) for k in dix]
    gshapes = [jax.ShapeDtypeStruct(ops[k].gshape, ops[k].gdtype) for k in dix]
    res = pl.pallas_call(
        body, name=name, grid=grid,
        in_specs=[pl.BlockSpec(o.block, o.imap) for o in ops] + [pl.BlockSpec(o.block, o.imap) for o in outs],
        out_specs=gspecs, out_shape=gshapes,
        compiler_params=_cparams(("arbitrary", "arbitrary")),
    )(*[o.arr for o in ops], *douts)
    return tuple(res)


def _rows(arr, tm, diff=True, gdtype=f32):
    return Op(arr, (tm, arr.shape[1]), lambda j, i: (i, 0), diff=diff, gdtype=gdtype)


def _param(arr, diff=True):
    return Op(arr, arr.shape, lambda j, i: (0,) * arr.ndim, diff=diff, acc='ij')


def _rows_out(n, c, tm, dtype=f32):
    return Out((n, c), (tm, c), lambda j, i: (i, 0), dtype)


def _cols(arr, t, tc, off=0, width=None, gdtype=f32):
    width = arr.shape[1] if width is None else width
    return Op(arr, (t, tc), lambda j, i: (i, j + off), gshape=(arr.shape[0], width), gimap=lambda j, i: (i, j),
              gdtype=gdtype)


def _cparam(arr, tc):
    return Op(arr, (arr.shape[0], tc), lambda j, i: (0, j), acc='i')


def _colblock(arr, tm, off, width, gdtype=f32):
    return Op(arr, (tm, width), lambda j, i: (i, off // width), gshape=(arr.shape[0], width),
              gimap=lambda j, i: (i, 0), gdtype=gdtype)


def _tile(n, cap):
    best = None
    for t in range(LANE, min(n, cap) + 1, LANE):
        if n % t == 0:
            best = t
    return n if best is None else best


def _rtile(rows, cols, cap_bytes=2 * 1024 * 1024):
    best = None
    for t in range(SUBLANE, rows + 1, SUBLANE):
        if rows % t == 0 and t * cols * 4 <= cap_bytes:
            best = t
    return rows if best is None else best


def mm(a, b, name, ta=False, add=None, pieces=None, into=None):
    m = a.shape[1] if ta else a.shape[0]
    kd = a.shape[0] if ta else a.shape[1]
    n = b.shape[1]
    tm, tn = _tile(m, 1408), _tile(n, 1408)
    tk = kd if kd <= 2048 else _tile(kd, 1408)
    nk = kd // tk
    ca = 0 if ta else 1
    n_extra = (add is not None) + (into is not None)

    def body(*refs):
        a_ref, b_ref = refs[:2]
        o_ref, acc = refs[2 + n_extra:]
        k = pl.program_id(2)

        @pl.when(k == 0)
        def _():
            acc[...] = jnp.zeros_like(acc)

        acc[...] += _dg(_b(a_ref[...]), _b(b_ref[...]), ca, 0)

        @pl.when(k == nk - 1)
        def _():
            o_ref[...] = (acc[...] if add is None else acc[...] + refs[2][...]).reshape(o_ref.shape)

    a_spec = pl.BlockSpec((tk, tm), lambda i, j, k: (k, i)) if ta else pl.BlockSpec((tm, tk), lambda i, j, k: (i, k))
    b_spec = pl.BlockSpec((tk, tn), lambda i, j, k: (k, j))
    o_spec = pl.BlockSpec((tm, tn), lambda i, j, k: (i, j))
    out_shape = jax.ShapeDtypeStruct((m, n), f32)
    args, specs, aliases = [a, b], [a_spec, b_spec], {}
    if add is not None:
        args.append(add)
        specs.append(o_spec)
    if pieces is not None:
        count, first, width = pieces
        per = width // tn
        o_spec = pl.BlockSpec((1, tm, tn), lambda i, j, k: (first + j // per, i, j % per))
        out_shape = jax.ShapeDtypeStruct((count, m, width), f32)
        if into is not None:
            aliases = {len(args): 0}
            args.append(into)
            specs.append(pl.BlockSpec(memory_space=pl.ANY))
    return pl.pallas_call(
        body, name=name, grid=(m // tm, n // tn, nk), in_specs=specs, out_specs=o_spec, out_shape=out_shape,
        scratch_shapes=[pltpu.VMEM((tm, tn), f32)], input_output_aliases=aliases,
        compiler_params=_cparams(("parallel", "parallel", "arbitrary")),
    )(*args)


def f_ln(h, y, g, b):
    x = ALPHA * h + y
    mu = jnp.mean(x, axis=-1, keepdims=True)
    xc = x - mu
    var = jnp.mean(xc * xc, axis=-1, keepdims=True)
    return (xc * lax.rsqrt(var + 1e-5) * g + b,)


def f_shift_mix(p, mix):
    return (p + (shift_down(p, 1) - p) * mix,)


def f_rwkv_pre(k, wa_lo, g_lo, w0, w2, a0, a2, g2, k_k, k_a, gh):
    w_lo, a_lo = wa_lo[:, :64], wa_lo[:, 64:]
    log_w = -softplus(-(w0 + bdot(jnp.tanh(w_lo), w2))) - 0.5
    lw = -jnp.exp(log_w)
    a = _sigmoid(a0 + bdot(a_lo, a2))
    g = bdot(_sigmoid(g_lo), g2)
    kk = k * k_k
    kk = kk / jnp.maximum(jnp.sqrt(mdot(kk * kk, gh)), 1e-12)
    k2 = k * (1.0 + (a - 1.0) * k_a)
    return lw, k2, -kk, kk * a, g


def f_rwkv_post(y, r, k2, v, g, ln_g, ln_b, r_k, gh):
    mu = mdot(y, gh) * (1.0 / HEAD_DIM)
    yc = y - mu
    var = mdot(yc * yc, gh) * (1.0 / HEAD_DIM)
    yn = yc * lax.rsqrt(var + RWKV_GN_EPS) * ln_g + ln_b
    bonus = mdot(r * k2 * r_k, gh) * v
    return ((yn + bonus) * g,)


def f_conv4_silu(x, w0, w1, w2, w3, b):
    y = b + shift_down(x, 3) * w0 + shift_down(x, 2) * w1 + shift_down(x, 1) * w2 + x * w3
    return (silu(y),)


def f_ssm_post(y, z, norm_g, gg):
    u = y * silu(z)
    ms = mdot(u * u, gg) * (1.0 / 256.0)
    return (u * lax.rsqrt(ms + 1e-5) * norm_g,)


def f_ffn_act(gate, up, w0, w1, w2, b):
    gc = b + shift_down(gate, 2) * w0 + shift_down(gate, 1) * w1 + gate * w2
    return (silu(gc) * up,)


def _rms(x, g, eps=1e-6):
    return x * lax.rsqrt(jnp.mean(x * x, axis=-1, keepdims=True) + eps) * g


def f_mla_pre(c_q, c_kv, kpe, pos, q_g, w_qn, w_qr, kv_g, w_ukv, inv_q, inv_k):
    qn_in = _rms(c_q, q_g)
    q_nope = bdot(qn_in, w_qn)
    qr = bdot(qn_in, w_qr)
    kv = bdot(_rms(c_kv, kv_g), w_ukv)
    ang_q = pos * inv_q
    ang_k = pos * inv_k
    return (q_nope, qr * jnp.cos(ang_q) + rot_half32(qr) * jnp.sin(ang_q), kv,
            kpe * jnp.cos(ang_k) + rot_half32(kpe) * jnp.sin(ang_k))


def rwkv_chunk(s0, r, lw, k, v, a, b):
    hs = range(len(r))
    l = r[0].shape[0]
    ri, ci = _iota2((l, l), 0), _iota2((l, l), 1)
    strict, incl = ri > ci, ri >= ci
    tri, eye = incl.astype(f32), (ri == ci).astype(f32)
    last = (_iota2((l, 1), 0) == l - 1).astype(f32)
    c = [hdot(tri, lw[h]) for h in hs]
    at = [a[h] * jnp.exp(c[h] - lw[h]) for h in hs]
    wi = [jnp.exp(-c[h]) for h in hs]
    bt = [b[h] * wi[h] for h in hs]
    kt = [k[h] * wi[h] for h in hs]
    rt = [r[h] * jnp.exp(c[h]) for h in hs]
    nab = [jnp.where(strict, mdot_nt(at[h], bt[h]), 0.0) for h in hs]
    nak = [jnp.where(strict, mdot_nt(at[h], kt[h]), 0.0) for h in hs]
    g = [bdot_nt(at[h], s0[h]) + bdot(nak[h], v[h]) for h in hs]
    x = [eye + nab[h] for h in hs]
    p = [mdot(nab[h], nab[h]) for h in hs]
    steps = max(1, (l - 1).bit_length()) - 1
    for it in range(steps):
        x = [x[h] + mdot(p[h], x[h]) for h in hs]
        if it < steps - 1:
            p = [mdot(p[h], p[h]) for h in hs]
    u = [mdot(x[h], g[h]) for h in hs]
    mrb = [jnp.where(incl, mdot_nt(rt[h], bt[h]), 0.0) for h in hs]
    mrk = [jnp.where(incl, mdot_nt(rt[h], kt[h]), 0.0) for h in hs]
    y = [bdot_nt(rt[h], s0[h]) + bdot(mrb[h], u[h]) + bdot(mrk[h], v[h]) for h in hs]
    s1 = [(s0[h] + bdot_tn(u[h], bt[h]) + bdot_tn(v[h], kt[h])) * jnp.exp(jnp.sum(c[h] * last, axis=0, keepdims=True))
          for h in hs]
    return y, s1


def ssd_chunk(xs, bm, cm, dt_raw, s_in, dt_bias, a_log, d_skip, e_heads):
    l = xs.shape[0]
    ri, ci = _iota2((l, l), 0), _iota2((l, l), 1)
    incl = ri >= ci
    tri = incl.astype(f32)
    dt = softplus(dt_raw + dt_bias)
    a128 = dt * (-jnp.exp(a_log))
    lane0 = (_iota2((1, HEAD_DIM), 1) == 0).astype(f32)
    last = (_iota2((l, 1), 0) == l - 1).astype(f32)
    hs = range(N_HEADS)
    group = lambda m, g: m[:, g * SSM_STATE:(g + 1) * SSM_STATE]
    cb = [bdot_nt(group(cm, g), group(bm, g)) for g in range(2)]
    e_all = jnp.concatenate(e_heads, axis=1)
    dt_all = hdot(dt, e_all)
    ac_all = hdot(tri, hdot(a128, e_all))
    xd_all = xs * dt_all
    skip_all = xs * hdot(jnp.broadcast_to(d_skip, (l, LANE)), e_all)
    ac = [ac_all[:, _head(h)] for h in hs]
    xd = [xd_all[:, _head(h)] for h in hs]
    col = [jnp.broadcast_to(jnp.sum(ac[h] * lane0, axis=1, keepdims=True), (l, l)) for h in hs]
    decay = [jnp.exp(jnp.where(incl, col[h] - col[h].T, -1e30)) for h in hs]
    y_diag = [bdot(cb[h // 4] * decay[h], xd[h]) for h in hs]
    a_tot = [jnp.sum(ac[h] * last, axis=0, keepdims=True) for h in hs]
    y_off = [jnp.exp(ac[h]) * bdot(group(cm, h // 4), s_in[h]) for h in hs]
    s_out = [jnp.exp(a_tot[h]) * s_in[h] + bdot_tn(group(bm, h // 4), xd[h] * jnp.exp(a_tot[h] - ac[h])) for h in hs]
    return jnp.concatenate([y_diag[h] + y_off[h] for h in hs], axis=1) + skip_all, s_out


SB_KEYS = LANE
MLA_KEYS = 256


def sb_tile(q, k, v, run, q0, k0):
    bq, kb = q.shape[0], k.shape[0]
    z = bdot_nt(q, k) * HEAD_DIM ** -0.5
    strict = (k0 + _iota2((bq, kb), 1)) < (q0 + _iota2((bq, kb), 0))
    lk = jnp.where(strict, -softplus_abs(z), 0.0)
    log_att = z + lk + suffix_sums(lk) + run
    att = jnp.where(strict, jnp.exp(jnp.where(strict, log_att, 0.0)), 0.0)
    return bdot(att, v), jnp.sum(lk, axis=1, keepdims=True)


def mla_scores(qn, qp, kn, kp, q0, k0):
    bq, kb = qn.shape[0], kn.shape[0]
    s = (bdot_nt(qn, kn) + bdot_nt(qp, kp)) * (HEAD_DIM + MLA_ROPE) ** -0.5
    causal = (k0 + _iota2((bq, kb), 1)) <= (q0 + _iota2((bq, kb), 0))
    return jnp.where(causal, s, -1e30), causal


def mla_tile_loss(qn, qp, kn, kp, v, do, lse, dsum, q0, k0):
    s, causal = mla_scores(qn, qp, kn, kp, q0, k0)
    p = jnp.where(causal, jnp.exp(s - lse), 0.0)
    return jnp.sum(do * bdot(p, v)) - jnp.sum(dsum * jnp.sum(p, axis=1, keepdims=True))


def _head(h):
    return slice(h * HEAD_DIM, (h + 1) * HEAD_DIM)


def _rwkv_specs(nc, rev):
    hp = RWKV_HEADS_PER_STEP
    w = hp * HEAD_DIM
    chunk = (lambda c: nc - 1 - c) if rev else (lambda c: c)
    tok = lambda off: pl.BlockSpec((RWKV_CHUNK, w), lambda b, g, c: (b * nc + chunk(c), off // w + g))
    st = pl.BlockSpec((1, hp, HEAD_DIM, HEAD_DIM), lambda b, g, c: ((b * (N_HEADS // hp) + g) * nc + chunk(c), 0, 0, 0))
    return tok, st


def _hosted_call(work, name, grid, in_specs, out_specs, out_shape, scratch, args, ride):
    n_in, n_out, n_scr = len(in_specs), len(out_specs), len(scratch)
    k = 0 if ride is None else len(ride.inputs)

    def body(*refs):
        ins, r_in = refs[:n_in], refs[n_in:n_in + k]
        outs, r_out = refs[n_in + k:n_in + k + n_out], refs[n_in + k + n_out:n_in + 2 * k + n_out]
        scr, r_sems = refs[n_in + 2 * k + n_out:n_in + 2 * k + n_out + n_scr], refs[n_in + 2 * k + n_out + n_scr:]
        ids = [pl.program_id(a) for a in range(len(grid))]
        if ride is not None:
            @pl.when(functools.reduce(jnp.logical_and, [i == 0 for i in ids]))
            def _():
                ride.start(r_in, r_out, r_sems)

        work(ins, outs, scr)
        if ride is not None:
            @pl.when(functools.reduce(jnp.logical_and, [i == g - 1 for i, g in zip(ids, grid)]))
            def _():
                ride.finish(r_in, r_out, r_sems)

    res = pl.pallas_call(
        body, name=name, grid=grid, in_specs=list(in_specs) + [ANY] * k, out_specs=list(out_specs) + [ANY] * k,
        out_shape=list(out_shape) + ([] if ride is None else ride.out_shapes),
        scratch_shapes=list(scratch) + ([] if ride is None else ride.scratch),
        compiler_params=_cparams(("arbitrary",) * len(grid)),
    )(*args, *([] if ride is None else ride.inputs))
    return res[:n_out], res[n_out:]


def rwkv_scan_fwd(ps, lw, k2, na, bb, nb, t, ride=None):
    hp, nc = RWKV_HEADS_PER_STEP, t // RWKV_CHUNK
    ng = N_HEADS // hp
    tok, st = _rwkv_specs(nc, False)

    def work(ins, outs, scr):
        r_ref, v_ref, lw_ref, k_ref, a_ref, b_ref = ins
        y_ref, s0_ref = outs
        (s,) = scr

        @pl.when(pl.program_id(2) == 0)
        def _():
            s[...] = jnp.zeros_like(s)

        s0_ref[0] = s[...]
        heads = lambda ref: [ref[:, _head(h)] for h in range(hp)]
        y, s1 = rwkv_chunk([s[h] for h in range(hp)], heads(r_ref), heads(lw_ref), heads(k_ref), heads(v_ref),
                           heads(a_ref), heads(b_ref))
        for h in range(hp):
            y_ref[:, _head(h)] = y[h]
            s[h] = s1[h]

    return _hosted_call(
        work, "rwkv_scan_fwd", (nb, ng, nc), [tok(0), tok(1024), tok(0), tok(0), tok(0), tok(0)], [tok(0), st],
        [jax.ShapeDtypeStruct((nb * t, N_HEADS * HEAD_DIM), f32),
         jax.ShapeDtypeStruct((nb * ng * nc, hp, HEAD_DIM, HEAD_DIM), f32)],
        [pltpu.VMEM((hp, HEAD_DIM, HEAD_DIM), f32)], (ps, ps, lw, k2, na, bb), ride)


def rwkv_scan_bwd(s0, ps, lw, k2, na, bb, dy, nb, t, ride=None):
    hp, nc = RWKV_HEADS_PER_STEP, t // RWKV_CHUNK
    ng = N_HEADS // hp
    tok, st = _rwkv_specs(nc, True)

    def work(ins, outs, scr):
        s0_ref, r_ref, v_ref, lw_ref, k_ref, a_ref, b_ref, dy_ref = ins
        (ds,) = scr

        @pl.when(pl.program_id(2) == 0)
        def _():
            ds[...] = jnp.zeros_like(ds)

        heads = lambda ref: [ref[:, _head(h)] for h in range(hp)]
        _, vjp = jax.vjp(rwkv_chunk, [s0_ref[0, h] for h in range(hp)], heads(r_ref), heads(lw_ref), heads(k_ref),
                         heads(v_ref), heads(a_ref), heads(b_ref))
        g = vjp((heads(dy_ref), [ds[h] for h in range(hp)]))
        for h in range(hp):
            ds[h] = g[0][h]
            for ref, val in zip(outs, g[1:]):
                ref[:, _head(h)] = val[h]

    return _hosted_call(
        work, "rwkv_scan_bwd", (nb, ng, nc), [st, tok(0), tok(1024), tok(0), tok(0), tok(0), tok(0), tok(0)],
        [tok(0)] * 6, [jax.ShapeDtypeStruct((nb * t, N_HEADS * HEAD_DIM), f32)] * 6,
        [pltpu.VMEM((hp, HEAD_DIM, HEAD_DIM), f32)], (s0, ps, ps, lw, k2, na, bb, dy), ride)


def _ssd_specs(nb, nch, rev):
    def row(b, c):
        return b * nch + (nch - 1 - c if rev else c)

    l = SSM_CHUNK
    xs = pl.BlockSpec((l, 512), lambda b, c: (row(b, c), 0))
    bm = pl.BlockSpec((l, 256), lambda b, c: (row(b, c), 2))
    cm = pl.BlockSpec((l, 256), lambda b, c: (row(b, c), 3))
    dt = pl.BlockSpec((l, LANE), lambda b, c: (row(b, c), (L0_PAD - LANE) // LANE))
    st = pl.BlockSpec((1, 1, N_HEADS, SSM_STATE, HEAD_DIM), lambda b, c: (b, (nch - 1 - c if rev else c), 0, 0, 0))
    par = pl.BlockSpec((1, LANE), lambda b, c: (0, 0))
    eh = pl.BlockSpec((N_HEADS, LANE, HEAD_DIM), lambda b, c: (0, 0, 0))
    return xs, bm, cm, dt, st, par, eh, row


def ssd_fwd(xbc_act, proj0, dt_bias, a_log, d_skip, e_heads, nb, t):
    nch = t // SSM_CHUNK
    n_tok = nb * t
    xs, bm, cm, dt, st, par, eh, row = _ssd_specs(nb, nch, False)

    def body(x_ref, b_ref, c_ref, dt_ref, db_ref, al_ref, dsk_ref, e_ref, y_ref, st_ref, s):
        @pl.when(pl.program_id(1) == 0)
        def _():
            s[...] = jnp.zeros_like(s)

        st_ref[0, 0] = s[...]
        y, s_out = ssd_chunk(x_ref[...], b_ref[...], c_ref[...], dt_ref[...], [s[h] for h in range(N_HEADS)],
                             db_ref[...], al_ref[...], dsk_ref[...], [e_ref[h] for h in range(N_HEADS)])
        y_ref[...] = y
        for h in range(N_HEADS):
            s[h] = s_out[h]

    return pl.pallas_call(
        body, name="ssd_fwd", grid=(nb, nch), in_specs=[xs, bm, cm, dt, par, par, par, eh],
        out_specs=[pl.BlockSpec((SSM_CHUNK, 512), lambda b, c: (row(b, c), 0)), st],
        out_shape=[jax.ShapeDtypeStruct((n_tok, 512), f32),
                   jax.ShapeDtypeStruct((nb, nch, N_HEADS, SSM_STATE, HEAD_DIM), f32)],
        scratch_shapes=[pltpu.VMEM((N_HEADS, SSM_STATE, HEAD_DIM), f32)],
        compiler_params=_cparams(("arbitrary", "arbitrary")),
    )(xbc_act, xbc_act, xbc_act, proj0, dt_bias, a_log, d_skip, e_heads)


def ssd_bwd(xbc_act, proj0, dt_bias, a_log, d_skip, e_heads, states, dy, nb, t):
    nch = t // SSM_CHUNK
    n_tok = nb * t
    xs, bm, cm, dt, st, par, eh, row = _ssd_specs(nb, nch, True)

    def body(x_ref, b_ref, c_ref, dt_ref, db_ref, al_ref, dsk_ref, e_ref, st_ref, dy_ref,
             dx_ref, dbm_ref, dcm_ref, ddt_ref, ddb_ref, dal_ref, ddsk_ref, ds):
        first = jnp.logical_and(pl.program_id(0) == 0, pl.program_id(1) == 0)

        @pl.when(pl.program_id(1) == 0)
        def _():
            ds[...] = jnp.zeros_like(ds)

        e_list = [e_ref[h] for h in range(N_HEADS)]

        def f(x, bmv, cmv, dtr, s_in, dbv, alv, dskv):
            return ssd_chunk(x, bmv, cmv, dtr, s_in, dbv, alv, dskv, e_list)

        _, vjp = jax.vjp(f, x_ref[...], b_ref[...], c_ref[...], dt_ref[...],
                         [st_ref[0, 0, h] for h in range(N_HEADS)], db_ref[...], al_ref[...], dsk_ref[...])
        g = vjp((dy_ref[...], [ds[h] for h in range(N_HEADS)]))
        dx_ref[...], dbm_ref[...], dcm_ref[...], ddt_ref[...] = g[0], g[1], g[2], g[3].astype(bf16)
        for h in range(N_HEADS):
            ds[h] = g[4][h]
        for ref, val in zip((ddb_ref, dal_ref, ddsk_ref), g[5:]):
            @pl.when(first)
            def _():
                ref[...] = val

            @pl.when(jnp.logical_not(first))
            def _():
                ref[...] += val

    rows_spec = lambda w: pl.BlockSpec((SSM_CHUNK, w), lambda b, c: (row(b, c), 0))
    return pl.pallas_call(
        body, name="ssd_bwd", grid=(nb, nch),
        in_specs=[xs, bm, cm, dt, par, par, par, eh, st, rows_spec(512)],
        out_specs=[rows_spec(512), rows_spec(256), rows_spec(256), rows_spec(LANE), par, par, par],
        out_shape=[jax.ShapeDtypeStruct((n_tok, 512), f32), jax.ShapeDtypeStruct((n_tok, 256), f32),
                   jax.ShapeDtypeStruct((n_tok, 256), f32), jax.ShapeDtypeStruct((n_tok, LANE), bf16)]
        + [jax.ShapeDtypeStruct((1, LANE), f32)] * 3,
        scratch_shapes=[pltpu.VMEM((N_HEADS, SSM_STATE, HEAD_DIM), f32)],
        compiler_params=_cparams(("arbitrary", "arbitrary")),
    )(xbc_act, xbc_act, xbc_act, proj0, dt_bias, a_log, d_skip, e_heads, states, dy)


ATT_BQ = 256
SB_BQ = 512
SB_TILES_PER_PASS = 2
SB_HEADS_PER_STEP = 2
MLA_HEADS_PER_STEP = 4


def _loop_tiles(n_tiles, per_pass, fn, init):
    def several(i, carry):
        for r in range(per_pass):
            carry = fn(per_pass * i + r, carry)
        return carry

    return lax.fori_loop(0, n_tiles // per_pass, several, init)


def _sb_specs(t, bq, nq):
    w = SB_HEADS_PER_STEP * HEAD_DIM
    qs = lambda off: pl.BlockSpec((bq, w), lambda b, g, i: (b * nq + i, off // w + g))
    ks = lambda off: pl.BlockSpec((t, w), lambda b, g, i: (b, off // w + g))
    return qs, ks


def _sb_mass_spec(bq, nq):
    return pl.BlockSpec((bq, SB_HEADS_PER_STEP * LANE), lambda b, g, i: (b * nq + i, g))


def sb_fwd(proj1, nb, t, ride=None):
    bq = min(SB_BQ, t)
    nq = t // bq
    qs, ks = _sb_specs(t, bq, nq)

    def work(ins, outs, _):
        q_ref, k_ref, v_ref = ins
        o_ref, mass_ref = outs
        q0 = pl.program_id(2) * bq
        n_tiles = (q0 + bq) // SB_KEYS
        hs = range(SB_HEADS_PER_STEP)
        q = [q_ref[:, _head(h)] for h in hs]
        lanes = _iota2((1, LANE), 1)

        def step(i, carry):
            j = n_tiles - 1 - i
            k0 = pl.multiple_of(j * SB_KEYS, SB_KEYS)
            out = []
            for h in hs:
                o, run, kept = carry[h]
                o_t, mass = sb_tile(q[h], k_ref[pl.ds(k0, SB_KEYS), _head(h)], v_ref[pl.ds(k0, SB_KEYS), _head(h)],
                                    run, q0, k0)
                out.append((o + o_t, run + mass, kept + mass * (lanes == j).astype(f32)))
            return out

        res = _loop_tiles(n_tiles, SB_TILES_PER_PASS, step,
                          [(jnp.zeros((bq, HEAD_DIM), f32), jnp.zeros((bq, 1), f32), jnp.zeros((bq, LANE), f32))
                           for _ in hs])
        for h in hs:
            o_ref[:, _head(h)] = res[h][0].astype(bf16)
            mass_ref[:, h * LANE:(h + 1) * LANE] = res[h][2]

    return _hosted_call(
        work, "sb_fwd", (nb, N_HEADS // SB_HEADS_PER_STEP, nq), [qs(0), ks(512), ks(1024)],
        [qs(0), _sb_mass_spec(bq, nq)],
        [jax.ShapeDtypeStruct((nb * t, 512), bf16), jax.ShapeDtypeStruct((nb * t, N_HEADS * LANE), f32)],
        [], (proj1, proj1, proj1), ride)


def sb_bwd(proj1, masses, do, nb, t):
    bq = min(SB_BQ, t)
    nq = t // bq
    qs, ks = _sb_specs(t, bq, nq)

    def body(q_ref, k_ref, v_ref, mass_ref, do_ref, dq_ref, dk_ref, dv_ref):
        @pl.when(pl.program_id(2) == 0)
        def _():
            dk_ref[...] = jnp.zeros_like(dk_ref)
            dv_ref[...] = jnp.zeros_like(dv_ref)

        q0 = pl.program_id(2) * bq
        n_tiles = (q0 + bq) // SB_KEYS
        hs = range(SB_HEADS_PER_STEP)
        q = [q_ref[:, _head(h)] for h in hs]
        do = [do_ref[:, _head(h)].astype(f32) for h in hs]
        col0 = jnp.zeros((bq, 1), f32)
        lanes = _iota2((1, LANE), 1)
        run_all = [hdot(mass_ref[:, h * LANE:(h + 1) * LANE], _upper(LANE)) for h in hs]

        def tile(ref, k0, h):
            return ref[pl.ds(k0, SB_KEYS), _head(h)]

        def grads(j, carry):
            k0 = pl.multiple_of(j * SB_KEYS, SB_KEYS)
            pick = (lanes == j).astype(f32)
            out = []
            for h in hs:
                dq, c = carry[h]
                run_in = jnp.sum(run_all[h] * pick, axis=1, keepdims=True)
                _, vjp = jax.vjp(lambda a, b, d, r: sb_tile(a, b, d, r, q0, k0),
                                 q[h], tile(k_ref, k0, h), tile(v_ref, k0, h), run_in)
                dq_t, dk_t, dv_t, drun = vjp((do[h], c))
                dk_ref[pl.ds(k0, SB_KEYS), _head(h)] += dk_t
                dv_ref[pl.ds(k0, SB_KEYS), _head(h)] += dv_t
                out.append((dq + dq_t, drun + c))
            return out

        res = _loop_tiles(n_tiles, SB_TILES_PER_PASS, grads, [(jnp.zeros((bq, HEAD_DIM), f32), col0) for _ in hs])
        for h in hs:
            dq_ref[:, _head(h)] = res[h][0]

    return pl.pallas_call(
        body, name="sb_bwd", grid=(nb, N_HEADS // SB_HEADS_PER_STEP, nq),
        in_specs=[qs(0), ks(512), ks(1024), _sb_mass_spec(bq, nq), qs(0)], out_specs=[qs(0), ks(0), ks(0)],
        out_shape=[jax.ShapeDtypeStruct((nb * t, 512), f32)] * 3,
        compiler_params=_cparams(("parallel", "parallel", "arbitrary")),
    )(proj1, proj1, proj1, masses, do)


def _mla_specs(t, bq, nq):
    hp = MLA_HEADS_PER_STEP
    qn = pl.BlockSpec((bq, hp * HEAD_DIM), lambda b, g, i: (b * nq + i, g))
    qr = pl.BlockSpec((bq, hp * MLA_ROPE), lambda b, g, i: (b * nq + i, g))
    kv = pl.BlockSpec((t, hp * 2 * HEAD_DIM), lambda b, g, i: (b, g))
    kp = pl.BlockSpec((t, LANE), lambda b, g, i: (b, 0))
    return qn, qr, kv, kp


def _mla_softmax_pass(qn, qp, kv_ref, kp_ref, q0, n_tiles, bq):
    hs = range(MLA_HEADS_PER_STEP)

    def step(j, carry):
        k0 = pl.multiple_of(j * MLA_KEYS, MLA_KEYS)
        kp = kp_ref[pl.ds(k0, MLA_KEYS), :MLA_ROPE]
        out = []
        for h in hs:
            m, l, acc = carry[h]
            s, _ = mla_scores(qn[h], qp[h], kv_ref[pl.ds(k0, MLA_KEYS), _head(2 * h)], kp, q0, k0)
            m_new = jnp.maximum(m, jnp.max(s, axis=1, keepdims=True))
            alpha, p = jnp.exp(m - m_new), jnp.exp(s - m_new)
            out.append((m_new, alpha * l + jnp.sum(p, axis=1, keepdims=True),
                        alpha * acc + bdot(p, kv_ref[pl.ds(k0, MLA_KEYS), _head(2 * h + 1)])))
        return out

    init = [(jnp.full((bq, 1), -1e30, f32), jnp.zeros((bq, 1), f32), jnp.zeros((bq, HEAD_DIM), f32)) for _ in hs]
    return lax.fori_loop(0, n_tiles, step, init)


def mla_fwd(q_nope, qr, kv, kpe, nb, t):
    bq = min(ATT_BQ, t)
    nq = t // bq
    sqn, sqr, skv, skp = _mla_specs(t, bq, nq)

    def body(qn_ref, qr_ref, kv_ref, kp_ref, o_ref, o32_ref, lse_ref):
        q0 = pl.program_id(2) * bq
        hs = range(MLA_HEADS_PER_STEP)
        qn = [qn_ref[:, _head(h)] for h in hs]
        qp = [qr_ref[:, h * MLA_ROPE:(h + 1) * MLA_ROPE] for h in hs]
        res = _mla_softmax_pass(qn, qp, kv_ref, kp_ref, q0, (q0 + bq) // MLA_KEYS, bq)
        for h in hs:
            m, l, acc = res[h]
            o = acc / l
            o_ref[:, _head(h)] = o.astype(bf16)
            o32_ref[:, _head(h)] = o
            lse_ref[:, _head(h)] = jnp.broadcast_to(m + jnp.log(l), (bq, HEAD_DIM))

    n = nb * t
    return pl.pallas_call(
        body, name="mla_fwd", grid=(nb, N_HEADS // MLA_HEADS_PER_STEP, nq), in_specs=[sqn, sqr, skv, skp],
        out_specs=[sqn, sqn, sqn],
        out_shape=[jax.ShapeDtypeStruct((n, 512), bf16), jax.ShapeDtypeStruct((n, 512), f32),
                   jax.ShapeDtypeStruct((n, 512), f32)],
        compiler_params=_cparams(("parallel", "arbitrary", "arbitrary")),
    )(q_nope, qr, kv, kpe)


def mla_bwd(q_nope, qr, kv, kpe, o32, lse_b, do, nb, t):
    bq = min(ATT_BQ, t)
    nq = t // bq
    sqn, sqr, skv, skp = _mla_specs(t, bq, nq)

    def body(qn_ref, qr_ref, kv_ref, kp_ref, o_ref, lse_ref, do_ref, dqn_ref, dqr_ref, dkv_ref, dkp_ref):
        first_q = pl.program_id(2) == 0

        @pl.when(first_q)
        def _():
            dkv_ref[...] = jnp.zeros_like(dkv_ref)

        @pl.when(jnp.logical_and(first_q, pl.program_id(1) == 0))
        def _():
            dkp_ref[...] = jnp.zeros_like(dkp_ref)

        q0 = pl.program_id(2) * bq
        n_tiles = (q0 + bq) // MLA_KEYS
        hs = range(MLA_HEADS_PER_STEP)
        qn = [qn_ref[:, _head(h)] for h in hs]
        qp = [qr_ref[:, h * MLA_ROPE:(h + 1) * MLA_ROPE] for h in hs]
        do = [do_ref[:, _head(h)].astype(f32) for h in hs]
        lse = [lse_ref[:, h * HEAD_DIM:h * HEAD_DIM + 1] for h in hs]
        dsum = [jnp.sum(do[h] * o_ref[:, _head(h)], axis=1, keepdims=True) for h in hs]

        def grads(j, carry):
            k0 = pl.multiple_of(j * MLA_KEYS, MLA_KEYS)
            rows = pl.ds(k0, MLA_KEYS)
            kp = kp_ref[rows, :MLA_ROPE]
            out = []
            for h in hs:
                dqn, dqp = carry[h]
                g = jax.grad(mla_tile_loss, argnums=(0, 1, 2, 3, 4))(
                    qn[h], qp[h], kv_ref[rows, _head(2 * h)], kp, kv_ref[rows, _head(2 * h + 1)],
                    do[h], lse[h], dsum[h], q0, k0)
                dkv_ref[rows, _head(2 * h)] += g[2]
                dkp_ref[rows, :MLA_ROPE] += g[3]
                dkv_ref[rows, _head(2 * h + 1)] += g[4]
                out.append((dqn + g[0], dqp + g[1]))
            return out

        res = lax.fori_loop(0, n_tiles, grads,
                            [(jnp.zeros((bq, HEAD_DIM), f32), jnp.zeros((bq, MLA_ROPE), f32)) for _ in hs])
        for h in hs:
            dqn_ref[:, _head(h)] = res[h][0]
            dqr_ref[:, h * MLA_ROPE:(h + 1) * MLA_ROPE] = res[h][1]

    n = nb * t
    return pl.pallas_call(
        body, name="mla_bwd", grid=(nb, N_HEADS // MLA_HEADS_PER_STEP, nq),
        in_specs=[sqn, sqr, skv, skp, sqn, sqn, sqn], out_specs=[sqn, sqr, skv, skp],
        out_shape=[jax.ShapeDtypeStruct((n, 512), f32), jax.ShapeDtypeStruct((n, N_HEADS * MLA_ROPE), f32),
                   jax.ShapeDtypeStruct((n, 1024), f32), jax.ShapeDtypeStruct((n, LANE), f32)],
        compiler_params=_cparams(("arbitrary", "arbitrary", "arbitrary")),
    )(q_nope, qr, kv, kpe, o32, lse_b, do)


def loss_head(h, target):
    n, d = h.shape
    tm = _tile(n, 512)

    def body(h_ref, t_ref, l_ref, dh_ref):
        diff = h_ref[...] - t_ref[...]
        dh_ref[...] = diff * (1.0 / d)
        part = 0.5 * jnp.sum(jnp.sum(diff * diff, axis=1, keepdims=True) * (1.0 / d), axis=0, keepdims=True)

        @pl.when(pl.program_id(0) == 0)
        def _():
            l_ref[...] = jnp.zeros_like(l_ref)

        l_ref[...] += jnp.broadcast_to(part, l_ref.shape)

    spec = pl.BlockSpec((tm, d), lambda i: (i, 0))
    return pl.pallas_call(
        body, name="loss_head", grid=(n // tm,), in_specs=[spec, spec],
        out_specs=[pl.BlockSpec((8, LANE), lambda i: (0, 0)), spec],
        out_shape=[jax.ShapeDtypeStruct((8, LANE), f32), jax.ShapeDtypeStruct((n, d), f32)],
        compiler_params=_cparams(("arbitrary",)),
    )(h, target)


def _row(v):
    return v.reshape(1, -1)


def _pad_cols(a, n):
    return jnp.pad(a, ((0, 0), (0, n - a.shape[1])))


def _pad_row(v, n=LANE):
    return jnp.pad(v.reshape(1, -1), ((0, 0), (0, n - v.shape[0])))


def _group_matrix(width, group):
    idx = np.arange(width) // group
    return jnp.asarray((idx[:, None] == idx[None, :]).astype(np.float32))


def _head_expand():
    e = np.zeros((N_HEADS, LANE, HEAD_DIM), np.float32)
    for h in range(N_HEADS):
        e[h, h, :] = 1.0
    return jnp.asarray(e)


def _rope_freqs():
    inv = 1.0 / (ROPE_THETA ** (np.arange(0, MLA_ROPE, 2, dtype=np.float32) / MLA_ROPE))
    inv = np.tile(inv.astype(np.float32), 2)
    inv_q = np.tile(inv, N_HEADS).reshape(1, N_HEADS * MLA_ROPE)
    inv_k = np.zeros((1, LANE), np.float32)
    inv_k[0, :MLA_ROPE] = inv
    return jnp.asarray(inv_q), jnp.asarray(inv_k)


def _uq_split(w):
    w3 = w.reshape(w.shape[0], N_HEADS, HEAD_DIM + MLA_ROPE)
    return w3[:, :, :HEAD_DIM].reshape(-1, 512), w3[:, :, HEAD_DIM:].reshape(-1, N_HEADS * MLA_ROPE)


def _uq_merge(gn, gr):
    r = gn.shape[0]
    return jnp.concatenate([gn.reshape(r, N_HEADS, HEAD_DIM), gr.reshape(r, N_HEADS, MLA_ROPE)], axis=2).reshape(r, 768)


EARLY_GRADS = ['ffn1_w_up', 'ffn1_w_down', 'l1_w_in', 'l1_w_out', 'ffn0_w_up', 'ffn0_w_down', 'l0_w_out']


def local_step(x, positions, target, w, late_weights=None, scatter_early=None):
    w = dict(w)
    nb, t, d = x.shape
    n = nb * t
    tm = 256
    ni = n // tm
    tc = 2 * LANE
    h0 = x.reshape(n, d)
    tgt = target.reshape(n, d)
    pos = positions.reshape(n, 1).astype(f32)
    gh = _group_matrix(512, HEAD_DIM)
    gg = _group_matrix(512, 256)
    e_heads = _head_expand()
    inv_q, inv_k = _rope_freqs()
    g = {}

    def ln_stage(h, y, gname, bname):
        ops = [_rows(h, tm), _rows(y, tm, gdtype=bf16), _param(_row(w[gname])), _param(_row(w[bname]))]
        return ops, [_rows_out(n, d, tm)]

    def ln_fwd(name, ops):
        return block_fwd(lambda *a: f_ln(*a) * 2, name, (1, ni), ops, [_rows_out(n, d, tm), _rows_out(n, d, tm, bf16)])

    def ffn_act_stage(u, cw, cb):
        nj = D_FF // tc
        ops = [_cols(u, t, tc, 0, D_FF, bf16), _cols(u, t, tc, nj, D_FF, bf16)] \
            + [_cparam(cw[i:i + 1], tc) for i in range(3)] + [_cparam(_row(cb), tc)]
        return ops, [Out((n, D_FF), (t, tc), lambda j, i: (i, j), bf16)], (nj, nb)

    w_in0 = _pad_cols(w['l0_w_in'], L0_PAD)
    h0b = h0.astype(bf16)
    proj0 = mm(h0b, w_in0, "l0_proj")

    shift_ops = [_cols(proj0, t, tc, 0, RWKV_COLS, bf16), _cparam(_row(w['rwkv_mix']), tc)]
    shift_outs = [Out((n, RWKV_COLS), (t, tc), lambda j, i: (i, j))]
    shift_grid = (RWKV_COLS // tc, nb)
    (ps,) = block_fwd(f_shift_mix, "rwkv_shift", shift_grid, shift_ops, shift_outs)

    pre_ops = [_colblock(ps, tm, 512, 512), _colblock(ps, tm, 1536, 128), _colblock(ps, tm, 1664, 128),
               _param(_row(w['rwkv_w0'])), _param(w['rwkv_w2']), _param(_row(w['rwkv_a0'])), _param(w['rwkv_a2']),
               _param(w['rwkv_g2']), _param(_row(w['rwkv_k_k'])), _param(_row(w['rwkv_k_a'])), _param(gh, diff=False)]
    pre_outs = [_rows_out(n, 512, tm) for _ in range(5)]
    lw, k2, na, bb, gate_r = block_fwd(f_rwkv_pre, "rwkv_pre", (1, ni), pre_ops, pre_outs)
    def arrived(group, gathered):
        if late_weights is not None:
            for name, got in zip(late_weights[group][0], gathered):
                w[name] = late_weights[2](name, got)

    ride = None if late_weights is None else GatherRide(late_weights[0][1])
    (y_tok, s0_saved), gathered = rwkv_scan_fwd(ps, lw, k2, na, bb, nb, t, ride)
    arrived(0, gathered)
    w_out0 = w['l0_w_out']

    post_ops = [_rows(y_tok, tm), _colblock(ps, tm, 0, 512), _rows(k2, tm), _colblock(ps, tm, 1024, 512),
                _rows(gate_r, tm), _param(_row(w['rwkv_ln_g'])), _param(_row(w['rwkv_ln_b'])),
                _param(w['rwkv_r_k'].reshape(1, 512)), _param(gh, diff=False)]
    post_outs = [_rows_out(n, 512, tm, bf16)]
    (y_a,) = block_fwd(f_rwkv_post, "rwkv_post", (1, ni), post_ops, post_outs)

    xbc_off = (RWKV_COLS + 512) // tc
    conv_ops = [_cols(proj0, t, tc, xbc_off, 1024, bf16)] + [_cparam(w['ssm_conv_w'][i:i + 1], tc) for i in range(4)] \
        + [_cparam(_row(w['ssm_conv_b']), tc)]
    conv_outs = [Out((n, 1024), (t, tc), lambda j, i: (i, j))]
    conv_grid = (1024 // tc, nb)
    (xbc_act,) = block_fwd(f_conv4_silu, "ssm_conv", conv_grid, conv_ops, conv_outs)

    dt_bias, a_log, d_skip = _pad_row(w['ssm_dt_bias']), _pad_row(w['ssm_a_log']), _pad_row(w['ssm_d'])
    y_ssd, ssd_states = ssd_fwd(xbc_act, proj0, dt_bias, a_log, d_skip, e_heads, nb, t)

    z_tok = proj0[:, RWKV_COLS:RWKV_COLS + 512]
    spost_ops = [_rows(y_ssd, tm), _rows(z_tok, tm, gdtype=bf16), _param(_row(w['ssm_norm_g'])), _param(gg, diff=False)]
    spost_outs = [_rows_out(n, 512, tm, bf16)]
    (y_b,) = block_fwd(f_ssm_post, "ssm_post", (1, ni), spost_ops, spost_outs)

    mixed0 = mm(y_b, w_out0[512:], "l0_out_b", add=mm(y_a, w_out0[:512], "l0_out_a"))
    ln1_ops, ln_outs = ln_stage(h0, mixed0, 'l0_ln1_g', 'l0_ln1_b')
    h1, h1b = ln_fwd("l0_ln1", ln1_ops)

    u0 = mm(h1b, w['ffn0_w_up'], "ffn0_up")
    act0_ops, act_outs, act_grid = ffn_act_stage(u0, w['ffn0_conv_w'], w['ffn0_conv_b'])
    (act0,) = block_fwd(f_ffn_act, "ffn0_act", act_grid, act0_ops, act_outs)
    f0 = mm(act0, w['ffn0_w_down'], "ffn0_down")
    ln2_ops, _ = ln_stage(h1, f0, 'l0_ln2_g', 'l0_ln2_b')
    h2, h2b = ln_fwd("l0_ln2", ln2_ops)

    w_in1 = _pad_cols(w['l1_w_in'], L1_PAD)
    proj1 = mm(h2b, w_in1, "l1_proj")
    w_qn, w_qr = _uq_split(w['mla_w_uq'])
    mpre_ops = [_colblock(proj1, tm, 1536, 256, bf16), _colblock(proj1, tm, 1792, 128, bf16),
                _colblock(proj1, tm, 1920, 128, bf16),
                Op(pos, (tm, 1), lambda j, i: (i, 0), diff=False),
                _param(_row(w['mla_q_norm_g'])), _param(w_qn), _param(w_qr),
                _param(_row(w['mla_kv_norm_g'])), _param(w['mla_w_ukv']), _param(inv_q, diff=False),
                _param(inv_k, diff=False)]
    mpre_outs = [_rows_out(n, 512, tm), _rows_out(n, N_HEADS * MLA_ROPE, tm), _rows_out(n, 1024, tm),
                 _rows_out(n, LANE, tm)]
    q_nope, q_rope, kv, kpe = block_fwd(f_mla_pre, "mla_pre", (1, ni), mpre_ops, mpre_outs)
    ride = None if late_weights is None else GatherRide(late_weights[1][1])
    (o_sb, sb_masses), gathered = sb_fwd(proj1, nb, t, ride)
    arrived(1, gathered)
    w_out1 = w['l1_w_out']
    o_mla, o_mla32, mla_lse = mla_fwd(q_nope, q_rope, kv, kpe, nb, t)

    mixed1 = mm(o_mla, w_out1[512:], "l1_out_b", add=mm(o_sb, w_out1[:512], "l1_out_a"))
    ln3_ops, _ = ln_stage(h2, mixed1, 'l1_ln1_g', 'l1_ln1_b')
    h3, h3b = ln_fwd("l1_ln1", ln3_ops)
    u1 = mm(h3b, w['ffn1_w_up'], "ffn1_up")
    act1_ops, _, _ = ffn_act_stage(u1, w['ffn1_conv_w'], w['ffn1_conv_b'])
    (act1,) = block_fwd(f_ffn_act, "ffn1_act", act_grid, act1_ops, act_outs)
    f1 = mm(act1, w['ffn1_w_down'], "ffn1_down")
    ln4_ops, _ = ln_stage(h3, f1, 'l1_ln2_g', 'l1_ln2_b')
    (h4,) = block_fwd(f_ln, "l1_ln2", (1, ni), ln4_ops, ln_outs)

    loss_part, dh4 = loss_head(h4, tgt)

    def vec(a_):
        return a_.reshape(-1)

    def ffn_bwd(tag, dh_out, ln_ops, act_ops, h_in, act, w_up, w_down, names):
        dh_res, df, gg_, gb_ = block_bwd(f_ln, tag + "_ln2_bwd", (1, ni), ln_ops, ln_outs, [dh_out])
        g[names[4]], g[names[5]] = vec(gg_), vec(gb_)
        g[names[3]] = mm(act, df, tag + "_down_dw", ta=True)
        dact = mm(df, w_down.T, tag + "_down_dx")
        dgate, dup, dw0, dw1, dw2, dcb = block_bwd(f_ffn_act, tag + "_act_bwd", act_grid, act_ops, act_outs, [dact])
        g[names[1]] = jnp.concatenate([dw0, dw1, dw2], axis=0)
        g[names[2]] = vec(dcb)
        quarter = 2 * D_FF // N_SHARD
        g[names[0]] = mm(h_in, dup, tag + "_upv_dw", ta=True, pieces=(N_SHARD, 2, quarter),
                         into=mm(h_in, dgate, tag + "_gate_dw", ta=True, pieces=(N_SHARD, 0, quarter)))
        w_up_t = w_up.T
        dh = mm(dgate, w_up_t[:D_FF], tag + "_gate_dx", add=dh_res)
        return mm(dup, w_up_t[D_FF:], tag + "_upv_dx", add=dh)

    def out_bwd(tag, dmixed, y_first, y_second, w_out, name):
        g[name] = jnp.concatenate([mm(y_first, dmixed, tag + "_a_dw", ta=True),
                                   mm(y_second, dmixed, tag + "_b_dw", ta=True)], axis=0)
        w_t = w_out.T
        return mm(dmixed, w_t[:, :512], tag + "_a_dx"), mm(dmixed, w_t[:, 512:], tag + "_b_dx")

    dh3 = ffn_bwd("ffn1", dh4, ln4_ops, act1_ops, h3b, act1, w['ffn1_w_up'], w['ffn1_w_down'],
                  ['ffn1_w_up', 'ffn1_conv_w', 'ffn1_conv_b', 'ffn1_w_down', 'l1_ln2_g', 'l1_ln2_b'])

    dh2_res, dmixed1, g3g, g3b = block_bwd(f_ln, "l1_ln1_bwd", (1, ni), ln3_ops, ln_outs, [dh3])
    g['l1_ln1_g'], g['l1_ln1_b'] = vec(g3g), vec(g3b)
    do_sb, do_mla = out_bwd("l1_out", dmixed1, o_sb, o_mla, w_out1, 'l1_w_out')

    dq_nope, dq_rope, dkv, dkpe = mla_bwd(q_nope, q_rope, kv, kpe, o_mla32, mla_lse, do_mla, nb, t)
    dsb_q, dsb_k, dsb_v = sb_bwd(proj1, sb_masses, do_sb, nb, t)
    (dc_q, dc_kv, dkpe_raw, gqg, gwqn, gwqr, gkvg, g['mla_w_ukv']) = block_bwd(
        f_mla_pre, "mla_pre_bwd", (1, ni), mpre_ops, mpre_outs, [dq_nope, dq_rope, dkv, dkpe])
    g['mla_q_norm_g'], g['mla_kv_norm_g'] = vec(gqg), vec(gkvg)
    g['mla_w_uq'] = _uq_merge(gwqn, gwqr)
    dproj1 = jnp.concatenate([dsb_q.astype(bf16), dsb_k.astype(bf16), dsb_v.astype(bf16), dc_q, dc_kv, dkpe_raw],
                             axis=1)
    g['l1_w_in'] = mm(h2b, dproj1, "l1_proj_dw", ta=True)[:, :L1_COLS]
    dh2 = mm(dproj1, w_in1.T, "l1_proj_dx", add=dh2_res)

    dh1 = ffn_bwd("ffn0", dh2, ln2_ops, act0_ops, h1b, act0, w['ffn0_w_up'], w['ffn0_w_down'],
                  ['ffn0_w_up', 'ffn0_conv_w', 'ffn0_conv_b', 'ffn0_w_down', 'l0_ln2_g', 'l0_ln2_b'])

    dh0_res, dmixed0, g1g, g1b = block_bwd(f_ln, "l0_ln1_bwd", (1, ni), ln1_ops, ln_outs, [dh1])
    g['l0_ln1_g'], g['l0_ln1_b'] = vec(g1g), vec(g1b)
    dy_a, dy_b = out_bwd("l0_out", dmixed0, y_a, y_b, w_out0, 'l0_w_out')

    dy_ssd, dz, gng = block_bwd(f_ssm_post, "ssm_post_bwd", (1, ni), spost_ops, spost_outs, [dy_b])
    g['ssm_norm_g'] = vec(gng)
    dxs, dbm, dcm, ddt_raw, gdb, gal, gdsk = ssd_bwd(xbc_act, proj0, dt_bias, a_log, d_skip, e_heads, ssd_states,
                                                     dy_ssd, nb, t)
    g['ssm_dt_bias'], g['ssm_a_log'], g['ssm_d'] = gdb[0, :8], gal[0, :8], gdsk[0, :8]
    dxbc_act = jnp.concatenate([dxs, dbm, dcm], axis=1)
    dxbc, cw0, cw1, cw2, cw3, gcb = block_bwd(f_conv4_silu, "ssm_conv_bwd", conv_grid, conv_ops, conv_outs, [dxbc_act])
    g['ssm_conv_w'] = jnp.concatenate([cw0, cw1, cw2, cw3], axis=0)
    g['ssm_conv_b'] = vec(gcb)

    dy_tok, dr_post, dk2_post, dv_post, dgate, glg, glb, grk = block_bwd(
        f_rwkv_post, "rwkv_post_bwd", (1, ni), post_ops, post_outs, [dy_a])
    g['rwkv_ln_g'], g['rwkv_ln_b'], g['rwkv_r_k'] = vec(glg), vec(glb), grk.reshape(N_HEADS, HEAD_DIM)
    ride = None if scatter_early is None else ScatterRide(scatter_early({name: g[name] for name in EARLY_GRADS}))
    (dr, dlw, dk2, dv, dna, dbb), early = rwkv_scan_bwd(s0_saved, ps, lw, k2, na, bb, dy_tok, nb, t, ride)
    (dk_pre, dwa_lo, dg_lo, gw0, g['rwkv_w2'], ga0, g['rwkv_a2'], g['rwkv_g2'], gkk, gka) = block_bwd(
        f_rwkv_pre, "rwkv_pre_bwd", (1, ni), pre_ops, pre_outs, [dlw, dk2 + dk2_post, dna, dbb, dgate])
    g['rwkv_w0'], g['rwkv_a0'], g['rwkv_k_k'], g['rwkv_k_a'] = vec(gw0), vec(ga0), vec(gkk), vec(gka)
    dps = jnp.concatenate([dr + dr_post, dk_pre, dv + dv_post, dwa_lo, dg_lo], axis=1)
    dp_rwkv, gmix = block_bwd(f_shift_mix, "rwkv_shift_bwd", shift_grid, shift_ops, shift_outs, [dps])
    g['rwkv_mix'] = vec(gmix)

    dproj0 = jnp.concatenate([dp_rwkv, dz, dxbc, ddt_raw], axis=1)
    g['l0_w_in'] = mm(h0b, dproj0, "l0_proj_dw", ta=True)[:, :L0_COLS]
    grad_x = mm(dproj0, w_in0.T, "l0_proj_dx", add=dh0_res)
    return loss_part, grad_x.reshape(nb, t, d), g, early


MESH = pl.DeviceIdType.MESH
ANY = pl.BlockSpec(memory_space=pl.ANY)
AXES = ("x", "y", "c")


def _place():
    x, y, c = lax.axis_index("x"), lax.axis_index("y"), lax.axis_index("c")
    chips = [(1 - x, y), (x, 1 - y), (1 - x, 1 - y)]
    return x, y, c, chips


def _dma_sems(n):
    return pltpu.SemaphoreType.DMA((n,))


class GatherRide:
    def __init__(self, shards):
        n = len(shards)
        self.inputs = list(shards)
        self.out_shapes = [jax.ShapeDtypeStruct((N_SHARD,) + a.shape, a.dtype) for a in shards]
        self.scratch = [_dma_sems(3 * n), _dma_sems(3 * n), _dma_sems(3 * n), _dma_sems(3 * n), _dma_sems(n)]

    def _copies(self, ins, outs, sems):
        ici_send, ici_recv, d2d_send, d2d_recv, local_sems = sems
        x, y, c, chips = _place()
        me = 2 * x + y
        pairs = list(enumerate(zip(ins, outs)))

        def over_ici(k, j, slot, to):
            return pltpu.make_async_remote_copy(
                src_ref=ins[k].at[c], dst_ref=outs[k].at[slot, c], send_sem=ici_send.at[3 * k + j],
                recv_sem=ici_recv.at[3 * k + j], device_id=to, device_id_type=MESH)

        def to_sibling(k, j, slot, half):
            return pltpu.make_async_remote_copy(
                src_ref=outs[k].at[slot, half], dst_ref=outs[k].at[slot, half], send_sem=d2d_send.at[3 * k + j],
                recv_sem=d2d_recv.at[3 * k + j], device_id=(x, y, 1 - c), device_id_type=MESH)

        mine = [pltpu.make_async_copy(a, o.at[me], local_sems.at[k]) for k, (a, o) in pairs]
        sends = [over_ici(k, j, me, (cx, cy, c)) for k, _ in pairs for j, (cx, cy) in enumerate(chips)]
        return c, chips, pairs, over_ici, to_sibling, mine, sends

    def start(self, ins, outs, sems):
        _, _, _, _, _, mine, sends = self._copies(ins, outs, sems)
        for cp in mine + sends:
            cp.start()

    def finish(self, ins, outs, sems):
        c, chips, pairs, over_ici, to_sibling, mine, sends = self._copies(ins, outs, sems)
        passed = []
        for k, _ in pairs:
            for j, (cx, cy) in enumerate(chips):
                over_ici(k, j, 2 * cx + cy, (cx, cy, c)).wait_recv()
                passed.append(to_sibling(k, j, 2 * cx + cy, c))
                passed[-1].start()
        for k, _ in pairs:
            for j, (cx, cy) in enumerate(chips):
                to_sibling(k, j, 2 * cx + cy, 1 - c).wait_recv()
        for cp in sends + passed:
            cp.wait_send()
        for cp in mine:
            cp.wait()


class ScatterRide:
    def __init__(self, parts):
        n = len(parts)
        self.inputs = list(parts)
        self.out_shapes = [jax.ShapeDtypeStruct(a.shape, a.dtype) for a in parts]
        self.scratch = [_dma_sems(3 * n), _dma_sems(3 * n), _dma_sems(n)]

    def _copies(self, ins, outs, sems):
        send_sems, recv_sems, local_sems = sems
        x, y, c, chips = _place()
        me = 2 * x + y
        pairs = list(enumerate(zip(ins, outs)))

        def over_ici(k, j, src_slot, dst_slot, to):
            return pltpu.make_async_remote_copy(
                src_ref=ins[k].at[src_slot], dst_ref=outs[k].at[dst_slot], send_sem=send_sems.at[3 * k + j],
                recv_sem=recv_sems.at[3 * k + j], device_id=to, device_id_type=MESH)

        mine = [pltpu.make_async_copy(a.at[me], o.at[me], local_sems.at[k]) for k, (a, o) in pairs]
        sends = [over_ici(k, j, 2 * cx + cy, me, (cx, cy, c)) for k, _ in pairs for j, (cx, cy) in enumerate(chips)]
        arrivals = lambda: [over_ici(k, j, me, 2 * cx + cy, (cx, cy, c))
                            for k, _ in pairs for j, (cx, cy) in enumerate(chips)]
        return mine, sends, arrivals

    def start(self, ins, outs, sems):
        mine, sends, _ = self._copies(ins, outs, sems)
        for cp in mine + sends:
            cp.start()

    def finish(self, ins, outs, sems):
        mine, sends, arrivals = self._copies(ins, outs, sems)
        for cp in arrivals():
            cp.wait_recv()
        for cp in sends:
            cp.wait_send()
        for cp in mine:
            cp.wait()


def _run_ride(ride, name):
    n = len(ride.inputs)

    def body(*refs):
        ins, outs, sems = refs[:n], refs[n:2 * n], refs[2 * n:]
        ride.start(ins, outs, sems)
        ride.finish(ins, outs, sems)

    return pl.pallas_call(body, name=name, in_specs=[ANY] * n, out_specs=[ANY] * n, out_shape=ride.out_shapes,
                          scratch_shapes=ride.scratch)(*ride.inputs)


def gather_shards(shards):
    return _run_ride(GatherRide(shards), "gather_shards")


def swap_halves(pieces, name):
    n = len(pieces)

    def body(*refs):
        ins, outs = refs[:n], refs[n:2 * n]
        send_sems, recv_sems = refs[2 * n:]
        x, y, c, _ = _place()
        cps = [pltpu.make_async_remote_copy(
            src_ref=a.at[:, 1 - c], dst_ref=o, send_sem=send_sems.at[k], recv_sem=recv_sems.at[k],
            device_id=(x, y, 1 - c), device_id_type=MESH) for k, (a, o) in enumerate(zip(ins, outs))]
        for cp in cps:
            cp.start()
        for cp in cps:
            cp.wait()

    return pl.pallas_call(
        body, name=name, in_specs=[ANY] * n, out_specs=[ANY] * n,
        out_shape=[jax.ShapeDtypeStruct((a.shape[0],) + a.shape[2:], a.dtype) for a in pieces],
        scratch_shapes=[_dma_sems(n), _dma_sems(n)],
    )(*pieces)


def scatter_to_chips(parts):
    return _run_ride(ScatterRide(parts), "scatter_to_chips")


def share_halves(bufs):
    n = len(bufs)

    def body(*refs):
        ins, outs = refs[:n], refs[n:2 * n]
        send_sems, recv_sems = refs[2 * n:]
        x, y, c, _ = _place()
        cps = [pltpu.make_async_remote_copy(
            src_ref=a.at[c], dst_ref=o.at[c], send_sem=send_sems.at[k], recv_sem=recv_sems.at[k],
            device_id=(x, y, 1 - c), device_id_type=MESH) for k, (a, o) in enumerate(zip(ins, outs))]
        for cp in cps:
            cp.start()
        for k, (a, o) in enumerate(zip(ins, outs)):
            cps[k].wait_send()
            pltpu.make_async_remote_copy(
                src_ref=a.at[c], dst_ref=o.at[1 - c], send_sem=send_sems.at[k], recv_sem=recv_sems.at[k],
                device_id=(x, y, 1 - c), device_id_type=MESH).wait_recv()

    return pl.pallas_call(
        body, name="share_halves", in_specs=[ANY] * n, out_specs=[ANY] * n,
        out_shape=[jax.ShapeDtypeStruct(a.shape, a.dtype) for a in bufs],
        input_output_aliases={k: k for k in range(n)},
        scratch_shapes=[_dma_sems(n), _dma_sems(n)],
    )(*bufs)


def pair_add(piece, recv, core, name, out_dtype):
    _, _, h, cdim = piece.shape
    tr = _rtile(h, cdim)

    def body(c_ref, a_ref, b_ref, o_ref):
        o_ref[...] = (a_ref[0] + b_ref[...]).astype(o_ref.dtype)

    spec = pl.BlockSpec((1, tr, cdim), lambda p, i, c_ref: (p, i, 0))
    return pl.pallas_call(
        body, name=name,
        grid_spec=pltpu.PrefetchScalarGridSpec(
            num_scalar_prefetch=1, grid=(N_SHARD, h // tr),
            in_specs=[pl.BlockSpec((1, 1, tr, cdim), lambda p, i, c_ref: (p, c_ref[0], i, 0)), spec],
            out_specs=spec),
        out_shape=jax.ShapeDtypeStruct((N_SHARD, h, cdim), out_dtype),
        compiler_params=_cparams(("parallel", "parallel")),
    )(core, piece, recv)


def chip_add(parts, core, name):
    _, h, cdim = parts.shape
    tr = _rtile(h, cdim, 1024 * 1024)

    def body(c_ref, p_ref, o_ref):
        p = [p_ref[s].astype(f32) for s in range(N_SHARD)]
        o_ref[0] = ((p[0] + p[1]) + p[2]) + p[3]

    return pl.pallas_call(
        body, name=name,
        grid_spec=pltpu.PrefetchScalarGridSpec(
            num_scalar_prefetch=1, grid=(h // tr,),
            in_specs=[pl.BlockSpec((N_SHARD, tr, cdim), lambda i, c_ref: (0, i, 0))],
            out_specs=pl.BlockSpec((1, tr, cdim), lambda i, c_ref: (c_ref[0], i, 0))),
        out_shape=jax.ShapeDtypeStruct((2, h, cdim), f32), compiler_params=_cparams(("parallel",)),
    )(core, parts)


def adamw(w, g, m, v, name):
    rows, cdim = w.shape
    tr = _rtile(rows, cdim, 1024 * 1024)

    def body(w_ref, g_ref, m_ref, v_ref, d_ref, nm_ref, nv_ref):
        gv = g_ref[...]
        m_new = ADAM_B1 * m_ref[...] + (1.0 - ADAM_B1) * gv
        v_new = ADAM_B2 * v_ref[...] + (1.0 - ADAM_B2) * jnp.square(gv)
        m_hat = m_new / (1.0 - ADAM_B1 ** ADAM_STEP)
        v_hat = v_new / (1.0 - ADAM_B2 ** ADAM_STEP)
        d_ref[...] = -ADAM_LR * (m_hat / (jnp.sqrt(v_hat) + ADAM_EPS) + ADAM_WD * w_ref[...])
        nm_ref[...] = m_new
        nv_ref[...] = v_new

    spec = pl.BlockSpec((tr, cdim), lambda i: (i, 0))
    return pl.pallas_call(body, name=name, grid=(rows // tr,), in_specs=[spec] * 4, out_specs=[spec] * 3,
                          out_shape=[jax.ShapeDtypeStruct(w.shape, f32)] * 3,
                          compiler_params=_cparams(("parallel",)))(w, g, m, v)


SMALL_MULTIPLE = 16 * LANE


def _pack_flat(parts, multiple=SMALL_MULTIPLE):
    flat = jnp.concatenate([p.reshape(-1) for p in parts])
    pad = (-flat.shape[0]) % multiple
    return jnp.pad(flat, (0, pad)).reshape(-1, LANE)


def _unpack_flat(buf, shapes):
    flat = buf.reshape(-1)
    out, off = [], 0
    for s in shapes:
        cnt = int(np.prod(s))
        out.append(flat[off:off + cnt].reshape(s))
        off += cnt
    return out


def _full_from_shards(name, gathered):
    if name in COL_SHARDED:
        return jnp.concatenate([gathered[s] for s in range(N_SHARD)], axis=1)
    return gathered.reshape(-1, gathered.shape[2])


def _pieces(name, grad):
    if grad.ndim == 3:
        return grad
    if name in COL_SHARDED:
        r, cdim = grad.shape
        return grad.reshape(r, N_SHARD, cdim // N_SHARD).transpose(1, 0, 2)
    return grad.reshape(N_SHARD, grad.shape[0] // N_SHARD, grad.shape[1])


def _small_pieces(name, grad):
    if name in COL_SHARDED or name in ROW_SHARDED:
        return _pieces(name, grad).reshape(N_SHARD, -1)
    return jnp.broadcast_to(grad.reshape(1, -1), (N_SHARD, grad.size))


def kernel(x, positions, l0_w_in, rwkv_mix, rwkv_w0, rwkv_w2, rwkv_a0, rwkv_a2, rwkv_g2, rwkv_k_k, rwkv_k_a, rwkv_r_k, rwkv_ln_g, rwkv_ln_b, ssm_conv_w, ssm_conv_b, ssm_dt_bias, ssm_a_log, ssm_d, ssm_norm_g, l0_w_out, l0_ln1_g, l0_ln1_b, ffn0_w_up, ffn0_conv_w, ffn0_conv_b, ffn0_w_down, l0_ln2_g, l0_ln2_b, l1_w_in, mla_q_norm_g, mla_w_uq, mla_kv_norm_g, mla_w_ukv, l1_w_out, l1_ln1_g, l1_ln1_b, ffn1_w_up, ffn1_conv_w, ffn1_conv_b, ffn1_w_down, l1_ln2_g, l1_ln2_b, loss_target, m_l0_w_in, m_rwkv_mix, m_rwkv_w0, m_rwkv_w2, m_rwkv_a0, m_rwkv_a2, m_rwkv_g2, m_rwkv_k_k, m_rwkv_k_a, m_rwkv_r_k, m_rwkv_ln_g, m_rwkv_ln_b, m_ssm_conv_w, m_ssm_conv_b, m_ssm_dt_bias, m_ssm_a_log, m_ssm_d, m_ssm_norm_g, m_l0_w_out, m_l0_ln1_g, m_l0_ln1_b, m_ffn0_w_up, m_ffn0_conv_w, m_ffn0_conv_b, m_ffn0_w_down, m_l0_ln2_g, m_l0_ln2_b, m_l1_w_in, m_mla_q_norm_g, m_mla_w_uq, m_mla_kv_norm_g, m_mla_w_ukv, m_l1_w_out, m_l1_ln1_g, m_l1_ln1_b, m_ffn1_w_up, m_ffn1_conv_w, m_ffn1_conv_b, m_ffn1_w_down, m_l1_ln2_g, m_l1_ln2_b, v_l0_w_in, v_rwkv_mix, v_rwkv_w0, v_rwkv_w2, v_rwkv_a0, v_rwkv_a2, v_rwkv_g2, v_rwkv_k_k, v_rwkv_k_a, v_rwkv_r_k, v_rwkv_ln_g, v_rwkv_ln_b, v_ssm_conv_w, v_ssm_conv_b, v_ssm_dt_bias, v_ssm_a_log, v_ssm_d, v_ssm_norm_g, v_l0_w_out, v_l0_ln1_g, v_l0_ln1_b, v_ffn0_w_up, v_ffn0_conv_w, v_ffn0_conv_b, v_ffn0_w_down, v_l0_ln2_g, v_l0_ln2_b, v_l1_w_in, v_mla_q_norm_g, v_mla_w_uq, v_mla_kv_norm_g, v_mla_w_ukv, v_l1_w_out, v_l1_ln1_g, v_l1_ln1_b, v_ffn1_w_up, v_ffn1_conv_w, v_ffn1_conv_b, v_ffn1_w_down, v_l1_ln2_g, v_l1_ln2_b):
    args = locals()
    w_loc = {n: args[n] for n in WEIGHTS}
    m_loc = {n: args["m_" + n] for n in WEIGHTS}
    v_loc = {n: args["v_" + n] for n in WEIGHTS}
    core = lax.axis_index("c").astype(jnp.int32).reshape(1)

    small_sharded = [n for n in SMALL if n in COL_SHARDED]
    halves = lambda a: a.reshape(2, a.shape[0] // 2, a.shape[1])
    whole = lambda name, got: _full_from_shards(name, got.reshape(N_SHARD, -1, got.shape[3]))
    first = gather_shards([halves(w_loc['l0_w_in'].astype(bf16)), halves(_pack_flat([w_loc[n] for n in small_sharded]))])
    w_have = {n: w_loc[n] for n in WEIGHTS if n not in BIG}
    w_have['l0_w_in'] = whole('l0_w_in', first[0])
    small_all = first[1].reshape(N_SHARD, -1, LANE)
    per_shard = [_unpack_flat(small_all[s], [w_loc[n].shape for n in small_sharded]) for s in range(N_SHARD)]
    for k, n in enumerate(small_sharded):
        w_have[n] = jnp.concatenate([per_shard[s][k] for s in range(N_SHARD)], axis=1)
    shards_of = lambda names: (names, [halves(w_loc[n].astype(bf16)) for n in names])
    late = (shards_of(['l0_w_out', 'ffn0_w_up', 'ffn0_w_down', 'l1_w_in']),
            shards_of(['l1_w_out', 'ffn1_w_up', 'ffn1_w_down']), whole)

    def pair_sums(names, pieces, tag):
        pieces = [p.reshape(N_SHARD, 2, p.shape[1] // 2, p.shape[2]) for p in pieces]
        return [pair_add(p, r, core, "pair_add_" + n, f32 if n == 'small' else bf16)
                for n, p, r in zip(names, pieces, swap_halves(pieces, "swap_halves_" + tag))]

    loss_part, grad_x, g_full, early = local_step(
        x, positions, loss_target, w_have, late,
        lambda gd: pair_sums(EARLY_GRADS, [_pieces(n, gd[n]) for n in EARLY_GRADS], "early"))
    loss = lax.psum(loss_part[0, 0], AXES)

    small_flat = jnp.concatenate([_small_pieces(n, g_full[n]) for n in SMALL], axis=1)
    pad = (-small_flat.shape[1]) % SMALL_MULTIPLE
    small_pieces = jnp.pad(small_flat, ((0, 0), (0, pad))).reshape(N_SHARD, -1, LANE)
    rest = scatter_to_chips(pair_sums(['l0_w_in', 'small'], [_pieces('l0_w_in', g_full['l0_w_in']), small_pieces],
                                      "rest"))
    from_chips = dict(zip(EARLY_GRADS + ['l0_w_in', 'small'], list(early) + list(rest)))
    units = BIG + ['small']
    both = share_halves([chip_add(from_chips[n], core, "chip_add_" + n) for n in units])
    reduced = [b.reshape(-1, b.shape[2]) for b in both]

    out = {}
    for n, gred in zip(BIG, reduced):
        out[n] = (gred,) + tuple(adamw(w_loc[n], gred, m_loc[n], v_loc[n], "adamw_" + n))
    shapes = [w_loc[n].shape for n in SMALL]
    packs = [_pack_flat([d[n] for n in SMALL]) for d in (w_loc, m_loc, v_loc)]
    small_res = (reduced[-1],) + tuple(adamw(packs[0], reduced[-1], packs[1], packs[2], "adamw_small"))
    small_unpacked = [_unpack_flat(b, shapes) for b in small_res]
    for k, n in enumerate(SMALL):
        out[n] = tuple(u[k] for u in small_unpacked)
    return (loss, grad_x, *[out[n][0] for n in WEIGHTS], *[out[n][1] for n in WEIGHTS],
            *[out[n][2] for n in WEIGHTS], *[out[n][3] for n in WEIGHTS])
```

```python
import functools

import numpy as np
import jax
import jax.numpy as jnp
from jax import lax
from jax.experimental import pallas as pl
from jax.experimental.pallas import tpu as pltpu

f32 = jnp.float32
bf16 = jnp.bfloat16
HI = lax.Precision.HIGHEST
MID = lax.Precision.HIGH

D_MODEL = 1024
HEAD_DIM = 64
N_HEADS = 8
RWKV_COLS = 1792
RWKV_GN_EPS = 64e-5
SSM_STATE = 128
SSM_CHUNK = 128
L0_COLS = 3336
L0_PAD = 3456
L1_COLS = 1952
L1_PAD = 2048
MLA_ROPE = 32
ROPE_THETA = 10000.0
D_FF = 2816
DEPTH = 2
ALPHA = (2 * DEPTH) ** 0.25
ADAM_LR = 0.001
ADAM_B1 = 0.9
ADAM_B2 = 0.999
ADAM_EPS = 1e-08
ADAM_WD = 0.01
ADAM_STEP = 10
RWKV_CHUNK = 64
RWKV_HEADS_PER_STEP = 8
LANE = 128
SUBLANE = 8
VMEM_LIMIT = 56 * 1024 * 1024

WEIGHTS = ['l0_w_in', 'rwkv_mix', 'rwkv_w0', 'rwkv_w2', 'rwkv_a0', 'rwkv_a2', 'rwkv_g2', 'rwkv_k_k', 'rwkv_k_a',
           'rwkv_r_k', 'rwkv_ln_g', 'rwkv_ln_b', 'ssm_conv_w', 'ssm_conv_b', 'ssm_dt_bias', 'ssm_a_log', 'ssm_d',
           'ssm_norm_g', 'l0_w_out', 'l0_ln1_g', 'l0_ln1_b', 'ffn0_w_up', 'ffn0_conv_w', 'ffn0_conv_b',
           'ffn0_w_down', 'l0_ln2_g', 'l0_ln2_b', 'l1_w_in', 'mla_q_norm_g', 'mla_w_uq', 'mla_kv_norm_g',
           'mla_w_ukv', 'l1_w_out', 'l1_ln1_g', 'l1_ln1_b', 'ffn1_w_up', 'ffn1_conv_w', 'ffn1_conv_b',
           'ffn1_w_down', 'l1_ln2_g', 'l1_ln2_b']
COL_SHARDED = ['l0_w_in', 'rwkv_w2', 'rwkv_a2', 'rwkv_g2', 'ssm_conv_w', 'ffn0_w_up', 'ffn0_conv_w', 'l1_w_in',
               'mla_w_uq', 'mla_w_ukv', 'ffn1_w_up', 'ffn1_conv_w']
ROW_SHARDED = ['l0_w_out', 'ffn0_w_down', 'l1_w_out', 'ffn1_w_down']
BIG = ['l0_w_in', 'l0_w_out', 'ffn0_w_up', 'ffn0_w_down', 'l1_w_in', 'l1_w_out', 'ffn1_w_up', 'ffn1_w_down']
SMALL = [n for n in WEIGHTS if n not in BIG]
N_SHARD = 4


def _cparams(sem):
    return pltpu.CompilerParams(dimension_semantics=sem, vmem_limit_bytes=VMEM_LIMIT)


def _dg(a, b, ca, cb, prec=None):
    return lax.dot_general(a, b, (((ca,), (cb,)), ((), ())), precision=prec, preferred_element_type=f32)


def hdot(a, b):
    return _dg(a, b, 1, 0, HI)


def mdot(a, b):
    return _dg(a, b, 1, 0, MID)


def mdot_nt(a, b):
    return _dg(a, b, 1, 1, MID)


def mdot_tn(a, b):
    return _dg(a, b, 0, 0, MID)


def _b(x):
    return x.astype(bf16)


@jax.custom_vjp
def bdot(x, w):
    return _dg(_b(x), _b(w), 1, 0)


def _bdot_fwd(x, w):
    return bdot(x, w), (x, w)


def _bdot_bwd(res, g):
    x, w = res
    return _dg(_b(g), _b(w), 1, 1).astype(x.dtype), _dg(_b(x), _b(g), 0, 0).astype(w.dtype)


bdot.defvjp(_bdot_fwd, _bdot_bwd)


@jax.custom_vjp
def bdot_nt(x, y):
    return _dg(_b(x), _b(y), 1, 1)


def _bdot_nt_fwd(x, y):
    return bdot_nt(x, y), (x, y)


def _bdot_nt_bwd(res, g):
    x, y = res
    return _dg(_b(g), _b(y), 1, 0), _dg(_b(g), _b(x), 0, 0)


bdot_nt.defvjp(_bdot_nt_fwd, _bdot_nt_bwd)


@jax.custom_vjp
def bdot_tn(x, y):
    return _dg(_b(x), _b(y), 0, 0)


def _bdot_tn_fwd(x, y):
    return bdot_tn(x, y), (x, y)


def _bdot_tn_bwd(res, g):
    x, y = res
    return _dg(_b(y), _b(g), 1, 1), _dg(_b(x), _b(g), 1, 0)


bdot_tn.defvjp(_bdot_tn_fwd, _bdot_tn_bwd)


def _sigmoid(x):
    return 1.0 / (1.0 + jnp.exp(-x))


@jax.custom_vjp
def softplus(x):
    e = jnp.exp(-jnp.abs(x))
    u = 1.0 + e
    log1p = jnp.where(u == 1.0, e, jnp.log(u) * e / jnp.where(u == 1.0, 1.0, u - 1.0))
    return jnp.maximum(x, 0.0) + log1p


def _softplus_fwd(x):
    return softplus(x), x


def _softplus_bwd(x, g):
    return (g * _sigmoid(x),)


softplus.defvjp(_softplus_fwd, _softplus_bwd)


@jax.custom_vjp
def softplus_abs(x):
    return jnp.maximum(x, 0.0) + jnp.log(1.0 + jnp.exp(-jnp.abs(x)))


def _softplus_abs_fwd(x):
    return softplus_abs(x), x


softplus_abs.defvjp(_softplus_abs_fwd, _softplus_bwd)


def _two_pass(x, m):
    hi = _b(x)
    lo = _b(x - hi.astype(f32))
    m16 = _b(m)
    return _dg(hi, m16, 1, 0) + _dg(lo, m16, 1, 0)


def _upper(n):
    return (_iota2((n, n), 0) > _iota2((n, n), 1)).astype(f32)


@jax.custom_vjp
def suffix_sums(x):
    return _two_pass(x, _upper(x.shape[1]))


def _suffix_sums_fwd(x):
    return suffix_sums(x), None


def _suffix_sums_bwd(_, g):
    return (_two_pass(g, _upper(g.shape[1]).T),)


suffix_sums.defvjp(_suffix_sums_fwd, _suffix_sums_bwd)


def silu(x):
    return x * _sigmoid(x)


def _shift_rows(x, k, up):
    if k == 0:
        return x
    t = x.shape[0]
    rows = lax.broadcasted_iota(jnp.int32, x.shape, 0)
    if up:
        return jnp.where(rows < t - k, pltpu.roll(x, t - k, 0), 0.0)
    return jnp.where(rows >= k, pltpu.roll(x, k, 0), 0.0)


@functools.partial(jax.custom_vjp, nondiff_argnums=(1,))
def shift_down(x, k):
    return _shift_rows(x, k, False)


def _shift_down_fwd(x, k):
    return _shift_rows(x, k, False), None


def _shift_down_bwd(k, _, g):
    return (_shift_rows(g, k, True),)


shift_down.defvjp(_shift_down_fwd, _shift_down_bwd)


@functools.partial(jax.custom_vjp, nondiff_argnums=(1,))
def lane_roll(x, s):
    return pltpu.roll(x, s % x.shape[1], 1)


def _lane_roll_fwd(x, s):
    return lane_roll(x, s), None


def _lane_roll_bwd(s, _, g):
    return (pltpu.roll(g, (-s) % g.shape[1], 1),)


lane_roll.defvjp(_lane_roll_fwd, _lane_roll_bwd)


def rot_half32(x):
    first = (lax.broadcasted_iota(jnp.int32, x.shape, 1) % MLA_ROPE) < (MLA_ROPE // 2)
    return jnp.where(first, -lane_roll(x, -(MLA_ROPE // 2)), lane_roll(x, MLA_ROPE // 2))


def _iota2(shape, axis):
    return lax.broadcasted_iota(jnp.int32, shape, axis)


class Op:
    def __init__(self, arr, block, imap, diff=True, acc=None, gshape=None, gimap=None, gdtype=f32):
        self.arr, self.block, self.imap, self.diff, self.acc = arr, tuple(block), imap, diff, acc
        self.gshape = tuple(arr.shape) if gshape is None else tuple(gshape)
        self.gimap = imap if gimap is None else gimap
        self.gdtype = gdtype


class Out:
    def __init__(self, shape, block, imap, dtype=f32):
        self.shape, self.block, self.imap, self.dtype = tuple(shape), tuple(block), imap, dtype


def block_fwd(fn, name, grid, ops, outs):
    n_in = len(ops)

    def body(*refs):
        vals = [r[...] for r in refs[:n_in]]
        res = fn(*vals)
        for r, v in zip(refs[n_in:], res):
            r[...] = v.astype(r.dtype)

    res = pl.pallas_call(
        body, name=name, grid=grid,
        in_specs=[pl.BlockSpec(o.block, o.imap) for o in ops],
        out_specs=[pl.BlockSpec(o.block, o.imap) for o in outs],
        out_shape=[jax.ShapeDtypeStruct(o.shape, o.dtype) for o in outs],
        compiler_params=_cparams(("arbitrary", "arbitrary")),
    )(*[o.arr for o in ops])
    return tuple(res)


def block_bwd(fn, name, grid, ops, outs, douts):
    n_in, n_out = len(ops), len(outs)
    dix = [k for k, o in enumerate(ops) if o.diff]

    def body(*refs):
        vals = [r[...] for r in refs[:n_in]]
        dvals = tuple(r[...] for r in refs[n_in:n_in + n_out])
        grefs = refs[n_in + n_out:]

        def f(*d):
            full = list(vals)
            for k, v in zip(dix, d):
                full[k] = v
            return tuple(fn(*full))

        _, vjp = jax.vjp(f, *[vals[k] for k in dix])
        grads = vjp(dvals)
        j, i = pl.program_id(0), pl.program_id(1)
        for k, gref, g in zip(dix, grefs, grads):
            acc = ops[k].acc
            if acc is None:
                gref[...] = g.astype(gref.dtype)
            else:
                first = (i == 0) if acc == 'i' else jnp.logical_and(i == 0, j == 0)

                @pl.when(first)
                def _():
                    gref[...] = g

                @pl.when(jnp.logical_not(first))
                def _():
                    gref[...] += g

    gspecs = [pl.BlockSpec(ops[k].block, ops[k].gimap) for k in dix]
    gshapes = [jax.ShapeDtypeStruct(ops[k].gshape, ops[k].gdtype) for k in dix]
    res = pl.pallas_call(
        body, name=name, grid=grid,
        in_specs=[pl.BlockSpec(o.block, o.imap) for o in ops] + [pl.BlockSpec(o.block, o.imap) for o in outs],
        out_specs=gspecs, out_shape=gshapes,
        compiler_params=_cparams(("arbitrary", "arbitrary")),
    )(*[o.arr for o in ops], *douts)
    return tuple(res)


def _rows(arr, tm, diff=True, gdtype=f32):
    return Op(arr, (tm, arr.shape[1]), lambda j, i: (i, 0), diff=diff, gdtype=gdtype)


def _param(arr, diff=True):
    return Op(arr, arr.shape, lambda j, i: (0,) * arr.ndim, diff=diff, acc='ij')


def _rows_out(n, c, tm, dtype=f32):
    return Out((n, c), (tm, c), lambda j, i: (i, 0), dtype)


def _cols(arr, t, tc, off=0, width=None, gdtype=f32):
    width = arr.shape[1] if width is None else width
    return Op(arr, (t, tc), lambda j, i: (i, j + off), gshape=(arr.shape[0], width), gimap=lambda j, i: (i, j),
              gdtype=gdtype)


def _cparam(arr, tc):
    return Op(arr, (arr.shape[0], tc), lambda j, i: (0, j), acc='i')


def _colblock(arr, tm, off, width, gdtype=f32):
    return Op(arr, (tm, width), lambda j, i: (i, off // width), gshape=(arr.shape[0], width),
              gimap=lambda j, i: (i, 0), gdtype=gdtype)


def _tile(n, cap):
    best = None
    for t in range(LANE, min(n, cap) + 1, LANE):
        if n % t == 0:
            best = t
    return n if best is None else best


def _rtile(rows, cols, cap_bytes=2 * 1024 * 1024):
    best = None
    for t in range(SUBLANE, rows + 1, SUBLANE):
        if rows % t == 0 and t * cols * 4 <= cap_bytes:
            best = t
    return rows if best is None else best


def mm(a, b, name, ta=False, add=None, pieces=None, into=None):
    m = a.shape[1] if ta else a.shape[0]
    kd = a.shape[0] if ta else a.shape[1]
    n = b.shape[1]
    tm, tn = _tile(m, 1408), _tile(n, 1408)
    tk = kd if kd <= 2048 else _tile(kd, 1408)
    nk = kd // tk
    ca = 0 if ta else 1
    n_extra = (add is not None) + (into is not None)

    def body(*refs):
        a_ref, b_ref = refs[:2]
        o_ref, acc = refs[2 + n_extra:]
        k = pl.program_id(2)

        @pl.when(k == 0)
        def _():
            acc[...] = jnp.zeros_like(acc)

        acc[...] += _dg(_b(a_ref[...]), _b(b_ref[...]), ca, 0)

        @pl.when(k == nk - 1)
        def _():
            o_ref[...] = (acc[...] if add is None else acc[...] + refs[2][...]).reshape(o_ref.shape)

    a_spec = pl.BlockSpec((tk, tm), lambda i, j, k: (k, i)) if ta else pl.BlockSpec((tm, tk), lambda i, j, k: (i, k))
    b_spec = pl.BlockSpec((tk, tn), lambda i, j, k: (k, j))
    o_spec = pl.BlockSpec((tm, tn), lambda i, j, k: (i, j))
    out_shape = jax.ShapeDtypeStruct((m, n), f32)
    args, specs, aliases = [a, b], [a_spec, b_spec], {}
    if add is not None:
        args.append(add)
        specs.append(o_spec)
    if pieces is not None:
        count, first, width = pieces
        per = width // tn
        o_spec = pl.BlockSpec((1, tm, tn), lambda i, j, k: (first + j // per, i, j % per))
        out_shape = jax.ShapeDtypeStruct((count, m, width), f32)
        if into is not None:
            aliases = {len(args): 0}
            args.append(into)
            specs.append(pl.BlockSpec(memory_space=pl.ANY))
    return pl.pallas_call(
        body, name=name, grid=(m // tm, n // tn, nk), in_specs=specs, out_specs=o_spec, out_shape=out_shape,
        scratch_shapes=[pltpu.VMEM((tm, tn), f32)], input_output_aliases=aliases,
        compiler_params=_cparams(("parallel", "parallel", "arbitrary")),
    )(*args)


def f_ln(h, y, g, b):
    x = ALPHA * h + y
    mu = jnp.mean(x, axis=-1, keepdims=True)
    xc = x - mu
    var = jnp.mean(xc * xc, axis=-1, keepdims=True)
    return (xc * lax.rsqrt(var + 1e-5) * g + b,)


def f_shift_mix(p, mix):
    return (p + (shift_down(p, 1) - p) * mix,)


def f_rwkv_pre(k, wa_lo, g_lo, w0, w2, a0, a2, g2, k_k, k_a, gh):
    w_lo, a_lo = wa_lo[:, :64], wa_lo[:, 64:]
    log_w = -softplus(-(w0 + bdot(jnp.tanh(w_lo), w2))) - 0.5
    lw = -jnp.exp(log_w)
    a = _sigmoid(a0 + bdot(a_lo, a2))
    g = bdot(_sigmoid(g_lo), g2)
    kk = k * k_k
    kk = kk / jnp.maximum(jnp.sqrt(mdot(kk * kk, gh)), 1e-12)
    k2 = k * (1.0 + (a - 1.0) * k_a)
    return lw, k2, -kk, kk * a, g


def f_rwkv_post(y, r, k2, v, g, ln_g, ln_b, r_k, gh):
    mu = mdot(y, gh) * (1.0 / HEAD_DIM)
    yc = y - mu
    var = mdot(yc * yc, gh) * (1.0 / HEAD_DIM)
    yn = yc * lax.rsqrt(var + RWKV_GN_EPS) * ln_g + ln_b
    bonus = mdot(r * k2 * r_k, gh) * v
    return ((yn + bonus) * g,)


def f_conv4_silu(x, w0, w1, w2, w3, b):
    y = b + shift_down(x, 3) * w0 + shift_down(x, 2) * w1 + shift_down(x, 1) * w2 + x * w3
    return (silu(y),)


def f_ssm_post(y, z, norm_g, gg):
    u = y * silu(z)
    ms = mdot(u * u, gg) * (1.0 / 256.0)
    return (u * lax.rsqrt(ms + 1e-5) * norm_g,)


def f_ffn_act(gate, up, w0, w1, w2, b):
    gc = b + shift_down(gate, 2) * w0 + shift_down(gate, 1) * w1 + gate * w2
    return (silu(gc) * up,)


def _rms(x, g, eps=1e-6):
    return x * lax.rsqrt(jnp.mean(x * x, axis=-1, keepdims=True) + eps) * g


def f_mla_pre(c_q, c_kv, kpe, pos, q_g, w_qn, w_qr, kv_g, w_ukv, inv_q, inv_k):
    qn_in = _rms(c_q, q_g)
    q_nope = bdot(qn_in, w_qn)
    qr = bdot(qn_in, w_qr)
    kv = bdot(_rms(c_kv, kv_g), w_ukv)
    ang_q = pos * inv_q
    ang_k = pos * inv_k
    return (q_nope, qr * jnp.cos(ang_q) + rot_half32(qr) * jnp.sin(ang_q), kv,
            kpe * jnp.cos(ang_k) + rot_half32(kpe) * jnp.sin(ang_k))


def rwkv_chunk(s0, r, lw, k, v, a, b):
    hs = range(len(r))
    l = r[0].shape[0]
    ri, ci = _iota2((l, l), 0), _iota2((l, l), 1)
    strict, incl = ri > ci, ri >= ci
    tri, eye = incl.astype(f32), (ri == ci).astype(f32)
    last = (_iota2((l, 1), 0) == l - 1).astype(f32)
    c = [hdot(tri, lw[h]) for h in hs]
    at = [a[h] * jnp.exp(c[h] - lw[h]) for h in hs]
    wi = [jnp.exp(-c[h]) for h in hs]
    bt = [b[h] * wi[h] for h in hs]
    kt = [k[h] * wi[h] for h in hs]
    rt = [r[h] * jnp.exp(c[h]) for h in hs]
    nab = [jnp.where(strict, mdot_nt(at[h], bt[h]), 0.0) for h in hs]
    nak = [jnp.where(strict, mdot_nt(at[h], kt[h]), 0.0) for h in hs]
    g = [bdot_nt(at[h], s0[h]) + bdot(nak[h], v[h]) for h in hs]
    x = [eye + nab[h] for h in hs]
    p = [mdot(nab[h], nab[h]) for h in hs]
    steps = max(1, (l - 1).bit_length()) - 1
    for it in range(steps):
        x = [x[h] + mdot(p[h], x[h]) for h in hs]
        if it < steps - 1:
            p = [mdot(p[h], p[h]) for h in hs]
    u = [mdot(x[h], g[h]) for h in hs]
    mrb = [jnp.where(incl, mdot_nt(rt[h], bt[h]), 0.0) for h in hs]
    mrk = [jnp.where(incl, mdot_nt(rt[h], kt[h]), 0.0) for h in hs]
    y = [bdot_nt(rt[h], s0[h]) + bdot(mrb[h], u[h]) + bdot(mrk[h], v[h]) for h in hs]
    s1 = [(s0[h] + bdot_tn(u[h], bt[h]) + bdot_tn(v[h], kt[h])) * jnp.exp(jnp.sum(c[h] * last, axis=0, keepdims=True))
          for h in hs]
    return y, s1


def ssd_chunk(xs, bm, cm, dt_raw, s_in, dt_bias, a_log, d_skip, e_heads):
    l = xs.shape[0]
    ri, ci = _iota2((l, l), 0), _iota2((l, l), 1)
    incl = ri >= ci
    tri = incl.astype(f32)
    dt = softplus(dt_raw + dt_bias)
    a128 = dt * (-jnp.exp(a_log))
    lane0 = (_iota2((1, HEAD_DIM), 1) == 0).astype(f32)
    last = (_iota2((l, 1), 0) == l - 1).astype(f32)
    hs = range(N_HEADS)
    group = lambda m, g: m[:, g * SSM_STATE:(g + 1) * SSM_STATE]
    cb = [bdot_nt(group(cm, g), group(bm, g)) for g in range(2)]
    e_all = jnp.concatenate(e_heads, axis=1)
    dt_all = mdot(dt, e_all)
    ac_all = hdot(tri, hdot(a128, e_all))
    xd_all = xs * dt_all
    skip_all = xs * mdot(jnp.broadcast_to(d_skip, (l, LANE)), e_all)
    ac = [ac_all[:, _head(h)] for h in hs]
    xd = [xd_all[:, _head(h)] for h in hs]
    col = [jnp.broadcast_to(jnp.sum(ac[h] * lane0, axis=1, keepdims=True), (l, l)) for h in hs]
    decay = [jnp.exp(jnp.where(incl, col[h] - col[h].T, -1e30)) for h in hs]
    y_diag = [bdot(cb[h // 4] * decay[h], xd[h]) for h in hs]
    a_tot = [jnp.sum(ac[h] * last, axis=0, keepdims=True) for h in hs]
    y_off = [jnp.exp(ac[h]) * bdot(group(cm, h // 4), s_in[h]) for h in hs]
    s_out = [jnp.exp(a_tot[h]) * s_in[h] + bdot_tn(group(bm, h // 4), xd[h] * jnp.exp(a_tot[h] - ac[h])) for h in hs]
    return jnp.concatenate([y_diag[h] + y_off[h] for h in hs], axis=1) + skip_all, s_out


SB_KEYS = LANE
MLA_KEYS = 256


def sb_tile(q, k, v, run, q0, k0, masked=True):
    bq, kb = q.shape[0], k.shape[0]
    z = bdot_nt(q, k) * HEAD_DIM ** -0.5
    if masked:
        strict = (k0 + _iota2((bq, kb), 1)) < (q0 + _iota2((bq, kb), 0))
        lk = jnp.where(strict, -softplus_abs(z), 0.0)
        log_att = z + lk + suffix_sums(lk) + run
        att = jnp.where(strict, jnp.exp(jnp.where(strict, log_att, 0.0)), 0.0)
    else:
        lk = -softplus_abs(z)
        att = jnp.exp(z + lk + suffix_sums(lk) + run)
    return bdot(att, v), jnp.sum(lk, axis=1, keepdims=True)


def mla_scores(qn, qp, kn, kp, q0, k0):
    bq, kb = qn.shape[0], kn.shape[0]
    s = (bdot_nt(qn, kn) + bdot_nt(qp, kp)) * (HEAD_DIM + MLA_ROPE) ** -0.5
    causal = (k0 + _iota2((bq, kb), 1)) <= (q0 + _iota2((bq, kb), 0))
    return jnp.where(causal, s, -1e30), causal


def mla_tile_loss(qn, qp, kn, kp, v, do, lse, dsum, q0, k0):
    s, causal = mla_scores(qn, qp, kn, kp, q0, k0)
    p = jnp.where(causal, jnp.exp(s - lse), 0.0)
    return jnp.sum(do * bdot(p, v)) - jnp.sum(dsum * jnp.sum(p, axis=1, keepdims=True))


def _head(h):
    return slice(h * HEAD_DIM, (h + 1) * HEAD_DIM)


def _rwkv_specs(nc, rev):
    hp = RWKV_HEADS_PER_STEP
    w = hp * HEAD_DIM
    chunk = (lambda c: nc - 1 - c) if rev else (lambda c: c)
    tok = lambda off: pl.BlockSpec((RWKV_CHUNK, w), lambda b, g, c: (b * nc + chunk(c), off // w + g))
    st = pl.BlockSpec((1, hp, HEAD_DIM, HEAD_DIM), lambda b, g, c: ((b * (N_HEADS // hp) + g) * nc + chunk(c), 0, 0, 0))
    return tok, st


def _hosted_call(work, name, grid, in_specs, out_specs, out_shape, scratch, args, ride):
    n_in, n_out, n_scr = len(in_specs), len(out_specs), len(scratch)
    k = 0 if ride is None else len(ride.inputs)

    def body(*refs):
        ins, r_in = refs[:n_in], refs[n_in:n_in + k]
        outs, r_out = refs[n_in + k:n_in + k + n_out], refs[n_in + k + n_out:n_in + 2 * k + n_out]
        scr, r_sems = refs[n_in + 2 * k + n_out:n_in + 2 * k + n_out + n_scr], refs[n_in + 2 * k + n_out + n_scr:]
        ids = [pl.program_id(a) for a in range(len(grid))]
        if ride is not None:
            @pl.when(functools.reduce(jnp.logical_and, [i == 0 for i in ids]))
            def _():
                ride.start(r_in, r_out, r_sems)

        work(ins, outs, scr)
        if ride is not None:
            @pl.when(functools.reduce(jnp.logical_and, [i == g - 1 for i, g in zip(ids, grid)]))
            def _():
                ride.finish(r_in, r_out, r_sems)

    res = pl.pallas_call(
        body, name=name, grid=grid, in_specs=list(in_specs) + [ANY] * k, out_specs=list(out_specs) + [ANY] * k,
        out_shape=list(out_shape) + ([] if ride is None else ride.out_shapes),
        scratch_shapes=list(scratch) + ([] if ride is None else ride.scratch),
        compiler_params=_cparams(("arbitrary",) * len(grid)),
    )(*args, *([] if ride is None else ride.inputs))
    return res[:n_out], res[n_out:]


def rwkv_scan_fwd(ps, lw, k2, na, bb, nb, t, ride=None):
    hp, nc = RWKV_HEADS_PER_STEP, t // RWKV_CHUNK
    ng = N_HEADS // hp
    tok, st = _rwkv_specs(nc, False)

    def work(ins, outs, scr):
        r_ref, v_ref, lw_ref, k_ref, a_ref, b_ref = ins
        y_ref, s0_ref = outs
        (s,) = scr

        @pl.when(pl.program_id(2) == 0)
        def _():
            s[...] = jnp.zeros_like(s)

        s0_ref[0] = s[...]
        heads = lambda ref: [ref[:, _head(h)] for h in range(hp)]
        y, s1 = rwkv_chunk([s[h] for h in range(hp)], heads(r_ref), heads(lw_ref), heads(k_ref), heads(v_ref),
                           heads(a_ref), heads(b_ref))
        for h in range(hp):
            y_ref[:, _head(h)] = y[h]
            s[h] = s1[h]

    return _hosted_call(
        work, "rwkv_scan_fwd", (nb, ng, nc), [tok(0), tok(1024), tok(0), tok(0), tok(0), tok(0)], [tok(0), st],
        [jax.ShapeDtypeStruct((nb * t, N_HEADS * HEAD_DIM), f32),
         jax.ShapeDtypeStruct((nb * ng * nc, hp, HEAD_DIM, HEAD_DIM), f32)],
        [pltpu.VMEM((hp, HEAD_DIM, HEAD_DIM), f32)], (ps, ps, lw, k2, na, bb), ride)


def rwkv_scan_bwd(s0, ps, lw, k2, na, bb, dy, nb, t, ride=None):
    hp, nc = RWKV_HEADS_PER_STEP, t // RWKV_CHUNK
    ng = N_HEADS // hp
    tok, st = _rwkv_specs(nc, True)

    def work(ins, outs, scr):
        s0_ref, r_ref, v_ref, lw_ref, k_ref, a_ref, b_ref, dy_ref = ins
        (ds,) = scr

        @pl.when(pl.program_id(2) == 0)
        def _():
            ds[...] = jnp.zeros_like(ds)

        heads = lambda ref: [ref[:, _head(h)] for h in range(hp)]
        _, vjp = jax.vjp(rwkv_chunk, [s0_ref[0, h] for h in range(hp)], heads(r_ref), heads(lw_ref), heads(k_ref),
                         heads(v_ref), heads(a_ref), heads(b_ref))
        g = vjp((heads(dy_ref), [ds[h] for h in range(hp)]))
        for h in range(hp):
            ds[h] = g[0][h]
            for ref, val in zip(outs, g[1:]):
                ref[:, _head(h)] = val[h]

    return _hosted_call(
        work, "rwkv_scan_bwd", (nb, ng, nc), [st, tok(0), tok(1024), tok(0), tok(0), tok(0), tok(0), tok(0)],
        [tok(0)] * 6, [jax.ShapeDtypeStruct((nb * t, N_HEADS * HEAD_DIM), f32)] * 6,
        [pltpu.VMEM((hp, HEAD_DIM, HEAD_DIM), f32)], (s0, ps, ps, lw, k2, na, bb, dy), ride)


def _ssd_specs(nb, nch, rev):
    def row(b, c):
        return b * nch + (nch - 1 - c if rev else c)

    l = SSM_CHUNK
    xs = pl.BlockSpec((l, 512), lambda b, c: (row(b, c), 0))
    bm = pl.BlockSpec((l, 256), lambda b, c: (row(b, c), 2))
    cm = pl.BlockSpec((l, 256), lambda b, c: (row(b, c), 3))
    dt = pl.BlockSpec((l, LANE), lambda b, c: (row(b, c), (L0_PAD - LANE) // LANE))
    st = pl.BlockSpec((1, 1, N_HEADS, SSM_STATE, HEAD_DIM), lambda b, c: (b, (nch - 1 - c if rev else c), 0, 0, 0))
    par = pl.BlockSpec((1, LANE), lambda b, c: (0, 0))
    eh = pl.BlockSpec((N_HEADS, LANE, HEAD_DIM), lambda b, c: (0, 0, 0))
    return xs, bm, cm, dt, st, par, eh, row


def ssd_fwd(xbc_act, proj0, dt_bias, a_log, d_skip, e_heads, nb, t):
    nch = t // SSM_CHUNK
    n_tok = nb * t
    xs, bm, cm, dt, st, par, eh, row = _ssd_specs(nb, nch, False)

    def body(x_ref, b_ref, c_ref, dt_ref, db_ref, al_ref, dsk_ref, e_ref, y_ref, st_ref, s):
        @pl.when(pl.program_id(1) == 0)
        def _():
            s[...] = jnp.zeros_like(s)

        st_ref[0, 0] = s[...]
        y, s_out = ssd_chunk(x_ref[...], b_ref[...], c_ref[...], dt_ref[...], [s[h] for h in range(N_HEADS)],
                             db_ref[...], al_ref[...], dsk_ref[...], [e_ref[h] for h in range(N_HEADS)])
        y_ref[...] = y
        for h in range(N_HEADS):
            s[h] = s_out[h]

    return pl.pallas_call(
        body, name="ssd_fwd", grid=(nb, nch), in_specs=[xs, bm, cm, dt, par, par, par, eh],
        out_specs=[pl.BlockSpec((SSM_CHUNK, 512), lambda b, c: (row(b, c), 0)), st],
        out_shape=[jax.ShapeDtypeStruct((n_tok, 512), f32),
                   jax.ShapeDtypeStruct((nb, nch, N_HEADS, SSM_STATE, HEAD_DIM), f32)],
        scratch_shapes=[pltpu.VMEM((N_HEADS, SSM_STATE, HEAD_DIM), f32)],
        compiler_params=_cparams(("arbitrary", "arbitrary")),
    )(xbc_act, xbc_act, xbc_act, proj0, dt_bias, a_log, d_skip, e_heads)


def ssd_bwd(xbc_act, proj0, dt_bias, a_log, d_skip, e_heads, states, dy, nb, t):
    nch = t // SSM_CHUNK
    n_tok = nb * t
    xs, bm, cm, dt, st, par, eh, row = _ssd_specs(nb, nch, True)

    def body(x_ref, b_ref, c_ref, dt_ref, db_ref, al_ref, dsk_ref, e_ref, st_ref, dy_ref,
             dx_ref, dbm_ref, dcm_ref, ddt_ref, ddb_ref, dal_ref, ddsk_ref, ds):
        first = jnp.logical_and(pl.program_id(0) == 0, pl.program_id(1) == 0)

        @pl.when(pl.program_id(1) == 0)
        def _():
            ds[...] = jnp.zeros_like(ds)

        e_list = [e_ref[h] for h in range(N_HEADS)]

        def f(x, bmv, cmv, dtr, s_in, dbv, alv, dskv):
            return ssd_chunk(x, bmv, cmv, dtr, s_in, dbv, alv, dskv, e_list)

        _, vjp = jax.vjp(f, x_ref[...], b_ref[...], c_ref[...], dt_ref[...],
                         [st_ref[0, 0, h] for h in range(N_HEADS)], db_ref[...], al_ref[...], dsk_ref[...])
        g = vjp((dy_ref[...], [ds[h] for h in range(N_HEADS)]))
        dx_ref[...], dbm_ref[...], dcm_ref[...], ddt_ref[...] = g[0], g[1], g[2], g[3].astype(bf16)
        for h in range(N_HEADS):
            ds[h] = g[4][h]
        for ref, val in zip((ddb_ref, dal_ref, ddsk_ref), g[5:]):
            @pl.when(first)
            def _():
                ref[...] = val

            @pl.when(jnp.logical_not(first))
            def _():
                ref[...] += val

    rows_spec = lambda w: pl.BlockSpec((SSM_CHUNK, w), lambda b, c: (row(b, c), 0))
    return pl.pallas_call(
        body, name="ssd_bwd", grid=(nb, nch),
        in_specs=[xs, bm, cm, dt, par, par, par, eh, st, rows_spec(512)],
        out_specs=[rows_spec(512), rows_spec(256), rows_spec(256), rows_spec(LANE), par, par, par],
        out_shape=[jax.ShapeDtypeStruct((n_tok, 512), f32), jax.ShapeDtypeStruct((n_tok, 256), f32),
                   jax.ShapeDtypeStruct((n_tok, 256), f32), jax.ShapeDtypeStruct((n_tok, LANE), bf16)]
        + [jax.ShapeDtypeStruct((1, LANE), f32)] * 3,
        scratch_shapes=[pltpu.VMEM((N_HEADS, SSM_STATE, HEAD_DIM), f32)],
        compiler_params=_cparams(("arbitrary", "arbitrary")),
    )(xbc_act, xbc_act, xbc_act, proj0, dt_bias, a_log, d_skip, e_heads, states, dy)


ATT_BQ = 256
SB_BQ = 512
SB_TILES_PER_PASS = 2
SB_HEADS_PER_STEP = 2
MLA_HEADS_PER_STEP = 4


def _loop_tiles(n_tiles, per_pass, fn, init):
    def several(i, carry):
        for r in range(per_pass):
            carry = fn(per_pass * i + r, carry)
        return carry

    return lax.fori_loop(0, n_tiles // per_pass, several, init)


def _sb_specs(t, bq, nq):
    w = SB_HEADS_PER_STEP * HEAD_DIM
    qs = lambda off: pl.BlockSpec((bq, w), lambda b, g, i: (b * nq + i, off // w + g))
    ks = lambda off: pl.BlockSpec((t, w), lambda b, g, i: (b, off // w + g))
    return qs, ks


def _sb_mass_spec(bq, nq):
    return pl.BlockSpec((bq, SB_HEADS_PER_STEP * LANE), lambda b, g, i: (b * nq + i, g))


def sb_fwd(proj1, nb, t, ride=None):
    bq = min(SB_BQ, t)
    nq = t // bq
    qs, ks = _sb_specs(t, bq, nq)

    def work(ins, outs, _):
        q_ref, k_ref, v_ref = ins
        o_ref, mass_ref = outs
        q0 = pl.program_id(2) * bq
        n_tiles = (q0 + bq) // SB_KEYS
        hs = range(SB_HEADS_PER_STEP)
        q = [q_ref[:, _head(h)] for h in hs]
        lanes = _iota2((1, LANE), 1)

        def step(i, carry, masked):
            j = n_tiles - 1 - i
            k0 = pl.multiple_of(j * SB_KEYS, SB_KEYS)
            out = []
            for h in hs:
                o, run, kept = carry[h]
                o_t, mass = sb_tile(q[h], k_ref[pl.ds(k0, SB_KEYS), _head(h)], v_ref[pl.ds(k0, SB_KEYS), _head(h)],
                                    run, q0, k0, masked)
                out.append((o + o_t, run + mass, kept + mass * (lanes == j).astype(f32)))
            return out

        diag = bq // SB_KEYS
        res = _loop_tiles(diag, SB_TILES_PER_PASS, functools.partial(step, masked=True),
                          [(jnp.zeros((bq, HEAD_DIM), f32), jnp.zeros((bq, 1), f32), jnp.zeros((bq, LANE), f32))
                           for _ in hs])
        res = _loop_tiles(n_tiles - diag, SB_TILES_PER_PASS, lambda i, cr: step(i + diag, cr, False), res)
        for h in hs:
            o_ref[:, _head(h)] = res[h][0].astype(bf16)
            mass_ref[:, h * LANE:(h + 1) * LANE] = res[h][2]

    return _hosted_call(
        work, "sb_fwd", (nb, N_HEADS // SB_HEADS_PER_STEP, nq), [qs(0), ks(512), ks(1024)],
        [qs(0), _sb_mass_spec(bq, nq)],
        [jax.ShapeDtypeStruct((nb * t, 512), bf16), jax.ShapeDtypeStruct((nb * t, N_HEADS * LANE), f32)],
        [], (proj1, proj1, proj1), ride)


def sb_bwd(proj1, masses, do, nb, t):
    bq = min(SB_BQ, t)
    nq = t // bq
    qs, ks = _sb_specs(t, bq, nq)

    def body(q_ref, k_ref, v_ref, mass_ref, do_ref, dq_ref, dk_ref, dv_ref):
        @pl.when(pl.program_id(2) == 0)
        def _():
            dk_ref[...] = jnp.zeros_like(dk_ref)
            dv_ref[...] = jnp.zeros_like(dv_ref)

        q0 = pl.program_id(2) * bq
        n_tiles = (q0 + bq) // SB_KEYS
        hs = range(SB_HEADS_PER_STEP)
        q = [q_ref[:, _head(h)] for h in hs]
        do = [do_ref[:, _head(h)].astype(f32) for h in hs]
        col0 = jnp.zeros((bq, 1), f32)
        lanes = _iota2((1, LANE), 1)
        run_all = [hdot(mass_ref[:, h * LANE:(h + 1) * LANE], _upper(LANE)) for h in hs]

        def tile(ref, k0, h):
            return ref[pl.ds(k0, SB_KEYS), _head(h)]

        def grads(j, carry, masked):
            k0 = pl.multiple_of(j * SB_KEYS, SB_KEYS)
            pick = (lanes == j).astype(f32)
            out = []
            for h in hs:
                dq, c = carry[h]
                run_in = jnp.sum(run_all[h] * pick, axis=1, keepdims=True)
                _, vjp = jax.vjp(lambda a, b, d, r: sb_tile(a, b, d, r, q0, k0, masked),
                                 q[h], tile(k_ref, k0, h), tile(v_ref, k0, h), run_in)
                dq_t, dk_t, dv_t, drun = vjp((do[h], c))
                dk_ref[pl.ds(k0, SB_KEYS), _head(h)] += dk_t
                dv_ref[pl.ds(k0, SB_KEYS), _head(h)] += dv_t
                out.append((dq + dq_t, drun + c))
            return out

        clear = n_tiles - bq // SB_KEYS
        res = _loop_tiles(clear, SB_TILES_PER_PASS, functools.partial(grads, masked=False),
                          [(jnp.zeros((bq, HEAD_DIM), f32), col0) for _ in hs])
        res = _loop_tiles(bq // SB_KEYS, SB_TILES_PER_PASS, lambda i, cr: grads(i + clear, cr, True), res)
        for h in hs:
            dq_ref[:, _head(h)] = res[h][0]

    return pl.pallas_call(
        body, name="sb_bwd", grid=(nb, N_HEADS // SB_HEADS_PER_STEP, nq),
        in_specs=[qs(0), ks(512), ks(1024), _sb_mass_spec(bq, nq), qs(0)], out_specs=[qs(0), ks(0), ks(0)],
        out_shape=[jax.ShapeDtypeStruct((nb * t, 512), f32)] * 3,
        compiler_params=_cparams(("parallel", "parallel", "arbitrary")),
    )(proj1, proj1, proj1, masses, do)


def _mla_specs(t, bq, nq):
    hp = MLA_HEADS_PER_STEP
    qn = pl.BlockSpec((bq, hp * HEAD_DIM), lambda b, g, i: (b * nq + i, g))
    qr = pl.BlockSpec((bq, hp * MLA_ROPE), lambda b, g, i: (b * nq + i, g))
    kv = pl.BlockSpec((t, hp * 2 * HEAD_DIM), lambda b, g, i: (b, g))
    kp = pl.BlockSpec((t, LANE), lambda b, g, i: (b, 0))
    return qn, qr, kv, kp


def _mla_softmax_pass(qn, qp, kv_ref, kp_ref, q0, n_tiles, bq):
    hs = range(MLA_HEADS_PER_STEP)

    def step(j, carry):
        k0 = pl.multiple_of(j * MLA_KEYS, MLA_KEYS)
        kp = kp_ref[pl.ds(k0, MLA_KEYS), :MLA_ROPE]
        out = []
        for h in hs:
            m, l, acc = carry[h]
            s, _ = mla_scores(qn[h], qp[h], kv_ref[pl.ds(k0, MLA_KEYS), _head(2 * h)], kp, q0, k0)
            m_new = jnp.maximum(m, jnp.max(s, axis=1, keepdims=True))
            alpha, p = jnp.exp(m - m_new), jnp.exp(s - m_new)
            out.append((m_new, alpha * l + jnp.sum(p, axis=1, keepdims=True),
                        alpha * acc + bdot(p, kv_ref[pl.ds(k0, MLA_KEYS), _head(2 * h + 1)])))
        return out

    init = [(jnp.full((bq, 1), -1e30, f32), jnp.zeros((bq, 1), f32), jnp.zeros((bq, HEAD_DIM), f32)) for _ in hs]
    return lax.fori_loop(0, n_tiles, step, init)


def mla_fwd(q_nope, qr, kv, kpe, nb, t):
    bq = min(ATT_BQ, t)
    nq = t // bq
    sqn, sqr, skv, skp = _mla_specs(t, bq, nq)

    def body(qn_ref, qr_ref, kv_ref, kp_ref, o_ref, o32_ref, lse_ref):
        q0 = pl.program_id(2) * bq
        hs = range(MLA_HEADS_PER_STEP)
        qn = [qn_ref[:, _head(h)] for h in hs]
        qp = [qr_ref[:, h * MLA_ROPE:(h + 1) * MLA_ROPE] for h in hs]
        res = _mla_softmax_pass(qn, qp, kv_ref, kp_ref, q0, (q0 + bq) // MLA_KEYS, bq)
        for h in hs:
            m, l, acc = res[h]
            o = acc / l
            o_ref[:, _head(h)] = o.astype(bf16)
            o32_ref[:, _head(h)] = o
            lse_ref[:, _head(h)] = jnp.broadcast_to(m + jnp.log(l), (bq, HEAD_DIM))

    n = nb * t
    return pl.pallas_call(
        body, name="mla_fwd", grid=(nb, N_HEADS // MLA_HEADS_PER_STEP, nq), in_specs=[sqn, sqr, skv, skp],
        out_specs=[sqn, sqn, sqn],
        out_shape=[jax.ShapeDtypeStruct((n, 512), bf16), jax.ShapeDtypeStruct((n, 512), f32),
                   jax.ShapeDtypeStruct((n, 512), f32)],
        compiler_params=_cparams(("parallel", "arbitrary", "arbitrary")),
    )(q_nope, qr, kv, kpe)


def mla_bwd(q_nope, qr, kv, kpe, o32, lse_b, do, nb, t):
    bq = min(ATT_BQ, t)
    nq = t // bq
    sqn, sqr, skv, skp = _mla_specs(t, bq, nq)

    def body(qn_ref, qr_ref, kv_ref, kp_ref, o_ref, lse_ref, do_ref, dqn_ref, dqr_ref, dkv_ref, dkp_ref):
        first_q = pl.program_id(2) == 0

        @pl.when(first_q)
        def _():
            dkv_ref[...] = jnp.zeros_like(dkv_ref)

        @pl.when(jnp.logical_and(first_q, pl.program_id(1) == 0))
        def _():
            dkp_ref[...] = jnp.zeros_like(dkp_ref)

        q0 = pl.program_id(2) * bq
        n_tiles = (q0 + bq) // MLA_KEYS
        hs = range(MLA_HEADS_PER_STEP)
        qn = [qn_ref[:, _head(h)] for h in hs]
        qp = [qr_ref[:, h * MLA_ROPE:(h + 1) * MLA_ROPE] for h in hs]
        do = [do_ref[:, _head(h)].astype(f32) for h in hs]
        lse = [lse_ref[:, h * HEAD_DIM:h * HEAD_DIM + 1] for h in hs]
        dsum = [jnp.sum(do[h] * o_ref[:, _head(h)], axis=1, keepdims=True) for h in hs]

        def grads(j, carry):
            k0 = pl.multiple_of(j * MLA_KEYS, MLA_KEYS)
            rows = pl.ds(k0, MLA_KEYS)
            kp = kp_ref[rows, :MLA_ROPE]
            out = []
            for h in hs:
                dqn, dqp = carry[h]
                g = jax.grad(mla_tile_loss, argnums=(0, 1, 2, 3, 4))(
                    qn[h], qp[h], kv_ref[rows, _head(2 * h)], kp, kv_ref[rows, _head(2 * h + 1)],
                    do[h], lse[h], dsum[h], q0, k0)
                dkv_ref[rows, _head(2 * h)] += g[2]
                dkp_ref[rows, :MLA_ROPE] += g[3]
                dkv_ref[rows, _head(2 * h + 1)] += g[4]
                out.append((dqn + g[0], dqp + g[1]))
            return out

        res = lax.fori_loop(0, n_tiles, grads,
                            [(jnp.zeros((bq, HEAD_DIM), f32), jnp.zeros((bq, MLA_ROPE), f32)) for _ in hs])
        for h in hs:
            dqn_ref[:, _head(h)] = res[h][0]
            dqr_ref[:, h * MLA_ROPE:(h + 1) * MLA_ROPE] = res[h][1]

    n = nb * t
    return pl.pallas_call(
        body, name="mla_bwd", grid=(nb, N_HEADS // MLA_HEADS_PER_STEP, nq),
        in_specs=[sqn, sqr, skv, skp, sqn, sqn, sqn], out_specs=[sqn, sqr, skv, skp],
        out_shape=[jax.ShapeDtypeStruct((n, 512), f32), jax.ShapeDtypeStruct((n, N_HEADS * MLA_ROPE), f32),
                   jax.ShapeDtypeStruct((n, 1024), f32), jax.ShapeDtypeStruct((n, LANE), f32)],
        compiler_params=_cparams(("arbitrary", "arbitrary", "arbitrary")),
    )(q_nope, qr, kv, kpe, o32, lse_b, do)


def loss_head(h, target):
    n, d = h.shape
    tm = _tile(n, 512)

    def body(h_ref, t_ref, l_ref, dh_ref):
        diff = h_ref[...] - t_ref[...]
        dh_ref[...] = diff * (1.0 / d)
        part = 0.5 * jnp.sum(jnp.sum(diff * diff, axis=1, keepdims=True) * (1.0 / d), axis=0, keepdims=True)

        @pl.when(pl.program_id(0) == 0)
        def _():
            l_ref[...] = jnp.zeros_like(l_ref)

        l_ref[...] += jnp.broadcast_to(part, l_ref.shape)

    spec = pl.BlockSpec((tm, d), lambda i: (i, 0))
    return pl.pallas_call(
        body, name="loss_head", grid=(n // tm,), in_specs=[spec, spec],
        out_specs=[pl.BlockSpec((8, LANE), lambda i: (0, 0)), spec],
        out_shape=[jax.ShapeDtypeStruct((8, LANE), f32), jax.ShapeDtypeStruct((n, d), f32)],
        compiler_params=_cparams(("arbitrary",)),
    )(h, target)


def _row(v):
    return v.reshape(1, -1)


def _pad_cols(a, n):
    return jnp.pad(a, ((0, 0), (0, n - a.shape[1])))


def _pad_row(v, n=LANE):
    return jnp.pad(v.reshape(1, -1), ((0, 0), (0, n - v.shape[0])))


def _group_matrix(width, group):
    idx = np.arange(width) // group
    return jnp.asarray((idx[:, None] == idx[None, :]).astype(np.float32))


def _head_expand():
    e = np.zeros((N_HEADS, LANE, HEAD_DIM), np.float32)
    for h in range(N_HEADS):
        e[h, h, :] = 1.0
    return jnp.asarray(e)


def _rope_freqs():
    inv = 1.0 / (ROPE_THETA ** (np.arange(0, MLA_ROPE, 2, dtype=np.float32) / MLA_ROPE))
    inv = np.tile(inv.astype(np.float32), 2)
    inv_q = np.tile(inv, N_HEADS).reshape(1, N_HEADS * MLA_ROPE)
    inv_k = np.zeros((1, LANE), np.float32)
    inv_k[0, :MLA_ROPE] = inv
    return jnp.asarray(inv_q), jnp.asarray(inv_k)


def _uq_split(w):
    w3 = w.reshape(w.shape[0], N_HEADS, HEAD_DIM + MLA_ROPE)
    return w3[:, :, :HEAD_DIM].reshape(-1, 512), w3[:, :, HEAD_DIM:].reshape(-1, N_HEADS * MLA_ROPE)


def _uq_merge(gn, gr):
    r = gn.shape[0]
    return jnp.concatenate([gn.reshape(r, N_HEADS, HEAD_DIM), gr.reshape(r, N_HEADS, MLA_ROPE)], axis=2).reshape(r, 768)


EARLY_GRADS = ['ffn1_w_up', 'ffn1_w_down', 'l1_w_in', 'l1_w_out', 'ffn0_w_up', 'ffn0_w_down', 'l0_w_out']


def local_step(x, positions, target, w, late_weights=None, scatter_early=None):
    w = dict(w)
    nb, t, d = x.shape
    n = nb * t
    tm = 256
    ni = n // tm
    tc = 2 * LANE
    h0 = x.reshape(n, d)
    tgt = target.reshape(n, d)
    pos = positions.reshape(n, 1).astype(f32)
    gh = _group_matrix(512, HEAD_DIM)
    gg = _group_matrix(512, 256)
    e_heads = _head_expand()
    inv_q, inv_k = _rope_freqs()
    g = {}

    def ln_stage(h, y, gname, bname):
        ops = [_rows(h, tm), _rows(y, tm, gdtype=bf16), _param(_row(w[gname])), _param(_row(w[bname]))]
        return ops, [_rows_out(n, d, tm)]

    def ln_fwd(name, ops):
        return block_fwd(lambda *a: f_ln(*a) * 2, name, (1, ni), ops, [_rows_out(n, d, tm), _rows_out(n, d, tm, bf16)])

    def ffn_act_stage(u, cw, cb):
        nj = D_FF // tc
        ops = [_cols(u, t, tc, 0, D_FF, bf16), _cols(u, t, tc, nj, D_FF, bf16)] \
            + [_cparam(cw[i:i + 1], tc) for i in range(3)] + [_cparam(_row(cb), tc)]
        return ops, [Out((n, D_FF), (t, tc), lambda j, i: (i, j), bf16)], (nj, nb)

    w_in0 = _pad_cols(w['l0_w_in'], L0_PAD)
    h0b = h0.astype(bf16)
    proj0 = mm(h0b, w_in0, "l0_proj")

    shift_ops = [_cols(proj0, t, tc, 0, RWKV_COLS, bf16), _cparam(_row(w['rwkv_mix']), tc)]
    shift_outs = [Out((n, RWKV_COLS), (t, tc), lambda j, i: (i, j))]
    shift_grid = (RWKV_COLS // tc, nb)
    (ps,) = block_fwd(f_shift_mix, "rwkv_shift", shift_grid, shift_ops, shift_outs)

    pre_ops = [_colblock(ps, tm, 512, 512), _colblock(ps, tm, 1536, 128), _colblock(ps, tm, 1664, 128),
               _param(_row(w['rwkv_w0'])), _param(w['rwkv_w2']), _param(_row(w['rwkv_a0'])), _param(w['rwkv_a2']),
               _param(w['rwkv_g2']), _param(_row(w['rwkv_k_k'])), _param(_row(w['rwkv_k_a'])), _param(gh, diff=False)]
    pre_outs = [_rows_out(n, 512, tm) for _ in range(5)]
    lw, k2, na, bb, gate_r = block_fwd(f_rwkv_pre, "rwkv_pre", (1, ni), pre_ops, pre_outs)
    def arrived(group, gathered):
        if late_weights is not None:
            for name, got in zip(late_weights[group][0], gathered):
                w[name] = late_weights[2](name, got)

    ride = None if late_weights is None else GatherRide(late_weights[0][1])
    (y_tok, s0_saved), gathered = rwkv_scan_fwd(ps, lw, k2, na, bb, nb, t, ride)
    arrived(0, gathered)
    w_out0 = w['l0_w_out']

    post_ops = [_rows(y_tok, tm), _colblock(ps, tm, 0, 512), _rows(k2, tm), _colblock(ps, tm, 1024, 512),
                _rows(gate_r, tm), _param(_row(w['rwkv_ln_g'])), _param(_row(w['rwkv_ln_b'])),
                _param(w['rwkv_r_k'].reshape(1, 512)), _param(gh, diff=False)]
    post_outs = [_rows_out(n, 512, tm, bf16)]
    (y_a,) = block_fwd(f_rwkv_post, "rwkv_post", (1, ni), post_ops, post_outs)

    xbc_off = (RWKV_COLS + 512) // tc
    conv_ops = [_cols(proj0, t, tc, xbc_off, 1024, bf16)] + [_cparam(w['ssm_conv_w'][i:i + 1], tc) for i in range(4)] \
        + [_cparam(_row(w['ssm_conv_b']), tc)]
    conv_outs = [Out((n, 1024), (t, tc), lambda j, i: (i, j))]
    conv_grid = (1024 // tc, nb)
    (xbc_act,) = block_fwd(f_conv4_silu, "ssm_conv", conv_grid, conv_ops, conv_outs)

    dt_bias, a_log, d_skip = _pad_row(w['ssm_dt_bias']), _pad_row(w['ssm_a_log']), _pad_row(w['ssm_d'])
    y_ssd, ssd_states = ssd_fwd(xbc_act, proj0, dt_bias, a_log, d_skip, e_heads, nb, t)

    z_tok = proj0[:, RWKV_COLS:RWKV_COLS + 512]
    spost_ops = [_rows(y_ssd, tm), _rows(z_tok, tm, gdtype=bf16), _param(_row(w['ssm_norm_g'])), _param(gg, diff=False)]
    spost_outs = [_rows_out(n, 512, tm, bf16)]
    (y_b,) = block_fwd(f_ssm_post, "ssm_post", (1, ni), spost_ops, spost_outs)

    mixed0 = mm(y_b, w_out0[512:], "l0_out_b", add=mm(y_a, w_out0[:512], "l0_out_a"))
    ln1_ops, ln_outs = ln_stage(h0, mixed0, 'l0_ln1_g', 'l0_ln1_b')
    h1, h1b = ln_fwd("l0_ln1", ln1_ops)

    u0 = mm(h1b, w['ffn0_w_up'], "ffn0_up")
    act0_ops, act_outs, act_grid = ffn_act_stage(u0, w['ffn0_conv_w'], w['ffn0_conv_b'])
    (act0,) = block_fwd(f_ffn_act, "ffn0_act", act_grid, act0_ops, act_outs)
    f0 = mm(act0, w['ffn0_w_down'], "ffn0_down")
    ln2_ops, _ = ln_stage(h1, f0, 'l0_ln2_g', 'l0_ln2_b')
    h2, h2b = ln_fwd("l0_ln2", ln2_ops)

    w_in1 = _pad_cols(w['l1_w_in'], L1_PAD)
    proj1 = mm(h2b, w_in1, "l1_proj")
    w_qn, w_qr = _uq_split(w['mla_w_uq'])
    mpre_ops = [_colblock(proj1, tm, 1536, 256, bf16), _colblock(proj1, tm, 1792, 128, bf16),
                _colblock(proj1, tm, 1920, 128, bf16),
                Op(pos, (tm, 1), lambda j, i: (i, 0), diff=False),
                _param(_row(w['mla_q_norm_g'])), _param(w_qn), _param(w_qr),
                _param(_row(w['mla_kv_norm_g'])), _param(w['mla_w_ukv']), _param(inv_q, diff=False),
                _param(inv_k, diff=False)]
    mpre_outs = [_rows_out(n, 512, tm), _rows_out(n, N_HEADS * MLA_ROPE, tm), _rows_out(n, 1024, tm),
                 _rows_out(n, LANE, tm)]
    q_nope, q_rope, kv, kpe = block_fwd(f_mla_pre, "mla_pre", (1, ni), mpre_ops, mpre_outs)
    ride = None if late_weights is None else GatherRide(late_weights[1][1])
    (o_sb, sb_masses), gathered = sb_fwd(proj1, nb, t, ride)
    arrived(1, gathered)
    w_out1 = w['l1_w_out']
    o_mla, o_mla32, mla_lse = mla_fwd(q_nope, q_rope, kv, kpe, nb, t)

    mixed1 = mm(o_mla, w_out1[512:], "l1_out_b", add=mm(o_sb, w_out1[:512], "l1_out_a"))
    ln3_ops, _ = ln_stage(h2, mixed1, 'l1_ln1_g', 'l1_ln1_b')
    h3, h3b = ln_fwd("l1_ln1", ln3_ops)
    u1 = mm(h3b, w['ffn1_w_up'], "ffn1_up")
    act1_ops, _, _ = ffn_act_stage(u1, w['ffn1_conv_w'], w['ffn1_conv_b'])
    (act1,) = block_fwd(f_ffn_act, "ffn1_act", act_grid, act1_ops, act_outs)
    f1 = mm(act1, w['ffn1_w_down'], "ffn1_down")
    ln4_ops, _ = ln_stage(h3, f1, 'l1_ln2_g', 'l1_ln2_b')
    (h4,) = block_fwd(f_ln, "l1_ln2", (1, ni), ln4_ops, ln_outs)

    loss_part, dh4 = loss_head(h4, tgt)

    def vec(a_):
        return a_.reshape(-1)

    def ffn_bwd(tag, dh_out, ln_ops, act_ops, h_in, act, w_up, w_down, names):
        dh_res, df, gg_, gb_ = block_bwd(f_ln, tag + "_ln2_bwd", (1, ni), ln_ops, ln_outs, [dh_out])
        g[names[4]], g[names[5]] = vec(gg_), vec(gb_)
        g[names[3]] = mm(act, df, tag + "_down_dw", ta=True)
        dact = mm(df, w_down.T, tag + "_down_dx")
        dgate, dup, dw0, dw1, dw2, dcb = block_bwd(f_ffn_act, tag + "_act_bwd", act_grid, act_ops, act_outs, [dact])
        g[names[1]] = jnp.concatenate([dw0, dw1, dw2], axis=0)
        g[names[2]] = vec(dcb)
        quarter = 2 * D_FF // N_SHARD
        g[names[0]] = mm(h_in, dup, tag + "_upv_dw", ta=True, pieces=(N_SHARD, 2, quarter),
                         into=mm(h_in, dgate, tag + "_gate_dw", ta=True, pieces=(N_SHARD, 0, quarter)))
        w_up_t = w_up.T
        dh = mm(dgate, w_up_t[:D_FF], tag + "_gate_dx", add=dh_res)
        return mm(dup, w_up_t[D_FF:], tag + "_upv_dx", add=dh)

    def out_bwd(tag, dmixed, y_first, y_second, w_out, name):
        g[name] = jnp.concatenate([mm(y_first, dmixed, tag + "_a_dw", ta=True),
                                   mm(y_second, dmixed, tag + "_b_dw", ta=True)], axis=0)
        w_t = w_out.T
        return mm(dmixed, w_t[:, :512], tag + "_a_dx"), mm(dmixed, w_t[:, 512:], tag + "_b_dx")

    dh3 = ffn_bwd("ffn1", dh4, ln4_ops, act1_ops, h3b, act1, w['ffn1_w_up'], w['ffn1_w_down'],
                  ['ffn1_w_up', 'ffn1_conv_w', 'ffn1_conv_b', 'ffn1_w_down', 'l1_ln2_g', 'l1_ln2_b'])

    dh2_res, dmixed1, g3g, g3b = block_bwd(f_ln, "l1_ln1_bwd", (1, ni), ln3_ops, ln_outs, [dh3])
    g['l1_ln1_g'], g['l1_ln1_b'] = vec(g3g), vec(g3b)
    do_sb, do_mla = out_bwd("l1_out", dmixed1, o_sb, o_mla, w_out1, 'l1_w_out')

    dq_nope, dq_rope, dkv, dkpe = mla_bwd(q_nope, q_rope, kv, kpe, o_mla32, mla_lse, do_mla, nb, t)
    dsb_q, dsb_k, dsb_v = sb_bwd(proj1, sb_masses, do_sb, nb, t)
    (dc_q, dc_kv, dkpe_raw, gqg, gwqn, gwqr, gkvg, g['mla_w_ukv']) = block_bwd(
        f_mla_pre, "mla_pre_bwd", (1, ni), mpre_ops, mpre_outs, [dq_nope, dq_rope, dkv, dkpe])
    g['mla_q_norm_g'], g['mla_kv_norm_g'] = vec(gqg), vec(gkvg)
    g['mla_w_uq'] = _uq_merge(gwqn, gwqr)
    dproj1 = jnp.concatenate([dsb_q.astype(bf16), dsb_k.astype(bf16), dsb_v.astype(bf16), dc_q, dc_kv, dkpe_raw],
                             axis=1)
    g['l1_w_in'] = mm(h2b, dproj1, "l1_proj_dw", ta=True)[:, :L1_COLS]
    dh2 = mm(dproj1, w_in1.T, "l1_proj_dx", add=dh2_res)

    dh1 = ffn_bwd("ffn0", dh2, ln2_ops, act0_ops, h1b, act0, w['ffn0_w_up'], w['ffn0_w_down'],
                  ['ffn0_w_up', 'ffn0_conv_w', 'ffn0_conv_b', 'ffn0_w_down', 'l0_ln2_g', 'l0_ln2_b'])

    dh0_res, dmixed0, g1g, g1b = block_bwd(f_ln, "l0_ln1_bwd", (1, ni), ln1_ops, ln_outs, [dh1])
    g['l0_ln1_g'], g['l0_ln1_b'] = vec(g1g), vec(g1b)
    dy_a, dy_b = out_bwd("l0_out", dmixed0, y_a, y_b, w_out0, 'l0_w_out')

    dy_ssd, dz, gng = block_bwd(f_ssm_post, "ssm_post_bwd", (1, ni), spost_ops, spost_outs, [dy_b])
    g['ssm_norm_g'] = vec(gng)
    dxs, dbm, dcm, ddt_raw, gdb, gal, gdsk = ssd_bwd(xbc_act, proj0, dt_bias, a_log, d_skip, e_heads, ssd_states,
                                                     dy_ssd, nb, t)
    g['ssm_dt_bias'], g['ssm_a_log'], g['ssm_d'] = gdb[0, :8], gal[0, :8], gdsk[0, :8]
    dxbc_act = jnp.concatenate([dxs, dbm, dcm], axis=1)
    dxbc, cw0, cw1, cw2, cw3, gcb = block_bwd(f_conv4_silu, "ssm_conv_bwd", conv_grid, conv_ops, conv_outs, [dxbc_act])
    g['ssm_conv_w'] = jnp.concatenate([cw0, cw1, cw2, cw3], axis=0)
    g['ssm_conv_b'] = vec(gcb)

    dy_tok, dr_post, dk2_post, dv_post, dgate, glg, glb, grk = block_bwd(
        f_rwkv_post, "rwkv_post_bwd", (1, ni), post_ops, post_outs, [dy_a])
    g['rwkv_ln_g'], g['rwkv_ln_b'], g['rwkv_r_k'] = vec(glg), vec(glb), grk.reshape(N_HEADS, HEAD_DIM)
    ride = None if scatter_early is None else ScatterRide(scatter_early({name: g[name] for name in EARLY_GRADS}))
    (dr, dlw, dk2, dv, dna, dbb), early = rwkv_scan_bwd(s0_saved, ps, lw, k2, na, bb, dy_tok, nb, t, ride)
    (dk_pre, dwa_lo, dg_lo, gw0, g['rwkv_w2'], ga0, g['rwkv_a2'], g['rwkv_g2'], gkk, gka) = block_bwd(
        f_rwkv_pre, "rwkv_pre_bwd", (1, ni), pre_ops, pre_outs, [dlw, dk2 + dk2_post, dna, dbb, dgate])
    g['rwkv_w0'], g['rwkv_a0'], g['rwkv_k_k'], g['rwkv_k_a'] = vec(gw0), vec(ga0), vec(gkk), vec(gka)
    dps = jnp.concatenate([dr + dr_post, dk_pre, dv + dv_post, dwa_lo, dg_lo], axis=1)
    dp_rwkv, gmix = block_bwd(f_shift_mix, "rwkv_shift_bwd", shift_grid, shift_ops, shift_outs, [dps])
    g['rwkv_mix'] = vec(gmix)

    dproj0 = jnp.concatenate([dp_rwkv, dz, dxbc, ddt_raw], axis=1)
    g['l0_w_in'] = mm(h0b, dproj0, "l0_proj_dw", ta=True)[:, :L0_COLS]
    grad_x = mm(dproj0, w_in0.T, "l0_proj_dx", add=dh0_res)
    return loss_part, grad_x.reshape(nb, t, d), g, early


MESH = pl.DeviceIdType.MESH
ANY = pl.BlockSpec(memory_space=pl.ANY)
AXES = ("x", "y", "c")


def _place():
    x, y, c = lax.axis_index("x"), lax.axis_index("y"), lax.axis_index("c")
    chips = [(1 - x, y), (x, 1 - y), (1 - x, 1 - y)]
    return x, y, c, chips


def _dma_sems(n):
    return pltpu.SemaphoreType.DMA((n,))


class GatherRide:
    def __init__(self, shards):
        n = len(shards)
        self.inputs = list(shards)
        self.out_shapes = [jax.ShapeDtypeStruct((N_SHARD,) + a.shape, a.dtype) for a in shards]
        self.scratch = [_dma_sems(3 * n), _dma_sems(3 * n), _dma_sems(3 * n), _dma_sems(3 * n), _dma_sems(n)]

    def _copies(self, ins, outs, sems):
        ici_send, ici_recv, d2d_send, d2d_recv, local_sems = sems
        x, y, c, chips = _place()
        me = 2 * x + y
        pairs = list(enumerate(zip(ins, outs)))

        def over_ici(k, j, slot, to):
            return pltpu.make_async_remote_copy(
                src_ref=ins[k].at[c], dst_ref=outs[k].at[slot, c], send_sem=ici_send.at[3 * k + j],
                recv_sem=ici_recv.at[3 * k + j], device_id=to, device_id_type=MESH)

        def to_sibling(k, j, slot, half):
            return pltpu.make_async_remote_copy(
                src_ref=outs[k].at[slot, half], dst_ref=outs[k].at[slot, half], send_sem=d2d_send.at[3 * k + j],
                recv_sem=d2d_recv.at[3 * k + j], device_id=(x, y, 1 - c), device_id_type=MESH)

        mine = [pltpu.make_async_copy(a, o.at[me], local_sems.at[k]) for k, (a, o) in pairs]
        sends = [over_ici(k, j, me, (cx, cy, c)) for k, _ in pairs for j, (cx, cy) in enumerate(chips)]
        return c, chips, pairs, over_ici, to_sibling, mine, sends

    def start(self, ins, outs, sems):
        _, _, _, _, _, mine, sends = self._copies(ins, outs, sems)
        for cp in mine + sends:
            cp.start()

    def finish(self, ins, outs, sems):
        c, chips, pairs, over_ici, to_sibling, mine, sends = self._copies(ins, outs, sems)
        passed = []
        for k, _ in pairs:
            for j, (cx, cy) in enumerate(chips):
                over_ici(k, j, 2 * cx + cy, (cx, cy, c)).wait_recv()
                passed.append(to_sibling(k, j, 2 * cx + cy, c))
                passed[-1].start()
        for k, _ in pairs:
            for j, (cx, cy) in enumerate(chips):
                to_sibling(k, j, 2 * cx + cy, 1 - c).wait_recv()
        for cp in sends + passed:
            cp.wait_send()
        for cp in mine:
            cp.wait()


class ScatterRide:
    def __init__(self, parts):
        n = len(parts)
        self.inputs = list(parts)
        self.out_shapes = [jax.ShapeDtypeStruct(a.shape, a.dtype) for a in parts]
        self.scratch = [_dma_sems(3 * n), _dma_sems(3 * n), _dma_sems(n)]

    def _copies(self, ins, outs, sems):
        send_sems, recv_sems, local_sems = sems
        x, y, c, chips = _place()
        me = 2 * x + y
        pairs = list(enumerate(zip(ins, outs)))

        def over_ici(k, j, src_slot, dst_slot, to):
            return pltpu.make_async_remote_copy(
                src_ref=ins[k].at[src_slot], dst_ref=outs[k].at[dst_slot], send_sem=send_sems.at[3 * k + j],
                recv_sem=recv_sems.at[3 * k + j], device_id=to, device_id_type=MESH)

        mine = [pltpu.make_async_copy(a.at[me], o.at[me], local_sems.at[k]) for k, (a, o) in pairs]
        sends = [over_ici(k, j, 2 * cx + cy, me, (cx, cy, c)) for k, _ in pairs for j, (cx, cy) in enumerate(chips)]
        arrivals = lambda: [over_ici(k, j, me, 2 * cx + cy, (cx, cy, c))
                            for k, _ in pairs for j, (cx, cy) in enumerate(chips)]
        return mine, sends, arrivals

    def start(self, ins, outs, sems):
        mine, sends, _ = self._copies(ins, outs, sems)
        for cp in mine + sends:
            cp.start()

    def finish(self, ins, outs, sems):
        mine, sends, arrivals = self._copies(ins, outs, sems)
        for cp in arrivals():
            cp.wait_recv()
        for cp in sends:
            cp.wait_send()
        for cp in mine:
            cp.wait()


def _run_ride(ride, name):
    n = len(ride.inputs)

    def body(*refs):
        ins, outs, sems = refs[:n], refs[n:2 * n], refs[2 * n:]
        ride.start(ins, outs, sems)
        ride.finish(ins, outs, sems)

    return pl.pallas_call(body, name=name, in_specs=[ANY] * n, out_specs=[ANY] * n, out_shape=ride.out_shapes,
                          scratch_shapes=ride.scratch)(*ride.inputs)


def gather_shards(shards):
    return _run_ride(GatherRide(shards), "gather_shards")


def swap_halves(pieces, name):
    n = len(pieces)

    def body(*refs):
        ins, outs = refs[:n], refs[n:2 * n]
        send_sems, recv_sems = refs[2 * n:]
        x, y, c, _ = _place()
        cps = [pltpu.make_async_remote_copy(
            src_ref=a.at[:, 1 - c], dst_ref=o, send_sem=send_sems.at[k], recv_sem=recv_sems.at[k],
            device_id=(x, y, 1 - c), device_id_type=MESH) for k, (a, o) in enumerate(zip(ins, outs))]
        for cp in cps:
            cp.start()
        for cp in cps:
            cp.wait()

    return pl.pallas_call(
        body, name=name, in_specs=[ANY] * n, out_specs=[ANY] * n,
        out_shape=[jax.ShapeDtypeStruct((a.shape[0],) + a.shape[2:], a.dtype) for a in pieces],
        scratch_shapes=[_dma_sems(n), _dma_sems(n)],
    )(*pieces)


def scatter_to_chips(parts):
    return _run_ride(ScatterRide(parts), "scatter_to_chips")


def share_halves(bufs):
    n = len(bufs)

    def body(*refs):
        ins, outs = refs[:n], refs[n:2 * n]
        send_sems, recv_sems = refs[2 * n:]
        x, y, c, _ = _place()
        cps = [pltpu.make_async_remote_copy(
            src_ref=a.at[c], dst_ref=o.at[c], send_sem=send_sems.at[k], recv_sem=recv_sems.at[k],
            device_id=(x, y, 1 - c), device_id_type=MESH) for k, (a, o) in enumerate(zip(ins, outs))]
        for cp in cps:
            cp.start()
        for k, (a, o) in enumerate(zip(ins, outs)):
            cps[k].wait_send()
            pltpu.make_async_remote_copy(
                src_ref=a.at[c], dst_ref=o.at[1 - c], send_sem=send_sems.at[k], recv_sem=recv_sems.at[k],
                device_id=(x, y, 1 - c), device_id_type=MESH).wait_recv()

    return pl.pallas_call(
        body, name="share_halves", in_specs=[ANY] * n, out_specs=[ANY] * n,
        out_shape=[jax.ShapeDtypeStruct(a.shape, a.dtype) for a in bufs],
        input_output_aliases={k: k for k in range(n)},
        scratch_shapes=[_dma_sems(n), _dma_sems(n)],
    )(*bufs)


def pair_add(piece, recv, core, name, out_dtype):
    _, _, h, cdim = piece.shape
    tr = _rtile(h, cdim)

    def body(c_ref, a_ref, b_ref, o_ref):
        o_ref[...] = (a_ref[0] + b_ref[...]).astype(o_ref.dtype)

    spec = pl.BlockSpec((1, tr, cdim), lambda p, i, c_ref: (p, i, 0))
    return pl.pallas_call(
        body, name=name,
        grid_spec=pltpu.PrefetchScalarGridSpec(
            num_scalar_prefetch=1, grid=(N_SHARD, h // tr),
            in_specs=[pl.BlockSpec((1, 1, tr, cdim), lambda p, i, c_ref: (p, c_ref[0], i, 0)), spec],
            out_specs=spec),
        out_shape=jax.ShapeDtypeStruct((N_SHARD, h, cdim), out_dtype),
        compiler_params=_cparams(("parallel", "parallel")),
    )(core, piece, recv)


def chip_add(parts, core, name):
    _, h, cdim = parts.shape
    tr = _rtile(h, cdim, 1024 * 1024)

    def body(c_ref, p_ref, o_ref):
        p = [p_ref[s].astype(f32) for s in range(N_SHARD)]
        o_ref[0] = ((p[0] + p[1]) + p[2]) + p[3]

    return pl.pallas_call(
        body, name=name,
        grid_spec=pltpu.PrefetchScalarGridSpec(
            num_scalar_prefetch=1, grid=(h // tr,),
            in_specs=[pl.BlockSpec((N_SHARD, tr, cdim), lambda i, c_ref: (0, i, 0))],
            out_specs=pl.BlockSpec((1, tr, cdim), lambda i, c_ref: (c_ref[0], i, 0))),
        out_shape=jax.ShapeDtypeStruct((2, h, cdim), f32), compiler_params=_cparams(("parallel",)),
    )(core, parts)


def adamw(w, g, m, v, name):
    rows, cdim = w.shape
    tr = _rtile(rows, cdim, 1024 * 1024)

    def body(w_ref, g_ref, m_ref, v_ref, d_ref, nm_ref, nv_ref):
        gv = g_ref[...]
        m_new = ADAM_B1 * m_ref[...] + (1.0 - ADAM_B1) * gv
        v_new = ADAM_B2 * v_ref[...] + (1.0 - ADAM_B2) * jnp.square(gv)
        m_hat = m_new / (1.0 - ADAM_B1 ** ADAM_STEP)
        v_hat = v_new / (1.0 - ADAM_B2 ** ADAM_STEP)
        d_ref[...] = -ADAM_LR * (m_hat / (jnp.sqrt(v_hat) + ADAM_EPS) + ADAM_WD * w_ref[...])
        nm_ref[...] = m_new
        nv_ref[...] = v_new

    spec = pl.BlockSpec((tr, cdim), lambda i: (i, 0))
    return pl.pallas_call(body, name=name, grid=(rows // tr,), in_specs=[spec] * 4, out_specs=[spec] * 3,
                          out_shape=[jax.ShapeDtypeStruct(w.shape, f32)] * 3,
                          compiler_params=_cparams(("parallel",)))(w, g, m, v)


SMALL_MULTIPLE = 16 * LANE


def _pack_flat(parts, multiple=SMALL_MULTIPLE):
    flat = jnp.concatenate([p.reshape(-1) for p in parts])
    pad = (-flat.shape[0]) % multiple
    return jnp.pad(flat, (0, pad)).reshape(-1, LANE)


def _unpack_flat(buf, shapes):
    flat = buf.reshape(-1)
    out, off = [], 0
    for s in shapes:
        cnt = int(np.prod(s))
        out.append(flat[off:off + cnt].reshape(s))
        off += cnt
    return out


def _full_from_shards(name, gathered):
    if name in COL_SHARDED:
        return jnp.concatenate([gathered[s] for s in range(N_SHARD)], axis=1)
    return gathered.reshape(-1, gathered.shape[2])


def _pieces(name, grad):
    if grad.ndim == 3:
        return grad
    if name in COL_SHARDED:
        r, cdim = grad.shape
        return grad.reshape(r, N_SHARD, cdim // N_SHARD).transpose(1, 0, 2)
    return grad.reshape(N_SHARD, grad.shape[0] // N_SHARD, grad.shape[1])


def _small_pieces(name, grad):
    if name in COL_SHARDED or name in ROW_SHARDED:
        return _pieces(name, grad).reshape(N_SHARD, -1)
    return jnp.broadcast_to(grad.reshape(1, -1), (N_SHARD, grad.size))


def kernel(x, positions, l0_w_in, rwkv_mix, rwkv_w0, rwkv_w2, rwkv_a0, rwkv_a2, rwkv_g2, rwkv_k_k, rwkv_k_a, rwkv_r_k, rwkv_ln_g, rwkv_ln_b, ssm_conv_w, ssm_conv_b, ssm_dt_bias, ssm_a_log, ssm_d, ssm_norm_g, l0_w_out, l0_ln1_g, l0_ln1_b, ffn0_w_up, ffn0_conv_w, ffn0_conv_b, ffn0_w_down, l0_ln2_g, l0_ln2_b, l1_w_in, mla_q_norm_g, mla_w_uq, mla_kv_norm_g, mla_w_ukv, l1_w_out, l1_ln1_g, l1_ln1_b, ffn1_w_up, ffn1_conv_w, ffn1_conv_b, ffn1_w_down, l1_ln2_g, l1_ln2_b, loss_target, m_l0_w_in, m_rwkv_mix, m_rwkv_w0, m_rwkv_w2, m_rwkv_a0, m_rwkv_a2, m_rwkv_g2, m_rwkv_k_k, m_rwkv_k_a, m_rwkv_r_k, m_rwkv_ln_g, m_rwkv_ln_b, m_ssm_conv_w, m_ssm_conv_b, m_ssm_dt_bias, m_ssm_a_log, m_ssm_d, m_ssm_norm_g, m_l0_w_out, m_l0_ln1_g, m_l0_ln1_b, m_ffn0_w_up, m_ffn0_conv_w, m_ffn0_conv_b, m_ffn0_w_down, m_l0_ln2_g, m_l0_ln2_b, m_l1_w_in, m_mla_q_norm_g, m_mla_w_uq, m_mla_kv_norm_g, m_mla_w_ukv, m_l1_w_out, m_l1_ln1_g, m_l1_ln1_b, m_ffn1_w_up, m_ffn1_conv_w, m_ffn1_conv_b, m_ffn1_w_down, m_l1_ln2_g, m_l1_ln2_b, v_l0_w_in, v_rwkv_mix, v_rwkv_w0, v_rwkv_w2, v_rwkv_a0, v_rwkv_a2, v_rwkv_g2, v_rwkv_k_k, v_rwkv_k_a, v_rwkv_r_k, v_rwkv_ln_g, v_rwkv_ln_b, v_ssm_conv_w, v_ssm_conv_b, v_ssm_dt_bias, v_ssm_a_log, v_ssm_d, v_ssm_norm_g, v_l0_w_out, v_l0_ln1_g, v_l0_ln1_b, v_ffn0_w_up, v_ffn0_conv_w, v_ffn0_conv_b, v_ffn0_w_down, v_l0_ln2_g, v_l0_ln2_b, v_l1_w_in, v_mla_q_norm_g, v_mla_w_uq, v_mla_kv_norm_g, v_mla_w_ukv, v_l1_w_out, v_l1_ln1_g, v_l1_ln1_b, v_ffn1_w_up, v_ffn1_conv_w, v_ffn1_conv_b, v_ffn1_w_down, v_l1_ln2_g, v_l1_ln2_b):
    args = locals()
    w_loc = {n: args[n] for n in WEIGHTS}
    m_loc = {n: args["m_" + n] for n in WEIGHTS}
    v_loc = {n: args["v_" + n] for n in WEIGHTS}
    core = lax.axis_index("c").astype(jnp.int32).reshape(1)

    small_sharded = [n for n in SMALL if n in COL_SHARDED]
    halves = lambda a: a.reshape(2, a.shape[0] // 2, a.shape[1])
    whole = lambda name, got: _full_from_shards(name, got.reshape(N_SHARD, -1, got.shape[3]))
    first = gather_shards([halves(w_loc['l0_w_in'].astype(bf16)), halves(_pack_flat([w_loc[n] for n in small_sharded]))])
    w_have = {n: w_loc[n] for n in WEIGHTS if n not in BIG}
    w_have['l0_w_in'] = whole('l0_w_in', first[0])
    small_all = first[1].reshape(N_SHARD, -1, LANE)
    per_shard = [_unpack_flat(small_all[s], [w_loc[n].shape for n in small_sharded]) for s in range(N_SHARD)]
    for k, n in enumerate(small_sharded):
        w_have[n] = jnp.concatenate([per_shard[s][k] for s in range(N_SHARD)], axis=1)
    shards_of = lambda names: (names, [halves(w_loc[n].astype(bf16)) for n in names])
    late = (shards_of(['l0_w_out', 'ffn0_w_up', 'ffn0_w_down', 'l1_w_in']),
            shards_of(['l1_w_out', 'ffn1_w_up', 'ffn1_w_down']), whole)

    def pair_sums(names, pieces, tag):
        pieces = [p.reshape(N_SHARD, 2, p.shape[1] // 2, p.shape[2]) for p in pieces]
        return [pair_add(p, r, core, "pair_add_" + n, f32 if n == 'small' else bf16)
                for n, p, r in zip(names, pieces, swap_halves(pieces, "swap_halves_" + tag))]

    loss_part, grad_x, g_full, early = local_step(
        x, positions, loss_target, w_have, late,
        lambda gd: pair_sums(EARLY_GRADS, [_pieces(n, gd[n]) for n in EARLY_GRADS], "early"))
    loss = lax.psum(loss_part[0, 0], AXES)

    small_flat = jnp.concatenate([_small_pieces(n, g_full[n]) for n in SMALL], axis=1)
    pad = (-small_flat.shape[1]) % SMALL_MULTIPLE
    small_pieces = jnp.pad(small_flat, ((0, 0), (0, pad))).reshape(N_SHARD, -1, LANE)
    rest = scatter_to_chips(pair_sums(['l0_w_in', 'small'], [_pieces('l0_w_in', g_full['l0_w_in']), small_pieces],
                                      "rest"))
    from_chips = dict(zip(EARLY_GRADS + ['l0_w_in', 'small'], list(early) + list(rest)))
    units = BIG + ['small']
    both = share_halves([chip_add(from_chips[n], core, "chip_add_" + n) for n in units])
    reduced = [b.reshape(-1, b.shape[2]) for b in both]

    out = {}
    for n, gred in zip(BIG, reduced):
        out[n] = (gred,) + tuple(adamw(w_loc[n], gred, m_loc[n], v_loc[n], "adamw_" + n))
    shapes = [w_loc[n].shape for n in SMALL]
    packs = [_pack_flat([d[n] for n in SMALL]) for d in (w_loc, m_loc, v_loc)]
    small_res = (reduced[-1],) + tuple(adamw(packs[0], reduced[-1], packs[1], packs[2], "adamw_small"))
    small_unpacked = [_unpack_flat(b, shapes) for b in small_res]
    for k, n in enumerate(SMALL):
        out[n] = tuple(u[k] for u in small_unpacked)
    return (loss, grad_x, *[out[n][0] for n in WEIGHTS], *[out[n][1] for n in WEIGHTS],
            *[out[n][2] for n in WEIGHTS], *[out[n][3] for n in WEIGHTS])
```

```python
import functools

import numpy as np
import jax
import jax.numpy as jnp
from jax import lax
from jax.experimental import pallas as pl
from jax.experimental.pallas import tpu as pltpu

f32 = jnp.float32
bf16 = jnp.bfloat16
HI = lax.Precision.HIGHEST
MID = lax.Precision.HIGH

D_MODEL = 1024
HEAD_DIM = 64
N_HEADS = 8
RWKV_COLS = 1792
RWKV_GN_EPS = 64e-5
SSM_STATE = 128
SSM_CHUNK = 128
L0_COLS = 3336
L0_PAD = 3456
L1_COLS = 1952
L1_PAD = 2048
MLA_ROPE = 32
ROPE_THETA = 10000.0
D_FF = 2816
DEPTH = 2
ALPHA = (2 * DEPTH) ** 0.25
ADAM_LR = 0.001
ADAM_B1 = 0.9
ADAM_B2 = 0.999
ADAM_EPS = 1e-08
ADAM_WD = 0.01
ADAM_STEP = 10
RWKV_CHUNK = 64
RWKV_HEADS_PER_STEP = 8
LANE = 128
SUBLANE = 8
VMEM_LIMIT = 56 * 1024 * 1024

WEIGHTS = ['l0_w_in', 'rwkv_mix', 'rwkv_w0', 'rwkv_w2', 'rwkv_a0', 'rwkv_a2', 'rwkv_g2', 'rwkv_k_k', 'rwkv_k_a',
           'rwkv_r_k', 'rwkv_ln_g', 'rwkv_ln_b', 'ssm_conv_w', 'ssm_conv_b', 'ssm_dt_bias', 'ssm_a_log', 'ssm_d',
           'ssm_norm_g', 'l0_w_out', 'l0_ln1_g', 'l0_ln1_b', 'ffn0_w_up', 'ffn0_conv_w', 'ffn0_conv_b',
           'ffn0_w_down', 'l0_ln2_g', 'l0_ln2_b', 'l1_w_in', 'mla_q_norm_g', 'mla_w_uq', 'mla_kv_norm_g',
           'mla_w_ukv', 'l1_w_out', 'l1_ln1_g', 'l1_ln1_b', 'ffn1_w_up', 'ffn1_conv_w', 'ffn1_conv_b',
           'ffn1_w_down', 'l1_ln2_g', 'l1_ln2_b']
COL_SHARDED = ['l0_w_in', 'rwkv_w2', 'rwkv_a2', 'rwkv_g2', 'ssm_conv_w', 'ffn0_w_up', 'ffn0_conv_w', 'l1_w_in',
               'mla_w_uq', 'mla_w_ukv', 'ffn1_w_up', 'ffn1_conv_w']
ROW_SHARDED = ['l0_w_out', 'ffn0_w_down', 'l1_w_out', 'ffn1_w_down']
BIG = ['l0_w_in', 'l0_w_out', 'ffn0_w_up', 'ffn0_w_down', 'l1_w_in', 'l1_w_out', 'ffn1_w_up', 'ffn1_w_down']
SMALL = [n for n in WEIGHTS if n not in BIG]
N_SHARD = 4


def _cparams(sem):
    return pltpu.CompilerParams(dimension_semantics=sem, vmem_limit_bytes=VMEM_LIMIT)


def _dg(a, b, ca, cb, prec=None):
    return lax.dot_general(a, b, (((ca,), (cb,)), ((), ())), precision=prec, preferred_element_type=f32)


def hdot(a, b):
    return _dg(a, b, 1, 0, HI)


def mdot(a, b):
    return _dg(a, b, 1, 0, MID)


def mdot_nt(a, b):
    return _dg(a, b, 1, 1, MID)


def mdot_tn(a, b):
    return _dg(a, b, 0, 0, MID)


def _b(x):
    return x.astype(bf16)


@jax.custom_vjp
def bdot(x, w):
    return _dg(_b(x), _b(w), 1, 0)


def _bdot_fwd(x, w):
    return bdot(x, w), (x, w)


def _bdot_bwd(res, g):
    x, w = res
    return _dg(_b(g), _b(w), 1, 1).astype(x.dtype), _dg(_b(x), _b(g), 0, 0).astype(w.dtype)


bdot.defvjp(_bdot_fwd, _bdot_bwd)


@jax.custom_vjp
def bdot_nt(x, y):
    return _dg(_b(x), _b(y), 1, 1)


def _bdot_nt_fwd(x, y):
    return bdot_nt(x, y), (x, y)


def _bdot_nt_bwd(res, g):
    x, y = res
    return _dg(_b(g), _b(y), 1, 0), _dg(_b(g), _b(x), 0, 0)


bdot_nt.defvjp(_bdot_nt_fwd, _bdot_nt_bwd)


@jax.custom_vjp
def bdot_tn(x, y):
    return _dg(_b(x), _b(y), 0, 0)


def _bdot_tn_fwd(x, y):
    return bdot_tn(x, y), (x, y)


def _bdot_tn_bwd(res, g):
    x, y = res
    return _dg(_b(y), _b(g), 1, 1), _dg(_b(x), _b(g), 1, 0)


bdot_tn.defvjp(_bdot_tn_fwd, _bdot_tn_bwd)


def _sigmoid(x):
    return 1.0 / (1.0 + jnp.exp(-x))


@jax.custom_vjp
def softplus(x):
    e = jnp.exp(-jnp.abs(x))
    u = 1.0 + e
    log1p = jnp.where(u == 1.0, e, jnp.log(u) * e / jnp.where(u == 1.0, 1.0, u - 1.0))
    return jnp.maximum(x, 0.0) + log1p


def _softplus_fwd(x):
    return softplus(x), x


def _softplus_bwd(x, g):
    return (g * _sigmoid(x),)


softplus.defvjp(_softplus_fwd, _softplus_bwd)


@jax.custom_vjp
def softplus_abs(x):
    return jnp.maximum(x, 0.0) + jnp.log(1.0 + jnp.exp(-jnp.abs(x)))


def _softplus_abs_fwd(x):
    return softplus_abs(x), x


softplus_abs.defvjp(_softplus_abs_fwd, _softplus_bwd)


def _two_pass(x, m):
    hi = _b(x)
    lo = _b(x - hi.astype(f32))
    m16 = _b(m)
    return _dg(hi, m16, 1, 0) + _dg(lo, m16, 1, 0)


def _upper(n):
    return (_iota2((n, n), 0) > _iota2((n, n), 1)).astype(f32)


@jax.custom_vjp
def suffix_sums(x):
    return _two_pass(x, _upper(x.shape[1]))


def _suffix_sums_fwd(x):
    return suffix_sums(x), None


def _suffix_sums_bwd(_, g):
    return (_two_pass(g, _upper(g.shape[1]).T),)


suffix_sums.defvjp(_suffix_sums_fwd, _suffix_sums_bwd)


def silu(x):
    return x * _sigmoid(x)


def _shift_rows(x, k, up):
    if k == 0:
        return x
    t = x.shape[0]
    rows = lax.broadcasted_iota(jnp.int32, x.shape, 0)
    if up:
        return jnp.where(rows < t - k, pltpu.roll(x, t - k, 0), 0.0)
    return jnp.where(rows >= k, pltpu.roll(x, k, 0), 0.0)


@functools.partial(jax.custom_vjp, nondiff_argnums=(1,))
def shift_down(x, k):
    return _shift_rows(x, k, False)


def _shift_down_fwd(x, k):
    return _shift_rows(x, k, False), None


def _shift_down_bwd(k, _, g):
    return (_shift_rows(g, k, True),)


shift_down.defvjp(_shift_down_fwd, _shift_down_bwd)


@functools.partial(jax.custom_vjp, nondiff_argnums=(1,))
def lane_roll(x, s):
    return pltpu.roll(x, s % x.shape[1], 1)


def _lane_roll_fwd(x, s):
    return lane_roll(x, s), None


def _lane_roll_bwd(s, _, g):
    return (pltpu.roll(g, (-s) % g.shape[1], 1),)


lane_roll.defvjp(_lane_roll_fwd, _lane_roll_bwd)


def rot_half32(x):
    first = (lax.broadcasted_iota(jnp.int32, x.shape, 1) % MLA_ROPE) < (MLA_ROPE // 2)
    return jnp.where(first, -lane_roll(x, -(MLA_ROPE // 2)), lane_roll(x, MLA_ROPE // 2))


def _iota2(shape, axis):
    return lax.broadcasted_iota(jnp.int32, shape, axis)


class Op:
    def __init__(self, arr, block, imap, diff=True, acc=None, gshape=None, gimap=None, gdtype=f32):
        self.arr, self.block, self.imap, self.diff, self.acc = arr, tuple(block), imap, diff, acc
        self.gshape = tuple(arr.shape) if gshape is None else tuple(gshape)
        self.gimap = imap if gimap is None else gimap
        self.gdtype = gdtype


class Out:
    def __init__(self, shape, block, imap, dtype=f32):
        self.shape, self.block, self.imap, self.dtype = tuple(shape), tuple(block), imap, dtype


def block_fwd(fn, name, grid, ops, outs):
    n_in = len(ops)

    def body(*refs):
        vals = [r[...] for r in refs[:n_in]]
        res = fn(*vals)
        for r, v in zip(refs[n_in:], res):
            r[...] = v.astype(r.dtype)

    res = pl.pallas_call(
        body, name=name, grid=grid,
        in_specs=[pl.BlockSpec(o.block, o.imap) for o in ops],
        out_specs=[pl.BlockSpec(o.block, o.imap) for o in outs],
        out_shape=[jax.ShapeDtypeStruct(o.shape, o.dtype) for o in outs],
        compiler_params=_cparams(("arbitrary", "arbitrary")),
    )(*[o.arr for o in ops])
    return tuple(res)


def block_bwd(fn, name, grid, ops, outs, douts):
    n_in, n_out = len(ops), len(outs)
    dix = [k for k, o in enumerate(ops) if o.diff]

    def body(*refs):
        vals = [r[...] for r in refs[:n_in]]
        dvals = tuple(r[...] for r in refs[n_in:n_in + n_out])
        grefs = refs[n_in + n_out:]

        def f(*d):
            full = list(vals)
            for k, v in zip(dix, d):
                full[k] = v
            return tuple(fn(*full))

        _, vjp = jax.vjp(f, *[vals[k] for k in dix])
        grads = vjp(dvals)
        j, i = pl.program_id(0), pl.program_id(1)
        for k, gref, g in zip(dix, grefs, grads):
            acc = ops[k].acc
            if acc is None:
                gref[...] = g.astype(gref.dtype)
            else:
                first = (i == 0) if acc == 'i' else jnp.logical_and(i == 0, j == 0)

                @pl.when(first)
                def _():
                    gref[...] = g

                @pl.when(jnp.logical_not(first))
                def _():
                    gref[...] += g

    gspecs = [pl.BlockSpec(ops[k].block, ops[k].gimap) for k in dix]
    gshapes = [jax.ShapeDtypeStruct(ops[k].gshape, ops[k].gdtype) for k in dix]
    res = pl.pallas_call(
        body, name=name, grid=grid,
        in_specs=[pl.BlockSpec(o.block, o.imap) for o in ops] + [pl.BlockSpec(o.block, o.imap) for o in outs],
        out_specs=gspecs, out_shape=gshapes,
        compiler_params=_cparams(("arbitrary", "arbitrary")),
    )(*[o.arr for o in ops], *douts)
    return tuple(res)


def _rows(arr, tm, diff=True, gdtype=f32):
    return Op(arr, (tm, arr.shape[1]), lambda j, i: (i, 0), diff=diff, gdtype=gdtype)


def _param(arr, diff=True):
    return Op(arr, arr.shape, lambda j, i: (0,) * arr.ndim, diff=diff, acc='ij')


def _rows_out(n, c, tm, dtype=f32):
    return Out((n, c), (tm, c), lambda j, i: (i, 0), dtype)


def _cols(arr, t, tc, off=0, width=None, gdtype=f32):
    width = arr.shape[1] if width is None else width
    return Op(arr, (t, tc), lambda j, i: (i, j + off), gshape=(arr.shape[0], width), gimap=lambda j, i: (i, j),
              gdtype=gdtype)


def _cparam(arr, tc):
    return Op(arr, (arr.shape[0], tc), lambda j, i: (0, j), acc='i')


def _colblock(arr, tm, off, width, gdtype=f32):
    return Op(arr, (tm, width), lambda j, i: (i, off // width), gshape=(arr.shape[0], width),
              gimap=lambda j, i: (i, 0), gdtype=gdtype)


def _tile(n, cap):
    best = None
    for t in range(LANE, min(n, cap) + 1, LANE):
        if n % t == 0:
            best = t
    return n if best is None else best


def _rtile(rows, cols, cap_bytes=2 * 1024 * 1024):
    best = None
    for t in range(SUBLANE, rows + 1, SUBLANE):
        if rows % t == 0 and t * cols * 4 <= cap_bytes:
            best = t
    return rows if best is None else best


def mm(a, b, name, ta=False, add=None, pieces=None, into=None):
    m = a.shape[1] if ta else a.shape[0]
    kd = a.shape[0] if ta else a.shape[1]
    n = b.shape[1]
    tm, tn = _tile(m, 1408), _tile(n, 1408)
    tk = kd if kd <= 2048 else _tile(kd, 1408)
    nk = kd // tk
    ca = 0 if ta else 1
    n_extra = (add is not None) + (into is not None)

    def body(*refs):
        a_ref, b_ref = refs[:2]
        o_ref, acc = refs[2 + n_extra:]
        k = pl.program_id(2)

        @pl.when(k == 0)
        def _():
            acc[...] = jnp.zeros_like(acc)

        acc[...] += _dg(_b(a_ref[...]), _b(b_ref[...]), ca, 0)

        @pl.when(k == nk - 1)
        def _():
            o_ref[...] = (acc[...] if add is None else acc[...] + refs[2][...]).reshape(o_ref.shape)

    a_spec = pl.BlockSpec((tk, tm), lambda i, j, k: (k, i)) if ta else pl.BlockSpec((tm, tk), lambda i, j, k: (i, k))
    b_spec = pl.BlockSpec((tk, tn), lambda i, j, k: (k, j))
    o_spec = pl.BlockSpec((tm, tn), lambda i, j, k: (i, j))
    out_shape = jax.ShapeDtypeStruct((m, n), f32)
    args, specs, aliases = [a, b], [a_spec, b_spec], {}
    if add is not None:
        args.append(add)
        specs.append(o_spec)
    if pieces is not None:
        count, first, width = pieces
        per = width // tn
        o_spec = pl.BlockSpec((1, tm, tn), lambda i, j, k: (first + j // per, i, j % per))
        out_shape = jax.ShapeDtypeStruct((count, m, width), f32)
        if into is not None:
            aliases = {len(args): 0}
            args.append(into)
            specs.append(pl.BlockSpec(memory_space=pl.ANY))
    return pl.pallas_call(
        body, name=name, grid=(m // tm, n // tn, nk), in_specs=specs, out_specs=o_spec, out_shape=out_shape,
        scratch_shapes=[pltpu.VMEM((tm, tn), f32)], input_output_aliases=aliases,
        compiler_params=_cparams(("parallel", "parallel", "arbitrary")),
    )(*args)


def f_ln(h, y, g, b):
    x = ALPHA * h + y
    mu = jnp.mean(x, axis=-1, keepdims=True)
    xc = x - mu
    var = jnp.mean(xc * xc, axis=-1, keepdims=True)
    return (xc * lax.rsqrt(var + 1e-5) * g + b,)


def f_shift_mix(p, mix):
    return (p + (shift_down(p, 1) - p) * mix,)


def f_rwkv_pre(k, wa_lo, g_lo, w0, w2, a0, a2, g2, k_k, k_a, gh):
    w_lo, a_lo = wa_lo[:, :64], wa_lo[:, 64:]
    log_w = -softplus(-(w0 + bdot(jnp.tanh(w_lo), w2))) - 0.5
    lw = -jnp.exp(log_w)
    a = _sigmoid(a0 + bdot(a_lo, a2))
    g = bdot(_sigmoid(g_lo), g2)
    kk = k * k_k
    kk = kk / jnp.maximum(jnp.sqrt(mdot(kk * kk, gh)), 1e-12)
    k2 = k * (1.0 + (a - 1.0) * k_a)
    return lw, k2, -kk, kk * a, g


def f_rwkv_post(y, r, k2, v, g, ln_g, ln_b, r_k, gh):
    mu = mdot(y, gh) * (1.0 / HEAD_DIM)
    yc = y - mu
    var = mdot(yc * yc, gh) * (1.0 / HEAD_DIM)
    yn = yc * lax.rsqrt(var + RWKV_GN_EPS) * ln_g + ln_b
    bonus = mdot(r * k2 * r_k, gh) * v
    return ((yn + bonus) * g,)


def f_conv4_silu(x, w0, w1, w2, w3, b):
    y = b + shift_down(x, 3) * w0 + shift_down(x, 2) * w1 + shift_down(x, 1) * w2 + x * w3
    return (silu(y),)


def f_ssm_post(y, z, norm_g, gg):
    u = y * silu(z)
    ms = mdot(u * u, gg) * (1.0 / 256.0)
    return (u * lax.rsqrt(ms + 1e-5) * norm_g,)


def f_ffn_act(gate, up, w0, w1, w2, b):
    gc = b + shift_down(gate, 2) * w0 + shift_down(gate, 1) * w1 + gate * w2
    return (silu(gc) * up,)


def _rms(x, g, eps=1e-6):
    return x * lax.rsqrt(jnp.mean(x * x, axis=-1, keepdims=True) + eps) * g


def f_mla_pre(c_q, c_kv, kpe, pos, q_g, w_qn, w_qr, kv_g, w_ukv, inv_q, inv_k):
    qn_in = _rms(c_q, q_g)
    q_nope = bdot(qn_in, w_qn)
    qr = bdot(qn_in, w_qr)
    kv = bdot(_rms(c_kv, kv_g), w_ukv)
    ang_q = pos * inv_q
    ang_k = pos * inv_k
    return (q_nope, qr * jnp.cos(ang_q) + rot_half32(qr) * jnp.sin(ang_q), kv,
            kpe * jnp.cos(ang_k) + rot_half32(kpe) * jnp.sin(ang_k))


def rwkv_chunk(s0, r, lw, k, v, a, b):
    hs = range(len(r))
    l = r[0].shape[0]
    ri, ci = _iota2((l, l), 0), _iota2((l, l), 1)
    strict, incl = ri > ci, ri >= ci
    tri, eye = incl.astype(f32), (ri == ci).astype(f32)
    last = (_iota2((l, 1), 0) == l - 1).astype(f32)
    c = [hdot(tri, lw[h]) for h in hs]
    at = [a[h] * jnp.exp(c[h] - lw[h]) for h in hs]
    wi = [jnp.exp(-c[h]) for h in hs]
    bt = [b[h] * wi[h] for h in hs]
    kt = [k[h] * wi[h] for h in hs]
    rt = [r[h] * jnp.exp(c[h]) for h in hs]
    nab = [jnp.where(strict, mdot_nt(at[h], bt[h]), 0.0) for h in hs]
    nak = [jnp.where(strict, bdot_nt(at[h], kt[h]), 0.0) for h in hs]
    g = [bdot_nt(at[h], s0[h]) + bdot(nak[h], v[h]) for h in hs]
    x = [eye + nab[h] for h in hs]
    p = [mdot(nab[h], nab[h]) for h in hs]
    steps = max(1, (l - 1).bit_length()) - 1
    for it in range(steps):
        x = [x[h] + mdot(p[h], x[h]) for h in hs]
        if it < steps - 1:
            p = [mdot(p[h], p[h]) for h in hs]
    u = [mdot(x[h], g[h]) for h in hs]
    mrb = [jnp.where(incl, bdot_nt(rt[h], bt[h]), 0.0) for h in hs]
    mrk = [jnp.where(incl, bdot_nt(rt[h], kt[h]), 0.0) for h in hs]
    y = [bdot_nt(rt[h], s0[h]) + bdot(mrb[h], u[h]) + bdot(mrk[h], v[h]) for h in hs]
    s1 = [(s0[h] + bdot_tn(u[h], bt[h]) + bdot_tn(v[h], kt[h])) * jnp.exp(jnp.sum(c[h] * last, axis=0, keepdims=True))
          for h in hs]
    return y, s1


def ssd_chunk(xs, bm, cm, dt_raw, s_in, dt_bias, a_log, d_skip, e_heads):
    l = xs.shape[0]
    ri, ci = _iota2((l, l), 0), _iota2((l, l), 1)
    incl = ri >= ci
    tri = incl.astype(f32)
    dt = softplus(dt_raw + dt_bias)
    a128 = dt * (-jnp.exp(a_log))
    lane0 = (_iota2((1, HEAD_DIM), 1) == 0).astype(f32)
    last = (_iota2((l, 1), 0) == l - 1).astype(f32)
    hs = range(N_HEADS)
    group = lambda m, g: m[:, g * SSM_STATE:(g + 1) * SSM_STATE]
    cb = [bdot_nt(group(cm, g), group(bm, g)) for g in range(2)]
    e_all = jnp.concatenate(e_heads, axis=1)
    dt_all = mdot(dt, e_all)
    ac_all = hdot(tri, hdot(a128, e_all))
    xd_all = xs * dt_all
    skip_all = xs * mdot(jnp.broadcast_to(d_skip, (l, LANE)), e_all)
    ac = [ac_all[:, _head(h)] for h in hs]
    xd = [xd_all[:, _head(h)] for h in hs]
    col = [jnp.broadcast_to(jnp.sum(ac[h] * lane0, axis=1, keepdims=True), (l, l)) for h in hs]
    decay = [jnp.exp(jnp.where(incl, col[h] - col[h].T, -1e30)) for h in hs]
    y_diag = [bdot(cb[h // 4] * decay[h], xd[h]) for h in hs]
    a_tot = [jnp.sum(ac[h] * last, axis=0, keepdims=True) for h in hs]
    y_off = [jnp.exp(ac[h]) * bdot(group(cm, h // 4), s_in[h]) for h in hs]
    s_out = [jnp.exp(a_tot[h]) * s_in[h] + bdot_tn(group(bm, h // 4), xd[h] * jnp.exp(a_tot[h] - ac[h])) for h in hs]
    return jnp.concatenate([y_diag[h] + y_off[h] for h in hs], axis=1) + skip_all, s_out


SB_KEYS = LANE
MLA_KEYS = 256


def sb_tile(q, k, v, run, q0, k0, masked=True):
    bq, kb = q.shape[0], k.shape[0]
    z = bdot_nt(q, k) * HEAD_DIM ** -0.5
    if masked:
        strict = (k0 + _iota2((bq, kb), 1)) < (q0 + _iota2((bq, kb), 0))
        lk = jnp.where(strict, -softplus_abs(z), 0.0)
        log_att = z + lk + suffix_sums(lk) + run
        att = jnp.where(strict, jnp.exp(jnp.where(strict, log_att, 0.0)), 0.0)
    else:
        lk = -softplus_abs(z)
        att = jnp.exp(z + lk + suffix_sums(lk) + run)
    return bdot(att, v), jnp.sum(lk, axis=1, keepdims=True)


def mla_scores(qn, qp, kn, kp, q0, k0):
    bq, kb = qn.shape[0], kn.shape[0]
    s = (bdot_nt(qn, kn) + bdot_nt(qp, kp)) * (HEAD_DIM + MLA_ROPE) ** -0.5
    causal = (k0 + _iota2((bq, kb), 1)) <= (q0 + _iota2((bq, kb), 0))
    return jnp.where(causal, s, -1e30), causal


def mla_tile_loss(qn, qp, kn, kp, v, do, lse, dsum, q0, k0):
    s, causal = mla_scores(qn, qp, kn, kp, q0, k0)
    p = jnp.where(causal, jnp.exp(s - lse), 0.0)
    return jnp.sum(do * bdot(p, v)) - jnp.sum(dsum * jnp.sum(p, axis=1, keepdims=True))


def _head(h):
    return slice(h * HEAD_DIM, (h + 1) * HEAD_DIM)


def _rwkv_specs(nc, rev):
    hp = RWKV_HEADS_PER_STEP
    w = hp * HEAD_DIM
    chunk = (lambda c: nc - 1 - c) if rev else (lambda c: c)
    tok = lambda off: pl.BlockSpec((RWKV_CHUNK, w), lambda b, g, c: (b * nc + chunk(c), off // w + g))
    st = pl.BlockSpec((1, hp, HEAD_DIM, HEAD_DIM), lambda b, g, c: ((b * (N_HEADS // hp) + g) * nc + chunk(c), 0, 0, 0))
    return tok, st


def _hosted_call(work, name, grid, in_specs, out_specs, out_shape, scratch, args, ride):
    n_in, n_out, n_scr = len(in_specs), len(out_specs), len(scratch)
    k = 0 if ride is None else len(ride.inputs)

    def body(*refs):
        ins, r_in = refs[:n_in], refs[n_in:n_in + k]
        outs, r_out = refs[n_in + k:n_in + k + n_out], refs[n_in + k + n_out:n_in + 2 * k + n_out]
        scr, r_sems = refs[n_in + 2 * k + n_out:n_in + 2 * k + n_out + n_scr], refs[n_in + 2 * k + n_out + n_scr:]
        ids = [pl.program_id(a) for a in range(len(grid))]
        if ride is not None:
            @pl.when(functools.reduce(jnp.logical_and, [i == 0 for i in ids]))
            def _():
                ride.start(r_in, r_out, r_sems)

        work(ins, outs, scr)
        if ride is not None:
            @pl.when(functools.reduce(jnp.logical_and, [i == g - 1 for i, g in zip(ids, grid)]))
            def _():
                ride.finish(r_in, r_out, r_sems)

    res = pl.pallas_call(
        body, name=name, grid=grid, in_specs=list(in_specs) + [ANY] * k, out_specs=list(out_specs) + [ANY] * k,
        out_shape=list(out_shape) + ([] if ride is None else ride.out_shapes),
        scratch_shapes=list(scratch) + ([] if ride is None else ride.scratch),
        compiler_params=_cparams(("arbitrary",) * len(grid)),
    )(*args, *([] if ride is None else ride.inputs))
    return res[:n_out], res[n_out:]


def rwkv_scan_fwd(ps, lw, k2, na, bb, nb, t, ride=None):
    hp, nc = RWKV_HEADS_PER_STEP, t // RWKV_CHUNK
    ng = N_HEADS // hp
    tok, st = _rwkv_specs(nc, False)

    def work(ins, outs, scr):
        r_ref, v_ref, lw_ref, k_ref, a_ref, b_ref = ins
        y_ref, s0_ref = outs
        (s,) = scr

        @pl.when(pl.program_id(2) == 0)
        def _():
            s[...] = jnp.zeros_like(s)

        s0_ref[0] = s[...]
        heads = lambda ref: [ref[:, _head(h)] for h in range(hp)]
        y, s1 = rwkv_chunk([s[h] for h in range(hp)], heads(r_ref), heads(lw_ref), heads(k_ref), heads(v_ref),
                           heads(a_ref), heads(b_ref))
        for h in range(hp):
            y_ref[:, _head(h)] = y[h]
            s[h] = s1[h]

    return _hosted_call(
        work, "rwkv_scan_fwd", (nb, ng, nc), [tok(0), tok(1024), tok(0), tok(0), tok(0), tok(0)], [tok(0), st],
        [jax.ShapeDtypeStruct((nb * t, N_HEADS * HEAD_DIM), f32),
         jax.ShapeDtypeStruct((nb * ng * nc, hp, HEAD_DIM, HEAD_DIM), f32)],
        [pltpu.VMEM((hp, HEAD_DIM, HEAD_DIM), f32)], (ps, ps, lw, k2, na, bb), ride)


def rwkv_scan_bwd(s0, ps, lw, k2, na, bb, dy, nb, t, ride=None):
    hp, nc = RWKV_HEADS_PER_STEP, t // RWKV_CHUNK
    ng = N_HEADS // hp
    tok, st = _rwkv_specs(nc, True)

    def work(ins, outs, scr):
        s0_ref, r_ref, v_ref, lw_ref, k_ref, a_ref, b_ref, dy_ref = ins
        (ds,) = scr

        @pl.when(pl.program_id(2) == 0)
        def _():
            ds[...] = jnp.zeros_like(ds)

        heads = lambda ref: [ref[:, _head(h)] for h in range(hp)]
        _, vjp = jax.vjp(rwkv_chunk, [s0_ref[0, h] for h in range(hp)], heads(r_ref), heads(lw_ref), heads(k_ref),
                         heads(v_ref), heads(a_ref), heads(b_ref))
        g = vjp((heads(dy_ref), [ds[h] for h in range(hp)]))
        for h in range(hp):
            ds[h] = g[0][h]
            for ref, val in zip(outs, g[1:]):
                ref[:, _head(h)] = val[h]

    return _hosted_call(
        work, "rwkv_scan_bwd", (nb, ng, nc), [st, tok(0), tok(1024), tok(0), tok(0), tok(0), tok(0), tok(0)],
        [tok(0)] * 6, [jax.ShapeDtypeStruct((nb * t, N_HEADS * HEAD_DIM), f32)] * 6,
        [pltpu.VMEM((hp, HEAD_DIM, HEAD_DIM), f32)], (s0, ps, ps, lw, k2, na, bb, dy), ride)


def _ssd_specs(nb, nch, rev):
    def row(b, c):
        return b * nch + (nch - 1 - c if rev else c)

    l = SSM_CHUNK
    xs = pl.BlockSpec((l, 512), lambda b, c: (row(b, c), 0))
    bm = pl.BlockSpec((l, 256), lambda b, c: (row(b, c), 2))
    cm = pl.BlockSpec((l, 256), lambda b, c: (row(b, c), 3))
    dt = pl.BlockSpec((l, LANE), lambda b, c: (row(b, c), (L0_PAD - LANE) // LANE))
    st = pl.BlockSpec((1, 1, N_HEADS, SSM_STATE, HEAD_DIM), lambda b, c: (b, (nch - 1 - c if rev else c), 0, 0, 0))
    par = pl.BlockSpec((1, LANE), lambda b, c: (0, 0))
    eh = pl.BlockSpec((N_HEADS, LANE, HEAD_DIM), lambda b, c: (0, 0, 0))
    return xs, bm, cm, dt, st, par, eh, row


def ssd_fwd(xbc_act, proj0, dt_bias, a_log, d_skip, e_heads, nb, t):
    nch = t // SSM_CHUNK
    n_tok = nb * t
    xs, bm, cm, dt, st, par, eh, row = _ssd_specs(nb, nch, False)

    def body(x_ref, b_ref, c_ref, dt_ref, db_ref, al_ref, dsk_ref, e_ref, y_ref, st_ref, s):
        @pl.when(pl.program_id(1) == 0)
        def _():
            s[...] = jnp.zeros_like(s)

        st_ref[0, 0] = s[...]
        y, s_out = ssd_chunk(x_ref[...], b_ref[...], c_ref[...], dt_ref[...], [s[h] for h in range(N_HEADS)],
                             db_ref[...], al_ref[...], dsk_ref[...], [e_ref[h] for h in range(N_HEADS)])
        y_ref[...] = y
        for h in range(N_HEADS):
            s[h] = s_out[h]

    return pl.pallas_call(
        body, name="ssd_fwd", grid=(nb, nch), in_specs=[xs, bm, cm, dt, par, par, par, eh],
        out_specs=[pl.BlockSpec((SSM_CHUNK, 512), lambda b, c: (row(b, c), 0)), st],
        out_shape=[jax.ShapeDtypeStruct((n_tok, 512), f32),
                   jax.ShapeDtypeStruct((nb, nch, N_HEADS, SSM_STATE, HEAD_DIM), f32)],
        scratch_shapes=[pltpu.VMEM((N_HEADS, SSM_STATE, HEAD_DIM), f32)],
        compiler_params=_cparams(("arbitrary", "arbitrary")),
    )(xbc_act, xbc_act, xbc_act, proj0, dt_bias, a_log, d_skip, e_heads)


def ssd_bwd(xbc_act, proj0, dt_bias, a_log, d_skip, e_heads, states, dy, nb, t, ride=None):
    nch = t // SSM_CHUNK
    n_tok = nb * t
    xs, bm, cm, dt, st, par, eh, row = _ssd_specs(nb, nch, True)

    def work(ins, outs, scr):
        x_ref, b_ref, c_ref, dt_ref, db_ref, al_ref, dsk_ref, e_ref, st_ref, dy_ref = ins
        dx_ref, dbm_ref, dcm_ref, ddt_ref, ddb_ref, dal_ref, ddsk_ref = outs
        (ds,) = scr
        first = jnp.logical_and(pl.program_id(0) == 0, pl.program_id(1) == 0)

        @pl.when(pl.program_id(1) == 0)
        def _():
            ds[...] = jnp.zeros_like(ds)

        e_list = [e_ref[h] for h in range(N_HEADS)]

        def f(x, bmv, cmv, dtr, s_in, dbv, alv, dskv):
            return ssd_chunk(x, bmv, cmv, dtr, s_in, dbv, alv, dskv, e_list)

        _, vjp = jax.vjp(f, x_ref[...], b_ref[...], c_ref[...], dt_ref[...],
                         [st_ref[0, 0, h] for h in range(N_HEADS)], db_ref[...], al_ref[...], dsk_ref[...])
        g = vjp((dy_ref[...], [ds[h] for h in range(N_HEADS)]))
        dx_ref[...], dbm_ref[...], dcm_ref[...], ddt_ref[...] = g[0], g[1], g[2], g[3].astype(bf16)
        for h in range(N_HEADS):
            ds[h] = g[4][h]
        for ref, val in zip((ddb_ref, dal_ref, ddsk_ref), g[5:]):
            @pl.when(first)
            def _():
                ref[...] = val

            @pl.when(jnp.logical_not(first))
            def _():
                ref[...] += val

    rows_spec = lambda w: pl.BlockSpec((SSM_CHUNK, w), lambda b, c: (row(b, c), 0))
    return _hosted_call(
        work, "ssd_bwd", (nb, nch), [xs, bm, cm, dt, par, par, par, eh, st, rows_spec(512)],
        [rows_spec(512), rows_spec(256), rows_spec(256), rows_spec(LANE), par, par, par],
        [jax.ShapeDtypeStruct((n_tok, 512), f32), jax.ShapeDtypeStruct((n_tok, 256), f32),
         jax.ShapeDtypeStruct((n_tok, 256), f32), jax.ShapeDtypeStruct((n_tok, LANE), bf16)]
        + [jax.ShapeDtypeStruct((1, LANE), f32)] * 3,
        [pltpu.VMEM((N_HEADS, SSM_STATE, HEAD_DIM), f32)],
        (xbc_act, xbc_act, xbc_act, proj0, dt_bias, a_log, d_skip, e_heads, states, dy), ride)


ATT_BQ = 512
SB_BQ = 512
SB_TILES_PER_PASS = 2
SB_HEADS_PER_STEP = 2
MLA_HEADS_PER_STEP = 4


def _loop_tiles(n_tiles, per_pass, fn, init):
    def several(i, carry):
        for r in range(per_pass):
            carry = fn(per_pass * i + r, carry)
        return carry

    return lax.fori_loop(0, n_tiles // per_pass, several, init)


def _sb_specs(t, bq, nq):
    w = SB_HEADS_PER_STEP * HEAD_DIM
    qs = lambda off: pl.BlockSpec((bq, w), lambda b, g, i: (b * nq + i, off // w + g))
    ks = lambda off: pl.BlockSpec((t, w), lambda b, g, i: (b, off // w + g))
    return qs, ks


def _sb_mass_spec(bq, nq):
    return pl.BlockSpec((bq, SB_HEADS_PER_STEP * LANE), lambda b, g, i: (b * nq + i, g))


def sb_fwd(proj1, nb, t, ride=None):
    bq = min(SB_BQ, t)
    nq = t // bq
    qs, ks = _sb_specs(t, bq, nq)

    def work(ins, outs, _):
        q_ref, k_ref, v_ref = ins
        o_ref, mass_ref = outs
        q0 = pl.program_id(2) * bq
        n_tiles = (q0 + bq) // SB_KEYS
        hs = range(SB_HEADS_PER_STEP)
        q = [q_ref[:, _head(h)] for h in hs]
        lanes = _iota2((1, LANE), 1)

        def step(i, carry, masked):
            j = n_tiles - 1 - i
            k0 = pl.multiple_of(j * SB_KEYS, SB_KEYS)
            out = []
            for h in hs:
                o, run, kept = carry[h]
                o_t, mass = sb_tile(q[h], k_ref[pl.ds(k0, SB_KEYS), _head(h)], v_ref[pl.ds(k0, SB_KEYS), _head(h)],
                                    run, q0, k0, masked)
                out.append((o + o_t, run + mass, kept + mass * (lanes == j).astype(f32)))
            return out

        diag = bq // SB_KEYS
        res = _loop_tiles(diag, SB_TILES_PER_PASS, functools.partial(step, masked=True),
                          [(jnp.zeros((bq, HEAD_DIM), f32), jnp.zeros((bq, 1), f32), jnp.zeros((bq, LANE), f32))
                           for _ in hs])
        res = _loop_tiles(n_tiles - diag, SB_TILES_PER_PASS, lambda i, cr: step(i + diag, cr, False), res)
        for h in hs:
            o_ref[:, _head(h)] = res[h][0].astype(bf16)
            mass_ref[:, h * LANE:(h + 1) * LANE] = res[h][2]

    return _hosted_call(
        work, "sb_fwd", (nb, N_HEADS // SB_HEADS_PER_STEP, nq), [qs(0), ks(512), ks(1024)],
        [qs(0), _sb_mass_spec(bq, nq)],
        [jax.ShapeDtypeStruct((nb * t, 512), bf16), jax.ShapeDtypeStruct((nb * t, N_HEADS * LANE), f32)],
        [], (proj1, proj1, proj1), ride)


def sb_bwd(proj1, masses, do, nb, t):
    bq = min(SB_BQ, t)
    nq = t // bq
    qs, ks = _sb_specs(t, bq, nq)

    def body(q_ref, k_ref, v_ref, mass_ref, do_ref, dq_ref, dk_ref, dv_ref):
        @pl.when(pl.program_id(2) == 0)
        def _():
            dk_ref[...] = jnp.zeros_like(dk_ref)
            dv_ref[...] = jnp.zeros_like(dv_ref)

        q0 = pl.program_id(2) * bq
        n_tiles = (q0 + bq) // SB_KEYS
        hs = range(SB_HEADS_PER_STEP)
        q = [q_ref[:, _head(h)] for h in hs]
        do = [do_ref[:, _head(h)].astype(f32) for h in hs]
        col0 = jnp.zeros((bq, 1), f32)
        lanes = _iota2((1, LANE), 1)
        run_all = [hdot(mass_ref[:, h * LANE:(h + 1) * LANE], _upper(LANE)) for h in hs]

        def tile(ref, k0, h):
            return ref[pl.ds(k0, SB_KEYS), _head(h)]

        def grads(j, carry, masked):
            k0 = pl.multiple_of(j * SB_KEYS, SB_KEYS)
            pick = (lanes == j).astype(f32)
            out = []
            for h in hs:
                dq, c = carry[h]
                run_in = jnp.sum(run_all[h] * pick, axis=1, keepdims=True)
                _, vjp = jax.vjp(lambda a, b, d, r: sb_tile(a, b, d, r, q0, k0, masked),
                                 q[h], tile(k_ref, k0, h), tile(v_ref, k0, h), run_in)
                dq_t, dk_t, dv_t, drun = vjp((do[h], c))
                dk_ref[pl.ds(k0, SB_KEYS), _head(h)] += dk_t
                dv_ref[pl.ds(k0, SB_KEYS), _head(h)] += dv_t
                out.append((dq + dq_t, drun + c))
            return out

        clear = n_tiles - bq // SB_KEYS
        res = _loop_tiles(clear, SB_TILES_PER_PASS, functools.partial(grads, masked=False),
                          [(jnp.zeros((bq, HEAD_DIM), f32), col0) for _ in hs])
        res = _loop_tiles(bq // SB_KEYS, SB_TILES_PER_PASS, lambda i, cr: grads(i + clear, cr, True), res)
        for h in hs:
            dq_ref[:, _head(h)] = res[h][0]

    return pl.pallas_call(
        body, name="sb_bwd", grid=(nb, N_HEADS // SB_HEADS_PER_STEP, nq),
        in_specs=[qs(0), ks(512), ks(1024), _sb_mass_spec(bq, nq), qs(0)], out_specs=[qs(0), ks(0), ks(0)],
        out_shape=[jax.ShapeDtypeStruct((nb * t, 512), f32)] * 3,
        compiler_params=_cparams(("parallel", "parallel", "arbitrary")),
    )(proj1, proj1, proj1, masses, do)


def _mla_specs(t, bq, nq):
    hp = MLA_HEADS_PER_STEP
    qn = pl.BlockSpec((bq, hp * HEAD_DIM), lambda b, g, i: (b * nq + i, g))
    qr = pl.BlockSpec((bq, hp * MLA_ROPE), lambda b, g, i: (b * nq + i, g))
    kv = pl.BlockSpec((t, hp * 2 * HEAD_DIM), lambda b, g, i: (b, g))
    kp = pl.BlockSpec((t, LANE), lambda b, g, i: (b, 0))
    return qn, qr, kv, kp


def _mla_softmax_pass(qn, qp, kv_ref, kp_ref, q0, n_tiles, bq):
    hs = range(MLA_HEADS_PER_STEP)

    def step(j, carry):
        k0 = pl.multiple_of(j * MLA_KEYS, MLA_KEYS)
        kp = kp_ref[pl.ds(k0, MLA_KEYS), :MLA_ROPE]
        out = []
        for h in hs:
            m, l, acc = carry[h]
            s, _ = mla_scores(qn[h], qp[h], kv_ref[pl.ds(k0, MLA_KEYS), _head(2 * h)], kp, q0, k0)
            m_new = jnp.maximum(m, jnp.max(s, axis=1, keepdims=True))
            alpha, p = jnp.exp(m - m_new), jnp.exp(s - m_new)
            out.append((m_new, alpha * l + jnp.sum(p, axis=1, keepdims=True),
                        alpha * acc + bdot(p, kv_ref[pl.ds(k0, MLA_KEYS), _head(2 * h + 1)])))
        return out

    init = [(jnp.full((bq, 1), -1e30, f32), jnp.zeros((bq, 1), f32), jnp.zeros((bq, HEAD_DIM), f32)) for _ in hs]
    return lax.fori_loop(0, n_tiles, step, init)


def mla_fwd(q_nope, qr, kv, kpe, nb, t):
    bq = min(ATT_BQ, t)
    nq = t // bq
    sqn, sqr, skv, skp = _mla_specs(t, bq, nq)

    def body(qn_ref, qr_ref, kv_ref, kp_ref, o_ref, o32_ref, lse_ref):
        q0 = pl.program_id(2) * bq
        hs = range(MLA_HEADS_PER_STEP)
        qn = [qn_ref[:, _head(h)] for h in hs]
        qp = [qr_ref[:, h * MLA_ROPE:(h + 1) * MLA_ROPE] for h in hs]
        res = _mla_softmax_pass(qn, qp, kv_ref, kp_ref, q0, (q0 + bq) // MLA_KEYS, bq)
        for h in hs:
            m, l, acc = res[h]
            o = acc / l
            o_ref[:, _head(h)] = o.astype(bf16)
            o32_ref[:, _head(h)] = o
            lse_ref[:, _head(h)] = jnp.broadcast_to(m + jnp.log(l), (bq, HEAD_DIM))

    n = nb * t
    return pl.pallas_call(
        body, name="mla_fwd", grid=(nb, N_HEADS // MLA_HEADS_PER_STEP, nq), in_specs=[sqn, sqr, skv, skp],
        out_specs=[sqn, sqn, sqn],
        out_shape=[jax.ShapeDtypeStruct((n, 512), bf16), jax.ShapeDtypeStruct((n, 512), f32),
                   jax.ShapeDtypeStruct((n, 512), f32)],
        compiler_params=_cparams(("parallel", "arbitrary", "arbitrary")),
    )(q_nope, qr, kv, kpe)


def mla_bwd(q_nope, qr, kv, kpe, o32, lse_b, do, nb, t):
    bq = min(ATT_BQ, t)
    nq = t // bq
    sqn, sqr, skv, skp = _mla_specs(t, bq, nq)

    def body(qn_ref, qr_ref, kv_ref, kp_ref, o_ref, lse_ref, do_ref, dqn_ref, dqr_ref, dkv_ref, dkp_ref):
        first_q = pl.program_id(2) == 0

        @pl.when(first_q)
        def _():
            dkv_ref[...] = jnp.zeros_like(dkv_ref)

        @pl.when(jnp.logical_and(first_q, pl.program_id(1) == 0))
        def _():
            dkp_ref[...] = jnp.zeros_like(dkp_ref)

        q0 = pl.program_id(2) * bq
        n_tiles = (q0 + bq) // MLA_KEYS
        hs = range(MLA_HEADS_PER_STEP)
        qn = [qn_ref[:, _head(h)] for h in hs]
        qp = [qr_ref[:, h * MLA_ROPE:(h + 1) * MLA_ROPE] for h in hs]
        do = [do_ref[:, _head(h)].astype(f32) for h in hs]
        lse = [lse_ref[:, h * HEAD_DIM:h * HEAD_DIM + 1] for h in hs]
        dsum = [jnp.sum(do[h] * o_ref[:, _head(h)], axis=1, keepdims=True) for h in hs]

        def grads(j, carry):
            k0 = pl.multiple_of(j * MLA_KEYS, MLA_KEYS)
            rows = pl.ds(k0, MLA_KEYS)
            kp = kp_ref[rows, :MLA_ROPE]
            out = []
            for h in hs:
                dqn, dqp = carry[h]
                g = jax.grad(mla_tile_loss, argnums=(0, 1, 2, 3, 4))(
                    qn[h], qp[h], kv_ref[rows, _head(2 * h)], kp, kv_ref[rows, _head(2 * h + 1)],
                    do[h], lse[h], dsum[h], q0, k0)
                dkv_ref[rows, _head(2 * h)] += g[2]
                dkp_ref[rows, :MLA_ROPE] += g[3]
                dkv_ref[rows, _head(2 * h + 1)] += g[4]
                out.append((dqn + g[0], dqp + g[1]))
            return out

        res = lax.fori_loop(0, n_tiles, grads,
                            [(jnp.zeros((bq, HEAD_DIM), f32), jnp.zeros((bq, MLA_ROPE), f32)) for _ in hs])
        for h in hs:
            dqn_ref[:, _head(h)] = res[h][0]
            dqr_ref[:, h * MLA_ROPE:(h + 1) * MLA_ROPE] = res[h][1]

    n = nb * t
    return pl.pallas_call(
        body, name="mla_bwd", grid=(nb, N_HEADS // MLA_HEADS_PER_STEP, nq),
        in_specs=[sqn, sqr, skv, skp, sqn, sqn, sqn], out_specs=[sqn, sqr, skv, skp],
        out_shape=[jax.ShapeDtypeStruct((n, 512), f32), jax.ShapeDtypeStruct((n, N_HEADS * MLA_ROPE), f32),
                   jax.ShapeDtypeStruct((n, 1024), f32), jax.ShapeDtypeStruct((n, LANE), f32)],
        compiler_params=_cparams(("arbitrary", "arbitrary", "arbitrary")),
    )(q_nope, qr, kv, kpe, o32, lse_b, do)


def loss_head(h, target):
    n, d = h.shape
    tm = _tile(n, 512)

    def body(h_ref, t_ref, l_ref, dh_ref):
        diff = h_ref[...] - t_ref[...]
        dh_ref[...] = diff * (1.0 / d)
        part = 0.5 * jnp.sum(jnp.sum(diff * diff, axis=1, keepdims=True) * (1.0 / d), axis=0, keepdims=True)

        @pl.when(pl.program_id(0) == 0)
        def _():
            l_ref[...] = jnp.zeros_like(l_ref)

        l_ref[...] += jnp.broadcast_to(part, l_ref.shape)

    spec = pl.BlockSpec((tm, d), lambda i: (i, 0))
    return pl.pallas_call(
        body, name="loss_head", grid=(n // tm,), in_specs=[spec, spec],
        out_specs=[pl.BlockSpec((8, LANE), lambda i: (0, 0)), spec],
        out_shape=[jax.ShapeDtypeStruct((8, LANE), f32), jax.ShapeDtypeStruct((n, d), f32)],
        compiler_params=_cparams(("arbitrary",)),
    )(h, target)


def _row(v):
    return v.reshape(1, -1)


def _pad_cols(a, n):
    return jnp.pad(a, ((0, 0), (0, n - a.shape[1])))


def _pad_row(v, n=LANE):
    return jnp.pad(v.reshape(1, -1), ((0, 0), (0, n - v.shape[0])))


def _group_matrix(width, group):
    idx = np.arange(width) // group
    return jnp.asarray((idx[:, None] == idx[None, :]).astype(np.float32))


def _head_expand():
    e = np.zeros((N_HEADS, LANE, HEAD_DIM), np.float32)
    for h in range(N_HEADS):
        e[h, h, :] = 1.0
    return jnp.asarray(e)


def _rope_freqs():
    inv = 1.0 / (ROPE_THETA ** (np.arange(0, MLA_ROPE, 2, dtype=np.float32) / MLA_ROPE))
    inv = np.tile(inv.astype(np.float32), 2)
    inv_q = np.tile(inv, N_HEADS).reshape(1, N_HEADS * MLA_ROPE)
    inv_k = np.zeros((1, LANE), np.float32)
    inv_k[0, :MLA_ROPE] = inv
    return jnp.asarray(inv_q), jnp.asarray(inv_k)


def _uq_split(w):
    w3 = w.reshape(w.shape[0], N_HEADS, HEAD_DIM + MLA_ROPE)
    return w3[:, :, :HEAD_DIM].reshape(-1, 512), w3[:, :, HEAD_DIM:].reshape(-1, N_HEADS * MLA_ROPE)


def _uq_merge(gn, gr):
    r = gn.shape[0]
    return jnp.concatenate([gn.reshape(r, N_HEADS, HEAD_DIM), gr.reshape(r, N_HEADS, MLA_ROPE)], axis=2).reshape(r, 768)


EARLY_GRADS = ['ffn1_w_up', 'ffn1_w_down', 'l1_w_in', 'l1_w_out', 'ffn0_w_up', 'ffn0_w_down', 'l0_w_out']


def local_step(x, positions, target, w, late_weights=None, scatter_early=None):
    w = dict(w)
    nb, t, d = x.shape
    n = nb * t
    tm = 256
    ni = n // tm
    tc = 2 * LANE
    h0 = x.reshape(n, d)
    tgt = target.reshape(n, d)
    pos = positions.reshape(n, 1).astype(f32)
    gh = _group_matrix(512, HEAD_DIM)
    gg = _group_matrix(512, 256)
    e_heads = _head_expand()
    inv_q, inv_k = _rope_freqs()
    g = {}

    def ln_stage(h, y, gname, bname):
        ops = [_rows(h, tm), _rows(y, tm, gdtype=bf16), _param(_row(w[gname])), _param(_row(w[bname]))]
        return ops, [_rows_out(n, d, tm)]

    def ln_fwd(name, ops):
        return block_fwd(lambda *a: f_ln(*a) * 2, name, (1, ni), ops, [_rows_out(n, d, tm), _rows_out(n, d, tm, bf16)])

    def ffn_act_stage(u, cw, cb):
        nj = D_FF // tc
        ops = [_cols(u, t, tc, 0, D_FF, bf16), _cols(u, t, tc, nj, D_FF, bf16)] \
            + [_cparam(cw[i:i + 1], tc) for i in range(3)] + [_cparam(_row(cb), tc)]
        return ops, [Out((n, D_FF), (t, tc), lambda j, i: (i, j), bf16)], (nj, nb)

    w_in0 = _pad_cols(w['l0_w_in'], L0_PAD)
    h0b = h0.astype(bf16)
    proj0 = mm(h0b, w_in0, "l0_proj")

    shift_ops = [_cols(proj0, t, tc, 0, RWKV_COLS, bf16), _cparam(_row(w['rwkv_mix']), tc)]
    shift_outs = [Out((n, RWKV_COLS), (t, tc), lambda j, i: (i, j))]
    shift_grid = (RWKV_COLS // tc, nb)
    (ps,) = block_fwd(f_shift_mix, "rwkv_shift", shift_grid, shift_ops, shift_outs)

    pre_ops = [_colblock(ps, tm, 512, 512), _colblock(ps, tm, 1536, 128), _colblock(ps, tm, 1664, 128),
               _param(_row(w['rwkv_w0'])), _param(w['rwkv_w2']), _param(_row(w['rwkv_a0'])), _param(w['rwkv_a2']),
               _param(w['rwkv_g2']), _param(_row(w['rwkv_k_k'])), _param(_row(w['rwkv_k_a'])), _param(gh, diff=False)]
    pre_outs = [_rows_out(n, 512, tm) for _ in range(5)]
    lw, k2, na, bb, gate_r = block_fwd(f_rwkv_pre, "rwkv_pre", (1, ni), pre_ops, pre_outs)
    def arrived(group, gathered):
        if late_weights is not None:
            for name, got in zip(late_weights[group][0], gathered):
                w[name] = late_weights[2](name, got)

    ride = None if late_weights is None else GatherRide(late_weights[0][1])
    (y_tok, s0_saved), gathered = rwkv_scan_fwd(ps, lw, k2, na, bb, nb, t, ride)
    arrived(0, gathered)
    w_out0 = w['l0_w_out']

    post_ops = [_rows(y_tok, tm), _colblock(ps, tm, 0, 512), _rows(k2, tm), _colblock(ps, tm, 1024, 512),
                _rows(gate_r, tm), _param(_row(w['rwkv_ln_g'])), _param(_row(w['rwkv_ln_b'])),
                _param(w['rwkv_r_k'].reshape(1, 512)), _param(gh, diff=False)]
    post_outs = [_rows_out(n, 512, tm, bf16)]
    (y_a,) = block_fwd(f_rwkv_post, "rwkv_post", (1, ni), post_ops, post_outs)

    xbc_off = (RWKV_COLS + 512) // tc
    conv_ops = [_cols(proj0, t, tc, xbc_off, 1024, bf16)] + [_cparam(w['ssm_conv_w'][i:i + 1], tc) for i in range(4)] \
        + [_cparam(_row(w['ssm_conv_b']), tc)]
    conv_outs = [Out((n, 1024), (t, tc), lambda j, i: (i, j))]
    conv_grid = (1024 // tc, nb)
    (xbc_act,) = block_fwd(f_conv4_silu, "ssm_conv", conv_grid, conv_ops, conv_outs)

    dt_bias, a_log, d_skip = _pad_row(w['ssm_dt_bias']), _pad_row(w['ssm_a_log']), _pad_row(w['ssm_d'])
    y_ssd, ssd_states = ssd_fwd(xbc_act, proj0, dt_bias, a_log, d_skip, e_heads, nb, t)

    z_tok = proj0[:, RWKV_COLS:RWKV_COLS + 512]
    spost_ops = [_rows(y_ssd, tm), _rows(z_tok, tm, gdtype=bf16), _param(_row(w['ssm_norm_g'])), _param(gg, diff=False)]
    spost_outs = [_rows_out(n, 512, tm, bf16)]
    (y_b,) = block_fwd(f_ssm_post, "ssm_post", (1, ni), spost_ops, spost_outs)

    mixed0 = mm(y_b, w_out0[512:], "l0_out_b", add=mm(y_a, w_out0[:512], "l0_out_a"))
    ln1_ops, ln_outs = ln_stage(h0, mixed0, 'l0_ln1_g', 'l0_ln1_b')
    h1, h1b = ln_fwd("l0_ln1", ln1_ops)

    u0 = mm(h1b, w['ffn0_w_up'], "ffn0_up")
    act0_ops, act_outs, act_grid = ffn_act_stage(u0, w['ffn0_conv_w'], w['ffn0_conv_b'])
    (act0,) = block_fwd(f_ffn_act, "ffn0_act", act_grid, act0_ops, act_outs)
    f0 = mm(act0, w['ffn0_w_down'], "ffn0_down")
    ln2_ops, _ = ln_stage(h1, f0, 'l0_ln2_g', 'l0_ln2_b')
    h2, h2b = ln_fwd("l0_ln2", ln2_ops)

    w_in1 = _pad_cols(w['l1_w_in'], L1_PAD)
    proj1 = mm(h2b, w_in1, "l1_proj")
    w_qn, w_qr = _uq_split(w['mla_w_uq'])
    mpre_ops = [_colblock(proj1, tm, 1536, 256, bf16), _colblock(proj1, tm, 1792, 128, bf16),
                _colblock(proj1, tm, 1920, 128, bf16),
                Op(pos, (tm, 1), lambda j, i: (i, 0), diff=False),
                _param(_row(w['mla_q_norm_g'])), _param(w_qn), _param(w_qr),
                _param(_row(w['mla_kv_norm_g'])), _param(w['mla_w_ukv']), _param(inv_q, diff=False),
                _param(inv_k, diff=False)]
    mpre_outs = [_rows_out(n, 512, tm), _rows_out(n, N_HEADS * MLA_ROPE, tm), _rows_out(n, 1024, tm),
                 _rows_out(n, LANE, tm)]
    q_nope, q_rope, kv, kpe = block_fwd(f_mla_pre, "mla_pre", (1, ni), mpre_ops, mpre_outs)
    ride = None if late_weights is None else GatherRide(late_weights[1][1])
    (o_sb, sb_masses), gathered = sb_fwd(proj1, nb, t, ride)
    arrived(1, gathered)
    w_out1 = w['l1_w_out']
    o_mla, o_mla32, mla_lse = mla_fwd(q_nope, q_rope, kv, kpe, nb, t)

    mixed1 = mm(o_mla, w_out1[512:], "l1_out_b", add=mm(o_sb, w_out1[:512], "l1_out_a"))
    ln3_ops, _ = ln_stage(h2, mixed1, 'l1_ln1_g', 'l1_ln1_b')
    h3, h3b = ln_fwd("l1_ln1", ln3_ops)
    u1 = mm(h3b, w['ffn1_w_up'], "ffn1_up")
    act1_ops, _, _ = ffn_act_stage(u1, w['ffn1_conv_w'], w['ffn1_conv_b'])
    (act1,) = block_fwd(f_ffn_act, "ffn1_act", act_grid, act1_ops, act_outs)
    f1 = mm(act1, w['ffn1_w_down'], "ffn1_down")
    ln4_ops, _ = ln_stage(h3, f1, 'l1_ln2_g', 'l1_ln2_b')
    (h4,) = block_fwd(f_ln, "l1_ln2", (1, ni), ln4_ops, ln_outs)

    loss_part, dh4 = loss_head(h4, tgt)

    def vec(a_):
        return a_.reshape(-1)

    def ffn_bwd(tag, dh_out, ln_ops, act_ops, h_in, act, w_up, w_down, names):
        dh_res, df, gg_, gb_ = block_bwd(f_ln, tag + "_ln2_bwd", (1, ni), ln_ops, ln_outs, [dh_out])
        g[names[4]], g[names[5]] = vec(gg_), vec(gb_)
        g[names[3]] = mm(act, df, tag + "_down_dw", ta=True)
        dact = mm(df, w_down.T, tag + "_down_dx")
        dgate, dup, dw0, dw1, dw2, dcb = block_bwd(f_ffn_act, tag + "_act_bwd", act_grid, act_ops, act_outs, [dact])
        g[names[1]] = jnp.concatenate([dw0, dw1, dw2], axis=0)
        g[names[2]] = vec(dcb)
        quarter = 2 * D_FF // N_SHARD
        g[names[0]] = mm(h_in, dup, tag + "_upv_dw", ta=True, pieces=(N_SHARD, 2, quarter),
                         into=mm(h_in, dgate, tag + "_gate_dw", ta=True, pieces=(N_SHARD, 0, quarter)))
        w_up_t = w_up.T
        dh = mm(dgate, w_up_t[:D_FF], tag + "_gate_dx", add=dh_res)
        return mm(dup, w_up_t[D_FF:], tag + "_upv_dx", add=dh)

    def out_bwd(tag, dmixed, y_first, y_second, w_out, name):
        g[name] = jnp.concatenate([mm(y_first, dmixed, tag + "_a_dw", ta=True),
                                   mm(y_second, dmixed, tag + "_b_dw", ta=True)], axis=0)
        w_t = w_out.T
        return mm(dmixed, w_t[:, :512], tag + "_a_dx"), mm(dmixed, w_t[:, 512:], tag + "_b_dx")

    dh3 = ffn_bwd("ffn1", dh4, ln4_ops, act1_ops, h3b, act1, w['ffn1_w_up'], w['ffn1_w_down'],
                  ['ffn1_w_up', 'ffn1_conv_w', 'ffn1_conv_b', 'ffn1_w_down', 'l1_ln2_g', 'l1_ln2_b'])

    dh2_res, dmixed1, g3g, g3b = block_bwd(f_ln, "l1_ln1_bwd", (1, ni), ln3_ops, ln_outs, [dh3])
    g['l1_ln1_g'], g['l1_ln1_b'] = vec(g3g), vec(g3b)
    do_sb, do_mla = out_bwd("l1_out", dmixed1, o_sb, o_mla, w_out1, 'l1_w_out')

    dq_nope, dq_rope, dkv, dkpe = mla_bwd(q_nope, q_rope, kv, kpe, o_mla32, mla_lse, do_mla, nb, t)
    dsb_q, dsb_k, dsb_v = sb_bwd(proj1, sb_masses, do_sb, nb, t)
    (dc_q, dc_kv, dkpe_raw, gqg, gwqn, gwqr, gkvg, g['mla_w_ukv']) = block_bwd(
        f_mla_pre, "mla_pre_bwd", (1, ni), mpre_ops, mpre_outs, [dq_nope, dq_rope, dkv, dkpe])
    g['mla_q_norm_g'], g['mla_kv_norm_g'] = vec(gqg), vec(gkvg)
    g['mla_w_uq'] = _uq_merge(gwqn, gwqr)
    dproj1 = jnp.concatenate([dsb_q.astype(bf16), dsb_k.astype(bf16), dsb_v.astype(bf16), dc_q, dc_kv, dkpe_raw],
                             axis=1)
    g['l1_w_in'] = mm(h2b, dproj1, "l1_proj_dw", ta=True)[:, :L1_COLS]
    dh2 = mm(dproj1, w_in1.T, "l1_proj_dx", add=dh2_res)

    dh1 = ffn_bwd("ffn0", dh2, ln2_ops, act0_ops, h1b, act0, w['ffn0_w_up'], w['ffn0_w_down'],
                  ['ffn0_w_up', 'ffn0_conv_w', 'ffn0_conv_b', 'ffn0_w_down', 'l0_ln2_g', 'l0_ln2_b'])

    dh0_res, dmixed0, g1g, g1b = block_bwd(f_ln, "l0_ln1_bwd", (1, ni), ln1_ops, ln_outs, [dh1])
    g['l0_ln1_g'], g['l0_ln1_b'] = vec(g1g), vec(g1b)
    dy_a, dy_b = out_bwd("l0_out", dmixed0, y_a, y_b, w_out0, 'l0_w_out')

    dy_ssd, dz, gng = block_bwd(f_ssm_post, "ssm_post_bwd", (1, ni), spost_ops, spost_outs, [dy_b])
    g['ssm_norm_g'] = vec(gng)
    early_pieces = None if scatter_early is None else scatter_early[0]({name: g[name] for name in EARLY_GRADS})
    ride = None if scatter_early is None else SwapRide(early_pieces)
    (dxs, dbm, dcm, ddt_raw, gdb, gal, gdsk), from_sibling = ssd_bwd(
        xbc_act, proj0, dt_bias, a_log, d_skip, e_heads, ssd_states, dy_ssd, nb, t, ride)
    g['ssm_dt_bias'], g['ssm_a_log'], g['ssm_d'] = gdb[0, :8], gal[0, :8], gdsk[0, :8]
    dxbc_act = jnp.concatenate([dxs, dbm, dcm], axis=1)
    dxbc, cw0, cw1, cw2, cw3, gcb = block_bwd(f_conv4_silu, "ssm_conv_bwd", conv_grid, conv_ops, conv_outs, [dxbc_act])
    g['ssm_conv_w'] = jnp.concatenate([cw0, cw1, cw2, cw3], axis=0)
    g['ssm_conv_b'] = vec(gcb)

    dy_tok, dr_post, dk2_post, dv_post, dgate, glg, glb, grk = block_bwd(
        f_rwkv_post, "rwkv_post_bwd", (1, ni), post_ops, post_outs, [dy_a])
    g['rwkv_ln_g'], g['rwkv_ln_b'], g['rwkv_r_k'] = vec(glg), vec(glb), grk.reshape(N_HEADS, HEAD_DIM)
    ride = None if scatter_early is None else ScatterRide(scatter_early[1](early_pieces, from_sibling))
    (dr, dlw, dk2, dv, dna, dbb), early = rwkv_scan_bwd(s0_saved, ps, lw, k2, na, bb, dy_tok, nb, t, ride)
    (dk_pre, dwa_lo, dg_lo, gw0, g['rwkv_w2'], ga0, g['rwkv_a2'], g['rwkv_g2'], gkk, gka) = block_bwd(
        f_rwkv_pre, "rwkv_pre_bwd", (1, ni), pre_ops, pre_outs, [dlw, dk2 + dk2_post, dna, dbb, dgate])
    g['rwkv_w0'], g['rwkv_a0'], g['rwkv_k_k'], g['rwkv_k_a'] = vec(gw0), vec(ga0), vec(gkk), vec(gka)
    dps = jnp.concatenate([dr + dr_post, dk_pre, dv + dv_post, dwa_lo, dg_lo], axis=1)
    dp_rwkv, gmix = block_bwd(f_shift_mix, "rwkv_shift_bwd", shift_grid, shift_ops, shift_outs, [dps])
    g['rwkv_mix'] = vec(gmix)

    dproj0 = jnp.concatenate([dp_rwkv, dz, dxbc, ddt_raw], axis=1)
    g['l0_w_in'] = mm(h0b, dproj0, "l0_proj_dw", ta=True)[:, :L0_COLS]
    grad_x = mm(dproj0, w_in0.T, "l0_proj_dx", add=dh0_res)
    return loss_part, grad_x.reshape(nb, t, d), g, early


MESH = pl.DeviceIdType.MESH
ANY = pl.BlockSpec(memory_space=pl.ANY)
AXES = ("x", "y", "c")


def _place():
    x, y, c = lax.axis_index("x"), lax.axis_index("y"), lax.axis_index("c")
    chips = [(1 - x, y), (x, 1 - y), (1 - x, 1 - y)]
    return x, y, c, chips


def _dma_sems(n):
    return pltpu.SemaphoreType.DMA((n,))


class GatherRide:
    def __init__(self, shards):
        n = len(shards)
        self.inputs = list(shards)
        self.out_shapes = [jax.ShapeDtypeStruct((N_SHARD,) + a.shape, a.dtype) for a in shards]
        self.scratch = [_dma_sems(3 * n), _dma_sems(3 * n), _dma_sems(3 * n), _dma_sems(3 * n), _dma_sems(n)]

    def _copies(self, ins, outs, sems):
        ici_send, ici_recv, d2d_send, d2d_recv, local_sems = sems
        x, y, c, chips = _place()
        me = 2 * x + y
        pairs = list(enumerate(zip(ins, outs)))

        def over_ici(k, j, slot, to):
            return pltpu.make_async_remote_copy(
                src_ref=ins[k].at[c], dst_ref=outs[k].at[slot, c], send_sem=ici_send.at[3 * k + j],
                recv_sem=ici_recv.at[3 * k + j], device_id=to, device_id_type=MESH)

        def to_sibling(k, j, slot, half):
            return pltpu.make_async_remote_copy(
                src_ref=outs[k].at[slot, half], dst_ref=outs[k].at[slot, half], send_sem=d2d_send.at[3 * k + j],
                recv_sem=d2d_recv.at[3 * k + j], device_id=(x, y, 1 - c), device_id_type=MESH)

        mine = [pltpu.make_async_copy(a, o.at[me], local_sems.at[k]) for k, (a, o) in pairs]
        sends = [over_ici(k, j, me, (cx, cy, c)) for k, _ in pairs for j, (cx, cy) in enumerate(chips)]
        return c, chips, pairs, over_ici, to_sibling, mine, sends

    def start(self, ins, outs, sems):
        _, _, _, _, _, mine, sends = self._copies(ins, outs, sems)
        for cp in mine + sends:
            cp.start()

    def finish(self, ins, outs, sems):
        c, chips, pairs, over_ici, to_sibling, mine, sends = self._copies(ins, outs, sems)
        passed = []
        for k, _ in pairs:
            for j, (cx, cy) in enumerate(chips):
                over_ici(k, j, 2 * cx + cy, (cx, cy, c)).wait_recv()
                passed.append(to_sibling(k, j, 2 * cx + cy, c))
                passed[-1].start()
        for k, _ in pairs:
            for j, (cx, cy) in enumerate(chips):
                to_sibling(k, j, 2 * cx + cy, 1 - c).wait_recv()
        for cp in sends + passed:
            cp.wait_send()
        for cp in mine:
            cp.wait()


class ScatterRide:
    def __init__(self, parts):
        n = len(parts)
        self.inputs = list(parts)
        self.out_shapes = [jax.ShapeDtypeStruct(a.shape, a.dtype) for a in parts]
        self.scratch = [_dma_sems(3 * n), _dma_sems(3 * n), _dma_sems(n)]

    def _copies(self, ins, outs, sems):
        send_sems, recv_sems, local_sems = sems
        x, y, c, chips = _place()
        me = 2 * x + y
        pairs = list(enumerate(zip(ins, outs)))

        def over_ici(k, j, src_slot, dst_slot, to):
            return pltpu.make_async_remote_copy(
                src_ref=ins[k].at[src_slot], dst_ref=outs[k].at[dst_slot], send_sem=send_sems.at[3 * k + j],
                recv_sem=recv_sems.at[3 * k + j], device_id=to, device_id_type=MESH)

        mine = [pltpu.make_async_copy(a.at[me], o.at[me], local_sems.at[k]) for k, (a, o) in pairs]
        sends = [over_ici(k, j, 2 * cx + cy, me, (cx, cy, c)) for k, _ in pairs for j, (cx, cy) in enumerate(chips)]
        arrivals = lambda: [over_ici(k, j, me, 2 * cx + cy, (cx, cy, c))
                            for k, _ in pairs for j, (cx, cy) in enumerate(chips)]
        return mine, sends, arrivals

    def start(self, ins, outs, sems):
        mine, sends, _ = self._copies(ins, outs, sems)
        for cp in mine + sends:
            cp.start()

    def finish(self, ins, outs, sems):
        mine, sends, arrivals = self._copies(ins, outs, sems)
        for cp in arrivals():
            cp.wait_recv()
        for cp in sends:
            cp.wait_send()
        for cp in mine:
            cp.wait()


def _run_ride(ride, name):
    n = len(ride.inputs)

    def body(*refs):
        ins, outs, sems = refs[:n], refs[n:2 * n], refs[2 * n:]
        ride.start(ins, outs, sems)
        ride.finish(ins, outs, sems)

    return pl.pallas_call(body, name=name, in_specs=[ANY] * n, out_specs=[ANY] * n, out_shape=ride.out_shapes,
                          scratch_shapes=ride.scratch)(*ride.inputs)


def gather_shards(shards):
    return _run_ride(GatherRide(shards), "gather_shards")


class SwapRide:
    def __init__(self, pieces):
        n = len(pieces)
        self.inputs = list(pieces)
        self.out_shapes = [jax.ShapeDtypeStruct((a.shape[0],) + a.shape[2:], a.dtype) for a in pieces]
        self.scratch = [_dma_sems(n), _dma_sems(n)]

    def _copies(self, ins, outs, sems):
        send_sems, recv_sems = sems
        x, y, c, _ = _place()
        return [pltpu.make_async_remote_copy(
            src_ref=a.at[:, 1 - c], dst_ref=o, send_sem=send_sems.at[k], recv_sem=recv_sems.at[k],
            device_id=(x, y, 1 - c), device_id_type=MESH) for k, (a, o) in enumerate(zip(ins, outs))]

    def start(self, ins, outs, sems):
        for cp in self._copies(ins, outs, sems):
            cp.start()

    def finish(self, ins, outs, sems):
        for cp in self._copies(ins, outs, sems):
            cp.wait()


def swap_halves(pieces, name):
    return _run_ride(SwapRide(pieces), name)


def scatter_to_chips(parts):
    return _run_ride(ScatterRide(parts), "scatter_to_chips")


def share_halves(bufs):
    n = len(bufs)

    def body(*refs):
        ins, outs = refs[:n], refs[n:2 * n]
        send_sems, recv_sems = refs[2 * n:]
        x, y, c, _ = _place()
        cps = [pltpu.make_async_remote_copy(
            src_ref=a.at[c], dst_ref=o.at[c], send_sem=send_sems.at[k], recv_sem=recv_sems.at[k],
            device_id=(x, y, 1 - c), device_id_type=MESH) for k, (a, o) in enumerate(zip(ins, outs))]
        for cp in cps:
            cp.start()
        for k, (a, o) in enumerate(zip(ins, outs)):
            cps[k].wait_send()
            pltpu.make_async_remote_copy(
                src_ref=a.at[c], dst_ref=o.at[1 - c], send_sem=send_sems.at[k], recv_sem=recv_sems.at[k],
                device_id=(x, y, 1 - c), device_id_type=MESH).wait_recv()

    return pl.pallas_call(
        body, name="share_halves", in_specs=[ANY] * n, out_specs=[ANY] * n,
        out_shape=[jax.ShapeDtypeStruct(a.shape, a.dtype) for a in bufs],
        input_output_aliases={k: k for k in range(n)},
        scratch_shapes=[_dma_sems(n), _dma_sems(n)],
    )(*bufs)


def pair_add(piece, recv, core, name, out_dtype):
    _, _, h, cdim = piece.shape
    tr = _rtile(h, cdim)

    def body(c_ref, a_ref, b_ref, o_ref):
        o_ref[...] = (a_ref[0] + b_ref[...]).astype(o_ref.dtype)

    spec = pl.BlockSpec((1, tr, cdim), lambda p, i, c_ref: (p, i, 0))
    return pl.pallas_call(
        body, name=name,
        grid_spec=pltpu.PrefetchScalarGridSpec(
            num_scalar_prefetch=1, grid=(N_SHARD, h // tr),
            in_specs=[pl.BlockSpec((1, 1, tr, cdim), lambda p, i, c_ref: (p, c_ref[0], i, 0)), spec],
            out_specs=spec),
        out_shape=jax.ShapeDtypeStruct((N_SHARD, h, cdim), out_dtype),
        compiler_params=_cparams(("parallel", "parallel")),
    )(core, piece, recv)


def chip_add(parts, core, name):
    _, h, cdim = parts.shape
    tr = _rtile(h, cdim, 1024 * 1024)

    def body(c_ref, p_ref, o_ref):
        p = [p_ref[s].astype(f32) for s in range(N_SHARD)]
        o_ref[0] = ((p[0] + p[1]) + p[2]) + p[3]

    return pl.pallas_call(
        body, name=name,
        grid_spec=pltpu.PrefetchScalarGridSpec(
            num_scalar_prefetch=1, grid=(h // tr,),
            in_specs=[pl.BlockSpec((N_SHARD, tr, cdim), lambda i, c_ref: (0, i, 0))],
            out_specs=pl.BlockSpec((1, tr, cdim), lambda i, c_ref: (c_ref[0], i, 0))),
        out_shape=jax.ShapeDtypeStruct((2, h, cdim), f32), compiler_params=_cparams(("parallel",)),
    )(core, parts)


def adamw(w, g, m, v, name):
    rows, cdim = w.shape
    tr = _rtile(rows, cdim, 1024 * 1024)

    def body(w_ref, g_ref, m_ref, v_ref, d_ref, nm_ref, nv_ref):
        gv = g_ref[...]
        m_new = ADAM_B1 * m_ref[...] + (1.0 - ADAM_B1) * gv
        v_new = ADAM_B2 * v_ref[...] + (1.0 - ADAM_B2) * jnp.square(gv)
        m_hat = m_new / (1.0 - ADAM_B1 ** ADAM_STEP)
        v_hat = v_new / (1.0 - ADAM_B2 ** ADAM_STEP)
        d_ref[...] = -ADAM_LR * (m_hat / (jnp.sqrt(v_hat) + ADAM_EPS) + ADAM_WD * w_ref[...])
        nm_ref[...] = m_new
        nv_ref[...] = v_new

    spec = pl.BlockSpec((tr, cdim), lambda i: (i, 0))
    return pl.pallas_call(body, name=name, grid=(rows // tr,), in_specs=[spec] * 4, out_specs=[spec] * 3,
                          out_shape=[jax.ShapeDtypeStruct(w.shape, f32)] * 3,
                          compiler_params=_cparams(("parallel",)))(w, g, m, v)


SMALL_MULTIPLE = 16 * LANE


def _pack_flat(parts, multiple=SMALL_MULTIPLE):
    flat = jnp.concatenate([p.reshape(-1) for p in parts])
    pad = (-flat.shape[0]) % multiple
    return jnp.pad(flat, (0, pad)).reshape(-1, LANE)


def _unpack_flat(buf, shapes):
    flat = buf.reshape(-1)
    out, off = [], 0
    for s in shapes:
        cnt = int(np.prod(s))
        out.append(flat[off:off + cnt].reshape(s))
        off += cnt
    return out


def _full_from_shards(name, gathered):
    if name in COL_SHARDED:
        return jnp.concatenate([gathered[s] for s in range(N_SHARD)], axis=1)
    return gathered.reshape(-1, gathered.shape[2])


def _pieces(name, grad):
    if grad.ndim == 3:
        return grad
    if name in COL_SHARDED:
        r, cdim = grad.shape
        return grad.reshape(r, N_SHARD, cdim // N_SHARD).transpose(1, 0, 2)
    return grad.reshape(N_SHARD, grad.shape[0] // N_SHARD, grad.shape[1])


def _small_pieces(name, grad):
    if name in COL_SHARDED or name in ROW_SHARDED:
        return _pieces(name, grad).reshape(N_SHARD, -1)
    return jnp.broadcast_to(grad.reshape(1, -1), (N_SHARD, grad.size))


def kernel(x, positions, l0_w_in, rwkv_mix, rwkv_w0, rwkv_w2, rwkv_a0, rwkv_a2, rwkv_g2, rwkv_k_k, rwkv_k_a, rwkv_r_k, rwkv_ln_g, rwkv_ln_b, ssm_conv_w, ssm_conv_b, ssm_dt_bias, ssm_a_log, ssm_d, ssm_norm_g, l0_w_out, l0_ln1_g, l0_ln1_b, ffn0_w_up, ffn0_conv_w, ffn0_conv_b, ffn0_w_down, l0_ln2_g, l0_ln2_b, l1_w_in, mla_q_norm_g, mla_w_uq, mla_kv_norm_g, mla_w_ukv, l1_w_out, l1_ln1_g, l1_ln1_b, ffn1_w_up, ffn1_conv_w, ffn1_conv_b, ffn1_w_down, l1_ln2_g, l1_ln2_b, loss_target, m_l0_w_in, m_rwkv_mix, m_rwkv_w0, m_rwkv_w2, m_rwkv_a0, m_rwkv_a2, m_rwkv_g2, m_rwkv_k_k, m_rwkv_k_a, m_rwkv_r_k, m_rwkv_ln_g, m_rwkv_ln_b, m_ssm_conv_w, m_ssm_conv_b, m_ssm_dt_bias, m_ssm_a_log, m_ssm_d, m_ssm_norm_g, m_l0_w_out, m_l0_ln1_g, m_l0_ln1_b, m_ffn0_w_up, m_ffn0_conv_w, m_ffn0_conv_b, m_ffn0_w_down, m_l0_ln2_g, m_l0_ln2_b, m_l1_w_in, m_mla_q_norm_g, m_mla_w_uq, m_mla_kv_norm_g, m_mla_w_ukv, m_l1_w_out, m_l1_ln1_g, m_l1_ln1_b, m_ffn1_w_up, m_ffn1_conv_w, m_ffn1_conv_b, m_ffn1_w_down, m_l1_ln2_g, m_l1_ln2_b, v_l0_w_in, v_rwkv_mix, v_rwkv_w0, v_rwkv_w2, v_rwkv_a0, v_rwkv_a2, v_rwkv_g2, v_rwkv_k_k, v_rwkv_k_a, v_rwkv_r_k, v_rwkv_ln_g, v_rwkv_ln_b, v_ssm_conv_w, v_ssm_conv_b, v_ssm_dt_bias, v_ssm_a_log, v_ssm_d, v_ssm_norm_g, v_l0_w_out, v_l0_ln1_g, v_l0_ln1_b, v_ffn0_w_up, v_ffn0_conv_w, v_ffn0_conv_b, v_ffn0_w_down, v_l0_ln2_g, v_l0_ln2_b, v_l1_w_in, v_mla_q_norm_g, v_mla_w_uq, v_mla_kv_norm_g, v_mla_w_ukv, v_l1_w_out, v_l1_ln1_g, v_l1_ln1_b, v_ffn1_w_up, v_ffn1_conv_w, v_ffn1_conv_b, v_ffn1_w_down, v_l1_ln2_g, v_l1_ln2_b):
    args = locals()
    w_loc = {n: args[n] for n in WEIGHTS}
    m_loc = {n: args["m_" + n] for n in WEIGHTS}
    v_loc = {n: args["v_" + n] for n in WEIGHTS}
    core = lax.axis_index("c").astype(jnp.int32).reshape(1)

    small_sharded = [n for n in SMALL if n in COL_SHARDED]
    halves = lambda a: a.reshape(2, a.shape[0] // 2, a.shape[1])
    whole = lambda name, got: _full_from_shards(name, got.reshape(N_SHARD, -1, got.shape[3]))
    first = gather_shards([halves(w_loc['l0_w_in'].astype(bf16)), halves(_pack_flat([w_loc[n] for n in small_sharded]))])
    w_have = {n: w_loc[n] for n in WEIGHTS if n not in BIG}
    w_have['l0_w_in'] = whole('l0_w_in', first[0])
    small_all = first[1].reshape(N_SHARD, -1, LANE)
    per_shard = [_unpack_flat(small_all[s], [w_loc[n].shape for n in small_sharded]) for s in range(N_SHARD)]
    for k, n in enumerate(small_sharded):
        w_have[n] = jnp.concatenate([per_shard[s][k] for s in range(N_SHARD)], axis=1)
    shards_of = lambda names: (names, [halves(w_loc[n].astype(bf16)) for n in names])
    late = (shards_of(['l0_w_out', 'ffn0_w_up', 'ffn0_w_down', 'l1_w_in']),
            shards_of(['l1_w_out', 'ffn1_w_up', 'ffn1_w_down']), whole)

    def in_halves(pieces):
        return [p.reshape(N_SHARD, 2, p.shape[1] // 2, p.shape[2]) for p in pieces]

    def pair_sums(names, pieces, from_sibling):
        return [pair_add(p, r, core, "pair_add_" + n, f32 if n == 'small' else bf16)
                for n, p, r in zip(names, pieces, from_sibling)]

    loss_part, grad_x, g_full, early = local_step(
        x, positions, loss_target, w_have, late,
        (lambda gd: in_halves([_pieces(n, gd[n]) for n in EARLY_GRADS]),
         lambda pieces, from_sibling: pair_sums(EARLY_GRADS, pieces, from_sibling)))
    loss = lax.psum(loss_part[0, 0], AXES)

    small_flat = jnp.concatenate([_small_pieces(n, g_full[n]) for n in SMALL], axis=1)
    pad = (-small_flat.shape[1]) % SMALL_MULTIPLE
    small_pieces = jnp.pad(small_flat, ((0, 0), (0, pad))).reshape(N_SHARD, -1, LANE)
    rest_pieces = in_halves([_pieces('l0_w_in', g_full['l0_w_in']), small_pieces])
    rest = scatter_to_chips(pair_sums(['l0_w_in', 'small'], rest_pieces, swap_halves(rest_pieces, "swap_halves_rest")))
    from_chips = dict(zip(EARLY_GRADS + ['l0_w_in', 'small'], list(early) + list(rest)))
    units = BIG + ['small']
    both = share_halves([chip_add(from_chips[n], core, "chip_add_" + n) for n in units])
    reduced = [b.reshape(-1, b.shape[2]) for b in both]

    out = {}
    for n, gred in zip(BIG, reduced):
        out[n] = (gred,) + tuple(adamw(w_loc[n], gred, m_loc[n], v_loc[n], "adamw_" + n))
    shapes = [w_loc[n].shape for n in SMALL]
    packs = [_pack_flat([d[n] for n in SMALL]) for d in (w_loc, m_loc, v_loc)]
    small_res = (reduced[-1],) + tuple(adamw(packs[0], reduced[-1], packs[1], packs[2], "adamw_small"))
    small_unpacked = [_unpack_flat(b, shapes) for b in small_res]
    for k, n in enumerate(SMALL):
        out[n] = tuple(u[k] for u in small_unpacked)
    return (loss, grad_x, *[out[n][0] for n in WEIGHTS], *[out[n][1] for n in WEIGHTS],
            *[out[n][2] for n in WEIGHTS], *[out[n][3] for n in WEIGHTS])
```

```python
import functools

import numpy as np
import jax
import jax.numpy as jnp
from jax import lax
from jax.experimental import pallas as pl
from jax.experimental.pallas import tpu as pltpu

f32 = jnp.float32
bf16 = jnp.bfloat16
HI = lax.Precision.HIGHEST
MID = lax.Precision.HIGH

D_MODEL = 1024
HEAD_DIM = 64
N_HEADS = 8
RWKV_COLS = 1792
RWKV_GN_EPS = 64e-5
SSM_STATE = 128
SSM_CHUNK = 128
L0_COLS = 3336
L0_PAD = 3456
L1_COLS = 1952
L1_PAD = 2048
MLA_ROPE = 32
ROPE_THETA = 10000.0
D_FF = 2816
DEPTH = 2
ALPHA = (2 * DEPTH) ** 0.25
ADAM_LR = 0.001
ADAM_B1 = 0.9
ADAM_B2 = 0.999
ADAM_EPS = 1e-08
ADAM_WD = 0.01
ADAM_STEP = 10
RWKV_CHUNK = 64
RWKV_HEADS_PER_STEP = 8
LANE = 128
SUBLANE = 8
VMEM_LIMIT = 56 * 1024 * 1024

WEIGHTS = ['l0_w_in', 'rwkv_mix', 'rwkv_w0', 'rwkv_w2', 'rwkv_a0', 'rwkv_a2', 'rwkv_g2', 'rwkv_k_k', 'rwkv_k_a',
           'rwkv_r_k', 'rwkv_ln_g', 'rwkv_ln_b', 'ssm_conv_w', 'ssm_conv_b', 'ssm_dt_bias', 'ssm_a_log', 'ssm_d',
           'ssm_norm_g', 'l0_w_out', 'l0_ln1_g', 'l0_ln1_b', 'ffn0_w_up', 'ffn0_conv_w', 'ffn0_conv_b',
           'ffn0_w_down', 'l0_ln2_g', 'l0_ln2_b', 'l1_w_in', 'mla_q_norm_g', 'mla_w_uq', 'mla_kv_norm_g',
           'mla_w_ukv', 'l1_w_out', 'l1_ln1_g', 'l1_ln1_b', 'ffn1_w_up', 'ffn1_conv_w', 'ffn1_conv_b',
           'ffn1_w_down', 'l1_ln2_g', 'l1_ln2_b']
COL_SHARDED = ['l0_w_in', 'rwkv_w2', 'rwkv_a2', 'rwkv_g2', 'ssm_conv_w', 'ffn0_w_up', 'ffn0_conv_w', 'l1_w_in',
               'mla_w_uq', 'mla_w_ukv', 'ffn1_w_up', 'ffn1_conv_w']
ROW_SHARDED = ['l0_w_out', 'ffn0_w_down', 'l1_w_out', 'ffn1_w_down']
BIG = ['l0_w_in', 'l0_w_out', 'ffn0_w_up', 'ffn0_w_down', 'l1_w_in', 'l1_w_out', 'ffn1_w_up', 'ffn1_w_down']
SMALL = [n for n in WEIGHTS if n not in BIG]
N_SHARD = 4


def _cparams(sem):
    return pltpu.CompilerParams(dimension_semantics=sem, vmem_limit_bytes=VMEM_LIMIT)


def _dg(a, b, ca, cb, prec=None):
    return lax.dot_general(a, b, (((ca,), (cb,)), ((), ())), precision=prec, preferred_element_type=f32)


def hdot(a, b):
    return _dg(a, b, 1, 0, HI)


def mdot(a, b):
    return _dg(a, b, 1, 0, MID)


def mdot_nt(a, b):
    return _dg(a, b, 1, 1, MID)


def mdot_tn(a, b):
    return _dg(a, b, 0, 0, MID)


def _b(x):
    return x.astype(bf16)


@jax.custom_vjp
def bdot(x, w):
    return _dg(_b(x), _b(w), 1, 0)


def _bdot_fwd(x, w):
    return bdot(x, w), (x, w)


def _bdot_bwd(res, g):
    x, w = res
    return _dg(_b(g), _b(w), 1, 1).astype(x.dtype), _dg(_b(x), _b(g), 0, 0).astype(w.dtype)


bdot.defvjp(_bdot_fwd, _bdot_bwd)


@jax.custom_vjp
def bdot_nt(x, y):
    return _dg(_b(x), _b(y), 1, 1)


def _bdot_nt_fwd(x, y):
    return bdot_nt(x, y), (x, y)


def _bdot_nt_bwd(res, g):
    x, y = res
    return _dg(_b(g), _b(y), 1, 0), _dg(_b(g), _b(x), 0, 0)


bdot_nt.defvjp(_bdot_nt_fwd, _bdot_nt_bwd)


@jax.custom_vjp
def bdot_tn(x, y):
    return _dg(_b(x), _b(y), 0, 0)


def _bdot_tn_fwd(x, y):
    return bdot_tn(x, y), (x, y)


def _bdot_tn_bwd(res, g):
    x, y = res
    return _dg(_b(y), _b(g), 1, 1), _dg(_b(x), _b(g), 1, 0)


bdot_tn.defvjp(_bdot_tn_fwd, _bdot_tn_bwd)


def _sigmoid(x):
    return 1.0 / (1.0 + jnp.exp(-x))


@jax.custom_vjp
def softplus(x):
    e = jnp.exp(-jnp.abs(x))
    u = 1.0 + e
    log1p = jnp.where(u == 1.0, e, jnp.log(u) * e / jnp.where(u == 1.0, 1.0, u - 1.0))
    return jnp.maximum(x, 0.0) + log1p


def _softplus_fwd(x):
    return softplus(x), x


def _softplus_bwd(x, g):
    return (g * _sigmoid(x),)


softplus.defvjp(_softplus_fwd, _softplus_bwd)


@jax.custom_vjp
def softplus_abs(x):
    return jnp.maximum(x, 0.0) + jnp.log(1.0 + jnp.exp(-jnp.abs(x)))


def _softplus_abs_fwd(x):
    return softplus_abs(x), x


softplus_abs.defvjp(_softplus_abs_fwd, _softplus_bwd)


def _two_pass(x, m):
    hi = _b(x)
    lo = _b(x - hi.astype(f32))
    m16 = _b(m)
    return _dg(hi, m16, 1, 0) + _dg(lo, m16, 1, 0)


def _upper(n):
    return (_iota2((n, n), 0) > _iota2((n, n), 1)).astype(f32)


@jax.custom_vjp
def suffix_sums(x):
    return _two_pass(x, _upper(x.shape[1]))


def _suffix_sums_fwd(x):
    return suffix_sums(x), None


def _suffix_sums_bwd(_, g):
    return (_two_pass(g, _upper(g.shape[1]).T),)


suffix_sums.defvjp(_suffix_sums_fwd, _suffix_sums_bwd)


def silu(x):
    return x * _sigmoid(x)


def _shift_rows(x, k, up):
    if k == 0:
        return x
    t = x.shape[0]
    rows = lax.broadcasted_iota(jnp.int32, x.shape, 0)
    if up:
        return jnp.where(rows < t - k, pltpu.roll(x, t - k, 0), 0.0)
    return jnp.where(rows >= k, pltpu.roll(x, k, 0), 0.0)


@functools.partial(jax.custom_vjp, nondiff_argnums=(1,))
def shift_down(x, k):
    return _shift_rows(x, k, False)


def _shift_down_fwd(x, k):
    return _shift_rows(x, k, False), None


def _shift_down_bwd(k, _, g):
    return (_shift_rows(g, k, True),)


shift_down.defvjp(_shift_down_fwd, _shift_down_bwd)


@functools.partial(jax.custom_vjp, nondiff_argnums=(1,))
def lane_roll(x, s):
    return pltpu.roll(x, s % x.shape[1], 1)


def _lane_roll_fwd(x, s):
    return lane_roll(x, s), None


def _lane_roll_bwd(s, _, g):
    return (pltpu.roll(g, (-s) % g.shape[1], 1),)


lane_roll.defvjp(_lane_roll_fwd, _lane_roll_bwd)


def rot_half32(x):
    first = (lax.broadcasted_iota(jnp.int32, x.shape, 1) % MLA_ROPE) < (MLA_ROPE // 2)
    return jnp.where(first, -lane_roll(x, -(MLA_ROPE // 2)), lane_roll(x, MLA_ROPE // 2))


def _iota2(shape, axis):
    return lax.broadcasted_iota(jnp.int32, shape, axis)


class Op:
    def __init__(self, arr, block, imap, diff=True, acc=None, gshape=None, gimap=None, gdtype=f32):
        self.arr, self.block, self.imap, self.diff, self.acc = arr, tuple(block), imap, diff, acc
        self.gshape = tuple(arr.shape) if gshape is None else tuple(gshape)
        self.gimap = imap if gimap is None else gimap
        self.gdtype = gdtype


class Out:
    def __init__(self, shape, block, imap, dtype=f32):
        self.shape, self.block, self.imap, self.dtype = tuple(shape), tuple(block), imap, dtype


def block_fwd(fn, name, grid, ops, outs):
    n_in = len(ops)

    def body(*refs):
        vals = [r[...] for r in refs[:n_in]]
        res = fn(*vals)
        for r, v in zip(refs[n_in:], res):
            r[...] = v.astype(r.dtype)

    res = pl.pallas_call(
        body, name=name, grid=grid,
        in_specs=[pl.BlockSpec(o.block, o.imap) for o in ops],
        out_specs=[pl.BlockSpec(o.block, o.imap) for o in outs],
        out_shape=[jax.ShapeDtypeStruct(o.shape, o.dtype) for o in outs],
        compiler_params=_cparams(("arbitrary", "arbitrary")),
    )(*[o.arr for o in ops])
    return tuple(res)


def block_bwd(fn, name, grid, ops, outs, douts):
    n_in, n_out = len(ops), len(outs)
    dix = [k for k, o in enumerate(ops) if o.diff]

    def body(*refs):
        vals = [r[...] for r in refs[:n_in]]
        dvals = tuple(r[...] for r in refs[n_in:n_in + n_out])
        grefs = refs[n_in + n_out:]

        def f(*d):
            full = list(vals)
            for k, v in zip(dix, d):
                full[k] = v
            return tuple(fn(*full))

        _, vjp = jax.vjp(f, *[vals[k] for k in dix])
        grads = vjp(dvals)
        j, i = pl.program_id(0), pl.program_id(1)
        for k, gref, g in zip(dix, grefs, grads):
            acc = ops[k].acc
            if acc is None:
                gref[...] = g.astype(gref.dtype)
            else:
                first = (i == 0) if acc == 'i' else jnp.logical_and(i == 0, j == 0)

                @pl.when(first)
                def _():
                    gref[...] = g

                @pl.when(jnp.logical_not(first))
                def _():
                    gref[...] += g

    gspecs = [pl.BlockSpec(ops[k].block, ops[k].gimap) for k in dix]
    gshapes = [jax.ShapeDtypeStruct(ops[k].gshape, ops[k].gdtype) for k in dix]
    res = pl.pallas_call(
        body, name=name, grid=grid,
        in_specs=[pl.BlockSpec(o.block, o.imap) for o in ops] + [pl.BlockSpec(o.block, o.imap) for o in outs],
        out_specs=gspecs, out_shape=gshapes,
        compiler_params=_cparams(("arbitrary", "arbitrary")),
    )(*[o.arr for o in ops], *douts)
    return tuple(res)


def _rows(arr, tm, diff=True, gdtype=f32):
    return Op(arr, (tm, arr.shape[1]), lambda j, i: (i, 0), diff=diff, gdtype=gdtype)


def _param(arr, diff=True):
    return Op(arr, arr.shape, lambda j, i: (0,) * arr.ndim, diff=diff, acc='ij')


def _rows_out(n, c, tm, dtype=f32):
    return Out((n, c), (tm, c), lambda j, i: (i, 0), dtype)


def _cols(arr, t, tc, off=0, width=None, gdtype=f32):
    width = arr.shape[1] if width is None else width
    return Op(arr, (t, tc), lambda j, i: (i, j + off), gshape=(arr.shape[0], width), gimap=lambda j, i: (i, j),
              gdtype=gdtype)


def _cparam(arr, tc):
    return Op(arr, (arr.shape[0], tc), lambda j, i: (0, j), acc='i')


def _colblock(arr, tm, off, width, gdtype=f32):
    return Op(arr, (tm, width), lambda j, i: (i, off // width), gshape=(arr.shape[0], width),
              gimap=lambda j, i: (i, 0), gdtype=gdtype)


def _tile(n, cap):
    best = None
    for t in range(LANE, min(n, cap) + 1, LANE):
        if n % t == 0:
            best = t
    return n if best is None else best


def _rtile(rows, cols, cap_bytes=2 * 1024 * 1024):
    best = None
    for t in range(SUBLANE, rows + 1, SUBLANE):
        if rows % t == 0 and t * cols * 4 <= cap_bytes:
            best = t
    return rows if best is None else best


def mm(a, b, name, ta=False, add=None, pieces=None, into=None):
    m = a.shape[1] if ta else a.shape[0]
    kd = a.shape[0] if ta else a.shape[1]
    n = b.shape[1]
    tm, tn = _tile(m, 1408), _tile(n, 1408)
    tk = kd if kd <= 2048 else _tile(kd, 1408)
    nk = kd // tk
    ca = 0 if ta else 1
    n_extra = (add is not None) + (into is not None)

    def body(*refs):
        a_ref, b_ref = refs[:2]
        o_ref, acc = refs[2 + n_extra:]
        k = pl.program_id(2)

        @pl.when(k == 0)
        def _():
            acc[...] = jnp.zeros_like(acc)

        acc[...] += _dg(_b(a_ref[...]), _b(b_ref[...]), ca, 0)

        @pl.when(k == nk - 1)
        def _():
            o_ref[...] = (acc[...] if add is None else acc[...] + refs[2][...]).reshape(o_ref.shape)

    a_spec = pl.BlockSpec((tk, tm), lambda i, j, k: (k, i)) if ta else pl.BlockSpec((tm, tk), lambda i, j, k: (i, k))
    b_spec = pl.BlockSpec((tk, tn), lambda i, j, k: (k, j))
    o_spec = pl.BlockSpec((tm, tn), lambda i, j, k: (i, j))
    out_shape = jax.ShapeDtypeStruct((m, n), f32)
    args, specs, aliases = [a, b], [a_spec, b_spec], {}
    if add is not None:
        args.append(add)
        specs.append(o_spec)
    if pieces is not None:
        count, first, width = pieces
        per = width // tn
        o_spec = pl.BlockSpec((1, tm, tn), lambda i, j, k: (first + j // per, i, j % per))
        out_shape = jax.ShapeDtypeStruct((count, m, width), f32)
        if into is not None:
            aliases = {len(args): 0}
            args.append(into)
            specs.append(pl.BlockSpec(memory_space=pl.ANY))
    return pl.pallas_call(
        body, name=name, grid=(m // tm, n // tn, nk), in_specs=specs, out_specs=o_spec, out_shape=out_shape,
        scratch_shapes=[pltpu.VMEM((tm, tn), f32)], input_output_aliases=aliases,
        compiler_params=_cparams(("parallel", "parallel", "arbitrary")),
    )(*args)


def f_ln(h, y, g, b):
    x = ALPHA * h + y
    mu = jnp.mean(x, axis=-1, keepdims=True)
    xc = x - mu
    var = jnp.mean(xc * xc, axis=-1, keepdims=True)
    return (xc * lax.rsqrt(var + 1e-5) * g + b,)


def f_shift_mix(p, mix):
    return (p + (shift_down(p, 1) - p) * mix,)


def f_rwkv_pre(k, wa_lo, g_lo, w0, w2, a0, a2, g2, k_k, k_a, gh):
    w_lo, a_lo = wa_lo[:, :64], wa_lo[:, 64:]
    log_w = -softplus(-(w0 + bdot(jnp.tanh(w_lo), w2))) - 0.5
    lw = -jnp.exp(log_w)
    a = _sigmoid(a0 + bdot(a_lo, a2))
    g = bdot(_sigmoid(g_lo), g2)
    kk = k * k_k
    kk = kk / jnp.maximum(jnp.sqrt(mdot(kk * kk, gh)), 1e-12)
    k2 = k * (1.0 + (a - 1.0) * k_a)
    return lw, k2, -kk, kk * a, g


def f_rwkv_post(y, r, k2, v, g, ln_g, ln_b, r_k, gh):
    mu = mdot(y, gh) * (1.0 / HEAD_DIM)
    yc = y - mu
    var = mdot(yc * yc, gh) * (1.0 / HEAD_DIM)
    yn = yc * lax.rsqrt(var + RWKV_GN_EPS) * ln_g + ln_b
    bonus = mdot(r * k2 * r_k, gh) * v
    return ((yn + bonus) * g,)


def f_conv4_silu(x, w0, w1, w2, w3, b):
    y = b + shift_down(x, 3) * w0 + shift_down(x, 2) * w1 + shift_down(x, 1) * w2 + x * w3
    return (silu(y),)


def f_ssm_post(y, z, norm_g, gg):
    u = y * silu(z)
    ms = mdot(u * u, gg) * (1.0 / 256.0)
    return (u * lax.rsqrt(ms + 1e-5) * norm_g,)


def f_ffn_act(gate, up, w0, w1, w2, b):
    gc = b + shift_down(gate, 2) * w0 + shift_down(gate, 1) * w1 + gate * w2
    return (silu(gc) * up,)


def _rms(x, g, eps=1e-6):
    return x * lax.rsqrt(jnp.mean(x * x, axis=-1, keepdims=True) + eps) * g


def f_mla_pre(c_q, c_kv, kpe, pos, q_g, w_qn, w_qr, kv_g, w_ukv, inv_q, inv_k):
    qn_in = _rms(c_q, q_g)
    q_nope = bdot(qn_in, w_qn)
    qr = bdot(qn_in, w_qr)
    kv = bdot(_rms(c_kv, kv_g), w_ukv)
    ang_q = pos * inv_q
    ang_k = pos * inv_k
    return (q_nope, qr * jnp.cos(ang_q) + rot_half32(qr) * jnp.sin(ang_q), kv,
            kpe * jnp.cos(ang_k) + rot_half32(kpe) * jnp.sin(ang_k))


def rwkv_chunk(s0, r, lw, k, v, a, b):
    hs = range(len(r))
    l = r[0].shape[0]
    ri, ci = _iota2((l, l), 0), _iota2((l, l), 1)
    strict, incl = ri > ci, ri >= ci
    tri, eye = incl.astype(f32), (ri == ci).astype(f32)
    last = (_iota2((l, 1), 0) == l - 1).astype(f32)
    c = [hdot(tri, lw[h]) for h in hs]
    at = [a[h] * jnp.exp(c[h] - lw[h]) for h in hs]
    wi = [jnp.exp(-c[h]) for h in hs]
    bt = [b[h] * wi[h] for h in hs]
    kt = [k[h] * wi[h] for h in hs]
    rt = [r[h] * jnp.exp(c[h]) for h in hs]
    nab = [jnp.where(strict, mdot_nt(at[h], bt[h]), 0.0) for h in hs]
    nak = [jnp.where(strict, bdot_nt(at[h], kt[h]), 0.0) for h in hs]
    g = [bdot_nt(at[h], s0[h]) + bdot(nak[h], v[h]) for h in hs]
    x = [eye + nab[h] for h in hs]
    p = [mdot(nab[h], nab[h]) for h in hs]
    steps = max(1, (l - 1).bit_length()) - 1
    for it in range(steps):
        x = [x[h] + mdot(p[h], x[h]) for h in hs]
        if it < steps - 1:
            p = [mdot(p[h], p[h]) for h in hs]
    u = [mdot(x[h], g[h]) for h in hs]
    mrb = [jnp.where(incl, bdot_nt(rt[h], bt[h]), 0.0) for h in hs]
    mrk = [jnp.where(incl, bdot_nt(rt[h], kt[h]), 0.0) for h in hs]
    y = [bdot_nt(rt[h], s0[h]) + bdot(mrb[h], u[h]) + bdot(mrk[h], v[h]) for h in hs]
    s1 = [(s0[h] + bdot_tn(u[h], bt[h]) + bdot_tn(v[h], kt[h])) * jnp.exp(jnp.sum(c[h] * last, axis=0, keepdims=True))
          for h in hs]
    return y, s1


def ssd_chunk(xs, bm, cm, dt_raw, s_in, dt_bias, a_log, d_skip, e_heads):
    l = xs.shape[0]
    ri, ci = _iota2((l, l), 0), _iota2((l, l), 1)
    incl = ri >= ci
    tri = incl.astype(f32)
    dt = softplus(dt_raw + dt_bias)
    a128 = dt * (-jnp.exp(a_log))
    lane0 = (_iota2((1, HEAD_DIM), 1) == 0).astype(f32)
    last = (_iota2((l, 1), 0) == l - 1).astype(f32)
    hs = range(N_HEADS)
    group = lambda m, g: m[:, g * SSM_STATE:(g + 1) * SSM_STATE]
    cb = [bdot_nt(group(cm, g), group(bm, g)) for g in range(2)]
    e_all = jnp.concatenate(e_heads, axis=1)
    dt_all = mdot(dt, e_all)
    ac_all = hdot(tri, hdot(a128, e_all))
    xd_all = xs * dt_all
    skip_all = xs * mdot(jnp.broadcast_to(d_skip, (l, LANE)), e_all)
    ac = [ac_all[:, _head(h)] for h in hs]
    xd = [xd_all[:, _head(h)] for h in hs]
    col = [jnp.broadcast_to(jnp.sum(ac[h] * lane0, axis=1, keepdims=True), (l, l)) for h in hs]
    decay = [jnp.exp(jnp.where(incl, col[h] - col[h].T, -1e30)) for h in hs]
    y_diag = [bdot(cb[h // 4] * decay[h], xd[h]) for h in hs]
    a_tot = [jnp.sum(ac[h] * last, axis=0, keepdims=True) for h in hs]
    y_off = [jnp.exp(ac[h]) * bdot(group(cm, h // 4), s_in[h]) for h in hs]
    s_out = [jnp.exp(a_tot[h]) * s_in[h] + bdot_tn(group(bm, h // 4), xd[h] * jnp.exp(a_tot[h] - ac[h])) for h in hs]
    return jnp.concatenate([y_diag[h] + y_off[h] for h in hs], axis=1) + skip_all, s_out


SB_KEYS = LANE
MLA_KEYS = 256


def sb_tile(q, k, v, run, q0, k0, masked=True):
    bq, kb = q.shape[0], k.shape[0]
    z = bdot_nt(q, k) * HEAD_DIM ** -0.5
    if masked:
        strict = (k0 + _iota2((bq, kb), 1)) < (q0 + _iota2((bq, kb), 0))
        lk = jnp.where(strict, -softplus_abs(z), 0.0)
        log_att = z + lk + suffix_sums(lk) + run
        att = jnp.where(strict, jnp.exp(jnp.where(strict, log_att, 0.0)), 0.0)
    else:
        lk = -softplus_abs(z)
        att = jnp.exp(z + lk + suffix_sums(lk) + run)
    return bdot(att, v), jnp.sum(lk, axis=1, keepdims=True)


def mla_scores(qn, qp, kn, kp, q0, k0):
    bq, kb = qn.shape[0], kn.shape[0]
    s = (bdot_nt(qn, kn) + bdot_nt(qp, kp)) * (HEAD_DIM + MLA_ROPE) ** -0.5
    causal = (k0 + _iota2((bq, kb), 1)) <= (q0 + _iota2((bq, kb), 0))
    return jnp.where(causal, s, -1e30), causal


def mla_tile_loss(qn, qp, kn, kp, v, do, lse, dsum, q0, k0):
    s, causal = mla_scores(qn, qp, kn, kp, q0, k0)
    p = jnp.where(causal, jnp.exp(s - lse), 0.0)
    return jnp.sum(do * bdot(p, v)) - jnp.sum(dsum * jnp.sum(p, axis=1, keepdims=True))


def _head(h):
    return slice(h * HEAD_DIM, (h + 1) * HEAD_DIM)


def _rwkv_specs(nc, rev):
    hp = RWKV_HEADS_PER_STEP
    w = hp * HEAD_DIM
    chunk = (lambda c: nc - 1 - c) if rev else (lambda c: c)
    tok = lambda off: pl.BlockSpec((RWKV_CHUNK, w), lambda b, g, c: (b * nc + chunk(c), off // w + g))
    st = pl.BlockSpec((1, hp, HEAD_DIM, HEAD_DIM), lambda b, g, c: ((b * (N_HEADS // hp) + g) * nc + chunk(c), 0, 0, 0))
    return tok, st


def _hosted_call(work, name, grid, in_specs, out_specs, out_shape, scratch, args, ride):
    n_in, n_out, n_scr = len(in_specs), len(out_specs), len(scratch)
    k = 0 if ride is None else len(ride.inputs)

    def body(*refs):
        ins, r_in = refs[:n_in], refs[n_in:n_in + k]
        outs, r_out = refs[n_in + k:n_in + k + n_out], refs[n_in + k + n_out:n_in + 2 * k + n_out]
        scr, r_sems = refs[n_in + 2 * k + n_out:n_in + 2 * k + n_out + n_scr], refs[n_in + 2 * k + n_out + n_scr:]
        ids = [pl.program_id(a) for a in range(len(grid))]
        if ride is not None:
            @pl.when(functools.reduce(jnp.logical_and, [i == 0 for i in ids]))
            def _():
                ride.start(r_in, r_out, r_sems)

        work(ins, outs, scr)
        if ride is not None:
            @pl.when(functools.reduce(jnp.logical_and, [i == g - 1 for i, g in zip(ids, grid)]))
            def _():
                ride.finish(r_in, r_out, r_sems)

    res = pl.pallas_call(
        body, name=name, grid=grid, in_specs=list(in_specs) + [ANY] * k, out_specs=list(out_specs) + [ANY] * k,
        out_shape=list(out_shape) + ([] if ride is None else ride.out_shapes),
        scratch_shapes=list(scratch) + ([] if ride is None else ride.scratch),
        compiler_params=_cparams(("arbitrary",) * len(grid)),
    )(*args, *([] if ride is None else ride.inputs))
    return res[:n_out], res[n_out:]


def rwkv_scan_fwd(ps, lw, k2, na, bb, nb, t, ride=None):
    hp, nc = RWKV_HEADS_PER_STEP, t // RWKV_CHUNK
    ng = N_HEADS // hp
    tok, st = _rwkv_specs(nc, False)

    def work(ins, outs, scr):
        r_ref, v_ref, lw_ref, k_ref, a_ref, b_ref = ins
        y_ref, s0_ref = outs
        (s,) = scr

        @pl.when(pl.program_id(2) == 0)
        def _():
            s[...] = jnp.zeros_like(s)

        s0_ref[0] = s[...]
        heads = lambda ref: [ref[:, _head(h)] for h in range(hp)]
        y, s1 = rwkv_chunk([s[h] for h in range(hp)], heads(r_ref), heads(lw_ref), heads(k_ref), heads(v_ref),
                           heads(a_ref), heads(b_ref))
        for h in range(hp):
            y_ref[:, _head(h)] = y[h]
            s[h] = s1[h]

    return _hosted_call(
        work, "rwkv_scan_fwd", (nb, ng, nc), [tok(0), tok(1024), tok(0), tok(0), tok(0), tok(0)], [tok(0), st],
        [jax.ShapeDtypeStruct((nb * t, N_HEADS * HEAD_DIM), f32),
         jax.ShapeDtypeStruct((nb * ng * nc, hp, HEAD_DIM, HEAD_DIM), f32)],
        [pltpu.VMEM((hp, HEAD_DIM, HEAD_DIM), f32)], (ps, ps, lw, k2, na, bb), ride)


def rwkv_scan_bwd(s0, ps, lw, k2, na, bb, dy, nb, t, ride=None):
    hp, nc = RWKV_HEADS_PER_STEP, t // RWKV_CHUNK
    ng = N_HEADS // hp
    tok, st = _rwkv_specs(nc, True)

    def work(ins, outs, scr):
        s0_ref, r_ref, v_ref, lw_ref, k_ref, a_ref, b_ref, dy_ref = ins
        (ds,) = scr

        @pl.when(pl.program_id(2) == 0)
        def _():
            ds[...] = jnp.zeros_like(ds)

        heads = lambda ref: [ref[:, _head(h)] for h in range(hp)]
        _, vjp = jax.vjp(rwkv_chunk, [s0_ref[0, h] for h in range(hp)], heads(r_ref), heads(lw_ref), heads(k_ref),
                         heads(v_ref), heads(a_ref), heads(b_ref))
        g = vjp((heads(dy_ref), [ds[h] for h in range(hp)]))
        for h in range(hp):
            ds[h] = g[0][h]
            for ref, val in zip(outs, g[1:]):
                ref[:, _head(h)] = val[h]

    return _hosted_call(
        work, "rwkv_scan_bwd", (nb, ng, nc), [st, tok(0), tok(1024), tok(0), tok(0), tok(0), tok(0), tok(0)],
        [tok(0)] * 6, [jax.ShapeDtypeStruct((nb * t, N_HEADS * HEAD_DIM), f32)] * 6,
        [pltpu.VMEM((hp, HEAD_DIM, HEAD_DIM), f32)], (s0, ps, ps, lw, k2, na, bb, dy), ride)


def _ssd_specs(nb, nch, rev):
    def row(b, c):
        return b * nch + (nch - 1 - c if rev else c)

    l = SSM_CHUNK
    xs = pl.BlockSpec((l, 512), lambda b, c: (row(b, c), 0))
    bm = pl.BlockSpec((l, 256), lambda b, c: (row(b, c), 2))
    cm = pl.BlockSpec((l, 256), lambda b, c: (row(b, c), 3))
    dt = pl.BlockSpec((l, LANE), lambda b, c: (row(b, c), (L0_PAD - LANE) // LANE))
    st = pl.BlockSpec((1, 1, N_HEADS, SSM_STATE, HEAD_DIM), lambda b, c: (b, (nch - 1 - c if rev else c), 0, 0, 0))
    par = pl.BlockSpec((1, LANE), lambda b, c: (0, 0))
    eh = pl.BlockSpec((N_HEADS, LANE, HEAD_DIM), lambda b, c: (0, 0, 0))
    return xs, bm, cm, dt, st, par, eh, row


def ssd_fwd(xbc_act, proj0, dt_bias, a_log, d_skip, e_heads, nb, t):
    nch = t // SSM_CHUNK
    n_tok = nb * t
    xs, bm, cm, dt, st, par, eh, row = _ssd_specs(nb, nch, False)

    def body(x_ref, b_ref, c_ref, dt_ref, db_ref, al_ref, dsk_ref, e_ref, y_ref, st_ref, s):
        @pl.when(pl.program_id(1) == 0)
        def _():
            s[...] = jnp.zeros_like(s)

        st_ref[0, 0] = s[...]
        y, s_out = ssd_chunk(x_ref[...], b_ref[...], c_ref[...], dt_ref[...], [s[h] for h in range(N_HEADS)],
                             db_ref[...], al_ref[...], dsk_ref[...], [e_ref[h] for h in range(N_HEADS)])
        y_ref[...] = y
        for h in range(N_HEADS):
            s[h] = s_out[h]

    return pl.pallas_call(
        body, name="ssd_fwd", grid=(nb, nch), in_specs=[xs, bm, cm, dt, par, par, par, eh],
        out_specs=[pl.BlockSpec((SSM_CHUNK, 512), lambda b, c: (row(b, c), 0)), st],
        out_shape=[jax.ShapeDtypeStruct((n_tok, 512), f32),
                   jax.ShapeDtypeStruct((nb, nch, N_HEADS, SSM_STATE, HEAD_DIM), f32)],
        scratch_shapes=[pltpu.VMEM((N_HEADS, SSM_STATE, HEAD_DIM), f32)],
        compiler_params=_cparams(("arbitrary", "arbitrary")),
    )(xbc_act, xbc_act, xbc_act, proj0, dt_bias, a_log, d_skip, e_heads)


def ssd_bwd(xbc_act, proj0, dt_bias, a_log, d_skip, e_heads, states, dy, nb, t, ride=None):
    nch = t // SSM_CHUNK
    n_tok = nb * t
    xs, bm, cm, dt, st, par, eh, row = _ssd_specs(nb, nch, True)

    def work(ins, outs, scr):
        x_ref, b_ref, c_ref, dt_ref, db_ref, al_ref, dsk_ref, e_ref, st_ref, dy_ref = ins
        dx_ref, dbm_ref, dcm_ref, ddt_ref, ddb_ref, dal_ref, ddsk_ref = outs
        (ds,) = scr
        first = jnp.logical_and(pl.program_id(0) == 0, pl.program_id(1) == 0)

        @pl.when(pl.program_id(1) == 0)
        def _():
            ds[...] = jnp.zeros_like(ds)

        e_list = [e_ref[h] for h in range(N_HEADS)]

        def f(x, bmv, cmv, dtr, s_in, dbv, alv, dskv):
            return ssd_chunk(x, bmv, cmv, dtr, s_in, dbv, alv, dskv, e_list)

        _, vjp = jax.vjp(f, x_ref[...], b_ref[...], c_ref[...], dt_ref[...],
                         [st_ref[0, 0, h] for h in range(N_HEADS)], db_ref[...], al_ref[...], dsk_ref[...])
        g = vjp((dy_ref[...], [ds[h] for h in range(N_HEADS)]))
        dx_ref[...], dbm_ref[...], dcm_ref[...], ddt_ref[...] = g[0], g[1], g[2], g[3].astype(bf16)
        for h in range(N_HEADS):
            ds[h] = g[4][h]
        for ref, val in zip((ddb_ref, dal_ref, ddsk_ref), g[5:]):
            @pl.when(first)
            def _():
                ref[...] = val

            @pl.when(jnp.logical_not(first))
            def _():
                ref[...] += val

    rows_spec = lambda w: pl.BlockSpec((SSM_CHUNK, w), lambda b, c: (row(b, c), 0))
    return _hosted_call(
        work, "ssd_bwd", (nb, nch), [xs, bm, cm, dt, par, par, par, eh, st, rows_spec(512)],
        [rows_spec(512), rows_spec(256), rows_spec(256), rows_spec(LANE), par, par, par],
        [jax.ShapeDtypeStruct((n_tok, 512), f32), jax.ShapeDtypeStruct((n_tok, 256), f32),
         jax.ShapeDtypeStruct((n_tok, 256), f32), jax.ShapeDtypeStruct((n_tok, LANE), bf16)]
        + [jax.ShapeDtypeStruct((1, LANE), f32)] * 3,
        [pltpu.VMEM((N_HEADS, SSM_STATE, HEAD_DIM), f32)],
        (xbc_act, xbc_act, xbc_act, proj0, dt_bias, a_log, d_skip, e_heads, states, dy), ride)


ATT_BQ = 512
SB_BQ = 512
SB_TILES_PER_PASS = 4
SB_HEADS_PER_STEP = 2
MLA_HEADS_PER_STEP = 4


def _loop_tiles(n_tiles, per_pass, fn, init):
    def several(i, carry):
        for r in range(per_pass):
            carry = fn(per_pass * i + r, carry)
        return carry

    return lax.fori_loop(0, n_tiles // per_pass, several, init)


def _sb_specs(t, bq, nq):
    w = SB_HEADS_PER_STEP * HEAD_DIM
    qs = lambda off: pl.BlockSpec((bq, w), lambda b, g, i: (b * nq + i, off // w + g))
    ks = lambda off: pl.BlockSpec((t, w), lambda b, g, i: (b, off // w + g))
    return qs, ks


def _sb_mass_spec(bq, nq):
    return pl.BlockSpec((bq, SB_HEADS_PER_STEP * LANE), lambda b, g, i: (b * nq + i, g))


def sb_fwd(proj1, nb, t, ride=None):
    bq = min(SB_BQ, t)
    nq = t // bq
    qs, ks = _sb_specs(t, bq, nq)

    def work(ins, outs, _):
        q_ref, k_ref, v_ref = ins
        o_ref, mass_ref = outs
        q0 = pl.program_id(2) * bq
        n_tiles = (q0 + bq) // SB_KEYS
        hs = range(SB_HEADS_PER_STEP)
        q = [q_ref[:, _head(h)] for h in hs]
        lanes = _iota2((1, LANE), 1)

        def step(i, carry, masked):
            j = n_tiles - 1 - i
            k0 = pl.multiple_of(j * SB_KEYS, SB_KEYS)
            out = []
            for h in hs:
                o, run, kept = carry[h]
                o_t, mass = sb_tile(q[h], k_ref[pl.ds(k0, SB_KEYS), _head(h)], v_ref[pl.ds(k0, SB_KEYS), _head(h)],
                                    run, q0, k0, masked)
                out.append((o + o_t, run + mass, kept + mass * (lanes == j).astype(f32)))
            return out

        diag = bq // SB_KEYS
        res = _loop_tiles(diag, SB_TILES_PER_PASS, functools.partial(step, masked=True),
                          [(jnp.zeros((bq, HEAD_DIM), f32), jnp.zeros((bq, 1), f32), jnp.zeros((bq, LANE), f32))
                           for _ in hs])
        res = _loop_tiles(n_tiles - diag, SB_TILES_PER_PASS, lambda i, cr: step(i + diag, cr, False), res)
        for h in hs:
            o_ref[:, _head(h)] = res[h][0].astype(bf16)
            mass_ref[:, h * LANE:(h + 1) * LANE] = res[h][2]

    return _hosted_call(
        work, "sb_fwd", (nb, N_HEADS // SB_HEADS_PER_STEP, nq), [qs(0), ks(512), ks(1024)],
        [qs(0), _sb_mass_spec(bq, nq)],
        [jax.ShapeDtypeStruct((nb * t, 512), bf16), jax.ShapeDtypeStruct((nb * t, N_HEADS * LANE), f32)],
        [], (proj1, proj1, proj1), ride)


def sb_bwd(proj1, masses, do, nb, t):
    bq = min(SB_BQ, t)
    nq = t // bq
    qs, ks = _sb_specs(t, bq, nq)

    def body(q_ref, k_ref, v_ref, mass_ref, do_ref, dq_ref, dk_ref, dv_ref):
        @pl.when(pl.program_id(2) == 0)
        def _():
            dk_ref[...] = jnp.zeros_like(dk_ref)
            dv_ref[...] = jnp.zeros_like(dv_ref)

        q0 = pl.program_id(2) * bq
        n_tiles = (q0 + bq) // SB_KEYS
        hs = range(SB_HEADS_PER_STEP)
        q = [q_ref[:, _head(h)] for h in hs]
        do = [do_ref[:, _head(h)].astype(f32) for h in hs]
        col0 = jnp.zeros((bq, 1), f32)
        lanes = _iota2((1, LANE), 1)
        run_all = [hdot(mass_ref[:, h * LANE:(h + 1) * LANE], _upper(LANE)) for h in hs]

        def tile(ref, k0, h):
            return ref[pl.ds(k0, SB_KEYS), _head(h)]

        def grads(j, carry, masked):
            k0 = pl.multiple_of(j * SB_KEYS, SB_KEYS)
            pick = (lanes == j).astype(f32)
            out = []
            for h in hs:
                dq, c = carry[h]
                run_in = jnp.sum(run_all[h] * pick, axis=1, keepdims=True)
                _, vjp = jax.vjp(lambda a, b, d, r: sb_tile(a, b, d, r, q0, k0, masked),
                                 q[h], tile(k_ref, k0, h), tile(v_ref, k0, h), run_in)
                dq_t, dk_t, dv_t, drun = vjp((do[h], c))
                dk_ref[pl.ds(k0, SB_KEYS), _head(h)] += dk_t
                dv_ref[pl.ds(k0, SB_KEYS), _head(h)] += dv_t
                out.append((dq + dq_t, drun + c))
            return out

        clear = n_tiles - bq // SB_KEYS
        res = _loop_tiles(clear, SB_TILES_PER_PASS, functools.partial(grads, masked=False),
                          [(jnp.zeros((bq, HEAD_DIM), f32), col0) for _ in hs])
        res = _loop_tiles(bq // SB_KEYS, SB_TILES_PER_PASS, lambda i, cr: grads(i + clear, cr, True), res)
        for h in hs:
            dq_ref[:, _head(h)] = res[h][0]

    return pl.pallas_call(
        body, name="sb_bwd", grid=(nb, N_HEADS // SB_HEADS_PER_STEP, nq),
        in_specs=[qs(0), ks(512), ks(1024), _sb_mass_spec(bq, nq), qs(0)], out_specs=[qs(0), ks(0), ks(0)],
        out_shape=[jax.ShapeDtypeStruct((nb * t, 512), f32)] * 3,
        compiler_params=_cparams(("parallel", "parallel", "arbitrary")),
    )(proj1, proj1, proj1, masses, do)


def _mla_specs(t, bq, nq):
    hp = MLA_HEADS_PER_STEP
    qn = pl.BlockSpec((bq, hp * HEAD_DIM), lambda b, g, i: (b * nq + i, g))
    qr = pl.BlockSpec((bq, hp * MLA_ROPE), lambda b, g, i: (b * nq + i, g))
    kv = pl.BlockSpec((t, hp * 2 * HEAD_DIM), lambda b, g, i: (b, g))
    kp = pl.BlockSpec((t, LANE), lambda b, g, i: (b, 0))
    return qn, qr, kv, kp


def _mla_softmax_pass(qn, qp, kv_ref, kp_ref, q0, n_tiles, bq):
    hs = range(MLA_HEADS_PER_STEP)

    def step(j, carry):
        k0 = pl.multiple_of(j * MLA_KEYS, MLA_KEYS)
        kp = kp_ref[pl.ds(k0, MLA_KEYS), :MLA_ROPE]
        out = []
        for h in hs:
            m, l, acc = carry[h]
            s, _ = mla_scores(qn[h], qp[h], kv_ref[pl.ds(k0, MLA_KEYS), _head(2 * h)], kp, q0, k0)
            m_new = jnp.maximum(m, jnp.max(s, axis=1, keepdims=True))
            alpha, p = jnp.exp(m - m_new), jnp.exp(s - m_new)
            out.append((m_new, alpha * l + jnp.sum(p, axis=1, keepdims=True),
                        alpha * acc + bdot(p, kv_ref[pl.ds(k0, MLA_KEYS), _head(2 * h + 1)])))
        return out

    init = [(jnp.full((bq, 1), -1e30, f32), jnp.zeros((bq, 1), f32), jnp.zeros((bq, HEAD_DIM), f32)) for _ in hs]
    return lax.fori_loop(0, n_tiles, step, init)


def mla_fwd(q_nope, qr, kv, kpe, nb, t):
    bq = min(ATT_BQ, t)
    nq = t // bq
    sqn, sqr, skv, skp = _mla_specs(t, bq, nq)

    def body(qn_ref, qr_ref, kv_ref, kp_ref, o_ref, o32_ref, lse_ref):
        q0 = pl.program_id(2) * bq
        hs = range(MLA_HEADS_PER_STEP)
        qn = [qn_ref[:, _head(h)] for h in hs]
        qp = [qr_ref[:, h * MLA_ROPE:(h + 1) * MLA_ROPE] for h in hs]
        res = _mla_softmax_pass(qn, qp, kv_ref, kp_ref, q0, (q0 + bq) // MLA_KEYS, bq)
        for h in hs:
            m, l, acc = res[h]
            o = acc / l
            o_ref[:, _head(h)] = o.astype(bf16)
            o32_ref[:, _head(h)] = o
            lse_ref[:, _head(h)] = jnp.broadcast_to(m + jnp.log(l), (bq, HEAD_DIM))

    n = nb * t
    return pl.pallas_call(
        body, name="mla_fwd", grid=(nb, N_HEADS // MLA_HEADS_PER_STEP, nq), in_specs=[sqn, sqr, skv, skp],
        out_specs=[sqn, sqn, sqn],
        out_shape=[jax.ShapeDtypeStruct((n, 512), bf16), jax.ShapeDtypeStruct((n, 512), f32),
                   jax.ShapeDtypeStruct((n, 512), f32)],
        compiler_params=_cparams(("parallel", "arbitrary", "arbitrary")),
    )(q_nope, qr, kv, kpe)


def mla_bwd(q_nope, qr, kv, kpe, o32, lse_b, do, nb, t):
    bq = min(ATT_BQ, t)
    nq = t // bq
    sqn, sqr, skv, skp = _mla_specs(t, bq, nq)

    def body(qn_ref, qr_ref, kv_ref, kp_ref, o_ref, lse_ref, do_ref, dqn_ref, dqr_ref, dkv_ref, dkp_ref):
        first_q = pl.program_id(2) == 0

        @pl.when(first_q)
        def _():
            dkv_ref[...] = jnp.zeros_like(dkv_ref)

        @pl.when(jnp.logical_and(first_q, pl.program_id(1) == 0))
        def _():
            dkp_ref[...] = jnp.zeros_like(dkp_ref)

        q0 = pl.program_id(2) * bq
        n_tiles = (q0 + bq) // MLA_KEYS
        hs = range(MLA_HEADS_PER_STEP)
        qn = [qn_ref[:, _head(h)] for h in hs]
        qp = [qr_ref[:, h * MLA_ROPE:(h + 1) * MLA_ROPE] for h in hs]
        do = [do_ref[:, _head(h)].astype(f32) for h in hs]
        lse = [lse_ref[:, h * HEAD_DIM:h * HEAD_DIM + 1] for h in hs]
        dsum = [jnp.sum(do[h] * o_ref[:, _head(h)], axis=1, keepdims=True) for h in hs]

        def grads(j, carry):
            k0 = pl.multiple_of(j * MLA_KEYS, MLA_KEYS)
            rows = pl.ds(k0, MLA_KEYS)
            kp = kp_ref[rows, :MLA_ROPE]
            out = []
            for h in hs:
                dqn, dqp = carry[h]
                g = jax.grad(mla_tile_loss, argnums=(0, 1, 2, 3, 4))(
                    qn[h], qp[h], kv_ref[rows, _head(2 * h)], kp, kv_ref[rows, _head(2 * h + 1)],
                    do[h], lse[h], dsum[h], q0, k0)
                dkv_ref[rows, _head(2 * h)] += g[2]
                dkp_ref[rows, :MLA_ROPE] += g[3]
                dkv_ref[rows, _head(2 * h + 1)] += g[4]
                out.append((dqn + g[0], dqp + g[1]))
            return out

        res = lax.fori_loop(0, n_tiles, grads,
                            [(jnp.zeros((bq, HEAD_DIM), f32), jnp.zeros((bq, MLA_ROPE), f32)) for _ in hs])
        for h in hs:
            dqn_ref[:, _head(h)] = res[h][0]
            dqr_ref[:, h * MLA_ROPE:(h + 1) * MLA_ROPE] = res[h][1]

    n = nb * t
    return pl.pallas_call(
        body, name="mla_bwd", grid=(nb, N_HEADS // MLA_HEADS_PER_STEP, nq),
        in_specs=[sqn, sqr, skv, skp, sqn, sqn, sqn], out_specs=[sqn, sqr, skv, skp],
        out_shape=[jax.ShapeDtypeStruct((n, 512), f32), jax.ShapeDtypeStruct((n, N_HEADS * MLA_ROPE), f32),
                   jax.ShapeDtypeStruct((n, 1024), f32), jax.ShapeDtypeStruct((n, LANE), f32)],
        compiler_params=_cparams(("arbitrary", "arbitrary", "arbitrary")),
    )(q_nope, qr, kv, kpe, o32, lse_b, do)


def loss_head(h, target):
    n, d = h.shape
    tm = _tile(n, 512)

    def body(h_ref, t_ref, l_ref, dh_ref):
        diff = h_ref[...] - t_ref[...]
        dh_ref[...] = diff * (1.0 / d)
        part = 0.5 * jnp.sum(jnp.sum(diff * diff, axis=1, keepdims=True) * (1.0 / d), axis=0, keepdims=True)

        @pl.when(pl.program_id(0) == 0)
        def _():
            l_ref[...] = jnp.zeros_like(l_ref)

        l_ref[...] += jnp.broadcast_to(part, l_ref.shape)

    spec = pl.BlockSpec((tm, d), lambda i: (i, 0))
    return pl.pallas_call(
        body, name="loss_head", grid=(n // tm,), in_specs=[spec, spec],
        out_specs=[pl.BlockSpec((8, LANE), lambda i: (0, 0)), spec],
        out_shape=[jax.ShapeDtypeStruct((8, LANE), f32), jax.ShapeDtypeStruct((n, d), f32)],
        compiler_params=_cparams(("arbitrary",)),
    )(h, target)


def _row(v):
    return v.reshape(1, -1)


def _pad_cols(a, n):
    return jnp.pad(a, ((0, 0), (0, n - a.shape[1])))


def _pad_row(v, n=LANE):
    return jnp.pad(v.reshape(1, -1), ((0, 0), (0, n - v.shape[0])))


def _group_matrix(width, group):
    idx = np.arange(width) // group
    return jnp.asarray((idx[:, None] == idx[None, :]).astype(np.float32))


def _head_expand():
    e = np.zeros((N_HEADS, LANE, HEAD_DIM), np.float32)
    for h in range(N_HEADS):
        e[h, h, :] = 1.0
    return jnp.asarray(e)


def _rope_freqs():
    inv = 1.0 / (ROPE_THETA ** (np.arange(0, MLA_ROPE, 2, dtype=np.float32) / MLA_ROPE))
    inv = np.tile(inv.astype(np.float32), 2)
    inv_q = np.tile(inv, N_HEADS).reshape(1, N_HEADS * MLA_ROPE)
    inv_k = np.zeros((1, LANE), np.float32)
    inv_k[0, :MLA_ROPE] = inv
    return jnp.asarray(inv_q), jnp.asarray(inv_k)


def _uq_split(w):
    w3 = w.reshape(w.shape[0], N_HEADS, HEAD_DIM + MLA_ROPE)
    return w3[:, :, :HEAD_DIM].reshape(-1, 512), w3[:, :, HEAD_DIM:].reshape(-1, N_HEADS * MLA_ROPE)


def _uq_merge(gn, gr):
    r = gn.shape[0]
    return jnp.concatenate([gn.reshape(r, N_HEADS, HEAD_DIM), gr.reshape(r, N_HEADS, MLA_ROPE)], axis=2).reshape(r, 768)


EARLY_GRADS = ['ffn1_w_up', 'ffn1_w_down', 'l1_w_in', 'l1_w_out', 'ffn0_w_up', 'ffn0_w_down', 'l0_w_out']


def local_step(x, positions, target, w, late_weights=None, scatter_early=None):
    w = dict(w)
    nb, t, d = x.shape
    n = nb * t
    tm = 256
    ni = n // tm
    tc = 2 * LANE
    h0 = x.reshape(n, d)
    tgt = target.reshape(n, d)
    pos = positions.reshape(n, 1).astype(f32)
    gh = _group_matrix(512, HEAD_DIM)
    gg = _group_matrix(512, 256)
    e_heads = _head_expand()
    inv_q, inv_k = _rope_freqs()
    g = {}

    def ln_stage(h, y, gname, bname):
        ops = [_rows(h, tm), _rows(y, tm, gdtype=bf16), _param(_row(w[gname])), _param(_row(w[bname]))]
        return ops, [_rows_out(n, d, tm)]

    def ln_fwd(name, ops):
        return block_fwd(lambda *a: f_ln(*a) * 2, name, (1, ni), ops, [_rows_out(n, d, tm), _rows_out(n, d, tm, bf16)])

    def ffn_act_stage(u, cw, cb):
        nj = D_FF // tc
        ops = [_cols(u, t, tc, 0, D_FF, bf16), _cols(u, t, tc, nj, D_FF, bf16)] \
            + [_cparam(cw[i:i + 1], tc) for i in range(3)] + [_cparam(_row(cb), tc)]
        return ops, [Out((n, D_FF), (t, tc), lambda j, i: (i, j), bf16)], (nj, nb)

    w_in0 = _pad_cols(w['l0_w_in'], L0_PAD)
    h0b = h0.astype(bf16)
    proj0 = mm(h0b, w_in0, "l0_proj")

    shift_ops = [_cols(proj0, t, tc, 0, RWKV_COLS, bf16), _cparam(_row(w['rwkv_mix']), tc)]
    shift_outs = [Out((n, RWKV_COLS), (t, tc), lambda j, i: (i, j))]
    shift_grid = (RWKV_COLS // tc, nb)
    (ps,) = block_fwd(f_shift_mix, "rwkv_shift", shift_grid, shift_ops, shift_outs)

    pre_ops = [_colblock(ps, tm, 512, 512), _colblock(ps, tm, 1536, 128), _colblock(ps, tm, 1664, 128),
               _param(_row(w['rwkv_w0'])), _param(w['rwkv_w2']), _param(_row(w['rwkv_a0'])), _param(w['rwkv_a2']),
               _param(w['rwkv_g2']), _param(_row(w['rwkv_k_k'])), _param(_row(w['rwkv_k_a'])), _param(gh, diff=False)]
    pre_outs = [_rows_out(n, 512, tm) for _ in range(5)]
    lw, k2, na, bb, gate_r = block_fwd(f_rwkv_pre, "rwkv_pre", (1, ni), pre_ops, pre_outs)
    def arrived(group, gathered):
        if late_weights is not None:
            for name, got in zip(late_weights[group][0], gathered):
                w[name] = late_weights[2](name, got)

    ride = None if late_weights is None else GatherRide(late_weights[0][1])
    (y_tok, s0_saved), gathered = rwkv_scan_fwd(ps, lw, k2, na, bb, nb, t, ride)
    arrived(0, gathered)
    w_out0 = w['l0_w_out']

    post_ops = [_rows(y_tok, tm), _colblock(ps, tm, 0, 512), _rows(k2, tm), _colblock(ps, tm, 1024, 512),
                _rows(gate_r, tm), _param(_row(w['rwkv_ln_g'])), _param(_row(w['rwkv_ln_b'])),
                _param(w['rwkv_r_k'].reshape(1, 512)), _param(gh, diff=False)]
    post_outs = [_rows_out(n, 512, tm, bf16)]
    (y_a,) = block_fwd(f_rwkv_post, "rwkv_post", (1, ni), post_ops, post_outs)

    xbc_off = (RWKV_COLS + 512) // tc
    conv_ops = [_cols(proj0, t, tc, xbc_off, 1024, bf16)] + [_cparam(w['ssm_conv_w'][i:i + 1], tc) for i in range(4)] \
        + [_cparam(_row(w['ssm_conv_b']), tc)]
    conv_outs = [Out((n, 1024), (t, tc), lambda j, i: (i, j))]
    conv_grid = (1024 // tc, nb)
    (xbc_act,) = block_fwd(f_conv4_silu, "ssm_conv", conv_grid, conv_ops, conv_outs)

    dt_bias, a_log, d_skip = _pad_row(w['ssm_dt_bias']), _pad_row(w['ssm_a_log']), _pad_row(w['ssm_d'])
    y_ssd, ssd_states = ssd_fwd(xbc_act, proj0, dt_bias, a_log, d_skip, e_heads, nb, t)

    z_tok = proj0[:, RWKV_COLS:RWKV_COLS + 512]
    spost_ops = [_rows(y_ssd, tm), _rows(z_tok, tm, gdtype=bf16), _param(_row(w['ssm_norm_g'])), _param(gg, diff=False)]
    spost_outs = [_rows_out(n, 512, tm, bf16)]
    (y_b,) = block_fwd(f_ssm_post, "ssm_post", (1, ni), spost_ops, spost_outs)

    mixed0 = mm(y_b, w_out0[512:], "l0_out_b", add=mm(y_a, w_out0[:512], "l0_out_a"))
    ln1_ops, ln_outs = ln_stage(h0, mixed0, 'l0_ln1_g', 'l0_ln1_b')
    h1, h1b = ln_fwd("l0_ln1", ln1_ops)

    u0 = mm(h1b, w['ffn0_w_up'], "ffn0_up")
    act0_ops, act_outs, act_grid = ffn_act_stage(u0, w['ffn0_conv_w'], w['ffn0_conv_b'])
    (act0,) = block_fwd(f_ffn_act, "ffn0_act", act_grid, act0_ops, act_outs)
    f0 = mm(act0, w['ffn0_w_down'], "ffn0_down")
    ln2_ops, _ = ln_stage(h1, f0, 'l0_ln2_g', 'l0_ln2_b')
    h2, h2b = ln_fwd("l0_ln2", ln2_ops)

    w_in1 = _pad_cols(w['l1_w_in'], L1_PAD)
    proj1 = mm(h2b, w_in1, "l1_proj")
    w_qn, w_qr = _uq_split(w['mla_w_uq'])
    mpre_ops = [_colblock(proj1, tm, 1536, 256, bf16), _colblock(proj1, tm, 1792, 128, bf16),
                _colblock(proj1, tm, 1920, 128, bf16),
                Op(pos, (tm, 1), lambda j, i: (i, 0), diff=False),
                _param(_row(w['mla_q_norm_g'])), _param(w_qn), _param(w_qr),
                _param(_row(w['mla_kv_norm_g'])), _param(w['mla_w_ukv']), _param(inv_q, diff=False),
                _param(inv_k, diff=False)]
    mpre_outs = [_rows_out(n, 512, tm), _rows_out(n, N_HEADS * MLA_ROPE, tm), _rows_out(n, 1024, tm),
                 _rows_out(n, LANE, tm)]
    q_nope, q_rope, kv, kpe = block_fwd(f_mla_pre, "mla_pre", (1, ni), mpre_ops, mpre_outs)
    ride = None if late_weights is None else GatherRide(late_weights[1][1])
    (o_sb, sb_masses), gathered = sb_fwd(proj1, nb, t, ride)
    arrived(1, gathered)
    w_out1 = w['l1_w_out']
    o_mla, o_mla32, mla_lse = mla_fwd(q_nope, q_rope, kv, kpe, nb, t)

    mixed1 = mm(o_mla, w_out1[512:], "l1_out_b", add=mm(o_sb, w_out1[:512], "l1_out_a"))
    ln3_ops, _ = ln_stage(h2, mixed1, 'l1_ln1_g', 'l1_ln1_b')
    h3, h3b = ln_fwd("l1_ln1", ln3_ops)
    u1 = mm(h3b, w['ffn1_w_up'], "ffn1_up")
    act1_ops, _, _ = ffn_act_stage(u1, w['ffn1_conv_w'], w['ffn1_conv_b'])
    (act1,) = block_fwd(f_ffn_act, "ffn1_act", act_grid, act1_ops, act_outs)
    f1 = mm(act1, w['ffn1_w_down'], "ffn1_down")
    ln4_ops, _ = ln_stage(h3, f1, 'l1_ln2_g', 'l1_ln2_b')
    (h4,) = block_fwd(f_ln, "l1_ln2", (1, ni), ln4_ops, ln_outs)

    loss_part, dh4 = loss_head(h4, tgt)

    def vec(a_):
        return a_.reshape(-1)

    def ffn_bwd(tag, dh_out, ln_ops, act_ops, h_in, act, w_up, w_down, names):
        dh_res, df, gg_, gb_ = block_bwd(f_ln, tag + "_ln2_bwd", (1, ni), ln_ops, ln_outs, [dh_out])
        g[names[4]], g[names[5]] = vec(gg_), vec(gb_)
        g[names[3]] = mm(act, df, tag + "_down_dw", ta=True)
        dact = mm(df, w_down.T, tag + "_down_dx")
        dgate, dup, dw0, dw1, dw2, dcb = block_bwd(f_ffn_act, tag + "_act_bwd", act_grid, act_ops, act_outs, [dact])
        g[names[1]] = jnp.concatenate([dw0, dw1, dw2], axis=0)
        g[names[2]] = vec(dcb)
        quarter = 2 * D_FF // N_SHARD
        g[names[0]] = mm(h_in, dup, tag + "_upv_dw", ta=True, pieces=(N_SHARD, 2, quarter),
                         into=mm(h_in, dgate, tag + "_gate_dw", ta=True, pieces=(N_SHARD, 0, quarter)))
        w_up_t = w_up.T
        dh = mm(dgate, w_up_t[:D_FF], tag + "_gate_dx", add=dh_res)
        return mm(dup, w_up_t[D_FF:], tag + "_upv_dx", add=dh)

    def out_bwd(tag, dmixed, y_first, y_second, w_out, name):
        g[name] = jnp.concatenate([mm(y_first, dmixed, tag + "_a_dw", ta=True),
                                   mm(y_second, dmixed, tag + "_b_dw", ta=True)], axis=0)
        w_t = w_out.T
        return mm(dmixed, w_t[:, :512], tag + "_a_dx"), mm(dmixed, w_t[:, 512:], tag + "_b_dx")

    dh3 = ffn_bwd("ffn1", dh4, ln4_ops, act1_ops, h3b, act1, w['ffn1_w_up'], w['ffn1_w_down'],
                  ['ffn1_w_up', 'ffn1_conv_w', 'ffn1_conv_b', 'ffn1_w_down', 'l1_ln2_g', 'l1_ln2_b'])

    dh2_res, dmixed1, g3g, g3b = block_bwd(f_ln, "l1_ln1_bwd", (1, ni), ln3_ops, ln_outs, [dh3])
    g['l1_ln1_g'], g['l1_ln1_b'] = vec(g3g), vec(g3b)
    do_sb, do_mla = out_bwd("l1_out", dmixed1, o_sb, o_mla, w_out1, 'l1_w_out')

    dq_nope, dq_rope, dkv, dkpe = mla_bwd(q_nope, q_rope, kv, kpe, o_mla32, mla_lse, do_mla, nb, t)
    dsb_q, dsb_k, dsb_v = sb_bwd(proj1, sb_masses, do_sb, nb, t)
    (dc_q, dc_kv, dkpe_raw, gqg, gwqn, gwqr, gkvg, g['mla_w_ukv']) = block_bwd(
        f_mla_pre, "mla_pre_bwd", (1, ni), mpre_ops, mpre_outs, [dq_nope, dq_rope, dkv, dkpe])
    g['mla_q_norm_g'], g['mla_kv_norm_g'] = vec(gqg), vec(gkvg)
    g['mla_w_uq'] = _uq_merge(gwqn, gwqr)
    dproj1 = jnp.concatenate([dsb_q.astype(bf16), dsb_k.astype(bf16), dsb_v.astype(bf16), dc_q, dc_kv, dkpe_raw],
                             axis=1)
    g['l1_w_in'] = mm(h2b, dproj1, "l1_proj_dw", ta=True)[:, :L1_COLS]
    dh2 = mm(dproj1, w_in1.T, "l1_proj_dx", add=dh2_res)

    dh1 = ffn_bwd("ffn0", dh2, ln2_ops, act0_ops, h1b, act0, w['ffn0_w_up'], w['ffn0_w_down'],
                  ['ffn0_w_up', 'ffn0_conv_w', 'ffn0_conv_b', 'ffn0_w_down', 'l0_ln2_g', 'l0_ln2_b'])

    dh0_res, dmixed0, g1g, g1b = block_bwd(f_ln, "l0_ln1_bwd", (1, ni), ln1_ops, ln_outs, [dh1])
    g['l0_ln1_g'], g['l0_ln1_b'] = vec(g1g), vec(g1b)
    dy_a, dy_b = out_bwd("l0_out", dmixed0, y_a, y_b, w_out0, 'l0_w_out')

    dy_ssd, dz, gng = block_bwd(f_ssm_post, "ssm_post_bwd", (1, ni), spost_ops, spost_outs, [dy_b])
    g['ssm_norm_g'] = vec(gng)
    early_pieces = None if scatter_early is None else scatter_early[0]({name: g[name] for name in EARLY_GRADS})
    ride = None if scatter_early is None else SwapRide(early_pieces)
    (dxs, dbm, dcm, ddt_raw, gdb, gal, gdsk), from_sibling = ssd_bwd(
        xbc_act, proj0, dt_bias, a_log, d_skip, e_heads, ssd_states, dy_ssd, nb, t, ride)
    g['ssm_dt_bias'], g['ssm_a_log'], g['ssm_d'] = gdb[0, :8], gal[0, :8], gdsk[0, :8]
    dxbc_act = jnp.concatenate([dxs, dbm, dcm], axis=1)
    dxbc, cw0, cw1, cw2, cw3, gcb = block_bwd(f_conv4_silu, "ssm_conv_bwd", conv_grid, conv_ops, conv_outs, [dxbc_act])
    g['ssm_conv_w'] = jnp.concatenate([cw0, cw1, cw2, cw3], axis=0)
    g['ssm_conv_b'] = vec(gcb)

    dy_tok, dr_post, dk2_post, dv_post, dgate, glg, glb, grk = block_bwd(
        f_rwkv_post, "rwkv_post_bwd", (1, ni), post_ops, post_outs, [dy_a])
    g['rwkv_ln_g'], g['rwkv_ln_b'], g['rwkv_r_k'] = vec(glg), vec(glb), grk.reshape(N_HEADS, HEAD_DIM)
    ride = None if scatter_early is None else ScatterRide(scatter_early[1](early_pieces, from_sibling))
    (dr, dlw, dk2, dv, dna, dbb), early = rwkv_scan_bwd(s0_saved, ps, lw, k2, na, bb, dy_tok, nb, t, ride)
    (dk_pre, dwa_lo, dg_lo, gw0, g['rwkv_w2'], ga0, g['rwkv_a2'], g['rwkv_g2'], gkk, gka) = block_bwd(
        f_rwkv_pre, "rwkv_pre_bwd", (1, ni), pre_ops, pre_outs, [dlw, dk2 + dk2_post, dna, dbb, dgate])
    g['rwkv_w0'], g['rwkv_a0'], g['rwkv_k_k'], g['rwkv_k_a'] = vec(gw0), vec(ga0), vec(gkk), vec(gka)
    dps = jnp.concatenate([dr + dr_post, dk_pre, dv + dv_post, dwa_lo, dg_lo], axis=1)
    dp_rwkv, gmix = block_bwd(f_shift_mix, "rwkv_shift_bwd", shift_grid, shift_ops, shift_outs, [dps])
    g['rwkv_mix'] = vec(gmix)

    dproj0 = jnp.concatenate([dp_rwkv, dz, dxbc, ddt_raw], axis=1)
    g['l0_w_in'] = mm(h0b, dproj0, "l0_proj_dw", ta=True)[:, :L0_COLS]
    grad_x = mm(dproj0, w_in0.T, "l0_proj_dx", add=dh0_res)
    return loss_part, grad_x.reshape(nb, t, d), g, early


MESH = pl.DeviceIdType.MESH
ANY = pl.BlockSpec(memory_space=pl.ANY)
AXES = ("x", "y", "c")


def _place():
    x, y, c = lax.axis_index("x"), lax.axis_index("y"), lax.axis_index("c")
    chips = [(1 - x, y), (x, 1 - y), (1 - x, 1 - y)]
    return x, y, c, chips


def _dma_sems(n):
    return pltpu.SemaphoreType.DMA((n,))


class GatherRide:
    def __init__(self, shards):
        n = len(shards)
        self.inputs = list(shards)
        self.out_shapes = [jax.ShapeDtypeStruct((N_SHARD,) + a.shape, a.dtype) for a in shards]
        self.scratch = [_dma_sems(3 * n), _dma_sems(3 * n), _dma_sems(3 * n), _dma_sems(3 * n), _dma_sems(n)]

    def _copies(self, ins, outs, sems):
        ici_send, ici_recv, d2d_send, d2d_recv, local_sems = sems
        x, y, c, chips = _place()
        me = 2 * x + y
        pairs = list(enumerate(zip(ins, outs)))

        def over_ici(k, j, slot, to):
            return pltpu.make_async_remote_copy(
                src_ref=ins[k].at[c], dst_ref=outs[k].at[slot, c], send_sem=ici_send.at[3 * k + j],
                recv_sem=ici_recv.at[3 * k + j], device_id=to, device_id_type=MESH)

        def to_sibling(k, j, slot, half):
            return pltpu.make_async_remote_copy(
                src_ref=outs[k].at[slot, half], dst_ref=outs[k].at[slot, half], send_sem=d2d_send.at[3 * k + j],
                recv_sem=d2d_recv.at[3 * k + j], device_id=(x, y, 1 - c), device_id_type=MESH)

        mine = [pltpu.make_async_copy(a, o.at[me], local_sems.at[k]) for k, (a, o) in pairs]
        sends = [over_ici(k, j, me, (cx, cy, c)) for k, _ in pairs for j, (cx, cy) in enumerate(chips)]
        return c, chips, pairs, over_ici, to_sibling, mine, sends

    def start(self, ins, outs, sems):
        _, _, _, _, _, mine, sends = self._copies(ins, outs, sems)
        for cp in mine + sends:
            cp.start()

    def finish(self, ins, outs, sems):
        c, chips, pairs, over_ici, to_sibling, mine, sends = self._copies(ins, outs, sems)
        passed = []
        for k, _ in pairs:
            for j, (cx, cy) in enumerate(chips):
                over_ici(k, j, 2 * cx + cy, (cx, cy, c)).wait_recv()
                passed.append(to_sibling(k, j, 2 * cx + cy, c))
                passed[-1].start()
        for k, _ in pairs:
            for j, (cx, cy) in enumerate(chips):
                to_sibling(k, j, 2 * cx + cy, 1 - c).wait_recv()
        for cp in sends + passed:
            cp.wait_send()
        for cp in mine:
            cp.wait()


class ScatterRide:
    def __init__(self, parts):
        n = len(parts)
        self.inputs = list(parts)
        self.out_shapes = [jax.ShapeDtypeStruct(a.shape, a.dtype) for a in parts]
        self.scratch = [_dma_sems(3 * n), _dma_sems(3 * n), _dma_sems(n)]

    def _copies(self, ins, outs, sems):
        send_sems, recv_sems, local_sems = sems
        x, y, c, chips = _place()
        me = 2 * x + y
        pairs = list(enumerate(zip(ins, outs)))

        def over_ici(k, j, src_slot, dst_slot, to):
            return pltpu.make_async_remote_copy(
                src_ref=ins[k].at[src_slot], dst_ref=outs[k].at[dst_slot], send_sem=send_sems.at[3 * k + j],
                recv_sem=recv_sems.at[3 * k + j], device_id=to, device_id_type=MESH)

        mine = [pltpu.make_async_copy(a.at[me], o.at[me], local_sems.at[k]) for k, (a, o) in pairs]
        sends = [over_ici(k, j, 2 * cx + cy, me, (cx, cy, c)) for k, _ in pairs for j, (cx, cy) in enumerate(chips)]
        arrivals = lambda: [over_ici(k, j, me, 2 * cx + cy, (cx, cy, c))
                            for k, _ in pairs for j, (cx, cy) in enumerate(chips)]
        return mine, sends, arrivals

    def start(self, ins, outs, sems):
        mine, sends, _ = self._copies(ins, outs, sems)
        for cp in mine + sends:
            cp.start()

    def finish(self, ins, outs, sems):
        mine, sends, arrivals = self._copies(ins, outs, sems)
        for cp in arrivals():
            cp.wait_recv()
        for cp in sends:
            cp.wait_send()
        for cp in mine:
            cp.wait()


def _run_ride(ride, name):
    n = len(ride.inputs)

    def body(*refs):
        ins, outs, sems = refs[:n], refs[n:2 * n], refs[2 * n:]
        ride.start(ins, outs, sems)
        ride.finish(ins, outs, sems)

    return pl.pallas_call(body, name=name, in_specs=[ANY] * n, out_specs=[ANY] * n, out_shape=ride.out_shapes,
                          scratch_shapes=ride.scratch)(*ride.inputs)


def gather_shards(shards):
    return _run_ride(GatherRide(shards), "gather_shards")


class SwapRide:
    def __init__(self, pieces):
        n = len(pieces)
        self.inputs = list(pieces)
        self.out_shapes = [jax.ShapeDtypeStruct((a.shape[0],) + a.shape[2:], a.dtype) for a in pieces]
        self.scratch = [_dma_sems(n), _dma_sems(n)]

    def _copies(self, ins, outs, sems):
        send_sems, recv_sems = sems
        x, y, c, _ = _place()
        return [pltpu.make_async_remote_copy(
            src_ref=a.at[:, 1 - c], dst_ref=o, send_sem=send_sems.at[k], recv_sem=recv_sems.at[k],
            device_id=(x, y, 1 - c), device_id_type=MESH) for k, (a, o) in enumerate(zip(ins, outs))]

    def start(self, ins, outs, sems):
        for cp in self._copies(ins, outs, sems):
            cp.start()

    def finish(self, ins, outs, sems):
        for cp in self._copies(ins, outs, sems):
            cp.wait()


def swap_halves(pieces, name):
    return _run_ride(SwapRide(pieces), name)


def scatter_to_chips(parts):
    return _run_ride(ScatterRide(parts), "scatter_to_chips")


def share_halves(bufs):
    n = len(bufs)

    def body(*refs):
        ins, outs = refs[:n], refs[n:2 * n]
        send_sems, recv_sems = refs[2 * n:]
        x, y, c, _ = _place()
        cps = [pltpu.make_async_remote_copy(
            src_ref=a.at[c], dst_ref=o.at[c], send_sem=send_sems.at[k], recv_sem=recv_sems.at[k],
            device_id=(x, y, 1 - c), device_id_type=MESH) for k, (a, o) in enumerate(zip(ins, outs))]
        for cp in cps:
            cp.start()
        for k, (a, o) in enumerate(zip(ins, outs)):
            cps[k].wait_send()
            pltpu.make_async_remote_copy(
                src_ref=a.at[c], dst_ref=o.at[1 - c], send_sem=send_sems.at[k], recv_sem=recv_sems.at[k],
                device_id=(x, y, 1 - c), device_id_type=MESH).wait_recv()

    return pl.pallas_call(
        body, name="share_halves", in_specs=[ANY] * n, out_specs=[ANY] * n,
        out_shape=[jax.ShapeDtypeStruct(a.shape, a.dtype) for a in bufs],
        input_output_aliases={k: k for k in range(n)},
        scratch_shapes=[_dma_sems(n), _dma_sems(n)],
    )(*bufs)


def pair_add(piece, recv, core, name, out_dtype):
    _, _, h, cdim = piece.shape
    tr = _rtile(h, cdim)

    def body(c_ref, a_ref, b_ref, o_ref):
        o_ref[...] = (a_ref[0] + b_ref[...]).astype(o_ref.dtype)

    spec = pl.BlockSpec((1, tr, cdim), lambda p, i, c_ref: (p, i, 0))
    return pl.pallas_call(
        body, name=name,
        grid_spec=pltpu.PrefetchScalarGridSpec(
            num_scalar_prefetch=1, grid=(N_SHARD, h // tr),
            in_specs=[pl.BlockSpec((1, 1, tr, cdim), lambda p, i, c_ref: (p, c_ref[0], i, 0)), spec],
            out_specs=spec),
        out_shape=jax.ShapeDtypeStruct((N_SHARD, h, cdim), out_dtype),
        compiler_params=_cparams(("parallel", "parallel")),
    )(core, piece, recv)


def chip_add(parts, core, name):
    _, h, cdim = parts.shape
    tr = _rtile(h, cdim, 1024 * 1024)

    def body(c_ref, p_ref, o_ref):
        p = [p_ref[s].astype(f32) for s in range(N_SHARD)]
        o_ref[0] = ((p[0] + p[1]) + p[2]) + p[3]

    return pl.pallas_call(
        body, name=name,
        grid_spec=pltpu.PrefetchScalarGridSpec(
            num_scalar_prefetch=1, grid=(h // tr,),
            in_specs=[pl.BlockSpec((N_SHARD, tr, cdim), lambda i, c_ref: (0, i, 0))],
            out_specs=pl.BlockSpec((1, tr, cdim), lambda i, c_ref: (c_ref[0], i, 0))),
        out_shape=jax.ShapeDtypeStruct((2, h, cdim), f32), compiler_params=_cparams(("parallel",)),
    )(core, parts)


def adamw(w, g, m, v, name):
    rows, cdim = w.shape
    tr = _rtile(rows, cdim, 1024 * 1024)

    def body(w_ref, g_ref, m_ref, v_ref, d_ref, nm_ref, nv_ref):
        gv = g_ref[...]
        m_new = ADAM_B1 * m_ref[...] + (1.0 - ADAM_B1) * gv
        v_new = ADAM_B2 * v_ref[...] + (1.0 - ADAM_B2) * jnp.square(gv)
        m_hat = m_new / (1.0 - ADAM_B1 ** ADAM_STEP)
        v_hat = v_new / (1.0 - ADAM_B2 ** ADAM_STEP)
        d_ref[...] = -ADAM_LR * (m_hat / (jnp.sqrt(v_hat) + ADAM_EPS) + ADAM_WD * w_ref[...])
        nm_ref[...] = m_new
        nv_ref[...] = v_new

    spec = pl.BlockSpec((tr, cdim), lambda i: (i, 0))
    return pl.pallas_call(body, name=name, grid=(rows // tr,), in_specs=[spec] * 4, out_specs=[spec] * 3,
                          out_shape=[jax.ShapeDtypeStruct(w.shape, f32)] * 3,
                          compiler_params=_cparams(("parallel",)))(w, g, m, v)


SMALL_MULTIPLE = 16 * LANE


def _pack_flat(parts, multiple=SMALL_MULTIPLE):
    flat = jnp.concatenate([p.reshape(-1) for p in parts])
    pad = (-flat.shape[0]) % multiple
    return jnp.pad(flat, (0, pad)).reshape(-1, LANE)


def _unpack_flat(buf, shapes):
    flat = buf.reshape(-1)
    out, off = [], 0
    for s in shapes:
        cnt = int(np.prod(s))
        out.append(flat[off:off + cnt].reshape(s))
        off += cnt
    return out


def _full_from_shards(name, gathered):
    if name in COL_SHARDED:
        return jnp.concatenate([gathered[s] for s in range(N_SHARD)], axis=1)
    return gathered.reshape(-1, gathered.shape[2])


def _pieces(name, grad):
    if grad.ndim == 3:
        return grad
    if name in COL_SHARDED:
        r, cdim = grad.shape
        return grad.reshape(r, N_SHARD, cdim // N_SHARD).transpose(1, 0, 2)
    return grad.reshape(N_SHARD, grad.shape[0] // N_SHARD, grad.shape[1])


def _small_pieces(name, grad):
    if name in COL_SHARDED or name in ROW_SHARDED:
        return _pieces(name, grad).reshape(N_SHARD, -1)
    return jnp.broadcast_to(grad.reshape(1, -1), (N_SHARD, grad.size))


def kernel(x, positions, l0_w_in, rwkv_mix, rwkv_w0, rwkv_w2, rwkv_a0, rwkv_a2, rwkv_g2, rwkv_k_k, rwkv_k_a, rwkv_r_k, rwkv_ln_g, rwkv_ln_b, ssm_conv_w, ssm_conv_b, ssm_dt_bias, ssm_a_log, ssm_d, ssm_norm_g, l0_w_out, l0_ln1_g, l0_ln1_b, ffn0_w_up, ffn0_conv_w, ffn0_conv_b, ffn0_w_down, l0_ln2_g, l0_ln2_b, l1_w_in, mla_q_norm_g, mla_w_uq, mla_kv_norm_g, mla_w_ukv, l1_w_out, l1_ln1_g, l1_ln1_b, ffn1_w_up, ffn1_conv_w, ffn1_conv_b, ffn1_w_down, l1_ln2_g, l1_ln2_b, loss_target, m_l0_w_in, m_rwkv_mix, m_rwkv_w0, m_rwkv_w2, m_rwkv_a0, m_rwkv_a2, m_rwkv_g2, m_rwkv_k_k, m_rwkv_k_a, m_rwkv_r_k, m_rwkv_ln_g, m_rwkv_ln_b, m_ssm_conv_w, m_ssm_conv_b, m_ssm_dt_bias, m_ssm_a_log, m_ssm_d, m_ssm_norm_g, m_l0_w_out, m_l0_ln1_g, m_l0_ln1_b, m_ffn0_w_up, m_ffn0_conv_w, m_ffn0_conv_b, m_ffn0_w_down, m_l0_ln2_g, m_l0_ln2_b, m_l1_w_in, m_mla_q_norm_g, m_mla_w_uq, m_mla_kv_norm_g, m_mla_w_ukv, m_l1_w_out, m_l1_ln1_g, m_l1_ln1_b, m_ffn1_w_up, m_ffn1_conv_w, m_ffn1_conv_b, m_ffn1_w_down, m_l1_ln2_g, m_l1_ln2_b, v_l0_w_in, v_rwkv_mix, v_rwkv_w0, v_rwkv_w2, v_rwkv_a0, v_rwkv_a2, v_rwkv_g2, v_rwkv_k_k, v_rwkv_k_a, v_rwkv_r_k, v_rwkv_ln_g, v_rwkv_ln_b, v_ssm_conv_w, v_ssm_conv_b, v_ssm_dt_bias, v_ssm_a_log, v_ssm_d, v_ssm_norm_g, v_l0_w_out, v_l0_ln1_g, v_l0_ln1_b, v_ffn0_w_up, v_ffn0_conv_w, v_ffn0_conv_b, v_ffn0_w_down, v_l0_ln2_g, v_l0_ln2_b, v_l1_w_in, v_mla_q_norm_g, v_mla_w_uq, v_mla_kv_norm_g, v_mla_w_ukv, v_l1_w_out, v_l1_ln1_g, v_l1_ln1_b, v_ffn1_w_up, v_ffn1_conv_w, v_ffn1_conv_b, v_ffn1_w_down, v_l1_ln2_g, v_l1_ln2_b):
    args = locals()
    w_loc = {n: args[n] for n in WEIGHTS}
    m_loc = {n: args["m_" + n] for n in WEIGHTS}
    v_loc = {n: args["v_" + n] for n in WEIGHTS}
    core = lax.axis_index("c").astype(jnp.int32).reshape(1)

    small_sharded = [n for n in SMALL if n in COL_SHARDED]
    halves = lambda a: a.reshape(2, a.shape[0] // 2, a.shape[1])
    whole = lambda name, got: _full_from_shards(name, got.reshape(N_SHARD, -1, got.shape[3]))
    first = gather_shards([halves(w_loc['l0_w_in'].astype(bf16)), halves(_pack_flat([w_loc[n] for n in small_sharded]))])
    w_have = {n: w_loc[n] for n in WEIGHTS if n not in BIG}
    w_have['l0_w_in'] = whole('l0_w_in', first[0])
    small_all = first[1].reshape(N_SHARD, -1, LANE)
    per_shard = [_unpack_flat(small_all[s], [w_loc[n].shape for n in small_sharded]) for s in range(N_SHARD)]
    for k, n in enumerate(small_sharded):
        w_have[n] = jnp.concatenate([per_shard[s][k] for s in range(N_SHARD)], axis=1)
    shards_of = lambda names: (names, [halves(w_loc[n].astype(bf16)) for n in names])
    late = (shards_of(['l0_w_out', 'ffn0_w_up', 'ffn0_w_down', 'l1_w_in']),
            shards_of(['l1_w_out', 'ffn1_w_up', 'ffn1_w_down']), whole)

    def in_halves(pieces):
        return [p.reshape(N_SHARD, 2, p.shape[1] // 2, p.shape[2]) for p in pieces]

    def pair_sums(names, pieces, from_sibling):
        return [pair_add(p, r, core, "pair_add_" + n, f32 if n == 'small' else bf16)
                for n, p, r in zip(names, pieces, from_sibling)]

    loss_part, grad_x, g_full, early = local_step(
        x, positions, loss_target, w_have, late,
        (lambda gd: in_halves([_pieces(n, gd[n]) for n in EARLY_GRADS]),
         lambda pieces, from_sibling: pair_sums(EARLY_GRADS, pieces, from_sibling)))
    loss = lax.psum(loss_part[0, 0], AXES)

    small_flat = jnp.concatenate([_small_pieces(n, g_full[n]) for n in SMALL], axis=1)
    pad = (-small_flat.shape[1]) % SMALL_MULTIPLE
    small_pieces = jnp.pad(small_flat, ((0, 0), (0, pad))).reshape(N_SHARD, -1, LANE)
    rest_pieces = in_halves([_pieces('l0_w_in', g_full['l0_w_in']), small_pieces])
    rest = scatter_to_chips(pair_sums(['l0_w_in', 'small'], rest_pieces, swap_halves(rest_pieces, "swap_halves_rest")))
    from_chips = dict(zip(EARLY_GRADS + ['l0_w_in', 'small'], list(early) + list(rest)))
    units = BIG + ['small']
    both = share_halves([chip_add(from_chips[n], core, "chip_add_" + n) for n in units])
    reduced = [b.reshape(-1, b.shape[2]) for b in both]

    out = {}
    for n, gred in zip(BIG, reduced):
        out[n] = (gred,) + tuple(adamw(w_loc[n], gred, m_loc[n], v_loc[n], "adamw_" + n))
    shapes = [w_loc[n].shape for n in SMALL]
    packs = [_pack_flat([d[n] for n in SMALL]) for d in (w_loc, m_loc, v_loc)]
    small_res = (reduced[-1],) + tuple(adamw(packs[0], reduced[-1], packs[1], packs[2], "adamw_small"))
    small_unpacked = [_unpack_flat(b, shapes) for b in small_res]
    for k, n in enumerate(SMALL):
        out[n] = tuple(u[k] for u in small_unpacked)
    return (loss, grad_x, *[out[n][0] for n in WEIGHTS], *[out[n][1] for n in WEIGHTS],
            *[out[n][2] for n in WEIGHTS], *[out[n][3] for n in WEIGHTS])
```

```python
import functools

import numpy as np
import jax
import jax.numpy as jnp
from jax import lax
from jax.experimental import pallas as pl
from jax.experimental.pallas import tpu as pltpu

f32 = jnp.float32
bf16 = jnp.bfloat16
HI = lax.Precision.HIGHEST
MID = lax.Precision.HIGH

D_MODEL = 1024
HEAD_DIM = 64
N_HEADS = 8
RWKV_COLS = 1792
RWKV_GN_EPS = 64e-5
SSM_STATE = 128
SSM_CHUNK = 128
L0_COLS = 3336
L0_PAD = 3456
L1_COLS = 1952
L1_PAD = 2048
MLA_ROPE = 32
ROPE_THETA = 10000.0
D_FF = 2816
DEPTH = 2
ALPHA = (2 * DEPTH) ** 0.25
ADAM_LR = 0.001
ADAM_B1 = 0.9
ADAM_B2 = 0.999
ADAM_EPS = 1e-08
ADAM_WD = 0.01
ADAM_STEP = 10
RWKV_CHUNK = 64
RWKV_HEADS_PER_STEP = 8
LANE = 128
SUBLANE = 8
VMEM_LIMIT = 56 * 1024 * 1024

WEIGHTS = ['l0_w_in', 'rwkv_mix', 'rwkv_w0', 'rwkv_w2', 'rwkv_a0', 'rwkv_a2', 'rwkv_g2', 'rwkv_k_k', 'rwkv_k_a',
           'rwkv_r_k', 'rwkv_ln_g', 'rwkv_ln_b', 'ssm_conv_w', 'ssm_conv_b', 'ssm_dt_bias', 'ssm_a_log', 'ssm_d',
           'ssm_norm_g', 'l0_w_out', 'l0_ln1_g', 'l0_ln1_b', 'ffn0_w_up', 'ffn0_conv_w', 'ffn0_conv_b',
           'ffn0_w_down', 'l0_ln2_g', 'l0_ln2_b', 'l1_w_in', 'mla_q_norm_g', 'mla_w_uq', 'mla_kv_norm_g',
           'mla_w_ukv', 'l1_w_out', 'l1_ln1_g', 'l1_ln1_b', 'ffn1_w_up', 'ffn1_conv_w', 'ffn1_conv_b',
           'ffn1_w_down', 'l1_ln2_g', 'l1_ln2_b']
COL_SHARDED = ['l0_w_in', 'rwkv_w2', 'rwkv_a2', 'rwkv_g2', 'ssm_conv_w', 'ffn0_w_up', 'ffn0_conv_w', 'l1_w_in',
               'mla_w_uq', 'mla_w_ukv', 'ffn1_w_up', 'ffn1_conv_w']
ROW_SHARDED = ['l0_w_out', 'ffn0_w_down', 'l1_w_out', 'ffn1_w_down']
BIG = ['l0_w_in', 'l0_w_out', 'ffn0_w_up', 'ffn0_w_down', 'l1_w_in', 'l1_w_out', 'ffn1_w_up', 'ffn1_w_down']
SMALL = [n for n in WEIGHTS if n not in BIG]
N_SHARD = 4


def _cparams(sem):
    return pltpu.CompilerParams(dimension_semantics=sem, vmem_limit_bytes=VMEM_LIMIT)


def _dg(a, b, ca, cb, prec=None):
    return lax.dot_general(a, b, (((ca,), (cb,)), ((), ())), precision=prec, preferred_element_type=f32)


def hdot(a, b):
    return _dg(a, b, 1, 0, HI)


def mdot(a, b):
    return _dg(a, b, 1, 0, MID)


def mdot_nt(a, b):
    return _dg(a, b, 1, 1, MID)


def mdot_tn(a, b):
    return _dg(a, b, 0, 0, MID)


def _b(x):
    return x.astype(bf16)


@jax.custom_vjp
def bdot(x, w):
    return _dg(_b(x), _b(w), 1, 0)


def _bdot_fwd(x, w):
    return bdot(x, w), (x, w)


def _bdot_bwd(res, g):
    x, w = res
    return _dg(_b(g), _b(w), 1, 1).astype(x.dtype), _dg(_b(x), _b(g), 0, 0).astype(w.dtype)


bdot.defvjp(_bdot_fwd, _bdot_bwd)


@jax.custom_vjp
def bdot_nt(x, y):
    return _dg(_b(x), _b(y), 1, 1)


def _bdot_nt_fwd(x, y):
    return bdot_nt(x, y), (x, y)


def _bdot_nt_bwd(res, g):
    x, y = res
    return _dg(_b(g), _b(y), 1, 0), _dg(_b(g), _b(x), 0, 0)


bdot_nt.defvjp(_bdot_nt_fwd, _bdot_nt_bwd)


@jax.custom_vjp
def bdot_tn(x, y):
    return _dg(_b(x), _b(y), 0, 0)


def _bdot_tn_fwd(x, y):
    return bdot_tn(x, y), (x, y)


def _bdot_tn_bwd(res, g):
    x, y = res
    return _dg(_b(y), _b(g), 1, 1), _dg(_b(x), _b(g), 1, 0)


bdot_tn.defvjp(_bdot_tn_fwd, _bdot_tn_bwd)


def _sigmoid(x):
    return 1.0 / (1.0 + jnp.exp(-x))


@jax.custom_vjp
def softplus(x):
    e = jnp.exp(-jnp.abs(x))
    u = 1.0 + e
    log1p = jnp.where(u == 1.0, e, jnp.log(u) * e / jnp.where(u == 1.0, 1.0, u - 1.0))
    return jnp.maximum(x, 0.0) + log1p


def _softplus_fwd(x):
    return softplus(x), x


def _softplus_bwd(x, g):
    return (g * _sigmoid(x),)


softplus.defvjp(_softplus_fwd, _softplus_bwd)


@jax.custom_vjp
def softplus_abs(x):
    return jnp.maximum(x, 0.0) + jnp.log(1.0 + jnp.exp(-jnp.abs(x)))


def _softplus_abs_fwd(x):
    return softplus_abs(x), x


softplus_abs.defvjp(_softplus_abs_fwd, _softplus_bwd)


def _two_pass(x, m):
    hi = _b(x)
    lo = _b(x - hi.astype(f32))
    m16 = _b(m)
    return _dg(hi, m16, 1, 0) + _dg(lo, m16, 1, 0)


def _upper(n):
    return (_iota2((n, n), 0) > _iota2((n, n), 1)).astype(f32)


@jax.custom_vjp
def suffix_sums(x):
    return _two_pass(x, _upper(x.shape[1]))


def _suffix_sums_fwd(x):
    return suffix_sums(x), None


def _suffix_sums_bwd(_, g):
    return (_two_pass(g, _upper(g.shape[1]).T),)


suffix_sums.defvjp(_suffix_sums_fwd, _suffix_sums_bwd)


def silu(x):
    return x * _sigmoid(x)


def _shift_rows(x, k, up):
    if k == 0:
        return x
    t = x.shape[0]
    rows = lax.broadcasted_iota(jnp.int32, x.shape, 0)
    if up:
        return jnp.where(rows < t - k, pltpu.roll(x, t - k, 0), 0.0)
    return jnp.where(rows >= k, pltpu.roll(x, k, 0), 0.0)


@functools.partial(jax.custom_vjp, nondiff_argnums=(1,))
def shift_down(x, k):
    return _shift_rows(x, k, False)


def _shift_down_fwd(x, k):
    return _shift_rows(x, k, False), None


def _shift_down_bwd(k, _, g):
    return (_shift_rows(g, k, True),)


shift_down.defvjp(_shift_down_fwd, _shift_down_bwd)


@functools.partial(jax.custom_vjp, nondiff_argnums=(1,))
def lane_roll(x, s):
    return pltpu.roll(x, s % x.shape[1], 1)


def _lane_roll_fwd(x, s):
    return lane_roll(x, s), None


def _lane_roll_bwd(s, _, g):
    return (pltpu.roll(g, (-s) % g.shape[1], 1),)


lane_roll.defvjp(_lane_roll_fwd, _lane_roll_bwd)


def rot_half32(x):
    first = (lax.broadcasted_iota(jnp.int32, x.shape, 1) % MLA_ROPE) < (MLA_ROPE // 2)
    return jnp.where(first, -lane_roll(x, -(MLA_ROPE // 2)), lane_roll(x, MLA_ROPE // 2))


def _iota2(shape, axis):
    return lax.broadcasted_iota(jnp.int32, shape, axis)


class Op:
    def __init__(self, arr, block, imap, diff=True, acc=None, gshape=None, gimap=None, gdtype=f32):
        self.arr, self.block, self.imap, self.diff, self.acc = arr, tuple(block), imap, diff, acc
        self.gshape = tuple(arr.shape) if gshape is None else tuple(gshape)
        self.gimap = imap if gimap is None else gimap
        self.gdtype = gdtype


class Out:
    def __init__(self, shape, block, imap, dtype=f32):
        self.shape, self.block, self.imap, self.dtype = tuple(shape), tuple(block), imap, dtype


def block_fwd(fn, name, grid, ops, outs):
    n_in = len(ops)

    def body(*refs):
        vals = [r[...] for r in refs[:n_in]]
        res = fn(*vals)
        for r, v in zip(refs[n_in:], res):
            r[...] = v.astype(r.dtype)

    res = pl.pallas_call(
        body, name=name, grid=grid,
        in_specs=[pl.BlockSpec(o.block, o.imap) for o in ops],
        out_specs=[pl.BlockSpec(o.block, o.imap) for o in outs],
        out_shape=[jax.ShapeDtypeStruct(o.shape, o.dtype) for o in outs],
        compiler_params=_cparams(("arbitrary", "arbitrary")),
    )(*[o.arr for o in ops])
    return tuple(res)


def block_bwd(fn, name, grid, ops, outs, douts):
    n_in, n_out = len(ops), len(outs)
    dix = [k for k, o in enumerate(ops) if o.diff]

    def body(*refs):
        vals = [r[...] for r in refs[:n_in]]
        dvals = tuple(r[...] for r in refs[n_in:n_in + n_out])
        grefs = refs[n_in + n_out:]

        def f(*d):
            full = list(vals)
            for k, v in zip(dix, d):
                full[k] = v
            return tuple(fn(*full))

        _, vjp = jax.vjp(f, *[vals[k] for k in dix])
        grads = vjp(dvals)
        j, i = pl.program_id(0), pl.program_id(1)
        for k, gref, g in zip(dix, grefs, grads):
            acc = ops[k].acc
            if acc is None:
                gref[...] = g.astype(gref.dtype)
            else:
                first = (i == 0) if acc == 'i' else jnp.logical_and(i == 0, j == 0)

                @pl.when(first)
                def _():
                    gref[...] = g

                @pl.when(jnp.logical_not(first))
                def _():
                    gref[...] += g

    gspecs = [pl.BlockSpec(ops[k].block, ops[k].gimap) for k in dix]
    gshapes = [jax.ShapeDtypeStruct(ops[k].gshape, ops[k].gdtype) for k in dix]
    res = pl.pallas_call(
        body, name=name, grid=grid,
        in_specs=[pl.BlockSpec(o.block, o.imap) for o in ops] + [pl.BlockSpec(o.block, o.imap) for o in outs],
        out_specs=gspecs, out_shape=gshapes,
        compiler_params=_cparams(("arbitrary", "arbitrary")),
    )(*[o.arr for o in ops], *douts)
    return tuple(res)


def _rows(arr, tm, diff=True, gdtype=f32):
    return Op(arr, (tm, arr.shape[1]), lambda j, i: (i, 0), diff=diff, gdtype=gdtype)


def _param(arr, diff=True):
    return Op(arr, arr.shape, lambda j, i: (0,) * arr.ndim, diff=diff, acc='ij')


def _rows_out(n, c, tm, dtype=f32):
    return Out((n, c), (tm, c), lambda j, i: (i, 0), dtype)


def _cols(arr, t, tc, off=0, width=None, gdtype=f32):
    width = arr.shape[1] if width is None else width
    return Op(arr, (t, tc), lambda j, i: (i, j + off), gshape=(arr.shape[0], width), gimap=lambda j, i: (i, j),
              gdtype=gdtype)


def _cparam(arr, tc):
    return Op(arr, (arr.shape[0], tc), lambda j, i: (0, j), acc='i')


def _colblock(arr, tm, off, width, gdtype=f32):
    return Op(arr, (tm, width), lambda j, i: (i, off // width), gshape=(arr.shape[0], width),
              gimap=lambda j, i: (i, 0), gdtype=gdtype)


def _tile(n, cap):
    best = None
    for t in range(LANE, min(n, cap) + 1, LANE):
        if n % t == 0:
            best = t
    return n if best is None else best


def _rtile(rows, cols, cap_bytes=2 * 1024 * 1024):
    best = None
    for t in range(SUBLANE, rows + 1, SUBLANE):
        if rows % t == 0 and t * cols * 4 <= cap_bytes:
            best = t
    return rows if best is None else best


def mm(a, b, name, ta=False, add=None, pieces=None, into=None):
    m = a.shape[1] if ta else a.shape[0]
    kd = a.shape[0] if ta else a.shape[1]
    n = b.shape[1]
    tm, tn = _tile(m, 1408), _tile(n, 1408)
    tk = kd if kd <= 2048 else _tile(kd, 2048)
    nk = kd // tk
    ca = 0 if ta else 1
    n_extra = (add is not None) + (into is not None)

    def body(*refs):
        a_ref, b_ref = refs[:2]
        o_ref, acc = refs[2 + n_extra:]
        k = pl.program_id(2)

        @pl.when(k == 0)
        def _():
            acc[...] = jnp.zeros_like(acc)

        acc[...] += _dg(_b(a_ref[...]), _b(b_ref[...]), ca, 0)

        @pl.when(k == nk - 1)
        def _():
            o_ref[...] = (acc[...] if add is None else acc[...] + refs[2][...]).reshape(o_ref.shape)

    a_spec = pl.BlockSpec((tk, tm), lambda i, j, k: (k, i)) if ta else pl.BlockSpec((tm, tk), lambda i, j, k: (i, k))
    b_spec = pl.BlockSpec((tk, tn), lambda i, j, k: (k, j))
    o_spec = pl.BlockSpec((tm, tn), lambda i, j, k: (i, j))
    out_shape = jax.ShapeDtypeStruct((m, n), f32)
    args, specs, aliases = [a, b], [a_spec, b_spec], {}
    if add is not None:
        args.append(add)
        specs.append(o_spec)
    if pieces is not None:
        count, first, width = pieces
        per = width // tn
        o_spec = pl.BlockSpec((1, tm, tn), lambda i, j, k: (first + j // per, i, j % per))
        out_shape = jax.ShapeDtypeStruct((count, m, width), f32)
        if into is not None:
            aliases = {len(args): 0}
            args.append(into)
            specs.append(pl.BlockSpec(memory_space=pl.ANY))
    return pl.pallas_call(
        body, name=name, grid=(m // tm, n // tn, nk), in_specs=specs, out_specs=o_spec, out_shape=out_shape,
        scratch_shapes=[pltpu.VMEM((tm, tn), f32)], input_output_aliases=aliases,
        compiler_params=_cparams(("parallel", "parallel", "arbitrary")),
    )(*args)


def f_ln(h, y, g, b):
    x = ALPHA * h + y
    mu = jnp.mean(x, axis=-1, keepdims=True)
    xc = x - mu
    var = jnp.mean(xc * xc, axis=-1, keepdims=True)
    return (xc * lax.rsqrt(var + 1e-5) * g + b,)


def f_shift_mix(p, mix):
    return (p + (shift_down(p, 1) - p) * mix,)


def f_rwkv_pre(k, wa_lo, g_lo, w0, w2, a0, a2, g2, k_k, k_a, gh):
    w_lo, a_lo = wa_lo[:, :64], wa_lo[:, 64:]
    log_w = -softplus(-(w0 + bdot(jnp.tanh(w_lo), w2))) - 0.5
    lw = -jnp.exp(log_w)
    a = _sigmoid(a0 + bdot(a_lo, a2))
    g = bdot(_sigmoid(g_lo), g2)
    kk = k * k_k
    kk = kk / jnp.maximum(jnp.sqrt(mdot(kk * kk, gh)), 1e-12)
    k2 = k * (1.0 + (a - 1.0) * k_a)
    return lw, k2, -kk, kk * a, g


def f_rwkv_post(y, r, k2, v, g, ln_g, ln_b, r_k, gh):
    mu = mdot(y, gh) * (1.0 / HEAD_DIM)
    yc = y - mu
    var = mdot(yc * yc, gh) * (1.0 / HEAD_DIM)
    yn = yc * lax.rsqrt(var + RWKV_GN_EPS) * ln_g + ln_b
    bonus = mdot(r * k2 * r_k, gh) * v
    return ((yn + bonus) * g,)


def f_conv4_silu(x, w0, w1, w2, w3, b):
    y = b + shift_down(x, 3) * w0 + shift_down(x, 2) * w1 + shift_down(x, 1) * w2 + x * w3
    return (silu(y),)


def f_ssm_post(y, z, norm_g, gg):
    u = y * silu(z)
    ms = mdot(u * u, gg) * (1.0 / 256.0)
    return (u * lax.rsqrt(ms + 1e-5) * norm_g,)


def f_ffn_act(gate, up, w0, w1, w2, b):
    gc = b + shift_down(gate, 2) * w0 + shift_down(gate, 1) * w1 + gate * w2
    return (silu(gc) * up,)


def _rms(x, g, eps=1e-6):
    return x * lax.rsqrt(jnp.mean(x * x, axis=-1, keepdims=True) + eps) * g


def f_mla_pre(c_q, c_kv, kpe, pos, q_g, w_qn, w_qr, kv_g, w_ukv, inv_q, inv_k):
    qn_in = _rms(c_q, q_g)
    q_nope = bdot(qn_in, w_qn)
    qr = bdot(qn_in, w_qr)
    kv = bdot(_rms(c_kv, kv_g), w_ukv)
    ang_q = pos * inv_q
    ang_k = pos * inv_k
    return (q_nope, qr * jnp.cos(ang_q) + rot_half32(qr) * jnp.sin(ang_q), kv,
            kpe * jnp.cos(ang_k) + rot_half32(kpe) * jnp.sin(ang_k))


def rwkv_chunk(s0, r, lw, k, v, a, b):
    hs = range(len(r))
    l = r[0].shape[0]
    ri, ci = _iota2((l, l), 0), _iota2((l, l), 1)
    strict, incl = ri > ci, ri >= ci
    tri, eye = incl.astype(f32), (ri == ci).astype(f32)
    last = (_iota2((l, 1), 0) == l - 1).astype(f32)
    c = [hdot(tri, lw[h]) for h in hs]
    at = [a[h] * jnp.exp(c[h] - lw[h]) for h in hs]
    wi = [jnp.exp(-c[h]) for h in hs]
    bt = [b[h] * wi[h] for h in hs]
    kt = [k[h] * wi[h] for h in hs]
    rt = [r[h] * jnp.exp(c[h]) for h in hs]
    nab = [jnp.where(strict, mdot_nt(at[h], bt[h]), 0.0) for h in hs]
    nak = [jnp.where(strict, bdot_nt(at[h], kt[h]), 0.0) for h in hs]
    g = [bdot_nt(at[h], s0[h]) + bdot(nak[h], v[h]) for h in hs]
    x = [eye + nab[h] for h in hs]
    p = [mdot(nab[h], nab[h]) for h in hs]
    steps = max(1, (l - 1).bit_length()) - 1
    for it in range(steps):
        x = [x[h] + mdot(p[h], x[h]) for h in hs]
        if it < steps - 1:
            p = [mdot(p[h], p[h]) for h in hs]
    u = [mdot(x[h], g[h]) for h in hs]
    mrb = [jnp.where(incl, bdot_nt(rt[h], bt[h]), 0.0) for h in hs]
    mrk = [jnp.where(incl, bdot_nt(rt[h], kt[h]), 0.0) for h in hs]
    y = [bdot_nt(rt[h], s0[h]) + bdot(mrb[h], u[h]) + bdot(mrk[h], v[h]) for h in hs]
    s1 = [(s0[h] + bdot_tn(u[h], bt[h]) + bdot_tn(v[h], kt[h])) * jnp.exp(jnp.sum(c[h] * last, axis=0, keepdims=True))
          for h in hs]
    return y, s1


def ssd_chunk(xs, bm, cm, dt_raw, s_in, dt_bias, a_log, d_skip, e_heads):
    l = xs.shape[0]
    ri, ci = _iota2((l, l), 0), _iota2((l, l), 1)
    incl = ri >= ci
    tri = incl.astype(f32)
    dt = softplus(dt_raw + dt_bias)
    a128 = dt * (-jnp.exp(a_log))
    lane0 = (_iota2((1, HEAD_DIM), 1) == 0).astype(f32)
    last = (_iota2((l, 1), 0) == l - 1).astype(f32)
    hs = range(N_HEADS)
    group = lambda m, g: m[:, g * SSM_STATE:(g + 1) * SSM_STATE]
    cb = [bdot_nt(group(cm, g), group(bm, g)) for g in range(2)]
    e_all = jnp.concatenate(e_heads, axis=1)
    dt_all = mdot(dt, e_all)
    ac_all = hdot(tri, hdot(a128, e_all))
    xd_all = xs * dt_all
    skip_all = xs * mdot(jnp.broadcast_to(d_skip, (l, LANE)), e_all)
    ac = [ac_all[:, _head(h)] for h in hs]
    xd = [xd_all[:, _head(h)] for h in hs]
    col = [jnp.broadcast_to(jnp.sum(ac[h] * lane0, axis=1, keepdims=True), (l, l)) for h in hs]
    decay = [jnp.exp(jnp.where(incl, col[h] - col[h].T, -1e30)) for h in hs]
    y_diag = [bdot(cb[h // 4] * decay[h], xd[h]) for h in hs]
    a_tot = [jnp.sum(ac[h] * last, axis=0, keepdims=True) for h in hs]
    y_off = [jnp.exp(ac[h]) * bdot(group(cm, h // 4), s_in[h]) for h in hs]
    s_out = [jnp.exp(a_tot[h]) * s_in[h] + bdot_tn(group(bm, h // 4), xd[h] * jnp.exp(a_tot[h] - ac[h])) for h in hs]
    return jnp.concatenate([y_diag[h] + y_off[h] for h in hs], axis=1) + skip_all, s_out


SB_KEYS = LANE
MLA_KEYS = 256


def sb_tile(q, k, v, run, q0, k0, masked=True):
    bq, kb = q.shape[0], k.shape[0]
    z = bdot_nt(q, k) * HEAD_DIM ** -0.5
    if masked:
        strict = (k0 + _iota2((bq, kb), 1)) < (q0 + _iota2((bq, kb), 0))
        lk = jnp.where(strict, -softplus_abs(z), 0.0)
        log_att = z + lk + suffix_sums(lk) + run
        att = jnp.where(strict, jnp.exp(jnp.where(strict, log_att, 0.0)), 0.0)
    else:
        lk = -softplus_abs(z)
        att = jnp.exp(z + lk + suffix_sums(lk) + run)
    return bdot(att, v), jnp.sum(lk, axis=1, keepdims=True)


def mla_scores(qn, qp, kn, kp, q0, k0):
    bq, kb = qn.shape[0], kn.shape[0]
    s = (bdot_nt(qn, kn) + bdot_nt(qp, kp)) * (HEAD_DIM + MLA_ROPE) ** -0.5
    causal = (k0 + _iota2((bq, kb), 1)) <= (q0 + _iota2((bq, kb), 0))
    return jnp.where(causal, s, -1e30), causal


def mla_tile_loss(qn, qp, kn, kp, v, do, lse, dsum, q0, k0):
    s, causal = mla_scores(qn, qp, kn, kp, q0, k0)
    p = jnp.where(causal, jnp.exp(s - lse), 0.0)
    return jnp.sum(do * bdot(p, v)) - jnp.sum(dsum * jnp.sum(p, axis=1, keepdims=True))


def _head(h):
    return slice(h * HEAD_DIM, (h + 1) * HEAD_DIM)


def _rwkv_specs(nc, rev):
    hp = RWKV_HEADS_PER_STEP
    w = hp * HEAD_DIM
    chunk = (lambda c: nc - 1 - c) if rev else (lambda c: c)
    tok = lambda off: pl.BlockSpec((RWKV_CHUNK, w), lambda b, g, c: (b * nc + chunk(c), off // w + g))
    st = pl.BlockSpec((1, hp, HEAD_DIM, HEAD_DIM), lambda b, g, c: ((b * (N_HEADS // hp) + g) * nc + chunk(c), 0, 0, 0))
    return tok, st


def _hosted_call(work, name, grid, in_specs, out_specs, out_shape, scratch, args, ride):
    n_in, n_out, n_scr = len(in_specs), len(out_specs), len(scratch)
    k = 0 if ride is None else len(ride.inputs)

    def body(*refs):
        ins, r_in = refs[:n_in], refs[n_in:n_in + k]
        outs, r_out = refs[n_in + k:n_in + k + n_out], refs[n_in + k + n_out:n_in + 2 * k + n_out]
        scr, r_sems = refs[n_in + 2 * k + n_out:n_in + 2 * k + n_out + n_scr], refs[n_in + 2 * k + n_out + n_scr:]
        ids = [pl.program_id(a) for a in range(len(grid))]
        if ride is not None:
            @pl.when(functools.reduce(jnp.logical_and, [i == 0 for i in ids]))
            def _():
                ride.start(r_in, r_out, r_sems)

        work(ins, outs, scr)
        if ride is not None:
            @pl.when(functools.reduce(jnp.logical_and, [i == g - 1 for i, g in zip(ids, grid)]))
            def _():
                ride.finish(r_in, r_out, r_sems)

    res = pl.pallas_call(
        body, name=name, grid=grid, in_specs=list(in_specs) + [ANY] * k, out_specs=list(out_specs) + [ANY] * k,
        out_shape=list(out_shape) + ([] if ride is None else ride.out_shapes),
        scratch_shapes=list(scratch) + ([] if ride is None else ride.scratch),
        compiler_params=_cparams(("arbitrary",) * len(grid)),
    )(*args, *([] if ride is None else ride.inputs))
    return res[:n_out], res[n_out:]


def rwkv_scan_fwd(ps, lw, k2, na, bb, nb, t, ride=None):
    hp, nc = RWKV_HEADS_PER_STEP, t // RWKV_CHUNK
    ng = N_HEADS // hp
    tok, st = _rwkv_specs(nc, False)

    def work(ins, outs, scr):
        r_ref, v_ref, lw_ref, k_ref, a_ref, b_ref = ins
        y_ref, s0_ref = outs
        (s,) = scr

        @pl.when(pl.program_id(2) == 0)
        def _():
            s[...] = jnp.zeros_like(s)

        s0_ref[0] = s[...]
        heads = lambda ref: [ref[:, _head(h)] for h in range(hp)]
        y, s1 = rwkv_chunk([s[h] for h in range(hp)], heads(r_ref), heads(lw_ref), heads(k_ref), heads(v_ref),
                           heads(a_ref), heads(b_ref))
        for h in range(hp):
            y_ref[:, _head(h)] = y[h]
            s[h] = s1[h]

    return _hosted_call(
        work, "rwkv_scan_fwd", (nb, ng, nc), [tok(0), tok(1024), tok(0), tok(0), tok(0), tok(0)], [tok(0), st],
        [jax.ShapeDtypeStruct((nb * t, N_HEADS * HEAD_DIM), f32),
         jax.ShapeDtypeStruct((nb * ng * nc, hp, HEAD_DIM, HEAD_DIM), f32)],
        [pltpu.VMEM((hp, HEAD_DIM, HEAD_DIM), f32)], (ps, ps, lw, k2, na, bb), ride)


def rwkv_scan_bwd(s0, ps, lw, k2, na, bb, dy, nb, t, ride=None):
    hp, nc = RWKV_HEADS_PER_STEP, t // RWKV_CHUNK
    ng = N_HEADS // hp
    tok, st = _rwkv_specs(nc, True)

    def work(ins, outs, scr):
        s0_ref, r_ref, v_ref, lw_ref, k_ref, a_ref, b_ref, dy_ref = ins
        (ds,) = scr

        @pl.when(pl.program_id(2) == 0)
        def _():
            ds[...] = jnp.zeros_like(ds)

        heads = lambda ref: [ref[:, _head(h)] for h in range(hp)]
        _, vjp = jax.vjp(rwkv_chunk, [s0_ref[0, h] for h in range(hp)], heads(r_ref), heads(lw_ref), heads(k_ref),
                         heads(v_ref), heads(a_ref), heads(b_ref))
        g = vjp((heads(dy_ref), [ds[h] for h in range(hp)]))
        for h in range(hp):
            ds[h] = g[0][h]
            for ref, val in zip(outs, g[1:]):
                ref[:, _head(h)] = val[h]

    return _hosted_call(
        work, "rwkv_scan_bwd", (nb, ng, nc), [st, tok(0), tok(1024), tok(0), tok(0), tok(0), tok(0), tok(0)],
        [tok(0)] * 6, [jax.ShapeDtypeStruct((nb * t, N_HEADS * HEAD_DIM), f32)] * 6,
        [pltpu.VMEM((hp, HEAD_DIM, HEAD_DIM), f32)], (s0, ps, ps, lw, k2, na, bb, dy), ride)


def _ssd_specs(nb, nch, rev):
    def row(b, c):
        return b * nch + (nch - 1 - c if rev else c)

    l = SSM_CHUNK
    xs = pl.BlockSpec((l, 512), lambda b, c: (row(b, c), 0))
    bm = pl.BlockSpec((l, 256), lambda b, c: (row(b, c), 2))
    cm = pl.BlockSpec((l, 256), lambda b, c: (row(b, c), 3))
    dt = pl.BlockSpec((l, LANE), lambda b, c: (row(b, c), (L0_PAD - LANE) // LANE))
    st = pl.BlockSpec((1, 1, N_HEADS, SSM_STATE, HEAD_DIM), lambda b, c: (b, (nch - 1 - c if rev else c), 0, 0, 0))
    par = pl.BlockSpec((1, LANE), lambda b, c: (0, 0))
    eh = pl.BlockSpec((N_HEADS, LANE, HEAD_DIM), lambda b, c: (0, 0, 0))
    return xs, bm, cm, dt, st, par, eh, row


def ssd_fwd(xbc_act, proj0, dt_bias, a_log, d_skip, e_heads, nb, t):
    nch = t // SSM_CHUNK
    n_tok = nb * t
    xs, bm, cm, dt, st, par, eh, row = _ssd_specs(nb, nch, False)

    def body(x_ref, b_ref, c_ref, dt_ref, db_ref, al_ref, dsk_ref, e_ref, y_ref, st_ref, s):
        @pl.when(pl.program_id(1) == 0)
        def _():
            s[...] = jnp.zeros_like(s)

        st_ref[0, 0] = s[...]
        y, s_out = ssd_chunk(x_ref[...], b_ref[...], c_ref[...], dt_ref[...], [s[h] for h in range(N_HEADS)],
                             db_ref[...], al_ref[...], dsk_ref[...], [e_ref[h] for h in range(N_HEADS)])
        y_ref[...] = y
        for h in range(N_HEADS):
            s[h] = s_out[h]

    return pl.pallas_call(
        body, name="ssd_fwd", grid=(nb, nch), in_specs=[xs, bm, cm, dt, par, par, par, eh],
        out_specs=[pl.BlockSpec((SSM_CHUNK, 512), lambda b, c: (row(b, c), 0)), st],
        out_shape=[jax.ShapeDtypeStruct((n_tok, 512), f32),
                   jax.ShapeDtypeStruct((nb, nch, N_HEADS, SSM_STATE, HEAD_DIM), f32)],
        scratch_shapes=[pltpu.VMEM((N_HEADS, SSM_STATE, HEAD_DIM), f32)],
        compiler_params=_cparams(("arbitrary", "arbitrary")),
    )(xbc_act, xbc_act, xbc_act, proj0, dt_bias, a_log, d_skip, e_heads)


def ssd_bwd(xbc_act, proj0, dt_bias, a_log, d_skip, e_heads, states, dy, nb, t, ride=None):
    nch = t // SSM_CHUNK
    n_tok = nb * t
    xs, bm, cm, dt, st, par, eh, row = _ssd_specs(nb, nch, True)

    def work(ins, outs, scr):
        x_ref, b_ref, c_ref, dt_ref, db_ref, al_ref, dsk_ref, e_ref, st_ref, dy_ref = ins
        dx_ref, dbm_ref, dcm_ref, ddt_ref, ddb_ref, dal_ref, ddsk_ref = outs
        (ds,) = scr
        first = jnp.logical_and(pl.program_id(0) == 0, pl.program_id(1) == 0)

        @pl.when(pl.program_id(1) == 0)
        def _():
            ds[...] = jnp.zeros_like(ds)

        e_list = [e_ref[h] for h in range(N_HEADS)]

        def f(x, bmv, cmv, dtr, s_in, dbv, alv, dskv):
            return ssd_chunk(x, bmv, cmv, dtr, s_in, dbv, alv, dskv, e_list)

        _, vjp = jax.vjp(f, x_ref[...], b_ref[...], c_ref[...], dt_ref[...],
                         [st_ref[0, 0, h] for h in range(N_HEADS)], db_ref[...], al_ref[...], dsk_ref[...])
        g = vjp((dy_ref[...], [ds[h] for h in range(N_HEADS)]))
        dx_ref[...], dbm_ref[...], dcm_ref[...], ddt_ref[...] = g[0], g[1], g[2], g[3].astype(bf16)
        for h in range(N_HEADS):
            ds[h] = g[4][h]
        for ref, val in zip((ddb_ref, dal_ref, ddsk_ref), g[5:]):
            @pl.when(first)
            def _():
                ref[...] = val

            @pl.when(jnp.logical_not(first))
            def _():
                ref[...] += val

    rows_spec = lambda w: pl.BlockSpec((SSM_CHUNK, w), lambda b, c: (row(b, c), 0))
    return _hosted_call(
        work, "ssd_bwd", (nb, nch), [xs, bm, cm, dt, par, par, par, eh, st, rows_spec(512)],
        [rows_spec(512), rows_spec(256), rows_spec(256), rows_spec(LANE), par, par, par],
        [jax.ShapeDtypeStruct((n_tok, 512), f32), jax.ShapeDtypeStruct((n_tok, 256), f32),
         jax.ShapeDtypeStruct((n_tok, 256), f32), jax.ShapeDtypeStruct((n_tok, LANE), bf16)]
        + [jax.ShapeDtypeStruct((1, LANE), f32)] * 3,
        [pltpu.VMEM((N_HEADS, SSM_STATE, HEAD_DIM), f32)],
        (xbc_act, xbc_act, xbc_act, proj0, dt_bias, a_log, d_skip, e_heads, states, dy), ride)


ATT_BQ = 512
SB_BQ = 512
SB_TILES_PER_PASS = 4
SB_HEADS_PER_STEP = 2
MLA_HEADS_PER_STEP = 4


def _loop_tiles(n_tiles, per_pass, fn, init):
    def several(i, carry):
        for r in range(per_pass):
            carry = fn(per_pass * i + r, carry)
        return carry

    return lax.fori_loop(0, n_tiles // per_pass, several, init)


def _sb_specs(t, bq, nq):
    w = SB_HEADS_PER_STEP * HEAD_DIM
    qs = lambda off: pl.BlockSpec((bq, w), lambda b, g, i: (b * nq + i, off // w + g))
    ks = lambda off: pl.BlockSpec((t, w), lambda b, g, i: (b, off // w + g))
    return qs, ks


def _sb_mass_spec(bq, nq):
    return pl.BlockSpec((bq, SB_HEADS_PER_STEP * LANE), lambda b, g, i: (b * nq + i, g))


def sb_fwd(proj1, nb, t, ride=None):
    bq = min(SB_BQ, t)
    nq = t // bq
    qs, ks = _sb_specs(t, bq, nq)

    def work(ins, outs, _):
        q_ref, k_ref, v_ref = ins
        o_ref, mass_ref = outs
        q0 = pl.program_id(2) * bq
        n_tiles = (q0 + bq) // SB_KEYS
        hs = range(SB_HEADS_PER_STEP)
        q = [q_ref[:, _head(h)] for h in hs]
        lanes = _iota2((1, LANE), 1)

        def step(i, carry, masked):
            j = n_tiles - 1 - i
            k0 = pl.multiple_of(j * SB_KEYS, SB_KEYS)
            out = []
            for h in hs:
                o, run, kept = carry[h]
                o_t, mass = sb_tile(q[h], k_ref[pl.ds(k0, SB_KEYS), _head(h)], v_ref[pl.ds(k0, SB_KEYS), _head(h)],
                                    run, q0, k0, masked)
                out.append((o + o_t, run + mass, kept + mass * (lanes == j).astype(f32)))
            return out

        diag = bq // SB_KEYS
        res = _loop_tiles(diag, SB_TILES_PER_PASS, functools.partial(step, masked=True),
                          [(jnp.zeros((bq, HEAD_DIM), f32), jnp.zeros((bq, 1), f32), jnp.zeros((bq, LANE), f32))
                           for _ in hs])
        res = _loop_tiles(n_tiles - diag, SB_TILES_PER_PASS, lambda i, cr: step(i + diag, cr, False), res)
        for h in hs:
            o_ref[:, _head(h)] = res[h][0].astype(bf16)
            mass_ref[:, h * LANE:(h + 1) * LANE] = res[h][2]

    return _hosted_call(
        work, "sb_fwd", (nb, N_HEADS // SB_HEADS_PER_STEP, nq), [qs(0), ks(512), ks(1024)],
        [qs(0), _sb_mass_spec(bq, nq)],
        [jax.ShapeDtypeStruct((nb * t, 512), bf16), jax.ShapeDtypeStruct((nb * t, N_HEADS * LANE), f32)],
        [], (proj1, proj1, proj1), ride)


def sb_bwd(proj1, masses, do, nb, t):
    bq = min(SB_BQ, t)
    nq = t // bq
    qs, ks = _sb_specs(t, bq, nq)

    def body(q_ref, k_ref, v_ref, mass_ref, do_ref, dq_ref, dk_ref, dv_ref):
        @pl.when(pl.program_id(2) == 0)
        def _():
            dk_ref[...] = jnp.zeros_like(dk_ref)
            dv_ref[...] = jnp.zeros_like(dv_ref)

        q0 = pl.program_id(2) * bq
        n_tiles = (q0 + bq) // SB_KEYS
        hs = range(SB_HEADS_PER_STEP)
        q = [q_ref[:, _head(h)] for h in hs]
        do = [do_ref[:, _head(h)].astype(f32) for h in hs]
        col0 = jnp.zeros((bq, 1), f32)
        lanes = _iota2((1, LANE), 1)
        run_all = [hdot(mass_ref[:, h * LANE:(h + 1) * LANE], _upper(LANE)) for h in hs]

        def tile(ref, k0, h):
            return ref[pl.ds(k0, SB_KEYS), _head(h)]

        def grads(j, carry, masked):
            k0 = pl.multiple_of(j * SB_KEYS, SB_KEYS)
            pick = (lanes == j).astype(f32)
            out = []
            for h in hs:
                dq, c = carry[h]
                run_in = jnp.sum(run_all[h] * pick, axis=1, keepdims=True)
                _, vjp = jax.vjp(lambda a, b, d, r: sb_tile(a, b, d, r, q0, k0, masked),
                                 q[h], tile(k_ref, k0, h), tile(v_ref, k0, h), run_in)
                dq_t, dk_t, dv_t, drun = vjp((do[h], c))
                dk_ref[pl.ds(k0, SB_KEYS), _head(h)] += dk_t
                dv_ref[pl.ds(k0, SB_KEYS), _head(h)] += dv_t
                out.append((dq + dq_t, drun + c))
            return out

        clear = n_tiles - bq // SB_KEYS
        res = _loop_tiles(clear, SB_TILES_PER_PASS, functools.partial(grads, masked=False),
                          [(jnp.zeros((bq, HEAD_DIM), f32), col0) for _ in hs])
        res = _loop_tiles(bq // SB_KEYS, SB_TILES_PER_PASS, lambda i, cr: grads(i + clear, cr, True), res)
        for h in hs:
            dq_ref[:, _head(h)] = res[h][0]

    return pl.pallas_call(
        body, name="sb_bwd", grid=(nb, N_HEADS // SB_HEADS_PER_STEP, nq),
        in_specs=[qs(0), ks(512), ks(1024), _sb_mass_spec(bq, nq), qs(0)], out_specs=[qs(0), ks(0), ks(0)],
        out_shape=[jax.ShapeDtypeStruct((nb * t, 512), f32)] * 3,
        compiler_params=_cparams(("parallel", "parallel", "arbitrary")),
    )(proj1, proj1, proj1, masses, do)


def _mla_specs(t, bq, nq):
    hp = MLA_HEADS_PER_STEP
    qn = pl.BlockSpec((bq, hp * HEAD_DIM), lambda b, g, i: (b * nq + i, g))
    qr = pl.BlockSpec((bq, hp * MLA_ROPE), lambda b, g, i: (b * nq + i, g))
    kv = pl.BlockSpec((t, hp * 2 * HEAD_DIM), lambda b, g, i: (b, g))
    kp = pl.BlockSpec((t, LANE), lambda b, g, i: (b, 0))
    return qn, qr, kv, kp


def _mla_softmax_pass(qn, qp, kv_ref, kp_ref, q0, n_tiles, bq):
    hs = range(MLA_HEADS_PER_STEP)

    def step(j, carry):
        k0 = pl.multiple_of(j * MLA_KEYS, MLA_KEYS)
        kp = kp_ref[pl.ds(k0, MLA_KEYS), :MLA_ROPE]
        out = []
        for h in hs:
            m, l, acc = carry[h]
            s, _ = mla_scores(qn[h], qp[h], kv_ref[pl.ds(k0, MLA_KEYS), _head(2 * h)], kp, q0, k0)
            m_new = jnp.maximum(m, jnp.max(s, axis=1, keepdims=True))
            alpha, p = jnp.exp(m - m_new), jnp.exp(s - m_new)
            out.append((m_new, alpha * l + jnp.sum(p, axis=1, keepdims=True),
                        alpha * acc + bdot(p, kv_ref[pl.ds(k0, MLA_KEYS), _head(2 * h + 1)])))
        return out

    init = [(jnp.full((bq, 1), -1e30, f32), jnp.zeros((bq, 1), f32), jnp.zeros((bq, HEAD_DIM), f32)) for _ in hs]
    return lax.fori_loop(0, n_tiles, step, init)


def mla_fwd(q_nope, qr, kv, kpe, nb, t):
    bq = min(ATT_BQ, t)
    nq = t // bq
    sqn, sqr, skv, skp = _mla_specs(t, bq, nq)

    def body(qn_ref, qr_ref, kv_ref, kp_ref, o_ref, o32_ref, lse_ref):
        q0 = pl.program_id(2) * bq
        hs = range(MLA_HEADS_PER_STEP)
        qn = [qn_ref[:, _head(h)] for h in hs]
        qp = [qr_ref[:, h * MLA_ROPE:(h + 1) * MLA_ROPE] for h in hs]
        res = _mla_softmax_pass(qn, qp, kv_ref, kp_ref, q0, (q0 + bq) // MLA_KEYS, bq)
        for h in hs:
            m, l, acc = res[h]
            o = acc / l
            o_ref[:, _head(h)] = o.astype(bf16)
            o32_ref[:, _head(h)] = o
            lse_ref[:, _head(h)] = jnp.broadcast_to(m + jnp.log(l), (bq, HEAD_DIM))

    n = nb * t
    return pl.pallas_call(
        body, name="mla_fwd", grid=(nb, N_HEADS // MLA_HEADS_PER_STEP, nq), in_specs=[sqn, sqr, skv, skp],
        out_specs=[sqn, sqn, sqn],
        out_shape=[jax.ShapeDtypeStruct((n, 512), bf16), jax.ShapeDtypeStruct((n, 512), f32),
                   jax.ShapeDtypeStruct((n, 512), f32)],
        compiler_params=_cparams(("parallel", "arbitrary", "arbitrary")),
    )(q_nope, qr, kv, kpe)


def mla_bwd(q_nope, qr, kv, kpe, o32, lse_b, do, nb, t):
    bq = min(ATT_BQ, t)
    nq = t // bq
    sqn, sqr, skv, skp = _mla_specs(t, bq, nq)

    def body(qn_ref, qr_ref, kv_ref, kp_ref, o_ref, lse_ref, do_ref, dqn_ref, dqr_ref, dkv_ref, dkp_ref):
        first_q = pl.program_id(2) == 0

        @pl.when(first_q)
        def _():
            dkv_ref[...] = jnp.zeros_like(dkv_ref)

        @pl.when(jnp.logical_and(first_q, pl.program_id(1) == 0))
        def _():
            dkp_ref[...] = jnp.zeros_like(dkp_ref)

        q0 = pl.program_id(2) * bq
        n_tiles = (q0 + bq) // MLA_KEYS
        hs = range(MLA_HEADS_PER_STEP)
        qn = [qn_ref[:, _head(h)] for h in hs]
        qp = [qr_ref[:, h * MLA_ROPE:(h + 1) * MLA_ROPE] for h in hs]
        do = [do_ref[:, _head(h)].astype(f32) for h in hs]
        lse = [lse_ref[:, h * HEAD_DIM:h * HEAD_DIM + 1] for h in hs]
        dsum = [jnp.sum(do[h] * o_ref[:, _head(h)], axis=1, keepdims=True) for h in hs]

        def grads(j, carry):
            k0 = pl.multiple_of(j * MLA_KEYS, MLA_KEYS)
            rows = pl.ds(k0, MLA_KEYS)
            kp = kp_ref[rows, :MLA_ROPE]
            out = []
            for h in hs:
                dqn, dqp = carry[h]
                g = jax.grad(mla_tile_loss, argnums=(0, 1, 2, 3, 4))(
                    qn[h], qp[h], kv_ref[rows, _head(2 * h)], kp, kv_ref[rows, _head(2 * h + 1)],
                    do[h], lse[h], dsum[h], q0, k0)
                dkv_ref[rows, _head(2 * h)] += g[2]
                dkp_ref[rows, :MLA_ROPE] += g[3]
                dkv_ref[rows, _head(2 * h + 1)] += g[4]
                out.append((dqn + g[0], dqp + g[1]))
            return out

        res = lax.fori_loop(0, n_tiles, grads,
                            [(jnp.zeros((bq, HEAD_DIM), f32), jnp.zeros((bq, MLA_ROPE), f32)) for _ in hs])
        for h in hs:
            dqn_ref[:, _head(h)] = res[h][0]
            dqr_ref[:, h * MLA_ROPE:(h + 1) * MLA_ROPE] = res[h][1]

    n = nb * t
    return pl.pallas_call(
        body, name="mla_bwd", grid=(nb, N_HEADS // MLA_HEADS_PER_STEP, nq),
        in_specs=[sqn, sqr, skv, skp, sqn, sqn, sqn], out_specs=[sqn, sqr, skv, skp],
        out_shape=[jax.ShapeDtypeStruct((n, 512), f32), jax.ShapeDtypeStruct((n, N_HEADS * MLA_ROPE), f32),
                   jax.ShapeDtypeStruct((n, 1024), f32), jax.ShapeDtypeStruct((n, LANE), f32)],
        compiler_params=_cparams(("arbitrary", "arbitrary", "arbitrary")),
    )(q_nope, qr, kv, kpe, o32, lse_b, do)


def loss_head(h, target):
    n, d = h.shape
    tm = _tile(n, 512)

    def body(h_ref, t_ref, l_ref, dh_ref):
        diff = h_ref[...] - t_ref[...]
        dh_ref[...] = diff * (1.0 / d)
        part = 0.5 * jnp.sum(jnp.sum(diff * diff, axis=1, keepdims=True) * (1.0 / d), axis=0, keepdims=True)

        @pl.when(pl.program_id(0) == 0)
        def _():
            l_ref[...] = jnp.zeros_like(l_ref)

        l_ref[...] += jnp.broadcast_to(part, l_ref.shape)

    spec = pl.BlockSpec((tm, d), lambda i: (i, 0))
    return pl.pallas_call(
        body, name="loss_head", grid=(n // tm,), in_specs=[spec, spec],
        out_specs=[pl.BlockSpec((8, LANE), lambda i: (0, 0)), spec],
        out_shape=[jax.ShapeDtypeStruct((8, LANE), f32), jax.ShapeDtypeStruct((n, d), f32)],
        compiler_params=_cparams(("arbitrary",)),
    )(h, target)


def _row(v):
    return v.reshape(1, -1)


def _pad_cols(a, n):
    return jnp.pad(a, ((0, 0), (0, n - a.shape[1])))


def _pad_row(v, n=LANE):
    return jnp.pad(v.reshape(1, -1), ((0, 0), (0, n - v.shape[0])))


def _group_matrix(width, group):
    idx = np.arange(width) // group
    return jnp.asarray((idx[:, None] == idx[None, :]).astype(np.float32))


def _head_expand():
    e = np.zeros((N_HEADS, LANE, HEAD_DIM), np.float32)
    for h in range(N_HEADS):
        e[h, h, :] = 1.0
    return jnp.asarray(e)


def _rope_freqs():
    inv = 1.0 / (ROPE_THETA ** (np.arange(0, MLA_ROPE, 2, dtype=np.float32) / MLA_ROPE))
    inv = np.tile(inv.astype(np.float32), 2)
    inv_q = np.tile(inv, N_HEADS).reshape(1, N_HEADS * MLA_ROPE)
    inv_k = np.zeros((1, LANE), np.float32)
    inv_k[0, :MLA_ROPE] = inv
    return jnp.asarray(inv_q), jnp.asarray(inv_k)


def _uq_split(w):
    w3 = w.reshape(w.shape[0], N_HEADS, HEAD_DIM + MLA_ROPE)
    return w3[:, :, :HEAD_DIM].reshape(-1, 512), w3[:, :, HEAD_DIM:].reshape(-1, N_HEADS * MLA_ROPE)


def _uq_merge(gn, gr):
    r = gn.shape[0]
    return jnp.concatenate([gn.reshape(r, N_HEADS, HEAD_DIM), gr.reshape(r, N_HEADS, MLA_ROPE)], axis=2).reshape(r, 768)


EARLY_GRADS = ['ffn1_w_up', 'ffn1_w_down', 'l1_w_in', 'l1_w_out', 'ffn0_w_up', 'ffn0_w_down', 'l0_w_out']


def local_step(x, positions, target, w, late_weights=None, scatter_early=None):
    w = dict(w)
    nb, t, d = x.shape
    n = nb * t
    tm = 256
    ni = n // tm
    tc = 2 * LANE
    h0 = x.reshape(n, d)
    tgt = target.reshape(n, d)
    pos = positions.reshape(n, 1).astype(f32)
    gh = _group_matrix(512, HEAD_DIM)
    gg = _group_matrix(512, 256)
    e_heads = _head_expand()
    inv_q, inv_k = _rope_freqs()
    g = {}

    def ln_stage(h, y, gname, bname):
        ops = [_rows(h, tm), _rows(y, tm, gdtype=bf16), _param(_row(w[gname])), _param(_row(w[bname]))]
        return ops, [_rows_out(n, d, tm)]

    def ln_fwd(name, ops):
        return block_fwd(lambda *a: f_ln(*a) * 2, name, (1, ni), ops, [_rows_out(n, d, tm), _rows_out(n, d, tm, bf16)])

    def ffn_act_stage(u, cw, cb):
        nj = D_FF // tc
        ops = [_cols(u, t, tc, 0, D_FF, bf16), _cols(u, t, tc, nj, D_FF, bf16)] \
            + [_cparam(cw[i:i + 1], tc) for i in range(3)] + [_cparam(_row(cb), tc)]
        return ops, [Out((n, D_FF), (t, tc), lambda j, i: (i, j), bf16)], (nj, nb)

    w_in0 = _pad_cols(w['l0_w_in'], L0_PAD)
    h0b = h0.astype(bf16)
    proj0 = mm(h0b, w_in0, "l0_proj")

    shift_ops = [_cols(proj0, t, tc, 0, RWKV_COLS, bf16), _cparam(_row(w['rwkv_mix']), tc)]
    shift_outs = [Out((n, RWKV_COLS), (t, tc), lambda j, i: (i, j))]
    shift_grid = (RWKV_COLS // tc, nb)
    (ps,) = block_fwd(f_shift_mix, "rwkv_shift", shift_grid, shift_ops, shift_outs)

    pre_ops = [_colblock(ps, tm, 512, 512), _colblock(ps, tm, 1536, 128), _colblock(ps, tm, 1664, 128),
               _param(_row(w['rwkv_w0'])), _param(w['rwkv_w2']), _param(_row(w['rwkv_a0'])), _param(w['rwkv_a2']),
               _param(w['rwkv_g2']), _param(_row(w['rwkv_k_k'])), _param(_row(w['rwkv_k_a'])), _param(gh, diff=False)]
    pre_outs = [_rows_out(n, 512, tm) for _ in range(5)]
    lw, k2, na, bb, gate_r = block_fwd(f_rwkv_pre, "rwkv_pre", (1, ni), pre_ops, pre_outs)
    def arrived(group, gathered):
        if late_weights is not None:
            for name, got in zip(late_weights[group][0], gathered):
                w[name] = late_weights[2](name, got)

    ride = None if late_weights is None else GatherRide(late_weights[0][1])
    (y_tok, s0_saved), gathered = rwkv_scan_fwd(ps, lw, k2, na, bb, nb, t, ride)
    arrived(0, gathered)
    w_out0 = w['l0_w_out']

    post_ops = [_rows(y_tok, tm), _colblock(ps, tm, 0, 512), _rows(k2, tm), _colblock(ps, tm, 1024, 512),
                _rows(gate_r, tm), _param(_row(w['rwkv_ln_g'])), _param(_row(w['rwkv_ln_b'])),
                _param(w['rwkv_r_k'].reshape(1, 512)), _param(gh, diff=False)]
    post_outs = [_rows_out(n, 512, tm, bf16)]
    (y_a,) = block_fwd(f_rwkv_post, "rwkv_post", (1, ni), post_ops, post_outs)

    xbc_off = (RWKV_COLS + 512) // tc
    conv_ops = [_cols(proj0, t, tc, xbc_off, 1024, bf16)] + [_cparam(w['ssm_conv_w'][i:i + 1], tc) for i in range(4)] \
        + [_cparam(_row(w['ssm_conv_b']), tc)]
    conv_outs = [Out((n, 1024), (t, tc), lambda j, i: (i, j))]
    conv_grid = (1024 // tc, nb)
    (xbc_act,) = block_fwd(f_conv4_silu, "ssm_conv", conv_grid, conv_ops, conv_outs)

    dt_bias, a_log, d_skip = _pad_row(w['ssm_dt_bias']), _pad_row(w['ssm_a_log']), _pad_row(w['ssm_d'])
    y_ssd, ssd_states = ssd_fwd(xbc_act, proj0, dt_bias, a_log, d_skip, e_heads, nb, t)

    z_tok = proj0[:, RWKV_COLS:RWKV_COLS + 512]
    spost_ops = [_rows(y_ssd, tm), _rows(z_tok, tm, gdtype=bf16), _param(_row(w['ssm_norm_g'])), _param(gg, diff=False)]
    spost_outs = [_rows_out(n, 512, tm, bf16)]
    (y_b,) = block_fwd(f_ssm_post, "ssm_post", (1, ni), spost_ops, spost_outs)

    mixed0 = mm(y_b, w_out0[512:], "l0_out_b", add=mm(y_a, w_out0[:512], "l0_out_a"))
    ln1_ops, ln_outs = ln_stage(h0, mixed0, 'l0_ln1_g', 'l0_ln1_b')
    h1, h1b = ln_fwd("l0_ln1", ln1_ops)

    u0 = mm(h1b, w['ffn0_w_up'], "ffn0_up")
    act0_ops, act_outs, act_grid = ffn_act_stage(u0, w['ffn0_conv_w'], w['ffn0_conv_b'])
    (act0,) = block_fwd(f_ffn_act, "ffn0_act", act_grid, act0_ops, act_outs)
    f0 = mm(act0, w['ffn0_w_down'], "ffn0_down")
    ln2_ops, _ = ln_stage(h1, f0, 'l0_ln2_g', 'l0_ln2_b')
    h2, h2b = ln_fwd("l0_ln2", ln2_ops)

    w_in1 = _pad_cols(w['l1_w_in'], L1_PAD)
    proj1 = mm(h2b, w_in1, "l1_proj")
    w_qn, w_qr = _uq_split(w['mla_w_uq'])
    mpre_ops = [_colblock(proj1, tm, 1536, 256, bf16), _colblock(proj1, tm, 1792, 128, bf16),
                _colblock(proj1, tm, 1920, 128, bf16),
                Op(pos, (tm, 1), lambda j, i: (i, 0), diff=False),
                _param(_row(w['mla_q_norm_g'])), _param(w_qn), _param(w_qr),
                _param(_row(w['mla_kv_norm_g'])), _param(w['mla_w_ukv']), _param(inv_q, diff=False),
                _param(inv_k, diff=False)]
    mpre_outs = [_rows_out(n, 512, tm), _rows_out(n, N_HEADS * MLA_ROPE, tm), _rows_out(n, 1024, tm),
                 _rows_out(n, LANE, tm)]
    q_nope, q_rope, kv, kpe = block_fwd(f_mla_pre, "mla_pre", (1, ni), mpre_ops, mpre_outs)
    ride = None if late_weights is None else GatherRide(late_weights[1][1])
    (o_sb, sb_masses), gathered = sb_fwd(proj1, nb, t, ride)
    arrived(1, gathered)
    w_out1 = w['l1_w_out']
    o_mla, o_mla32, mla_lse = mla_fwd(q_nope, q_rope, kv, kpe, nb, t)

    mixed1 = mm(o_mla, w_out1[512:], "l1_out_b", add=mm(o_sb, w_out1[:512], "l1_out_a"))
    ln3_ops, _ = ln_stage(h2, mixed1, 'l1_ln1_g', 'l1_ln1_b')
    h3, h3b = ln_fwd("l1_ln1", ln3_ops)
    u1 = mm(h3b, w['ffn1_w_up'], "ffn1_up")
    act1_ops, _, _ = ffn_act_stage(u1, w['ffn1_conv_w'], w['ffn1_conv_b'])
    (act1,) = block_fwd(f_ffn_act, "ffn1_act", act_grid, act1_ops, act_outs)
    f1 = mm(act1, w['ffn1_w_down'], "ffn1_down")
    ln4_ops, _ = ln_stage(h3, f1, 'l1_ln2_g', 'l1_ln2_b')
    (h4,) = block_fwd(f_ln, "l1_ln2", (1, ni), ln4_ops, ln_outs)

    loss_part, dh4 = loss_head(h4, tgt)

    def vec(a_):
        return a_.reshape(-1)

    def ffn_bwd(tag, dh_out, ln_ops, act_ops, h_in, act, w_up, w_down, names):
        dh_res, df, gg_, gb_ = block_bwd(f_ln, tag + "_ln2_bwd", (1, ni), ln_ops, ln_outs, [dh_out])
        g[names[4]], g[names[5]] = vec(gg_), vec(gb_)
        g[names[3]] = mm(act, df, tag + "_down_dw", ta=True)
        dact = mm(df, w_down.T, tag + "_down_dx")
        dgate, dup, dw0, dw1, dw2, dcb = block_bwd(f_ffn_act, tag + "_act_bwd", act_grid, act_ops, act_outs, [dact])
        g[names[1]] = jnp.concatenate([dw0, dw1, dw2], axis=0)
        g[names[2]] = vec(dcb)
        quarter = 2 * D_FF // N_SHARD
        g[names[0]] = mm(h_in, dup, tag + "_upv_dw", ta=True, pieces=(N_SHARD, 2, quarter),
                         into=mm(h_in, dgate, tag + "_gate_dw", ta=True, pieces=(N_SHARD, 0, quarter)))
        w_up_t = w_up.T
        dh = mm(dgate, w_up_t[:D_FF], tag + "_gate_dx", add=dh_res)
        return mm(dup, w_up_t[D_FF:], tag + "_upv_dx", add=dh)

    def out_bwd(tag, dmixed, y_first, y_second, w_out, name):
        g[name] = jnp.concatenate([mm(y_first, dmixed, tag + "_a_dw", ta=True),
                                   mm(y_second, dmixed, tag + "_b_dw", ta=True)], axis=0)
        w_t = w_out.T
        return mm(dmixed, w_t[:, :512], tag + "_a_dx"), mm(dmixed, w_t[:, 512:], tag + "_b_dx")

    dh3 = ffn_bwd("ffn1", dh4, ln4_ops, act1_ops, h3b, act1, w['ffn1_w_up'], w['ffn1_w_down'],
                  ['ffn1_w_up', 'ffn1_conv_w', 'ffn1_conv_b', 'ffn1_w_down', 'l1_ln2_g', 'l1_ln2_b'])

    dh2_res, dmixed1, g3g, g3b = block_bwd(f_ln, "l1_ln1_bwd", (1, ni), ln3_ops, ln_outs, [dh3])
    g['l1_ln1_g'], g['l1_ln1_b'] = vec(g3g), vec(g3b)
    do_sb, do_mla = out_bwd("l1_out", dmixed1, o_sb, o_mla, w_out1, 'l1_w_out')

    dq_nope, dq_rope, dkv, dkpe = mla_bwd(q_nope, q_rope, kv, kpe, o_mla32, mla_lse, do_mla, nb, t)
    dsb_q, dsb_k, dsb_v = sb_bwd(proj1, sb_masses, do_sb, nb, t)
    (dc_q, dc_kv, dkpe_raw, gqg, gwqn, gwqr, gkvg, g['mla_w_ukv']) = block_bwd(
        f_mla_pre, "mla_pre_bwd", (1, ni), mpre_ops, mpre_outs, [dq_nope, dq_rope, dkv, dkpe])
    g['mla_q_norm_g'], g['mla_kv_norm_g'] = vec(gqg), vec(gkvg)
    g['mla_w_uq'] = _uq_merge(gwqn, gwqr)
    dproj1 = jnp.concatenate([dsb_q.astype(bf16), dsb_k.astype(bf16), dsb_v.astype(bf16), dc_q, dc_kv, dkpe_raw],
                             axis=1)
    g['l1_w_in'] = mm(h2b, dproj1, "l1_proj_dw", ta=True)[:, :L1_COLS]
    dh2 = mm(dproj1, w_in1.T, "l1_proj_dx", add=dh2_res)

    dh1 = ffn_bwd("ffn0", dh2, ln2_ops, act0_ops, h1b, act0, w['ffn0_w_up'], w['ffn0_w_down'],
                  ['ffn0_w_up', 'ffn0_conv_w', 'ffn0_conv_b', 'ffn0_w_down', 'l0_ln2_g', 'l0_ln2_b'])

    dh0_res, dmixed0, g1g, g1b = block_bwd(f_ln, "l0_ln1_bwd", (1, ni), ln1_ops, ln_outs, [dh1])
    g['l0_ln1_g'], g['l0_ln1_b'] = vec(g1g), vec(g1b)
    dy_a, dy_b = out_bwd("l0_out", dmixed0, y_a, y_b, w_out0, 'l0_w_out')

    dy_ssd, dz, gng = block_bwd(f_ssm_post, "ssm_post_bwd", (1, ni), spost_ops, spost_outs, [dy_b])
    g['ssm_norm_g'] = vec(gng)
    early_pieces = None if scatter_early is None else scatter_early[0]({name: g[name] for name in EARLY_GRADS})
    ride = None if scatter_early is None else SwapRide(early_pieces)
    (dxs, dbm, dcm, ddt_raw, gdb, gal, gdsk), from_sibling = ssd_bwd(
        xbc_act, proj0, dt_bias, a_log, d_skip, e_heads, ssd_states, dy_ssd, nb, t, ride)
    g['ssm_dt_bias'], g['ssm_a_log'], g['ssm_d'] = gdb[0, :8], gal[0, :8], gdsk[0, :8]
    dxbc_act = jnp.concatenate([dxs, dbm, dcm], axis=1)
    dxbc, cw0, cw1, cw2, cw3, gcb = block_bwd(f_conv4_silu, "ssm_conv_bwd", conv_grid, conv_ops, conv_outs, [dxbc_act])
    g['ssm_conv_w'] = jnp.concatenate([cw0, cw1, cw2, cw3], axis=0)
    g['ssm_conv_b'] = vec(gcb)

    dy_tok, dr_post, dk2_post, dv_post, dgate, glg, glb, grk = block_bwd(
        f_rwkv_post, "rwkv_post_bwd", (1, ni), post_ops, post_outs, [dy_a])
    g['rwkv_ln_g'], g['rwkv_ln_b'], g['rwkv_r_k'] = vec(glg), vec(glb), grk.reshape(N_HEADS, HEAD_DIM)
    ride = None if scatter_early is None else ScatterRide(scatter_early[1](early_pieces, from_sibling))
    (dr, dlw, dk2, dv, dna, dbb), early = rwkv_scan_bwd(s0_saved, ps, lw, k2, na, bb, dy_tok, nb, t, ride)
    (dk_pre, dwa_lo, dg_lo, gw0, g['rwkv_w2'], ga0, g['rwkv_a2'], g['rwkv_g2'], gkk, gka) = block_bwd(
        f_rwkv_pre, "rwkv_pre_bwd", (1, ni), pre_ops, pre_outs, [dlw, dk2 + dk2_post, dna, dbb, dgate])
    g['rwkv_w0'], g['rwkv_a0'], g['rwkv_k_k'], g['rwkv_k_a'] = vec(gw0), vec(ga0), vec(gkk), vec(gka)
    dps = jnp.concatenate([dr + dr_post, dk_pre, dv + dv_post, dwa_lo, dg_lo], axis=1)
    dp_rwkv, gmix = block_bwd(f_shift_mix, "rwkv_shift_bwd", shift_grid, shift_ops, shift_outs, [dps])
    g['rwkv_mix'] = vec(gmix)

    dproj0 = jnp.concatenate([dp_rwkv, dz, dxbc, ddt_raw], axis=1)
    g['l0_w_in'] = mm(h0b, dproj0, "l0_proj_dw", ta=True)[:, :L0_COLS]
    grad_x = mm(dproj0, w_in0.T, "l0_proj_dx", add=dh0_res)
    return loss_part, grad_x.reshape(nb, t, d), g, early


MESH = pl.DeviceIdType.MESH
ANY = pl.BlockSpec(memory_space=pl.ANY)
AXES = ("x", "y", "c")


def _place():
    x, y, c = lax.axis_index("x"), lax.axis_index("y"), lax.axis_index("c")
    chips = [(1 - x, y), (x, 1 - y), (1 - x, 1 - y)]
    return x, y, c, chips


def _dma_sems(n):
    return pltpu.SemaphoreType.DMA((n,))


class GatherRide:
    def __init__(self, shards):
        n = len(shards)
        self.inputs = list(shards)
        self.out_shapes = [jax.ShapeDtypeStruct((N_SHARD,) + a.shape, a.dtype) for a in shards]
        self.scratch = [_dma_sems(3 * n), _dma_sems(3 * n), _dma_sems(3 * n), _dma_sems(3 * n), _dma_sems(n)]

    def _copies(self, ins, outs, sems):
        ici_send, ici_recv, d2d_send, d2d_recv, local_sems = sems
        x, y, c, chips = _place()
        me = 2 * x + y
        pairs = list(enumerate(zip(ins, outs)))

        def over_ici(k, j, slot, to):
            return pltpu.make_async_remote_copy(
                src_ref=ins[k].at[c], dst_ref=outs[k].at[slot, c], send_sem=ici_send.at[3 * k + j],
                recv_sem=ici_recv.at[3 * k + j], device_id=to, device_id_type=MESH)

        def to_sibling(k, j, slot, half):
            return pltpu.make_async_remote_copy(
                src_ref=outs[k].at[slot, half], dst_ref=outs[k].at[slot, half], send_sem=d2d_send.at[3 * k + j],
                recv_sem=d2d_recv.at[3 * k + j], device_id=(x, y, 1 - c), device_id_type=MESH)

        mine = [pltpu.make_async_copy(a, o.at[me], local_sems.at[k]) for k, (a, o) in pairs]
        sends = [over_ici(k, j, me, (cx, cy, c)) for k, _ in pairs for j, (cx, cy) in enumerate(chips)]
        return c, chips, pairs, over_ici, to_sibling, mine, sends

    def start(self, ins, outs, sems):
        _, _, _, _, _, mine, sends = self._copies(ins, outs, sems)
        for cp in mine + sends:
            cp.start()

    def finish(self, ins, outs, sems):
        c, chips, pairs, over_ici, to_sibling, mine, sends = self._copies(ins, outs, sems)
        passed = []
        for k, _ in pairs:
            for j, (cx, cy) in enumerate(chips):
                over_ici(k, j, 2 * cx + cy, (cx, cy, c)).wait_recv()
                passed.append(to_sibling(k, j, 2 * cx + cy, c))
                passed[-1].start()
        for k, _ in pairs:
            for j, (cx, cy) in enumerate(chips):
                to_sibling(k, j, 2 * cx + cy, 1 - c).wait_recv()
        for cp in sends + passed:
            cp.wait_send()
        for cp in mine:
            cp.wait()


class ScatterRide:
    def __init__(self, parts):
        n = len(parts)
        self.inputs = list(parts)
        self.out_shapes = [jax.ShapeDtypeStruct(a.shape, a.dtype) for a in parts]
        self.scratch = [_dma_sems(3 * n), _dma_sems(3 * n), _dma_sems(n)]

    def _copies(self, ins, outs, sems):
        send_sems, recv_sems, local_sems = sems
        x, y, c, chips = _place()
        me = 2 * x + y
        pairs = list(enumerate(zip(ins, outs)))

        def over_ici(k, j, src_slot, dst_slot, to):
            return pltpu.make_async_remote_copy(
                src_ref=ins[k].at[src_slot], dst_ref=outs[k].at[dst_slot], send_sem=send_sems.at[3 * k + j],
                recv_sem=recv_sems.at[3 * k + j], device_id=to, device_id_type=MESH)

        mine = [pltpu.make_async_copy(a.at[me], o.at[me], local_sems.at[k]) for k, (a, o) in pairs]
        sends = [over_ici(k, j, 2 * cx + cy, me, (cx, cy, c)) for k, _ in pairs for j, (cx, cy) in enumerate(chips)]
        arrivals = lambda: [over_ici(k, j, me, 2 * cx + cy, (cx, cy, c))
                            for k, _ in pairs for j, (cx, cy) in enumerate(chips)]
        return mine, sends, arrivals

    def start(self, ins, outs, sems):
        mine, sends, _ = self._copies(ins, outs, sems)
        for cp in mine + sends:
            cp.start()

    def finish(self, ins, outs, sems):
        mine, sends, arrivals = self._copies(ins, outs, sems)
        for cp in arrivals():
            cp.wait_recv()
        for cp in sends:
            cp.wait_send()
        for cp in mine:
            cp.wait()


def _run_ride(ride, name):
    n = len(ride.inputs)

    def body(*refs):
        ins, outs, sems = refs[:n], refs[n:2 * n], refs[2 * n:]
        ride.start(ins, outs, sems)
        ride.finish(ins, outs, sems)

    return pl.pallas_call(body, name=name, in_specs=[ANY] * n, out_specs=[ANY] * n, out_shape=ride.out_shapes,
                          scratch_shapes=ride.scratch)(*ride.inputs)


def gather_shards(shards):
    return _run_ride(GatherRide(shards), "gather_shards")


class SwapRide:
    def __init__(self, pieces):
        n = len(pieces)
        self.inputs = list(pieces)
        self.out_shapes = [jax.ShapeDtypeStruct((a.shape[0],) + a.shape[2:], a.dtype) for a in pieces]
        self.scratch = [_dma_sems(n), _dma_sems(n)]

    def _copies(self, ins, outs, sems):
        send_sems, recv_sems = sems
        x, y, c, _ = _place()
        return [pltpu.make_async_remote_copy(
            src_ref=a.at[:, 1 - c], dst_ref=o, send_sem=send_sems.at[k], recv_sem=recv_sems.at[k],
            device_id=(x, y, 1 - c), device_id_type=MESH) for k, (a, o) in enumerate(zip(ins, outs))]

    def start(self, ins, outs, sems):
        for cp in self._copies(ins, outs, sems):
            cp.start()

    def finish(self, ins, outs, sems):
        for cp in self._copies(ins, outs, sems):
            cp.wait()


def swap_halves(pieces, name):
    return _run_ride(SwapRide(pieces), name)


def scatter_to_chips(parts):
    return _run_ride(ScatterRide(parts), "scatter_to_chips")


def share_halves(bufs):
    n = len(bufs)

    def body(*refs):
        ins, outs = refs[:n], refs[n:2 * n]
        send_sems, recv_sems = refs[2 * n:]
        x, y, c, _ = _place()
        cps = [pltpu.make_async_remote_copy(
            src_ref=a.at[c], dst_ref=o.at[c], send_sem=send_sems.at[k], recv_sem=recv_sems.at[k],
            device_id=(x, y, 1 - c), device_id_type=MESH) for k, (a, o) in enumerate(zip(ins, outs))]
        for cp in cps:
            cp.start()
        for k, (a, o) in enumerate(zip(ins, outs)):
            cps[k].wait_send()
            pltpu.make_async_remote_copy(
                src_ref=a.at[c], dst_ref=o.at[1 - c], send_sem=send_sems.at[k], recv_sem=recv_sems.at[k],
                device_id=(x, y, 1 - c), device_id_type=MESH).wait_recv()

    return pl.pallas_call(
        body, name="share_halves", in_specs=[ANY] * n, out_specs=[ANY] * n,
        out_shape=[jax.ShapeDtypeStruct(a.shape, a.dtype) for a in bufs],
        input_output_aliases={k: k for k in range(n)},
        scratch_shapes=[_dma_sems(n), _dma_sems(n)],
    )(*bufs)


def pair_add(piece, recv, core, name, out_dtype):
    _, _, h, cdim = piece.shape
    tr = _rtile(h, cdim)

    def body(c_ref, a_ref, b_ref, o_ref):
        o_ref[...] = (a_ref[0] + b_ref[...]).astype(o_ref.dtype)

    spec = pl.BlockSpec((1, tr, cdim), lambda p, i, c_ref: (p, i, 0))
    return pl.pallas_call(
        body, name=name,
        grid_spec=pltpu.PrefetchScalarGridSpec(
            num_scalar_prefetch=1, grid=(N_SHARD, h // tr),
            in_specs=[pl.BlockSpec((1, 1, tr, cdim), lambda p, i, c_ref: (p, c_ref[0], i, 0)), spec],
            out_specs=spec),
        out_shape=jax.ShapeDtypeStruct((N_SHARD, h, cdim), out_dtype),
        compiler_params=_cparams(("parallel", "parallel")),
    )(core, piece, recv)


def chip_add(parts, core, name):
    _, h, cdim = parts.shape
    tr = _rtile(h, cdim, 1024 * 1024)

    def body(c_ref, p_ref, o_ref):
        p = [p_ref[s].astype(f32) for s in range(N_SHARD)]
        o_ref[0] = ((p[0] + p[1]) + p[2]) + p[3]

    return pl.pallas_call(
        body, name=name,
        grid_spec=pltpu.PrefetchScalarGridSpec(
            num_scalar_prefetch=1, grid=(h // tr,),
            in_specs=[pl.BlockSpec((N_SHARD, tr, cdim), lambda i, c_ref: (0, i, 0))],
            out_specs=pl.BlockSpec((1, tr, cdim), lambda i, c_ref: (c_ref[0], i, 0))),
        out_shape=jax.ShapeDtypeStruct((2, h, cdim), f32), compiler_params=_cparams(("parallel",)),
    )(core, parts)


def adamw(w, g, m, v, name):
    rows, cdim = w.shape
    tr = _rtile(rows, cdim, 1024 * 1024)

    def body(w_ref, g_ref, m_ref, v_ref, d_ref, nm_ref, nv_ref):
        gv = g_ref[...]
        m_new = ADAM_B1 * m_ref[...] + (1.0 - ADAM_B1) * gv
        v_new = ADAM_B2 * v_ref[...] + (1.0 - ADAM_B2) * jnp.square(gv)
        m_hat = m_new / (1.0 - ADAM_B1 ** ADAM_STEP)
        v_hat = v_new / (1.0 - ADAM_B2 ** ADAM_STEP)
        d_ref[...] = -ADAM_LR * (m_hat / (jnp.sqrt(v_hat) + ADAM_EPS) + ADAM_WD * w_ref[...])
        nm_ref[...] = m_new
        nv_ref[...] = v_new

    spec = pl.BlockSpec((tr, cdim), lambda i: (i, 0))
    return pl.pallas_call(body, name=name, grid=(rows // tr,), in_specs=[spec] * 4, out_specs=[spec] * 3,
                          out_shape=[jax.ShapeDtypeStruct(w.shape, f32)] * 3,
                          compiler_params=_cparams(("parallel",)))(w, g, m, v)


SMALL_MULTIPLE = 16 * LANE


def _pack_flat(parts, multiple=SMALL_MULTIPLE):
    flat = jnp.concatenate([p.reshape(-1) for p in parts])
    pad = (-flat.shape[0]) % multiple
    return jnp.pad(flat, (0, pad)).reshape(-1, LANE)


def _unpack_flat(buf, shapes):
    flat = buf.reshape(-1)
    out, off = [], 0
    for s in shapes:
        cnt = int(np.prod(s))
        out.append(flat[off:off + cnt].reshape(s))
        off += cnt
    return out


def _full_from_shards(name, gathered):
    if name in COL_SHARDED:
        return jnp.concatenate([gathered[s] for s in range(N_SHARD)], axis=1)
    return gathered.reshape(-1, gathered.shape[2])


def _pieces(name, grad):
    if grad.ndim == 3:
        return grad
    if name in COL_SHARDED:
        r, cdim = grad.shape
        return grad.reshape(r, N_SHARD, cdim // N_SHARD).transpose(1, 0, 2)
    return grad.reshape(N_SHARD, grad.shape[0] // N_SHARD, grad.shape[1])


def _small_pieces(name, grad):
    if name in COL_SHARDED or name in ROW_SHARDED:
        return _pieces(name, grad).reshape(N_SHARD, -1)
    return jnp.broadcast_to(grad.reshape(1, -1), (N_SHARD, grad.size))


def kernel(x, positions, l0_w_in, rwkv_mix, rwkv_w0, rwkv_w2, rwkv_a0, rwkv_a2, rwkv_g2, rwkv_k_k, rwkv_k_a, rwkv_r_k, rwkv_ln_g, rwkv_ln_b, ssm_conv_w, ssm_conv_b, ssm_dt_bias, ssm_a_log, ssm_d, ssm_norm_g, l0_w_out, l0_ln1_g, l0_ln1_b, ffn0_w_up, ffn0_conv_w, ffn0_conv_b, ffn0_w_down, l0_ln2_g, l0_ln2_b, l1_w_in, mla_q_norm_g, mla_w_uq, mla_kv_norm_g, mla_w_ukv, l1_w_out, l1_ln1_g, l1_ln1_b, ffn1_w_up, ffn1_conv_w, ffn1_conv_b, ffn1_w_down, l1_ln2_g, l1_ln2_b, loss_target, m_l0_w_in, m_rwkv_mix, m_rwkv_w0, m_rwkv_w2, m_rwkv_a0, m_rwkv_a2, m_rwkv_g2, m_rwkv_k_k, m_rwkv_k_a, m_rwkv_r_k, m_rwkv_ln_g, m_rwkv_ln_b, m_ssm_conv_w, m_ssm_conv_b, m_ssm_dt_bias, m_ssm_a_log, m_ssm_d, m_ssm_norm_g, m_l0_w_out, m_l0_ln1_g, m_l0_ln1_b, m_ffn0_w_up, m_ffn0_conv_w, m_ffn0_conv_b, m_ffn0_w_down, m_l0_ln2_g, m_l0_ln2_b, m_l1_w_in, m_mla_q_norm_g, m_mla_w_uq, m_mla_kv_norm_g, m_mla_w_ukv, m_l1_w_out, m_l1_ln1_g, m_l1_ln1_b, m_ffn1_w_up, m_ffn1_conv_w, m_ffn1_conv_b, m_ffn1_w_down, m_l1_ln2_g, m_l1_ln2_b, v_l0_w_in, v_rwkv_mix, v_rwkv_w0, v_rwkv_w2, v_rwkv_a0, v_rwkv_a2, v_rwkv_g2, v_rwkv_k_k, v_rwkv_k_a, v_rwkv_r_k, v_rwkv_ln_g, v_rwkv_ln_b, v_ssm_conv_w, v_ssm_conv_b, v_ssm_dt_bias, v_ssm_a_log, v_ssm_d, v_ssm_norm_g, v_l0_w_out, v_l0_ln1_g, v_l0_ln1_b, v_ffn0_w_up, v_ffn0_conv_w, v_ffn0_conv_b, v_ffn0_w_down, v_l0_ln2_g, v_l0_ln2_b, v_l1_w_in, v_mla_q_norm_g, v_mla_w_uq, v_mla_kv_norm_g, v_mla_w_ukv, v_l1_w_out, v_l1_ln1_g, v_l1_ln1_b, v_ffn1_w_up, v_ffn1_conv_w, v_ffn1_conv_b, v_ffn1_w_down, v_l1_ln2_g, v_l1_ln2_b):
    args = locals()
    w_loc = {n: args[n] for n in WEIGHTS}
    m_loc = {n: args["m_" + n] for n in WEIGHTS}
    v_loc = {n: args["v_" + n] for n in WEIGHTS}
    core = lax.axis_index("c").astype(jnp.int32).reshape(1)

    small_sharded = [n for n in SMALL if n in COL_SHARDED]
    halves = lambda a: a.reshape(2, a.shape[0] // 2, a.shape[1])
    whole = lambda name, got: _full_from_shards(name, got.reshape(N_SHARD, -1, got.shape[3]))
    first = gather_shards([halves(w_loc['l0_w_in'].astype(bf16)), halves(_pack_flat([w_loc[n] for n in small_sharded]))])
    w_have = {n: w_loc[n] for n in WEIGHTS if n not in BIG}
    w_have['l0_w_in'] = whole('l0_w_in', first[0])
    small_all = first[1].reshape(N_SHARD, -1, LANE)
    per_shard = [_unpack_flat(small_all[s], [w_loc[n].shape for n in small_sharded]) for s in range(N_SHARD)]
    for k, n in enumerate(small_sharded):
        w_have[n] = jnp.concatenate([per_shard[s][k] for s in range(N_SHARD)], axis=1)
    shards_of = lambda names: (names, [halves(w_loc[n].astype(bf16)) for n in names])
    late = (shards_of(['l0_w_out', 'ffn0_w_up', 'ffn0_w_down', 'l1_w_in']),
            shards_of(['l1_w_out', 'ffn1_w_up', 'ffn1_w_down']), whole)

    def in_halves(pieces):
        return [p.reshape(N_SHARD, 2, p.shape[1] // 2, p.shape[2]) for p in pieces]

    def pair_sums(names, pieces, from_sibling):
        return [pair_add(p, r, core, "pair_add_" + n, f32 if n == 'small' else bf16)
                for n, p, r in zip(names, pieces, from_sibling)]

    loss_part, grad_x, g_full, early = local_step(
        x, positions, loss_target, w_have, late,
        (lambda gd: in_halves([_pieces(n, gd[n]) for n in EARLY_GRADS]),
         lambda pieces, from_sibling: pair_sums(EARLY_GRADS, pieces, from_sibling)))
    loss = lax.psum(loss_part[0, 0], AXES)

    small_flat = jnp.concatenate([_small_pieces(n, g_full[n]) for n in SMALL], axis=1)
    pad = (-small_flat.shape[1]) % SMALL_MULTIPLE
    small_pieces = jnp.pad(small_flat, ((0, 0), (0, pad))).reshape(N_SHARD, -1, LANE)
    rest_pieces = in_halves([_pieces('l0_w_in', g_full['l0_w_in']), small_pieces])
    rest = scatter_to_chips(pair_sums(['l0_w_in', 'small'], rest_pieces, swap_halves(rest_pieces, "swap_halves_rest")))
    from_chips = dict(zip(EARLY_GRADS + ['l0_w_in', 'small'], list(early) + list(rest)))
    units = BIG + ['small']
    both = share_halves([chip_add(from_chips[n], core, "chip_add_" + n) for n in units])
    reduced = [b.reshape(-1, b.shape[2]) for b in both]

    out = {}
    for n, gred in zip(BIG, reduced):
        out[n] = (gred,) + tuple(adamw(w_loc[n], gred, m_loc[n], v_loc[n], "adamw_" + n))
    shapes = [w_loc[n].shape for n in SMALL]
    packs = [_pack_flat([d[n] for n in SMALL]) for d in (w_loc, m_loc, v_loc)]
    small_res = (reduced[-1],) + tuple(adamw(packs[0], reduced[-1], packs[1], packs[2], "adamw_small"))
    small_unpacked = [_unpack_flat(b, shapes) for b in small_res]
    for k, n in enumerate(SMALL):
        out[n] = tuple(u[k] for u in small_unpacked)
    return (loss, grad_x, *[out[n][0] for n in WEIGHTS], *[out[n][1] for n in WEIGHTS],
            *[out[n][2] for n in WEIGHTS], *[out[n][3] for n in WEIGHTS])
```

```python
import functools

import numpy as np
import jax
import jax.numpy as jnp
from jax import lax
from jax.experimental import pallas as pl
from jax.experimental.pallas import tpu as pltpu

f32 = jnp.float32
bf16 = jnp.bfloat16
HI = lax.Precision.HIGHEST
MID = lax.Precision.HIGH

D_MODEL = 1024
HEAD_DIM = 64
N_HEADS = 8
RWKV_COLS = 1792
RWKV_GN_EPS = 64e-5
SSM_STATE = 128
SSM_CHUNK = 128
L0_COLS = 3336
L0_PAD = 3456
L1_COLS = 1952
L1_PAD = 2048
MLA_ROPE = 32
ROPE_THETA = 10000.0
D_FF = 2816
DEPTH = 2
ALPHA = (2 * DEPTH) ** 0.25
ADAM_LR = 0.001
ADAM_B1 = 0.9
ADAM_B2 = 0.999
ADAM_EPS = 1e-08
ADAM_WD = 0.01
ADAM_STEP = 10
RWKV_CHUNK = 64
RWKV_HEADS_PER_STEP = 8
LANE = 128
SUBLANE = 8
VMEM_LIMIT = 56 * 1024 * 1024

WEIGHTS = ['l0_w_in', 'rwkv_mix', 'rwkv_w0', 'rwkv_w2', 'rwkv_a0', 'rwkv_a2', 'rwkv_g2', 'rwkv_k_k', 'rwkv_k_a',
           'rwkv_r_k', 'rwkv_ln_g', 'rwkv_ln_b', 'ssm_conv_w', 'ssm_conv_b', 'ssm_dt_bias', 'ssm_a_log', 'ssm_d',
           'ssm_norm_g', 'l0_w_out', 'l0_ln1_g', 'l0_ln1_b', 'ffn0_w_up', 'ffn0_conv_w', 'ffn0_conv_b',
           'ffn0_w_down', 'l0_ln2_g', 'l0_ln2_b', 'l1_w_in', 'mla_q_norm_g', 'mla_w_uq', 'mla_kv_norm_g',
           'mla_w_ukv', 'l1_w_out', 'l1_ln1_g', 'l1_ln1_b', 'ffn1_w_up', 'ffn1_conv_w', 'ffn1_conv_b',
           'ffn1_w_down', 'l1_ln2_g', 'l1_ln2_b']
COL_SHARDED = ['l0_w_in', 'rwkv_w2', 'rwkv_a2', 'rwkv_g2', 'ssm_conv_w', 'ffn0_w_up', 'ffn0_conv_w', 'l1_w_in',
               'mla_w_uq', 'mla_w_ukv', 'ffn1_w_up', 'ffn1_conv_w']
ROW_SHARDED = ['l0_w_out', 'ffn0_w_down', 'l1_w_out', 'ffn1_w_down']
BIG = ['l0_w_in', 'l0_w_out', 'ffn0_w_up', 'ffn0_w_down', 'l1_w_in', 'l1_w_out', 'ffn1_w_up', 'ffn1_w_down']
SMALL = [n for n in WEIGHTS if n not in BIG]
N_SHARD = 4


def _cparams(sem):
    return pltpu.CompilerParams(dimension_semantics=sem, vmem_limit_bytes=VMEM_LIMIT)


def _dg(a, b, ca, cb, prec=None):
    return lax.dot_general(a, b, (((ca,), (cb,)), ((), ())), precision=prec, preferred_element_type=f32)


def hdot(a, b):
    return _dg(a, b, 1, 0, HI)


def mdot(a, b):
    return _dg(a, b, 1, 0, MID)


def mdot_nt(a, b):
    return _dg(a, b, 1, 1, MID)


def mdot_tn(a, b):
    return _dg(a, b, 0, 0, MID)


def _b(x):
    return x.astype(bf16)


@jax.custom_vjp
def bdot(x, w):
    return _dg(_b(x), _b(w), 1, 0)


def _bdot_fwd(x, w):
    return bdot(x, w), (x, w)


def _bdot_bwd(res, g):
    x, w = res
    return _dg(_b(g), _b(w), 1, 1).astype(x.dtype), _dg(_b(x), _b(g), 0, 0).astype(w.dtype)


bdot.defvjp(_bdot_fwd, _bdot_bwd)


@jax.custom_vjp
def bdot_nt(x, y):
    return _dg(_b(x), _b(y), 1, 1)


def _bdot_nt_fwd(x, y):
    return bdot_nt(x, y), (x, y)


def _bdot_nt_bwd(res, g):
    x, y = res
    return _dg(_b(g), _b(y), 1, 0), _dg(_b(g), _b(x), 0, 0)


bdot_nt.defvjp(_bdot_nt_fwd, _bdot_nt_bwd)


@jax.custom_vjp
def bdot_tn(x, y):
    return _dg(_b(x), _b(y), 0, 0)


def _bdot_tn_fwd(x, y):
    return bdot_tn(x, y), (x, y)


def _bdot_tn_bwd(res, g):
    x, y = res
    return _dg(_b(y), _b(g), 1, 1), _dg(_b(x), _b(g), 1, 0)


bdot_tn.defvjp(_bdot_tn_fwd, _bdot_tn_bwd)


def _sigmoid(x):
    return 1.0 / (1.0 + jnp.exp(-x))


@jax.custom_vjp
def softplus(x):
    e = jnp.exp(-jnp.abs(x))
    u = 1.0 + e
    log1p = jnp.where(u == 1.0, e, jnp.log(u) * e / jnp.where(u == 1.0, 1.0, u - 1.0))
    return jnp.maximum(x, 0.0) + log1p


def _softplus_fwd(x):
    return softplus(x), x


def _softplus_bwd(x, g):
    return (g * _sigmoid(x),)


softplus.defvjp(_softplus_fwd, _softplus_bwd)


@jax.custom_vjp
def softplus_abs(x):
    return jnp.maximum(x, 0.0) + jnp.log(1.0 + jnp.exp(-jnp.abs(x)))


def _softplus_abs_fwd(x):
    return softplus_abs(x), x


softplus_abs.defvjp(_softplus_abs_fwd, _softplus_bwd)


def _two_pass(x, m):
    hi = _b(x)
    lo = _b(x - hi.astype(f32))
    m16 = _b(m)
    return _dg(hi, m16, 1, 0) + _dg(lo, m16, 1, 0)


def _upper(n):
    return (_iota2((n, n), 0) > _iota2((n, n), 1)).astype(f32)


@jax.custom_vjp
def suffix_sums(x):
    return _two_pass(x, _upper(x.shape[1]))


def _suffix_sums_fwd(x):
    return suffix_sums(x), None


def _suffix_sums_bwd(_, g):
    return (_two_pass(g, _upper(g.shape[1]).T),)


suffix_sums.defvjp(_suffix_sums_fwd, _suffix_sums_bwd)


def silu(x):
    return x * _sigmoid(x)


def _shift_rows(x, k, up):
    if k == 0:
        return x
    t = x.shape[0]
    rows = lax.broadcasted_iota(jnp.int32, x.shape, 0)
    if up:
        return jnp.where(rows < t - k, pltpu.roll(x, t - k, 0), 0.0)
    return jnp.where(rows >= k, pltpu.roll(x, k, 0), 0.0)


@functools.partial(jax.custom_vjp, nondiff_argnums=(1,))
def shift_down(x, k):
    return _shift_rows(x, k, False)


def _shift_down_fwd(x, k):
    return _shift_rows(x, k, False), None


def _shift_down_bwd(k, _, g):
    return (_shift_rows(g, k, True),)


shift_down.defvjp(_shift_down_fwd, _shift_down_bwd)


@functools.partial(jax.custom_vjp, nondiff_argnums=(1,))
def lane_roll(x, s):
    return pltpu.roll(x, s % x.shape[1], 1)


def _lane_roll_fwd(x, s):
    return lane_roll(x, s), None


def _lane_roll_bwd(s, _, g):
    return (pltpu.roll(g, (-s) % g.shape[1], 1),)


lane_roll.defvjp(_lane_roll_fwd, _lane_roll_bwd)


def rot_half32(x):
    first = (lax.broadcasted_iota(jnp.int32, x.shape, 1) % MLA_ROPE) < (MLA_ROPE // 2)
    return jnp.where(first, -lane_roll(x, -(MLA_ROPE // 2)), lane_roll(x, MLA_ROPE // 2))


def _iota2(shape, axis):
    return lax.broadcasted_iota(jnp.int32, shape, axis)


class Op:
    def __init__(self, arr, block, imap, diff=True, acc=None, gshape=None, gimap=None, gdtype=f32):
        self.arr, self.block, self.imap, self.diff, self.acc = arr, tuple(block), imap, diff, acc
        self.gshape = tuple(arr.shape) if gshape is None else tuple(gshape)
        self.gimap = imap if gimap is None else gimap
        self.gdtype = gdtype


class Out:
    def __init__(self, shape, block, imap, dtype=f32):
        self.shape, self.block, self.imap, self.dtype = tuple(shape), tuple(block), imap, dtype


def block_fwd(fn, name, grid, ops, outs):
    n_in = len(ops)

    def body(*refs):
        vals = [r[...] for r in refs[:n_in]]
        res = fn(*vals)
        for r, v in zip(refs[n_in:], res):
            r[...] = v.astype(r.dtype)

    res = pl.pallas_call(
        body, name=name, grid=grid,
        in_specs=[pl.BlockSpec(o.block, o.imap) for o in ops],
        out_specs=[pl.BlockSpec(o.block, o.imap) for o in outs],
        out_shape=[jax.ShapeDtypeStruct(o.shape, o.dtype) for o in outs],
        compiler_params=_cparams(("arbitrary", "arbitrary")),
    )(*[o.arr for o in ops])
    return tuple(res)


def block_bwd(fn, name, grid, ops, outs, douts):
    n_in, n_out = len(ops), len(outs)
    dix = [k for k, o in enumerate(ops) if o.diff]

    def body(*refs):
        vals = [r[...] for r in refs[:n_in]]
        dvals = tuple(r[...] for r in refs[n_in:n_in + n_out])
        grefs = refs[n_in + n_out:]

        def f(*d):
            full = list(vals)
            for k, v in zip(dix, d):
                full[k] = v
            return tuple(fn(*full))

        _, vjp = jax.vjp(f, *[vals[k] for k in dix])
        grads = vjp(dvals)
        j, i = pl.program_id(0), pl.program_id(1)
        for k, gref, g in zip(dix, grefs, grads):
            acc = ops[k].acc
            if acc is None:
                gref[...] = g.astype(gref.dtype)
            else:
                first = (i == 0) if acc == 'i' else jnp.logical_and(i == 0, j == 0)

                @pl.when(first)
                def _():
                    gref[...] = g

                @pl.when(jnp.logical_not(first))
                def _():
                    gref[...] += g

    gspecs = [pl.BlockSpec(ops[k].block, ops[k].gimap) for k in dix]
    gshapes = [jax.ShapeDtypeStruct(ops[k].gshape, ops[k].gdtype) for k in dix]
    res = pl.pallas_call(
        body, name=name, grid=grid,
        in_specs=[pl.BlockSpec(o.block, o.imap) for o in ops] + [pl.BlockSpec(o.block, o.imap) for o in outs],
        out_specs=gspecs, out_shape=gshapes,
        compiler_params=_cparams(("arbitrary", "arbitrary")),
    )(*[o.arr for o in ops], *douts)
    return tuple(res)


def _rows(arr, tm, diff=True, gdtype=f32):
    return Op(arr, (tm, arr.shape[1]), lambda j, i: (i, 0), diff=diff, gdtype=gdtype)


def _param(arr, diff=True):
    return Op(arr, arr.shape, lambda j, i: (0,) * arr.ndim, diff=diff, acc='ij')


def _rows_out(n, c, tm, dtype=f32):
    return Out((n, c), (tm, c), lambda j, i: (i, 0), dtype)


def _cols(arr, t, tc, off=0, width=None, gdtype=f32):
    width = arr.shape[1] if width is None else width
    return Op(arr, (t, tc), lambda j, i: (i, j + off), gshape=(arr.shape[0], width), gimap=lambda j, i: (i, j),
              gdtype=gdtype)


def _cparam(arr, tc):
    return Op(arr, (arr.shape[0], tc), lambda j, i: (0, j), acc='i')


def _colblock(arr, tm, off, width, gdtype=f32):
    return Op(arr, (tm, width), lambda j, i: (i, off // width), gshape=(arr.shape[0], width),
              gimap=lambda j, i: (i, 0), gdtype=gdtype)


def _tile(n, cap):
    best = None
    for t in range(LANE, min(n, cap) + 1, LANE):
        if n % t == 0:
            best = t
    return n if best is None else best


def _rtile(rows, cols, cap_bytes=2 * 1024 * 1024):
    best = None
    for t in range(SUBLANE, rows + 1, SUBLANE):
        if rows % t == 0 and t * cols * 4 <= cap_bytes:
            best = t
    return rows if best is None else best


def mm(a, b, name, ta=False, tb=False, add=None, pieces=None, into=None):
    m = a.shape[1] if ta else a.shape[0]
    kd = a.shape[0] if ta else a.shape[1]
    n = b.shape[0] if tb else b.shape[1]
    tm, tn = _tile(m, 1408), _tile(n, 1408)
    tk = kd if kd <= 2048 else _tile(kd, 2048)
    nk = kd // tk
    ca = 0 if ta else 1
    n_extra = (add is not None) + (into is not None)

    def body(*refs):
        a_ref, b_ref = refs[:2]
        o_ref, acc = refs[2 + n_extra:]
        k = pl.program_id(2)

        @pl.when(k == 0)
        def _():
            acc[...] = jnp.zeros_like(acc)

        acc[...] += _dg(_b(a_ref[...]), _b(b_ref[...]), ca, 1 if tb else 0)

        @pl.when(k == nk - 1)
        def _():
            o_ref[...] = (acc[...] if add is None else acc[...] + refs[2][...]).reshape(o_ref.shape)

    a_spec = pl.BlockSpec((tk, tm), lambda i, j, k: (k, i)) if ta else pl.BlockSpec((tm, tk), lambda i, j, k: (i, k))
    b_spec = pl.BlockSpec((tn, tk), lambda i, j, k: (j, k)) if tb else pl.BlockSpec((tk, tn), lambda i, j, k: (k, j))
    o_spec = pl.BlockSpec((tm, tn), lambda i, j, k: (i, j))
    out_shape = jax.ShapeDtypeStruct((m, n), f32)
    args, specs, aliases = [a, b], [a_spec, b_spec], {}
    if add is not None:
        args.append(add)
        specs.append(o_spec)
    if pieces is not None:
        count, first, width = pieces
        per = width // tn
        o_spec = pl.BlockSpec((1, tm, tn), lambda i, j, k: (first + j // per, i, j % per))
        out_shape = jax.ShapeDtypeStruct((count, m, width), f32)
        if into is not None:
            aliases = {len(args): 0}
            args.append(into)
            specs.append(pl.BlockSpec(memory_space=pl.ANY))
    return pl.pallas_call(
        body, name=name, grid=(m // tm, n // tn, nk), in_specs=specs, out_specs=o_spec, out_shape=out_shape,
        scratch_shapes=[pltpu.VMEM((tm, tn), f32)], input_output_aliases=aliases,
        compiler_params=_cparams(("parallel", "parallel", "arbitrary")),
    )(*args)


def f_ln(h, y, g, b):
    x = ALPHA * h + y
    mu = jnp.mean(x, axis=-1, keepdims=True)
    xc = x - mu
    var = jnp.mean(xc * xc, axis=-1, keepdims=True)
    return (xc * lax.rsqrt(var + 1e-5) * g + b,)


def f_shift_mix(p, mix):
    return (p + (shift_down(p, 1) - p) * mix,)


def f_rwkv_pre(k, wa_lo, g_lo, w0, w2, a0, a2, g2, k_k, k_a, gh):
    w_lo, a_lo = wa_lo[:, :64], wa_lo[:, 64:]
    log_w = -softplus(-(w0 + bdot(jnp.tanh(w_lo), w2))) - 0.5
    lw = -jnp.exp(log_w)
    a = _sigmoid(a0 + bdot(a_lo, a2))
    g = bdot(_sigmoid(g_lo), g2)
    kk = k * k_k
    kk = kk / jnp.maximum(jnp.sqrt(mdot(kk * kk, gh)), 1e-12)
    k2 = k * (1.0 + (a - 1.0) * k_a)
    return lw, k2, -kk, kk * a, g


def f_rwkv_post(y, r, k2, v, g, ln_g, ln_b, r_k, gh):
    mu = mdot(y, gh) * (1.0 / HEAD_DIM)
    yc = y - mu
    var = mdot(yc * yc, gh) * (1.0 / HEAD_DIM)
    yn = yc * lax.rsqrt(var + RWKV_GN_EPS) * ln_g + ln_b
    bonus = mdot(r * k2 * r_k, gh) * v
    return ((yn + bonus) * g,)


def f_conv4_silu(x, w0, w1, w2, w3, b):
    y = b + shift_down(x, 3) * w0 + shift_down(x, 2) * w1 + shift_down(x, 1) * w2 + x * w3
    return (silu(y),)


def f_ssm_post(y, z, norm_g, gg):
    u = y * silu(z)
    ms = mdot(u * u, gg) * (1.0 / 256.0)
    return (u * lax.rsqrt(ms + 1e-5) * norm_g,)


def f_ffn_act(gate, up, w0, w1, w2, b):
    gc = b + shift_down(gate, 2) * w0 + shift_down(gate, 1) * w1 + gate * w2
    return (silu(gc) * up,)


def _rms(x, g, eps=1e-6):
    return x * lax.rsqrt(jnp.mean(x * x, axis=-1, keepdims=True) + eps) * g


def f_mla_pre(c_q, c_kv, kpe, pos, q_g, w_qn, w_qr, kv_g, w_ukv, inv_q, inv_k):
    qn_in = _rms(c_q, q_g)
    q_nope = bdot(qn_in, w_qn)
    qr = bdot(qn_in, w_qr)
    kv = bdot(_rms(c_kv, kv_g), w_ukv)
    ang_q = pos * inv_q
    ang_k = pos * inv_k
    return (q_nope, qr * jnp.cos(ang_q) + rot_half32(qr) * jnp.sin(ang_q), kv,
            kpe * jnp.cos(ang_k) + rot_half32(kpe) * jnp.sin(ang_k))


def rwkv_chunk(s0, r, lw, k, v, a, b):
    hs = range(len(r))
    l = r[0].shape[0]
    ri, ci = _iota2((l, l), 0), _iota2((l, l), 1)
    strict, incl = ri > ci, ri >= ci
    tri, eye = incl.astype(f32), (ri == ci).astype(f32)
    last = (_iota2((l, 1), 0) == l - 1).astype(f32)
    c = [hdot(tri, lw[h]) for h in hs]
    at = [a[h] * jnp.exp(c[h] - lw[h]) for h in hs]
    wi = [jnp.exp(-c[h]) for h in hs]
    bt = [b[h] * wi[h] for h in hs]
    kt = [k[h] * wi[h] for h in hs]
    rt = [r[h] * jnp.exp(c[h]) for h in hs]
    nab = [jnp.where(strict, mdot_nt(at[h], bt[h]), 0.0) for h in hs]
    nak = [jnp.where(strict, bdot_nt(at[h], kt[h]), 0.0) for h in hs]
    g = [bdot_nt(at[h], s0[h]) + bdot(nak[h], v[h]) for h in hs]
    x = [eye + nab[h] for h in hs]
    p = [mdot(nab[h], nab[h]) for h in hs]
    steps = max(1, (l - 1).bit_length()) - 1
    for it in range(steps):
        x = [x[h] + mdot(p[h], x[h]) for h in hs]
        if it < steps - 1:
            p = [mdot(p[h], p[h]) for h in hs]
    u = [mdot(x[h], g[h]) for h in hs]
    mrb = [jnp.where(incl, bdot_nt(rt[h], bt[h]), 0.0) for h in hs]
    mrk = [jnp.where(incl, bdot_nt(rt[h], kt[h]), 0.0) for h in hs]
    y = [bdot_nt(rt[h], s0[h]) + bdot(mrb[h], u[h]) + bdot(mrk[h], v[h]) for h in hs]
    s1 = [(s0[h] + bdot_tn(u[h], bt[h]) + bdot_tn(v[h], kt[h])) * jnp.exp(jnp.sum(c[h] * last, axis=0, keepdims=True))
          for h in hs]
    return y, s1


def ssd_chunk(xs, bm, cm, dt_raw, s_in, dt_bias, a_log, d_skip, e_heads):
    l = xs.shape[0]
    ri, ci = _iota2((l, l), 0), _iota2((l, l), 1)
    incl = ri >= ci
    tri = incl.astype(f32)
    dt = softplus(dt_raw + dt_bias)
    a128 = dt * (-jnp.exp(a_log))
    lane0 = (_iota2((1, HEAD_DIM), 1) == 0).astype(f32)
    last = (_iota2((l, 1), 0) == l - 1).astype(f32)
    hs = range(N_HEADS)
    group = lambda m, g: m[:, g * SSM_STATE:(g + 1) * SSM_STATE]
    cb = [bdot_nt(group(cm, g), group(bm, g)) for g in range(2)]
    e_all = jnp.concatenate(e_heads, axis=1)
    dt_all = mdot(dt, e_all)
    ac_all = hdot(tri, hdot(a128, e_all))
    xd_all = xs * dt_all
    skip_all = xs * mdot(jnp.broadcast_to(d_skip, (l, LANE)), e_all)
    ac = [ac_all[:, _head(h)] for h in hs]
    xd = [xd_all[:, _head(h)] for h in hs]
    col = [jnp.broadcast_to(jnp.sum(ac[h] * lane0, axis=1, keepdims=True), (l, l)) for h in hs]
    decay = [jnp.exp(jnp.where(incl, col[h] - col[h].T, -1e30)) for h in hs]
    y_diag = [bdot(cb[h // 4] * decay[h], xd[h]) for h in hs]
    a_tot = [jnp.sum(ac[h] * last, axis=0, keepdims=True) for h in hs]
    y_off = [jnp.exp(ac[h]) * bdot(group(cm, h // 4), s_in[h]) for h in hs]
    s_out = [jnp.exp(a_tot[h]) * s_in[h] + bdot_tn(group(bm, h // 4), xd[h] * jnp.exp(a_tot[h] - ac[h])) for h in hs]
    return jnp.concatenate([y_diag[h] + y_off[h] for h in hs], axis=1) + skip_all, s_out


SB_KEYS = LANE
MLA_KEYS = 256


def sb_tile(q, k, v, run, q0, k0, masked=True):
    bq, kb = q.shape[0], k.shape[0]
    z = bdot_nt(q, k) * HEAD_DIM ** -0.5
    if masked:
        strict = (k0 + _iota2((bq, kb), 1)) < (q0 + _iota2((bq, kb), 0))
        lk = jnp.where(strict, -softplus_abs(z), 0.0)
        log_att = z + lk + suffix_sums(lk) + run
        att = jnp.where(strict, jnp.exp(jnp.where(strict, log_att, 0.0)), 0.0)
    else:
        lk = -softplus_abs(z)
        att = jnp.exp(z + lk + suffix_sums(lk) + run)
    return bdot(att, v), jnp.sum(lk, axis=1, keepdims=True)


def mla_scores(qn, qp, kn, kp, q0, k0):
    bq, kb = qn.shape[0], kn.shape[0]
    s = (bdot_nt(qn, kn) + bdot_nt(qp, kp)) * (HEAD_DIM + MLA_ROPE) ** -0.5
    causal = (k0 + _iota2((bq, kb), 1)) <= (q0 + _iota2((bq, kb), 0))
    return jnp.where(causal, s, -1e30), causal


def mla_tile_loss(qn, qp, kn, kp, v, do, lse, dsum, q0, k0):
    s, causal = mla_scores(qn, qp, kn, kp, q0, k0)
    p = jnp.where(causal, jnp.exp(s - lse), 0.0)
    return jnp.sum(do * bdot(p, v)) - jnp.sum(dsum * jnp.sum(p, axis=1, keepdims=True))


def _head(h):
    return slice(h * HEAD_DIM, (h + 1) * HEAD_DIM)


def _rwkv_specs(nc, rev):
    hp = RWKV_HEADS_PER_STEP
    w = hp * HEAD_DIM
    chunk = (lambda c: nc - 1 - c) if rev else (lambda c: c)
    tok = lambda off: pl.BlockSpec((RWKV_CHUNK, w), lambda b, g, c: (b * nc + chunk(c), off // w + g))
    st = pl.BlockSpec((1, hp, HEAD_DIM, HEAD_DIM), lambda b, g, c: ((b * (N_HEADS // hp) + g) * nc + chunk(c), 0, 0, 0))
    return tok, st


def _hosted_call(work, name, grid, in_specs, out_specs, out_shape, scratch, args, ride):
    n_in, n_out, n_scr = len(in_specs), len(out_specs), len(scratch)
    k = 0 if ride is None else len(ride.inputs)

    def body(*refs):
        ins, r_in = refs[:n_in], refs[n_in:n_in + k]
        outs, r_out = refs[n_in + k:n_in + k + n_out], refs[n_in + k + n_out:n_in + 2 * k + n_out]
        scr, r_sems = refs[n_in + 2 * k + n_out:n_in + 2 * k + n_out + n_scr], refs[n_in + 2 * k + n_out + n_scr:]
        ids = [pl.program_id(a) for a in range(len(grid))]
        if ride is not None:
            @pl.when(functools.reduce(jnp.logical_and, [i == 0 for i in ids]))
            def _():
                ride.start(r_in, r_out, r_sems)

        work(ins, outs, scr)
        if ride is not None:
            @pl.when(functools.reduce(jnp.logical_and, [i == g - 1 for i, g in zip(ids, grid)]))
            def _():
                ride.finish(r_in, r_out, r_sems)

    res = pl.pallas_call(
        body, name=name, grid=grid, in_specs=list(in_specs) + [ANY] * k, out_specs=list(out_specs) + [ANY] * k,
        out_shape=list(out_shape) + ([] if ride is None else ride.out_shapes),
        scratch_shapes=list(scratch) + ([] if ride is None else ride.scratch),
        compiler_params=_cparams(("arbitrary",) * len(grid)),
    )(*args, *([] if ride is None else ride.inputs))
    return res[:n_out], res[n_out:]


def rwkv_scan_fwd(ps, lw, k2, na, bb, nb, t, ride=None):
    hp, nc = RWKV_HEADS_PER_STEP, t // RWKV_CHUNK
    ng = N_HEADS // hp
    tok, st = _rwkv_specs(nc, False)

    def work(ins, outs, scr):
        r_ref, v_ref, lw_ref, k_ref, a_ref, b_ref = ins
        y_ref, s0_ref = outs
        (s,) = scr

        @pl.when(pl.program_id(2) == 0)
        def _():
            s[...] = jnp.zeros_like(s)

        s0_ref[0] = s[...]
        heads = lambda ref: [ref[:, _head(h)] for h in range(hp)]
        y, s1 = rwkv_chunk([s[h] for h in range(hp)], heads(r_ref), heads(lw_ref), heads(k_ref), heads(v_ref),
                           heads(a_ref), heads(b_ref))
        for h in range(hp):
            y_ref[:, _head(h)] = y[h]
            s[h] = s1[h]

    return _hosted_call(
        work, "rwkv_scan_fwd", (nb, ng, nc), [tok(0), tok(1024), tok(0), tok(0), tok(0), tok(0)], [tok(0), st],
        [jax.ShapeDtypeStruct((nb * t, N_HEADS * HEAD_DIM), f32),
         jax.ShapeDtypeStruct((nb * ng * nc, hp, HEAD_DIM, HEAD_DIM), f32)],
        [pltpu.VMEM((hp, HEAD_DIM, HEAD_DIM), f32)], (ps, ps, lw, k2, na, bb), ride)


def rwkv_scan_bwd(s0, ps, lw, k2, na, bb, dy, nb, t, ride=None):
    hp, nc = RWKV_HEADS_PER_STEP, t // RWKV_CHUNK
    ng = N_HEADS // hp
    tok, st = _rwkv_specs(nc, True)

    def work(ins, outs, scr):
        s0_ref, r_ref, v_ref, lw_ref, k_ref, a_ref, b_ref, dy_ref = ins
        (ds,) = scr

        @pl.when(pl.program_id(2) == 0)
        def _():
            ds[...] = jnp.zeros_like(ds)

        heads = lambda ref: [ref[:, _head(h)] for h in range(hp)]
        _, vjp = jax.vjp(rwkv_chunk, [s0_ref[0, h] for h in range(hp)], heads(r_ref), heads(lw_ref), heads(k_ref),
                         heads(v_ref), heads(a_ref), heads(b_ref))
        g = vjp((heads(dy_ref), [ds[h] for h in range(hp)]))
        for h in range(hp):
            ds[h] = g[0][h]
            for ref, val in zip(outs, g[1:]):
                ref[:, _head(h)] = val[h]

    return _hosted_call(
        work, "rwkv_scan_bwd", (nb, ng, nc), [st, tok(0), tok(1024), tok(0), tok(0), tok(0), tok(0), tok(0)],
        [tok(0)] * 6, [jax.ShapeDtypeStruct((nb * t, N_HEADS * HEAD_DIM), f32)] * 6,
        [pltpu.VMEM((hp, HEAD_DIM, HEAD_DIM), f32)], (s0, ps, ps, lw, k2, na, bb, dy), ride)


def _ssd_specs(nb, nch, rev):
    def row(b, c):
        return b * nch + (nch - 1 - c if rev else c)

    l = SSM_CHUNK
    xs = pl.BlockSpec((l, 512), lambda b, c: (row(b, c), 0))
    bm = pl.BlockSpec((l, 256), lambda b, c: (row(b, c), 2))
    cm = pl.BlockSpec((l, 256), lambda b, c: (row(b, c), 3))
    dt = pl.BlockSpec((l, LANE), lambda b, c: (row(b, c), (L0_PAD - LANE) // LANE))
    st = pl.BlockSpec((1, 1, N_HEADS, SSM_STATE, HEAD_DIM), lambda b, c: (b, (nch - 1 - c if rev else c), 0, 0, 0))
    par = pl.BlockSpec((1, LANE), lambda b, c: (0, 0))
    eh = pl.BlockSpec((N_HEADS, LANE, HEAD_DIM), lambda b, c: (0, 0, 0))
    return xs, bm, cm, dt, st, par, eh, row


def ssd_fwd(xbc_act, proj0, dt_bias, a_log, d_skip, e_heads, nb, t):
    nch = t // SSM_CHUNK
    n_tok = nb * t
    xs, bm, cm, dt, st, par, eh, row = _ssd_specs(nb, nch, False)

    def body(x_ref, b_ref, c_ref, dt_ref, db_ref, al_ref, dsk_ref, e_ref, y_ref, st_ref, s):
        @pl.when(pl.program_id(1) == 0)
        def _():
            s[...] = jnp.zeros_like(s)

        st_ref[0, 0] = s[...]
        y, s_out = ssd_chunk(x_ref[...], b_ref[...], c_ref[...], dt_ref[...], [s[h] for h in range(N_HEADS)],
                             db_ref[...], al_ref[...], dsk_ref[...], [e_ref[h] for h in range(N_HEADS)])
        y_ref[...] = y
        for h in range(N_HEADS):
            s[h] = s_out[h]

    return pl.pallas_call(
        body, name="ssd_fwd", grid=(nb, nch), in_specs=[xs, bm, cm, dt, par, par, par, eh],
        out_specs=[pl.BlockSpec((SSM_CHUNK, 512), lambda b, c: (row(b, c), 0)), st],
        out_shape=[jax.ShapeDtypeStruct((n_tok, 512), f32),
                   jax.ShapeDtypeStruct((nb, nch, N_HEADS, SSM_STATE, HEAD_DIM), f32)],
        scratch_shapes=[pltpu.VMEM((N_HEADS, SSM_STATE, HEAD_DIM), f32)],
        compiler_params=_cparams(("arbitrary", "arbitrary")),
    )(xbc_act, xbc_act, xbc_act, proj0, dt_bias, a_log, d_skip, e_heads)


def ssd_bwd(xbc_act, proj0, dt_bias, a_log, d_skip, e_heads, states, dy, nb, t, ride=None):
    nch = t // SSM_CHUNK
    n_tok = nb * t
    xs, bm, cm, dt, st, par, eh, row = _ssd_specs(nb, nch, True)

    def work(ins, outs, scr):
        x_ref, b_ref, c_ref, dt_ref, db_ref, al_ref, dsk_ref, e_ref, st_ref, dy_ref = ins
        dx_ref, dbm_ref, dcm_ref, ddt_ref, ddb_ref, dal_ref, ddsk_ref = outs
        (ds,) = scr
        first = jnp.logical_and(pl.program_id(0) == 0, pl.program_id(1) == 0)

        @pl.when(pl.program_id(1) == 0)
        def _():
            ds[...] = jnp.zeros_like(ds)

        e_list = [e_ref[h] for h in range(N_HEADS)]

        def f(x, bmv, cmv, dtr, s_in, dbv, alv, dskv):
            return ssd_chunk(x, bmv, cmv, dtr, s_in, dbv, alv, dskv, e_list)

        _, vjp = jax.vjp(f, x_ref[...], b_ref[...], c_ref[...], dt_ref[...],
                         [st_ref[0, 0, h] for h in range(N_HEADS)], db_ref[...], al_ref[...], dsk_ref[...])
        g = vjp((dy_ref[...], [ds[h] for h in range(N_HEADS)]))
        dx_ref[...], dbm_ref[...], dcm_ref[...], ddt_ref[...] = g[0], g[1], g[2], g[3].astype(bf16)
        for h in range(N_HEADS):
            ds[h] = g[4][h]
        for ref, val in zip((ddb_ref, dal_ref, ddsk_ref), g[5:]):
            @pl.when(first)
            def _():
                ref[...] = val

            @pl.when(jnp.logical_not(first))
            def _():
                ref[...] += val

    rows_spec = lambda w: pl.BlockSpec((SSM_CHUNK, w), lambda b, c: (row(b, c), 0))
    return _hosted_call(
        work, "ssd_bwd", (nb, nch), [xs, bm, cm, dt, par, par, par, eh, st, rows_spec(512)],
        [rows_spec(512), rows_spec(256), rows_spec(256), rows_spec(LANE), par, par, par],
        [jax.ShapeDtypeStruct((n_tok, 512), f32), jax.ShapeDtypeStruct((n_tok, 256), f32),
         jax.ShapeDtypeStruct((n_tok, 256), f32), jax.ShapeDtypeStruct((n_tok, LANE), bf16)]
        + [jax.ShapeDtypeStruct((1, LANE), f32)] * 3,
        [pltpu.VMEM((N_HEADS, SSM_STATE, HEAD_DIM), f32)],
        (xbc_act, xbc_act, xbc_act, proj0, dt_bias, a_log, d_skip, e_heads, states, dy), ride)


ATT_BQ = 512
SB_BQ = 512
SB_TILES_PER_PASS = 4
SB_HEADS_PER_STEP = 2
MLA_HEADS_PER_STEP = 4


def _loop_tiles(n_tiles, per_pass, fn, init):
    def several(i, carry):
        for r in range(per_pass):
            carry = fn(per_pass * i + r, carry)
        return carry

    return lax.fori_loop(0, n_tiles // per_pass, several, init)


def _sb_specs(t, bq, nq):
    w = SB_HEADS_PER_STEP * HEAD_DIM
    qs = lambda off: pl.BlockSpec((bq, w), lambda b, g, i: (b * nq + i, off // w + g))
    ks = lambda off: pl.BlockSpec((t, w), lambda b, g, i: (b, off // w + g))
    return qs, ks


def _sb_mass_spec(bq, nq):
    return pl.BlockSpec((bq, SB_HEADS_PER_STEP * LANE), lambda b, g, i: (b * nq + i, g))


def sb_fwd(proj1, nb, t, ride=None):
    bq = min(SB_BQ, t)
    nq = t // bq
    qs, ks = _sb_specs(t, bq, nq)

    def work(ins, outs, _):
        q_ref, k_ref, v_ref = ins
        o_ref, mass_ref = outs
        q0 = pl.program_id(2) * bq
        n_tiles = (q0 + bq) // SB_KEYS
        hs = range(SB_HEADS_PER_STEP)
        q = [q_ref[:, _head(h)] for h in hs]
        lanes = _iota2((1, LANE), 1)

        def step(i, carry, masked):
            j = n_tiles - 1 - i
            k0 = pl.multiple_of(j * SB_KEYS, SB_KEYS)
            out = []
            for h in hs:
                o, run, kept = carry[h]
                o_t, mass = sb_tile(q[h], k_ref[pl.ds(k0, SB_KEYS), _head(h)], v_ref[pl.ds(k0, SB_KEYS), _head(h)],
                                    run, q0, k0, masked)
                out.append((o + o_t, run + mass, kept + mass * (lanes == j).astype(f32)))
            return out

        diag = bq // SB_KEYS
        res = _loop_tiles(diag, SB_TILES_PER_PASS, functools.partial(step, masked=True),
                          [(jnp.zeros((bq, HEAD_DIM), f32), jnp.zeros((bq, 1), f32), jnp.zeros((bq, LANE), f32))
                           for _ in hs])
        res = _loop_tiles(n_tiles - diag, SB_TILES_PER_PASS, lambda i, cr: step(i + diag, cr, False), res)
        for h in hs:
            o_ref[:, _head(h)] = res[h][0].astype(bf16)
            mass_ref[:, h * LANE:(h + 1) * LANE] = res[h][2]

    return _hosted_call(
        work, "sb_fwd", (nb, N_HEADS // SB_HEADS_PER_STEP, nq), [qs(0), ks(512), ks(1024)],
        [qs(0), _sb_mass_spec(bq, nq)],
        [jax.ShapeDtypeStruct((nb * t, 512), bf16), jax.ShapeDtypeStruct((nb * t, N_HEADS * LANE), f32)],
        [], (proj1, proj1, proj1), ride)


def sb_bwd(proj1, masses, do, nb, t):
    bq = min(SB_BQ, t)
    nq = t // bq
    qs, ks = _sb_specs(t, bq, nq)

    def body(q_ref, k_ref, v_ref, mass_ref, do_ref, dq_ref, dk_ref, dv_ref):
        @pl.when(pl.program_id(2) == 0)
        def _():
            dk_ref[...] = jnp.zeros_like(dk_ref)
            dv_ref[...] = jnp.zeros_like(dv_ref)

        q0 = pl.program_id(2) * bq
        n_tiles = (q0 + bq) // SB_KEYS
        hs = range(SB_HEADS_PER_STEP)
        q = [q_ref[:, _head(h)] for h in hs]
        do = [do_ref[:, _head(h)].astype(f32) for h in hs]
        col0 = jnp.zeros((bq, 1), f32)
        lanes = _iota2((1, LANE), 1)
        run_all = [hdot(mass_ref[:, h * LANE:(h + 1) * LANE], _upper(LANE)) for h in hs]

        def tile(ref, k0, h):
            return ref[pl.ds(k0, SB_KEYS), _head(h)]

        def grads(j, carry, masked):
            k0 = pl.multiple_of(j * SB_KEYS, SB_KEYS)
            pick = (lanes == j).astype(f32)
            out = []
            for h in hs:
                dq, c = carry[h]
                run_in = jnp.sum(run_all[h] * pick, axis=1, keepdims=True)
                _, vjp = jax.vjp(lambda a, b, d, r: sb_tile(a, b, d, r, q0, k0, masked),
                                 q[h], tile(k_ref, k0, h), tile(v_ref, k0, h), run_in)
                dq_t, dk_t, dv_t, drun = vjp((do[h], c))
                dk_ref[pl.ds(k0, SB_KEYS), _head(h)] += dk_t
                dv_ref[pl.ds(k0, SB_KEYS), _head(h)] += dv_t
                out.append((dq + dq_t, drun + c))
            return out

        clear = n_tiles - bq // SB_KEYS
        res = _loop_tiles(clear, SB_TILES_PER_PASS, functools.partial(grads, masked=False),
                          [(jnp.zeros((bq, HEAD_DIM), f32), col0) for _ in hs])
        res = _loop_tiles(bq // SB_KEYS, SB_TILES_PER_PASS, lambda i, cr: grads(i + clear, cr, True), res)
        for h in hs:
            dq_ref[:, _head(h)] = res[h][0]

    return pl.pallas_call(
        body, name="sb_bwd", grid=(nb, N_HEADS // SB_HEADS_PER_STEP, nq),
        in_specs=[qs(0), ks(512), ks(1024), _sb_mass_spec(bq, nq), qs(0)], out_specs=[qs(0), ks(0), ks(0)],
        out_shape=[jax.ShapeDtypeStruct((nb * t, 512), f32)] * 3,
        compiler_params=_cparams(("parallel", "parallel", "arbitrary")),
    )(proj1, proj1, proj1, masses, do)


def _mla_specs(t, bq, nq):
    hp = MLA_HEADS_PER_STEP
    qn = pl.BlockSpec((bq, hp * HEAD_DIM), lambda b, g, i: (b * nq + i, g))
    qr = pl.BlockSpec((bq, hp * MLA_ROPE), lambda b, g, i: (b * nq + i, g))
    kv = pl.BlockSpec((t, hp * 2 * HEAD_DIM), lambda b, g, i: (b, g))
    kp = pl.BlockSpec((t, LANE), lambda b, g, i: (b, 0))
    return qn, qr, kv, kp


def _mla_softmax_pass(qn, qp, kv_ref, kp_ref, q0, n_tiles, bq):
    hs = range(MLA_HEADS_PER_STEP)

    def step(j, carry):
        k0 = pl.multiple_of(j * MLA_KEYS, MLA_KEYS)
        kp = kp_ref[pl.ds(k0, MLA_KEYS), :MLA_ROPE]
        out = []
        for h in hs:
            m, l, acc = carry[h]
            s, _ = mla_scores(qn[h], qp[h], kv_ref[pl.ds(k0, MLA_KEYS), _head(2 * h)], kp, q0, k0)
            m_new = jnp.maximum(m, jnp.max(s, axis=1, keepdims=True))
            alpha, p = jnp.exp(m - m_new), jnp.exp(s - m_new)
            out.append((m_new, alpha * l + jnp.sum(p, axis=1, keepdims=True),
                        alpha * acc + bdot(p, kv_ref[pl.ds(k0, MLA_KEYS), _head(2 * h + 1)])))
        return out

    init = [(jnp.full((bq, 1), -1e30, f32), jnp.zeros((bq, 1), f32), jnp.zeros((bq, HEAD_DIM), f32)) for _ in hs]
    return lax.fori_loop(0, n_tiles, step, init)


def mla_fwd(q_nope, qr, kv, kpe, nb, t):
    bq = min(ATT_BQ, t)
    nq = t // bq
    sqn, sqr, skv, skp = _mla_specs(t, bq, nq)

    def body(qn_ref, qr_ref, kv_ref, kp_ref, o_ref, o32_ref, lse_ref):
        q0 = pl.program_id(2) * bq
        hs = range(MLA_HEADS_PER_STEP)
        qn = [qn_ref[:, _head(h)] for h in hs]
        qp = [qr_ref[:, h * MLA_ROPE:(h + 1) * MLA_ROPE] for h in hs]
        res = _mla_softmax_pass(qn, qp, kv_ref, kp_ref, q0, (q0 + bq) // MLA_KEYS, bq)
        for h in hs:
            m, l, acc = res[h]
            o = acc / l
            o_ref[:, _head(h)] = o.astype(bf16)
            o32_ref[:, _head(h)] = o
            lse_ref[:, _head(h)] = jnp.broadcast_to(m + jnp.log(l), (bq, HEAD_DIM))

    n = nb * t
    return pl.pallas_call(
        body, name="mla_fwd", grid=(nb, N_HEADS // MLA_HEADS_PER_STEP, nq), in_specs=[sqn, sqr, skv, skp],
        out_specs=[sqn, sqn, sqn],
        out_shape=[jax.ShapeDtypeStruct((n, 512), bf16), jax.ShapeDtypeStruct((n, 512), f32),
                   jax.ShapeDtypeStruct((n, 512), f32)],
        compiler_params=_cparams(("parallel", "arbitrary", "arbitrary")),
    )(q_nope, qr, kv, kpe)


def mla_bwd(q_nope, qr, kv, kpe, o32, lse_b, do, nb, t):
    bq = min(ATT_BQ, t)
    nq = t // bq
    sqn, sqr, skv, skp = _mla_specs(t, bq, nq)

    def body(qn_ref, qr_ref, kv_ref, kp_ref, o_ref, lse_ref, do_ref, dqn_ref, dqr_ref, dkv_ref, dkp_ref):
        first_q = pl.program_id(2) == 0

        @pl.when(first_q)
        def _():
            dkv_ref[...] = jnp.zeros_like(dkv_ref)

        @pl.when(jnp.logical_and(first_q, pl.program_id(1) == 0))
        def _():
            dkp_ref[...] = jnp.zeros_like(dkp_ref)

        q0 = pl.program_id(2) * bq
        n_tiles = (q0 + bq) // MLA_KEYS
        hs = range(MLA_HEADS_PER_STEP)
        qn = [qn_ref[:, _head(h)] for h in hs]
        qp = [qr_ref[:, h * MLA_ROPE:(h + 1) * MLA_ROPE] for h in hs]
        do = [do_ref[:, _head(h)].astype(f32) for h in hs]
        lse = [lse_ref[:, h * HEAD_DIM:h * HEAD_DIM + 1] for h in hs]
        dsum = [jnp.sum(do[h] * o_ref[:, _head(h)], axis=1, keepdims=True) for h in hs]

        def grads(j, carry):
            k0 = pl.multiple_of(j * MLA_KEYS, MLA_KEYS)
            rows = pl.ds(k0, MLA_KEYS)
            kp = kp_ref[rows, :MLA_ROPE]
            out = []
            for h in hs:
                dqn, dqp = carry[h]
                g = jax.grad(mla_tile_loss, argnums=(0, 1, 2, 3, 4))(
                    qn[h], qp[h], kv_ref[rows, _head(2 * h)], kp, kv_ref[rows, _head(2 * h + 1)],
                    do[h], lse[h], dsum[h], q0, k0)
                dkv_ref[rows, _head(2 * h)] += g[2]
                dkp_ref[rows, :MLA_ROPE] += g[3]
                dkv_ref[rows, _head(2 * h + 1)] += g[4]
                out.append((dqn + g[0], dqp + g[1]))
            return out

        res = lax.fori_loop(0, n_tiles, grads,
                            [(jnp.zeros((bq, HEAD_DIM), f32), jnp.zeros((bq, MLA_ROPE), f32)) for _ in hs])
        for h in hs:
            dqn_ref[:, _head(h)] = res[h][0]
            dqr_ref[:, h * MLA_ROPE:(h + 1) * MLA_ROPE] = res[h][1]

    n = nb * t
    return pl.pallas_call(
        body, name="mla_bwd", grid=(nb, N_HEADS // MLA_HEADS_PER_STEP, nq),
        in_specs=[sqn, sqr, skv, skp, sqn, sqn, sqn], out_specs=[sqn, sqr, skv, skp],
        out_shape=[jax.ShapeDtypeStruct((n, 512), f32), jax.ShapeDtypeStruct((n, N_HEADS * MLA_ROPE), f32),
                   jax.ShapeDtypeStruct((n, 1024), f32), jax.ShapeDtypeStruct((n, LANE), f32)],
        compiler_params=_cparams(("arbitrary", "arbitrary", "arbitrary")),
    )(q_nope, qr, kv, kpe, o32, lse_b, do)


def loss_head(h, target):
    n, d = h.shape
    tm = _tile(n, 512)

    def body(h_ref, t_ref, l_ref, dh_ref):
        diff = h_ref[...] - t_ref[...]
        dh_ref[...] = diff * (1.0 / d)
        part = 0.5 * jnp.sum(jnp.sum(diff * diff, axis=1, keepdims=True) * (1.0 / d), axis=0, keepdims=True)

        @pl.when(pl.program_id(0) == 0)
        def _():
            l_ref[...] = jnp.zeros_like(l_ref)

        l_ref[...] += jnp.broadcast_to(part, l_ref.shape)

    spec = pl.BlockSpec((tm, d), lambda i: (i, 0))
    return pl.pallas_call(
        body, name="loss_head", grid=(n // tm,), in_specs=[spec, spec],
        out_specs=[pl.BlockSpec((8, LANE), lambda i: (0, 0)), spec],
        out_shape=[jax.ShapeDtypeStruct((8, LANE), f32), jax.ShapeDtypeStruct((n, d), f32)],
        compiler_params=_cparams(("arbitrary",)),
    )(h, target)


def _row(v):
    return v.reshape(1, -1)


def _pad_cols(a, n):
    return jnp.pad(a, ((0, 0), (0, n - a.shape[1])))


def _pad_row(v, n=LANE):
    return jnp.pad(v.reshape(1, -1), ((0, 0), (0, n - v.shape[0])))


def _group_matrix(width, group):
    idx = np.arange(width) // group
    return jnp.asarray((idx[:, None] == idx[None, :]).astype(np.float32))


def _head_expand():
    e = np.zeros((N_HEADS, LANE, HEAD_DIM), np.float32)
    for h in range(N_HEADS):
        e[h, h, :] = 1.0
    return jnp.asarray(e)


def _rope_freqs():
    inv = 1.0 / (ROPE_THETA ** (np.arange(0, MLA_ROPE, 2, dtype=np.float32) / MLA_ROPE))
    inv = np.tile(inv.astype(np.float32), 2)
    inv_q = np.tile(inv, N_HEADS).reshape(1, N_HEADS * MLA_ROPE)
    inv_k = np.zeros((1, LANE), np.float32)
    inv_k[0, :MLA_ROPE] = inv
    return jnp.asarray(inv_q), jnp.asarray(inv_k)


def _uq_split(w):
    w3 = w.reshape(w.shape[0], N_HEADS, HEAD_DIM + MLA_ROPE)
    return w3[:, :, :HEAD_DIM].reshape(-1, 512), w3[:, :, HEAD_DIM:].reshape(-1, N_HEADS * MLA_ROPE)


def _uq_merge(gn, gr):
    r = gn.shape[0]
    return jnp.concatenate([gn.reshape(r, N_HEADS, HEAD_DIM), gr.reshape(r, N_HEADS, MLA_ROPE)], axis=2).reshape(r, 768)


EARLY_GRADS = ['ffn1_w_up', 'ffn1_w_down', 'l1_w_in', 'l1_w_out', 'ffn0_w_up', 'ffn0_w_down', 'l0_w_out']


def local_step(x, positions, target, w, late_weights=None, scatter_early=None):
    w = dict(w)
    nb, t, d = x.shape
    n = nb * t
    tm = 256
    ni = n // tm
    tc = 2 * LANE
    h0 = x.reshape(n, d)
    tgt = target.reshape(n, d)
    pos = positions.reshape(n, 1).astype(f32)
    gh = _group_matrix(512, HEAD_DIM)
    gg = _group_matrix(512, 256)
    e_heads = _head_expand()
    inv_q, inv_k = _rope_freqs()
    g = {}

    def ln_stage(h, y, gname, bname):
        ops = [_rows(h, tm), _rows(y, tm, gdtype=bf16), _param(_row(w[gname])), _param(_row(w[bname]))]
        return ops, [_rows_out(n, d, tm)]

    def ln_fwd(name, ops):
        return block_fwd(lambda *a: f_ln(*a) * 2, name, (1, ni), ops, [_rows_out(n, d, tm), _rows_out(n, d, tm, bf16)])

    def ffn_act_stage(u, cw, cb):
        nj = D_FF // tc
        ops = [_cols(u, t, tc, 0, D_FF, bf16), _cols(u, t, tc, nj, D_FF, bf16)] \
            + [_cparam(cw[i:i + 1], tc) for i in range(3)] + [_cparam(_row(cb), tc)]
        return ops, [Out((n, D_FF), (t, tc), lambda j, i: (i, j), bf16)], (nj, nb)

    w_in0 = _pad_cols(w['l0_w_in'], L0_PAD)
    h0b = h0.astype(bf16)
    proj0 = mm(h0b, w_in0, "l0_proj")

    shift_ops = [_cols(proj0, t, tc, 0, RWKV_COLS, bf16), _cparam(_row(w['rwkv_mix']), tc)]
    shift_outs = [Out((n, RWKV_COLS), (t, tc), lambda j, i: (i, j))]
    shift_grid = (RWKV_COLS // tc, nb)
    (ps,) = block_fwd(f_shift_mix, "rwkv_shift", shift_grid, shift_ops, shift_outs)

    pre_ops = [_colblock(ps, tm, 512, 512), _colblock(ps, tm, 1536, 128), _colblock(ps, tm, 1664, 128),
               _param(_row(w['rwkv_w0'])), _param(w['rwkv_w2']), _param(_row(w['rwkv_a0'])), _param(w['rwkv_a2']),
               _param(w['rwkv_g2']), _param(_row(w['rwkv_k_k'])), _param(_row(w['rwkv_k_a'])), _param(gh, diff=False)]
    pre_outs = [_rows_out(n, 512, tm) for _ in range(5)]
    lw, k2, na, bb, gate_r = block_fwd(f_rwkv_pre, "rwkv_pre", (1, ni), pre_ops, pre_outs)
    def arrived(group, gathered):
        if late_weights is not None:
            for name, got in zip(late_weights[group][0], gathered):
                w[name] = late_weights[2](name, got)

    ride = None if late_weights is None else GatherRide(late_weights[0][1])
    (y_tok, s0_saved), gathered = rwkv_scan_fwd(ps, lw, k2, na, bb, nb, t, ride)
    arrived(0, gathered)
    w_out0 = w['l0_w_out']

    post_ops = [_rows(y_tok, tm), _colblock(ps, tm, 0, 512), _rows(k2, tm), _colblock(ps, tm, 1024, 512),
                _rows(gate_r, tm), _param(_row(w['rwkv_ln_g'])), _param(_row(w['rwkv_ln_b'])),
                _param(w['rwkv_r_k'].reshape(1, 512)), _param(gh, diff=False)]
    post_outs = [_rows_out(n, 512, tm, bf16)]
    (y_a,) = block_fwd(f_rwkv_post, "rwkv_post", (1, ni), post_ops, post_outs)

    xbc_off = (RWKV_COLS + 512) // tc
    conv_ops = [_cols(proj0, t, tc, xbc_off, 1024, bf16)] + [_cparam(w['ssm_conv_w'][i:i + 1], tc) for i in range(4)] \
        + [_cparam(_row(w['ssm_conv_b']), tc)]
    conv_outs = [Out((n, 1024), (t, tc), lambda j, i: (i, j))]
    conv_grid = (1024 // tc, nb)
    (xbc_act,) = block_fwd(f_conv4_silu, "ssm_conv", conv_grid, conv_ops, conv_outs)

    dt_bias, a_log, d_skip = _pad_row(w['ssm_dt_bias']), _pad_row(w['ssm_a_log']), _pad_row(w['ssm_d'])
    y_ssd, ssd_states = ssd_fwd(xbc_act, proj0, dt_bias, a_log, d_skip, e_heads, nb, t)

    z_tok = proj0[:, RWKV_COLS:RWKV_COLS + 512]
    spost_ops = [_rows(y_ssd, tm), _rows(z_tok, tm, gdtype=bf16), _param(_row(w['ssm_norm_g'])), _param(gg, diff=False)]
    spost_outs = [_rows_out(n, 512, tm, bf16)]
    (y_b,) = block_fwd(f_ssm_post, "ssm_post", (1, ni), spost_ops, spost_outs)

    mixed0 = mm(y_b, w_out0[512:], "l0_out_b", add=mm(y_a, w_out0[:512], "l0_out_a"))
    ln1_ops, ln_outs = ln_stage(h0, mixed0, 'l0_ln1_g', 'l0_ln1_b')
    h1, h1b = ln_fwd("l0_ln1", ln1_ops)

    u0 = mm(h1b, w['ffn0_w_up'], "ffn0_up")
    act0_ops, act_outs, act_grid = ffn_act_stage(u0, w['ffn0_conv_w'], w['ffn0_conv_b'])
    (act0,) = block_fwd(f_ffn_act, "ffn0_act", act_grid, act0_ops, act_outs)
    f0 = mm(act0, w['ffn0_w_down'], "ffn0_down")
    ln2_ops, _ = ln_stage(h1, f0, 'l0_ln2_g', 'l0_ln2_b')
    h2, h2b = ln_fwd("l0_ln2", ln2_ops)

    w_in1 = _pad_cols(w['l1_w_in'], L1_PAD)
    proj1 = mm(h2b, w_in1, "l1_proj")
    w_qn, w_qr = _uq_split(w['mla_w_uq'])
    mpre_ops = [_colblock(proj1, tm, 1536, 256, bf16), _colblock(proj1, tm, 1792, 128, bf16),
                _colblock(proj1, tm, 1920, 128, bf16),
                Op(pos, (tm, 1), lambda j, i: (i, 0), diff=False),
                _param(_row(w['mla_q_norm_g'])), _param(w_qn), _param(w_qr),
                _param(_row(w['mla_kv_norm_g'])), _param(w['mla_w_ukv']), _param(inv_q, diff=False),
                _param(inv_k, diff=False)]
    mpre_outs = [_rows_out(n, 512, tm), _rows_out(n, N_HEADS * MLA_ROPE, tm), _rows_out(n, 1024, tm),
                 _rows_out(n, LANE, tm)]
    q_nope, q_rope, kv, kpe = block_fwd(f_mla_pre, "mla_pre", (1, ni), mpre_ops, mpre_outs)
    ride = None if late_weights is None else GatherRide(late_weights[1][1])
    (o_sb, sb_masses), gathered = sb_fwd(proj1, nb, t, ride)
    arrived(1, gathered)
    w_out1 = w['l1_w_out']
    o_mla, o_mla32, mla_lse = mla_fwd(q_nope, q_rope, kv, kpe, nb, t)

    mixed1 = mm(o_mla, w_out1[512:], "l1_out_b", add=mm(o_sb, w_out1[:512], "l1_out_a"))
    ln3_ops, _ = ln_stage(h2, mixed1, 'l1_ln1_g', 'l1_ln1_b')
    h3, h3b = ln_fwd("l1_ln1", ln3_ops)
    u1 = mm(h3b, w['ffn1_w_up'], "ffn1_up")
    act1_ops, _, _ = ffn_act_stage(u1, w['ffn1_conv_w'], w['ffn1_conv_b'])
    (act1,) = block_fwd(f_ffn_act, "ffn1_act", act_grid, act1_ops, act_outs)
    f1 = mm(act1, w['ffn1_w_down'], "ffn1_down")
    ln4_ops, _ = ln_stage(h3, f1, 'l1_ln2_g', 'l1_ln2_b')
    (h4,) = block_fwd(f_ln, "l1_ln2", (1, ni), ln4_ops, ln_outs)

    loss_part, dh4 = loss_head(h4, tgt)

    def vec(a_):
        return a_.reshape(-1)

    def ffn_bwd(tag, dh_out, ln_ops, act_ops, h_in, act, w_up, w_down, names):
        dh_res, df, gg_, gb_ = block_bwd(f_ln, tag + "_ln2_bwd", (1, ni), ln_ops, ln_outs, [dh_out])
        g[names[4]], g[names[5]] = vec(gg_), vec(gb_)
        g[names[3]] = mm(act, df, tag + "_down_dw", ta=True)
        dact = mm(df, w_down, tag + "_down_dx", tb=True)
        dgate, dup, dw0, dw1, dw2, dcb = block_bwd(f_ffn_act, tag + "_act_bwd", act_grid, act_ops, act_outs, [dact])
        g[names[1]] = jnp.concatenate([dw0, dw1, dw2], axis=0)
        g[names[2]] = vec(dcb)
        quarter = 2 * D_FF // N_SHARD
        g[names[0]] = mm(h_in, dup, tag + "_upv_dw", ta=True, pieces=(N_SHARD, 2, quarter),
                         into=mm(h_in, dgate, tag + "_gate_dw", ta=True, pieces=(N_SHARD, 0, quarter)))
        dh = mm(dgate, w_up[:, :D_FF], tag + "_gate_dx", tb=True, add=dh_res)
        return mm(dup, w_up[:, D_FF:], tag + "_upv_dx", tb=True, add=dh)

    def out_bwd(tag, dmixed, y_first, y_second, w_out, name):
        g[name] = jnp.concatenate([mm(y_first, dmixed, tag + "_a_dw", ta=True),
                                   mm(y_second, dmixed, tag + "_b_dw", ta=True)], axis=0)
        return (mm(dmixed, w_out[:512], tag + "_a_dx", tb=True), mm(dmixed, w_out[512:], tag + "_b_dx", tb=True))

    dh3 = ffn_bwd("ffn1", dh4, ln4_ops, act1_ops, h3b, act1, w['ffn1_w_up'], w['ffn1_w_down'],
                  ['ffn1_w_up', 'ffn1_conv_w', 'ffn1_conv_b', 'ffn1_w_down', 'l1_ln2_g', 'l1_ln2_b'])

    dh2_res, dmixed1, g3g, g3b = block_bwd(f_ln, "l1_ln1_bwd", (1, ni), ln3_ops, ln_outs, [dh3])
    g['l1_ln1_g'], g['l1_ln1_b'] = vec(g3g), vec(g3b)
    do_sb, do_mla = out_bwd("l1_out", dmixed1, o_sb, o_mla, w_out1, 'l1_w_out')

    dq_nope, dq_rope, dkv, dkpe = mla_bwd(q_nope, q_rope, kv, kpe, o_mla32, mla_lse, do_mla, nb, t)
    dsb_q, dsb_k, dsb_v = sb_bwd(proj1, sb_masses, do_sb, nb, t)
    (dc_q, dc_kv, dkpe_raw, gqg, gwqn, gwqr, gkvg, g['mla_w_ukv']) = block_bwd(
        f_mla_pre, "mla_pre_bwd", (1, ni), mpre_ops, mpre_outs, [dq_nope, dq_rope, dkv, dkpe])
    g['mla_q_norm_g'], g['mla_kv_norm_g'] = vec(gqg), vec(gkvg)
    g['mla_w_uq'] = _uq_merge(gwqn, gwqr)
    dproj1 = jnp.concatenate([dsb_q.astype(bf16), dsb_k.astype(bf16), dsb_v.astype(bf16), dc_q, dc_kv, dkpe_raw],
                             axis=1)
    g['l1_w_in'] = mm(h2b, dproj1, "l1_proj_dw", ta=True)[:, :L1_COLS]
    dh2 = mm(dproj1, w_in1, "l1_proj_dx", tb=True, add=dh2_res)

    dh1 = ffn_bwd("ffn0", dh2, ln2_ops, act0_ops, h1b, act0, w['ffn0_w_up'], w['ffn0_w_down'],
                  ['ffn0_w_up', 'ffn0_conv_w', 'ffn0_conv_b', 'ffn0_w_down', 'l0_ln2_g', 'l0_ln2_b'])

    dh0_res, dmixed0, g1g, g1b = block_bwd(f_ln, "l0_ln1_bwd", (1, ni), ln1_ops, ln_outs, [dh1])
    g['l0_ln1_g'], g['l0_ln1_b'] = vec(g1g), vec(g1b)
    dy_a, dy_b = out_bwd("l0_out", dmixed0, y_a, y_b, w_out0, 'l0_w_out')

    dy_ssd, dz, gng = block_bwd(f_ssm_post, "ssm_post_bwd", (1, ni), spost_ops, spost_outs, [dy_b])
    g['ssm_norm_g'] = vec(gng)
    early_pieces = None if scatter_early is None else scatter_early[0]({name: g[name] for name in EARLY_GRADS})
    ride = None if scatter_early is None else SwapRide(early_pieces)
    (dxs, dbm, dcm, ddt_raw, gdb, gal, gdsk), from_sibling = ssd_bwd(
        xbc_act, proj0, dt_bias, a_log, d_skip, e_heads, ssd_states, dy_ssd, nb, t, ride)
    g['ssm_dt_bias'], g['ssm_a_log'], g['ssm_d'] = gdb[0, :8], gal[0, :8], gdsk[0, :8]
    dxbc_act = jnp.concatenate([dxs, dbm, dcm], axis=1)
    dxbc, cw0, cw1, cw2, cw3, gcb = block_bwd(f_conv4_silu, "ssm_conv_bwd", conv_grid, conv_ops, conv_outs, [dxbc_act])
    g['ssm_conv_w'] = jnp.concatenate([cw0, cw1, cw2, cw3], axis=0)
    g['ssm_conv_b'] = vec(gcb)

    dy_tok, dr_post, dk2_post, dv_post, dgate, glg, glb, grk = block_bwd(
        f_rwkv_post, "rwkv_post_bwd", (1, ni), post_ops, post_outs, [dy_a])
    g['rwkv_ln_g'], g['rwkv_ln_b'], g['rwkv_r_k'] = vec(glg), vec(glb), grk.reshape(N_HEADS, HEAD_DIM)
    ride = None if scatter_early is None else ScatterRide(scatter_early[1](early_pieces, from_sibling))
    (dr, dlw, dk2, dv, dna, dbb), early = rwkv_scan_bwd(s0_saved, ps, lw, k2, na, bb, dy_tok, nb, t, ride)
    (dk_pre, dwa_lo, dg_lo, gw0, g['rwkv_w2'], ga0, g['rwkv_a2'], g['rwkv_g2'], gkk, gka) = block_bwd(
        f_rwkv_pre, "rwkv_pre_bwd", (1, ni), pre_ops, pre_outs, [dlw, dk2 + dk2_post, dna, dbb, dgate])
    g['rwkv_w0'], g['rwkv_a0'], g['rwkv_k_k'], g['rwkv_k_a'] = vec(gw0), vec(ga0), vec(gkk), vec(gka)
    dps = jnp.concatenate([dr + dr_post, dk_pre, dv + dv_post, dwa_lo, dg_lo], axis=1)
    dp_rwkv, gmix = block_bwd(f_shift_mix, "rwkv_shift_bwd", shift_grid, shift_ops, shift_outs, [dps])
    g['rwkv_mix'] = vec(gmix)

    dproj0 = jnp.concatenate([dp_rwkv, dz, dxbc, ddt_raw], axis=1)
    g['l0_w_in'] = mm(h0b, dproj0, "l0_proj_dw", ta=True)[:, :L0_COLS]
    grad_x = mm(dproj0, w_in0, "l0_proj_dx", tb=True, add=dh0_res)
    return loss_part, grad_x.reshape(nb, t, d), g, early


MESH = pl.DeviceIdType.MESH
ANY = pl.BlockSpec(memory_space=pl.ANY)
AXES = ("x", "y", "c")


def _place():
    x, y, c = lax.axis_index("x"), lax.axis_index("y"), lax.axis_index("c")
    chips = [(1 - x, y), (x, 1 - y), (1 - x, 1 - y)]
    return x, y, c, chips


def _dma_sems(n):
    return pltpu.SemaphoreType.DMA((n,))


class GatherRide:
    def __init__(self, shards):
        n = len(shards)
        self.inputs = list(shards)
        self.out_shapes = [jax.ShapeDtypeStruct((N_SHARD,) + a.shape, a.dtype) for a in shards]
        self.scratch = [_dma_sems(3 * n), _dma_sems(3 * n), _dma_sems(3 * n), _dma_sems(3 * n), _dma_sems(n)]

    def _copies(self, ins, outs, sems):
        ici_send, ici_recv, d2d_send, d2d_recv, local_sems = sems
        x, y, c, chips = _place()
        me = 2 * x + y
        pairs = list(enumerate(zip(ins, outs)))

        def over_ici(k, j, slot, to):
            return pltpu.make_async_remote_copy(
                src_ref=ins[k].at[c], dst_ref=outs[k].at[slot, c], send_sem=ici_send.at[3 * k + j],
                recv_sem=ici_recv.at[3 * k + j], device_id=to, device_id_type=MESH)

        def to_sibling(k, j, slot, half):
            return pltpu.make_async_remote_copy(
                src_ref=outs[k].at[slot, half], dst_ref=outs[k].at[slot, half], send_sem=d2d_send.at[3 * k + j],
                recv_sem=d2d_recv.at[3 * k + j], device_id=(x, y, 1 - c), device_id_type=MESH)

        mine = [pltpu.make_async_copy(a, o.at[me], local_sems.at[k]) for k, (a, o) in pairs]
        sends = [over_ici(k, j, me, (cx, cy, c)) for k, _ in pairs for j, (cx, cy) in enumerate(chips)]
        return c, chips, pairs, over_ici, to_sibling, mine, sends

    def start(self, ins, outs, sems):
        _, _, _, _, _, mine, sends = self._copies(ins, outs, sems)
        for cp in mine + sends:
            cp.start()

    def finish(self, ins, outs, sems):
        c, chips, pairs, over_ici, to_sibling, mine, sends = self._copies(ins, outs, sems)
        passed = []
        for k, _ in pairs:
            for j, (cx, cy) in enumerate(chips):
                over_ici(k, j, 2 * cx + cy, (cx, cy, c)).wait_recv()
                passed.append(to_sibling(k, j, 2 * cx + cy, c))
                passed[-1].start()
        for k, _ in pairs:
            for j, (cx, cy) in enumerate(chips):
                to_sibling(k, j, 2 * cx + cy, 1 - c).wait_recv()
        for cp in sends + passed:
            cp.wait_send()
        for cp in mine:
            cp.wait()


class ScatterRide:
    def __init__(self, parts):
        n = len(parts)
        self.inputs = list(parts)
        self.out_shapes = [jax.ShapeDtypeStruct(a.shape, a.dtype) for a in parts]
        self.scratch = [_dma_sems(3 * n), _dma_sems(3 * n), _dma_sems(n)]

    def _copies(self, ins, outs, sems):
        send_sems, recv_sems, local_sems = sems
        x, y, c, chips = _place()
        me = 2 * x + y
        pairs = list(enumerate(zip(ins, outs)))

        def over_ici(k, j, src_slot, dst_slot, to):
            return pltpu.make_async_remote_copy(
                src_ref=ins[k].at[src_slot], dst_ref=outs[k].at[dst_slot], send_sem=send_sems.at[3 * k + j],
                recv_sem=recv_sems.at[3 * k + j], device_id=to, device_id_type=MESH)

        mine = [pltpu.make_async_copy(a.at[me], o.at[me], local_sems.at[k]) for k, (a, o) in pairs]
        sends = [over_ici(k, j, 2 * cx + cy, me, (cx, cy, c)) for k, _ in pairs for j, (cx, cy) in enumerate(chips)]
        arrivals = lambda: [over_ici(k, j, me, 2 * cx + cy, (cx, cy, c))
                            for k, _ in pairs for j, (cx, cy) in enumerate(chips)]
        return mine, sends, arrivals

    def start(self, ins, outs, sems):
        mine, sends, _ = self._copies(ins, outs, sems)
        for cp in mine + sends:
            cp.start()

    def finish(self, ins, outs, sems):
        mine, sends, arrivals = self._copies(ins, outs, sems)
        for cp in arrivals():
            cp.wait_recv()
        for cp in sends:
            cp.wait_send()
        for cp in mine:
            cp.wait()


def _run_ride(ride, name):
    n = len(ride.inputs)

    def body(*refs):
        ins, outs, sems = refs[:n], refs[n:2 * n], refs[2 * n:]
        ride.start(ins, outs, sems)
        ride.finish(ins, outs, sems)

    return pl.pallas_call(body, name=name, in_specs=[ANY] * n, out_specs=[ANY] * n, out_shape=ride.out_shapes,
                          scratch_shapes=ride.scratch)(*ride.inputs)


def gather_shards(shards):
    return _run_ride(GatherRide(shards), "gather_shards")


class SwapRide:
    def __init__(self, pieces):
        n = len(pieces)
        self.inputs = list(pieces)
        self.out_shapes = [jax.ShapeDtypeStruct((a.shape[0],) + a.shape[2:], a.dtype) for a in pieces]
        self.scratch = [_dma_sems(n), _dma_sems(n)]

    def _copies(self, ins, outs, sems):
        send_sems, recv_sems = sems
        x, y, c, _ = _place()
        return [pltpu.make_async_remote_copy(
            src_ref=a.at[:, 1 - c], dst_ref=o, send_sem=send_sems.at[k], recv_sem=recv_sems.at[k],
            device_id=(x, y, 1 - c), device_id_type=MESH) for k, (a, o) in enumerate(zip(ins, outs))]

    def start(self, ins, outs, sems):
        for cp in self._copies(ins, outs, sems):
            cp.start()

    def finish(self, ins, outs, sems):
        for cp in self._copies(ins, outs, sems):
            cp.wait()


def swap_halves(pieces, name):
    return _run_ride(SwapRide(pieces), name)


def scatter_to_chips(parts):
    return _run_ride(ScatterRide(parts), "scatter_to_chips")


def share_halves(bufs):
    n = len(bufs)

    def body(*refs):
        ins, outs = refs[:n], refs[n:2 * n]
        send_sems, recv_sems = refs[2 * n:]
        x, y, c, _ = _place()
        cps = [pltpu.make_async_remote_copy(
            src_ref=a.at[c], dst_ref=o.at[c], send_sem=send_sems.at[k], recv_sem=recv_sems.at[k],
            device_id=(x, y, 1 - c), device_id_type=MESH) for k, (a, o) in enumerate(zip(ins, outs))]
        for cp in cps:
            cp.start()
        for k, (a, o) in enumerate(zip(ins, outs)):
            cps[k].wait_send()
            pltpu.make_async_remote_copy(
                src_ref=a.at[c], dst_ref=o.at[1 - c], send_sem=send_sems.at[k], recv_sem=recv_sems.at[k],
                device_id=(x, y, 1 - c), device_id_type=MESH).wait_recv()

    return pl.pallas_call(
        body, name="share_halves", in_specs=[ANY] * n, out_specs=[ANY] * n,
        out_shape=[jax.ShapeDtypeStruct(a.shape, a.dtype) for a in bufs],
        input_output_aliases={k: k for k in range(n)},
        scratch_shapes=[_dma_sems(n), _dma_sems(n)],
    )(*bufs)


def pair_add(piece, recv, core, name, out_dtype):
    _, _, h, cdim = piece.shape
    tr = _rtile(h, cdim)

    def body(c_ref, a_ref, b_ref, o_ref):
        o_ref[...] = (a_ref[0] + b_ref[...]).astype(o_ref.dtype)

    spec = pl.BlockSpec((1, tr, cdim), lambda p, i, c_ref: (p, i, 0))
    return pl.pallas_call(
        body, name=name,
        grid_spec=pltpu.PrefetchScalarGridSpec(
            num_scalar_prefetch=1, grid=(N_SHARD, h // tr),
            in_specs=[pl.BlockSpec((1, 1, tr, cdim), lambda p, i, c_ref: (p, c_ref[0], i, 0)), spec],
            out_specs=spec),
        out_shape=jax.ShapeDtypeStruct((N_SHARD, h, cdim), out_dtype),
        compiler_params=_cparams(("parallel", "parallel")),
    )(core, piece, recv)


def chip_add(parts, core, name):
    _, h, cdim = parts.shape
    tr = _rtile(h, cdim, 1024 * 1024)

    def body(c_ref, p_ref, o_ref):
        p = [p_ref[s].astype(f32) for s in range(N_SHARD)]
        o_ref[0] = ((p[0] + p[1]) + p[2]) + p[3]

    return pl.pallas_call(
        body, name=name,
        grid_spec=pltpu.PrefetchScalarGridSpec(
            num_scalar_prefetch=1, grid=(h // tr,),
            in_specs=[pl.BlockSpec((N_SHARD, tr, cdim), lambda i, c_ref: (0, i, 0))],
            out_specs=pl.BlockSpec((1, tr, cdim), lambda i, c_ref: (c_ref[0], i, 0))),
        out_shape=jax.ShapeDtypeStruct((2, h, cdim), f32), compiler_params=_cparams(("parallel",)),
    )(core, parts)


def adamw(w, g, m, v, name):
    rows, cdim = w.shape
    tr = _rtile(rows, cdim, 1024 * 1024)

    def body(w_ref, g_ref, m_ref, v_ref, d_ref, nm_ref, nv_ref):
        gv = g_ref[...]
        m_new = ADAM_B1 * m_ref[...] + (1.0 - ADAM_B1) * gv
        v_new = ADAM_B2 * v_ref[...] + (1.0 - ADAM_B2) * jnp.square(gv)
        m_hat = m_new / (1.0 - ADAM_B1 ** ADAM_STEP)
        v_hat = v_new / (1.0 - ADAM_B2 ** ADAM_STEP)
        d_ref[...] = -ADAM_LR * (m_hat / (jnp.sqrt(v_hat) + ADAM_EPS) + ADAM_WD * w_ref[...])
        nm_ref[...] = m_new
        nv_ref[...] = v_new

    spec = pl.BlockSpec((tr, cdim), lambda i: (i, 0))
    return pl.pallas_call(body, name=name, grid=(rows // tr,), in_specs=[spec] * 4, out_specs=[spec] * 3,
                          out_shape=[jax.ShapeDtypeStruct(w.shape, f32)] * 3,
                          compiler_params=_cparams(("parallel",)))(w, g, m, v)


SMALL_MULTIPLE = 16 * LANE


def _pack_flat(parts, multiple=SMALL_MULTIPLE):
    flat = jnp.concatenate([p.reshape(-1) for p in parts])
    pad = (-flat.shape[0]) % multiple
    return jnp.pad(flat, (0, pad)).reshape(-1, LANE)


def _unpack_flat(buf, shapes):
    flat = buf.reshape(-1)
    out, off = [], 0
    for s in shapes:
        cnt = int(np.prod(s))
        out.append(flat[off:off + cnt].reshape(s))
        off += cnt
    return out


def _full_from_shards(name, gathered):
    if name in COL_SHARDED:
        return jnp.concatenate([gathered[s] for s in range(N_SHARD)], axis=1)
    return gathered.reshape(-1, gathered.shape[2])


def _pieces(name, grad):
    if grad.ndim == 3:
        return grad
    if name in COL_SHARDED:
        r, cdim = grad.shape
        return grad.reshape(r, N_SHARD, cdim // N_SHARD).transpose(1, 0, 2)
    return grad.reshape(N_SHARD, grad.shape[0] // N_SHARD, grad.shape[1])


def _small_pieces(name, grad):
    if name in COL_SHARDED or name in ROW_SHARDED:
        return _pieces(name, grad).reshape(N_SHARD, -1)
    return jnp.broadcast_to(grad.reshape(1, -1), (N_SHARD, grad.size))


def kernel(x, positions, l0_w_in, rwkv_mix, rwkv_w0, rwkv_w2, rwkv_a0, rwkv_a2, rwkv_g2, rwkv_k_k, rwkv_k_a, rwkv_r_k, rwkv_ln_g, rwkv_ln_b, ssm_conv_w, ssm_conv_b, ssm_dt_bias, ssm_a_log, ssm_d, ssm_norm_g, l0_w_out, l0_ln1_g, l0_ln1_b, ffn0_w_up, ffn0_conv_w, ffn0_conv_b, ffn0_w_down, l0_ln2_g, l0_ln2_b, l1_w_in, mla_q_norm_g, mla_w_uq, mla_kv_norm_g, mla_w_ukv, l1_w_out, l1_ln1_g, l1_ln1_b, ffn1_w_up, ffn1_conv_w, ffn1_conv_b, ffn1_w_down, l1_ln2_g, l1_ln2_b, loss_target, m_l0_w_in, m_rwkv_mix, m_rwkv_w0, m_rwkv_w2, m_rwkv_a0, m_rwkv_a2, m_rwkv_g2, m_rwkv_k_k, m_rwkv_k_a, m_rwkv_r_k, m_rwkv_ln_g, m_rwkv_ln_b, m_ssm_conv_w, m_ssm_conv_b, m_ssm_dt_bias, m_ssm_a_log, m_ssm_d, m_ssm_norm_g, m_l0_w_out, m_l0_ln1_g, m_l0_ln1_b, m_ffn0_w_up, m_ffn0_conv_w, m_ffn0_conv_b, m_ffn0_w_down, m_l0_ln2_g, m_l0_ln2_b, m_l1_w_in, m_mla_q_norm_g, m_mla_w_uq, m_mla_kv_norm_g, m_mla_w_ukv, m_l1_w_out, m_l1_ln1_g, m_l1_ln1_b, m_ffn1_w_up, m_ffn1_conv_w, m_ffn1_conv_b, m_ffn1_w_down, m_l1_ln2_g, m_l1_ln2_b, v_l0_w_in, v_rwkv_mix, v_rwkv_w0, v_rwkv_w2, v_rwkv_a0, v_rwkv_a2, v_rwkv_g2, v_rwkv_k_k, v_rwkv_k_a, v_rwkv_r_k, v_rwkv_ln_g, v_rwkv_ln_b, v_ssm_conv_w, v_ssm_conv_b, v_ssm_dt_bias, v_ssm_a_log, v_ssm_d, v_ssm_norm_g, v_l0_w_out, v_l0_ln1_g, v_l0_ln1_b, v_ffn0_w_up, v_ffn0_conv_w, v_ffn0_conv_b, v_ffn0_w_down, v_l0_ln2_g, v_l0_ln2_b, v_l1_w_in, v_mla_q_norm_g, v_mla_w_uq, v_mla_kv_norm_g, v_mla_w_ukv, v_l1_w_out, v_l1_ln1_g, v_l1_ln1_b, v_ffn1_w_up, v_ffn1_conv_w, v_ffn1_conv_b, v_ffn1_w_down, v_l1_ln2_g, v_l1_ln2_b):
    args = locals()
    w_loc = {n: args[n] for n in WEIGHTS}
    m_loc = {n: args["m_" + n] for n in WEIGHTS}
    v_loc = {n: args["v_" + n] for n in WEIGHTS}
    core = lax.axis_index("c").astype(jnp.int32).reshape(1)

    small_sharded = [n for n in SMALL if n in COL_SHARDED]
    halves = lambda a: a.reshape(2, a.shape[0] // 2, a.shape[1])
    whole = lambda name, got: _full_from_shards(name, got.reshape(N_SHARD, -1, got.shape[3]))
    first = gather_shards([halves(w_loc['l0_w_in'].astype(bf16)), halves(_pack_flat([w_loc[n] for n in small_sharded]))])
    w_have = {n: w_loc[n] for n in WEIGHTS if n not in BIG}
    w_have['l0_w_in'] = whole('l0_w_in', first[0])
    small_all = first[1].reshape(N_SHARD, -1, LANE)
    per_shard = [_unpack_flat(small_all[s], [w_loc[n].shape for n in small_sharded]) for s in range(N_SHARD)]
    for k, n in enumerate(small_sharded):
        w_have[n] = jnp.concatenate([per_shard[s][k] for s in range(N_SHARD)], axis=1)
    shards_of = lambda names: (names, [halves(w_loc[n].astype(bf16)) for n in names])
    late = (shards_of(['l0_w_out', 'ffn0_w_up', 'ffn0_w_down', 'l1_w_in']),
            shards_of(['l1_w_out', 'ffn1_w_up', 'ffn1_w_down']), whole)

    def in_halves(pieces):
        return [p.reshape(N_SHARD, 2, p.shape[1] // 2, p.shape[2]) for p in pieces]

    def pair_sums(names, pieces, from_sibling):
        return [pair_add(p, r, core, "pair_add_" + n, f32 if n == 'small' else bf16)
                for n, p, r in zip(names, pieces, from_sibling)]

    loss_part, grad_x, g_full, early = local_step(
        x, positions, loss_target, w_have, late,
        (lambda gd: in_halves([_pieces(n, gd[n]) for n in EARLY_GRADS]),
         lambda pieces, from_sibling: pair_sums(EARLY_GRADS, pieces, from_sibling)))
    loss = lax.psum(loss_part[0, 0], AXES)

    small_flat = jnp.concatenate([_small_pieces(n, g_full[n]) for n in SMALL], axis=1)
    pad = (-small_flat.shape[1]) % SMALL_MULTIPLE
    small_pieces = jnp.pad(small_flat, ((0, 0), (0, pad))).reshape(N_SHARD, -1, LANE)
    rest_pieces = in_halves([_pieces('l0_w_in', g_full['l0_w_in']), small_pieces])
    rest = scatter_to_chips(pair_sums(['l0_w_in', 'small'], rest_pieces, swap_halves(rest_pieces, "swap_halves_rest")))
    from_chips = dict(zip(EARLY_GRADS + ['l0_w_in', 'small'], list(early) + list(rest)))
    units = BIG + ['small']
    both = share_halves([chip_add(from_chips[n], core, "chip_add_" + n) for n in units])
    reduced = [b.reshape(-1, b.shape[2]) for b in both]

    out = {}
    for n, gred in zip(BIG, reduced):
        out[n] = (gred,) + tuple(adamw(w_loc[n], gred, m_loc[n], v_loc[n], "adamw_" + n))
    shapes = [w_loc[n].shape for n in SMALL]
    packs = [_pack_flat([d[n] for n in SMALL]) for d in (w_loc, m_loc, v_loc)]
    small_res = (reduced[-1],) + tuple(adamw(packs[0], reduced[-1], packs[1], packs[2], "adamw_small"))
    small_unpacked = [_unpack_flat(b, shapes) for b in small_res]
    for k, n in enumerate(SMALL):
        out[n] = tuple(u[k] for u in small_unpacked)
    return (loss, grad_x, *[out[n][0] for n in WEIGHTS], *[out[n][1] for n in WEIGHTS],
            *[out[n][2] for n in WEIGHTS], *[out[n][3] for n in WEIGHTS])
```
